```python
import math
import jax
import jax.numpy as jnp
from jax import lax
import numpy as np

D_MODEL = 1024
BATCH = 2
SEQ = 8192
DEPTH = 2
DEC_BATCH = 128
DEC_SEQ = 8
PAST_LEN = 16384
PAGE_SIZE = 128

F32 = jnp.float32
HEAD_DIM = 64
MIX_WIDTH = D_MODEL
POOL_WIDTH = MIX_WIDTH // 4
POOL_WINDOWS = (2, 4, 8, 16)
POOL_GROUP = POOL_WIDTH // len(POOL_WINDOWS)
POOL_BUF = max(POOL_WINDOWS) - 1
SWA_HEADS = MIX_WIDTH // 4 // HEAD_DIM
SWA_KV_HEADS = 2
SWA_GROUP = SWA_HEADS // SWA_KV_HEADS
SWA_WIDTH = SWA_HEADS * HEAD_DIM
SWA_KV_WIDTH = SWA_KV_HEADS * HEAD_DIM
SWA_WINDOW = 128
SWA_BLOCK = 128
SSM_WIDTH = MIX_WIDTH // 4
SSM_CH = 16
SSM_GROUPS = SSM_WIDTH // SSM_CH
SSM_STATE = 64
RET_HEADS = MIX_WIDTH // 4 // HEAD_DIM
RET_WIDTH = RET_HEADS * HEAD_DIM
RET_CHUNK = 128
ROPE_BASE = 10000.0
IN_WIDTH = POOL_WIDTH + SWA_WIDTH + 2 * SWA_KV_WIDTH + SSM_WIDTH + 4 * RET_WIDTH
D_FF = 2816
N_EXPERTS = 8
TOP_K = 2
D_FF_EXPERT = 2816
N_DENSE = (DEPTH + 1) // 2
N_MOE = DEPTH // 2
T5_BUCKETS = 32
T5_MAX_DIST = 128
RMS_EPS = 1e-6

kernel_name = 'hymba_pool_swa_s5_retention_decode_step'


def rmsnorm(x, g):
    xf = x.astype(F32)
    y = xf * lax.rsqrt(jnp.mean(xf * xf, axis=-1, keepdims=True) + RMS_EPS)
    return (y * g.astype(F32)).astype(x.dtype)


def swiglu(h, wg, wu, wd):
    return jnp.einsum('blf,fd->bld', jax.nn.silu(jnp.einsum('bld,df->blf', h, wg)) * jnp.einsum('bld,df->blf', h, wu), wd)


def moe_swiglu(h, router, wg, wu, wd):
    logits = jnp.einsum('bld,de->ble', h.astype(F32), router.astype(F32))
    top_v, top_i = lax.top_k(logits, TOP_K)
    gates = jax.nn.softmax(top_v, axis=-1)
    combine = jnp.sum(gates[..., None] * jax.nn.one_hot(top_i, N_EXPERTS, dtype=F32), axis=-2)
    out = jnp.zeros(h.shape, F32)
    for e in range(N_EXPERTS):
        out = out + combine[..., e:e + 1] * swiglu(h, wg[e], wu[e], wd[e]).astype(F32)
    return out.astype(h.dtype)


def t5_bucket(rel):
    n = jnp.maximum(rel, 0)
    max_exact = T5_BUCKETS // 2
    nf = jnp.maximum(n, max_exact).astype(F32)
    large = max_exact + (jnp.log(nf / max_exact) / math.log(T5_MAX_DIST / max_exact) * (T5_BUCKETS - max_exact)).astype(jnp.int32)
    large = jnp.minimum(large, T5_BUCKETS - 1)
    return jnp.where(n < max_exact, n, large)


def pool_mixer(u, buf, pos, pool_w, pool_scale):
    B, L, C = u.shape
    uf = u.astype(F32)
    buf_f = jnp.zeros((B, POOL_BUF, C), F32) if buf is None else buf.astype(F32)
    ext = jnp.concatenate([buf_f, uf], axis=1)
    cs = jnp.concatenate([jnp.zeros((B, 1, C), F32), jnp.cumsum(ext, axis=1)], axis=1)
    hi = cs[:, POOL_BUF + 1:]
    groups = []
    for gi, w in enumerate(POOL_WINDOWS):
        sl = slice(gi * POOL_GROUP, (gi + 1) * POOL_GROUP)
        lo = cs[:, POOL_BUF + 1 - w:POOL_BUF + 1 - w + L, sl]
        cnt = jnp.minimum(w, pos + 1).astype(F32)[None, :, None]
        groups.append((hi[..., sl] - lo) / cnt - uf[..., sl])
    pooled = jnp.stack(groups, axis=2)
    y = jnp.einsum('blgc,gcd->blgd', pooled, pool_w.astype(F32)).reshape(B, L, C) * pool_scale.astype(F32)
    return y.astype(u.dtype), ext[:, -POOL_BUF:].astype(u.dtype)


def swa_core(q, k, v, rel, valid, sinks, rel_bias):
    Q, S = rel.shape
    s = jnp.einsum('bnqkgd,bnskd->bnkgqs', q.astype(F32), k.astype(F32)) * (HEAD_DIM ** -0.5)
    bias = rel_bias.astype(F32)[t5_bucket(rel)]
    bias = jnp.transpose(bias, (2, 0, 1)).reshape(SWA_KV_HEADS, SWA_GROUP, Q, S)
    s = jnp.where(valid[None, :, None, None], s + bias, -jnp.inf)
    sink = sinks.astype(F32).reshape(SWA_KV_HEADS, SWA_GROUP)[None, None, :, :, None, None]
    m = jnp.maximum(jnp.max(s, axis=-1, keepdims=True), sink)
    p = jnp.exp(s - m)
    denom = jnp.sum(p, axis=-1, keepdims=True) + jnp.exp(sink - m)
    return jnp.einsum('bnkgqs,bnskd->bnqkgd', p / denom, v.astype(F32))


def swa_prompt(q, k, v, sinks, rel_bias):
    B, L = q.shape[:2]
    nb, blk = L // SWA_BLOCK, SWA_BLOCK
    qb = q.reshape(B, nb, blk, SWA_KV_HEADS, SWA_GROUP, HEAD_DIM)

    def band(t):
        tb = t.reshape(B, nb, blk, SWA_KV_HEADS, HEAD_DIM)
        prev = jnp.concatenate([jnp.zeros_like(tb[:, :1]), tb[:, :-1]], axis=1)
        return jnp.concatenate([prev, tb], axis=2)

    i = jnp.arange(blk)[:, None]
    j = jnp.arange(2 * blk)[None, :]
    rel = i - j + blk
    valid = ((rel >= 0) & (rel < SWA_WINDOW))[None] & ((jnp.arange(nb)[:, None, None] > 0) | (j[None] >= blk))
    o = swa_core(qb, band(k), band(v), rel, valid, sinks, rel_bias)
    return o.reshape(B, L, SWA_WIDTH)


def swa_sample(q, k, v, buf_k, buf_v, sinks, rel_bias):
    B, L = q.shape[:2]
    wb = buf_k.shape[1]
    kk = jnp.concatenate([buf_k.astype(k.dtype), k], axis=1)
    vv = jnp.concatenate([buf_v.astype(v.dtype), v], axis=1)
    rel = jnp.arange(L)[:, None] - jnp.arange(wb + L)[None, :] + wb
    valid = ((rel >= 0) & (rel < SWA_WINDOW))[None]
    o = swa_core(q.reshape(B, 1, L, SWA_KV_HEADS, SWA_GROUP, HEAD_DIM), kk[:, None], vv[:, None], rel, valid, sinks, rel_bias)
    return o.reshape(B, L, SWA_WIDTH), kk[:, -SWA_WINDOW:], vv[:, -SWA_WINDOW:]


def _ssm_combine(c1, c2):
    a1r, a1i, b1r, b1i = c1
    a2r, a2i, b2r, b2i = c2
    return (a1r * a2r - a1i * a2i, a1r * a2i + a1i * a2r,
            a2r * b1r - a2i * b1i + b2r, a2r * b1i + a2i * b1r + b2i)


def ssm_mixer(u, h0, lam_re, lam_im, log_dt, b_re, b_im, c_re, c_im, d_skip, w_glu):
    B, L, _ = u.shape
    uf = u.astype(F32).reshape(B, L, SSM_GROUPS, SSM_CH)
    lr, li = lam_re.astype(F32), lam_im.astype(F32)
    dt = jnp.exp(log_dt.astype(F32))[:, None]
    mag = jnp.exp(lr * dt)
    ab_re, ab_im = mag * jnp.cos(li * dt), mag * jnp.sin(li * dt)
    den = lr * lr + li * li
    nr = ab_re - 1.0
    f_re = (nr * lr + ab_im * li) / den
    f_im = (ab_im * lr - nr * li) / den
    br, bi = b_re.astype(F32), b_im.astype(F32)
    bb_re = f_re[..., None] * br - f_im[..., None] * bi
    bb_im = f_re[..., None] * bi + f_im[..., None] * br
    bu_re = jnp.einsum('gpc,blgc->blgp', bb_re, uf)
    bu_im = jnp.einsum('gpc,blgc->blgp', bb_im, uf)
    a_re = jnp.broadcast_to(ab_re, bu_re.shape)
    a_im = jnp.broadcast_to(ab_im, bu_re.shape)
    A_re, A_im, h_re, h_im = lax.associative_scan(_ssm_combine, (a_re, a_im, bu_re, bu_im), axis=1)
    if h0 is not None:
        h0r = h0[..., 0].astype(F32)[:, None]
        h0i = h0[..., 1].astype(F32)[:, None]
        h_re = h_re + A_re * h0r - A_im * h0i
        h_im = h_im + A_re * h0i + A_im * h0r
    y = jnp.einsum('gcp,blgp->blgc', c_re.astype(F32), h_re) - jnp.einsum('gcp,blgp->blgc', c_im.astype(F32), h_im)
    y = y.reshape(B, L, SSM_WIDTH) + d_skip.astype(F32) * u.astype(F32)
    y = jax.nn.gelu(y)
    y = y * jax.nn.sigmoid(jnp.einsum('blc,ce->ble', y, w_glu.astype(F32)))
    new_h = jnp.stack([h_re[:, -1], h_im[:, -1]], axis=-1)
    return y.astype(u.dtype), new_h.astype(u.dtype)


def rotate_pairs(x, pos):
    half = HEAD_DIM // 2
    theta = 1.0 / (ROPE_BASE ** jnp.linspace(0.0, 1.0, half, dtype=F32))
    ang = pos.astype(F32)[:, None] * theta[None, :]
    cos = jnp.cos(ang)[None, :, None, :]
    sin = jnp.sin(ang)[None, :, None, :]
    x1, x2 = x[..., 0::2], x[..., 1::2]
    return jnp.stack([x1 * cos - x2 * sin, x1 * sin + x2 * cos], axis=-1).reshape(x.shape)


def retention_chunk(q, k, v, R0, log_g):
    C = q.shape[1]
    idx = jnp.arange(C, dtype=F32)
    diff = idx[:, None] - idx[None, :]
    decay = jnp.where(diff >= 0, jnp.exp(jnp.maximum(diff, 0.0)[None] * log_g[:, None, None]), 0.0)
    s = jnp.einsum('bihd,bjhd->bhij', q, k) * decay[None]
    inner = jnp.einsum('bhij,bjhd->bihd', s, v)
    q_dec = jnp.exp((idx + 1.0)[:, None] * log_g[None, :])
    cross = jnp.einsum('bihd,bhde->bihe', q, R0) * q_dec[None, :, :, None]
    k_dec = jnp.exp((C - 1.0 - idx)[:, None] * log_g[None, :])
    R_new = jnp.exp(C * log_g)[None, :, None, None] * R0 + jnp.einsum('bjhd,bjhe->bhde', k * k_dec[None, :, :, None], v)
    return R_new, inner + cross


def retention_mixer(q, k, v, g, R0, pos, ret_norm):
    B, L, _ = q.shape
    shp = (B, L, RET_HEADS, HEAD_DIM)
    qf = rotate_pairs(q.reshape(shp).astype(F32), pos)
    kf = rotate_pairs(k.reshape(shp).astype(F32), pos) * (HEAD_DIM ** -0.5)
    vf = v.reshape(shp).astype(F32)
    log_g = jnp.log1p(-jnp.exp2(-5.0 - jnp.arange(RET_HEADS, dtype=F32)))
    if R0 is None:
        R0 = jnp.zeros((B, RET_HEADS, HEAD_DIM, HEAD_DIM), F32)
        chunk = RET_CHUNK
    else:
        R0 = R0.astype(F32)
        chunk = L
    nc = L // chunk
    xs = tuple(t.reshape(B, nc, chunk, RET_HEADS, HEAD_DIM).swapaxes(0, 1) for t in (qf, kf, vf))

    def step(R, c):
        return retention_chunk(c[0], c[1], c[2], R, log_g)

    R, o = lax.scan(step, R0, xs)
    o = o.swapaxes(0, 1).reshape(shp)
    o = rmsnorm(o, ret_norm.reshape(RET_HEADS, HEAD_DIM))
    y = o.reshape(B, L, RET_WIDTH) * jax.nn.silu(g.astype(F32))
    return y.astype(q.dtype), R.astype(q.dtype)


def mixer_layer(x, l, start_pos, pool_buf, k_buf, v_buf, h0, R0, p):
    B, L, _ = x.shape
    h = rmsnorm(x, p['norm_mix'][l])
    proj = jnp.einsum('bld,de->ble', h, p['w_in'][l])
    sizes = [POOL_WIDTH, SWA_WIDTH, SWA_KV_WIDTH, SWA_KV_WIDTH, SSM_WIDTH, RET_WIDTH, RET_WIDTH, RET_WIDTH, RET_WIDTH]
    u_pool, q_s, k_s, v_s, u_ssm, q_r, k_r, v_r, g_r = jnp.split(proj, np.cumsum(sizes)[:-1].tolist(), axis=-1)
    pos = start_pos + jnp.arange(L, dtype=jnp.int32)
    y_pool, new_pool = pool_mixer(u_pool, pool_buf, pos, p['pool_w'][l], p['pool_scale'][l])
    q_s = rmsnorm(q_s.reshape(B, L, SWA_HEADS, HEAD_DIM), p['swa_q_norm'][l])
    k_s = rmsnorm(k_s.reshape(B, L, SWA_KV_HEADS, HEAD_DIM), p['swa_k_norm'][l])
    v_s = v_s.reshape(B, L, SWA_KV_HEADS, HEAD_DIM)
    if k_buf is None:
        y_swa = swa_prompt(q_s, k_s, v_s, p['swa_sinks'][l], p['rel_bias'])
        new_k, new_v = k_s[:, -SWA_WINDOW:], v_s[:, -SWA_WINDOW:]
    else:
        y_swa, new_k, new_v = swa_sample(q_s, k_s, v_s, k_buf, v_buf, p['swa_sinks'][l], p['rel_bias'])
    y_ssm, new_h = ssm_mixer(u_ssm, h0, p['ssm_lambda_re'][l], p['ssm_lambda_im'][l], p['ssm_log_dt'][l],
                             p['ssm_b_re'][l], p['ssm_b_im'][l], p['ssm_c_re'][l], p['ssm_c_im'][l],
                             p['ssm_d'][l], p['ssm_w_glu'][l])
    y_ret, new_R = retention_mixer(q_r, k_r, v_r, g_r, R0, pos, p['ret_norm'][l])
    y = jnp.concatenate([y_pool, y_swa.astype(x.dtype), y_ssm, y_ret], axis=-1)
    x = x + jnp.einsum('ble,ed->bld', y, p['w_out'][l])
    return x, (new_pool, new_k, new_v, new_h, new_R)


def trunk(x, start_pos, state_pool, cache_k, cache_v, state_ssm, state_ret, p):
    pools, ks, vs, hs, rs = [], [], [], [], []
    cached = state_pool is not None
    for l in range(DEPTH):
        x, st = mixer_layer(x, l, start_pos,
                            state_pool[l] if cached else None,
                            cache_k[l] if cached else None,
                            cache_v[l] if cached else None,
                            state_ssm[l] if cached else None,
                            state_ret[l] if cached else None, p)
        pools.append(st[0]); ks.append(st[1]); vs.append(st[2]); hs.append(st[3]); rs.append(st[4])
        h = rmsnorm(x, p['norm_ffn'][l])
        if l % 2 == 0:
            i = l // 2
            f = swiglu(h, p['ffn_w_gate'][i], p['ffn_w_up'][i], p['ffn_w_down'][i])
        else:
            i = l // 2
            f = moe_swiglu(h, p['moe_router'][i], p['moe_w_gate'][i], p['moe_w_up'][i], p['moe_w_down'][i])
        x = x + f
    return x, jnp.stack(pools), jnp.stack(ks), jnp.stack(vs), jnp.stack(hs), jnp.stack(rs)


def setup_inputs(seed: int = 0) -> dict:
    key = jax.random.key(seed)
    ks = iter(jax.random.split(key, 48))

    def nrm(shape, scale):
        return jax.random.normal(next(ks), shape, F32) * scale

    swa_buf = min(SWA_WINDOW, PAST_LEN)
    lam_im = math.pi * jnp.arange(SSM_STATE, dtype=F32)[None, None, :] + nrm((DEPTH, SSM_GROUPS, SSM_STATE), 0.01)
    return {
        'x_prompt': nrm((BATCH, SEQ, D_MODEL), 1.0),
        'x_sample': nrm((DEC_BATCH, DEC_SEQ, D_MODEL), 1.0),
        'state_pool': nrm((DEPTH, DEC_BATCH, POOL_BUF, POOL_WIDTH), 1.0),
        'cache_swa_k': nrm((DEPTH, DEC_BATCH, swa_buf, SWA_KV_HEADS, HEAD_DIM), 1.0),
        'cache_swa_v': nrm((DEPTH, DEC_BATCH, swa_buf, SWA_KV_HEADS, HEAD_DIM), 1.0),
        'state_ssm': nrm((DEPTH, DEC_BATCH, SSM_GROUPS, SSM_STATE, 2), 0.1),
        'state_ret': nrm((DEPTH, DEC_BATCH, RET_HEADS, HEAD_DIM, HEAD_DIM), 0.5),
        'norm_mix': 1.0 + nrm((DEPTH, D_MODEL), 0.05),
        'norm_ffn': 1.0 + nrm((DEPTH, D_MODEL), 0.05),
        'w_in': nrm((DEPTH, D_MODEL, IN_WIDTH), D_MODEL ** -0.5),
        'w_out': nrm((DEPTH, MIX_WIDTH, D_MODEL), MIX_WIDTH ** -0.5),
        'pool_w': nrm((DEPTH, len(POOL_WINDOWS), POOL_GROUP, POOL_GROUP), POOL_GROUP ** -0.5),
        'pool_scale': 1.0 + nrm((DEPTH, POOL_WIDTH), 0.1),
        'swa_q_norm': 1.0 + nrm((DEPTH, HEAD_DIM), 0.05),
        'swa_k_norm': 1.0 + nrm((DEPTH, HEAD_DIM), 0.05),
        'swa_sinks': nrm((DEPTH, SWA_HEADS), 0.5),
        'rel_bias': nrm((T5_BUCKETS, SWA_HEADS), 0.5),
        'ssm_lambda_re': -0.5 * jnp.exp(nrm((DEPTH, SSM_GROUPS, SSM_STATE), 0.05)),
        'ssm_lambda_im': lam_im,
        'ssm_log_dt': jax.random.uniform(next(ks), (DEPTH, SSM_GROUPS), F32, math.log(1e-3), math.log(1e-1)),
        'ssm_b_re': nrm((DEPTH, SSM_GROUPS, SSM_STATE, SSM_CH), (2 * SSM_CH) ** -0.5),
        'ssm_b_im': nrm((DEPTH, SSM_GROUPS, SSM_STATE, SSM_CH), (2 * SSM_CH) ** -0.5),
        'ssm_c_re': nrm((DEPTH, SSM_GROUPS, SSM_CH, SSM_STATE), SSM_STATE ** -0.5),
        'ssm_c_im': nrm((DEPTH, SSM_GROUPS, SSM_CH, SSM_STATE), SSM_STATE ** -0.5),
        'ssm_d': nrm((DEPTH, SSM_WIDTH), 1.0),
        'ssm_w_glu': nrm((DEPTH, SSM_WIDTH, SSM_WIDTH), SSM_WIDTH ** -0.5),
        'ret_norm': 1.0 + nrm((DEPTH, RET_WIDTH), 0.05),
        'ffn_w_gate': nrm((N_DENSE, D_MODEL, D_FF), D_MODEL ** -0.5),
        'ffn_w_up': nrm((N_DENSE, D_MODEL, D_FF), D_MODEL ** -0.5),
        'ffn_w_down': nrm((N_DENSE, D_FF, D_MODEL), D_FF ** -0.5),
        'moe_router': nrm((N_MOE, D_MODEL, N_EXPERTS), D_MODEL ** -0.5),
        'moe_w_gate': nrm((N_MOE, N_EXPERTS, D_MODEL, D_FF_EXPERT), D_MODEL ** -0.5),
        'moe_w_up': nrm((N_MOE, N_EXPERTS, D_MODEL, D_FF_EXPERT), D_MODEL ** -0.5),
        'moe_w_down': nrm((N_MOE, N_EXPERTS, D_FF_EXPERT, D_MODEL), D_FF_EXPERT ** -0.5),
    }


def reference(x_prompt, x_sample, state_pool, cache_swa_k, cache_swa_v, state_ssm, state_ret,
              norm_mix, norm_ffn, w_in, w_out, pool_w, pool_scale, swa_q_norm, swa_k_norm, swa_sinks,
              rel_bias, ssm_lambda_re, ssm_lambda_im, ssm_log_dt, ssm_b_re, ssm_b_im, ssm_c_re, ssm_c_im,
              ssm_d, ssm_w_glu, ret_norm, ffn_w_gate, ffn_w_up, ffn_w_down, moe_router, moe_w_gate,
              moe_w_up, moe_w_down):
    p = dict(norm_mix=norm_mix, norm_ffn=norm_ffn, w_in=w_in, w_out=w_out, pool_w=pool_w,
             pool_scale=pool_scale, swa_q_norm=swa_q_norm, swa_k_norm=swa_k_norm, swa_sinks=swa_sinks,
             rel_bias=rel_bias, ssm_lambda_re=ssm_lambda_re, ssm_lambda_im=ssm_lambda_im,
             ssm_log_dt=ssm_log_dt, ssm_b_re=ssm_b_re, ssm_b_im=ssm_b_im, ssm_c_re=ssm_c_re,
             ssm_c_im=ssm_c_im, ssm_d=ssm_d, ssm_w_glu=ssm_w_glu, ret_norm=ret_norm,
             ffn_w_gate=ffn_w_gate, ffn_w_up=ffn_w_up, ffn_w_down=ffn_w_down, moe_router=moe_router,
             moe_w_gate=moe_w_gate, moe_w_up=moe_w_up, moe_w_down=moe_w_down)
    y_p, pool_p, k_p, v_p, ssm_p, ret_p = trunk(x_prompt, 0, None, None, None, None, None, p)
    y_s, pool_s, k_s, v_s, ssm_s, ret_s = trunk(x_sample, PAST_LEN, state_pool, cache_swa_k, cache_swa_v,
                                                state_ssm, state_ret, p)
    return (y_p, y_s, pool_p, pool_s, k_p, k_s, v_p, v_s, ssm_p, ssm_s, ret_p, ret_s)
```

```python
import functools
import math

import numpy as np
import jax
import jax.numpy as jnp
from jax import lax
from jax.experimental import pallas as pl
from jax.experimental.pallas import tpu as pltpu

F32 = jnp.float32
BF16 = jnp.bfloat16

D_MODEL = 1024
HEAD_DIM = 64
POOL_WIDTH = 256
POOL_WINDOWS = (2, 4, 8, 16)
POOL_BUF = 15
POOL_HALO = 16
SWA_HEADS = 4
SWA_KV_HEADS = 2
SWA_WINDOW = 128
SWA_BLOCK = 128
SSM_WIDTH = 256
SSM_CH = 16
SSM_GROUPS = 16
SSM_STATE = 64
SSM_N = SSM_GROUPS * SSM_STATE
RET_HEADS = 4
RET_CHUNK = 128
ROPE_BASE = 10000.0
IN_WIDTH = 2048
MIX_BLOCK = 256
D_FF = 2816
N_EXPERTS = 8
T5_BUCKETS = 32
T5_MAX_DIST = 128
RMS_EPS = 1e-6
NEG = -1e30
SCAN_ROWS = 8

COL_POOL, COL_SWA_Q, COL_SWA_KV, COL_SSM, COL_RET_Q, COL_RET_K, COL_RET_V, COL_RET_G = range(8)

VMEM_LIMIT = 48 * 1024 * 1024


def _cparams(sem):
    return pltpu.CompilerParams(dimension_semantics=sem, vmem_limit_bytes=VMEM_LIMIT)


def _rms(x, g):
    ms = jnp.mean(x * x, axis=-1, keepdims=True)
    return x * lax.rsqrt(ms + RMS_EPS) * g


def _dot(a, b):
    return jnp.dot(a, b, preferred_element_type=F32)


def _dot_nt(a, b):
    return lax.dot_general(a, b, (((1,), (1,)), ((), ())), preferred_element_type=F32)


def _dot_tn(a, b):
    return lax.dot_general(a, b, (((0,), (0,)), ((), ())), preferred_element_type=F32)


def _sigmoid(x):
    return 1.0 / (1.0 + jnp.exp(-x))


def _norm_matmul_kernel(x_ref, g_ref, w_ref, o_ref):
    h = _rms(x_ref[...], g_ref[...]).astype(BF16)
    o_ref[...] = _dot(h, w_ref[...])


def _norm_matmul(x, g, w, tm):
    t, d = x.shape
    tm = min(tm, t)
    n = w.shape[1]
    return pl.pallas_call(
        _norm_matmul_kernel,
        grid=(t // tm,),
        in_specs=[pl.BlockSpec((tm, d), lambda i: (i, 0)),
                  pl.BlockSpec((1, d), lambda i: (0, 0)),
                  pl.BlockSpec((d, n), lambda i: (0, 0))],
        out_specs=pl.BlockSpec((tm, n), lambda i: (i, 0)),
        out_shape=jax.ShapeDtypeStruct((t, n), F32),
        compiler_params=_cparams(("parallel",)),
        name="norm_matmul",
    )(x, g, w)


def _out_proj_kernel(x_ref, y0_ref, y1_ref, y2_ref, y3_ref, w_ref, o_ref):
    acc = x_ref[...]
    for k, y_ref in enumerate((y0_ref, y1_ref, y2_ref, y3_ref)):
        acc = acc + _dot(y_ref[...], w_ref[k * MIX_BLOCK:(k + 1) * MIX_BLOCK, :])
    o_ref[...] = acc


def _out_proj(x, ys, w, tm):
    t, d = x.shape
    tm = min(tm, t)
    yspec = pl.BlockSpec((tm, MIX_BLOCK), lambda i: (i, 0))
    return pl.pallas_call(
        _out_proj_kernel,
        grid=(t // tm,),
        in_specs=[pl.BlockSpec((tm, d), lambda i: (i, 0)), yspec, yspec, yspec, yspec,
                  pl.BlockSpec((d, d), lambda i: (0, 0))],
        out_specs=pl.BlockSpec((tm, d), lambda i: (i, 0)),
        out_shape=jax.ShapeDtypeStruct((t, d), F32),
        compiler_params=_cparams(("parallel",)),
        name="out_proj",
    )(x, *ys, w)


def _ffn_kernel(x_ref, g_ref, c_ref, wg_ref, wu_ref, wd_ref, o_ref, h_scr, *, use_combine):
    e = pl.program_id(1)
    j = pl.program_id(2)

    @pl.when((e == 0) & (j == 0))
    def _():
        x = x_ref[...]
        h_scr[...] = _rms(x, g_ref[...]).astype(BF16)
        o_ref[...] = x

    h = h_scr[...]
    a = _dot(h, wg_ref[0])
    b = _dot(h, wu_ref[0])
    act = a * _sigmoid(a) * b
    if use_combine:
        c = c_ref[...]
        lane = lax.broadcasted_iota(jnp.int32, c.shape, 1)
        act = act * jnp.sum(jnp.where(lane == e, c, 0.0), axis=1, keepdims=True)
    o_ref[...] += _dot(act.astype(BF16), wd_ref[0])


def _ffn(x, g, combine, wg, wu, wd, tm, tf):
    t, d = x.shape
    tm = min(tm, t)
    ne, _, f = wg.shape
    use_combine = combine is not None
    if combine is None:
        combine = jnp.ones((t, 128), F32)
    return pl.pallas_call(
        functools.partial(_ffn_kernel, use_combine=use_combine),
        grid=(t // tm, ne, f // tf),
        in_specs=[pl.BlockSpec((tm, d), lambda i, e, j: (i, 0)),
                  pl.BlockSpec((1, d), lambda i, e, j: (0, 0)),
                  pl.BlockSpec((tm, 128), lambda i, e, j: (i, 0)),
                  pl.BlockSpec((1, d, tf), lambda i, e, j: (e, 0, j)),
                  pl.BlockSpec((1, d, tf), lambda i, e, j: (e, 0, j)),
                  pl.BlockSpec((1, tf, d), lambda i, e, j: (e, j, 0))],
        out_specs=pl.BlockSpec((tm, d), lambda i, e, j: (i, 0)),
        out_shape=jax.ShapeDtypeStruct((t, d), F32),
        scratch_shapes=[pltpu.VMEM((tm, d), BF16)],
        compiler_params=_cparams(("parallel", "arbitrary", "arbitrary")),
        name="ffn",
    )(x, g, combine, wg, wu, wd)


def _router_kernel(x_ref, g_ref, wr_ref, c_ref):
    h = _rms(x_ref[...], g_ref[...])
    logits = jnp.dot(h, wr_ref[...], preferred_element_type=F32, precision=lax.Precision.HIGHEST)
    lane = lax.broadcasted_iota(jnp.int32, logits.shape, 1).astype(F32)
    lg = jnp.where(lane < N_EXPERTS, logits, NEG)
    m1 = jnp.max(lg, axis=-1, keepdims=True)
    i1 = jnp.min(jnp.where(lg == m1, lane, 128.0), axis=-1, keepdims=True)
    lg2 = jnp.where(lane == i1, NEG, lg)
    m2 = jnp.max(lg2, axis=-1, keepdims=True)
    i2 = jnp.min(jnp.where(lg2 == m2, lane, 128.0), axis=-1, keepdims=True)
    ex = jnp.exp(m2 - m1)
    g1 = 1.0 / (1.0 + ex)
    g2 = ex / (1.0 + ex)
    c_ref[...] = jnp.where(lane == i1, g1, 0.0) + jnp.where(lane == i2, g2, 0.0)


def _router(x, g, wr, tm):
    t, d = x.shape
    tm = min(tm, t)
    return pl.pallas_call(
        _router_kernel,
        grid=(t // tm,),
        in_specs=[pl.BlockSpec((tm, d), lambda i: (i, 0)),
                  pl.BlockSpec((1, d), lambda i: (0, 0)),
                  pl.BlockSpec((d, 128), lambda i: (0, 0))],
        out_specs=pl.BlockSpec((tm, 128), lambda i: (i, 0)),
        out_shape=jax.ShapeDtypeStruct((t, 128), F32),
        compiler_params=_cparams(("parallel",)),
        name="router",
    )(x, g, wr)


def _pool_kernel(u_ref, halo_ref, w_ref, scale_ref, o_ref, *, tiles_per_seq, pos0, tb):
    ti = pl.program_id(0) % tiles_per_seq
    u = u_ref[...]
    halo = jnp.where(ti == 0, 0.0, halo_ref[...])
    ext = jnp.concatenate([halo, u], axis=0)
    s2 = ext + pltpu.roll(ext, 1, 0)
    s4 = s2 + pltpu.roll(s2, 2, 0)
    s8 = s4 + pltpu.roll(s4, 4, 0)
    s16 = s8 + pltpu.roll(s8, 8, 0)
    grp = lax.broadcasted_iota(jnp.int32, (tb, POOL_WIDTH), 1) // (POOL_WIDTH // 4)
    row = lax.broadcasted_iota(jnp.int32, (tb, POOL_WIDTH), 0)
    s = jnp.where(grp == 0, s2[POOL_HALO:],
                  jnp.where(grp == 1, s4[POOL_HALO:],
                            jnp.where(grp == 2, s8[POOL_HALO:], s16[POOL_HALO:])))
    win = jnp.where(grp == 0, 2, jnp.where(grp == 1, 4, jnp.where(grp == 2, 8, 16)))
    cnt = jnp.minimum(win, pos0 + ti * tb + row + 1).astype(F32)
    pooled = s / cnt - u
    o_ref[...] = (_dot(pooled.astype(BF16), w_ref[...]) * scale_ref[...]).astype(o_ref.dtype)


def _pool(proj2d, col, w, scale, *, n_rows, tb, tiles_per_seq, pos0):
    per = tb // POOL_HALO
    return pl.pallas_call(
        functools.partial(_pool_kernel, tiles_per_seq=tiles_per_seq, pos0=pos0, tb=tb),
        grid=(n_rows // tb,),
        in_specs=[pl.BlockSpec((tb, POOL_WIDTH), lambda i: (i, col)),
                  pl.BlockSpec((POOL_HALO, POOL_WIDTH), lambda i: (jnp.maximum(i * per - 1, 0), col)),
                  pl.BlockSpec((POOL_WIDTH, POOL_WIDTH), lambda i: (0, 0)),
                  pl.BlockSpec((1, POOL_WIDTH), lambda i: (0, 0))],
        out_specs=pl.BlockSpec((tb, POOL_WIDTH), lambda i: (i, 0)),
        out_shape=jax.ShapeDtypeStruct((n_rows, POOL_WIDTH), BF16),
        compiler_params=_cparams(("parallel",)),
        name="pool",
    )(proj2d, proj2d, w, scale)


def _t5_bucket_np(rel):
    n = np.maximum(rel, 0)
    max_exact = T5_BUCKETS // 2
    nf = np.maximum(n, max_exact).astype(np.float32)
    large = max_exact + (np.log(nf / max_exact) / math.log(T5_MAX_DIST / max_exact)
                         * (T5_BUCKETS - max_exact)).astype(np.int32)
    large = np.minimum(large, T5_BUCKETS - 1)
    return np.where(n < max_exact, n, large)


def _swa_bias(rel_bias, rel):
    valid = (rel >= 0) & (rel < SWA_WINDOW)
    b = rel_bias.astype(F32)[_t5_bucket_np(rel)]
    b = jnp.transpose(b, (2, 0, 1))
    return jnp.where(valid[None], b, NEG)


def _softmax_parts(parts, sink):
    m = sink
    for s in parts:
        m = jnp.maximum(m, jnp.max(s, axis=-1, keepdims=True))
    ps = [jnp.exp(s - m) for s in parts]
    denom = jnp.exp(sink - m)
    for p in ps:
        denom = denom + jnp.sum(p, axis=-1, keepdims=True)
    return ps, denom


def _swa_prompt_kernel(sinks_ref, q_ref, kv_ref, kvp_ref, qg_ref, kg_ref, bias_ref, y_ref, kn_ref):
    has_prev = pl.program_id(1) > 0
    q = q_ref[0]
    kv = kv_ref[0]
    kvp = kvp_ref[0]
    qg = qg_ref[...]
    kg = kg_ref[...]
    kw = SWA_KV_HEADS * HEAD_DIM
    for kh in range(SWA_KV_HEADS):
        ksl = slice(kh * HEAD_DIM, (kh + 1) * HEAD_DIM)
        vsl = slice(kw + kh * HEAD_DIM, kw + (kh + 1) * HEAD_DIM)
        kc = _rms(kv[:, ksl], kg)
        kp = _rms(kvp[:, ksl], kg)
        kn_ref[0, :, ksl] = kc
        kcb, kpb = kc.astype(BF16), kp.astype(BF16)
        vc, vp = kv[:, vsl].astype(BF16), kvp[:, vsl].astype(BF16)
        for gq in range(SWA_HEADS // SWA_KV_HEADS):
            h = kh * (SWA_HEADS // SWA_KV_HEADS) + gq
            hsl = slice(h * HEAD_DIM, (h + 1) * HEAD_DIM)
            qn = _rms(q[:, hsl], qg).astype(BF16)
            s_c = _dot_nt(qn, kcb) * (HEAD_DIM ** -0.5) + bias_ref[h, :, SWA_BLOCK:]
            s_p = _dot_nt(qn, kpb) * (HEAD_DIM ** -0.5) + bias_ref[h, :, :SWA_BLOCK]
            s_p = jnp.where(has_prev, s_p, NEG)
            (p_c, p_p), denom = _softmax_parts((s_c, s_p), sinks_ref[h])
            o = _dot(p_c.astype(BF16), vc) + _dot(p_p.astype(BF16), vp)
            y_ref[0, :, hsl] = (o / denom).astype(y_ref.dtype)


def _swa_prompt(proj3d, sinks, qg, kg, bias):
    b, l, _ = proj3d.shape
    nb = l // SWA_BLOCK
    blk = (1, SWA_BLOCK, MIX_BLOCK)
    return pl.pallas_call(
        _swa_prompt_kernel,
        grid=(b, nb),
        in_specs=[pl.BlockSpec(memory_space=pltpu.SMEM),
                  pl.BlockSpec(blk, lambda bi, i: (bi, i, COL_SWA_Q)),
                  pl.BlockSpec(blk, lambda bi, i: (bi, i, COL_SWA_KV)),
                  pl.BlockSpec(blk, lambda bi, i: (bi, jnp.maximum(i - 1, 0), COL_SWA_KV)),
                  pl.BlockSpec((1, HEAD_DIM), lambda bi, i: (0, 0)),
                  pl.BlockSpec((1, HEAD_DIM), lambda bi, i: (0, 0)),
                  pl.BlockSpec((SWA_HEADS, SWA_BLOCK, 2 * SWA_BLOCK), lambda bi, i: (0, 0, 0))],
        out_specs=[pl.BlockSpec(blk, lambda bi, i: (bi, i, 0)),
                   pl.BlockSpec((1, SWA_BLOCK, SWA_KV_HEADS * HEAD_DIM), lambda bi, i: (bi, 0, 0))],
        out_shape=[jax.ShapeDtypeStruct((b, l, MIX_BLOCK), BF16),
                   jax.ShapeDtypeStruct((b, SWA_BLOCK, SWA_KV_HEADS * HEAD_DIM), F32)],
        compiler_params=_cparams(("parallel", "arbitrary")),
        name="swa_prompt",
    )(sinks, proj3d, proj3d, proj3d, qg, kg, bias)


def _swa_sample_kernel(sinks_ref, q_ref, kv_ref, ck_ref, cv_ref, qg_ref, kg_ref, bias_ref,
                       y_ref, nk_ref, nv_ref, *, n_new):
    q = q_ref[...]
    kv = kv_ref[...]
    ck = ck_ref[...]
    cv = cv_ref[...]
    qg = qg_ref[...]
    kg = kg_ref[...]
    kw = SWA_KV_HEADS * HEAD_DIM
    bdot = functools.partial(jnp.einsum, preferred_element_type=F32)
    for kh in range(SWA_KV_HEADS):
        ksl = slice(kh * HEAD_DIM, (kh + 1) * HEAD_DIM)
        vsl = slice(kw + kh * HEAD_DIM, kw + (kh + 1) * HEAD_DIM)
        kn = _rms(kv[:, :, ksl], kg)
        vn = kv[:, :, vsl]
        nk_ref[:, :SWA_WINDOW - n_new, ksl] = ck[:, n_new:, ksl]
        nk_ref[:, SWA_WINDOW - n_new:, ksl] = kn
        nv_ref[:, :SWA_WINDOW - n_new, ksl] = cv[:, n_new:, ksl]
        nv_ref[:, SWA_WINDOW - n_new:, ksl] = vn
        for gq in range(SWA_HEADS // SWA_KV_HEADS):
            h = kh * (SWA_HEADS // SWA_KV_HEADS) + gq
            hsl = slice(h * HEAD_DIM, (h + 1) * HEAD_DIM)
            qn = _rms(q[:, :, hsl], qg)
            s_c = bdot('sqd,skd->sqk', qn, ck[:, :, ksl]) * (HEAD_DIM ** -0.5) + bias_ref[h, :, :SWA_WINDOW]
            s_n = bdot('sqd,skd->sqk', qn, kn) * (HEAD_DIM ** -0.5) + bias_ref[h, :, SWA_WINDOW:]
            (p_c, p_n), denom = _softmax_parts((s_c, s_n), sinks_ref[h])
            o = bdot('sqk,skd->sqd', p_c, cv[:, :, ksl]) + bdot('sqk,skd->sqd', p_n, vn)
            y_ref[:, :, hsl] = (o / denom).astype(y_ref.dtype)


def _swa_sample(proj3d, cache_k, cache_v, sinks, qg, kg, bias, s_blk):
    nseq, n_new, _ = proj3d.shape
    kw = SWA_KV_HEADS * HEAD_DIM
    blk = (s_blk, n_new, MIX_BLOCK)
    cblk = (s_blk, SWA_WINDOW, kw)
    return pl.pallas_call(
        functools.partial(_swa_sample_kernel, n_new=n_new),
        grid=(nseq // s_blk,),
        in_specs=[pl.BlockSpec(memory_space=pltpu.SMEM),
                  pl.BlockSpec(blk, lambda i: (i, 0, COL_SWA_Q)),
                  pl.BlockSpec(blk, lambda i: (i, 0, COL_SWA_KV)),
                  pl.BlockSpec(cblk, lambda i: (i, 0, 0)),
                  pl.BlockSpec(cblk, lambda i: (i, 0, 0)),
                  pl.BlockSpec((1, HEAD_DIM), lambda i: (0, 0)),
                  pl.BlockSpec((1, HEAD_DIM), lambda i: (0, 0)),
                  pl.BlockSpec((SWA_HEADS, n_new, SWA_WINDOW + n_new), lambda i: (0, 0, 0))],
        out_specs=[pl.BlockSpec(blk, lambda i: (i, 0, 0)),
                   pl.BlockSpec(cblk, lambda i: (i, 0, 0)),
                   pl.BlockSpec(cblk, lambda i: (i, 0, 0))],
        out_shape=[jax.ShapeDtypeStruct((nseq, n_new, MIX_BLOCK), BF16),
                   jax.ShapeDtypeStruct((nseq, SWA_WINDOW, kw), F32),
                   jax.ShapeDtypeStruct((nseq, SWA_WINDOW, kw), F32)],
        compiler_params=_cparams(("parallel",)),
        name="swa_sample",
    )(sinks, proj3d, proj3d, cache_k, cache_v, qg, kg, bias)


def _ssm_kernel(u_ref, h0_ref, wb_ref, tab_ref, wc_ref, d_ref, wglu_ref, y_ref, hn_ref,
                bu_scr, carry_scr, *, chained, tiles_per_seq, tb):
    n = SSM_N
    u = u_ref[...]
    bu_scr[...] = _dot(u.astype(BF16), wb_ref[...])

    if chained:
        @pl.when(pl.program_id(1) % tiles_per_seq == 0)
        def _():
            carry_scr[...] = jnp.zeros_like(carry_scr)

    def tile_scan(r0, cr, ci):
        hr = bu_scr[pl.ds(r0, SCAN_ROWS), :n]
        hi = bu_scr[pl.ds(r0, SCAN_ROWS), n:]
        for k, shift in enumerate((1, 2, 4)):
            ar, ai = tab_ref[2 * k], tab_ref[2 * k + 1]
            sr, si = pltpu.roll(hr, shift, 0), pltpu.roll(hi, shift, 0)
            hr, hi = hr + ar * sr - ai * si, hi + ar * si + ai * sr
        pr, pi = tab_ref[6], tab_ref[7]
        hr, hi = hr + pr * cr - pi * ci, hi + pr * ci + pi * cr
        bu_scr[pl.ds(r0, SCAN_ROWS), :n] = hr
        bu_scr[pl.ds(r0, SCAN_ROWS), n:] = hi
        return hr[SCAN_ROWS - 1:], hi[SCAN_ROWS - 1:]

    if chained:
        def body(t, carry):
            r0 = pl.multiple_of(t * SCAN_ROWS, SCAN_ROWS)
            lr, li = tile_scan(r0, *carry)
            return (jnp.broadcast_to(lr, (SCAN_ROWS, n)), jnp.broadcast_to(li, (SCAN_ROWS, n)))

        cr, ci = lax.fori_loop(0, tb // SCAN_ROWS, body, (carry_scr[:, :n], carry_scr[:, n:]))
        carry_scr[:, :n] = cr
        carry_scr[:, n:] = ci
        hn_ref[0, :, :n] = cr
        hn_ref[0, :, n:] = ci
    else:
        def body(t, _):
            r0 = pl.multiple_of(t * SCAN_ROWS, SCAN_ROWS)
            h0 = h0_ref[pl.ds(t, 1), :]
            cr = jnp.broadcast_to(h0[:, :n], (SCAN_ROWS, n))
            ci = jnp.broadcast_to(h0[:, n:], (SCAN_ROWS, n))
            lr, li = tile_scan(r0, cr, ci)
            hn_ref[pl.ds(t, 1), :n] = lr
            hn_ref[pl.ds(t, 1), n:] = li
            return 0

        lax.fori_loop(0, tb // SCAN_ROWS, body, 0)

    y = _dot(bu_scr[...].astype(BF16), wc_ref[...]) + d_ref[...] * u
    y = 0.5 * y * (1.0 + jnp.tanh(math.sqrt(2.0 / math.pi) * (y + 0.044715 * (y * y * y))))
    y = y * _sigmoid(_dot(y.astype(BF16), wglu_ref[...]))
    if chained:
        y_ref[0] = y.astype(y_ref.dtype)
    else:
        y_ref[...] = y.astype(y_ref.dtype)


def _ssm_common_specs(zero_map2, zero_map3):
    return [pl.BlockSpec((SSM_WIDTH, 2 * SSM_N), zero_map2),
            pl.BlockSpec((8, SCAN_ROWS, SSM_N), zero_map3),
            pl.BlockSpec((2 * SSM_N, SSM_WIDTH), zero_map2),
            pl.BlockSpec((1, SSM_WIDTH), zero_map2),
            pl.BlockSpec((SSM_WIDTH, SSM_WIDTH), zero_map2)]


def _ssm_prompt(proj3d, sp, tb):
    b, l, _ = proj3d.shape
    nt = l // tb
    dummy_h0 = jnp.zeros((SCAN_ROWS, 2 * SSM_N), F32)
    kern = functools.partial(_ssm_kernel, chained=True, tiles_per_seq=nt, tb=tb)

    def kernel(u_ref, h0_ref, wb, tab, wc, d, wglu, y_ref, hn_ref, bu_scr, carry_scr):
        kern(u_ref.at[0], h0_ref, wb, tab, wc, d, wglu, y_ref, hn_ref, bu_scr, carry_scr)

    return pl.pallas_call(
        kernel,
        grid=(b, nt),
        in_specs=[pl.BlockSpec((1, tb, SSM_WIDTH), lambda bi, i: (bi, i, COL_SSM)),
                  pl.BlockSpec((SCAN_ROWS, 2 * SSM_N), lambda bi, i: (0, 0))]
                 + _ssm_common_specs(lambda bi, i: (0, 0), lambda bi, i: (0, 0, 0)),
        out_specs=[pl.BlockSpec((1, tb, SSM_WIDTH), lambda bi, i: (bi, i, 0)),
                   pl.BlockSpec((1, SCAN_ROWS, 2 * SSM_N), lambda bi, i: (bi, 0, 0))],
        out_shape=[jax.ShapeDtypeStruct((b, l, SSM_WIDTH), BF16),
                   jax.ShapeDtypeStruct((b, SCAN_ROWS, 2 * SSM_N), F32)],
        scratch_shapes=[pltpu.VMEM((tb, 2 * SSM_N), F32), pltpu.VMEM((SCAN_ROWS, 2 * SSM_N), F32)],
        compiler_params=_cparams(("parallel", "arbitrary")),
        name="ssm_prompt",
    )(proj3d, dummy_h0, sp["wb"], sp["tab"], sp["wc"], sp["d"], sp["wglu"])


def _ssm_sample(proj2d, h0, sp):
    rows = proj2d.shape[0]
    nseq = h0.shape[0]
    kern = functools.partial(_ssm_kernel, chained=False, tiles_per_seq=1, tb=rows)
    return pl.pallas_call(
        kern,
        grid=(1,),
        in_specs=[pl.BlockSpec((rows, SSM_WIDTH), lambda i: (0, COL_SSM)),
                  pl.BlockSpec((nseq, 2 * SSM_N), lambda i: (0, 0))]
                 + _ssm_common_specs(lambda i: (0, 0), lambda i: (0, 0, 0)),
        out_specs=[pl.BlockSpec((rows, SSM_WIDTH), lambda i: (0, 0)),
                   pl.BlockSpec((nseq, 2 * SSM_N), lambda i: (0, 0))],
        out_shape=[jax.ShapeDtypeStruct((rows, SSM_WIDTH), BF16),
                   jax.ShapeDtypeStruct((nseq, 2 * SSM_N), F32)],
        scratch_shapes=[pltpu.VMEM((rows, 2 * SSM_N), F32), pltpu.VMEM((SCAN_ROWS, 2 * SSM_N), F32)],
        compiler_params=_cparams(("arbitrary",)),
        name="ssm_sample",
    )(proj2d, h0, sp["wb"], sp["tab"], sp["wc"], sp["d"], sp["wglu"])


def _ssm_params(lam_re, lam_im, log_dt, b_re, b_im, c_re, c_im, d_skip, w_glu):
    lr, li = lam_re.astype(F32), lam_im.astype(F32)
    dt = jnp.exp(log_dt.astype(F32))[:, None]
    mag = jnp.exp(lr * dt)
    ab_re, ab_im = mag * jnp.cos(li * dt), mag * jnp.sin(li * dt)
    den = lr * lr + li * li
    nr = ab_re - 1.0
    f_re = (nr * lr + ab_im * li) / den
    f_im = (ab_im * lr - nr * li) / den
    br, bi = b_re.astype(F32), b_im.astype(F32)
    bb_re = f_re[..., None] * br - f_im[..., None] * bi
    bb_im = f_re[..., None] * bi + f_im[..., None] * br
    eye = jnp.eye(SSM_GROUPS, dtype=F32)

    def in_mat(bb):
        return jnp.einsum('gpc,gh->gchp', bb, eye).reshape(SSM_WIDTH, SSM_N)

    def out_mat(c):
        return jnp.einsum('gcp,gh->gphc', c.astype(F32), eye).reshape(SSM_N, SSM_WIDTH)

    wb = jnp.concatenate([in_mat(bb_re), in_mat(bb_im)], axis=1).astype(BF16)
    wc = jnp.concatenate([out_mat(c_re), -out_mat(c_im)], axis=0).astype(BF16)

    ar, ai = ab_re.reshape(1, SSM_N), ab_im.reshape(1, SSM_N)

    def cmul(x, y):
        return (x[0] * y[0] - x[1] * y[1], x[0] * y[1] + x[1] * y[0])

    pw = [(ar, ai)]
    for _ in range(SCAN_ROWS - 1):
        pw.append(cmul(pw[-1], (ar, ai)))
    row = jnp.arange(SCAN_ROWS)[:, None]
    tabs = []
    for shift in (1, 2, 4):
        for part in pw[shift - 1]:
            tabs.append(jnp.where(row >= shift, part, 0.0))
    tabs.append(jnp.concatenate([p[0] for p in pw], axis=0))
    tabs.append(jnp.concatenate([p[1] for p in pw], axis=0))
    tab = jnp.stack([jnp.broadcast_to(t, (SCAN_ROWS, SSM_N)) for t in tabs])
    return dict(wb=wb, tab=tab, wc=wc, d=d_skip.astype(F32).reshape(1, SSM_WIDTH), wglu=w_glu.astype(BF16))


_RET_G = 1.0 - np.exp2(-5.0 - np.arange(RET_HEADS, dtype=np.float64))


def _ret_consts(chunk, n_rows):
    idx = np.arange(n_rows)
    loc = idx % chunk
    same = (idx[:, None] // chunk) == (idx[None, :] // chunk)
    diff = loc[:, None] - loc[None, :]
    dec = np.where(same & (diff >= 0), _RET_G[:, None, None] ** np.maximum(diff, 0)[None], 0.0)
    qdec = np.repeat((_RET_G[None, :] ** (loc[:, None] + 1.0)), HEAD_DIM, axis=1)
    kdec = np.repeat((_RET_G[None, :] ** (chunk - 1.0 - loc[:, None])), HEAD_DIM, axis=1)
    return (jnp.asarray(dec, F32), jnp.asarray(qdec, F32), jnp.asarray(kdec, F32),
            jnp.asarray(_RET_G ** chunk, F32))


def _rope_tables(pos):
    half = HEAD_DIM // 2
    theta = 1.0 / (ROPE_BASE ** np.linspace(0.0, 1.0, half))
    ang = np.asarray(pos, np.float64)[:, None] * theta[None, :]
    cos = np.repeat(np.cos(ang), 2, axis=1)
    sin = np.repeat(np.sin(ang), 2, axis=1) * np.tile([-1.0, 1.0], half)[None]
    return (jnp.asarray(np.tile(cos, (1, RET_HEADS)), F32), jnp.asarray(np.tile(sin, (1, RET_HEADS)), F32))


def _rotate_pairs(x, cos, sin_signed):
    lane = lax.broadcasted_iota(jnp.int32, x.shape, 1)
    nxt = pltpu.roll(x, x.shape[1] - 1, 1)
    prv = pltpu.roll(x, 1, 1)
    return x * cos + jnp.where(lane % 2 == 0, nxt, prv) * sin_signed


def _ret_head_out(o, gate, norm):
    ms = jnp.mean(o * o, axis=-1, keepdims=True)
    return o * lax.rsqrt(ms + RMS_EPS) * norm * (gate * _sigmoid(gate))


def _ret_prompt_kernel(gc_ref, q_ref, k_ref, v_ref, g_ref, cos_ref, sin_ref, dec_ref, qdec_ref,
                       kdec_ref, norm_ref, y_ref, r_ref):
    @pl.when(pl.program_id(1) == 0)
    def _():
        r_ref[...] = jnp.zeros_like(r_ref)

    cos, sin = cos_ref[...], sin_ref[...]
    q = _rotate_pairs(q_ref[0], cos, sin)
    k = _rotate_pairs(k_ref[0], cos, sin) * (HEAD_DIM ** -0.5)
    v = v_ref[0]
    g = g_ref[0]
    kd = k * kdec_ref[...]
    qdec = qdec_ref[...]
    norm = norm_ref[...]
    for h in range(RET_HEADS):
        sl = slice(h * HEAD_DIM, (h + 1) * HEAD_DIM)
        qh, kh, vh = q[:, sl].astype(BF16), k[:, sl].astype(BF16), v[:, sl].astype(BF16)
        s = _dot_nt(qh, kh) * dec_ref[h]
        r = r_ref[0, h]
        o = _dot(s.astype(BF16), vh) + _dot(qh, r.astype(BF16)) * qdec[:, sl]
        y_ref[0, :, sl] = _ret_head_out(o, g[:, sl], norm[:, sl]).astype(y_ref.dtype)
        r_ref[0, h] = gc_ref[h] * r + _dot_tn(kd[:, sl].astype(BF16), vh)


def _ret_prompt(proj3d, cos, sin, norm):
    b, l, _ = proj3d.shape
    c = RET_CHUNK
    dec, qdec, kdec, gc = _ret_consts(c, c)
    blk = (1, c, MIX_BLOCK)
    tspec = pl.BlockSpec((c, MIX_BLOCK), lambda bi, i: (i, 0))
    cspec = pl.BlockSpec((c, MIX_BLOCK), lambda bi, i: (0, 0))
    return pl.pallas_call(
        _ret_prompt_kernel,
        grid=(b, l // c),
        in_specs=[pl.BlockSpec(memory_space=pltpu.SMEM),
                  pl.BlockSpec(blk, lambda bi, i: (bi, i, COL_RET_Q)),
                  pl.BlockSpec(blk, lambda bi, i: (bi, i, COL_RET_K)),
                  pl.BlockSpec(blk, lambda bi, i: (bi, i, COL_RET_V)),
                  pl.BlockSpec(blk, lambda bi, i: (bi, i, COL_RET_G)),
                  tspec, tspec,
                  pl.BlockSpec((RET_HEADS, c, c), lambda bi, i: (0, 0, 0)),
                  cspec, cspec,
                  pl.BlockSpec((1, MIX_BLOCK), lambda bi, i: (0, 0))],
        out_specs=[pl.BlockSpec(blk, lambda bi, i: (bi, i, 0)),
                   pl.BlockSpec((1, RET_HEADS, HEAD_DIM, HEAD_DIM), lambda bi, i: (bi, 0, 0, 0))],
        out_shape=[jax.ShapeDtypeStruct((b, l, MIX_BLOCK), BF16),
                   jax.ShapeDtypeStruct((b, RET_HEADS, HEAD_DIM, HEAD_DIM), F32)],
        compiler_params=_cparams(("parallel", "arbitrary")),
        name="ret_prompt",
    )(gc, proj3d, proj3d, proj3d, proj3d, cos, sin, dec, qdec, kdec, norm)


def _ret_sample_kernel(gc_ref, q_ref, k_ref, v_ref, g_ref, cos_ref, sin_ref, dec_ref, qdec_ref,
                       kdec_ref, norm_ref, r0_ref, y_ref, rn_ref, *, n_new, s_blk):
    cos, sin = cos_ref[...], sin_ref[...]
    q = _rotate_pairs(q_ref[...], cos, sin)
    k = _rotate_pairs(k_ref[...], cos, sin) * (HEAD_DIM ** -0.5)
    v = v_ref[...]
    g = g_ref[...]
    kd = k * kdec_ref[...]
    qdec = qdec_ref[...]
    norm = norm_ref[...]
    rows = s_blk * n_new
    seq = lax.broadcasted_iota(jnp.int32, (rows, HEAD_DIM), 0) // n_new
    for h in range(RET_HEADS):
        sl = slice(h * HEAD_DIM, (h + 1) * HEAD_DIM)
        qf, kdf = q[:, sl], kd[:, sl]
        qh, kh, vh = qf.astype(BF16), k[:, sl].astype(BF16), v[:, sl].astype(BF16)
        s = _dot_nt(qh, kh) * dec_ref[h]
        cross = jnp.zeros((rows, HEAD_DIM), F32)
        for si in range(s_blk):
            mine = seq == si
            r = r0_ref[si, h]
            cross = cross + _dot(jnp.where(mine, qf, 0.0).astype(BF16), r.astype(BF16))
            rn_ref[si, h] = gc_ref[h] * r + _dot_tn(jnp.where(mine, kdf, 0.0).astype(BF16), vh)
        o = _dot(s.astype(BF16), vh) + cross * qdec[:, sl]
        y_ref[:, sl] = _ret_head_out(o, g[:, sl], norm[:, sl]).astype(y_ref.dtype)


def _ret_sample(proj2d, r0, cos, sin, norm, n_new, s_blk):
    rows = s_blk * n_new
    nseq = r0.shape[0]
    dec, qdec, kdec, gc = _ret_consts(n_new, rows)
    blk = (rows, MIX_BLOCK)
    cspec = pl.BlockSpec(blk, lambda i: (0, 0))
    rblk = (s_blk, RET_HEADS, HEAD_DIM, HEAD_DIM)
    return pl.pallas_call(
        functools.partial(_ret_sample_kernel, n_new=n_new, s_blk=s_blk),
        grid=(nseq // s_blk,),
        in_specs=[pl.BlockSpec(memory_space=pltpu.SMEM),
                  pl.BlockSpec(blk, lambda i: (i, COL_RET_Q)),
                  pl.BlockSpec(blk, lambda i: (i, COL_RET_K)),
                  pl.BlockSpec(blk, lambda i: (i, COL_RET_V)),
                  pl.BlockSpec(blk, lambda i: (i, COL_RET_G)),
                  cspec, cspec,
                  pl.BlockSpec((RET_HEADS, rows, rows), lambda i: (0, 0, 0)),
                  cspec, cspec,
                  pl.BlockSpec((1, MIX_BLOCK), lambda i: (0, 0)),
                  pl.BlockSpec(rblk, lambda i: (i, 0, 0, 0))],
        out_specs=[pl.BlockSpec(blk, lambda i: (i, 0)),
                   pl.BlockSpec(rblk, lambda i: (i, 0, 0, 0))],
        out_shape=[jax.ShapeDtypeStruct((nseq * n_new, MIX_BLOCK), BF16),
                   jax.ShapeDtypeStruct((nseq, RET_HEADS, HEAD_DIM, HEAD_DIM), F32)],
        compiler_params=_cparams(("parallel",)),
        name="ret_sample",
    )(gc, proj2d, proj2d, proj2d, proj2d, cos, sin, dec, qdec, kdec, norm, r0)


def _block_diag(w):
    g, n, _ = w.shape
    return jnp.einsum('gcd,gh->gchd', w, jnp.eye(g, dtype=w.dtype)).reshape(g * n, g * n)


def _layer_params(l, p):
    return dict(
        norm_mix=p['norm_mix'][l].reshape(1, D_MODEL),
        norm_ffn=p['norm_ffn'][l].reshape(1, D_MODEL),
        w_in=p['w_in'][l].astype(BF16),
        w_out=p['w_out'][l].astype(BF16),
        pool_w=_block_diag(p['pool_w'][l].astype(F32)).astype(BF16),
        pool_scale=p['pool_scale'][l].astype(F32).reshape(1, POOL_WIDTH),
        qg=p['swa_q_norm'][l].astype(F32).reshape(1, HEAD_DIM),
        kg=p['swa_k_norm'][l].astype(F32).reshape(1, HEAD_DIM),
        sinks=p['swa_sinks'][l].astype(F32),
        ssm=_ssm_params(p['ssm_lambda_re'][l], p['ssm_lambda_im'][l], p['ssm_log_dt'][l],
                        p['ssm_b_re'][l], p['ssm_b_im'][l], p['ssm_c_re'][l], p['ssm_c_im'][l],
                        p['ssm_d'][l], p['ssm_w_glu'][l]),
        ret_norm=p['ret_norm'][l].astype(F32).reshape(1, MIX_BLOCK),
    )


def _channel_mix(x, l, lp, p, tm):
    i = l // 2
    if l % 2 == 0:
        return _ffn(x, lp['norm_ffn'], None, p['ffn_w_gate'][i:i + 1].astype(BF16),
                    p['ffn_w_up'][i:i + 1].astype(BF16), p['ffn_w_down'][i:i + 1].astype(BF16), tm, 256)
    wr = jnp.pad(p['moe_router'][i].astype(F32), ((0, 0), (0, 128 - N_EXPERTS)))
    combine = _router(x, lp['norm_ffn'], wr, 512)
    return _ffn(x, lp['norm_ffn'], combine, p['moe_w_gate'][i].astype(BF16),
                p['moe_w_up'][i].astype(BF16), p['moe_w_down'][i].astype(BF16), tm, 256)


def _trunk_prompt(x, p, lps, rel_bias):
    b, l, d = x.shape
    x2 = x.reshape(b * l, d)
    rel = np.arange(SWA_BLOCK)[:, None] - np.arange(2 * SWA_BLOCK)[None, :] + SWA_BLOCK
    bias = _swa_bias(rel_bias, rel)
    cos, sin = _rope_tables(np.arange(l))
    states = [[] for _ in range(5)]
    for li, lp in enumerate(lps):
        proj2 = _norm_matmul(x2, lp['norm_mix'], lp['w_in'], 512)
        proj3 = proj2.reshape(b, l, IN_WIDTH)
        tb = 512
        y_pool = _pool(proj2, COL_POOL, lp['pool_w'], lp['pool_scale'], n_rows=b * l, tb=tb,
                       tiles_per_seq=l // tb, pos0=0)
        y_swa, kn = _swa_prompt(proj3, lp['sinks'], lp['qg'], lp['kg'], bias)
        y_ssm, hn = _ssm_prompt(proj3, lp['ssm'], 512)
        y_ret, rn = _ret_prompt(proj3, cos, sin, lp['ret_norm'])
        ys = (y_pool, y_swa.reshape(b * l, MIX_BLOCK), y_ssm.reshape(b * l, MIX_BLOCK),
              y_ret.reshape(b * l, MIX_BLOCK))
        x2 = _out_proj(x2, ys, lp['w_out'], 512)
        x2 = _channel_mix(x2, li, lp, p, 1024)
        kw = SWA_KV_HEADS * HEAD_DIM
        states[0].append(proj3[:, l - POOL_BUF:, :POOL_WIDTH])
        states[1].append(kn.reshape(b, SWA_WINDOW, SWA_KV_HEADS, HEAD_DIM))
        states[2].append(proj3[:, l - SWA_WINDOW:, COL_SWA_KV * MIX_BLOCK + kw:(COL_SWA_KV + 1) * MIX_BLOCK]
                         .reshape(b, SWA_WINDOW, SWA_KV_HEADS, HEAD_DIM))
        hn = hn[:, 0]
        states[3].append(jnp.stack([hn[:, :SSM_N], hn[:, SSM_N:]], axis=-1)
                         .reshape(b, SSM_GROUPS, SSM_STATE, 2))
        states[4].append(rn)
    return (x2.reshape(b, l, d),) + tuple(jnp.stack(s) for s in states)


def _trunk_sample(x, start_pos, state_pool, cache_k, cache_v, state_ssm, state_ret, p, lps, rel_bias):
    nseq, n_new, d = x.shape
    rows = nseq * n_new
    x2 = x.reshape(rows, d)
    wb = cache_k.shape[2]
    rel = np.arange(n_new)[:, None] - np.arange(wb + n_new)[None, :] + wb
    bias = _swa_bias(rel_bias, rel)
    s_blk = 16
    cos, sin = _rope_tables(start_pos + (np.arange(s_blk * n_new) % n_new))
    kw = SWA_KV_HEADS * HEAD_DIM
    ext_rows = POOL_HALO + n_new
    states = [[] for _ in range(5)]
    for li, lp in enumerate(lps):
        proj2 = _norm_matmul(x2, lp['norm_mix'], lp['w_in'], 512)
        proj3 = proj2.reshape(nseq, n_new, IN_WIDTH)
        u_pool = proj3[:, :, :POOL_WIDTH]
        buf = state_pool[li].astype(F32)
        ext = jnp.concatenate([jnp.zeros((nseq, POOL_HALO - POOL_BUF, POOL_WIDTH), F32), buf, u_pool], axis=1)
        y_pool = _pool(ext.reshape(nseq * ext_rows, POOL_WIDTH), 0, lp['pool_w'], lp['pool_scale'],
                       n_rows=nseq * ext_rows, tb=nseq * ext_rows, tiles_per_seq=1, pos0=start_pos)
        y_pool = y_pool.reshape(nseq, ext_rows, POOL_WIDTH)[:, POOL_HALO:].reshape(rows, POOL_WIDTH)
        y_swa, nk, nv = _swa_sample(proj3, cache_k[li].reshape(nseq, wb, kw).astype(F32),
                                    cache_v[li].reshape(nseq, wb, kw).astype(F32),
                                    lp['sinks'], lp['qg'], lp['kg'], bias, s_blk)
        h0 = state_ssm[li].astype(F32).reshape(nseq, SSM_N, 2)
        h0 = jnp.concatenate([h0[..., 0], h0[..., 1]], axis=1)
        y_ssm, hn = _ssm_sample(proj2, h0, lp['ssm'])
        y_ret, rn = _ret_sample(proj2, state_ret[li].astype(F32), cos, sin, lp['ret_norm'], n_new, s_blk)
        ys = (y_pool, y_swa.reshape(rows, MIX_BLOCK), y_ssm, y_ret)
        x2 = _out_proj(x2, ys, lp['w_out'], 512)
        x2 = _channel_mix(x2, li, lp, p, 1024)
        states[0].append(jnp.concatenate([buf, u_pool], axis=1)[:, -POOL_BUF:])
        states[1].append(nk.reshape(nseq, SWA_WINDOW, SWA_KV_HEADS, HEAD_DIM))
        states[2].append(nv.reshape(nseq, SWA_WINDOW, SWA_KV_HEADS, HEAD_DIM))
        states[3].append(jnp.stack([hn[:, :SSM_N], hn[:, SSM_N:]], axis=-1)
                         .reshape(nseq, SSM_GROUPS, SSM_STATE, 2))
        states[4].append(rn)
    return (x2.reshape(nseq, n_new, d),) + tuple(jnp.stack(s) for s in states)


PAST_LEN = 16384


def kernel(x_prompt, x_sample, state_pool, cache_swa_k, cache_swa_v, state_ssm, state_ret,
           norm_mix, norm_ffn, w_in, w_out, pool_w, pool_scale, swa_q_norm, swa_k_norm, swa_sinks,
           rel_bias, ssm_lambda_re, ssm_lambda_im, ssm_log_dt, ssm_b_re, ssm_b_im, ssm_c_re, ssm_c_im,
           ssm_d, ssm_w_glu, ret_norm, ffn_w_gate, ffn_w_up, ffn_w_down, moe_router, moe_w_gate,
           moe_w_up, moe_w_down):
    p = dict(norm_mix=norm_mix, norm_ffn=norm_ffn, w_in=w_in, w_out=w_out, pool_w=pool_w,
             pool_scale=pool_scale, swa_q_norm=swa_q_norm, swa_k_norm=swa_k_norm, swa_sinks=swa_sinks,
             ssm_lambda_re=ssm_lambda_re, ssm_lambda_im=ssm_lambda_im, ssm_log_dt=ssm_log_dt,
             ssm_b_re=ssm_b_re, ssm_b_im=ssm_b_im, ssm_c_re=ssm_c_re, ssm_c_im=ssm_c_im,
             ssm_d=ssm_d, ssm_w_glu=ssm_w_glu, ret_norm=ret_norm,
             ffn_w_gate=ffn_w_gate, ffn_w_up=ffn_w_up, ffn_w_down=ffn_w_down, moe_router=moe_router,
             moe_w_gate=moe_w_gate, moe_w_up=moe_w_up, moe_w_down=moe_w_down)
    depth = norm_mix.shape[0]
    lps = [_layer_params(l, p) for l in range(depth)]
    y_p, pool_p, k_p, v_p, ssm_p, ret_p = _trunk_prompt(x_prompt, p, lps, rel_bias)
    y_s, pool_s, k_s, v_s, ssm_s, ret_s = _trunk_sample(
        x_sample, PAST_LEN, state_pool, cache_swa_k, cache_swa_v, state_ssm, state_ret, p, lps, rel_bias)
    return (y_p, y_s, pool_p, pool_s, k_p, k_s, v_p, v_s, ssm_p, ssm_s, ret_p, ret_s)
```

```python
import functools
import math

import numpy as np
import jax
import jax.numpy as jnp
from jax import lax
from jax.experimental import pallas as pl
from jax.experimental.pallas import tpu as pltpu

F32 = jnp.float32
BF16 = jnp.bfloat16

D_MODEL = 1024
HEAD_DIM = 64
POOL_WIDTH = 256
POOL_WINDOWS = (2, 4, 8, 16)
POOL_BUF = 15
POOL_HALO = 16
SWA_HEADS = 4
SWA_KV_HEADS = 2
SWA_WINDOW = 128
SWA_BLOCK = 128
SSM_WIDTH = 256
SSM_CH = 16
SSM_GROUPS = 16
SSM_STATE = 64
SSM_N = SSM_GROUPS * SSM_STATE
RET_HEADS = 4
RET_CHUNK = 128
ROPE_BASE = 10000.0
IN_WIDTH = 2048
MIX_BLOCK = 256
D_FF = 2816
N_EXPERTS = 8
T5_BUCKETS = 32
T5_MAX_DIST = 128
RMS_EPS = 1e-6
NEG = -1e30
SCAN_ROWS = 8

COL_POOL, COL_SWA_Q, COL_SWA_KV, COL_SSM, COL_RET_Q, COL_RET_K, COL_RET_V, COL_RET_G = range(8)

VMEM_LIMIT = 48 * 1024 * 1024


def _cparams(sem):
    return pltpu.CompilerParams(dimension_semantics=sem, vmem_limit_bytes=VMEM_LIMIT)


def _rms(x, g):
    ms = jnp.mean(x * x, axis=-1, keepdims=True)
    return x * lax.rsqrt(ms + RMS_EPS) * g


def _dot(a, b):
    return jnp.dot(a, b, preferred_element_type=F32)


def _dot_nt(a, b):
    return lax.dot_general(a, b, (((1,), (1,)), ((), ())), preferred_element_type=F32)


def _dot_tn(a, b):
    return lax.dot_general(a, b, (((0,), (0,)), ((), ())), preferred_element_type=F32)


def _sigmoid(x):
    return 1.0 / (1.0 + jnp.exp(-x))


def _norm_matmul_kernel(x_ref, g_ref, w_ref, o_ref):
    h = _rms(x_ref[...], g_ref[...]).astype(BF16)
    o_ref[...] = _dot(h, w_ref[...])


def _norm_matmul(x, g, w, tm):
    t, d = x.shape
    tm = min(tm, t)
    n = w.shape[1]
    return pl.pallas_call(
        _norm_matmul_kernel,
        grid=(t // tm,),
        in_specs=[pl.BlockSpec((tm, d), lambda i: (i, 0)),
                  pl.BlockSpec((1, d), lambda i: (0, 0)),
                  pl.BlockSpec((d, n), lambda i: (0, 0))],
        out_specs=pl.BlockSpec((tm, n), lambda i: (i, 0)),
        out_shape=jax.ShapeDtypeStruct((t, n), F32),
        compiler_params=_cparams(("parallel",)),
        name="norm_matmul",
    )(x, g, w)


def _out_proj_kernel(x_ref, y0_ref, y1_ref, y2_ref, y3_ref, w_ref, o_ref):
    acc = x_ref[...]
    for k, y_ref in enumerate((y0_ref, y1_ref, y2_ref, y3_ref)):
        acc = acc + _dot(y_ref[...], w_ref[k * MIX_BLOCK:(k + 1) * MIX_BLOCK, :])
    o_ref[...] = acc


def _out_proj(x, ys, w, tm):
    t, d = x.shape
    tm = min(tm, t)
    yspec = pl.BlockSpec((tm, MIX_BLOCK), lambda i: (i, 0))
    return pl.pallas_call(
        _out_proj_kernel,
        grid=(t // tm,),
        in_specs=[pl.BlockSpec((tm, d), lambda i: (i, 0)), yspec, yspec, yspec, yspec,
                  pl.BlockSpec((d, d), lambda i: (0, 0))],
        out_specs=pl.BlockSpec((tm, d), lambda i: (i, 0)),
        out_shape=jax.ShapeDtypeStruct((t, d), F32),
        compiler_params=_cparams(("parallel",)),
        name="out_proj",
    )(x, *ys, w)


def _swiglu_chunk(h, wg, wu, wd):
    a = _dot(h, wg)
    b = _dot(h, wu)
    return _dot((a * _sigmoid(a) * b).astype(BF16), wd)


def _ffn_kernel(x_ref, g_ref, wg_ref, wu_ref, wd_ref, o_ref, h_scr):
    @pl.when(pl.program_id(1) == 0)
    def _():
        x = x_ref[...]
        h_scr[...] = _rms(x, g_ref[...]).astype(BF16)
        o_ref[...] = x

    o_ref[...] += _swiglu_chunk(h_scr[...], wg_ref[...], wu_ref[...], wd_ref[...])


def _ffn(x, g, wg, wu, wd, tm, tf):
    t, d = x.shape
    tm = min(tm, t)
    f = wg.shape[1]
    return pl.pallas_call(
        _ffn_kernel,
        grid=(t // tm, f // tf),
        in_specs=[pl.BlockSpec((tm, d), lambda i, j: (i, 0)),
                  pl.BlockSpec((1, d), lambda i, j: (0, 0)),
                  pl.BlockSpec((d, tf), lambda i, j: (0, j)),
                  pl.BlockSpec((d, tf), lambda i, j: (0, j)),
                  pl.BlockSpec((tf, d), lambda i, j: (j, 0))],
        out_specs=pl.BlockSpec((tm, d), lambda i, j: (i, 0)),
        out_shape=jax.ShapeDtypeStruct((t, d), F32),
        scratch_shapes=[pltpu.VMEM((tm, d), BF16)],
        compiler_params=_cparams(("parallel", "arbitrary")),
        name="ffn",
    )(x, g, wg, wu, wd)


ROUTE_ID_LANES = (0, 1)
ROUTE_GATE_LANES = (2, 3)


def _router_kernel(x_ref, g_ref, wr_ref, c_ref):
    h = _rms(x_ref[...], g_ref[...])
    logits = jnp.dot(h, wr_ref[...], preferred_element_type=F32, precision=lax.Precision.HIGHEST)
    lane = lax.broadcasted_iota(jnp.int32, logits.shape, 1).astype(F32)
    lg = jnp.where(lane < N_EXPERTS, logits, NEG)
    m1 = jnp.max(lg, axis=-1, keepdims=True)
    i1 = jnp.min(jnp.where(lg == m1, lane, 128.0), axis=-1, keepdims=True)
    lg2 = jnp.where(lane == i1, NEG, lg)
    m2 = jnp.max(lg2, axis=-1, keepdims=True)
    i2 = jnp.min(jnp.where(lg2 == m2, lane, 128.0), axis=-1, keepdims=True)
    ex = jnp.exp(m2 - m1)
    vals = (i1, i2, 1.0 / (1.0 + ex), ex / (1.0 + ex))
    out = jnp.zeros_like(logits)
    for ln, v in zip(ROUTE_ID_LANES + ROUTE_GATE_LANES, vals):
        out = jnp.where(lane == ln, v, out)
    c_ref[...] = out


def _router(x, g, wr, tm):
    t, d = x.shape
    tm = min(tm, t)
    return pl.pallas_call(
        _router_kernel,
        grid=(t // tm,),
        in_specs=[pl.BlockSpec((tm, d), lambda i: (i, 0)),
                  pl.BlockSpec((1, d), lambda i: (0, 0)),
                  pl.BlockSpec((d, 128), lambda i: (0, 0))],
        out_specs=pl.BlockSpec((tm, 128), lambda i: (i, 0)),
        out_shape=jax.ShapeDtypeStruct((t, 128), F32),
        compiler_params=_cparams(("parallel",)),
        name="router",
    )(x, g, wr)


def _row_copy(src, i, dst, j, sem):
    return pltpu.make_async_copy(src.at[pl.ds(i, 1)], dst.at[pl.ds(j, 1)], sem)


def _dispatch_kernel(pos_ref, x_hbm, xs_in, xs_hbm, sem, *, td):
    del xs_in
    base = pl.program_id(0) * td

    def issue(j, c):
        for k in range(2):
            _row_copy(x_hbm, base + j, xs_hbm, pos_ref[0, 0, 2 * j + k], sem).start()
        return c

    lax.fori_loop(0, td, issue, 0)
    pltpu.make_async_copy(x_hbm.at[pl.ds(0, 2 * td)], xs_hbm.at[pl.ds(0, 2 * td)], sem).wait()


def _dispatch(x, pos, xs, td):
    t, d = x.shape
    td = min(td, t)
    pos3 = pos.reshape(t // td, 1, 2 * td)
    return pl.pallas_call(
        functools.partial(_dispatch_kernel, td=td),
        grid=(t // td,),
        in_specs=[pl.BlockSpec((1, 1, 2 * td), lambda i: (i, 0, 0), memory_space=pltpu.SMEM),
                  pl.BlockSpec(memory_space=pl.ANY),
                  pl.BlockSpec(memory_space=pl.ANY)],
        out_specs=pl.BlockSpec(memory_space=pl.ANY),
        out_shape=jax.ShapeDtypeStruct(xs.shape, xs.dtype),
        scratch_shapes=[pltpu.SemaphoreType.DMA],
        input_output_aliases={2: 0},
        compiler_params=_cparams(("arbitrary",)),
        name="moe_dispatch",
    )(pos3, x, xs)


def _grouped_ffn_kernel(te_ref, nu_ref, x_ref, g_ref, wg_ref, wu_ref, wd_ref, o_ref, h_scr):
    del te_ref
    j = pl.program_id(1)
    used = pl.program_id(0) < nu_ref[0]

    @pl.when(jnp.logical_not(used) & (j == 0))
    def _():
        o_ref[...] = jnp.zeros_like(o_ref)

    @pl.when(used)
    def _():
        @pl.when(j == 0)
        def _():
            h_scr[...] = _rms(x_ref[...], g_ref[...]).astype(BF16)

        y = _swiglu_chunk(h_scr[...], wg_ref[0], wu_ref[0], wd_ref[0])

        @pl.when(j == 0)
        def _():
            o_ref[...] = y

        @pl.when(j > 0)
        def _():
            o_ref[...] += y


def _grouped_ffn(xs, g, tile_expert, n_used, wg, wu, wd, tm, tf):
    r, d = xs.shape
    f = wg.shape[2]
    nj = f // tf

    def row_map(i, j, te, nu):
        return (i, 0)

    def col_of(i, j, nu):
        return jnp.where(i < nu[0], j, nj - 1)

    grid_spec = pltpu.PrefetchScalarGridSpec(
        num_scalar_prefetch=2,
        grid=(r // tm, nj),
        in_specs=[pl.BlockSpec((tm, d), row_map),
                  pl.BlockSpec((1, d), lambda i, j, te, nu: (0, 0)),
                  pl.BlockSpec((1, d, tf), lambda i, j, te, nu: (te[i], 0, col_of(i, j, nu))),
                  pl.BlockSpec((1, d, tf), lambda i, j, te, nu: (te[i], 0, col_of(i, j, nu))),
                  pl.BlockSpec((1, tf, d), lambda i, j, te, nu: (te[i], col_of(i, j, nu), 0))],
        out_specs=pl.BlockSpec((tm, d), row_map),
        scratch_shapes=[pltpu.VMEM((tm, d), BF16)],
    )
    return pl.pallas_call(
        _grouped_ffn_kernel,
        grid_spec=grid_spec,
        out_shape=jax.ShapeDtypeStruct((r, d), F32),
        compiler_params=_cparams(("arbitrary", "arbitrary")),
        name="moe_grouped_ffn",
    )(tile_expert, n_used, xs, g, wg, wu, wd)


def _combine_kernel(pos_ref, x_ref, route_ref, ys_hbm, o_ref, buf0, buf1, sem, *, tc):
    def issue(j, c):
        _row_copy(ys_hbm, pos_ref[0, 0, 2 * j], buf0, j, sem).start()
        _row_copy(ys_hbm, pos_ref[0, 0, 2 * j + 1], buf1, j, sem).start()
        return c

    lax.fori_loop(0, tc, issue, 0)
    for buf in (buf0, buf1):
        pltpu.make_async_copy(ys_hbm.at[pl.ds(0, tc)], buf, sem).wait()
    route = route_ref[...]
    g0 = route[:, ROUTE_GATE_LANES[0]:ROUTE_GATE_LANES[0] + 1]
    g1 = route[:, ROUTE_GATE_LANES[1]:ROUTE_GATE_LANES[1] + 1]
    o_ref[...] = x_ref[...] + g0 * buf0[...] + g1 * buf1[...]


def _combine(x, route, pos, ys, tc):
    t, d = x.shape
    tc = min(tc, t)
    pos3 = pos.reshape(t // tc, 1, 2 * tc)
    return pl.pallas_call(
        functools.partial(_combine_kernel, tc=tc),
        grid=(t // tc,),
        in_specs=[pl.BlockSpec((1, 1, 2 * tc), lambda i: (i, 0, 0), memory_space=pltpu.SMEM),
                  pl.BlockSpec((tc, d), lambda i: (i, 0)),
                  pl.BlockSpec((tc, 128), lambda i: (i, 0)),
                  pl.BlockSpec(memory_space=pl.ANY)],
        out_specs=pl.BlockSpec((tc, d), lambda i: (i, 0)),
        out_shape=jax.ShapeDtypeStruct((t, d), F32),
        scratch_shapes=[pltpu.VMEM((tc, d), F32), pltpu.VMEM((tc, d), F32), pltpu.SemaphoreType.DMA],
        compiler_params=_cparams(("arbitrary",)),
        name="moe_combine",
    )(pos3, x, route, ys)


MOE_TM = 512


def _route_plan(expert_ids, tm):
    flat = expert_ids.reshape(-1)
    a = flat.shape[0]
    onehot = (flat[None, :] == jnp.arange(N_EXPERTS, dtype=jnp.int32)[:, None]).astype(jnp.int32)
    csum = jnp.cumsum(onehot, axis=1)
    counts = csum[:, -1]
    padded = (counts + tm - 1) // tm * tm
    ends = jnp.cumsum(padded)
    offs = ends - padded
    pos = jnp.sum(onehot * (offs[:, None] + csum - 1), axis=0)
    n_tiles = (a + N_EXPERTS * tm) // tm
    tile_start = jnp.arange(n_tiles, dtype=jnp.int32) * tm
    tile_expert = jnp.minimum(jnp.sum(tile_start[:, None] >= ends[None, :], axis=1), N_EXPERTS - 1)
    n_used = (ends[-1] // tm).reshape(1)
    last = jnp.take(tile_expert, n_used[0] - 1)
    tile_expert = jnp.where(tile_start < ends[-1], tile_expert, last)
    return pos.astype(jnp.int32), tile_expert.astype(jnp.int32), n_used.astype(jnp.int32), n_tiles * tm


def _moe(xs_list, g, wr, wg, wu, wd):
    routes = [_router(x, g, wr, 512) for x in xs_list]
    ids = jnp.concatenate([r[:, ROUTE_ID_LANES[0]:ROUTE_ID_LANES[1] + 1] for r in routes]).astype(jnp.int32)
    pos, tile_expert, n_used, n_rows = _route_plan(ids, MOE_TM)
    xs = jnp.zeros((n_rows, D_MODEL), F32)
    bounds = np.cumsum([0] + [2 * x.shape[0] for x in xs_list])
    pos_list = [pos[lo:hi] for lo, hi in zip(bounds[:-1], bounds[1:])]
    for x, ps in zip(xs_list, pos_list):
        xs = _dispatch(x, ps, xs, 1024)
    ys = _grouped_ffn(xs, g, tile_expert, n_used, wg, wu, wd, MOE_TM, D_FF // 2)
    return [_combine(x, r, ps, ys, 512) for x, r, ps in zip(xs_list, routes, pos_list)]


def _pool_kernel(u_ref, halo_ref, w_ref, scale_ref, o_ref, *, tiles_per_seq, pos0, tb):
    ti = pl.program_id(0) % tiles_per_seq
    u = u_ref[...]
    halo = jnp.where(ti == 0, 0.0, halo_ref[...])
    ext = jnp.concatenate([halo, u], axis=0)
    s2 = ext + pltpu.roll(ext, 1, 0)
    s4 = s2 + pltpu.roll(s2, 2, 0)
    s8 = s4 + pltpu.roll(s4, 4, 0)
    s16 = s8 + pltpu.roll(s8, 8, 0)
    grp = lax.broadcasted_iota(jnp.int32, (tb, POOL_WIDTH), 1) // (POOL_WIDTH // 4)
    row = lax.broadcasted_iota(jnp.int32, (tb, POOL_WIDTH), 0)
    s = jnp.where(grp == 0, s2[POOL_HALO:],
                  jnp.where(grp == 1, s4[POOL_HALO:],
                            jnp.where(grp == 2, s8[POOL_HALO:], s16[POOL_HALO:])))
    win = jnp.where(grp == 0, 2, jnp.where(grp == 1, 4, jnp.where(grp == 2, 8, 16)))
    cnt = jnp.minimum(win, pos0 + ti * tb + row + 1).astype(F32)
    pooled = s / cnt - u
    o_ref[...] = (_dot(pooled.astype(BF16), w_ref[...]) * scale_ref[...]).astype(o_ref.dtype)


def _pool(proj2d, col, w, scale, *, n_rows, tb, tiles_per_seq, pos0):
    per = tb // POOL_HALO
    return pl.pallas_call(
        functools.partial(_pool_kernel, tiles_per_seq=tiles_per_seq, pos0=pos0, tb=tb),
        grid=(n_rows // tb,),
        in_specs=[pl.BlockSpec((tb, POOL_WIDTH), lambda i: (i, col)),
                  pl.BlockSpec((POOL_HALO, POOL_WIDTH), lambda i: (jnp.maximum(i * per - 1, 0), col)),
                  pl.BlockSpec((POOL_WIDTH, POOL_WIDTH), lambda i: (0, 0)),
                  pl.BlockSpec((1, POOL_WIDTH), lambda i: (0, 0))],
        out_specs=pl.BlockSpec((tb, POOL_WIDTH), lambda i: (i, 0)),
        out_shape=jax.ShapeDtypeStruct((n_rows, POOL_WIDTH), BF16),
        compiler_params=_cparams(("parallel",)),
        name="pool",
    )(proj2d, proj2d, w, scale)


def _t5_bucket_np(rel):
    n = np.maximum(rel, 0)
    max_exact = T5_BUCKETS // 2
    nf = np.maximum(n, max_exact).astype(np.float32)
    large = max_exact + (np.log(nf / max_exact) / math.log(T5_MAX_DIST / max_exact)
                         * (T5_BUCKETS - max_exact)).astype(np.int32)
    large = np.minimum(large, T5_BUCKETS - 1)
    return np.where(n < max_exact, n, large)


def _swa_bias(rel_bias, rel):
    valid = (rel >= 0) & (rel < SWA_WINDOW)
    b = rel_bias.astype(F32)[_t5_bucket_np(rel)]
    b = jnp.transpose(b, (2, 0, 1))
    return jnp.where(valid[None], b, NEG)


def _softmax_parts(parts, sink):
    m = sink
    for s in parts:
        m = jnp.maximum(m, jnp.max(s, axis=-1, keepdims=True))
    ps = [jnp.exp(s - m) for s in parts]
    denom = jnp.exp(sink - m)
    for p in ps:
        denom = denom + jnp.sum(p, axis=-1, keepdims=True)
    return ps, denom


def _swa_prompt_kernel(sinks_ref, q_ref, kv_ref, kvp_ref, qg_ref, kg_ref, bias_ref, y_ref, kn_ref):
    has_prev = pl.program_id(1) > 0
    q = q_ref[0]
    kv = kv_ref[0]
    kvp = kvp_ref[0]
    qg = qg_ref[...]
    kg = kg_ref[...]
    kw = SWA_KV_HEADS * HEAD_DIM
    for kh in range(SWA_KV_HEADS):
        ksl = slice(kh * HEAD_DIM, (kh + 1) * HEAD_DIM)
        vsl = slice(kw + kh * HEAD_DIM, kw + (kh + 1) * HEAD_DIM)
        kc = _rms(kv[:, ksl], kg)
        kp = _rms(kvp[:, ksl], kg)
        kn_ref[0, :, ksl] = kc
        kcb, kpb = kc.astype(BF16), kp.astype(BF16)
        vc, vp = kv[:, vsl].astype(BF16), kvp[:, vsl].astype(BF16)
        for gq in range(SWA_HEADS // SWA_KV_HEADS):
            h = kh * (SWA_HEADS // SWA_KV_HEADS) + gq
            hsl = slice(h * HEAD_DIM, (h + 1) * HEAD_DIM)
            qn = _rms(q[:, hsl], qg).astype(BF16)
            s_c = _dot_nt(qn, kcb) * (HEAD_DIM ** -0.5) + bias_ref[h, :, SWA_BLOCK:]
            s_p = _dot_nt(qn, kpb) * (HEAD_DIM ** -0.5) + bias_ref[h, :, :SWA_BLOCK]
            s_p = jnp.where(has_prev, s_p, NEG)
            (p_c, p_p), denom = _softmax_parts((s_c, s_p), sinks_ref[h])
            o = _dot(p_c.astype(BF16), vc) + _dot(p_p.astype(BF16), vp)
            y_ref[0, :, hsl] = (o / denom).astype(y_ref.dtype)


def _swa_prompt(proj3d, sinks, qg, kg, bias):
    b, l, _ = proj3d.shape
    nb = l // SWA_BLOCK
    blk = (1, SWA_BLOCK, MIX_BLOCK)
    return pl.pallas_call(
        _swa_prompt_kernel,
        grid=(b, nb),
        in_specs=[pl.BlockSpec(memory_space=pltpu.SMEM),
                  pl.BlockSpec(blk, lambda bi, i: (bi, i, COL_SWA_Q)),
                  pl.BlockSpec(blk, lambda bi, i: (bi, i, COL_SWA_KV)),
                  pl.BlockSpec(blk, lambda bi, i: (bi, jnp.maximum(i - 1, 0), COL_SWA_KV)),
                  pl.BlockSpec((1, HEAD_DIM), lambda bi, i: (0, 0)),
                  pl.BlockSpec((1, HEAD_DIM), lambda bi, i: (0, 0)),
                  pl.BlockSpec((SWA_HEADS, SWA_BLOCK, 2 * SWA_BLOCK), lambda bi, i: (0, 0, 0))],
        out_specs=[pl.BlockSpec(blk, lambda bi, i: (bi, i, 0)),
                   pl.BlockSpec((1, SWA_BLOCK, SWA_KV_HEADS * HEAD_DIM), lambda bi, i: (bi, 0, 0))],
        out_shape=[jax.ShapeDtypeStruct((b, l, MIX_BLOCK), BF16),
                   jax.ShapeDtypeStruct((b, SWA_BLOCK, SWA_KV_HEADS * HEAD_DIM), F32)],
        compiler_params=_cparams(("parallel", "arbitrary")),
        name="swa_prompt",
    )(sinks, proj3d, proj3d, proj3d, qg, kg, bias)


def _swa_sample_kernel(sinks_ref, q_ref, kv_ref, ck_ref, cv_ref, qg_ref, kg_ref, bias_ref,
                       y_ref, nk_ref, nv_ref, *, n_new):
    q = q_ref[...]
    kv = kv_ref[...]
    ck = ck_ref[...]
    cv = cv_ref[...]
    qg = qg_ref[...]
    kg = kg_ref[...]
    kw = SWA_KV_HEADS * HEAD_DIM
    bdot = functools.partial(jnp.einsum, preferred_element_type=F32)
    for kh in range(SWA_KV_HEADS):
        ksl = slice(kh * HEAD_DIM, (kh + 1) * HEAD_DIM)
        vsl = slice(kw + kh * HEAD_DIM, kw + (kh + 1) * HEAD_DIM)
        kn = _rms(kv[:, :, ksl], kg)
        vn = kv[:, :, vsl]
        nk_ref[:, :SWA_WINDOW - n_new, ksl] = ck[:, n_new:, ksl]
        nk_ref[:, SWA_WINDOW - n_new:, ksl] = kn
        nv_ref[:, :SWA_WINDOW - n_new, ksl] = cv[:, n_new:, ksl]
        nv_ref[:, SWA_WINDOW - n_new:, ksl] = vn
        for gq in range(SWA_HEADS // SWA_KV_HEADS):
            h = kh * (SWA_HEADS // SWA_KV_HEADS) + gq
            hsl = slice(h * HEAD_DIM, (h + 1) * HEAD_DIM)
            qn = _rms(q[:, :, hsl], qg)
            s_c = bdot('sqd,skd->sqk', qn, ck[:, :, ksl]) * (HEAD_DIM ** -0.5) + bias_ref[h, :, :SWA_WINDOW]
            s_n = bdot('sqd,skd->sqk', qn, kn) * (HEAD_DIM ** -0.5) + bias_ref[h, :, SWA_WINDOW:]
            (p_c, p_n), denom = _softmax_parts((s_c, s_n), sinks_ref[h])
            o = bdot('sqk,skd->sqd', p_c, cv[:, :, ksl]) + bdot('sqk,skd->sqd', p_n, vn)
            y_ref[:, :, hsl] = (o / denom).astype(y_ref.dtype)


def _swa_sample(proj3d, cache_k, cache_v, sinks, qg, kg, bias, s_blk):
    nseq, n_new, _ = proj3d.shape
    kw = SWA_KV_HEADS * HEAD_DIM
    blk = (s_blk, n_new, MIX_BLOCK)
    cblk = (s_blk, SWA_WINDOW, kw)
    return pl.pallas_call(
        functools.partial(_swa_sample_kernel, n_new=n_new),
        grid=(nseq // s_blk,),
        in_specs=[pl.BlockSpec(memory_space=pltpu.SMEM),
                  pl.BlockSpec(blk, lambda i: (i, 0, COL_SWA_Q)),
                  pl.BlockSpec(blk, lambda i: (i, 0, COL_SWA_KV)),
                  pl.BlockSpec(cblk, lambda i: (i, 0, 0)),
                  pl.BlockSpec(cblk, lambda i: (i, 0, 0)),
                  pl.BlockSpec((1, HEAD_DIM), lambda i: (0, 0)),
                  pl.BlockSpec((1, HEAD_DIM), lambda i: (0, 0)),
                  pl.BlockSpec((SWA_HEADS, n_new, SWA_WINDOW + n_new), lambda i: (0, 0, 0))],
        out_specs=[pl.BlockSpec(blk, lambda i: (i, 0, 0)),
                   pl.BlockSpec(cblk, lambda i: (i, 0, 0)),
                   pl.BlockSpec(cblk, lambda i: (i, 0, 0))],
        out_shape=[jax.ShapeDtypeStruct((nseq, n_new, MIX_BLOCK), BF16),
                   jax.ShapeDtypeStruct((nseq, SWA_WINDOW, kw), F32),
                   jax.ShapeDtypeStruct((nseq, SWA_WINDOW, kw), F32)],
        compiler_params=_cparams(("parallel",)),
        name="swa_sample",
    )(sinks, proj3d, proj3d, cache_k, cache_v, qg, kg, bias)


def _ssm_kernel(u_ref, h0_ref, wb_ref, tab_ref, wc_ref, d_ref, wglu_ref, y_ref, hn_ref,
                bu_scr, carry_scr, *, chained, tiles_per_seq, tb):
    n = SSM_N
    u = u_ref[...]
    bu_scr[...] = _dot(u.astype(BF16), wb_ref[...])

    if chained:
        @pl.when(pl.program_id(1) % tiles_per_seq == 0)
        def _():
            carry_scr[...] = jnp.zeros_like(carry_scr)

    def tile_scan(r0, cr, ci):
        hr = bu_scr[pl.ds(r0, SCAN_ROWS), :n]
        hi = bu_scr[pl.ds(r0, SCAN_ROWS), n:]
        for k, shift in enumerate((1, 2, 4)):
            ar, ai = tab_ref[2 * k], tab_ref[2 * k + 1]
            sr, si = pltpu.roll(hr, shift, 0), pltpu.roll(hi, shift, 0)
            hr, hi = hr + ar * sr - ai * si, hi + ar * si + ai * sr
        pr, pi = tab_ref[6], tab_ref[7]
        hr, hi = hr + pr * cr - pi * ci, hi + pr * ci + pi * cr
        bu_scr[pl.ds(r0, SCAN_ROWS), :n] = hr
        bu_scr[pl.ds(r0, SCAN_ROWS), n:] = hi
        return hr[SCAN_ROWS - 1:], hi[SCAN_ROWS - 1:]

    if chained:
        def body(t, carry):
            r0 = pl.multiple_of(t * SCAN_ROWS, SCAN_ROWS)
            lr, li = tile_scan(r0, *carry)
            return (jnp.broadcast_to(lr, (SCAN_ROWS, n)), jnp.broadcast_to(li, (SCAN_ROWS, n)))

        cr, ci = lax.fori_loop(0, tb // SCAN_ROWS, body, (carry_scr[:, :n], carry_scr[:, n:]))
        carry_scr[:, :n] = cr
        carry_scr[:, n:] = ci
        hn_ref[0, :, :n] = cr
        hn_ref[0, :, n:] = ci
    else:
        def body(t, _):
            r0 = pl.multiple_of(t * SCAN_ROWS, SCAN_ROWS)
            h0 = h0_ref[pl.ds(t, 1), :]
            cr = jnp.broadcast_to(h0[:, :n], (SCAN_ROWS, n))
            ci = jnp.broadcast_to(h0[:, n:], (SCAN_ROWS, n))
            lr, li = tile_scan(r0, cr, ci)
            hn_ref[pl.ds(t, 1), :n] = lr
            hn_ref[pl.ds(t, 1), n:] = li
            return 0

        lax.fori_loop(0, tb // SCAN_ROWS, body, 0)

    y = _dot(bu_scr[...].astype(BF16), wc_ref[...]) + d_ref[...] * u
    y = 0.5 * y * (1.0 + jnp.tanh(math.sqrt(2.0 / math.pi) * (y + 0.044715 * (y * y * y))))
    y = y * _sigmoid(_dot(y.astype(BF16), wglu_ref[...]))
    if chained:
        y_ref[0] = y.astype(y_ref.dtype)
    else:
        y_ref[...] = y.astype(y_ref.dtype)


def _ssm_common_specs(zero_map2, zero_map3):
    return [pl.BlockSpec((SSM_WIDTH, 2 * SSM_N), zero_map2),
            pl.BlockSpec((8, SCAN_ROWS, SSM_N), zero_map3),
            pl.BlockSpec((2 * SSM_N, SSM_WIDTH), zero_map2),
            pl.BlockSpec((1, SSM_WIDTH), zero_map2),
            pl.BlockSpec((SSM_WIDTH, SSM_WIDTH), zero_map2)]


def _ssm_prompt(proj3d, sp, tb):
    b, l, _ = proj3d.shape
    nt = l // tb
    dummy_h0 = jnp.zeros((SCAN_ROWS, 2 * SSM_N), F32)
    kern = functools.partial(_ssm_kernel, chained=True, tiles_per_seq=nt, tb=tb)

    def kernel(u_ref, h0_ref, wb, tab, wc, d, wglu, y_ref, hn_ref, bu_scr, carry_scr):
        kern(u_ref.at[0], h0_ref, wb, tab, wc, d, wglu, y_ref, hn_ref, bu_scr, carry_scr)

    return pl.pallas_call(
        kernel,
        grid=(b, nt),
        in_specs=[pl.BlockSpec((1, tb, SSM_WIDTH), lambda bi, i: (bi, i, COL_SSM)),
                  pl.BlockSpec((SCAN_ROWS, 2 * SSM_N), lambda bi, i: (0, 0))]
                 + _ssm_common_specs(lambda bi, i: (0, 0), lambda bi, i: (0, 0, 0)),
        out_specs=[pl.BlockSpec((1, tb, SSM_WIDTH), lambda bi, i: (bi, i, 0)),
                   pl.BlockSpec((1, SCAN_ROWS, 2 * SSM_N), lambda bi, i: (bi, 0, 0))],
        out_shape=[jax.ShapeDtypeStruct((b, l, SSM_WIDTH), BF16),
                   jax.ShapeDtypeStruct((b, SCAN_ROWS, 2 * SSM_N), F32)],
        scratch_shapes=[pltpu.VMEM((tb, 2 * SSM_N), F32), pltpu.VMEM((SCAN_ROWS, 2 * SSM_N), F32)],
        compiler_params=_cparams(("parallel", "arbitrary")),
        name="ssm_prompt",
    )(proj3d, dummy_h0, sp["wb"], sp["tab"], sp["wc"], sp["d"], sp["wglu"])


def _ssm_sample(proj2d, h0, sp):
    rows = proj2d.shape[0]
    nseq = h0.shape[0]
    kern = functools.partial(_ssm_kernel, chained=False, tiles_per_seq=1, tb=rows)
    return pl.pallas_call(
        kern,
        grid=(1,),
        in_specs=[pl.BlockSpec((rows, SSM_WIDTH), lambda i: (0, COL_SSM)),
                  pl.BlockSpec((nseq, 2 * SSM_N), lambda i: (0, 0))]
                 + _ssm_common_specs(lambda i: (0, 0), lambda i: (0, 0, 0)),
        out_specs=[pl.BlockSpec((rows, SSM_WIDTH), lambda i: (0, 0)),
                   pl.BlockSpec((nseq, 2 * SSM_N), lambda i: (0, 0))],
        out_shape=[jax.ShapeDtypeStruct((rows, SSM_WIDTH), BF16),
                   jax.ShapeDtypeStruct((nseq, 2 * SSM_N), F32)],
        scratch_shapes=[pltpu.VMEM((rows, 2 * SSM_N), F32), pltpu.VMEM((SCAN_ROWS, 2 * SSM_N), F32)],
        compiler_params=_cparams(("arbitrary",)),
        name="ssm_sample",
    )(proj2d, h0, sp["wb"], sp["tab"], sp["wc"], sp["d"], sp["wglu"])


def _ssm_params(lam_re, lam_im, log_dt, b_re, b_im, c_re, c_im, d_skip, w_glu):
    lr, li = lam_re.astype(F32), lam_im.astype(F32)
    dt = jnp.exp(log_dt.astype(F32))[:, None]
    mag = jnp.exp(lr * dt)
    ab_re, ab_im = mag * jnp.cos(li * dt), mag * jnp.sin(li * dt)
    den = lr * lr + li * li
    nr = ab_re - 1.0
    f_re = (nr * lr + ab_im * li) / den
    f_im = (ab_im * lr - nr * li) / den
    br, bi = b_re.astype(F32), b_im.astype(F32)
    bb_re = f_re[..., None] * br - f_im[..., None] * bi
    bb_im = f_re[..., None] * bi + f_im[..., None] * br
    eye = jnp.eye(SSM_GROUPS, dtype=F32)

    def in_mat(bb):
        return jnp.einsum('gpc,gh->gchp', bb, eye).reshape(SSM_WIDTH, SSM_N)

    def out_mat(c):
        return jnp.einsum('gcp,gh->gphc', c.astype(F32), eye).reshape(SSM_N, SSM_WIDTH)

    wb = jnp.concatenate([in_mat(bb_re), in_mat(bb_im)], axis=1).astype(BF16)
    wc = jnp.concatenate([out_mat(c_re), -out_mat(c_im)], axis=0).astype(BF16)

    ar, ai = ab_re.reshape(1, SSM_N), ab_im.reshape(1, SSM_N)

    def cmul(x, y):
        return (x[0] * y[0] - x[1] * y[1], x[0] * y[1] + x[1] * y[0])

    pw = [(ar, ai)]
    for _ in range(SCAN_ROWS - 1):
        pw.append(cmul(pw[-1], (ar, ai)))
    row = jnp.arange(SCAN_ROWS)[:, None]
    tabs = []
    for shift in (1, 2, 4):
        for part in pw[shift - 1]:
            tabs.append(jnp.where(row >= shift, part, 0.0))
    tabs.append(jnp.concatenate([p[0] for p in pw], axis=0))
    tabs.append(jnp.concatenate([p[1] for p in pw], axis=0))
    tab = jnp.stack([jnp.broadcast_to(t, (SCAN_ROWS, SSM_N)) for t in tabs])
    return dict(wb=wb, tab=tab, wc=wc, d=d_skip.astype(F32).reshape(1, SSM_WIDTH), wglu=w_glu.astype(BF16))


_RET_G = 1.0 - np.exp2(-5.0 - np.arange(RET_HEADS, dtype=np.float64))


def _ret_consts(chunk, n_rows):
    idx = np.arange(n_rows)
    loc = idx % chunk
    same = (idx[:, None] // chunk) == (idx[None, :] // chunk)
    diff = loc[:, None] - loc[None, :]
    dec = np.where(same & (diff >= 0), _RET_G[:, None, None] ** np.maximum(diff, 0)[None], 0.0)
    qdec = np.repeat((_RET_G[None, :] ** (loc[:, None] + 1.0)), HEAD_DIM, axis=1)
    kdec = np.repeat((_RET_G[None, :] ** (chunk - 1.0 - loc[:, None])), HEAD_DIM, axis=1)
    return (jnp.asarray(dec, F32), jnp.asarray(qdec, F32), jnp.asarray(kdec, F32),
            jnp.asarray(_RET_G ** chunk, F32))


def _rope_tables(pos):
    half = HEAD_DIM // 2
    theta = 1.0 / (ROPE_BASE ** np.linspace(0.0, 1.0, half))
    ang = np.asarray(pos, np.float64)[:, None] * theta[None, :]
    cos = np.repeat(np.cos(ang), 2, axis=1)
    sin = np.repeat(np.sin(ang), 2, axis=1) * np.tile([-1.0, 1.0], half)[None]
    return (jnp.asarray(np.tile(cos, (1, RET_HEADS)), F32), jnp.asarray(np.tile(sin, (1, RET_HEADS)), F32))


def _rotate_pairs(x, cos, sin_signed):
    lane = lax.broadcasted_iota(jnp.int32, x.shape, 1)
    nxt = pltpu.roll(x, x.shape[1] - 1, 1)
    prv = pltpu.roll(x, 1, 1)
    return x * cos + jnp.where(lane % 2 == 0, nxt, prv) * sin_signed


def _ret_head_out(o, gate, norm):
    ms = jnp.mean(o * o, axis=-1, keepdims=True)
    return o * lax.rsqrt(ms + RMS_EPS) * norm * (gate * _sigmoid(gate))


def _ret_prompt_kernel(gc_ref, q_ref, k_ref, v_ref, g_ref, cos_ref, sin_ref, dec_ref, qdec_ref,
                       kdec_ref, norm_ref, y_ref, r_ref):
    @pl.when(pl.program_id(1) == 0)
    def _():
        r_ref[...] = jnp.zeros_like(r_ref)

    cos, sin = cos_ref[...], sin_ref[...]
    q = _rotate_pairs(q_ref[0], cos, sin)
    k = _rotate_pairs(k_ref[0], cos, sin) * (HEAD_DIM ** -0.5)
    v = v_ref[0]
    g = g_ref[0]
    kd = k * kdec_ref[...]
    qdec = qdec_ref[...]
    norm = norm_ref[...]
    for h in range(RET_HEADS):
        sl = slice(h * HEAD_DIM, (h + 1) * HEAD_DIM)
        qh, kh, vh = q[:, sl].astype(BF16), k[:, sl].astype(BF16), v[:, sl].astype(BF16)
        s = _dot_nt(qh, kh) * dec_ref[h]
        r = r_ref[0, h]
        o = _dot(s.astype(BF16), vh) + _dot(qh, r.astype(BF16)) * qdec[:, sl]
        y_ref[0, :, sl] = _ret_head_out(o, g[:, sl], norm[:, sl]).astype(y_ref.dtype)
        r_ref[0, h] = gc_ref[h] * r + _dot_tn(kd[:, sl].astype(BF16), vh)


def _ret_prompt(proj3d, cos, sin, norm):
    b, l, _ = proj3d.shape
    c = RET_CHUNK
    dec, qdec, kdec, gc = _ret_consts(c, c)
    blk = (1, c, MIX_BLOCK)
    tspec = pl.BlockSpec((c, MIX_BLOCK), lambda bi, i: (i, 0))
    cspec = pl.BlockSpec((c, MIX_BLOCK), lambda bi, i: (0, 0))
    return pl.pallas_call(
        _ret_prompt_kernel,
        grid=(b, l // c),
        in_specs=[pl.BlockSpec(memory_space=pltpu.SMEM),
                  pl.BlockSpec(blk, lambda bi, i: (bi, i, COL_RET_Q)),
                  pl.BlockSpec(blk, lambda bi, i: (bi, i, COL_RET_K)),
                  pl.BlockSpec(blk, lambda bi, i: (bi, i, COL_RET_V)),
                  pl.BlockSpec(blk, lambda bi, i: (bi, i, COL_RET_G)),
                  tspec, tspec,
                  pl.BlockSpec((RET_HEADS, c, c), lambda bi, i: (0, 0, 0)),
                  cspec, cspec,
                  pl.BlockSpec((1, MIX_BLOCK), lambda bi, i: (0, 0))],
        out_specs=[pl.BlockSpec(blk, lambda bi, i: (bi, i, 0)),
                   pl.BlockSpec((1, RET_HEADS, HEAD_DIM, HEAD_DIM), lambda bi, i: (bi, 0, 0, 0))],
        out_shape=[jax.ShapeDtypeStruct((b, l, MIX_BLOCK), BF16),
                   jax.ShapeDtypeStruct((b, RET_HEADS, HEAD_DIM, HEAD_DIM), F32)],
        compiler_params=_cparams(("parallel", "arbitrary")),
        name="ret_prompt",
    )(gc, proj3d, proj3d, proj3d, proj3d, cos, sin, dec, qdec, kdec, norm)


def _ret_sample_kernel(gc_ref, q_ref, k_ref, v_ref, g_ref, cos_ref, sin_ref, dec_ref, qdec_ref,
                       kdec_ref, norm_ref, r0_ref, y_ref, rn_ref, *, n_new, s_blk):
    cos, sin = cos_ref[...], sin_ref[...]
    q = _rotate_pairs(q_ref[...], cos, sin)
    k = _rotate_pairs(k_ref[...], cos, sin) * (HEAD_DIM ** -0.5)
    v = v_ref[...]
    g = g_ref[...]
    kd = k * kdec_ref[...]
    qdec = qdec_ref[...]
    norm = norm_ref[...]
    rows = s_blk * n_new
    seq = lax.broadcasted_iota(jnp.int32, (rows, HEAD_DIM), 0) // n_new
    for h in range(RET_HEADS):
        sl = slice(h * HEAD_DIM, (h + 1) * HEAD_DIM)
        qf, kdf = q[:, sl], kd[:, sl]
        qh, kh, vh = qf.astype(BF16), k[:, sl].astype(BF16), v[:, sl].astype(BF16)
        s = _dot_nt(qh, kh) * dec_ref[h]
        cross = jnp.zeros((rows, HEAD_DIM), F32)
        for si in range(s_blk):
            mine = seq == si
            r = r0_ref[si, h]
            cross = cross + _dot(jnp.where(mine, qf, 0.0).astype(BF16), r.astype(BF16))
            rn_ref[si, h] = gc_ref[h] * r + _dot_tn(jnp.where(mine, kdf, 0.0).astype(BF16), vh)
        o = _dot(s.astype(BF16), vh) + cross * qdec[:, sl]
        y_ref[:, sl] = _ret_head_out(o, g[:, sl], norm[:, sl]).astype(y_ref.dtype)


def _ret_sample(proj2d, r0, cos, sin, norm, n_new, s_blk):
    rows = s_blk * n_new
    nseq = r0.shape[0]
    dec, qdec, kdec, gc = _ret_consts(n_new, rows)
    blk = (rows, MIX_BLOCK)
    cspec = pl.BlockSpec(blk, lambda i: (0, 0))
    rblk = (s_blk, RET_HEADS, HEAD_DIM, HEAD_DIM)
    return pl.pallas_call(
        functools.partial(_ret_sample_kernel, n_new=n_new, s_blk=s_blk),
        grid=(nseq // s_blk,),
        in_specs=[pl.BlockSpec(memory_space=pltpu.SMEM),
                  pl.BlockSpec(blk, lambda i: (i, COL_RET_Q)),
                  pl.BlockSpec(blk, lambda i: (i, COL_RET_K)),
                  pl.BlockSpec(blk, lambda i: (i, COL_RET_V)),
                  pl.BlockSpec(blk, lambda i: (i, COL_RET_G)),
                  cspec, cspec,
                  pl.BlockSpec((RET_HEADS, rows, rows), lambda i: (0, 0, 0)),
                  cspec, cspec,
                  pl.BlockSpec((1, MIX_BLOCK), lambda i: (0, 0)),
                  pl.BlockSpec(rblk, lambda i: (i, 0, 0, 0))],
        out_specs=[pl.BlockSpec(blk, lambda i: (i, 0)),
                   pl.BlockSpec(rblk, lambda i: (i, 0, 0, 0))],
        out_shape=[jax.ShapeDtypeStruct((nseq * n_new, MIX_BLOCK), BF16),
                   jax.ShapeDtypeStruct((nseq, RET_HEADS, HEAD_DIM, HEAD_DIM), F32)],
        compiler_params=_cparams(("parallel",)),
        name="ret_sample",
    )(gc, proj2d, proj2d, proj2d, proj2d, cos, sin, dec, qdec, kdec, norm, r0)


def _block_diag(w):
    g, n, _ = w.shape
    return jnp.einsum('gcd,gh->gchd', w, jnp.eye(g, dtype=w.dtype)).reshape(g * n, g * n)


def _layer_params(l, p):
    return dict(
        norm_mix=p['norm_mix'][l].reshape(1, D_MODEL),
        norm_ffn=p['norm_ffn'][l].reshape(1, D_MODEL),
        w_in=p['w_in'][l].astype(BF16),
        w_out=p['w_out'][l].astype(BF16),
        pool_w=_block_diag(p['pool_w'][l].astype(F32)).astype(BF16),
        pool_scale=p['pool_scale'][l].astype(F32).reshape(1, POOL_WIDTH),
        qg=p['swa_q_norm'][l].astype(F32).reshape(1, HEAD_DIM),
        kg=p['swa_k_norm'][l].astype(F32).reshape(1, HEAD_DIM),
        sinks=p['swa_sinks'][l].astype(F32),
        ssm=_ssm_params(p['ssm_lambda_re'][l], p['ssm_lambda_im'][l], p['ssm_log_dt'][l],
                        p['ssm_b_re'][l], p['ssm_b_im'][l], p['ssm_c_re'][l], p['ssm_c_im'][l],
                        p['ssm_d'][l], p['ssm_w_glu'][l]),
        ret_norm=p['ret_norm'][l].astype(F32).reshape(1, MIX_BLOCK),
    )


def _channel_mix(xs_list, l, lp, p):
    i = l // 2
    if l % 2 == 0:
        wg, wu, wd = (p[k][i].astype(BF16) for k in ('ffn_w_gate', 'ffn_w_up', 'ffn_w_down'))
        return [_ffn(x, lp['norm_ffn'], wg, wu, wd, 1024, 256) for x in xs_list]
    wr = jnp.pad(p['moe_router'][i].astype(F32), ((0, 0), (0, 128 - N_EXPERTS)))
    wg, wu, wd = (p[k][i].astype(BF16) for k in ('moe_w_gate', 'moe_w_up', 'moe_w_down'))
    return _moe(xs_list, lp['norm_ffn'], wr, wg, wu, wd)


def _mix_prompt(x2, b, l, lp, bias, cos, sin):
    proj2 = _norm_matmul(x2, lp['norm_mix'], lp['w_in'], 512)
    proj3 = proj2.reshape(b, l, IN_WIDTH)
    tb = 512
    y_pool = _pool(proj2, COL_POOL, lp['pool_w'], lp['pool_scale'], n_rows=b * l, tb=tb,
                   tiles_per_seq=l // tb, pos0=0)
    y_swa, kn = _swa_prompt(proj3, lp['sinks'], lp['qg'], lp['kg'], bias)
    y_ssm, hn = _ssm_prompt(proj3, lp['ssm'], 512)
    y_ret, rn = _ret_prompt(proj3, cos, sin, lp['ret_norm'])
    ys = (y_pool, y_swa.reshape(b * l, MIX_BLOCK), y_ssm.reshape(b * l, MIX_BLOCK),
          y_ret.reshape(b * l, MIX_BLOCK))
    x2 = _out_proj(x2, ys, lp['w_out'], 512)
    kw = SWA_KV_HEADS * HEAD_DIM
    hn = hn[:, 0]
    states = (proj3[:, l - POOL_BUF:, :POOL_WIDTH],
              kn.reshape(b, SWA_WINDOW, SWA_KV_HEADS, HEAD_DIM),
              proj3[:, l - SWA_WINDOW:, COL_SWA_KV * MIX_BLOCK + kw:(COL_SWA_KV + 1) * MIX_BLOCK]
              .reshape(b, SWA_WINDOW, SWA_KV_HEADS, HEAD_DIM),
              jnp.stack([hn[:, :SSM_N], hn[:, SSM_N:]], axis=-1).reshape(b, SSM_GROUPS, SSM_STATE, 2),
              rn)
    return x2, states


SAMPLE_SEQ_BLOCK = 16


def _mix_sample(x2, nseq, n_new, start_pos, lp, st, bias, cos, sin):
    state_pool, cache_k, cache_v, state_ssm, state_ret = st
    rows = nseq * n_new
    wb = cache_k.shape[1]
    kw = SWA_KV_HEADS * HEAD_DIM
    ext_rows = POOL_HALO + n_new
    proj2 = _norm_matmul(x2, lp['norm_mix'], lp['w_in'], 512)
    proj3 = proj2.reshape(nseq, n_new, IN_WIDTH)
    u_pool = proj3[:, :, :POOL_WIDTH]
    buf = state_pool.astype(F32)
    ext = jnp.concatenate([jnp.zeros((nseq, POOL_HALO - POOL_BUF, POOL_WIDTH), F32), buf, u_pool], axis=1)
    y_pool = _pool(ext.reshape(nseq * ext_rows, POOL_WIDTH), 0, lp['pool_w'], lp['pool_scale'],
                   n_rows=nseq * ext_rows, tb=nseq * ext_rows, tiles_per_seq=1, pos0=start_pos)
    y_pool = y_pool.reshape(nseq, ext_rows, POOL_WIDTH)[:, POOL_HALO:].reshape(rows, POOL_WIDTH)
    y_swa, nk, nv = _swa_sample(proj3, cache_k.reshape(nseq, wb, kw).astype(F32),
                                cache_v.reshape(nseq, wb, kw).astype(F32),
                                lp['sinks'], lp['qg'], lp['kg'], bias, SAMPLE_SEQ_BLOCK)
    h0 = state_ssm.astype(F32).reshape(nseq, SSM_N, 2)
    h0 = jnp.concatenate([h0[..., 0], h0[..., 1]], axis=1)
    y_ssm, hn = _ssm_sample(proj2, h0, lp['ssm'])
    y_ret, rn = _ret_sample(proj2, state_ret.astype(F32), cos, sin, lp['ret_norm'], n_new, SAMPLE_SEQ_BLOCK)
    ys = (y_pool, y_swa.reshape(rows, MIX_BLOCK), y_ssm, y_ret)
    x2 = _out_proj(x2, ys, lp['w_out'], 512)
    states = (jnp.concatenate([buf, u_pool], axis=1)[:, -POOL_BUF:],
              nk.reshape(nseq, SWA_WINDOW, SWA_KV_HEADS, HEAD_DIM),
              nv.reshape(nseq, SWA_WINDOW, SWA_KV_HEADS, HEAD_DIM),
              jnp.stack([hn[:, :SSM_N], hn[:, SSM_N:]], axis=-1).reshape(nseq, SSM_GROUPS, SSM_STATE, 2),
              rn)
    return x2, states


def _forward(x_prompt, x_sample, past_len, sample_state, p, rel_bias):
    b, l, d = x_prompt.shape
    nseq, n_new, _ = x_sample.shape
    wb = sample_state[1].shape[2]
    depth = p['norm_mix'].shape[0]
    bias_p = _swa_bias(rel_bias, np.arange(SWA_BLOCK)[:, None] - np.arange(2 * SWA_BLOCK)[None, :] + SWA_BLOCK)
    bias_s = _swa_bias(rel_bias, np.arange(n_new)[:, None] - np.arange(wb + n_new)[None, :] + wb)
    rope_p = _rope_tables(np.arange(l))
    rope_s = _rope_tables(past_len + (np.arange(SAMPLE_SEQ_BLOCK * n_new) % n_new))
    xp = x_prompt.reshape(b * l, d)
    xs = x_sample.reshape(nseq * n_new, d)
    st_p, st_s = [], []
    for li in range(depth):
        lp = _layer_params(li, p)
        xp, sp = _mix_prompt(xp, b, l, lp, bias_p, *rope_p)
        xs, ss = _mix_sample(xs, nseq, n_new, past_len, lp, [s[li] for s in sample_state], bias_s, *rope_s)
        xp, xs = _channel_mix([xp, xs], li, lp, p)
        st_p.append(sp)
        st_s.append(ss)
    outs = [xp.reshape(b, l, d), xs.reshape(nseq, n_new, d)]
    for k in range(5):
        outs.append(jnp.stack([s[k] for s in st_p]))
        outs.append(jnp.stack([s[k] for s in st_s]))
    return tuple(outs)


PAST_LEN = 16384


def kernel(x_prompt, x_sample, state_pool, cache_swa_k, cache_swa_v, state_ssm, state_ret,
           norm_mix, norm_ffn, w_in, w_out, pool_w, pool_scale, swa_q_norm, swa_k_norm, swa_sinks,
           rel_bias, ssm_lambda_re, ssm_lambda_im, ssm_log_dt, ssm_b_re, ssm_b_im, ssm_c_re, ssm_c_im,
           ssm_d, ssm_w_glu, ret_norm, ffn_w_gate, ffn_w_up, ffn_w_down, moe_router, moe_w_gate,
           moe_w_up, moe_w_down):
    p = dict(norm_mix=norm_mix, norm_ffn=norm_ffn, w_in=w_in, w_out=w_out, pool_w=pool_w,
             pool_scale=pool_scale, swa_q_norm=swa_q_norm, swa_k_norm=swa_k_norm, swa_sinks=swa_sinks,
             ssm_lambda_re=ssm_lambda_re, ssm_lambda_im=ssm_lambda_im, ssm_log_dt=ssm_log_dt,
             ssm_b_re=ssm_b_re, ssm_b_im=ssm_b_im, ssm_c_re=ssm_c_re, ssm_c_im=ssm_c_im,
             ssm_d=ssm_d, ssm_w_glu=ssm_w_glu, ret_norm=ret_norm,
             ffn_w_gate=ffn_w_gate, ffn_w_up=ffn_w_up, ffn_w_down=ffn_w_down, moe_router=moe_router,
             moe_w_gate=moe_w_gate, moe_w_up=moe_w_up, moe_w_down=moe_w_down)
    return _forward(x_prompt, x_sample, PAST_LEN,
                    (state_pool, cache_swa_k, cache_swa_v, state_ssm, state_ret), p, rel_bias)
```

```python
import functools
import math

import numpy as np
import jax
import jax.numpy as jnp
from jax import lax
from jax.experimental import pallas as pl
from jax.experimental.pallas import tpu as pltpu

F32 = jnp.float32
BF16 = jnp.bfloat16

D_MODEL = 1024
HEAD_DIM = 64
POOL_WIDTH = 256
POOL_WINDOWS = (2, 4, 8, 16)
POOL_BUF = 15
POOL_HALO = 16
SWA_HEADS = 4
SWA_KV_HEADS = 2
SWA_WINDOW = 128
SWA_BLOCK = 128
SSM_WIDTH = 256
SSM_CH = 16
SSM_GROUPS = 16
SSM_STATE = 64
SSM_N = SSM_GROUPS * SSM_STATE
RET_HEADS = 4
RET_CHUNK = 128
ROPE_BASE = 10000.0
IN_WIDTH = 2048
MIX_BLOCK = 256
D_FF = 2816
N_EXPERTS = 8
T5_BUCKETS = 32
T5_MAX_DIST = 128
RMS_EPS = 1e-6
NEG = -1e30
SCAN_ROWS = 8

COL_POOL, COL_SWA_Q, COL_SWA_KV, COL_SSM, COL_RET_Q, COL_RET_K, COL_RET_V, COL_RET_G = range(8)

VMEM_LIMIT = 48 * 1024 * 1024


def _cparams(sem):
    return pltpu.CompilerParams(dimension_semantics=sem, vmem_limit_bytes=VMEM_LIMIT)


def _rms(x, g):
    ms = jnp.mean(x * x, axis=-1, keepdims=True)
    return x * lax.rsqrt(ms + RMS_EPS) * g


def _dot(a, b):
    return jnp.dot(a, b, preferred_element_type=F32)


def _dot_nt(a, b):
    return lax.dot_general(a, b, (((1,), (1,)), ((), ())), preferred_element_type=F32)


def _dot_tn(a, b):
    return lax.dot_general(a, b, (((0,), (0,)), ((), ())), preferred_element_type=F32)


def _sigmoid(x):
    return 1.0 / (1.0 + jnp.exp(-x))


def _norm_matmul_kernel(x_ref, g_ref, w_ref, o_ref):
    h = _rms(x_ref[...], g_ref[...]).astype(BF16)
    o_ref[...] = _dot(h, w_ref[...])


def _norm_matmul(x, g, w, tm):
    t, d = x.shape
    tm = min(tm, t)
    n = w.shape[1]
    return pl.pallas_call(
        _norm_matmul_kernel,
        grid=(t // tm,),
        in_specs=[pl.BlockSpec((tm, d), lambda i: (i, 0)),
                  pl.BlockSpec((1, d), lambda i: (0, 0)),
                  pl.BlockSpec((d, n), lambda i: (0, 0))],
        out_specs=pl.BlockSpec((tm, n), lambda i: (i, 0)),
        out_shape=jax.ShapeDtypeStruct((t, n), F32),
        compiler_params=_cparams(("parallel",)),
        name="norm_matmul",
    )(x, g, w)


def _out_proj_kernel(x_ref, y0_ref, y1_ref, y2_ref, y3_ref, w_ref, o_ref):
    acc = x_ref[...]
    for k, y_ref in enumerate((y0_ref, y1_ref, y2_ref, y3_ref)):
        acc = acc + _dot(y_ref[...], w_ref[k * MIX_BLOCK:(k + 1) * MIX_BLOCK, :])
    o_ref[...] = acc


def _out_proj(x, ys, w, tm):
    t, d = x.shape
    tm = min(tm, t)
    yspec = pl.BlockSpec((tm, MIX_BLOCK), lambda i: (i, 0))
    return pl.pallas_call(
        _out_proj_kernel,
        grid=(t // tm,),
        in_specs=[pl.BlockSpec((tm, d), lambda i: (i, 0)), yspec, yspec, yspec, yspec,
                  pl.BlockSpec((d, d), lambda i: (0, 0))],
        out_specs=pl.BlockSpec((tm, d), lambda i: (i, 0)),
        out_shape=jax.ShapeDtypeStruct((t, d), F32),
        compiler_params=_cparams(("parallel",)),
        name="out_proj",
    )(x, *ys, w)


def _swiglu_chunk(h, wg, wu, wd):
    a = _dot(h, wg)
    b = _dot(h, wu)
    return _dot((a * _sigmoid(a) * b).astype(BF16), wd)


def _ffn_kernel(x_ref, g_ref, wg_ref, wu_ref, wd_ref, o_ref, h_scr):
    @pl.when(pl.program_id(1) == 0)
    def _():
        x = x_ref[...]
        h_scr[...] = _rms(x, g_ref[...]).astype(BF16)
        o_ref[...] = x

    o_ref[...] += _swiglu_chunk(h_scr[...], wg_ref[...], wu_ref[...], wd_ref[...])


def _ffn(x, g, wg, wu, wd, tm, tf):
    t, d = x.shape
    tm = min(tm, t)
    f = wg.shape[1]
    return pl.pallas_call(
        _ffn_kernel,
        grid=(t // tm, f // tf),
        in_specs=[pl.BlockSpec((tm, d), lambda i, j: (i, 0)),
                  pl.BlockSpec((1, d), lambda i, j: (0, 0)),
                  pl.BlockSpec((d, tf), lambda i, j: (0, j)),
                  pl.BlockSpec((d, tf), lambda i, j: (0, j)),
                  pl.BlockSpec((tf, d), lambda i, j: (j, 0))],
        out_specs=pl.BlockSpec((tm, d), lambda i, j: (i, 0)),
        out_shape=jax.ShapeDtypeStruct((t, d), F32),
        scratch_shapes=[pltpu.VMEM((tm, d), BF16)],
        compiler_params=_cparams(("parallel", "arbitrary")),
        name="ffn",
    )(x, g, wg, wu, wd)


ROUTE_ID_LANES = (0, 1)
ROUTE_GATE_LANES = (2, 3)


def _router_kernel(x_ref, g_ref, wr_ref, c_ref):
    h = _rms(x_ref[...], g_ref[...])
    logits = jnp.dot(h, wr_ref[...], preferred_element_type=F32, precision=lax.Precision.HIGHEST)
    lane = lax.broadcasted_iota(jnp.int32, logits.shape, 1).astype(F32)
    lg = jnp.where(lane < N_EXPERTS, logits, NEG)
    m1 = jnp.max(lg, axis=-1, keepdims=True)
    i1 = jnp.min(jnp.where(lg == m1, lane, 128.0), axis=-1, keepdims=True)
    lg2 = jnp.where(lane == i1, NEG, lg)
    m2 = jnp.max(lg2, axis=-1, keepdims=True)
    i2 = jnp.min(jnp.where(lg2 == m2, lane, 128.0), axis=-1, keepdims=True)
    ex = jnp.exp(m2 - m1)
    vals = (i1, i2, 1.0 / (1.0 + ex), ex / (1.0 + ex))
    out = jnp.zeros_like(logits)
    for ln, v in zip(ROUTE_ID_LANES + ROUTE_GATE_LANES, vals):
        out = jnp.where(lane == ln, v, out)
    c_ref[...] = out


def _router(x, g, wr, tm):
    t, d = x.shape
    tm = min(tm, t)
    return pl.pallas_call(
        _router_kernel,
        grid=(t // tm,),
        in_specs=[pl.BlockSpec((tm, d), lambda i: (i, 0)),
                  pl.BlockSpec((1, d), lambda i: (0, 0)),
                  pl.BlockSpec((d, 128), lambda i: (0, 0))],
        out_specs=pl.BlockSpec((tm, 128), lambda i: (i, 0)),
        out_shape=jax.ShapeDtypeStruct((t, 128), F32),
        compiler_params=_cparams(("parallel",)),
        name="router",
    )(x, g, wr)


def _row_copy(src, i, dst, j, sem):
    return pltpu.make_async_copy(src.at[pl.ds(i, 1)], dst.at[pl.ds(j, 1)], sem)


def _dispatch_kernel(pos_ref, x_ref, xs_in, xs_hbm, sem, *, td):
    del xs_in

    def issue(j, c):
        for k in range(2):
            _row_copy(x_ref, j, xs_hbm, pos_ref[0, 0, 2 * j + k], sem).start()
        return c

    lax.fori_loop(0, td, issue, 0)
    for _ in range(2):
        pltpu.make_async_copy(x_ref, xs_hbm.at[pl.ds(0, td)], sem).wait()


def _dispatch(x, pos, xs, td):
    t, d = x.shape
    td = min(td, t)
    pos3 = pos.reshape(t // td, 1, 2 * td)
    return pl.pallas_call(
        functools.partial(_dispatch_kernel, td=td),
        grid=(t // td,),
        in_specs=[pl.BlockSpec((1, 1, 2 * td), lambda i: (i, 0, 0), memory_space=pltpu.SMEM),
                  pl.BlockSpec((td, d), lambda i: (i, 0)),
                  pl.BlockSpec(memory_space=pl.ANY)],
        out_specs=pl.BlockSpec(memory_space=pl.ANY),
        out_shape=jax.ShapeDtypeStruct(xs.shape, xs.dtype),
        scratch_shapes=[pltpu.SemaphoreType.DMA],
        input_output_aliases={2: 0},
        compiler_params=_cparams(("arbitrary",)),
        name="moe_dispatch",
    )(pos3, x, xs)


def _grouped_ffn_kernel(te_ref, nu_ref, x_ref, g_ref, wg_ref, wu_ref, wd_ref, o_ref, h_scr):
    del te_ref
    j = pl.program_id(1)
    used = pl.program_id(0) < nu_ref[0]

    @pl.when(jnp.logical_not(used) & (j == 0))
    def _():
        o_ref[...] = jnp.zeros_like(o_ref)

    @pl.when(used)
    def _():
        @pl.when(j == 0)
        def _():
            h_scr[...] = _rms(x_ref[...], g_ref[...]).astype(BF16)

        y = _swiglu_chunk(h_scr[...], wg_ref[0], wu_ref[0], wd_ref[0])

        @pl.when(j == 0)
        def _():
            o_ref[...] = y

        @pl.when(j > 0)
        def _():
            o_ref[...] += y


def _grouped_ffn(xs, g, tile_expert, n_used, wg, wu, wd, tm, tf):
    r, d = xs.shape
    f = wg.shape[2]
    nj = f // tf

    def row_map(i, j, te, nu):
        return (i, 0)

    def col_of(i, j, nu):
        return jnp.where(i < nu[0], j, nj - 1)

    grid_spec = pltpu.PrefetchScalarGridSpec(
        num_scalar_prefetch=2,
        grid=(r // tm, nj),
        in_specs=[pl.BlockSpec((tm, d), row_map),
                  pl.BlockSpec((1, d), lambda i, j, te, nu: (0, 0)),
                  pl.BlockSpec((1, d, tf), lambda i, j, te, nu: (te[i], 0, col_of(i, j, nu))),
                  pl.BlockSpec((1, d, tf), lambda i, j, te, nu: (te[i], 0, col_of(i, j, nu))),
                  pl.BlockSpec((1, tf, d), lambda i, j, te, nu: (te[i], col_of(i, j, nu), 0))],
        out_specs=pl.BlockSpec((tm, d), row_map),
        scratch_shapes=[pltpu.VMEM((tm, d), BF16)],
    )
    return pl.pallas_call(
        _grouped_ffn_kernel,
        grid_spec=grid_spec,
        out_shape=jax.ShapeDtypeStruct((r, d), F32),
        compiler_params=_cparams(("arbitrary", "arbitrary")),
        name="moe_grouped_ffn",
    )(tile_expert, n_used, xs, g, wg, wu, wd)


def _combine_kernel(pos_ref, x_ref, route_ref, ys_hbm, o_ref, buf0, buf1, sem, *, tc):
    def issue(j, c):
        _row_copy(ys_hbm, pos_ref[0, 0, 2 * j], buf0, j, sem).start()
        _row_copy(ys_hbm, pos_ref[0, 0, 2 * j + 1], buf1, j, sem).start()
        return c

    lax.fori_loop(0, tc, issue, 0)
    for buf in (buf0, buf1):
        pltpu.make_async_copy(ys_hbm.at[pl.ds(0, tc)], buf, sem).wait()
    route = route_ref[...]
    g0 = route[:, ROUTE_GATE_LANES[0]:ROUTE_GATE_LANES[0] + 1]
    g1 = route[:, ROUTE_GATE_LANES[1]:ROUTE_GATE_LANES[1] + 1]
    o_ref[...] = x_ref[...] + g0 * buf0[...] + g1 * buf1[...]


def _combine(x, route, pos, ys, tc):
    t, d = x.shape
    tc = min(tc, t)
    pos3 = pos.reshape(t // tc, 1, 2 * tc)
    return pl.pallas_call(
        functools.partial(_combine_kernel, tc=tc),
        grid=(t // tc,),
        in_specs=[pl.BlockSpec((1, 1, 2 * tc), lambda i: (i, 0, 0), memory_space=pltpu.SMEM),
                  pl.BlockSpec((tc, d), lambda i: (i, 0)),
                  pl.BlockSpec((tc, 128), lambda i: (i, 0)),
                  pl.BlockSpec(memory_space=pl.ANY)],
        out_specs=pl.BlockSpec((tc, d), lambda i: (i, 0)),
        out_shape=jax.ShapeDtypeStruct((t, d), F32),
        scratch_shapes=[pltpu.VMEM((tc, d), F32), pltpu.VMEM((tc, d), F32), pltpu.SemaphoreType.DMA],
        compiler_params=_cparams(("arbitrary",)),
        name="moe_combine",
    )(pos3, x, route, ys)


MOE_TM = 512


def _route_plan(expert_ids, tm):
    flat = expert_ids.reshape(-1)
    a = flat.shape[0]
    onehot = (flat[None, :] == jnp.arange(N_EXPERTS, dtype=jnp.int32)[:, None]).astype(jnp.int32)
    csum = jnp.cumsum(onehot, axis=1)
    counts = csum[:, -1]
    padded = (counts + tm - 1) // tm * tm
    ends = jnp.cumsum(padded)
    offs = ends - padded
    pos = jnp.sum(onehot * (offs[:, None] + csum - 1), axis=0)
    n_tiles = (a + N_EXPERTS * tm) // tm
    tile_start = jnp.arange(n_tiles, dtype=jnp.int32) * tm
    tile_expert = jnp.minimum(jnp.sum(tile_start[:, None] >= ends[None, :], axis=1), N_EXPERTS - 1)
    n_used = (ends[-1] // tm).reshape(1)
    last = jnp.take(tile_expert, n_used[0] - 1)
    tile_expert = jnp.where(tile_start < ends[-1], tile_expert, last)
    return pos.astype(jnp.int32), tile_expert.astype(jnp.int32), n_used.astype(jnp.int32), n_tiles * tm


def _moe(xs_list, g, wr, wg, wu, wd):
    routes = [_router(x, g, wr, 512) for x in xs_list]
    ids = jnp.concatenate([r[:, ROUTE_ID_LANES[0]:ROUTE_ID_LANES[1] + 1] for r in routes]).astype(jnp.int32)
    pos, tile_expert, n_used, n_rows = _route_plan(ids, MOE_TM)
    xs = jnp.zeros((n_rows, D_MODEL), F32)
    bounds = np.cumsum([0] + [2 * x.shape[0] for x in xs_list])
    pos_list = [pos[lo:hi] for lo, hi in zip(bounds[:-1], bounds[1:])]
    for x, ps in zip(xs_list, pos_list):
        xs = _dispatch(x, ps, xs, 1024)
    ys = _grouped_ffn(xs, g, tile_expert, n_used, wg, wu, wd, MOE_TM, D_FF // 2)
    return [_combine(x, r, ps, ys, 512) for x, r, ps in zip(xs_list, routes, pos_list)]


def _pool_kernel(u_ref, halo_ref, w_ref, scale_ref, o_ref, *, tiles_per_seq, pos0, tb):
    ti = pl.program_id(0) % tiles_per_seq
    u = u_ref[...]
    halo = jnp.where(ti == 0, 0.0, halo_ref[...])
    ext = jnp.concatenate([halo, u], axis=0)
    s2 = ext + pltpu.roll(ext, 1, 0)
    s4 = s2 + pltpu.roll(s2, 2, 0)
    s8 = s4 + pltpu.roll(s4, 4, 0)
    s16 = s8 + pltpu.roll(s8, 8, 0)
    grp = lax.broadcasted_iota(jnp.int32, (tb, POOL_WIDTH), 1) // (POOL_WIDTH // 4)
    row = lax.broadcasted_iota(jnp.int32, (tb, POOL_WIDTH), 0)
    s = jnp.where(grp == 0, s2[POOL_HALO:],
                  jnp.where(grp == 1, s4[POOL_HALO:],
                            jnp.where(grp == 2, s8[POOL_HALO:], s16[POOL_HALO:])))
    win = jnp.where(grp == 0, 2, jnp.where(grp == 1, 4, jnp.where(grp == 2, 8, 16)))
    cnt = jnp.minimum(win, pos0 + ti * tb + row + 1).astype(F32)
    pooled = s / cnt - u
    o_ref[...] = (_dot(pooled.astype(BF16), w_ref[...]) * scale_ref[...]).astype(o_ref.dtype)


def _pool(proj2d, col, w, scale, *, n_rows, tb, tiles_per_seq, pos0):
    per = tb // POOL_HALO
    return pl.pallas_call(
        functools.partial(_pool_kernel, tiles_per_seq=tiles_per_seq, pos0=pos0, tb=tb),
        grid=(n_rows // tb,),
        in_specs=[pl.BlockSpec((tb, POOL_WIDTH), lambda i: (i, col)),
                  pl.BlockSpec((POOL_HALO, POOL_WIDTH), lambda i: (jnp.maximum(i * per - 1, 0), col)),
                  pl.BlockSpec((POOL_WIDTH, POOL_WIDTH), lambda i: (0, 0)),
                  pl.BlockSpec((1, POOL_WIDTH), lambda i: (0, 0))],
        out_specs=pl.BlockSpec((tb, POOL_WIDTH), lambda i: (i, 0)),
        out_shape=jax.ShapeDtypeStruct((n_rows, POOL_WIDTH), BF16),
        compiler_params=_cparams(("parallel",)),
        name="pool",
    )(proj2d, proj2d, w, scale)


def _t5_bucket_np(rel):
    n = np.maximum(rel, 0)
    max_exact = T5_BUCKETS // 2
    nf = np.maximum(n, max_exact).astype(np.float32)
    large = max_exact + (np.log(nf / max_exact) / math.log(T5_MAX_DIST / max_exact)
                         * (T5_BUCKETS - max_exact)).astype(np.int32)
    large = np.minimum(large, T5_BUCKETS - 1)
    return np.where(n < max_exact, n, large)


def _swa_bias(rel_bias, rel):
    valid = (rel >= 0) & (rel < SWA_WINDOW)
    onehot = jnp.asarray(_t5_bucket_np(rel)[..., None] == np.arange(T5_BUCKETS), F32)
    b = jnp.einsum('qsb,bh->hqs', onehot, rel_bias.astype(F32), precision=lax.Precision.HIGHEST)
    return jnp.where(valid[None], b, NEG)


def _softmax_parts(parts, sink):
    m = sink
    for s in parts:
        m = jnp.maximum(m, jnp.max(s, axis=-1, keepdims=True))
    ps = [jnp.exp(s - m) for s in parts]
    denom = jnp.exp(sink - m)
    for p in ps:
        denom = denom + jnp.sum(p, axis=-1, keepdims=True)
    return ps, denom


def _swa_prompt_kernel(sinks_ref, q_ref, kv_ref, kvp_ref, qg_ref, kg_ref, bias_ref, y_ref, kn_ref):
    has_prev = pl.program_id(1) > 0
    q = q_ref[0]
    kv = kv_ref[0]
    kvp = kvp_ref[0]
    qg = qg_ref[...]
    kg = kg_ref[...]
    kw = SWA_KV_HEADS * HEAD_DIM
    for kh in range(SWA_KV_HEADS):
        ksl = slice(kh * HEAD_DIM, (kh + 1) * HEAD_DIM)
        vsl = slice(kw + kh * HEAD_DIM, kw + (kh + 1) * HEAD_DIM)
        kc = _rms(kv[:, ksl], kg)
        kp = _rms(kvp[:, ksl], kg)
        kn_ref[0, :, ksl] = kc
        kcb, kpb = kc.astype(BF16), kp.astype(BF16)
        vc, vp = kv[:, vsl].astype(BF16), kvp[:, vsl].astype(BF16)
        for gq in range(SWA_HEADS // SWA_KV_HEADS):
            h = kh * (SWA_HEADS // SWA_KV_HEADS) + gq
            hsl = slice(h * HEAD_DIM, (h + 1) * HEAD_DIM)
            qn = _rms(q[:, hsl], qg).astype(BF16)
            s_c = _dot_nt(qn, kcb) * (HEAD_DIM ** -0.5) + bias_ref[h, :, SWA_BLOCK:]
            s_p = _dot_nt(qn, kpb) * (HEAD_DIM ** -0.5) + bias_ref[h, :, :SWA_BLOCK]
            s_p = jnp.where(has_prev, s_p, NEG)
            (p_c, p_p), denom = _softmax_parts((s_c, s_p), sinks_ref[h])
            o = _dot(p_c.astype(BF16), vc) + _dot(p_p.astype(BF16), vp)
            y_ref[0, :, hsl] = (o / denom).astype(y_ref.dtype)


def _swa_prompt(proj3d, sinks, qg, kg, bias):
    b, l, _ = proj3d.shape
    nb = l // SWA_BLOCK
    blk = (1, SWA_BLOCK, MIX_BLOCK)
    return pl.pallas_call(
        _swa_prompt_kernel,
        grid=(b, nb),
        in_specs=[pl.BlockSpec(memory_space=pltpu.SMEM),
                  pl.BlockSpec(blk, lambda bi, i: (bi, i, COL_SWA_Q)),
                  pl.BlockSpec(blk, lambda bi, i: (bi, i, COL_SWA_KV)),
                  pl.BlockSpec(blk, lambda bi, i: (bi, jnp.maximum(i - 1, 0), COL_SWA_KV)),
                  pl.BlockSpec((1, HEAD_DIM), lambda bi, i: (0, 0)),
                  pl.BlockSpec((1, HEAD_DIM), lambda bi, i: (0, 0)),
                  pl.BlockSpec((SWA_HEADS, SWA_BLOCK, 2 * SWA_BLOCK), lambda bi, i: (0, 0, 0))],
        out_specs=[pl.BlockSpec(blk, lambda bi, i: (bi, i, 0)),
                   pl.BlockSpec((1, SWA_BLOCK, SWA_KV_HEADS * HEAD_DIM), lambda bi, i: (bi, 0, 0))],
        out_shape=[jax.ShapeDtypeStruct((b, l, MIX_BLOCK), BF16),
                   jax.ShapeDtypeStruct((b, SWA_BLOCK, SWA_KV_HEADS * HEAD_DIM), F32)],
        compiler_params=_cparams(("parallel", "arbitrary")),
        name="swa_prompt",
    )(sinks, proj3d, proj3d, proj3d, qg, kg, bias)


def _swa_sample_kernel(sinks_ref, q_ref, kv_ref, ck_ref, cv_ref, qg_ref, kg_ref, bias_ref,
                       y_ref, nk_ref, nv_ref, *, n_new):
    q = q_ref[...]
    kv = kv_ref[...]
    ck = ck_ref[...]
    cv = cv_ref[...]
    qg = qg_ref[...]
    kg = kg_ref[...]
    kw = SWA_KV_HEADS * HEAD_DIM
    bdot = functools.partial(jnp.einsum, preferred_element_type=F32)
    for kh in range(SWA_KV_HEADS):
        ksl = slice(kh * HEAD_DIM, (kh + 1) * HEAD_DIM)
        vsl = slice(kw + kh * HEAD_DIM, kw + (kh + 1) * HEAD_DIM)
        kn = _rms(kv[:, :, ksl], kg)
        vn = kv[:, :, vsl]
        nk_ref[:, :SWA_WINDOW - n_new, ksl] = ck[:, n_new:, ksl]
        nk_ref[:, SWA_WINDOW - n_new:, ksl] = kn
        nv_ref[:, :SWA_WINDOW - n_new, ksl] = cv[:, n_new:, ksl]
        nv_ref[:, SWA_WINDOW - n_new:, ksl] = vn
        for gq in range(SWA_HEADS // SWA_KV_HEADS):
            h = kh * (SWA_HEADS // SWA_KV_HEADS) + gq
            hsl = slice(h * HEAD_DIM, (h + 1) * HEAD_DIM)
            qn = _rms(q[:, :, hsl], qg)
            s_c = bdot('sqd,skd->sqk', qn, ck[:, :, ksl]) * (HEAD_DIM ** -0.5) + bias_ref[h, :, :SWA_WINDOW]
            s_n = bdot('sqd,skd->sqk', qn, kn) * (HEAD_DIM ** -0.5) + bias_ref[h, :, SWA_WINDOW:]
            (p_c, p_n), denom = _softmax_parts((s_c, s_n), sinks_ref[h])
            o = bdot('sqk,skd->sqd', p_c, cv[:, :, ksl]) + bdot('sqk,skd->sqd', p_n, vn)
            y_ref[:, :, hsl] = (o / denom).astype(y_ref.dtype)


def _swa_sample(proj3d, cache_k, cache_v, sinks, qg, kg, bias, s_blk):
    nseq, n_new, _ = proj3d.shape
    kw = SWA_KV_HEADS * HEAD_DIM
    blk = (s_blk, n_new, MIX_BLOCK)
    cblk = (s_blk, SWA_WINDOW, kw)
    return pl.pallas_call(
        functools.partial(_swa_sample_kernel, n_new=n_new),
        grid=(nseq // s_blk,),
        in_specs=[pl.BlockSpec(memory_space=pltpu.SMEM),
                  pl.BlockSpec(blk, lambda i: (i, 0, COL_SWA_Q)),
                  pl.BlockSpec(blk, lambda i: (i, 0, COL_SWA_KV)),
                  pl.BlockSpec(cblk, lambda i: (i, 0, 0)),
                  pl.BlockSpec(cblk, lambda i: (i, 0, 0)),
                  pl.BlockSpec((1, HEAD_DIM), lambda i: (0, 0)),
                  pl.BlockSpec((1, HEAD_DIM), lambda i: (0, 0)),
                  pl.BlockSpec((SWA_HEADS, n_new, SWA_WINDOW + n_new), lambda i: (0, 0, 0))],
        out_specs=[pl.BlockSpec(blk, lambda i: (i, 0, 0)),
                   pl.BlockSpec(cblk, lambda i: (i, 0, 0)),
                   pl.BlockSpec(cblk, lambda i: (i, 0, 0))],
        out_shape=[jax.ShapeDtypeStruct((nseq, n_new, MIX_BLOCK), BF16),
                   jax.ShapeDtypeStruct((nseq, SWA_WINDOW, kw), F32),
                   jax.ShapeDtypeStruct((nseq, SWA_WINDOW, kw), F32)],
        compiler_params=_cparams(("parallel",)),
        name="swa_sample",
    )(sinks, proj3d, proj3d, cache_k, cache_v, qg, kg, bias)


def _ssm_kernel(u_ref, h0_ref, wb_ref, tab_ref, wc_ref, d_ref, wglu_ref, y_ref, hn_ref,
                bu_scr, carry_scr, *, chained, tiles_per_seq, tb):
    n = SSM_N
    u = u_ref[...]
    bu_scr[...] = _dot(u.astype(BF16), wb_ref[...])

    if chained:
        @pl.when(pl.program_id(1) % tiles_per_seq == 0)
        def _():
            carry_scr[...] = jnp.zeros_like(carry_scr)

    def tile_scan(r0, cr, ci):
        hr = bu_scr[pl.ds(r0, SCAN_ROWS), :n]
        hi = bu_scr[pl.ds(r0, SCAN_ROWS), n:]
        for k, shift in enumerate((1, 2, 4)):
            ar, ai = tab_ref[2 * k], tab_ref[2 * k + 1]
            sr, si = pltpu.roll(hr, shift, 0), pltpu.roll(hi, shift, 0)
            hr, hi = hr + ar * sr - ai * si, hi + ar * si + ai * sr
        pr, pi = tab_ref[6], tab_ref[7]
        hr, hi = hr + pr * cr - pi * ci, hi + pr * ci + pi * cr
        bu_scr[pl.ds(r0, SCAN_ROWS), :n] = hr
        bu_scr[pl.ds(r0, SCAN_ROWS), n:] = hi
        return hr[SCAN_ROWS - 1:], hi[SCAN_ROWS - 1:]

    if chained:
        def body(t, carry):
            r0 = pl.multiple_of(t * SCAN_ROWS, SCAN_ROWS)
            lr, li = tile_scan(r0, *carry)
            return (jnp.broadcast_to(lr, (SCAN_ROWS, n)), jnp.broadcast_to(li, (SCAN_ROWS, n)))

        cr, ci = lax.fori_loop(0, tb // SCAN_ROWS, body, (carry_scr[:, :n], carry_scr[:, n:]))
        carry_scr[:, :n] = cr
        carry_scr[:, n:] = ci
        hn_ref[0, :, :n] = cr
        hn_ref[0, :, n:] = ci
    else:
        def body(t, _):
            r0 = pl.multiple_of(t * SCAN_ROWS, SCAN_ROWS)
            h0 = h0_ref[pl.ds(t, 1), :]
            cr = jnp.broadcast_to(h0[:, :n], (SCAN_ROWS, n))
            ci = jnp.broadcast_to(h0[:, n:], (SCAN_ROWS, n))
            lr, li = tile_scan(r0, cr, ci)
            hn_ref[pl.ds(t, 1), :n] = lr
            hn_ref[pl.ds(t, 1), n:] = li
            return 0

        lax.fori_loop(0, tb // SCAN_ROWS, body, 0)

    y = _dot(bu_scr[...].astype(BF16), wc_ref[...]) + d_ref[...] * u
    y = 0.5 * y * (1.0 + jnp.tanh(math.sqrt(2.0 / math.pi) * (y + 0.044715 * (y * y * y))))
    y = y * _sigmoid(_dot(y.astype(BF16), wglu_ref[...]))
    if chained:
        y_ref[0] = y.astype(y_ref.dtype)
    else:
        y_ref[...] = y.astype(y_ref.dtype)


def _ssm_common_specs(zero_map2, zero_map3):
    return [pl.BlockSpec((SSM_WIDTH, 2 * SSM_N), zero_map2),
            pl.BlockSpec((8, SCAN_ROWS, SSM_N), zero_map3),
            pl.BlockSpec((2 * SSM_N, SSM_WIDTH), zero_map2),
            pl.BlockSpec((1, SSM_WIDTH), zero_map2),
            pl.BlockSpec((SSM_WIDTH, SSM_WIDTH), zero_map2)]


def _ssm_prompt(proj3d, sp, tb):
    b, l, _ = proj3d.shape
    nt = l // tb
    dummy_h0 = jnp.zeros((SCAN_ROWS, 2 * SSM_N), F32)
    kern = functools.partial(_ssm_kernel, chained=True, tiles_per_seq=nt, tb=tb)

    def kernel(u_ref, h0_ref, wb, tab, wc, d, wglu, y_ref, hn_ref, bu_scr, carry_scr):
        kern(u_ref.at[0], h0_ref, wb, tab, wc, d, wglu, y_ref, hn_ref, bu_scr, carry_scr)

    return pl.pallas_call(
        kernel,
        grid=(b, nt),
        in_specs=[pl.BlockSpec((1, tb, SSM_WIDTH), lambda bi, i: (bi, i, COL_SSM)),
                  pl.BlockSpec((SCAN_ROWS, 2 * SSM_N), lambda bi, i: (0, 0))]
                 + _ssm_common_specs(lambda bi, i: (0, 0), lambda bi, i: (0, 0, 0)),
        out_specs=[pl.BlockSpec((1, tb, SSM_WIDTH), lambda bi, i: (bi, i, 0)),
                   pl.BlockSpec((1, SCAN_ROWS, 2 * SSM_N), lambda bi, i: (bi, 0, 0))],
        out_shape=[jax.ShapeDtypeStruct((b, l, SSM_WIDTH), BF16),
                   jax.ShapeDtypeStruct((b, SCAN_ROWS, 2 * SSM_N), F32)],
        scratch_shapes=[pltpu.VMEM((tb, 2 * SSM_N), F32), pltpu.VMEM((SCAN_ROWS, 2 * SSM_N), F32)],
        compiler_params=_cparams(("parallel", "arbitrary")),
        name="ssm_prompt",
    )(proj3d, dummy_h0, sp["wb"], sp["tab"], sp["wc"], sp["d"], sp["wglu"])


def _ssm_sample(proj2d, h0, sp):
    rows = proj2d.shape[0]
    nseq = h0.shape[0]
    kern = functools.partial(_ssm_kernel, chained=False, tiles_per_seq=1, tb=rows)
    return pl.pallas_call(
        kern,
        grid=(1,),
        in_specs=[pl.BlockSpec((rows, SSM_WIDTH), lambda i: (0, COL_SSM)),
                  pl.BlockSpec((nseq, 2 * SSM_N), lambda i: (0, 0))]
                 + _ssm_common_specs(lambda i: (0, 0), lambda i: (0, 0, 0)),
        out_specs=[pl.BlockSpec((rows, SSM_WIDTH), lambda i: (0, 0)),
                   pl.BlockSpec((nseq, 2 * SSM_N), lambda i: (0, 0))],
        out_shape=[jax.ShapeDtypeStruct((rows, SSM_WIDTH), BF16),
                   jax.ShapeDtypeStruct((nseq, 2 * SSM_N), F32)],
        scratch_shapes=[pltpu.VMEM((rows, 2 * SSM_N), F32), pltpu.VMEM((SCAN_ROWS, 2 * SSM_N), F32)],
        compiler_params=_cparams(("arbitrary",)),
        name="ssm_sample",
    )(proj2d, h0, sp["wb"], sp["tab"], sp["wc"], sp["d"], sp["wglu"])


def _ssm_params(lam_re, lam_im, log_dt, b_re, b_im, c_re, c_im, d_skip, w_glu):
    lr, li = lam_re.astype(F32), lam_im.astype(F32)
    dt = jnp.exp(log_dt.astype(F32))[:, None]
    mag = jnp.exp(lr * dt)
    ab_re, ab_im = mag * jnp.cos(li * dt), mag * jnp.sin(li * dt)
    den = lr * lr + li * li
    nr = ab_re - 1.0
    f_re = (nr * lr + ab_im * li) / den
    f_im = (ab_im * lr - nr * li) / den
    br, bi = b_re.astype(F32), b_im.astype(F32)
    bb_re = f_re[..., None] * br - f_im[..., None] * bi
    bb_im = f_re[..., None] * bi + f_im[..., None] * br
    eye = jnp.eye(SSM_GROUPS, dtype=F32)

    def in_mat(bb):
        return jnp.einsum('gpc,gh->gchp', bb, eye).reshape(SSM_WIDTH, SSM_N)

    def out_mat(c):
        return jnp.einsum('gcp,gh->gphc', c.astype(F32), eye).reshape(SSM_N, SSM_WIDTH)

    wb = jnp.concatenate([in_mat(bb_re), in_mat(bb_im)], axis=1).astype(BF16)
    wc = jnp.concatenate([out_mat(c_re), -out_mat(c_im)], axis=0).astype(BF16)

    ar, ai = ab_re.reshape(1, SSM_N), ab_im.reshape(1, SSM_N)

    def cmul(x, y):
        return (x[0] * y[0] - x[1] * y[1], x[0] * y[1] + x[1] * y[0])

    pw = [(ar, ai)]
    for _ in range(SCAN_ROWS - 1):
        pw.append(cmul(pw[-1], (ar, ai)))
    row = jnp.arange(SCAN_ROWS)[:, None]
    tabs = []
    for shift in (1, 2, 4):
        for part in pw[shift - 1]:
            tabs.append(jnp.where(row >= shift, part, 0.0))
    tabs.append(jnp.concatenate([p[0] for p in pw], axis=0))
    tabs.append(jnp.concatenate([p[1] for p in pw], axis=0))
    tab = jnp.stack([jnp.broadcast_to(t, (SCAN_ROWS, SSM_N)) for t in tabs])
    return dict(wb=wb, tab=tab, wc=wc, d=d_skip.astype(F32).reshape(1, SSM_WIDTH), wglu=w_glu.astype(BF16))


_RET_G = 1.0 - np.exp2(-5.0 - np.arange(RET_HEADS, dtype=np.float64))


def _ret_consts(chunk, n_rows):
    idx = np.arange(n_rows)
    loc = idx % chunk
    same = (idx[:, None] // chunk) == (idx[None, :] // chunk)
    diff = loc[:, None] - loc[None, :]
    dec = np.where(same & (diff >= 0), _RET_G[:, None, None] ** np.maximum(diff, 0)[None], 0.0)
    qdec = np.repeat((_RET_G[None, :] ** (loc[:, None] + 1.0)), HEAD_DIM, axis=1)
    kdec = np.repeat((_RET_G[None, :] ** (chunk - 1.0 - loc[:, None])), HEAD_DIM, axis=1)
    return (jnp.asarray(dec, F32), jnp.asarray(qdec, F32), jnp.asarray(kdec, F32),
            jnp.asarray(_RET_G ** chunk, F32))


def _rope_tables(pos):
    half = HEAD_DIM // 2
    theta = 1.0 / (ROPE_BASE ** np.linspace(0.0, 1.0, half))
    ang = np.asarray(pos, np.float64)[:, None] * theta[None, :]
    cos = np.repeat(np.cos(ang), 2, axis=1)
    sin = np.repeat(np.sin(ang), 2, axis=1) * np.tile([-1.0, 1.0], half)[None]
    return (jnp.asarray(np.tile(cos, (1, RET_HEADS)), F32), jnp.asarray(np.tile(sin, (1, RET_HEADS)), F32))


def _rotate_pairs(x, cos, sin_signed):
    lane = lax.broadcasted_iota(jnp.int32, x.shape, 1)
    nxt = pltpu.roll(x, x.shape[1] - 1, 1)
    prv = pltpu.roll(x, 1, 1)
    return x * cos + jnp.where(lane % 2 == 0, nxt, prv) * sin_signed


def _ret_head_out(o, gate, norm):
    ms = jnp.mean(o * o, axis=-1, keepdims=True)
    return o * lax.rsqrt(ms + RMS_EPS) * norm * (gate * _sigmoid(gate))


def _ret_prompt_kernel(gc_ref, q_ref, k_ref, v_ref, g_ref, cos_ref, sin_ref, dec_ref, qdec_ref,
                       kdec_ref, norm_ref, y_ref, r_ref):
    @pl.when(pl.program_id(1) == 0)
    def _():
        r_ref[...] = jnp.zeros_like(r_ref)

    cos, sin = cos_ref[...], sin_ref[...]
    q = _rotate_pairs(q_ref[0], cos, sin)
    k = _rotate_pairs(k_ref[0], cos, sin) * (HEAD_DIM ** -0.5)
    v = v_ref[0]
    g = g_ref[0]
    kd = k * kdec_ref[...]
    qdec = qdec_ref[...]
    norm = norm_ref[...]
    for h in range(RET_HEADS):
        sl = slice(h * HEAD_DIM, (h + 1) * HEAD_DIM)
        qh, kh, vh = q[:, sl].astype(BF16), k[:, sl].astype(BF16), v[:, sl].astype(BF16)
        s = _dot_nt(qh, kh) * dec_ref[h]
        r = r_ref[0, h]
        o = _dot(s.astype(BF16), vh) + _dot(qh, r.astype(BF16)) * qdec[:, sl]
        y_ref[0, :, sl] = _ret_head_out(o, g[:, sl], norm[:, sl]).astype(y_ref.dtype)
        r_ref[0, h] = gc_ref[h] * r + _dot_tn(kd[:, sl].astype(BF16), vh)


def _ret_prompt(proj3d, cos, sin, norm):
    b, l, _ = proj3d.shape
    c = RET_CHUNK
    dec, qdec, kdec, gc = _ret_consts(c, c)
    blk = (1, c, MIX_BLOCK)
    tspec = pl.BlockSpec((c, MIX_BLOCK), lambda bi, i: (i, 0))
    cspec = pl.BlockSpec((c, MIX_BLOCK), lambda bi, i: (0, 0))
    return pl.pallas_call(
        _ret_prompt_kernel,
        grid=(b, l // c),
        in_specs=[pl.BlockSpec(memory_space=pltpu.SMEM),
                  pl.BlockSpec(blk, lambda bi, i: (bi, i, COL_RET_Q)),
                  pl.BlockSpec(blk, lambda bi, i: (bi, i, COL_RET_K)),
                  pl.BlockSpec(blk, lambda bi, i: (bi, i, COL_RET_V)),
                  pl.BlockSpec(blk, lambda bi, i: (bi, i, COL_RET_G)),
                  tspec, tspec,
                  pl.BlockSpec((RET_HEADS, c, c), lambda bi, i: (0, 0, 0)),
                  cspec, cspec,
                  pl.BlockSpec((1, MIX_BLOCK), lambda bi, i: (0, 0))],
        out_specs=[pl.BlockSpec(blk, lambda bi, i: (bi, i, 0)),
                   pl.BlockSpec((1, RET_HEADS, HEAD_DIM, HEAD_DIM), lambda bi, i: (bi, 0, 0, 0))],
        out_shape=[jax.ShapeDtypeStruct((b, l, MIX_BLOCK), BF16),
                   jax.ShapeDtypeStruct((b, RET_HEADS, HEAD_DIM, HEAD_DIM), F32)],
        compiler_params=_cparams(("parallel", "arbitrary")),
        name="ret_prompt",
    )(gc, proj3d, proj3d, proj3d, proj3d, cos, sin, dec, qdec, kdec, norm)


def _ret_sample_kernel(gc_ref, q_ref, k_ref, v_ref, g_ref, cos_ref, sin_ref, dec_ref, qdec_ref,
                       kdec_ref, norm_ref, r0_ref, y_ref, rn_ref, *, n_new, s_blk):
    cos, sin = cos_ref[...], sin_ref[...]
    q = _rotate_pairs(q_ref[...], cos, sin)
    k = _rotate_pairs(k_ref[...], cos, sin) * (HEAD_DIM ** -0.5)
    v = v_ref[...]
    g = g_ref[...]
    kd = k * kdec_ref[...]
    qdec = qdec_ref[...]
    norm = norm_ref[...]
    rows = s_blk * n_new
    seq = lax.broadcasted_iota(jnp.int32, (rows, HEAD_DIM), 0) // n_new
    for h in range(RET_HEADS):
        sl = slice(h * HEAD_DIM, (h + 1) * HEAD_DIM)
        qf, kdf = q[:, sl], kd[:, sl]
        qh, kh, vh = qf.astype(BF16), k[:, sl].astype(BF16), v[:, sl].astype(BF16)
        s = _dot_nt(qh, kh) * dec_ref[h]
        cross = jnp.zeros((rows, HEAD_DIM), F32)
        for si in range(s_blk):
            mine = seq == si
            r = r0_ref[si, h]
            cross = cross + _dot(jnp.where(mine, qf, 0.0).astype(BF16), r.astype(BF16))
            rn_ref[si, h] = gc_ref[h] * r + _dot_tn(jnp.where(mine, kdf, 0.0).astype(BF16), vh)
        o = _dot(s.astype(BF16), vh) + cross * qdec[:, sl]
        y_ref[:, sl] = _ret_head_out(o, g[:, sl], norm[:, sl]).astype(y_ref.dtype)


def _ret_sample(proj2d, r0, cos, sin, norm, n_new, s_blk):
    rows = s_blk * n_new
    nseq = r0.shape[0]
    dec, qdec, kdec, gc = _ret_consts(n_new, rows)
    blk = (rows, MIX_BLOCK)
    cspec = pl.BlockSpec(blk, lambda i: (0, 0))
    rblk = (s_blk, RET_HEADS, HEAD_DIM, HEAD_DIM)
    return pl.pallas_call(
        functools.partial(_ret_sample_kernel, n_new=n_new, s_blk=s_blk),
        grid=(nseq // s_blk,),
        in_specs=[pl.BlockSpec(memory_space=pltpu.SMEM),
                  pl.BlockSpec(blk, lambda i: (i, COL_RET_Q)),
                  pl.BlockSpec(blk, lambda i: (i, COL_RET_K)),
                  pl.BlockSpec(blk, lambda i: (i, COL_RET_V)),
                  pl.BlockSpec(blk, lambda i: (i, COL_RET_G)),
                  cspec, cspec,
                  pl.BlockSpec((RET_HEADS, rows, rows), lambda i: (0, 0, 0)),
                  cspec, cspec,
                  pl.BlockSpec((1, MIX_BLOCK), lambda i: (0, 0)),
                  pl.BlockSpec(rblk, lambda i: (i, 0, 0, 0))],
        out_specs=[pl.BlockSpec(blk, lambda i: (i, 0)),
                   pl.BlockSpec(rblk, lambda i: (i, 0, 0, 0))],
        out_shape=[jax.ShapeDtypeStruct((nseq * n_new, MIX_BLOCK), BF16),
                   jax.ShapeDtypeStruct((nseq, RET_HEADS, HEAD_DIM, HEAD_DIM), F32)],
        compiler_params=_cparams(("parallel",)),
        name="ret_sample",
    )(gc, proj2d, proj2d, proj2d, proj2d, cos, sin, dec, qdec, kdec, norm, r0)


def _block_diag(w):
    g, n, _ = w.shape
    return jnp.einsum('gcd,gh->gchd', w, jnp.eye(g, dtype=w.dtype)).reshape(g * n, g * n)


def _layer_params(l, p):
    return dict(
        norm_mix=p['norm_mix'][l].reshape(1, D_MODEL),
        norm_ffn=p['norm_ffn'][l].reshape(1, D_MODEL),
        w_in=p['w_in'][l].astype(BF16),
        w_out=p['w_out'][l].astype(BF16),
        pool_w=_block_diag(p['pool_w'][l].astype(F32)).astype(BF16),
        pool_scale=p['pool_scale'][l].astype(F32).reshape(1, POOL_WIDTH),
        qg=p['swa_q_norm'][l].astype(F32).reshape(1, HEAD_DIM),
        kg=p['swa_k_norm'][l].astype(F32).reshape(1, HEAD_DIM),
        sinks=p['swa_sinks'][l].astype(F32),
        ssm=_ssm_params(p['ssm_lambda_re'][l], p['ssm_lambda_im'][l], p['ssm_log_dt'][l],
                        p['ssm_b_re'][l], p['ssm_b_im'][l], p['ssm_c_re'][l], p['ssm_c_im'][l],
                        p['ssm_d'][l], p['ssm_w_glu'][l]),
        ret_norm=p['ret_norm'][l].astype(F32).reshape(1, MIX_BLOCK),
    )


def _channel_mix(xs_list, l, lp, p):
    i = l // 2
    if l % 2 == 0:
        wg, wu, wd = (p[k][i].astype(BF16) for k in ('ffn_w_gate', 'ffn_w_up', 'ffn_w_down'))
        return [_ffn(x, lp['norm_ffn'], wg, wu, wd, 1024, 256) for x in xs_list]
    wr = jnp.pad(p['moe_router'][i].astype(F32), ((0, 0), (0, 128 - N_EXPERTS)))
    wg, wu, wd = (p[k][i].astype(BF16) for k in ('moe_w_gate', 'moe_w_up', 'moe_w_down'))
    return _moe(xs_list, lp['norm_ffn'], wr, wg, wu, wd)


def _mix_prompt(x2, b, l, lp, bias, cos, sin):
    proj2 = _norm_matmul(x2, lp['norm_mix'], lp['w_in'], 512)
    proj3 = proj2.reshape(b, l, IN_WIDTH)
    tb = 512
    y_pool = _pool(proj2, COL_POOL, lp['pool_w'], lp['pool_scale'], n_rows=b * l, tb=tb,
                   tiles_per_seq=l // tb, pos0=0)
    y_swa, kn = _swa_prompt(proj3, lp['sinks'], lp['qg'], lp['kg'], bias)
    y_ssm, hn = _ssm_prompt(proj3, lp['ssm'], 512)
    y_ret, rn = _ret_prompt(proj3, cos, sin, lp['ret_norm'])
    ys = (y_pool, y_swa.reshape(b * l, MIX_BLOCK), y_ssm.reshape(b * l, MIX_BLOCK),
          y_ret.reshape(b * l, MIX_BLOCK))
    x2 = _out_proj(x2, ys, lp['w_out'], 512)
    kw = SWA_KV_HEADS * HEAD_DIM
    hn = hn[:, 0]
    states = (proj3[:, l - POOL_BUF:, :POOL_WIDTH],
              kn.reshape(b, SWA_WINDOW, SWA_KV_HEADS, HEAD_DIM),
              proj3[:, l - SWA_WINDOW:, COL_SWA_KV * MIX_BLOCK + kw:(COL_SWA_KV + 1) * MIX_BLOCK]
              .reshape(b, SWA_WINDOW, SWA_KV_HEADS, HEAD_DIM),
              jnp.stack([hn[:, :SSM_N], hn[:, SSM_N:]], axis=-1).reshape(b, SSM_GROUPS, SSM_STATE, 2),
              rn)
    return x2, states


SAMPLE_SEQ_BLOCK = 16


def _mix_sample(x2, nseq, n_new, start_pos, lp, st, bias, cos, sin):
    state_pool, cache_k, cache_v, state_ssm, state_ret = st
    rows = nseq * n_new
    wb = cache_k.shape[1]
    kw = SWA_KV_HEADS * HEAD_DIM
    ext_rows = POOL_HALO + n_new
    proj2 = _norm_matmul(x2, lp['norm_mix'], lp['w_in'], 512)
    proj3 = proj2.reshape(nseq, n_new, IN_WIDTH)
    u_pool = proj3[:, :, :POOL_WIDTH]
    buf = state_pool.astype(F32)
    ext = jnp.concatenate([jnp.zeros((nseq, POOL_HALO - POOL_BUF, POOL_WIDTH), F32), buf, u_pool], axis=1)
    y_pool = _pool(ext.reshape(nseq * ext_rows, POOL_WIDTH), 0, lp['pool_w'], lp['pool_scale'],
                   n_rows=nseq * ext_rows, tb=nseq * ext_rows, tiles_per_seq=1, pos0=start_pos)
    y_pool = y_pool.reshape(nseq, ext_rows, POOL_WIDTH)[:, POOL_HALO:].reshape(rows, POOL_WIDTH)
    y_swa, nk, nv = _swa_sample(proj3, cache_k.reshape(nseq, wb, kw).astype(F32),
                                cache_v.reshape(nseq, wb, kw).astype(F32),
                                lp['sinks'], lp['qg'], lp['kg'], bias, SAMPLE_SEQ_BLOCK)
    h0 = state_ssm.astype(F32).reshape(nseq, SSM_N, 2)
    h0 = jnp.concatenate([h0[..., 0], h0[..., 1]], axis=1)
    y_ssm, hn = _ssm_sample(proj2, h0, lp['ssm'])
    y_ret, rn = _ret_sample(proj2, state_ret.astype(F32), cos, sin, lp['ret_norm'], n_new, SAMPLE_SEQ_BLOCK)
    ys = (y_pool, y_swa.reshape(rows, MIX_BLOCK), y_ssm, y_ret)
    x2 = _out_proj(x2, ys, lp['w_out'], 512)
    states = (jnp.concatenate([buf, u_pool], axis=1)[:, -POOL_BUF:],
              nk.reshape(nseq, SWA_WINDOW, SWA_KV_HEADS, HEAD_DIM),
              nv.reshape(nseq, SWA_WINDOW, SWA_KV_HEADS, HEAD_DIM),
              jnp.stack([hn[:, :SSM_N], hn[:, SSM_N:]], axis=-1).reshape(nseq, SSM_GROUPS, SSM_STATE, 2),
              rn)
    return x2, states


def _forward(x_prompt, x_sample, past_len, sample_state, p, rel_bias):
    b, l, d = x_prompt.shape
    nseq, n_new, _ = x_sample.shape
    wb = sample_state[1].shape[2]
    depth = p['norm_mix'].shape[0]
    bias_p = _swa_bias(rel_bias, np.arange(SWA_BLOCK)[:, None] - np.arange(2 * SWA_BLOCK)[None, :] + SWA_BLOCK)
    bias_s = _swa_bias(rel_bias, np.arange(n_new)[:, None] - np.arange(wb + n_new)[None, :] + wb)
    rope_p = _rope_tables(np.arange(l))
    rope_s = _rope_tables(past_len + (np.arange(SAMPLE_SEQ_BLOCK * n_new) % n_new))
    xp = x_prompt.reshape(b * l, d)
    xs = x_sample.reshape(nseq * n_new, d)
    st_p, st_s = [], []
    for li in range(depth):
        lp = _layer_params(li, p)
        xp, sp = _mix_prompt(xp, b, l, lp, bias_p, *rope_p)
        xs, ss = _mix_sample(xs, nseq, n_new, past_len, lp, [s[li] for s in sample_state], bias_s, *rope_s)
        xp, xs = _channel_mix([xp, xs], li, lp, p)
        st_p.append(sp)
        st_s.append(ss)
    outs = [xp.reshape(b, l, d), xs.reshape(nseq, n_new, d)]
    for k in range(5):
        outs.append(jnp.stack([s[k] for s in st_p]))
        outs.append(jnp.stack([s[k] for s in st_s]))
    return tuple(outs)


PAST_LEN = 16384


def kernel(x_prompt, x_sample, state_pool, cache_swa_k, cache_swa_v, state_ssm, state_ret,
           norm_mix, norm_ffn, w_in, w_out, pool_w, pool_scale, swa_q_norm, swa_k_norm, swa_sinks,
           rel_bias, ssm_lambda_re, ssm_lambda_im, ssm_log_dt, ssm_b_re, ssm_b_im, ssm_c_re, ssm_c_im,
           ssm_d, ssm_w_glu, ret_norm, ffn_w_gate, ffn_w_up, ffn_w_down, moe_router, moe_w_gate,
           moe_w_up, moe_w_down):
    p = dict(norm_mix=norm_mix, norm_ffn=norm_ffn, w_in=w_in, w_out=w_out, pool_w=pool_w,
             pool_scale=pool_scale, swa_q_norm=swa_q_norm, swa_k_norm=swa_k_norm, swa_sinks=swa_sinks,
             ssm_lambda_re=ssm_lambda_re, ssm_lambda_im=ssm_lambda_im, ssm_log_dt=ssm_log_dt,
             ssm_b_re=ssm_b_re, ssm_b_im=ssm_b_im, ssm_c_re=ssm_c_re, ssm_c_im=ssm_c_im,
             ssm_d=ssm_d, ssm_w_glu=ssm_w_glu, ret_norm=ret_norm,
             ffn_w_gate=ffn_w_gate, ffn_w_up=ffn_w_up, ffn_w_down=ffn_w_down, moe_router=moe_router,
             moe_w_gate=moe_w_gate, moe_w_up=moe_w_up, moe_w_down=moe_w_down)
    return _forward(x_prompt, x_sample, PAST_LEN,
                    (state_pool, cache_swa_k, cache_swa_v, state_ssm, state_ret), p, rel_bias)
```

```python
import functools
import math

import numpy as np
import jax
import jax.numpy as jnp
from jax import lax
from jax.experimental import pallas as pl
from jax.experimental.pallas import tpu as pltpu

F32 = jnp.float32
BF16 = jnp.bfloat16

D_MODEL = 1024
HEAD_DIM = 64
POOL_WIDTH = 256
POOL_WINDOWS = (2, 4, 8, 16)
POOL_BUF = 15
POOL_HALO = 16
SWA_HEADS = 4
SWA_KV_HEADS = 2
SWA_WINDOW = 128
SWA_BLOCK = 128
SSM_WIDTH = 256
SSM_CH = 16
SSM_GROUPS = 16
SSM_STATE = 64
SSM_N = SSM_GROUPS * SSM_STATE
RET_HEADS = 4
RET_CHUNK = 128
ROPE_BASE = 10000.0
IN_WIDTH = 2048
MIX_BLOCK = 256
D_FF = 2816
N_EXPERTS = 8
T5_BUCKETS = 32
T5_MAX_DIST = 128
RMS_EPS = 1e-6
NEG = -1e30
SCAN_ROWS = 8

COL_POOL, COL_SWA_Q, COL_SWA_KV, COL_SSM, COL_RET_Q, COL_RET_K, COL_RET_V, COL_RET_G = range(8)

VMEM_LIMIT = 48 * 1024 * 1024


def _cparams(sem):
    return pltpu.CompilerParams(dimension_semantics=sem, vmem_limit_bytes=VMEM_LIMIT)


def _rms(x, g):
    ms = jnp.mean(x * x, axis=-1, keepdims=True)
    return x * lax.rsqrt(ms + RMS_EPS) * g


def _dot(a, b):
    return jnp.dot(a, b, preferred_element_type=F32)


def _dot_nt(a, b):
    return lax.dot_general(a, b, (((1,), (1,)), ((), ())), preferred_element_type=F32)


def _dot_tn(a, b):
    return lax.dot_general(a, b, (((0,), (0,)), ((), ())), preferred_element_type=F32)


def _sigmoid(x):
    return 1.0 / (1.0 + jnp.exp(-x))


def _norm_matmul_kernel(x_ref, g_ref, w_ref, o_ref):
    h = _rms(x_ref[...], g_ref[...]).astype(BF16)
    o_ref[...] = _dot(h, w_ref[...])


def _norm_matmul(x, g, w, tm):
    t, d = x.shape
    tm = min(tm, t)
    n = w.shape[1]
    return pl.pallas_call(
        _norm_matmul_kernel,
        grid=(t // tm,),
        in_specs=[pl.BlockSpec((tm, d), lambda i: (i, 0)),
                  pl.BlockSpec((1, d), lambda i: (0, 0)),
                  pl.BlockSpec((d, n), lambda i: (0, 0))],
        out_specs=pl.BlockSpec((tm, n), lambda i: (i, 0)),
        out_shape=jax.ShapeDtypeStruct((t, n), F32),
        compiler_params=_cparams(("parallel",)),
        name="norm_matmul",
    )(x, g, w)


def _out_proj_kernel(x_ref, y0_ref, y1_ref, y2_ref, y3_ref, w_ref, o_ref):
    acc = x_ref[...]
    for k, y_ref in enumerate((y0_ref, y1_ref, y2_ref, y3_ref)):
        acc = acc + _dot(y_ref[...], w_ref[k * MIX_BLOCK:(k + 1) * MIX_BLOCK, :])
    o_ref[...] = acc


def _out_proj(x, ys, w, tm):
    t, d = x.shape
    tm = min(tm, t)
    yspec = pl.BlockSpec((tm, MIX_BLOCK), lambda i: (i, 0))
    return pl.pallas_call(
        _out_proj_kernel,
        grid=(t // tm,),
        in_specs=[pl.BlockSpec((tm, d), lambda i: (i, 0)), yspec, yspec, yspec, yspec,
                  pl.BlockSpec((d, d), lambda i: (0, 0))],
        out_specs=pl.BlockSpec((tm, d), lambda i: (i, 0)),
        out_shape=jax.ShapeDtypeStruct((t, d), F32),
        compiler_params=_cparams(("parallel",)),
        name="out_proj",
    )(x, *ys, w)


def _swiglu_chunk(h, wg, wu, wd):
    a = _dot(h, wg)
    b = _dot(h, wu)
    return _dot((a * _sigmoid(a) * b).astype(BF16), wd)


def _ffn_kernel(x_ref, g_ref, wg_ref, wu_ref, wd_ref, o_ref, h_scr):
    @pl.when(pl.program_id(1) == 0)
    def _():
        x = x_ref[...]
        h_scr[...] = _rms(x, g_ref[...]).astype(BF16)
        o_ref[...] = x

    o_ref[...] += _swiglu_chunk(h_scr[...], wg_ref[...], wu_ref[...], wd_ref[...])


def _ffn(x, g, wg, wu, wd, tm, tf):
    t, d = x.shape
    tm = min(tm, t)
    f = wg.shape[1]
    return pl.pallas_call(
        _ffn_kernel,
        grid=(t // tm, f // tf),
        in_specs=[pl.BlockSpec((tm, d), lambda i, j: (i, 0)),
                  pl.BlockSpec((1, d), lambda i, j: (0, 0)),
                  pl.BlockSpec((d, tf), lambda i, j: (0, j)),
                  pl.BlockSpec((d, tf), lambda i, j: (0, j)),
                  pl.BlockSpec((tf, d), lambda i, j: (j, 0))],
        out_specs=pl.BlockSpec((tm, d), lambda i, j: (i, 0)),
        out_shape=jax.ShapeDtypeStruct((t, d), F32),
        scratch_shapes=[pltpu.VMEM((tm, d), BF16)],
        compiler_params=_cparams(("parallel", "arbitrary")),
        name="ffn",
    )(x, g, wg, wu, wd)


ROUTE_ID_LANES = (0, 1)
ROUTE_GATE_LANES = (2, 3)


def _router_kernel(x_ref, g_ref, wr_ref, c_ref):
    h = _rms(x_ref[...], g_ref[...])
    logits = jnp.dot(h, wr_ref[...], preferred_element_type=F32, precision=lax.Precision.HIGHEST)
    lane = lax.broadcasted_iota(jnp.int32, logits.shape, 1).astype(F32)
    lg = jnp.where(lane < N_EXPERTS, logits, NEG)
    m1 = jnp.max(lg, axis=-1, keepdims=True)
    i1 = jnp.min(jnp.where(lg == m1, lane, 128.0), axis=-1, keepdims=True)
    lg2 = jnp.where(lane == i1, NEG, lg)
    m2 = jnp.max(lg2, axis=-1, keepdims=True)
    i2 = jnp.min(jnp.where(lg2 == m2, lane, 128.0), axis=-1, keepdims=True)
    ex = jnp.exp(m2 - m1)
    vals = (i1, i2, 1.0 / (1.0 + ex), ex / (1.0 + ex))
    out = jnp.zeros_like(logits)
    for ln, v in zip(ROUTE_ID_LANES + ROUTE_GATE_LANES, vals):
        out = jnp.where(lane == ln, v, out)
    c_ref[...] = out


def _router(x, g, wr, tm):
    t, d = x.shape
    tm = min(tm, t)
    return pl.pallas_call(
        _router_kernel,
        grid=(t // tm,),
        in_specs=[pl.BlockSpec((tm, d), lambda i: (i, 0)),
                  pl.BlockSpec((1, d), lambda i: (0, 0)),
                  pl.BlockSpec((d, 128), lambda i: (0, 0))],
        out_specs=pl.BlockSpec((tm, 128), lambda i: (i, 0)),
        out_shape=jax.ShapeDtypeStruct((t, 128), F32),
        compiler_params=_cparams(("parallel",)),
        name="router",
    )(x, g, wr)


def _row_copy(src, i, dst, j, sem):
    return pltpu.make_async_copy(src.at[pl.ds(i, 1)], dst.at[pl.ds(j, 1)], sem)


def _dispatch_kernel(pos_ref, x_ref, xs_in, xs_hbm, sem, *, td):
    del xs_in

    def issue(j, c):
        for k in range(2):
            _row_copy(x_ref, j, xs_hbm, pos_ref[0, 0, 2 * j + k], sem).start()
        return c

    lax.fori_loop(0, td, issue, 0)
    for _ in range(2):
        pltpu.make_async_copy(x_ref, xs_hbm.at[pl.ds(0, td)], sem).wait()


def _dispatch(x, pos, xs, td):
    t, d = x.shape
    td = min(td, t)
    pos3 = pos.reshape(t // td, 1, 2 * td)
    return pl.pallas_call(
        functools.partial(_dispatch_kernel, td=td),
        grid=(t // td,),
        in_specs=[pl.BlockSpec((1, 1, 2 * td), lambda i: (i, 0, 0), memory_space=pltpu.SMEM),
                  pl.BlockSpec((td, d), lambda i: (i, 0)),
                  pl.BlockSpec(memory_space=pl.ANY)],
        out_specs=pl.BlockSpec(memory_space=pl.ANY),
        out_shape=jax.ShapeDtypeStruct(xs.shape, xs.dtype),
        scratch_shapes=[pltpu.SemaphoreType.DMA],
        input_output_aliases={2: 0},
        compiler_params=_cparams(("arbitrary",)),
        name="moe_dispatch",
    )(pos3, x, xs)


def _grouped_ffn_kernel(te_ref, nu_ref, x_ref, g_ref, wg_ref, wu_ref, wd_ref, o_ref, h_scr):
    del te_ref
    j = pl.program_id(1)
    used = pl.program_id(0) < nu_ref[0]

    @pl.when(jnp.logical_not(used) & (j == 0))
    def _():
        o_ref[...] = jnp.zeros_like(o_ref)

    @pl.when(used)
    def _():
        @pl.when(j == 0)
        def _():
            h_scr[...] = _rms(x_ref[...], g_ref[...]).astype(BF16)

        y = _swiglu_chunk(h_scr[...], wg_ref[0], wu_ref[0], wd_ref[0])

        @pl.when(j == 0)
        def _():
            o_ref[...] = y

        @pl.when(j > 0)
        def _():
            o_ref[...] += y


def _grouped_ffn(xs, g, tile_expert, n_used, wg, wu, wd, tm, tf):
    r, d = xs.shape
    f = wg.shape[2]
    nj = f // tf

    def row_map(i, j, te, nu):
        return (i, 0)

    def col_of(i, j, nu):
        return jnp.where(i < nu[0], j, nj - 1)

    grid_spec = pltpu.PrefetchScalarGridSpec(
        num_scalar_prefetch=2,
        grid=(r // tm, nj),
        in_specs=[pl.BlockSpec((tm, d), row_map),
                  pl.BlockSpec((1, d), lambda i, j, te, nu: (0, 0)),
                  pl.BlockSpec((1, d, tf), lambda i, j, te, nu: (te[i], 0, col_of(i, j, nu))),
                  pl.BlockSpec((1, d, tf), lambda i, j, te, nu: (te[i], 0, col_of(i, j, nu))),
                  pl.BlockSpec((1, tf, d), lambda i, j, te, nu: (te[i], col_of(i, j, nu), 0))],
        out_specs=pl.BlockSpec((tm, d), row_map),
        scratch_shapes=[pltpu.VMEM((tm, d), BF16)],
    )
    return pl.pallas_call(
        _grouped_ffn_kernel,
        grid_spec=grid_spec,
        out_shape=jax.ShapeDtypeStruct((r, d), F32),
        compiler_params=_cparams(("arbitrary", "arbitrary")),
        name="moe_grouped_ffn",
    )(tile_expert, n_used, xs, g, wg, wu, wd)


def _combine_kernel(pos_ref, x_ref, route_ref, ys_hbm, o_ref, buf0, buf1, sem, *, tc):
    def issue(j, c):
        _row_copy(ys_hbm, pos_ref[0, 0, 2 * j], buf0, j, sem).start()
        _row_copy(ys_hbm, pos_ref[0, 0, 2 * j + 1], buf1, j, sem).start()
        return c

    lax.fori_loop(0, tc, issue, 0)
    for buf in (buf0, buf1):
        pltpu.make_async_copy(ys_hbm.at[pl.ds(0, tc)], buf, sem).wait()
    route = route_ref[...]
    g0 = route[:, ROUTE_GATE_LANES[0]:ROUTE_GATE_LANES[0] + 1]
    g1 = route[:, ROUTE_GATE_LANES[1]:ROUTE_GATE_LANES[1] + 1]
    o_ref[...] = x_ref[...] + g0 * buf0[...] + g1 * buf1[...]


def _combine(x, route, pos, ys, tc):
    t, d = x.shape
    tc = min(tc, t)
    pos3 = pos.reshape(t // tc, 1, 2 * tc)
    return pl.pallas_call(
        functools.partial(_combine_kernel, tc=tc),
        grid=(t // tc,),
        in_specs=[pl.BlockSpec((1, 1, 2 * tc), lambda i: (i, 0, 0), memory_space=pltpu.SMEM),
                  pl.BlockSpec((tc, d), lambda i: (i, 0)),
                  pl.BlockSpec((tc, 128), lambda i: (i, 0)),
                  pl.BlockSpec(memory_space=pl.ANY)],
        out_specs=pl.BlockSpec((tc, d), lambda i: (i, 0)),
        out_shape=jax.ShapeDtypeStruct((t, d), F32),
        scratch_shapes=[pltpu.VMEM((tc, d), F32), pltpu.VMEM((tc, d), F32), pltpu.SemaphoreType.DMA],
        compiler_params=_cparams(("arbitrary",)),
        name="moe_combine",
    )(pos3, x, route, ys)


MOE_TM = 512


def _route_plan(expert_ids, tm):
    flat = expert_ids.reshape(-1)
    a = flat.shape[0]
    onehot = (flat[None, :] == jnp.arange(N_EXPERTS, dtype=jnp.int32)[:, None]).astype(jnp.int32)
    csum = jnp.cumsum(onehot, axis=1)
    counts = csum[:, -1]
    padded = (counts + tm - 1) // tm * tm
    ends = jnp.cumsum(padded)
    offs = ends - padded
    pos = jnp.sum(onehot * (offs[:, None] + csum - 1), axis=0)
    n_tiles = (a + N_EXPERTS * tm) // tm
    tile_start = jnp.arange(n_tiles, dtype=jnp.int32) * tm
    tile_expert = jnp.minimum(jnp.sum(tile_start[:, None] >= ends[None, :], axis=1), N_EXPERTS - 1)
    n_used = (ends[-1] // tm).reshape(1)
    last = jnp.take(tile_expert, n_used[0] - 1)
    tile_expert = jnp.where(tile_start < ends[-1], tile_expert, last)
    return pos.astype(jnp.int32), tile_expert.astype(jnp.int32), n_used.astype(jnp.int32), n_tiles * tm


def _moe(xs_list, g, wr, wg, wu, wd):
    routes = [_router(x, g, wr, 512) for x in xs_list]
    ids = jnp.concatenate([r[:, ROUTE_ID_LANES[0]:ROUTE_ID_LANES[1] + 1] for r in routes]).astype(jnp.int32)
    pos, tile_expert, n_used, n_rows = _route_plan(ids, MOE_TM)
    xs = jnp.zeros((n_rows, D_MODEL), F32)
    bounds = np.cumsum([0] + [2 * x.shape[0] for x in xs_list])
    pos_list = [pos[lo:hi] for lo, hi in zip(bounds[:-1], bounds[1:])]
    for x, ps in zip(xs_list, pos_list):
        xs = _dispatch(x, ps, xs, 1024)
    ys = _grouped_ffn(xs, g, tile_expert, n_used, wg, wu, wd, MOE_TM, D_FF // 2)
    return [_combine(x, r, ps, ys, 512) for x, r, ps in zip(xs_list, routes, pos_list)]


def _pool_kernel(u_ref, halo_ref, w_ref, scale_ref, o_ref, *, tiles_per_seq, pos0, tb):
    ti = pl.program_id(0) % tiles_per_seq
    u = u_ref[...]
    halo = jnp.where(ti == 0, 0.0, halo_ref[...])
    ext = jnp.concatenate([halo, u], axis=0)
    s2 = ext + pltpu.roll(ext, 1, 0)
    s4 = s2 + pltpu.roll(s2, 2, 0)
    s8 = s4 + pltpu.roll(s4, 4, 0)
    s16 = s8 + pltpu.roll(s8, 8, 0)
    grp = lax.broadcasted_iota(jnp.int32, (tb, POOL_WIDTH), 1) // (POOL_WIDTH // 4)
    row = lax.broadcasted_iota(jnp.int32, (tb, POOL_WIDTH), 0)
    s = jnp.where(grp == 0, s2[POOL_HALO:],
                  jnp.where(grp == 1, s4[POOL_HALO:],
                            jnp.where(grp == 2, s8[POOL_HALO:], s16[POOL_HALO:])))
    win = jnp.where(grp == 0, 2, jnp.where(grp == 1, 4, jnp.where(grp == 2, 8, 16)))
    cnt = jnp.minimum(win, pos0 + ti * tb + row + 1).astype(F32)
    pooled = s / cnt - u
    o_ref[...] = (_dot(pooled.astype(BF16), w_ref[...]) * scale_ref[...]).astype(o_ref.dtype)


def _pool(proj2d, col, w, scale, *, n_rows, tb, tiles_per_seq, pos0):
    per = tb // POOL_HALO
    return pl.pallas_call(
        functools.partial(_pool_kernel, tiles_per_seq=tiles_per_seq, pos0=pos0, tb=tb),
        grid=(n_rows // tb,),
        in_specs=[pl.BlockSpec((tb, POOL_WIDTH), lambda i: (i, col)),
                  pl.BlockSpec((POOL_HALO, POOL_WIDTH), lambda i: (jnp.maximum(i * per - 1, 0), col)),
                  pl.BlockSpec((POOL_WIDTH, POOL_WIDTH), lambda i: (0, 0)),
                  pl.BlockSpec((1, POOL_WIDTH), lambda i: (0, 0))],
        out_specs=pl.BlockSpec((tb, POOL_WIDTH), lambda i: (i, 0)),
        out_shape=jax.ShapeDtypeStruct((n_rows, POOL_WIDTH), BF16),
        compiler_params=_cparams(("parallel",)),
        name="pool",
    )(proj2d, proj2d, w, scale)


def _t5_bucket_np(rel):
    n = np.maximum(rel, 0)
    max_exact = T5_BUCKETS // 2
    nf = np.maximum(n, max_exact).astype(np.float32)
    large = max_exact + (np.log(nf / max_exact) / math.log(T5_MAX_DIST / max_exact)
                         * (T5_BUCKETS - max_exact)).astype(np.int32)
    large = np.minimum(large, T5_BUCKETS - 1)
    return np.where(n < max_exact, n, large)


def _swa_bias(rel_bias, rel):
    valid = (rel >= 0) & (rel < SWA_WINDOW)
    onehot = jnp.asarray(_t5_bucket_np(rel)[..., None] == np.arange(T5_BUCKETS), F32)
    b = jnp.einsum('qsb,bh->hqs', onehot, rel_bias.astype(F32), precision=lax.Precision.HIGHEST)
    return jnp.where(valid[None], b, NEG)


def _softmax_parts(parts, sink):
    m = sink
    for s in parts:
        m = jnp.maximum(m, jnp.max(s, axis=-1, keepdims=True))
    ps = [jnp.exp(s - m) for s in parts]
    denom = jnp.exp(sink - m)
    for p in ps:
        denom = denom + jnp.sum(p, axis=-1, keepdims=True)
    return ps, denom


def _head_mean_matrix(width):
    h = np.arange(width) // HEAD_DIM
    return jnp.asarray((h[:, None] == h[None, :]) / HEAD_DIM, F32)


def _head_rms(x, mean_mat, g):
    ms = jnp.dot(x * x, mean_mat, preferred_element_type=F32, precision=lax.Precision.HIGHEST)
    return x * lax.rsqrt(ms + RMS_EPS) * g


SWA_TILE = 512


def _swa_prompt_kernel(sinks_ref, q_ref, kv_ref, halo_ref, qg_ref, kg_ref, mq_ref, mk_ref, bias_ref,
                       y_ref, kn_ref):
    has_prev = pl.program_id(1) > 0
    kw = SWA_KV_HEADS * HEAD_DIM
    blk = SWA_BLOCK
    kv = kv_ref[0]
    halo = halo_ref[0]
    k_ext = jnp.concatenate([halo[:, :kw], kv[:, :kw]], axis=0)
    v_ext = jnp.concatenate([halo[:, kw:], kv[:, kw:]], axis=0).astype(BF16)
    kn = _head_rms(k_ext, mk_ref[...], kg_ref[...])
    kn_ref[0] = kn[SWA_TILE:]
    knb = kn.astype(BF16)
    qn = (_head_rms(q_ref[0], mq_ref[...], qg_ref[...]) * (HEAD_DIM ** -0.5)).astype(BF16)
    row = lax.broadcasted_iota(jnp.int32, (2 * blk, 1), 0)
    col = lax.broadcasted_iota(jnp.int32, (2 * blk, 2 * blk), 1)
    for c in range(SWA_TILE // blk):
        rows = slice(c * blk, (c + 1) * blk)
        keys = slice(c * blk, (c + 2) * blk)
        for kh in range(SWA_KV_HEADS):
            ksl = slice(kh * HEAD_DIM, (kh + 1) * HEAD_DIM)
            h0 = 2 * kh
            q2 = jnp.concatenate([qn[rows, h0 * HEAD_DIM:(h0 + 1) * HEAD_DIM],
                                  qn[rows, (h0 + 1) * HEAD_DIM:(h0 + 2) * HEAD_DIM]], axis=0)
            s = _dot_nt(q2, knb[keys, ksl]) + bias_ref[kh]
            if c == 0:
                s = jnp.where(has_prev | (col >= blk), s, NEG)
            sink = jnp.where(row < blk, sinks_ref[h0], sinks_ref[h0 + 1])
            (p,), denom = _softmax_parts((s,), sink)
            o = _dot(p.astype(BF16), v_ext[keys, ksl]) / denom
            y_ref[0, rows, h0 * HEAD_DIM:(h0 + 1) * HEAD_DIM] = o[:blk].astype(y_ref.dtype)
            y_ref[0, rows, (h0 + 1) * HEAD_DIM:(h0 + 2) * HEAD_DIM] = o[blk:].astype(y_ref.dtype)


def _swa_prompt(proj3d, sinks, qg, kg, bias):
    b, l, _ = proj3d.shape
    kw = SWA_KV_HEADS * HEAD_DIM
    per = SWA_TILE // SWA_BLOCK
    tile = (1, SWA_TILE, MIX_BLOCK)
    const2 = lambda bi, i: (0, 0)
    bias2 = bias.reshape(SWA_KV_HEADS, 2 * SWA_BLOCK, 2 * SWA_BLOCK)
    return pl.pallas_call(
        _swa_prompt_kernel,
        grid=(b, l // SWA_TILE),
        in_specs=[pl.BlockSpec(memory_space=pltpu.SMEM),
                  pl.BlockSpec(tile, lambda bi, i: (bi, i, COL_SWA_Q)),
                  pl.BlockSpec(tile, lambda bi, i: (bi, i, COL_SWA_KV)),
                  pl.BlockSpec((1, SWA_BLOCK, MIX_BLOCK),
                               lambda bi, i: (bi, jnp.maximum(i * per - 1, 0), COL_SWA_KV)),
                  pl.BlockSpec((1, MIX_BLOCK), const2),
                  pl.BlockSpec((1, kw), const2),
                  pl.BlockSpec((MIX_BLOCK, MIX_BLOCK), const2),
                  pl.BlockSpec((kw, kw), const2),
                  pl.BlockSpec((SWA_KV_HEADS, 2 * SWA_BLOCK, 2 * SWA_BLOCK), lambda bi, i: (0, 0, 0))],
        out_specs=[pl.BlockSpec(tile, lambda bi, i: (bi, i, 0)),
                   pl.BlockSpec((1, SWA_BLOCK, kw), lambda bi, i: (bi, 0, 0))],
        out_shape=[jax.ShapeDtypeStruct((b, l, MIX_BLOCK), BF16),
                   jax.ShapeDtypeStruct((b, SWA_BLOCK, kw), F32)],
        compiler_params=_cparams(("parallel", "arbitrary")),
        name="swa_prompt",
    )(sinks, proj3d, proj3d, proj3d, jnp.tile(qg, (1, SWA_HEADS)), jnp.tile(kg, (1, SWA_KV_HEADS)),
      _head_mean_matrix(MIX_BLOCK), _head_mean_matrix(kw), bias2)


def _swa_sample_kernel(sinks_ref, q_ref, kv_ref, ck_ref, cv_ref, qg_ref, kg_ref, bias_ref,
                       y_ref, nk_ref, nv_ref, *, n_new):
    q = q_ref[...]
    kv = kv_ref[...]
    ck = ck_ref[...]
    cv = cv_ref[...]
    qg = qg_ref[...]
    kg = kg_ref[...]
    kw = SWA_KV_HEADS * HEAD_DIM
    bdot = functools.partial(jnp.einsum, preferred_element_type=F32)
    for kh in range(SWA_KV_HEADS):
        ksl = slice(kh * HEAD_DIM, (kh + 1) * HEAD_DIM)
        vsl = slice(kw + kh * HEAD_DIM, kw + (kh + 1) * HEAD_DIM)
        kn = _rms(kv[:, :, ksl], kg)
        vn = kv[:, :, vsl]
        nk_ref[:, :SWA_WINDOW - n_new, ksl] = ck[:, n_new:, ksl]
        nk_ref[:, SWA_WINDOW - n_new:, ksl] = kn
        nv_ref[:, :SWA_WINDOW - n_new, ksl] = cv[:, n_new:, ksl]
        nv_ref[:, SWA_WINDOW - n_new:, ksl] = vn
        for gq in range(SWA_HEADS // SWA_KV_HEADS):
            h = kh * (SWA_HEADS // SWA_KV_HEADS) + gq
            hsl = slice(h * HEAD_DIM, (h + 1) * HEAD_DIM)
            qn = _rms(q[:, :, hsl], qg)
            s_c = bdot('sqd,skd->sqk', qn, ck[:, :, ksl]) * (HEAD_DIM ** -0.5) + bias_ref[h, :, :SWA_WINDOW]
            s_n = bdot('sqd,skd->sqk', qn, kn) * (HEAD_DIM ** -0.5) + bias_ref[h, :, SWA_WINDOW:]
            (p_c, p_n), denom = _softmax_parts((s_c, s_n), sinks_ref[h])
            o = bdot('sqk,skd->sqd', p_c, cv[:, :, ksl]) + bdot('sqk,skd->sqd', p_n, vn)
            y_ref[:, :, hsl] = (o / denom).astype(y_ref.dtype)


def _swa_sample(proj3d, cache_k, cache_v, sinks, qg, kg, bias, s_blk):
    nseq, n_new, _ = proj3d.shape
    kw = SWA_KV_HEADS * HEAD_DIM
    blk = (s_blk, n_new, MIX_BLOCK)
    cblk = (s_blk, SWA_WINDOW, kw)
    return pl.pallas_call(
        functools.partial(_swa_sample_kernel, n_new=n_new),
        grid=(nseq // s_blk,),
        in_specs=[pl.BlockSpec(memory_space=pltpu.SMEM),
                  pl.BlockSpec(blk, lambda i: (i, 0, COL_SWA_Q)),
                  pl.BlockSpec(blk, lambda i: (i, 0, COL_SWA_KV)),
                  pl.BlockSpec(cblk, lambda i: (i, 0, 0)),
                  pl.BlockSpec(cblk, lambda i: (i, 0, 0)),
                  pl.BlockSpec((1, HEAD_DIM), lambda i: (0, 0)),
                  pl.BlockSpec((1, HEAD_DIM), lambda i: (0, 0)),
                  pl.BlockSpec((SWA_HEADS, n_new, SWA_WINDOW + n_new), lambda i: (0, 0, 0))],
        out_specs=[pl.BlockSpec(blk, lambda i: (i, 0, 0)),
                   pl.BlockSpec(cblk, lambda i: (i, 0, 0)),
                   pl.BlockSpec(cblk, lambda i: (i, 0, 0))],
        out_shape=[jax.ShapeDtypeStruct((nseq, n_new, MIX_BLOCK), BF16),
                   jax.ShapeDtypeStruct((nseq, SWA_WINDOW, kw), F32),
                   jax.ShapeDtypeStruct((nseq, SWA_WINDOW, kw), F32)],
        compiler_params=_cparams(("parallel",)),
        name="swa_sample",
    )(sinks, proj3d, proj3d, cache_k, cache_v, qg, kg, bias)


def _ssm_kernel(u_ref, h0_ref, wb_ref, tab_ref, wc_ref, d_ref, wglu_ref, y_ref, hn_ref,
                bu_scr, carry_scr, *, chained, tiles_per_seq, tb):
    n = SSM_N
    u = u_ref[...]
    bu_scr[...] = _dot(u.astype(BF16), wb_ref[...])

    if chained:
        @pl.when(pl.program_id(1) % tiles_per_seq == 0)
        def _():
            carry_scr[...] = jnp.zeros_like(carry_scr)

    def tile_scan(r0, cr, ci):
        hr = bu_scr[pl.ds(r0, SCAN_ROWS), :n]
        hi = bu_scr[pl.ds(r0, SCAN_ROWS), n:]
        for k, shift in enumerate((1, 2, 4)):
            ar, ai = tab_ref[2 * k], tab_ref[2 * k + 1]
            sr, si = pltpu.roll(hr, shift, 0), pltpu.roll(hi, shift, 0)
            hr, hi = hr + ar * sr - ai * si, hi + ar * si + ai * sr
        pr, pi = tab_ref[6], tab_ref[7]
        hr, hi = hr + pr * cr - pi * ci, hi + pr * ci + pi * cr
        bu_scr[pl.ds(r0, SCAN_ROWS), :n] = hr
        bu_scr[pl.ds(r0, SCAN_ROWS), n:] = hi
        return hr[SCAN_ROWS - 1:], hi[SCAN_ROWS - 1:]

    if chained:
        def body(t, carry):
            r0 = pl.multiple_of(t * SCAN_ROWS, SCAN_ROWS)
            lr, li = tile_scan(r0, *carry)
            return (jnp.broadcast_to(lr, (SCAN_ROWS, n)), jnp.broadcast_to(li, (SCAN_ROWS, n)))

        cr, ci = lax.fori_loop(0, tb // SCAN_ROWS, body, (carry_scr[:, :n], carry_scr[:, n:]), unroll=2)
        carry_scr[:, :n] = cr
        carry_scr[:, n:] = ci
        hn_ref[0, :, :n] = cr
        hn_ref[0, :, n:] = ci
    else:
        def body(t, _):
            r0 = pl.multiple_of(t * SCAN_ROWS, SCAN_ROWS)
            h0 = h0_ref[pl.ds(t, 1), :]
            cr = jnp.broadcast_to(h0[:, :n], (SCAN_ROWS, n))
            ci = jnp.broadcast_to(h0[:, n:], (SCAN_ROWS, n))
            lr, li = tile_scan(r0, cr, ci)
            hn_ref[pl.ds(t, 1), :n] = lr
            hn_ref[pl.ds(t, 1), n:] = li
            return 0

        lax.fori_loop(0, tb // SCAN_ROWS, body, 0)

    y = _dot(bu_scr[...].astype(BF16), wc_ref[...]) + d_ref[...] * u
    y = 0.5 * y * (1.0 + jnp.tanh(math.sqrt(2.0 / math.pi) * (y + 0.044715 * (y * y * y))))
    y = y * _sigmoid(_dot(y.astype(BF16), wglu_ref[...]))
    if chained:
        y_ref[0] = y.astype(y_ref.dtype)
    else:
        y_ref[...] = y.astype(y_ref.dtype)


def _ssm_common_specs(zero_map2, zero_map3):
    return [pl.BlockSpec((SSM_WIDTH, 2 * SSM_N), zero_map2),
            pl.BlockSpec((8, SCAN_ROWS, SSM_N), zero_map3),
            pl.BlockSpec((2 * SSM_N, SSM_WIDTH), zero_map2),
            pl.BlockSpec((1, SSM_WIDTH), zero_map2),
            pl.BlockSpec((SSM_WIDTH, SSM_WIDTH), zero_map2)]


def _ssm_prompt(proj3d, sp, tb):
    b, l, _ = proj3d.shape
    nt = l // tb
    dummy_h0 = jnp.zeros((SCAN_ROWS, 2 * SSM_N), F32)
    kern = functools.partial(_ssm_kernel, chained=True, tiles_per_seq=nt, tb=tb)

    def kernel(u_ref, h0_ref, wb, tab, wc, d, wglu, y_ref, hn_ref, bu_scr, carry_scr):
        kern(u_ref.at[0], h0_ref, wb, tab, wc, d, wglu, y_ref, hn_ref, bu_scr, carry_scr)

    return pl.pallas_call(
        kernel,
        grid=(b, nt),
        in_specs=[pl.BlockSpec((1, tb, SSM_WIDTH), lambda bi, i: (bi, i, COL_SSM)),
                  pl.BlockSpec((SCAN_ROWS, 2 * SSM_N), lambda bi, i: (0, 0))]
                 + _ssm_common_specs(lambda bi, i: (0, 0), lambda bi, i: (0, 0, 0)),
        out_specs=[pl.BlockSpec((1, tb, SSM_WIDTH), lambda bi, i: (bi, i, 0)),
                   pl.BlockSpec((1, SCAN_ROWS, 2 * SSM_N), lambda bi, i: (bi, 0, 0))],
        out_shape=[jax.ShapeDtypeStruct((b, l, SSM_WIDTH), BF16),
                   jax.ShapeDtypeStruct((b, SCAN_ROWS, 2 * SSM_N), F32)],
        scratch_shapes=[pltpu.VMEM((tb, 2 * SSM_N), F32), pltpu.VMEM((SCAN_ROWS, 2 * SSM_N), F32)],
        compiler_params=_cparams(("parallel", "arbitrary")),
        name="ssm_prompt",
    )(proj3d, dummy_h0, sp["wb"], sp["tab"], sp["wc"], sp["d"], sp["wglu"])


def _ssm_sample(proj2d, h0, sp):
    rows = proj2d.shape[0]
    nseq = h0.shape[0]
    kern = functools.partial(_ssm_kernel, chained=False, tiles_per_seq=1, tb=rows)
    return pl.pallas_call(
        kern,
        grid=(1,),
        in_specs=[pl.BlockSpec((rows, SSM_WIDTH), lambda i: (0, COL_SSM)),
                  pl.BlockSpec((nseq, 2 * SSM_N), lambda i: (0, 0))]
                 + _ssm_common_specs(lambda i: (0, 0), lambda i: (0, 0, 0)),
        out_specs=[pl.BlockSpec((rows, SSM_WIDTH), lambda i: (0, 0)),
                   pl.BlockSpec((nseq, 2 * SSM_N), lambda i: (0, 0))],
        out_shape=[jax.ShapeDtypeStruct((rows, SSM_WIDTH), BF16),
                   jax.ShapeDtypeStruct((nseq, 2 * SSM_N), F32)],
        scratch_shapes=[pltpu.VMEM((rows, 2 * SSM_N), F32), pltpu.VMEM((SCAN_ROWS, 2 * SSM_N), F32)],
        compiler_params=_cparams(("arbitrary",)),
        name="ssm_sample",
    )(proj2d, h0, sp["wb"], sp["tab"], sp["wc"], sp["d"], sp["wglu"])


def _ssm_params(lam_re, lam_im, log_dt, b_re, b_im, c_re, c_im, d_skip, w_glu):
    lr, li = lam_re.astype(F32), lam_im.astype(F32)
    dt = jnp.exp(log_dt.astype(F32))[:, None]
    mag = jnp.exp(lr * dt)
    ab_re, ab_im = mag * jnp.cos(li * dt), mag * jnp.sin(li * dt)
    den = lr * lr + li * li
    nr = ab_re - 1.0
    f_re = (nr * lr + ab_im * li) / den
    f_im = (ab_im * lr - nr * li) / den
    br, bi = b_re.astype(F32), b_im.astype(F32)
    bb_re = f_re[..., None] * br - f_im[..., None] * bi
    bb_im = f_re[..., None] * bi + f_im[..., None] * br
    eye = jnp.eye(SSM_GROUPS, dtype=F32)

    def in_mat(bb):
        return jnp.einsum('gpc,gh->gchp', bb, eye).reshape(SSM_WIDTH, SSM_N)

    def out_mat(c):
        return jnp.einsum('gcp,gh->gphc', c.astype(F32), eye).reshape(SSM_N, SSM_WIDTH)

    wb = jnp.concatenate([in_mat(bb_re), in_mat(bb_im)], axis=1).astype(BF16)
    wc = jnp.concatenate([out_mat(c_re), -out_mat(c_im)], axis=0).astype(BF16)

    ar, ai = ab_re.reshape(1, SSM_N), ab_im.reshape(1, SSM_N)

    def cmul(x, y):
        return (x[0] * y[0] - x[1] * y[1], x[0] * y[1] + x[1] * y[0])

    pw = [(ar, ai)]
    for _ in range(SCAN_ROWS - 1):
        pw.append(cmul(pw[-1], (ar, ai)))
    row = jnp.arange(SCAN_ROWS)[:, None]
    tabs = []
    for shift in (1, 2, 4):
        for part in pw[shift - 1]:
            tabs.append(jnp.where(row >= shift, part, 0.0))
    tabs.append(jnp.concatenate([p[0] for p in pw], axis=0))
    tabs.append(jnp.concatenate([p[1] for p in pw], axis=0))
    tab = jnp.stack([jnp.broadcast_to(t, (SCAN_ROWS, SSM_N)) for t in tabs])
    return dict(wb=wb, tab=tab, wc=wc, d=d_skip.astype(F32).reshape(1, SSM_WIDTH), wglu=w_glu.astype(BF16))


_RET_G = 1.0 - np.exp2(-5.0 - np.arange(RET_HEADS, dtype=np.float64))


def _ret_consts(chunk, n_rows):
    idx = np.arange(n_rows)
    loc = idx % chunk
    same = (idx[:, None] // chunk) == (idx[None, :] // chunk)
    diff = loc[:, None] - loc[None, :]
    dec = np.where(same & (diff >= 0), _RET_G[:, None, None] ** np.maximum(diff, 0)[None], 0.0)
    qdec = np.repeat((_RET_G[None, :] ** (loc[:, None] + 1.0)), HEAD_DIM, axis=1)
    kdec = np.repeat((_RET_G[None, :] ** (chunk - 1.0 - loc[:, None])), HEAD_DIM, axis=1)
    return (jnp.asarray(dec, F32), jnp.asarray(qdec, F32), jnp.asarray(kdec, F32),
            jnp.asarray(_RET_G ** chunk, F32))


def _rope_tables(pos):
    half = HEAD_DIM // 2
    theta = 1.0 / (ROPE_BASE ** np.linspace(0.0, 1.0, half))
    ang = np.asarray(pos, np.float64)[:, None] * theta[None, :]
    cos = np.repeat(np.cos(ang), 2, axis=1)
    sin = np.repeat(np.sin(ang), 2, axis=1) * np.tile([-1.0, 1.0], half)[None]
    return (jnp.asarray(np.tile(cos, (1, RET_HEADS)), F32), jnp.asarray(np.tile(sin, (1, RET_HEADS)), F32))


def _rotate_pairs(x, cos, sin_signed):
    lane = lax.broadcasted_iota(jnp.int32, x.shape, 1)
    nxt = pltpu.roll(x, x.shape[1] - 1, 1)
    prv = pltpu.roll(x, 1, 1)
    return x * cos + jnp.where(lane % 2 == 0, nxt, prv) * sin_signed


def _ret_head_out(o, gate, norm):
    ms = jnp.mean(o * o, axis=-1, keepdims=True)
    return o * lax.rsqrt(ms + RMS_EPS) * norm * (gate * _sigmoid(gate))


RET_TILE = 512


def _ret_prompt_kernel(gc_ref, q_ref, k_ref, v_ref, g_ref, cos_ref, sin_ref, dec_ref, qdec_ref,
                       kdec_ref, norm_ref, mh_ref, y_ref, r_ref, o_scr):
    @pl.when(pl.program_id(1) == 0)
    def _():
        r_ref[...] = jnp.zeros_like(r_ref)

    cos, sin = cos_ref[...], sin_ref[...]
    q = _rotate_pairs(q_ref[0], cos, sin)
    k = _rotate_pairs(k_ref[0], cos, sin) * (HEAD_DIM ** -0.5)
    qb, kb, vb = q.astype(BF16), k.astype(BF16), v_ref[0].astype(BF16)
    kdb = (k * kdec_ref[...]).astype(BF16)
    for c in range(RET_TILE // RET_CHUNK):
        rows = slice(c * RET_CHUNK, (c + 1) * RET_CHUNK)
        for h in range(RET_HEADS):
            sl = slice(h * HEAD_DIM, (h + 1) * HEAD_DIM)
            qh, vh = qb[rows, sl], vb[rows, sl]
            s = _dot_nt(qh, kb[rows, sl]) * dec_ref[h]
            r = r_ref[0, h]
            o_scr[rows, sl] = _dot(s.astype(BF16), vh) + _dot(qh, r.astype(BF16)) * qdec_ref[rows, sl]
            r_ref[0, h] = gc_ref[h] * r + _dot_tn(kdb[rows, sl], vh)
    g = g_ref[0]
    y_ref[0] = (_head_rms(o_scr[...], mh_ref[...], norm_ref[...]) * (g * _sigmoid(g))).astype(y_ref.dtype)


def _ret_prompt(proj3d, cos, sin, norm):
    b, l, _ = proj3d.shape
    c = RET_CHUNK
    dec, _, _, gc = _ret_consts(c, c)
    _, qdec, kdec, _ = _ret_consts(c, RET_TILE)
    blk = (1, RET_TILE, MIX_BLOCK)
    tspec = pl.BlockSpec((RET_TILE, MIX_BLOCK), lambda bi, i: (i, 0))
    cspec = pl.BlockSpec((RET_TILE, MIX_BLOCK), lambda bi, i: (0, 0))
    return pl.pallas_call(
        _ret_prompt_kernel,
        grid=(b, l // RET_TILE),
        in_specs=[pl.BlockSpec(memory_space=pltpu.SMEM),
                  pl.BlockSpec(blk, lambda bi, i: (bi, i, COL_RET_Q)),
                  pl.BlockSpec(blk, lambda bi, i: (bi, i, COL_RET_K)),
                  pl.BlockSpec(blk, lambda bi, i: (bi, i, COL_RET_V)),
                  pl.BlockSpec(blk, lambda bi, i: (bi, i, COL_RET_G)),
                  tspec, tspec,
                  pl.BlockSpec((RET_HEADS, c, c), lambda bi, i: (0, 0, 0)),
                  cspec, cspec,
                  pl.BlockSpec((1, MIX_BLOCK), lambda bi, i: (0, 0)),
                  pl.BlockSpec((MIX_BLOCK, MIX_BLOCK), lambda bi, i: (0, 0))],
        out_specs=[pl.BlockSpec(blk, lambda bi, i: (bi, i, 0)),
                   pl.BlockSpec((1, RET_HEADS, HEAD_DIM, HEAD_DIM), lambda bi, i: (bi, 0, 0, 0))],
        out_shape=[jax.ShapeDtypeStruct((b, l, MIX_BLOCK), BF16),
                   jax.ShapeDtypeStruct((b, RET_HEADS, HEAD_DIM, HEAD_DIM), F32)],
        scratch_shapes=[pltpu.VMEM((RET_TILE, MIX_BLOCK), F32)],
        compiler_params=_cparams(("parallel", "arbitrary")),
        name="ret_prompt",
    )(gc, proj3d, proj3d, proj3d, proj3d, cos, sin, dec, qdec, kdec, norm, _head_mean_matrix(MIX_BLOCK))


def _ret_sample_kernel(gc_ref, q_ref, k_ref, v_ref, g_ref, cos_ref, sin_ref, dec_ref, qdec_ref,
                       kdec_ref, norm_ref, r0_ref, y_ref, rn_ref, *, n_new, s_blk):
    cos, sin = cos_ref[...], sin_ref[...]
    q = _rotate_pairs(q_ref[...], cos, sin)
    k = _rotate_pairs(k_ref[...], cos, sin) * (HEAD_DIM ** -0.5)
    v = v_ref[...]
    g = g_ref[...]
    kd = k * kdec_ref[...]
    qdec = qdec_ref[...]
    norm = norm_ref[...]
    rows = s_blk * n_new
    seq = lax.broadcasted_iota(jnp.int32, (rows, HEAD_DIM), 0) // n_new
    for h in range(RET_HEADS):
        sl = slice(h * HEAD_DIM, (h + 1) * HEAD_DIM)
        qf, kdf = q[:, sl], kd[:, sl]
        qh, kh, vh = qf.astype(BF16), k[:, sl].astype(BF16), v[:, sl].astype(BF16)
        s = _dot_nt(qh, kh) * dec_ref[h]
        cross = jnp.zeros((rows, HEAD_DIM), F32)
        for si in range(s_blk):
            mine = seq == si
            r = r0_ref[si, h]
            cross = cross + _dot(jnp.where(mine, qf, 0.0).astype(BF16), r.astype(BF16))
            rn_ref[si, h] = gc_ref[h] * r + _dot_tn(jnp.where(mine, kdf, 0.0).astype(BF16), vh)
        o = _dot(s.astype(BF16), vh) + cross * qdec[:, sl]
        y_ref[:, sl] = _ret_head_out(o, g[:, sl], norm[:, sl]).astype(y_ref.dtype)


def _ret_sample(proj2d, r0, cos, sin, norm, n_new, s_blk):
    rows = s_blk * n_new
    nseq = r0.shape[0]
    dec, qdec, kdec, gc = _ret_consts(n_new, rows)
    blk = (rows, MIX_BLOCK)
    cspec = pl.BlockSpec(blk, lambda i: (0, 0))
    rblk = (s_blk, RET_HEADS, HEAD_DIM, HEAD_DIM)
    return pl.pallas_call(
        functools.partial(_ret_sample_kernel, n_new=n_new, s_blk=s_blk),
        grid=(nseq // s_blk,),
        in_specs=[pl.BlockSpec(memory_space=pltpu.SMEM),
                  pl.BlockSpec(blk, lambda i: (i, COL_RET_Q)),
                  pl.BlockSpec(blk, lambda i: (i, COL_RET_K)),
                  pl.BlockSpec(blk, lambda i: (i, COL_RET_V)),
                  pl.BlockSpec(blk, lambda i: (i, COL_RET_G)),
                  cspec, cspec,
                  pl.BlockSpec((RET_HEADS, rows, rows), lambda i: (0, 0, 0)),
                  cspec, cspec,
                  pl.BlockSpec((1, MIX_BLOCK), lambda i: (0, 0)),
                  pl.BlockSpec(rblk, lambda i: (i, 0, 0, 0))],
        out_specs=[pl.BlockSpec(blk, lambda i: (i, 0)),
                   pl.BlockSpec(rblk, lambda i: (i, 0, 0, 0))],
        out_shape=[jax.ShapeDtypeStruct((nseq * n_new, MIX_BLOCK), BF16),
                   jax.ShapeDtypeStruct((nseq, RET_HEADS, HEAD_DIM, HEAD_DIM), F32)],
        compiler_params=_cparams(("parallel",)),
        name="ret_sample",
    )(gc, proj2d, proj2d, proj2d, proj2d, cos, sin, dec, qdec, kdec, norm, r0)


def _block_diag(w):
    g, n, _ = w.shape
    return jnp.einsum('gcd,gh->gchd', w, jnp.eye(g, dtype=w.dtype)).reshape(g * n, g * n)


def _layer_params(l, p):
    return dict(
        norm_mix=p['norm_mix'][l].reshape(1, D_MODEL),
        norm_ffn=p['norm_ffn'][l].reshape(1, D_MODEL),
        w_in=p['w_in'][l].astype(BF16),
        w_out=p['w_out'][l].astype(BF16),
        pool_w=_block_diag(p['pool_w'][l].astype(F32)).astype(BF16),
        pool_scale=p['pool_scale'][l].astype(F32).reshape(1, POOL_WIDTH),
        qg=p['swa_q_norm'][l].astype(F32).reshape(1, HEAD_DIM),
        kg=p['swa_k_norm'][l].astype(F32).reshape(1, HEAD_DIM),
        sinks=p['swa_sinks'][l].astype(F32),
        ssm=_ssm_params(p['ssm_lambda_re'][l], p['ssm_lambda_im'][l], p['ssm_log_dt'][l],
                        p['ssm_b_re'][l], p['ssm_b_im'][l], p['ssm_c_re'][l], p['ssm_c_im'][l],
                        p['ssm_d'][l], p['ssm_w_glu'][l]),
        ret_norm=p['ret_norm'][l].astype(F32).reshape(1, MIX_BLOCK),
    )


def _channel_mix(xs_list, l, lp, p):
    i = l // 2
    if l % 2 == 0:
        wg, wu, wd = (p[k][i].astype(BF16) for k in ('ffn_w_gate', 'ffn_w_up', 'ffn_w_down'))
        return [_ffn(x, lp['norm_ffn'], wg, wu, wd, 1024, 256) for x in xs_list]
    wr = jnp.pad(p['moe_router'][i].astype(F32), ((0, 0), (0, 128 - N_EXPERTS)))
    wg, wu, wd = (p[k][i].astype(BF16) for k in ('moe_w_gate', 'moe_w_up', 'moe_w_down'))
    return _moe(xs_list, lp['norm_ffn'], wr, wg, wu, wd)


def _mix_prompt(x2, b, l, lp, bias, cos, sin):
    proj2 = _norm_matmul(x2, lp['norm_mix'], lp['w_in'], 512)
    proj3 = proj2.reshape(b, l, IN_WIDTH)
    tb = 512
    y_pool = _pool(proj2, COL_POOL, lp['pool_w'], lp['pool_scale'], n_rows=b * l, tb=tb,
                   tiles_per_seq=l // tb, pos0=0)
    y_swa, kn = _swa_prompt(proj3, lp['sinks'], lp['qg'], lp['kg'], bias)
    y_ssm, hn = _ssm_prompt(proj3, lp['ssm'], 512)
    y_ret, rn = _ret_prompt(proj3, cos, sin, lp['ret_norm'])
    ys = (y_pool, y_swa.reshape(b * l, MIX_BLOCK), y_ssm.reshape(b * l, MIX_BLOCK),
          y_ret.reshape(b * l, MIX_BLOCK))
    x2 = _out_proj(x2, ys, lp['w_out'], 512)
    kw = SWA_KV_HEADS * HEAD_DIM
    hn = hn[:, 0]
    states = (proj3[:, l - POOL_BUF:, :POOL_WIDTH],
              kn.reshape(b, SWA_WINDOW, SWA_KV_HEADS, HEAD_DIM),
              proj3[:, l - SWA_WINDOW:, COL_SWA_KV * MIX_BLOCK + kw:(COL_SWA_KV + 1) * MIX_BLOCK]
              .reshape(b, SWA_WINDOW, SWA_KV_HEADS, HEAD_DIM),
              jnp.stack([hn[:, :SSM_N], hn[:, SSM_N:]], axis=-1).reshape(b, SSM_GROUPS, SSM_STATE, 2),
              rn)
    return x2, states


SAMPLE_SEQ_BLOCK = 16


def _mix_sample(x2, nseq, n_new, start_pos, lp, st, bias, cos, sin):
    state_pool, cache_k, cache_v, state_ssm, state_ret = st
    rows = nseq * n_new
    wb = cache_k.shape[1]
    kw = SWA_KV_HEADS * HEAD_DIM
    ext_rows = POOL_HALO + n_new
    proj2 = _norm_matmul(x2, lp['norm_mix'], lp['w_in'], 512)
    proj3 = proj2.reshape(nseq, n_new, IN_WIDTH)
    u_pool = proj3[:, :, :POOL_WIDTH]
    buf = state_pool.astype(F32)
    ext = jnp.concatenate([jnp.zeros((nseq, POOL_HALO - POOL_BUF, POOL_WIDTH), F32), buf, u_pool], axis=1)
    y_pool = _pool(ext.reshape(nseq * ext_rows, POOL_WIDTH), 0, lp['pool_w'], lp['pool_scale'],
                   n_rows=nseq * ext_rows, tb=nseq * ext_rows, tiles_per_seq=1, pos0=start_pos)
    y_pool = y_pool.reshape(nseq, ext_rows, POOL_WIDTH)[:, POOL_HALO:].reshape(rows, POOL_WIDTH)
    y_swa, nk, nv = _swa_sample(proj3, cache_k.reshape(nseq, wb, kw).astype(F32),
                                cache_v.reshape(nseq, wb, kw).astype(F32),
                                lp['sinks'], lp['qg'], lp['kg'], bias, SAMPLE_SEQ_BLOCK)
    h0 = state_ssm.astype(F32).reshape(nseq, SSM_N, 2)
    h0 = jnp.concatenate([h0[..., 0], h0[..., 1]], axis=1)
    y_ssm, hn = _ssm_sample(proj2, h0, lp['ssm'])
    y_ret, rn = _ret_sample(proj2, state_ret.astype(F32), cos, sin, lp['ret_norm'], n_new, SAMPLE_SEQ_BLOCK)
    ys = (y_pool, y_swa.reshape(rows, MIX_BLOCK), y_ssm, y_ret)
    x2 = _out_proj(x2, ys, lp['w_out'], 512)
    states = (jnp.concatenate([buf, u_pool], axis=1)[:, -POOL_BUF:],
              nk.reshape(nseq, SWA_WINDOW, SWA_KV_HEADS, HEAD_DIM),
              nv.reshape(nseq, SWA_WINDOW, SWA_KV_HEADS, HEAD_DIM),
              jnp.stack([hn[:, :SSM_N], hn[:, SSM_N:]], axis=-1).reshape(nseq, SSM_GROUPS, SSM_STATE, 2),
              rn)
    return x2, states


def _forward(x_prompt, x_sample, past_len, sample_state, p, rel_bias):
    b, l, d = x_prompt.shape
    nseq, n_new, _ = x_sample.shape
    wb = sample_state[1].shape[2]
    depth = p['norm_mix'].shape[0]
    bias_p = _swa_bias(rel_bias, np.arange(SWA_BLOCK)[:, None] - np.arange(2 * SWA_BLOCK)[None, :] + SWA_BLOCK)
    bias_s = _swa_bias(rel_bias, np.arange(n_new)[:, None] - np.arange(wb + n_new)[None, :] + wb)
    rope_p = _rope_tables(np.arange(l))
    rope_s = _rope_tables(past_len + (np.arange(SAMPLE_SEQ_BLOCK * n_new) % n_new))
    xp = x_prompt.reshape(b * l, d)
    xs = x_sample.reshape(nseq * n_new, d)
    st_p, st_s = [], []
    for li in range(depth):
        lp = _layer_params(li, p)
        xp, sp = _mix_prompt(xp, b, l, lp, bias_p, *rope_p)
        xs, ss = _mix_sample(xs, nseq, n_new, past_len, lp, [s[li] for s in sample_state], bias_s, *rope_s)
        xp, xs = _channel_mix([xp, xs], li, lp, p)
        st_p.append(sp)
        st_s.append(ss)
    outs = [xp.reshape(b, l, d), xs.reshape(nseq, n_new, d)]
    for k in range(5):
        outs.append(jnp.stack([s[k] for s in st_p]))
        outs.append(jnp.stack([s[k] for s in st_s]))
    return tuple(outs)


PAST_LEN = 16384


def kernel(x_prompt, x_sample, state_pool, cache_swa_k, cache_swa_v, state_ssm, state_ret,
           norm_mix, norm_ffn, w_in, w_out, pool_w, pool_scale, swa_q_norm, swa_k_norm, swa_sinks,
           rel_bias, ssm_lambda_re, ssm_lambda_im, ssm_log_dt, ssm_b_re, ssm_b_im, ssm_c_re, ssm_c_im,
           ssm_d, ssm_w_glu, ret_norm, ffn_w_gate, ffn_w_up, ffn_w_down, moe_router, moe_w_gate,
           moe_w_up, moe_w_down):
    p = dict(norm_mix=norm_mix, norm_ffn=norm_ffn, w_in=w_in, w_out=w_out, pool_w=pool_w,
             pool_scale=pool_scale, swa_q_norm=swa_q_norm, swa_k_norm=swa_k_norm, swa_sinks=swa_sinks,
             ssm_lambda_re=ssm_lambda_re, ssm_lambda_im=ssm_lambda_im, ssm_log_dt=ssm_log_dt,
             ssm_b_re=ssm_b_re, ssm_b_im=ssm_b_im, ssm_c_re=ssm_c_re, ssm_c_im=ssm_c_im,
             ssm_d=ssm_d, ssm_w_glu=ssm_w_glu, ret_norm=ret_norm,
             ffn_w_gate=ffn_w_gate, ffn_w_up=ffn_w_up, ffn_w_down=ffn_w_down, moe_router=moe_router,
             moe_w_gate=moe_w_gate, moe_w_up=moe_w_up, moe_w_down=moe_w_down)
    return _forward(x_prompt, x_sample, PAST_LEN,
                    (state_pool, cache_swa_k, cache_swa_v, state_ssm, state_ret), p, rel_bias)
```

```python
import functools
import math

import numpy as np
import jax
import jax.numpy as jnp
from jax import lax
from jax.experimental import pallas as pl
from jax.experimental.pallas import tpu as pltpu

F32 = jnp.float32
BF16 = jnp.bfloat16

D_MODEL = 1024
HEAD_DIM = 64
POOL_WIDTH = 256
POOL_WINDOWS = (2, 4, 8, 16)
POOL_BUF = 15
POOL_HALO = 16
SWA_HEADS = 4
SWA_KV_HEADS = 2
SWA_WINDOW = 128
SWA_BLOCK = 128
SSM_WIDTH = 256
SSM_CH = 16
SSM_GROUPS = 16
SSM_STATE = 64
SSM_N = SSM_GROUPS * SSM_STATE
RET_HEADS = 4
RET_CHUNK = 128
ROPE_BASE = 10000.0
IN_WIDTH = 2048
MIX_BLOCK = 256
D_FF = 2816
N_EXPERTS = 8
T5_BUCKETS = 32
T5_MAX_DIST = 128
RMS_EPS = 1e-6
NEG = -1e30
SCAN_ROWS = 8

COL_POOL, COL_SWA_Q, COL_SWA_KV, COL_SSM, COL_RET_Q, COL_RET_K, COL_RET_V, COL_RET_G = range(8)

VMEM_LIMIT = 48 * 1024 * 1024


def _cparams(sem):
    return pltpu.CompilerParams(dimension_semantics=sem, vmem_limit_bytes=VMEM_LIMIT)


def _rms(x, g):
    ms = jnp.mean(x * x, axis=-1, keepdims=True)
    return x * lax.rsqrt(ms + RMS_EPS) * g


def _dot(a, b):
    return jnp.dot(a, b, preferred_element_type=F32)


def _dot_nt(a, b):
    return lax.dot_general(a, b, (((1,), (1,)), ((), ())), preferred_element_type=F32)


def _dot_tn(a, b):
    return lax.dot_general(a, b, (((0,), (0,)), ((), ())), preferred_element_type=F32)


def _sigmoid(x):
    return 1.0 / (1.0 + jnp.exp(-x))


def _norm_matmul_kernel(x_ref, g_ref, w_ref, o_ref):
    h = _rms(x_ref[...], g_ref[...]).astype(BF16)
    o_ref[...] = _dot(h, w_ref[...])


def _norm_matmul(x, g, w, tm):
    t, d = x.shape
    tm = min(tm, t)
    n = w.shape[1]
    return pl.pallas_call(
        _norm_matmul_kernel,
        grid=(t // tm,),
        in_specs=[pl.BlockSpec((tm, d), lambda i: (i, 0)),
                  pl.BlockSpec((1, d), lambda i: (0, 0)),
                  pl.BlockSpec((d, n), lambda i: (0, 0))],
        out_specs=pl.BlockSpec((tm, n), lambda i: (i, 0)),
        out_shape=jax.ShapeDtypeStruct((t, n), F32),
        compiler_params=_cparams(("parallel",)),
        name="norm_matmul",
    )(x, g, w)


def _out_proj_kernel(x_ref, y0_ref, y1_ref, y2_ref, y3_ref, w_ref, o_ref):
    acc = x_ref[...]
    for k, y_ref in enumerate((y0_ref, y1_ref, y2_ref, y3_ref)):
        acc = acc + _dot(y_ref[...], w_ref[k * MIX_BLOCK:(k + 1) * MIX_BLOCK, :])
    o_ref[...] = acc


def _out_proj(x, ys, w, tm):
    t, d = x.shape
    tm = min(tm, t)
    yspec = pl.BlockSpec((tm, MIX_BLOCK), lambda i: (i, 0))
    return pl.pallas_call(
        _out_proj_kernel,
        grid=(t // tm,),
        in_specs=[pl.BlockSpec((tm, d), lambda i: (i, 0)), yspec, yspec, yspec, yspec,
                  pl.BlockSpec((d, d), lambda i: (0, 0))],
        out_specs=pl.BlockSpec((tm, d), lambda i: (i, 0)),
        out_shape=jax.ShapeDtypeStruct((t, d), F32),
        compiler_params=_cparams(("parallel",)),
        name="out_proj",
    )(x, *ys, w)


def _swiglu_chunk(h, wg, wu, wd):
    a = _dot(h, wg)
    b = _dot(h, wu)
    return _dot((a * _sigmoid(a) * b).astype(BF16), wd)


def _ffn_kernel(x_ref, g_ref, wg_ref, wu_ref, wd_ref, o_ref, h_scr):
    @pl.when(pl.program_id(1) == 0)
    def _():
        x = x_ref[...]
        h_scr[...] = _rms(x, g_ref[...]).astype(BF16)
        o_ref[...] = x

    o_ref[...] += _swiglu_chunk(h_scr[...], wg_ref[...], wu_ref[...], wd_ref[...])


def _ffn(x, g, wg, wu, wd, tm, tf):
    t, d = x.shape
    tm = min(tm, t)
    f = wg.shape[1]
    return pl.pallas_call(
        _ffn_kernel,
        grid=(t // tm, f // tf),
        in_specs=[pl.BlockSpec((tm, d), lambda i, j: (i, 0)),
                  pl.BlockSpec((1, d), lambda i, j: (0, 0)),
                  pl.BlockSpec((d, tf), lambda i, j: (0, j)),
                  pl.BlockSpec((d, tf), lambda i, j: (0, j)),
                  pl.BlockSpec((tf, d), lambda i, j: (j, 0))],
        out_specs=pl.BlockSpec((tm, d), lambda i, j: (i, 0)),
        out_shape=jax.ShapeDtypeStruct((t, d), F32),
        scratch_shapes=[pltpu.VMEM((tm, d), BF16)],
        compiler_params=_cparams(("parallel", "arbitrary")),
        name="ffn",
    )(x, g, wg, wu, wd)


ROUTE_ID_LANES = (0, 1)
ROUTE_GATE_LANES = (2, 3)


def _router_kernel(x_ref, g_ref, wr_ref, c_ref):
    h = _rms(x_ref[...], g_ref[...])
    logits = jnp.dot(h, wr_ref[...], preferred_element_type=F32, precision=lax.Precision.HIGHEST)
    lane = lax.broadcasted_iota(jnp.int32, logits.shape, 1).astype(F32)
    lg = jnp.where(lane < N_EXPERTS, logits, NEG)
    m1 = jnp.max(lg, axis=-1, keepdims=True)
    i1 = jnp.min(jnp.where(lg == m1, lane, 128.0), axis=-1, keepdims=True)
    lg2 = jnp.where(lane == i1, NEG, lg)
    m2 = jnp.max(lg2, axis=-1, keepdims=True)
    i2 = jnp.min(jnp.where(lg2 == m2, lane, 128.0), axis=-1, keepdims=True)
    ex = jnp.exp(m2 - m1)
    vals = (i1, i2, 1.0 / (1.0 + ex), ex / (1.0 + ex))
    out = jnp.zeros_like(logits)
    for ln, v in zip(ROUTE_ID_LANES + ROUTE_GATE_LANES, vals):
        out = jnp.where(lane == ln, v, out)
    c_ref[...] = out


def _router(x, g, wr, tm):
    t, d = x.shape
    tm = min(tm, t)
    return pl.pallas_call(
        _router_kernel,
        grid=(t // tm,),
        in_specs=[pl.BlockSpec((tm, d), lambda i: (i, 0)),
                  pl.BlockSpec((1, d), lambda i: (0, 0)),
                  pl.BlockSpec((d, 128), lambda i: (0, 0))],
        out_specs=pl.BlockSpec((tm, 128), lambda i: (i, 0)),
        out_shape=jax.ShapeDtypeStruct((t, 128), F32),
        compiler_params=_cparams(("parallel",)),
        name="router",
    )(x, g, wr)


DMA_ISSUE_UNROLL = 8


def _row_copy(src, i, dst, j, sem):
    return pltpu.make_async_copy(src.at[pl.ds(i, 1)], dst.at[pl.ds(j, 1)], sem)


def _dispatch_kernel(pos_ref, x_ref, xs_in, xs_hbm, sem, *, td):
    del xs_in

    def issue(j, c):
        for k in range(2):
            _row_copy(x_ref, j, xs_hbm, pos_ref[0, 0, 2 * j + k], sem).start()
        return c

    lax.fori_loop(0, td, issue, 0, unroll=DMA_ISSUE_UNROLL)
    for _ in range(2):
        pltpu.make_async_copy(x_ref, xs_hbm.at[pl.ds(0, td)], sem).wait()


def _dispatch(x, pos, xs, td):
    t, d = x.shape
    td = min(td, t)
    pos3 = pos.reshape(t // td, 1, 2 * td)
    return pl.pallas_call(
        functools.partial(_dispatch_kernel, td=td),
        grid=(t // td,),
        in_specs=[pl.BlockSpec((1, 1, 2 * td), lambda i: (i, 0, 0), memory_space=pltpu.SMEM),
                  pl.BlockSpec((td, d), lambda i: (i, 0)),
                  pl.BlockSpec(memory_space=pl.ANY)],
        out_specs=pl.BlockSpec(memory_space=pl.ANY),
        out_shape=jax.ShapeDtypeStruct(xs.shape, xs.dtype),
        scratch_shapes=[pltpu.SemaphoreType.DMA],
        input_output_aliases={2: 0},
        compiler_params=_cparams(("arbitrary",)),
        name="moe_dispatch",
    )(pos3, x, xs)


def _grouped_ffn_kernel(te_ref, nu_ref, x_ref, g_ref, wg_ref, wu_ref, wd_ref, o_ref, h_scr):
    del te_ref
    j = pl.program_id(1)
    used = pl.program_id(0) < nu_ref[0]

    @pl.when(jnp.logical_not(used) & (j == 0))
    def _():
        o_ref[...] = jnp.zeros_like(o_ref)

    @pl.when(used)
    def _():
        @pl.when(j == 0)
        def _():
            h_scr[...] = _rms(x_ref[...], g_ref[...]).astype(BF16)

        y = _swiglu_chunk(h_scr[...], wg_ref[0], wu_ref[0], wd_ref[0])

        @pl.when(j == 0)
        def _():
            o_ref[...] = y

        @pl.when(j > 0)
        def _():
            o_ref[...] += y


def _grouped_ffn(xs, g, tile_expert, n_used, wg, wu, wd, tm, tf):
    r, d = xs.shape
    f = wg.shape[2]
    nj = f // tf

    def row_map(i, j, te, nu):
        return (i, 0)

    def col_of(i, j, nu):
        return jnp.where(i < nu[0], j, nj - 1)

    grid_spec = pltpu.PrefetchScalarGridSpec(
        num_scalar_prefetch=2,
        grid=(r // tm, nj),
        in_specs=[pl.BlockSpec((tm, d), row_map),
                  pl.BlockSpec((1, d), lambda i, j, te, nu: (0, 0)),
                  pl.BlockSpec((1, d, tf), lambda i, j, te, nu: (te[i], 0, col_of(i, j, nu))),
                  pl.BlockSpec((1, d, tf), lambda i, j, te, nu: (te[i], 0, col_of(i, j, nu))),
                  pl.BlockSpec((1, tf, d), lambda i, j, te, nu: (te[i], col_of(i, j, nu), 0))],
        out_specs=pl.BlockSpec((tm, d), row_map),
        scratch_shapes=[pltpu.VMEM((tm, d), BF16)],
    )
    return pl.pallas_call(
        _grouped_ffn_kernel,
        grid_spec=grid_spec,
        out_shape=jax.ShapeDtypeStruct((r, d), F32),
        compiler_params=_cparams(("arbitrary", "arbitrary")),
        name="moe_grouped_ffn",
    )(tile_expert, n_used, xs, g, wg, wu, wd)


def _combine_kernel(pos_ref, x_ref, route_ref, ys_hbm, o_ref, buf0, buf1, sem, *, tc):
    def issue(j, c):
        _row_copy(ys_hbm, pos_ref[0, 0, 2 * j], buf0, j, sem).start()
        _row_copy(ys_hbm, pos_ref[0, 0, 2 * j + 1], buf1, j, sem).start()
        return c

    lax.fori_loop(0, tc, issue, 0, unroll=DMA_ISSUE_UNROLL)
    for buf in (buf0, buf1):
        pltpu.make_async_copy(ys_hbm.at[pl.ds(0, tc)], buf, sem).wait()
    route = route_ref[...]
    g0 = route[:, ROUTE_GATE_LANES[0]:ROUTE_GATE_LANES[0] + 1]
    g1 = route[:, ROUTE_GATE_LANES[1]:ROUTE_GATE_LANES[1] + 1]
    o_ref[...] = x_ref[...] + g0 * buf0[...] + g1 * buf1[...]


def _combine(x, route, pos, ys, tc):
    t, d = x.shape
    tc = min(tc, t)
    pos3 = pos.reshape(t // tc, 1, 2 * tc)
    return pl.pallas_call(
        functools.partial(_combine_kernel, tc=tc),
        grid=(t // tc,),
        in_specs=[pl.BlockSpec((1, 1, 2 * tc), lambda i: (i, 0, 0), memory_space=pltpu.SMEM),
                  pl.BlockSpec((tc, d), lambda i: (i, 0)),
                  pl.BlockSpec((tc, 128), lambda i: (i, 0)),
                  pl.BlockSpec(memory_space=pl.ANY)],
        out_specs=pl.BlockSpec((tc, d), lambda i: (i, 0)),
        out_shape=jax.ShapeDtypeStruct((t, d), F32),
        scratch_shapes=[pltpu.VMEM((tc, d), F32), pltpu.VMEM((tc, d), F32), pltpu.SemaphoreType.DMA],
        compiler_params=_cparams(("arbitrary",)),
        name="moe_combine",
    )(pos3, x, route, ys)


MOE_TM = 512


def _route_plan(expert_ids, tm):
    flat = expert_ids.reshape(-1)
    a = flat.shape[0]
    onehot = (flat[None, :] == jnp.arange(N_EXPERTS, dtype=jnp.int32)[:, None]).astype(jnp.int32)
    csum = jnp.cumsum(onehot, axis=1)
    counts = csum[:, -1]
    padded = (counts + tm - 1) // tm * tm
    ends = jnp.cumsum(padded)
    offs = ends - padded
    pos = jnp.sum(onehot * (offs[:, None] + csum - 1), axis=0)
    n_tiles = (a + N_EXPERTS * tm) // tm
    tile_start = jnp.arange(n_tiles, dtype=jnp.int32) * tm
    tile_expert = jnp.minimum(jnp.sum(tile_start[:, None] >= ends[None, :], axis=1), N_EXPERTS - 1)
    n_used = (ends[-1] // tm).reshape(1)
    last = jnp.take(tile_expert, n_used[0] - 1)
    tile_expert = jnp.where(tile_start < ends[-1], tile_expert, last)
    return pos.astype(jnp.int32), tile_expert.astype(jnp.int32), n_used.astype(jnp.int32), n_tiles * tm


def _moe(xs_list, g, wr, wg, wu, wd):
    routes = [_router(x, g, wr, 512) for x in xs_list]
    ids = jnp.concatenate([r[:, ROUTE_ID_LANES[0]:ROUTE_ID_LANES[1] + 1] for r in routes]).astype(jnp.int32)
    pos, tile_expert, n_used, n_rows = _route_plan(ids, MOE_TM)
    xs = jnp.zeros((n_rows, D_MODEL), F32)
    bounds = np.cumsum([0] + [2 * x.shape[0] for x in xs_list])
    pos_list = [pos[lo:hi] for lo, hi in zip(bounds[:-1], bounds[1:])]
    for x, ps in zip(xs_list, pos_list):
        xs = _dispatch(x, ps, xs, 1024)
    ys = _grouped_ffn(xs, g, tile_expert, n_used, wg, wu, wd, MOE_TM, D_FF // 2)
    return [_combine(x, r, ps, ys, 512) for x, r, ps in zip(xs_list, routes, pos_list)]


def _pool_kernel(u_ref, halo_ref, w_ref, scale_ref, o_ref, *, tiles_per_seq, pos0, tb):
    ti = pl.program_id(0) % tiles_per_seq
    u = u_ref[...]
    halo = jnp.where(ti == 0, 0.0, halo_ref[...])
    ext = jnp.concatenate([halo, u], axis=0)
    s2 = ext + pltpu.roll(ext, 1, 0)
    s4 = s2 + pltpu.roll(s2, 2, 0)
    s8 = s4 + pltpu.roll(s4, 4, 0)
    s16 = s8 + pltpu.roll(s8, 8, 0)
    grp = lax.broadcasted_iota(jnp.int32, (tb, POOL_WIDTH), 1) // (POOL_WIDTH // 4)
    row = lax.broadcasted_iota(jnp.int32, (tb, POOL_WIDTH), 0)
    s = jnp.where(grp == 0, s2[POOL_HALO:],
                  jnp.where(grp == 1, s4[POOL_HALO:],
                            jnp.where(grp == 2, s8[POOL_HALO:], s16[POOL_HALO:])))
    win = jnp.where(grp == 0, 2, jnp.where(grp == 1, 4, jnp.where(grp == 2, 8, 16)))
    cnt = jnp.minimum(win, pos0 + ti * tb + row + 1).astype(F32)
    pooled = s / cnt - u
    o_ref[...] = (_dot(pooled.astype(BF16), w_ref[...]) * scale_ref[...]).astype(o_ref.dtype)


def _pool(proj2d, col, w, scale, *, n_rows, tb, tiles_per_seq, pos0):
    per = tb // POOL_HALO
    return pl.pallas_call(
        functools.partial(_pool_kernel, tiles_per_seq=tiles_per_seq, pos0=pos0, tb=tb),
        grid=(n_rows // tb,),
        in_specs=[pl.BlockSpec((tb, POOL_WIDTH), lambda i: (i, col)),
                  pl.BlockSpec((POOL_HALO, POOL_WIDTH), lambda i: (jnp.maximum(i * per - 1, 0), col)),
                  pl.BlockSpec((POOL_WIDTH, POOL_WIDTH), lambda i: (0, 0)),
                  pl.BlockSpec((1, POOL_WIDTH), lambda i: (0, 0))],
        out_specs=pl.BlockSpec((tb, POOL_WIDTH), lambda i: (i, 0)),
        out_shape=jax.ShapeDtypeStruct((n_rows, POOL_WIDTH), BF16),
        compiler_params=_cparams(("parallel",)),
        name="pool",
    )(proj2d, proj2d, w, scale)


def _t5_bucket_np(rel):
    n = np.maximum(rel, 0)
    max_exact = T5_BUCKETS // 2
    nf = np.maximum(n, max_exact).astype(np.float32)
    large = max_exact + (np.log(nf / max_exact) / math.log(T5_MAX_DIST / max_exact)
                         * (T5_BUCKETS - max_exact)).astype(np.int32)
    large = np.minimum(large, T5_BUCKETS - 1)
    return np.where(n < max_exact, n, large)


def _swa_bias(rel_bias, rel):
    valid = (rel >= 0) & (rel < SWA_WINDOW)
    onehot = jnp.asarray(_t5_bucket_np(rel)[..., None] == np.arange(T5_BUCKETS), F32)
    b = jnp.einsum('qsb,bh->hqs', onehot, rel_bias.astype(F32), precision=lax.Precision.HIGHEST)
    return jnp.where(valid[None], b, NEG)


def _softmax_parts(parts, sink):
    m = sink
    for s in parts:
        m = jnp.maximum(m, jnp.max(s, axis=-1, keepdims=True))
    ps = [jnp.exp(s - m) for s in parts]
    denom = jnp.exp(sink - m)
    for p in ps:
        denom = denom + jnp.sum(p, axis=-1, keepdims=True)
    return ps, denom


def _head_mean_matrix(width):
    h = np.arange(width) // HEAD_DIM
    return jnp.asarray((h[:, None] == h[None, :]) / HEAD_DIM, BF16)


def _head_rms(x, mean_mat, g):
    ms = _dot((x * x).astype(BF16), mean_mat)
    return x * lax.rsqrt(ms + RMS_EPS) * g


SWA_TILE = 512


def _swa_prompt_kernel(sinks_ref, q_ref, kv_ref, halo_ref, qg_ref, kg_ref, mq_ref, mk_ref, bias_ref,
                       y_ref, kn_ref):
    has_prev = pl.program_id(1) > 0
    kw = SWA_KV_HEADS * HEAD_DIM
    blk = SWA_BLOCK
    kv = kv_ref[0]
    halo = halo_ref[0]
    k_ext = jnp.concatenate([halo[:, :kw], kv[:, :kw]], axis=0)
    v_ext = jnp.concatenate([halo[:, kw:], kv[:, kw:]], axis=0).astype(BF16)
    kn = _head_rms(k_ext, mk_ref[...], kg_ref[...])
    kn_ref[0] = kn[SWA_TILE:]
    knb = kn.astype(BF16)
    qn = (_head_rms(q_ref[0], mq_ref[...], qg_ref[...]) * (HEAD_DIM ** -0.5)).astype(BF16)
    row = lax.broadcasted_iota(jnp.int32, (2 * blk, 1), 0)
    col = lax.broadcasted_iota(jnp.int32, (2 * blk, 2 * blk), 1)
    for c in range(SWA_TILE // blk):
        rows = slice(c * blk, (c + 1) * blk)
        keys = slice(c * blk, (c + 2) * blk)
        for kh in range(SWA_KV_HEADS):
            ksl = slice(kh * HEAD_DIM, (kh + 1) * HEAD_DIM)
            h0 = 2 * kh
            q2 = jnp.concatenate([qn[rows, h0 * HEAD_DIM:(h0 + 1) * HEAD_DIM],
                                  qn[rows, (h0 + 1) * HEAD_DIM:(h0 + 2) * HEAD_DIM]], axis=0)
            s = _dot_nt(q2, knb[keys, ksl]) + bias_ref[kh]
            if c == 0:
                s = jnp.where(has_prev | (col >= blk), s, NEG)
            sink = jnp.where(row < blk, sinks_ref[h0], sinks_ref[h0 + 1])
            (p,), denom = _softmax_parts((s,), sink)
            o = _dot(p.astype(BF16), v_ext[keys, ksl]) / denom
            y_ref[0, rows, h0 * HEAD_DIM:(h0 + 1) * HEAD_DIM] = o[:blk].astype(y_ref.dtype)
            y_ref[0, rows, (h0 + 1) * HEAD_DIM:(h0 + 2) * HEAD_DIM] = o[blk:].astype(y_ref.dtype)


def _swa_prompt(proj3d, sinks, qg, kg, bias):
    b, l, _ = proj3d.shape
    kw = SWA_KV_HEADS * HEAD_DIM
    per = SWA_TILE // SWA_BLOCK
    tile = (1, SWA_TILE, MIX_BLOCK)
    const2 = lambda bi, i: (0, 0)
    bias2 = bias.reshape(SWA_KV_HEADS, 2 * SWA_BLOCK, 2 * SWA_BLOCK)
    return pl.pallas_call(
        _swa_prompt_kernel,
        grid=(b, l // SWA_TILE),
        in_specs=[pl.BlockSpec(memory_space=pltpu.SMEM),
                  pl.BlockSpec(tile, lambda bi, i: (bi, i, COL_SWA_Q)),
                  pl.BlockSpec(tile, lambda bi, i: (bi, i, COL_SWA_KV)),
                  pl.BlockSpec((1, SWA_BLOCK, MIX_BLOCK),
                               lambda bi, i: (bi, jnp.maximum(i * per - 1, 0), COL_SWA_KV)),
                  pl.BlockSpec((1, MIX_BLOCK), const2),
                  pl.BlockSpec((1, kw), const2),
                  pl.BlockSpec((MIX_BLOCK, MIX_BLOCK), const2),
                  pl.BlockSpec((kw, kw), const2),
                  pl.BlockSpec((SWA_KV_HEADS, 2 * SWA_BLOCK, 2 * SWA_BLOCK), lambda bi, i: (0, 0, 0))],
        out_specs=[pl.BlockSpec(tile, lambda bi, i: (bi, i, 0)),
                   pl.BlockSpec((1, SWA_BLOCK, kw), lambda bi, i: (bi, 0, 0))],
        out_shape=[jax.ShapeDtypeStruct((b, l, MIX_BLOCK), BF16),
                   jax.ShapeDtypeStruct((b, SWA_BLOCK, kw), F32)],
        compiler_params=_cparams(("parallel", "arbitrary")),
        name="swa_prompt",
    )(sinks, proj3d, proj3d, proj3d, jnp.tile(qg, (1, SWA_HEADS)), jnp.tile(kg, (1, SWA_KV_HEADS)),
      _head_mean_matrix(MIX_BLOCK), _head_mean_matrix(kw), bias2)


def _swa_sample_kernel(sinks_ref, q_ref, kv_ref, ck_ref, cv_ref, qg_ref, kg_ref, bias_ref,
                       y_ref, nk_ref, nv_ref, *, n_new):
    q = q_ref[...]
    kv = kv_ref[...]
    ck = ck_ref[...]
    cv = cv_ref[...]
    qg = qg_ref[...]
    kg = kg_ref[...]
    kw = SWA_KV_HEADS * HEAD_DIM
    bdot = functools.partial(jnp.einsum, preferred_element_type=F32)
    for kh in range(SWA_KV_HEADS):
        ksl = slice(kh * HEAD_DIM, (kh + 1) * HEAD_DIM)
        vsl = slice(kw + kh * HEAD_DIM, kw + (kh + 1) * HEAD_DIM)
        kn = _rms(kv[:, :, ksl], kg)
        vn = kv[:, :, vsl]
        nk_ref[:, :SWA_WINDOW - n_new, ksl] = ck[:, n_new:, ksl]
        nk_ref[:, SWA_WINDOW - n_new:, ksl] = kn
        nv_ref[:, :SWA_WINDOW - n_new, ksl] = cv[:, n_new:, ksl]
        nv_ref[:, SWA_WINDOW - n_new:, ksl] = vn
        for gq in range(SWA_HEADS // SWA_KV_HEADS):
            h = kh * (SWA_HEADS // SWA_KV_HEADS) + gq
            hsl = slice(h * HEAD_DIM, (h + 1) * HEAD_DIM)
            qn = _rms(q[:, :, hsl], qg)
            s_c = bdot('sqd,skd->sqk', qn, ck[:, :, ksl]) * (HEAD_DIM ** -0.5) + bias_ref[h, :, :SWA_WINDOW]
            s_n = bdot('sqd,skd->sqk', qn, kn) * (HEAD_DIM ** -0.5) + bias_ref[h, :, SWA_WINDOW:]
            (p_c, p_n), denom = _softmax_parts((s_c, s_n), sinks_ref[h])
            o = bdot('sqk,skd->sqd', p_c, cv[:, :, ksl]) + bdot('sqk,skd->sqd', p_n, vn)
            y_ref[:, :, hsl] = (o / denom).astype(y_ref.dtype)


def _swa_sample(proj3d, cache_k, cache_v, sinks, qg, kg, bias, s_blk):
    nseq, n_new, _ = proj3d.shape
    kw = SWA_KV_HEADS * HEAD_DIM
    blk = (s_blk, n_new, MIX_BLOCK)
    cblk = (s_blk, SWA_WINDOW, kw)
    return pl.pallas_call(
        functools.partial(_swa_sample_kernel, n_new=n_new),
        grid=(nseq // s_blk,),
        in_specs=[pl.BlockSpec(memory_space=pltpu.SMEM),
                  pl.BlockSpec(blk, lambda i: (i, 0, COL_SWA_Q)),
                  pl.BlockSpec(blk, lambda i: (i, 0, COL_SWA_KV)),
                  pl.BlockSpec(cblk, lambda i: (i, 0, 0)),
                  pl.BlockSpec(cblk, lambda i: (i, 0, 0)),
                  pl.BlockSpec((1, HEAD_DIM), lambda i: (0, 0)),
                  pl.BlockSpec((1, HEAD_DIM), lambda i: (0, 0)),
                  pl.BlockSpec((SWA_HEADS, n_new, SWA_WINDOW + n_new), lambda i: (0, 0, 0))],
        out_specs=[pl.BlockSpec(blk, lambda i: (i, 0, 0)),
                   pl.BlockSpec(cblk, lambda i: (i, 0, 0)),
                   pl.BlockSpec(cblk, lambda i: (i, 0, 0))],
        out_shape=[jax.ShapeDtypeStruct((nseq, n_new, MIX_BLOCK), BF16),
                   jax.ShapeDtypeStruct((nseq, SWA_WINDOW, kw), F32),
                   jax.ShapeDtypeStruct((nseq, SWA_WINDOW, kw), F32)],
        compiler_params=_cparams(("parallel",)),
        name="swa_sample",
    )(sinks, proj3d, proj3d, cache_k, cache_v, qg, kg, bias)


def _ssm_kernel(u_ref, h0_ref, wb_ref, tab_ref, wc_ref, d_ref, wglu_ref, y_ref, hn_ref,
                bu_scr, carry_scr, *, chained, tiles_per_seq, tb):
    n = SSM_N
    u = u_ref[...]
    bu_scr[...] = _dot(u.astype(BF16), wb_ref[...])

    if chained:
        @pl.when(pl.program_id(1) % tiles_per_seq == 0)
        def _():
            carry_scr[...] = jnp.zeros_like(carry_scr)

    def tile_scan(r0, cr, ci):
        hr = bu_scr[pl.ds(r0, SCAN_ROWS), :n]
        hi = bu_scr[pl.ds(r0, SCAN_ROWS), n:]
        for k, shift in enumerate((1, 2, 4)):
            ar, ai = tab_ref[2 * k], tab_ref[2 * k + 1]
            sr, si = pltpu.roll(hr, shift, 0), pltpu.roll(hi, shift, 0)
            hr, hi = hr + ar * sr - ai * si, hi + ar * si + ai * sr
        pr, pi = tab_ref[6], tab_ref[7]
        hr, hi = hr + pr * cr - pi * ci, hi + pr * ci + pi * cr
        bu_scr[pl.ds(r0, SCAN_ROWS), :n] = hr
        bu_scr[pl.ds(r0, SCAN_ROWS), n:] = hi
        return hr[SCAN_ROWS - 1:], hi[SCAN_ROWS - 1:]

    if chained:
        def body(t, carry):
            r0 = pl.multiple_of(t * SCAN_ROWS, SCAN_ROWS)
            lr, li = tile_scan(r0, *carry)
            return (jnp.broadcast_to(lr, (SCAN_ROWS, n)), jnp.broadcast_to(li, (SCAN_ROWS, n)))

        cr, ci = lax.fori_loop(0, tb // SCAN_ROWS, body, (carry_scr[:, :n], carry_scr[:, n:]), unroll=2)
        carry_scr[:, :n] = cr
        carry_scr[:, n:] = ci
        hn_ref[0, :, :n] = cr
        hn_ref[0, :, n:] = ci
    else:
        def body(t, _):
            r0 = pl.multiple_of(t * SCAN_ROWS, SCAN_ROWS)
            h0 = h0_ref[pl.ds(t, 1), :]
            cr = jnp.broadcast_to(h0[:, :n], (SCAN_ROWS, n))
            ci = jnp.broadcast_to(h0[:, n:], (SCAN_ROWS, n))
            lr, li = tile_scan(r0, cr, ci)
            hn_ref[pl.ds(t, 1), :n] = lr
            hn_ref[pl.ds(t, 1), n:] = li
            return 0

        lax.fori_loop(0, tb // SCAN_ROWS, body, 0)

    y = _dot(bu_scr[...].astype(BF16), wc_ref[...]) + d_ref[...] * u
    y = 0.5 * y * (1.0 + jnp.tanh(math.sqrt(2.0 / math.pi) * (y + 0.044715 * (y * y * y))))
    y = y * _sigmoid(_dot(y.astype(BF16), wglu_ref[...]))
    if chained:
        y_ref[0] = y.astype(y_ref.dtype)
    else:
        y_ref[...] = y.astype(y_ref.dtype)


def _ssm_common_specs(zero_map2, zero_map3):
    return [pl.BlockSpec((SSM_WIDTH, 2 * SSM_N), zero_map2),
            pl.BlockSpec((8, SCAN_ROWS, SSM_N), zero_map3),
            pl.BlockSpec((2 * SSM_N, SSM_WIDTH), zero_map2),
            pl.BlockSpec((1, SSM_WIDTH), zero_map2),
            pl.BlockSpec((SSM_WIDTH, SSM_WIDTH), zero_map2)]


def _ssm_prompt(proj3d, sp, tb):
    b, l, _ = proj3d.shape
    nt = l // tb
    dummy_h0 = jnp.zeros((SCAN_ROWS, 2 * SSM_N), F32)
    kern = functools.partial(_ssm_kernel, chained=True, tiles_per_seq=nt, tb=tb)

    def kernel(u_ref, h0_ref, wb, tab, wc, d, wglu, y_ref, hn_ref, bu_scr, carry_scr):
        kern(u_ref.at[0], h0_ref, wb, tab, wc, d, wglu, y_ref, hn_ref, bu_scr, carry_scr)

    return pl.pallas_call(
        kernel,
        grid=(b, nt),
        in_specs=[pl.BlockSpec((1, tb, SSM_WIDTH), lambda bi, i: (bi, i, COL_SSM)),
                  pl.BlockSpec((SCAN_ROWS, 2 * SSM_N), lambda bi, i: (0, 0))]
                 + _ssm_common_specs(lambda bi, i: (0, 0), lambda bi, i: (0, 0, 0)),
        out_specs=[pl.BlockSpec((1, tb, SSM_WIDTH), lambda bi, i: (bi, i, 0)),
                   pl.BlockSpec((1, SCAN_ROWS, 2 * SSM_N), lambda bi, i: (bi, 0, 0))],
        out_shape=[jax.ShapeDtypeStruct((b, l, SSM_WIDTH), BF16),
                   jax.ShapeDtypeStruct((b, SCAN_ROWS, 2 * SSM_N), F32)],
        scratch_shapes=[pltpu.VMEM((tb, 2 * SSM_N), F32), pltpu.VMEM((SCAN_ROWS, 2 * SSM_N), F32)],
        compiler_params=_cparams(("parallel", "arbitrary")),
        name="ssm_prompt",
    )(proj3d, dummy_h0, sp["wb"], sp["tab"], sp["wc"], sp["d"], sp["wglu"])


def _ssm_sample(proj2d, h0, sp):
    rows = proj2d.shape[0]
    nseq = h0.shape[0]
    kern = functools.partial(_ssm_kernel, chained=False, tiles_per_seq=1, tb=rows)
    return pl.pallas_call(
        kern,
        grid=(1,),
        in_specs=[pl.BlockSpec((rows, SSM_WIDTH), lambda i: (0, COL_SSM)),
                  pl.BlockSpec((nseq, 2 * SSM_N), lambda i: (0, 0))]
                 + _ssm_common_specs(lambda i: (0, 0), lambda i: (0, 0, 0)),
        out_specs=[pl.BlockSpec((rows, SSM_WIDTH), lambda i: (0, 0)),
                   pl.BlockSpec((nseq, 2 * SSM_N), lambda i: (0, 0))],
        out_shape=[jax.ShapeDtypeStruct((rows, SSM_WIDTH), BF16),
                   jax.ShapeDtypeStruct((nseq, 2 * SSM_N), F32)],
        scratch_shapes=[pltpu.VMEM((rows, 2 * SSM_N), F32), pltpu.VMEM((SCAN_ROWS, 2 * SSM_N), F32)],
        compiler_params=_cparams(("arbitrary",)),
        name="ssm_sample",
    )(proj2d, h0, sp["wb"], sp["tab"], sp["wc"], sp["d"], sp["wglu"])


def _ssm_params(lam_re, lam_im, log_dt, b_re, b_im, c_re, c_im, d_skip, w_glu):
    lr, li = lam_re.astype(F32), lam_im.astype(F32)
    dt = jnp.exp(log_dt.astype(F32))[:, None]
    mag = jnp.exp(lr * dt)
    ab_re, ab_im = mag * jnp.cos(li * dt), mag * jnp.sin(li * dt)
    den = lr * lr + li * li
    nr = ab_re - 1.0
    f_re = (nr * lr + ab_im * li) / den
    f_im = (ab_im * lr - nr * li) / den
    br, bi = b_re.astype(F32), b_im.astype(F32)
    bb_re = f_re[..., None] * br - f_im[..., None] * bi
    bb_im = f_re[..., None] * bi + f_im[..., None] * br
    eye = jnp.eye(SSM_GROUPS, dtype=F32)

    def in_mat(bb):
        return jnp.einsum('gpc,gh->gchp', bb, eye).reshape(SSM_WIDTH, SSM_N)

    def out_mat(c):
        return jnp.einsum('gcp,gh->gphc', c.astype(F32), eye).reshape(SSM_N, SSM_WIDTH)

    wb = jnp.concatenate([in_mat(bb_re), in_mat(bb_im)], axis=1).astype(BF16)
    wc = jnp.concatenate([out_mat(c_re), -out_mat(c_im)], axis=0).astype(BF16)

    ar, ai = ab_re.reshape(1, SSM_N), ab_im.reshape(1, SSM_N)

    def cmul(x, y):
        return (x[0] * y[0] - x[1] * y[1], x[0] * y[1] + x[1] * y[0])

    pw = [(ar, ai)]
    for _ in range(SCAN_ROWS - 1):
        pw.append(cmul(pw[-1], (ar, ai)))
    row = jnp.arange(SCAN_ROWS)[:, None]
    tabs = []
    for shift in (1, 2, 4):
        for part in pw[shift - 1]:
            tabs.append(jnp.where(row >= shift, part, 0.0))
    tabs.append(jnp.concatenate([p[0] for p in pw], axis=0))
    tabs.append(jnp.concatenate([p[1] for p in pw], axis=0))
    tab = jnp.stack([jnp.broadcast_to(t, (SCAN_ROWS, SSM_N)) for t in tabs])
    return dict(wb=wb, tab=tab, wc=wc, d=d_skip.astype(F32).reshape(1, SSM_WIDTH), wglu=w_glu.astype(BF16))


_RET_G = 1.0 - np.exp2(-5.0 - np.arange(RET_HEADS, dtype=np.float64))


def _ret_consts(chunk, n_rows):
    idx = np.arange(n_rows)
    loc = idx % chunk
    same = (idx[:, None] // chunk) == (idx[None, :] // chunk)
    diff = loc[:, None] - loc[None, :]
    dec = np.where(same & (diff >= 0), _RET_G[:, None, None] ** np.maximum(diff, 0)[None], 0.0)
    qdec = np.repeat((_RET_G[None, :] ** (loc[:, None] + 1.0)), HEAD_DIM, axis=1)
    kdec = np.repeat((_RET_G[None, :] ** (chunk - 1.0 - loc[:, None])), HEAD_DIM, axis=1)
    return (jnp.asarray(dec, F32), jnp.asarray(qdec, F32), jnp.asarray(kdec, F32),
            jnp.asarray(_RET_G ** chunk, F32))


def _rope_tables(pos):
    half = HEAD_DIM // 2
    theta = 1.0 / (ROPE_BASE ** np.linspace(0.0, 1.0, half))
    ang = np.asarray(pos, np.float64)[:, None] * theta[None, :]
    cos = np.repeat(np.cos(ang), 2, axis=1)
    sin = np.repeat(np.sin(ang), 2, axis=1) * np.tile([-1.0, 1.0], half)[None]
    return (jnp.asarray(np.tile(cos, (1, RET_HEADS)), F32), jnp.asarray(np.tile(sin, (1, RET_HEADS)), F32))


def _rotate_pairs(x, cos, sin_signed):
    lane = lax.broadcasted_iota(jnp.int32, x.shape, 1)
    nxt = pltpu.roll(x, x.shape[1] - 1, 1)
    prv = pltpu.roll(x, 1, 1)
    return x * cos + jnp.where(lane % 2 == 0, nxt, prv) * sin_signed


def _ret_head_out(o, gate, norm):
    ms = jnp.mean(o * o, axis=-1, keepdims=True)
    return o * lax.rsqrt(ms + RMS_EPS) * norm * (gate * _sigmoid(gate))


RET_TILE = 512


def _ret_prompt_kernel(gc_ref, q_ref, k_ref, v_ref, g_ref, cos_ref, sin_ref, dec_ref, qdec_ref,
                       kdec_ref, norm_ref, mh_ref, y_ref, r_ref, o_scr):
    @pl.when(pl.program_id(1) == 0)
    def _():
        r_ref[...] = jnp.zeros_like(r_ref)

    cos, sin = cos_ref[...], sin_ref[...]
    q = _rotate_pairs(q_ref[0], cos, sin)
    k = _rotate_pairs(k_ref[0], cos, sin) * (HEAD_DIM ** -0.5)
    qb, kb, vb = q.astype(BF16), k.astype(BF16), v_ref[0].astype(BF16)
    kdb = (k * kdec_ref[...]).astype(BF16)
    for c in range(RET_TILE // RET_CHUNK):
        rows = slice(c * RET_CHUNK, (c + 1) * RET_CHUNK)
        for h in range(RET_HEADS):
            sl = slice(h * HEAD_DIM, (h + 1) * HEAD_DIM)
            qh, vh = qb[rows, sl], vb[rows, sl]
            s = _dot_nt(qh, kb[rows, sl]) * dec_ref[h]
            r = r_ref[0, h]
            o_scr[rows, sl] = _dot(s.astype(BF16), vh) + _dot(qh, r.astype(BF16)) * qdec_ref[rows, sl]
            r_ref[0, h] = gc_ref[h] * r + _dot_tn(kdb[rows, sl], vh)
    g = g_ref[0]
    y_ref[0] = (_head_rms(o_scr[...], mh_ref[...], norm_ref[...]) * (g * _sigmoid(g))).astype(y_ref.dtype)


def _ret_prompt(proj3d, cos, sin, norm):
    b, l, _ = proj3d.shape
    c = RET_CHUNK
    dec, _, _, gc = _ret_consts(c, c)
    _, qdec, kdec, _ = _ret_consts(c, RET_TILE)
    blk = (1, RET_TILE, MIX_BLOCK)
    tspec = pl.BlockSpec((RET_TILE, MIX_BLOCK), lambda bi, i: (i, 0))
    cspec = pl.BlockSpec((RET_TILE, MIX_BLOCK), lambda bi, i: (0, 0))
    return pl.pallas_call(
        _ret_prompt_kernel,
        grid=(b, l // RET_TILE),
        in_specs=[pl.BlockSpec(memory_space=pltpu.SMEM),
                  pl.BlockSpec(blk, lambda bi, i: (bi, i, COL_RET_Q)),
                  pl.BlockSpec(blk, lambda bi, i: (bi, i, COL_RET_K)),
                  pl.BlockSpec(blk, lambda bi, i: (bi, i, COL_RET_V)),
                  pl.BlockSpec(blk, lambda bi, i: (bi, i, COL_RET_G)),
                  tspec, tspec,
                  pl.BlockSpec((RET_HEADS, c, c), lambda bi, i: (0, 0, 0)),
                  cspec, cspec,
                  pl.BlockSpec((1, MIX_BLOCK), lambda bi, i: (0, 0)),
                  pl.BlockSpec((MIX_BLOCK, MIX_BLOCK), lambda bi, i: (0, 0))],
        out_specs=[pl.BlockSpec(blk, lambda bi, i: (bi, i, 0)),
                   pl.BlockSpec((1, RET_HEADS, HEAD_DIM, HEAD_DIM), lambda bi, i: (bi, 0, 0, 0))],
        out_shape=[jax.ShapeDtypeStruct((b, l, MIX_BLOCK), BF16),
                   jax.ShapeDtypeStruct((b, RET_HEADS, HEAD_DIM, HEAD_DIM), F32)],
        scratch_shapes=[pltpu.VMEM((RET_TILE, MIX_BLOCK), F32)],
        compiler_params=_cparams(("parallel", "arbitrary")),
        name="ret_prompt",
    )(gc, proj3d, proj3d, proj3d, proj3d, cos, sin, dec, qdec, kdec, norm, _head_mean_matrix(MIX_BLOCK))


def _ret_sample_kernel(gc_ref, q_ref, k_ref, v_ref, g_ref, cos_ref, sin_ref, dec_ref, qdec_ref,
                       kdec_ref, norm_ref, r0_ref, y_ref, rn_ref, *, n_new, s_blk):
    cos, sin = cos_ref[...], sin_ref[...]
    q = _rotate_pairs(q_ref[...], cos, sin)
    k = _rotate_pairs(k_ref[...], cos, sin) * (HEAD_DIM ** -0.5)
    v = v_ref[...]
    g = g_ref[...]
    kd = k * kdec_ref[...]
    qdec = qdec_ref[...]
    norm = norm_ref[...]
    rows = s_blk * n_new
    seq = lax.broadcasted_iota(jnp.int32, (rows, HEAD_DIM), 0) // n_new
    for h in range(RET_HEADS):
        sl = slice(h * HEAD_DIM, (h + 1) * HEAD_DIM)
        qf, kdf = q[:, sl], kd[:, sl]
        qh, kh, vh = qf.astype(BF16), k[:, sl].astype(BF16), v[:, sl].astype(BF16)
        s = _dot_nt(qh, kh) * dec_ref[h]
        cross = jnp.zeros((rows, HEAD_DIM), F32)
        for si in range(s_blk):
            mine = seq == si
            r = r0_ref[si, h]
            cross = cross + _dot(jnp.where(mine, qf, 0.0).astype(BF16), r.astype(BF16))
            rn_ref[si, h] = gc_ref[h] * r + _dot_tn(jnp.where(mine, kdf, 0.0).astype(BF16), vh)
        o = _dot(s.astype(BF16), vh) + cross * qdec[:, sl]
        y_ref[:, sl] = _ret_head_out(o, g[:, sl], norm[:, sl]).astype(y_ref.dtype)


def _ret_sample(proj2d, r0, cos, sin, norm, n_new, s_blk):
    rows = s_blk * n_new
    nseq = r0.shape[0]
    dec, qdec, kdec, gc = _ret_consts(n_new, rows)
    blk = (rows, MIX_BLOCK)
    cspec = pl.BlockSpec(blk, lambda i: (0, 0))
    rblk = (s_blk, RET_HEADS, HEAD_DIM, HEAD_DIM)
    return pl.pallas_call(
        functools.partial(_ret_sample_kernel, n_new=n_new, s_blk=s_blk),
        grid=(nseq // s_blk,),
        in_specs=[pl.BlockSpec(memory_space=pltpu.SMEM),
                  pl.BlockSpec(blk, lambda i: (i, COL_RET_Q)),
                  pl.BlockSpec(blk, lambda i: (i, COL_RET_K)),
                  pl.BlockSpec(blk, lambda i: (i, COL_RET_V)),
                  pl.BlockSpec(blk, lambda i: (i, COL_RET_G)),
                  cspec, cspec,
                  pl.BlockSpec((RET_HEADS, rows, rows), lambda i: (0, 0, 0)),
                  cspec, cspec,
                  pl.BlockSpec((1, MIX_BLOCK), lambda i: (0, 0)),
                  pl.BlockSpec(rblk, lambda i: (i, 0, 0, 0))],
        out_specs=[pl.BlockSpec(blk, lambda i: (i, 0)),
                   pl.BlockSpec(rblk, lambda i: (i, 0, 0, 0))],
        out_shape=[jax.ShapeDtypeStruct((nseq * n_new, MIX_BLOCK), BF16),
                   jax.ShapeDtypeStruct((nseq, RET_HEADS, HEAD_DIM, HEAD_DIM), F32)],
        compiler_params=_cparams(("parallel",)),
        name="ret_sample",
    )(gc, proj2d, proj2d, proj2d, proj2d, cos, sin, dec, qdec, kdec, norm, r0)


def _block_diag(w):
    g, n, _ = w.shape
    return jnp.einsum('gcd,gh->gchd', w, jnp.eye(g, dtype=w.dtype)).reshape(g * n, g * n)


def _layer_params(l, p):
    return dict(
        norm_mix=p['norm_mix'][l].reshape(1, D_MODEL),
        norm_ffn=p['norm_ffn'][l].reshape(1, D_MODEL),
        w_in=p['w_in'][l].astype(BF16),
        w_out=p['w_out'][l].astype(BF16),
        pool_w=_block_diag(p['pool_w'][l].astype(F32)).astype(BF16),
        pool_scale=p['pool_scale'][l].astype(F32).reshape(1, POOL_WIDTH),
        qg=p['swa_q_norm'][l].astype(F32).reshape(1, HEAD_DIM),
        kg=p['swa_k_norm'][l].astype(F32).reshape(1, HEAD_DIM),
        sinks=p['swa_sinks'][l].astype(F32),
        ssm=_ssm_params(p['ssm_lambda_re'][l], p['ssm_lambda_im'][l], p['ssm_log_dt'][l],
                        p['ssm_b_re'][l], p['ssm_b_im'][l], p['ssm_c_re'][l], p['ssm_c_im'][l],
                        p['ssm_d'][l], p['ssm_w_glu'][l]),
        ret_norm=p['ret_norm'][l].astype(F32).reshape(1, MIX_BLOCK),
    )


def _channel_mix(xs_list, l, lp, p):
    i = l // 2
    if l % 2 == 0:
        wg, wu, wd = (p[k][i].astype(BF16) for k in ('ffn_w_gate', 'ffn_w_up', 'ffn_w_down'))
        return [_ffn(x, lp['norm_ffn'], wg, wu, wd, 512, D_FF // 2) for x in xs_list]
    wr = jnp.pad(p['moe_router'][i].astype(F32), ((0, 0), (0, 128 - N_EXPERTS)))
    wg, wu, wd = (p[k][i].astype(BF16) for k in ('moe_w_gate', 'moe_w_up', 'moe_w_down'))
    return _moe(xs_list, lp['norm_ffn'], wr, wg, wu, wd)


def _mix_prompt(x2, b, l, lp, bias, cos, sin):
    proj2 = _norm_matmul(x2, lp['norm_mix'], lp['w_in'], 512)
    proj3 = proj2.reshape(b, l, IN_WIDTH)
    tb = 512
    y_pool = _pool(proj2, COL_POOL, lp['pool_w'], lp['pool_scale'], n_rows=b * l, tb=tb,
                   tiles_per_seq=l // tb, pos0=0)
    y_swa, kn = _swa_prompt(proj3, lp['sinks'], lp['qg'], lp['kg'], bias)
    y_ssm, hn = _ssm_prompt(proj3, lp['ssm'], 512)
    y_ret, rn = _ret_prompt(proj3, cos, sin, lp['ret_norm'])
    ys = (y_pool, y_swa.reshape(b * l, MIX_BLOCK), y_ssm.reshape(b * l, MIX_BLOCK),
          y_ret.reshape(b * l, MIX_BLOCK))
    x2 = _out_proj(x2, ys, lp['w_out'], 512)
    kw = SWA_KV_HEADS * HEAD_DIM
    hn = hn[:, 0]
    states = (proj3[:, l - POOL_BUF:, :POOL_WIDTH],
              kn.reshape(b, SWA_WINDOW, SWA_KV_HEADS, HEAD_DIM),
              proj3[:, l - SWA_WINDOW:, COL_SWA_KV * MIX_BLOCK + kw:(COL_SWA_KV + 1) * MIX_BLOCK]
              .reshape(b, SWA_WINDOW, SWA_KV_HEADS, HEAD_DIM),
              jnp.stack([hn[:, :SSM_N], hn[:, SSM_N:]], axis=-1).reshape(b, SSM_GROUPS, SSM_STATE, 2),
              rn)
    return x2, states


SAMPLE_SEQ_BLOCK = 16


def _mix_sample(x2, nseq, n_new, start_pos, lp, st, bias, cos, sin):
    state_pool, cache_k, cache_v, state_ssm, state_ret = st
    rows = nseq * n_new
    wb = cache_k.shape[1]
    kw = SWA_KV_HEADS * HEAD_DIM
    ext_rows = POOL_HALO + n_new
    proj2 = _norm_matmul(x2, lp['norm_mix'], lp['w_in'], 512)
    proj3 = proj2.reshape(nseq, n_new, IN_WIDTH)
    u_pool = proj3[:, :, :POOL_WIDTH]
    buf = state_pool.astype(F32)
    ext = jnp.concatenate([jnp.zeros((nseq, POOL_HALO - POOL_BUF, POOL_WIDTH), F32), buf, u_pool], axis=1)
    y_pool = _pool(ext.reshape(nseq * ext_rows, POOL_WIDTH), 0, lp['pool_w'], lp['pool_scale'],
                   n_rows=nseq * ext_rows, tb=nseq * ext_rows, tiles_per_seq=1, pos0=start_pos)
    y_pool = y_pool.reshape(nseq, ext_rows, POOL_WIDTH)[:, POOL_HALO:].reshape(rows, POOL_WIDTH)
    y_swa, nk, nv = _swa_sample(proj3, cache_k.reshape(nseq, wb, kw).astype(F32),
                                cache_v.reshape(nseq, wb, kw).astype(F32),
                                lp['sinks'], lp['qg'], lp['kg'], bias, SAMPLE_SEQ_BLOCK)
    h0 = state_ssm.astype(F32).reshape(nseq, SSM_N, 2)
    h0 = jnp.concatenate([h0[..., 0], h0[..., 1]], axis=1)
    y_ssm, hn = _ssm_sample(proj2, h0, lp['ssm'])
    y_ret, rn = _ret_sample(proj2, state_ret.astype(F32), cos, sin, lp['ret_norm'], n_new, SAMPLE_SEQ_BLOCK)
    ys = (y_pool, y_swa.reshape(rows, MIX_BLOCK), y_ssm, y_ret)
    x2 = _out_proj(x2, ys, lp['w_out'], 512)
    states = (jnp.concatenate([buf, u_pool], axis=1)[:, -POOL_BUF:],
              nk.reshape(nseq, SWA_WINDOW, SWA_KV_HEADS, HEAD_DIM),
              nv.reshape(nseq, SWA_WINDOW, SWA_KV_HEADS, HEAD_DIM),
              jnp.stack([hn[:, :SSM_N], hn[:, SSM_N:]], axis=-1).reshape(nseq, SSM_GROUPS, SSM_STATE, 2),
              rn)
    return x2, states


def _forward(x_prompt, x_sample, past_len, sample_state, p, rel_bias):
    b, l, d = x_prompt.shape
    nseq, n_new, _ = x_sample.shape
    wb = sample_state[1].shape[2]
    depth = p['norm_mix'].shape[0]
    bias_p = _swa_bias(rel_bias, np.arange(SWA_BLOCK)[:, None] - np.arange(2 * SWA_BLOCK)[None, :] + SWA_BLOCK)
    bias_s = _swa_bias(rel_bias, np.arange(n_new)[:, None] - np.arange(wb + n_new)[None, :] + wb)
    rope_p = _rope_tables(np.arange(l))
    rope_s = _rope_tables(past_len + (np.arange(SAMPLE_SEQ_BLOCK * n_new) % n_new))
    xp = x_prompt.reshape(b * l, d)
    xs = x_sample.reshape(nseq * n_new, d)
    st_p, st_s = [], []
    for li in range(depth):
        lp = _layer_params(li, p)
        xp, sp = _mix_prompt(xp, b, l, lp, bias_p, *rope_p)
        xs, ss = _mix_sample(xs, nseq, n_new, past_len, lp, [s[li] for s in sample_state], bias_s, *rope_s)
        xp, xs = _channel_mix([xp, xs], li, lp, p)
        st_p.append(sp)
        st_s.append(ss)
    outs = [xp.reshape(b, l, d), xs.reshape(nseq, n_new, d)]
    for k in range(5):
        outs.append(jnp.stack([s[k] for s in st_p]))
        outs.append(jnp.stack([s[k] for s in st_s]))
    return tuple(outs)


PAST_LEN = 16384


def kernel(x_prompt, x_sample, state_pool, cache_swa_k, cache_swa_v, state_ssm, state_ret,
           norm_mix, norm_ffn, w_in, w_out, pool_w, pool_scale, swa_q_norm, swa_k_norm, swa_sinks,
           rel_bias, ssm_lambda_re, ssm_lambda_im, ssm_log_dt, ssm_b_re, ssm_b_im, ssm_c_re, ssm_c_im,
           ssm_d, ssm_w_glu, ret_norm, ffn_w_gate, ffn_w_up, ffn_w_down, moe_router, moe_w_gate,
           moe_w_up, moe_w_down):
    p = dict(norm_mix=norm_mix, norm_ffn=norm_ffn, w_in=w_in, w_out=w_out, pool_w=pool_w,
             pool_scale=pool_scale, swa_q_norm=swa_q_norm, swa_k_norm=swa_k_norm, swa_sinks=swa_sinks,
             ssm_lambda_re=ssm_lambda_re, ssm_lambda_im=ssm_lambda_im, ssm_log_dt=ssm_log_dt,
             ssm_b_re=ssm_b_re, ssm_b_im=ssm_b_im, ssm_c_re=ssm_c_re, ssm_c_im=ssm_c_im,
             ssm_d=ssm_d, ssm_w_glu=ssm_w_glu, ret_norm=ret_norm,
             ffn_w_gate=ffn_w_gate, ffn_w_up=ffn_w_up, ffn_w_down=ffn_w_down, moe_router=moe_router,
             moe_w_gate=moe_w_gate, moe_w_up=moe_w_up, moe_w_down=moe_w_down)
    return _forward(x_prompt, x_sample, PAST_LEN,
                    (state_pool, cache_swa_k, cache_swa_v, state_ssm, state_ret), p, rel_bias)
```

```python
import functools
import math

import numpy as np
import jax
import jax.numpy as jnp
from jax import lax
from jax.experimental import pallas as pl
from jax.experimental.pallas import tpu as pltpu

F32 = jnp.float32
BF16 = jnp.bfloat16

D_MODEL = 1024
HEAD_DIM = 64
POOL_WIDTH = 256
POOL_WINDOWS = (2, 4, 8, 16)
POOL_BUF = 15
POOL_HALO = 16
SWA_HEADS = 4
SWA_KV_HEADS = 2
SWA_WINDOW = 128
SWA_BLOCK = 128
SSM_WIDTH = 256
SSM_CH = 16
SSM_GROUPS = 16
SSM_STATE = 64
SSM_N = SSM_GROUPS * SSM_STATE
RET_HEADS = 4
RET_CHUNK = 128
ROPE_BASE = 10000.0
IN_WIDTH = 2048
MIX_BLOCK = 256
D_FF = 2816
N_EXPERTS = 8
T5_BUCKETS = 32
T5_MAX_DIST = 128
RMS_EPS = 1e-6
NEG = -1e30
SUBLANES = 8
SCAN_ROWS = SUBLANES

COL_POOL, COL_SWA_Q, COL_SWA_KV, COL_SSM, COL_RET_Q, COL_RET_K, COL_RET_V, COL_RET_G = range(8)

VMEM_LIMIT = 48 * 1024 * 1024


def _cparams(sem):
    return pltpu.CompilerParams(dimension_semantics=sem, vmem_limit_bytes=VMEM_LIMIT)


def _rms(x, g):
    ms = jnp.mean(x * x, axis=-1, keepdims=True)
    return x * lax.rsqrt(ms + RMS_EPS) * g


def _dot(a, b):
    return jnp.dot(a, b, preferred_element_type=F32)


def _dot_nt(a, b):
    return lax.dot_general(a, b, (((1,), (1,)), ((), ())), preferred_element_type=F32)


def _dot_tn(a, b):
    return lax.dot_general(a, b, (((0,), (0,)), ((), ())), preferred_element_type=F32)


def _sigmoid(x):
    return 1.0 / (1.0 + jnp.exp(-x))


def _norm_matmul_kernel(x_ref, g_ref, w_ref, o_ref):
    h = _rms(x_ref[...], g_ref[...]).astype(BF16)
    o_ref[...] = _dot(h, w_ref[...])


def _norm_matmul(x, g, w, tm):
    t, d = x.shape
    tm = min(tm, t)
    n = w.shape[1]
    return pl.pallas_call(
        _norm_matmul_kernel,
        grid=(t // tm,),
        in_specs=[pl.BlockSpec((tm, d), lambda i: (i, 0)),
                  pl.BlockSpec((1, d), lambda i: (0, 0)),
                  pl.BlockSpec((d, n), lambda i: (0, 0))],
        out_specs=pl.BlockSpec((tm, n), lambda i: (i, 0)),
        out_shape=jax.ShapeDtypeStruct((t, n), F32),
        compiler_params=_cparams(("parallel",)),
        name="norm_matmul",
    )(x, g, w)


def _swiglu_chunk(h, wg, wu, wd):
    a = _dot(h, wg)
    b = _dot(h, wu)
    return _dot((a * _sigmoid(a) * b).astype(BF16), wd)


def _mixed_residual(x_ref, y_refs, w_ref):
    acc = x_ref[...]
    for k, y_ref in enumerate(y_refs):
        acc = acc + _dot(y_ref[...], w_ref[k * MIX_BLOCK:(k + 1) * MIX_BLOCK, :])
    return acc


def _mix_in_specs(tm, d, imap):
    yspec = pl.BlockSpec((tm, MIX_BLOCK), imap(lambda i: (i, 0)))
    return [pl.BlockSpec((tm, d), imap(lambda i: (i, 0))), yspec, yspec, yspec, yspec,
            pl.BlockSpec((d, d), imap(lambda i: (0, 0)))]


def _out_proj_ffn_kernel(x_ref, y0_ref, y1_ref, y2_ref, y3_ref, wo_ref, g_ref, wg_ref, wu_ref, wd_ref,
                         o_ref, h_scr):
    @pl.when(pl.program_id(1) == 0)
    def _():
        x1 = _mixed_residual(x_ref, (y0_ref, y1_ref, y2_ref, y3_ref), wo_ref)
        h_scr[...] = _rms(x1, g_ref[...]).astype(BF16)
        o_ref[...] = x1

    o_ref[...] += _swiglu_chunk(h_scr[...], wg_ref[...], wu_ref[...], wd_ref[...])


def _out_proj_ffn(x, ys, w_out, g, wg, wu, wd, tm, tf):
    t, d = x.shape
    tm = min(tm, t)
    f = wg.shape[1]
    imap = lambda fn: (lambda i, j: fn(i))
    return pl.pallas_call(
        _out_proj_ffn_kernel,
        grid=(t // tm, f // tf),
        in_specs=_mix_in_specs(tm, d, imap)
                 + [pl.BlockSpec((1, d), lambda i, j: (0, 0)),
                    pl.BlockSpec((d, tf), lambda i, j: (0, j)),
                    pl.BlockSpec((d, tf), lambda i, j: (0, j)),
                    pl.BlockSpec((tf, d), lambda i, j: (j, 0))],
        out_specs=pl.BlockSpec((tm, d), lambda i, j: (i, 0)),
        out_shape=jax.ShapeDtypeStruct((t, d), F32),
        scratch_shapes=[pltpu.VMEM((tm, d), BF16)],
        compiler_params=_cparams(("parallel", "arbitrary")),
        name="out_proj_ffn",
    )(x, *ys, w_out, g, wg, wu, wd)


ROUTE_ID_LANES = (0, 1)
ROUTE_GATE_LANES = (2, 3)


def _split_bf16(x):
    hi = x.astype(BF16)
    return hi, (x - hi.astype(F32)).astype(BF16)


def _out_proj_router_kernel(x_ref, y0_ref, y1_ref, y2_ref, y3_ref, wo_ref, g_ref, wr_ref, x1_ref, c_ref):
    x1 = _mixed_residual(x_ref, (y0_ref, y1_ref, y2_ref, y3_ref), wo_ref)
    x1_ref[...] = x1
    h_hi, h_lo = _split_bf16(_rms(x1, g_ref[...]))
    logits = _dot(jnp.concatenate([h_hi, h_lo, h_hi], axis=1), wr_ref[...])
    lane = lax.broadcasted_iota(jnp.int32, logits.shape, 1).astype(F32)
    lg = jnp.where(lane < N_EXPERTS, logits, NEG)
    m1 = jnp.max(lg, axis=-1, keepdims=True)
    i1 = jnp.min(jnp.where(lg == m1, lane, 128.0), axis=-1, keepdims=True)
    lg2 = jnp.where(lane == i1, NEG, lg)
    m2 = jnp.max(lg2, axis=-1, keepdims=True)
    i2 = jnp.min(jnp.where(lg2 == m2, lane, 128.0), axis=-1, keepdims=True)
    ex = jnp.exp(m2 - m1)
    vals = (i1, i2, 1.0 / (1.0 + ex), ex / (1.0 + ex))
    out = jnp.zeros_like(logits)
    for ln, v in zip(ROUTE_ID_LANES + ROUTE_GATE_LANES, vals):
        out = jnp.where(lane == ln, v, out)
    c_ref[...] = out


def _router_weights(wr):
    w = jnp.pad(wr.astype(F32), ((0, 0), (0, 128 - N_EXPERTS)))
    hi, lo = _split_bf16(w)
    return jnp.concatenate([hi, hi, lo], axis=0)


def _out_proj_router(x, ys, w_out, g, wr3, tm):
    t, d = x.shape
    tm = min(tm, t)
    imap = lambda fn: fn
    return pl.pallas_call(
        _out_proj_router_kernel,
        grid=(t // tm,),
        in_specs=_mix_in_specs(tm, d, imap)
                 + [pl.BlockSpec((1, d), lambda i: (0, 0)),
                    pl.BlockSpec((3 * d, 128), lambda i: (0, 0))],
        out_specs=[pl.BlockSpec((tm, d), lambda i: (i, 0)),
                   pl.BlockSpec((tm, 128), lambda i: (i, 0))],
        out_shape=[jax.ShapeDtypeStruct((t, d), F32), jax.ShapeDtypeStruct((t, 128), F32)],
        compiler_params=_cparams(("parallel",)),
        name="out_proj_router",
    )(x, *ys, w_out, g, wr3)


DMA_ISSUE_UNROLL = 8


def _row_copy(src, i, dst, j, sem):
    return pltpu.make_async_copy(src.at[pl.ds(i, 1)], dst.at[pl.ds(j, 1)], sem)


def _dispatch_kernel(meta_ref, pos_ref, *rest, td, tm, n_tiles, first_step):
    n_streams = len(first_step) - 1
    x_refs, (xs_hbm, zero_scr, sem) = rest[:n_streams], rest[n_streams:]
    step = pl.program_id(0)

    def zero_row(r):
        return _row_copy(zero_scr, 0, xs_hbm, r, sem)

    @pl.when(step == 0)
    def _():
        zero_scr[...] = jnp.zeros_like(zero_scr)
        n_used = meta_ref[2 * N_EXPERTS]
        tile_fills = [(i >= n_used, pltpu.make_async_copy(zero_scr, xs_hbm.at[pl.ds(i * tm, tm)], sem))
                      for i in range(n_tiles)]
        for cond, copy in tile_fills:
            pl.when(cond)(copy.start)
        for e in range(N_EXPERTS):
            lax.fori_loop(meta_ref[e], meta_ref[N_EXPERTS + e], lambda r, c: (zero_row(r).start(), c)[1], 0)
        for e in range(N_EXPERTS):
            lax.fori_loop(meta_ref[e], meta_ref[N_EXPERTS + e], lambda r, c: (zero_row(r).wait(), c)[1], 0)
        for cond, copy in tile_fills:
            pl.when(cond)(copy.wait)

    def scatter(x_ref):
        def issue(j, c):
            for k in range(2):
                _row_copy(x_ref, j, xs_hbm, pos_ref[0, 0, 2 * j + k], sem).start()
            return c

        lax.fori_loop(0, td, issue, 0, unroll=DMA_ISSUE_UNROLL)
        for _ in range(2):
            pltpu.make_async_copy(x_ref, xs_hbm.at[pl.ds(0, td)], sem).wait()

    for s, x_ref in enumerate(x_refs):
        pl.when((step >= first_step[s]) & (step < first_step[s + 1]))(functools.partial(scatter, x_ref))


def _dispatch(xs_list, pos, meta, n_rows, tm, td):
    d = xs_list[0].shape[1]
    td = min([td] + [x.shape[0] for x in xs_list])
    first_step = [0]
    for x in xs_list:
        first_step.append(first_step[-1] + x.shape[0] // td)
    n_steps = first_step[-1]
    pos3 = pos.reshape(n_steps, 1, 2 * td)

    def tile_map(s):
        lo, hi = first_step[s], first_step[s + 1]
        return lambda i, m: (jnp.clip(i, lo, hi - 1) - lo, 0)

    in_specs = [pl.BlockSpec((1, 1, 2 * td), lambda i, m: (i, 0, 0), memory_space=pltpu.SMEM)]
    in_specs += [pl.BlockSpec((td, d), tile_map(s)) for s in range(len(xs_list))]
    return pl.pallas_call(
        functools.partial(_dispatch_kernel, td=td, tm=tm, n_tiles=n_rows // tm, first_step=tuple(first_step)),
        grid_spec=pltpu.PrefetchScalarGridSpec(
            num_scalar_prefetch=1, grid=(n_steps,), in_specs=in_specs,
            out_specs=pl.BlockSpec(memory_space=pl.ANY),
            scratch_shapes=[pltpu.VMEM((tm, d), F32), pltpu.SemaphoreType.DMA]),
        out_shape=jax.ShapeDtypeStruct((n_rows, d), F32),
        compiler_params=_cparams(("arbitrary",)),
        name="moe_dispatch",
    )(meta, pos3, *xs_list)


def _grouped_ffn_kernel(te_ref, nu_ref, x_ref, g_ref, wg_ref, wu_ref, wd_ref, o_ref, h_scr):
    del te_ref
    j = pl.program_id(1)
    used = pl.program_id(0) < nu_ref[0]

    @pl.when(jnp.logical_not(used) & (j == 0))
    def _():
        o_ref[...] = jnp.zeros_like(o_ref)

    @pl.when(used)
    def _():
        @pl.when(j == 0)
        def _():
            h_scr[...] = _rms(x_ref[...], g_ref[...]).astype(BF16)

        y = _swiglu_chunk(h_scr[...], wg_ref[0], wu_ref[0], wd_ref[0])

        @pl.when(j == 0)
        def _():
            o_ref[...] = y

        @pl.when(j > 0)
        def _():
            o_ref[...] += y


def _grouped_ffn(xs, g, tile_expert, n_used, wg, wu, wd, tm, tf):
    r, d = xs.shape
    f = wg.shape[2]
    nj = f // tf

    def row_map(i, j, te, nu):
        return (i, 0)

    def col_of(i, j, nu):
        return jnp.where(i < nu[0], j, nj - 1)

    grid_spec = pltpu.PrefetchScalarGridSpec(
        num_scalar_prefetch=2,
        grid=(r // tm, nj),
        in_specs=[pl.BlockSpec((tm, d), row_map),
                  pl.BlockSpec((1, d), lambda i, j, te, nu: (0, 0)),
                  pl.BlockSpec((1, d, tf), lambda i, j, te, nu: (te[i], 0, col_of(i, j, nu))),
                  pl.BlockSpec((1, d, tf), lambda i, j, te, nu: (te[i], 0, col_of(i, j, nu))),
                  pl.BlockSpec((1, tf, d), lambda i, j, te, nu: (te[i], col_of(i, j, nu), 0))],
        out_specs=pl.BlockSpec((tm, d), row_map),
        scratch_shapes=[pltpu.VMEM((tm, d), BF16)],
    )
    return pl.pallas_call(
        _grouped_ffn_kernel,
        grid_spec=grid_spec,
        out_shape=jax.ShapeDtypeStruct((r, d), F32),
        compiler_params=_cparams(("arbitrary", "arbitrary")),
        name="moe_grouped_ffn",
    )(tile_expert, n_used, xs, g, wg, wu, wd)


def _combine_kernel(pos_ref, x_ref, route_ref, ys_hbm, o_ref, buf0, buf1, sem, *, tc):
    def issue(j, c):
        _row_copy(ys_hbm, pos_ref[0, 0, 2 * j], buf0, j, sem).start()
        _row_copy(ys_hbm, pos_ref[0, 0, 2 * j + 1], buf1, j, sem).start()
        return c

    lax.fori_loop(0, tc, issue, 0, unroll=DMA_ISSUE_UNROLL)
    for buf in (buf0, buf1):
        pltpu.make_async_copy(ys_hbm.at[pl.ds(0, tc)], buf, sem).wait()
    route = route_ref[...]
    g0 = route[:, ROUTE_GATE_LANES[0]:ROUTE_GATE_LANES[0] + 1]
    g1 = route[:, ROUTE_GATE_LANES[1]:ROUTE_GATE_LANES[1] + 1]
    o_ref[...] = x_ref[...] + g0 * buf0[...] + g1 * buf1[...]


def _combine(x, route, pos, ys, tc):
    t, d = x.shape
    tc = min(tc, t)
    pos3 = pos.reshape(t // tc, 1, 2 * tc)
    return pl.pallas_call(
        functools.partial(_combine_kernel, tc=tc),
        grid=(t // tc,),
        in_specs=[pl.BlockSpec((1, 1, 2 * tc), lambda i: (i, 0, 0), memory_space=pltpu.SMEM),
                  pl.BlockSpec((tc, d), lambda i: (i, 0)),
                  pl.BlockSpec((tc, 128), lambda i: (i, 0)),
                  pl.BlockSpec(memory_space=pl.ANY)],
        out_specs=pl.BlockSpec((tc, d), lambda i: (i, 0)),
        out_shape=jax.ShapeDtypeStruct((t, d), F32),
        scratch_shapes=[pltpu.VMEM((tc, d), F32), pltpu.VMEM((tc, d), F32), pltpu.SemaphoreType.DMA],
        compiler_params=_cparams(("arbitrary",)),
        name="moe_combine",
    )(pos3, x, route, ys)


MOE_TM = 512


def _route_plan(expert_ids, tm):
    flat = expert_ids.reshape(-1)
    a = flat.shape[0]
    onehot = (flat[None, :] == jnp.arange(N_EXPERTS, dtype=jnp.int32)[:, None]).astype(jnp.int32)
    csum = jnp.cumsum(onehot, axis=1)
    counts = csum[:, -1]
    padded = (counts + tm - 1) // tm * tm
    ends = jnp.cumsum(padded)
    offs = ends - padded
    pos = jnp.sum(onehot * (offs[:, None] + csum - 1), axis=0)
    n_tiles = (a + N_EXPERTS * tm) // tm
    tile_start = jnp.arange(n_tiles, dtype=jnp.int32) * tm
    tile_expert = jnp.minimum(jnp.sum(tile_start[:, None] >= ends[None, :], axis=1), N_EXPERTS - 1)
    n_used = (ends[-1] // tm).reshape(1)
    last = jnp.take(tile_expert, n_used[0] - 1)
    tile_expert = jnp.where(tile_start < ends[-1], tile_expert, last)
    meta = jnp.concatenate([offs + counts, ends, n_used]).astype(jnp.int32)
    return pos.astype(jnp.int32), tile_expert.astype(jnp.int32), n_used.astype(jnp.int32), meta, n_tiles * tm


def _moe(xs_list, routes, g, wg, wu, wd):
    ids = jnp.concatenate([r[:, ROUTE_ID_LANES[0]:ROUTE_ID_LANES[1] + 1] for r in routes]).astype(jnp.int32)
    pos, tile_expert, n_used, meta, n_rows = _route_plan(ids, MOE_TM)
    bounds = np.cumsum([0] + [2 * x.shape[0] for x in xs_list])
    pos_list = [pos[lo:hi] for lo, hi in zip(bounds[:-1], bounds[1:])]
    xs = _dispatch(xs_list, pos, meta, n_rows, MOE_TM, 1024)
    ys = _grouped_ffn(xs, g, tile_expert, n_used, wg, wu, wd, MOE_TM, D_FF // 2)
    return [_combine(x, r, ps, ys, 512) for x, r, ps in zip(xs_list, routes, pos_list)]


def _pool_kernel(u_ref, halo_ref, w_ref, scale_ref, o_ref, *, tiles_per_seq, pos0, tb):
    ti = pl.program_id(0) % tiles_per_seq
    u = u_ref[...]
    halo = jnp.where(ti == 0, 0.0, halo_ref[...])
    ext = jnp.concatenate([halo, u], axis=0)
    s2 = ext + pltpu.roll(ext, 1, 0)
    s4 = s2 + pltpu.roll(s2, 2, 0)
    s8 = s4 + pltpu.roll(s4, 4, 0)
    s16 = s8 + pltpu.roll(s8, 8, 0)
    grp = lax.broadcasted_iota(jnp.int32, (tb, POOL_WIDTH), 1) // (POOL_WIDTH // 4)
    row = lax.broadcasted_iota(jnp.int32, (tb, POOL_WIDTH), 0)
    s = jnp.where(grp == 0, s2[POOL_HALO:],
                  jnp.where(grp == 1, s4[POOL_HALO:],
                            jnp.where(grp == 2, s8[POOL_HALO:], s16[POOL_HALO:])))
    win = jnp.where(grp == 0, 2, jnp.where(grp == 1, 4, jnp.where(grp == 2, 8, 16)))
    cnt = jnp.minimum(win, pos0 + ti * tb + row + 1).astype(F32)
    pooled = s / cnt - u
    o_ref[...] = (_dot(pooled.astype(BF16), w_ref[...]) * scale_ref[...]).astype(o_ref.dtype)


def _pool(proj2d, col, w, scale, *, n_rows, tb, tiles_per_seq, pos0):
    per = tb // POOL_HALO
    return pl.pallas_call(
        functools.partial(_pool_kernel, tiles_per_seq=tiles_per_seq, pos0=pos0, tb=tb),
        grid=(n_rows // tb,),
        in_specs=[pl.BlockSpec((tb, POOL_WIDTH), lambda i: (i, col)),
                  pl.BlockSpec((POOL_HALO, POOL_WIDTH), lambda i: (jnp.maximum(i * per - 1, 0), col)),
                  pl.BlockSpec((POOL_WIDTH, POOL_WIDTH), lambda i: (0, 0)),
                  pl.BlockSpec((1, POOL_WIDTH), lambda i: (0, 0))],
        out_specs=pl.BlockSpec((tb, POOL_WIDTH), lambda i: (i, 0)),
        out_shape=jax.ShapeDtypeStruct((n_rows, POOL_WIDTH), BF16),
        compiler_params=_cparams(("parallel",)),
        name="pool",
    )(proj2d, proj2d, w, scale)


def _t5_bucket_np(rel):
    n = np.maximum(rel, 0)
    max_exact = T5_BUCKETS // 2
    nf = np.maximum(n, max_exact).astype(np.float32)
    large = max_exact + (np.log(nf / max_exact) / math.log(T5_MAX_DIST / max_exact)
                         * (T5_BUCKETS - max_exact)).astype(np.int32)
    large = np.minimum(large, T5_BUCKETS - 1)
    return np.where(n < max_exact, n, large)


def _swa_bias(rel_bias, rel):
    valid = (rel >= 0) & (rel < SWA_WINDOW)
    onehot = jnp.asarray(_t5_bucket_np(rel)[..., None] == np.arange(T5_BUCKETS), F32)
    b = jnp.einsum('qsb,bh->hqs', onehot, rel_bias.astype(F32), precision=lax.Precision.HIGHEST)
    return jnp.where(valid[None], b, NEG)


def _softmax_parts(parts, sink):
    m = sink
    for s in parts:
        m = jnp.maximum(m, jnp.max(s, axis=-1, keepdims=True))
    ps = [jnp.exp(s - m) for s in parts]
    denom = jnp.exp(sink - m)
    for p in ps:
        denom = denom + jnp.sum(p, axis=-1, keepdims=True)
    return ps, denom


def _head_mean_matrix(width):
    h = np.arange(width) // HEAD_DIM
    return jnp.asarray((h[:, None] == h[None, :]) / HEAD_DIM, BF16)


def _head_rms(x, mean_mat, g):
    ms = _dot((x * x).astype(BF16), mean_mat)
    return x * lax.rsqrt(ms + RMS_EPS) * g


SWA_TILE = 512


def _swa_prompt_kernel(sinks_ref, q_ref, kv_ref, halo_ref, qg_ref, kg_ref, mq_ref, mk_ref, bias_ref,
                       y_ref, kn_ref):
    has_prev = pl.program_id(1) > 0
    kw = SWA_KV_HEADS * HEAD_DIM
    blk = SWA_BLOCK
    kv = kv_ref[0]
    halo = halo_ref[0]
    k_ext = jnp.concatenate([halo[:, :kw], kv[:, :kw]], axis=0)
    v_ext = jnp.concatenate([halo[:, kw:], kv[:, kw:]], axis=0).astype(BF16)
    kn = _head_rms(k_ext, mk_ref[...], kg_ref[...])
    kn_ref[0] = kn[SWA_TILE:]
    knb = kn.astype(BF16)
    qn = (_head_rms(q_ref[0], mq_ref[...], qg_ref[...]) * (HEAD_DIM ** -0.5)).astype(BF16)
    row = lax.broadcasted_iota(jnp.int32, (2 * blk, 1), 0)
    col = lax.broadcasted_iota(jnp.int32, (2 * blk, 2 * blk), 1)
    for c in range(SWA_TILE // blk):
        rows = slice(c * blk, (c + 1) * blk)
        keys = slice(c * blk, (c + 2) * blk)
        for kh in range(SWA_KV_HEADS):
            ksl = slice(kh * HEAD_DIM, (kh + 1) * HEAD_DIM)
            h0 = 2 * kh
            q2 = jnp.concatenate([qn[rows, h0 * HEAD_DIM:(h0 + 1) * HEAD_DIM],
                                  qn[rows, (h0 + 1) * HEAD_DIM:(h0 + 2) * HEAD_DIM]], axis=0)
            s = _dot_nt(q2, knb[keys, ksl]) + bias_ref[kh]
            if c == 0:
                s = jnp.where(has_prev | (col >= blk), s, NEG)
            sink = jnp.where(row < blk, sinks_ref[h0], sinks_ref[h0 + 1])
            (p,), denom = _softmax_parts((s,), sink)
            o = _dot(p.astype(BF16), v_ext[keys, ksl]) / denom
            y_ref[0, rows, h0 * HEAD_DIM:(h0 + 1) * HEAD_DIM] = o[:blk].astype(y_ref.dtype)
            y_ref[0, rows, (h0 + 1) * HEAD_DIM:(h0 + 2) * HEAD_DIM] = o[blk:].astype(y_ref.dtype)


def _swa_prompt(proj3d, sinks, qg, kg, bias):
    b, l, _ = proj3d.shape
    kw = SWA_KV_HEADS * HEAD_DIM
    per = SWA_TILE // SWA_BLOCK
    tile = (1, SWA_TILE, MIX_BLOCK)
    const2 = lambda bi, i: (0, 0)
    bias2 = bias.reshape(SWA_KV_HEADS, 2 * SWA_BLOCK, 2 * SWA_BLOCK)
    return pl.pallas_call(
        _swa_prompt_kernel,
        grid=(b, l // SWA_TILE),
        in_specs=[pl.BlockSpec(memory_space=pltpu.SMEM),
                  pl.BlockSpec(tile, lambda bi, i: (bi, i, COL_SWA_Q)),
                  pl.BlockSpec(tile, lambda bi, i: (bi, i, COL_SWA_KV)),
                  pl.BlockSpec((1, SWA_BLOCK, MIX_BLOCK),
                               lambda bi, i: (bi, jnp.maximum(i * per - 1, 0), COL_SWA_KV)),
                  pl.BlockSpec((1, MIX_BLOCK), const2),
                  pl.BlockSpec((1, kw), const2),
                  pl.BlockSpec((MIX_BLOCK, MIX_BLOCK), const2),
                  pl.BlockSpec((kw, kw), const2),
                  pl.BlockSpec((SWA_KV_HEADS, 2 * SWA_BLOCK, 2 * SWA_BLOCK), lambda bi, i: (0, 0, 0))],
        out_specs=[pl.BlockSpec(tile, lambda bi, i: (bi, i, 0)),
                   pl.BlockSpec((1, SWA_BLOCK, kw), lambda bi, i: (bi, 0, 0))],
        out_shape=[jax.ShapeDtypeStruct((b, l, MIX_BLOCK), BF16),
                   jax.ShapeDtypeStruct((b, SWA_BLOCK, kw), F32)],
        compiler_params=_cparams(("parallel", "arbitrary")),
        name="swa_prompt",
    )(sinks, proj3d, proj3d, proj3d, jnp.tile(qg, (1, SWA_HEADS)), jnp.tile(kg, (1, SWA_KV_HEADS)),
      _head_mean_matrix(MIX_BLOCK), _head_mean_matrix(kw), bias2)


def _swa_sample_kernel(sinks_ref, q_ref, kv_ref, ck_ref, cv_ref, qg_ref, kg_ref, bias_ref,
                       y_ref, nk_ref, nv_ref, *, n_new):
    q = q_ref[...]
    kv = kv_ref[...]
    ck = ck_ref[...]
    cv = cv_ref[...]
    qg = qg_ref[...]
    kg = kg_ref[...]
    kw = SWA_KV_HEADS * HEAD_DIM
    bdot = functools.partial(jnp.einsum, preferred_element_type=F32)
    for kh in range(SWA_KV_HEADS):
        ksl = slice(kh * HEAD_DIM, (kh + 1) * HEAD_DIM)
        vsl = slice(kw + kh * HEAD_DIM, kw + (kh + 1) * HEAD_DIM)
        kn = _rms(kv[:, :, ksl], kg)
        vn = kv[:, :, vsl]
        nk_ref[:, :SWA_WINDOW - n_new, ksl] = ck[:, n_new:, ksl]
        nk_ref[:, SWA_WINDOW - n_new:, ksl] = kn
        nv_ref[:, :SWA_WINDOW - n_new, ksl] = cv[:, n_new:, ksl]
        nv_ref[:, SWA_WINDOW - n_new:, ksl] = vn
        for gq in range(SWA_HEADS // SWA_KV_HEADS):
            h = kh * (SWA_HEADS // SWA_KV_HEADS) + gq
            hsl = slice(h * HEAD_DIM, (h + 1) * HEAD_DIM)
            qn = _rms(q[:, :, hsl], qg)
            s_c = bdot('sqd,skd->sqk', qn, ck[:, :, ksl]) * (HEAD_DIM ** -0.5) + bias_ref[h, :, :SWA_WINDOW]
            s_n = bdot('sqd,skd->sqk', qn, kn) * (HEAD_DIM ** -0.5) + bias_ref[h, :, SWA_WINDOW:]
            (p_c, p_n), denom = _softmax_parts((s_c, s_n), sinks_ref[h])
            o = bdot('sqk,skd->sqd', p_c, cv[:, :, ksl]) + bdot('sqk,skd->sqd', p_n, vn)
            y_ref[:, :, hsl] = (o / denom).astype(y_ref.dtype)


def _swa_sample(proj3d, cache_k, cache_v, sinks, qg, kg, bias, s_blk):
    nseq, n_new, _ = proj3d.shape
    kw = SWA_KV_HEADS * HEAD_DIM
    blk = (s_blk, n_new, MIX_BLOCK)
    cblk = (s_blk, SWA_WINDOW, kw)
    return pl.pallas_call(
        functools.partial(_swa_sample_kernel, n_new=n_new),
        grid=(nseq // s_blk,),
        in_specs=[pl.BlockSpec(memory_space=pltpu.SMEM),
                  pl.BlockSpec(blk, lambda i: (i, 0, COL_SWA_Q)),
                  pl.BlockSpec(blk, lambda i: (i, 0, COL_SWA_KV)),
                  pl.BlockSpec(cblk, lambda i: (i, 0, 0)),
                  pl.BlockSpec(cblk, lambda i: (i, 0, 0)),
                  pl.BlockSpec((1, HEAD_DIM), lambda i: (0, 0)),
                  pl.BlockSpec((1, HEAD_DIM), lambda i: (0, 0)),
                  pl.BlockSpec((SWA_HEADS, n_new, SWA_WINDOW + n_new), lambda i: (0, 0, 0))],
        out_specs=[pl.BlockSpec(blk, lambda i: (i, 0, 0)),
                   pl.BlockSpec(cblk, lambda i: (i, 0, 0)),
                   pl.BlockSpec(cblk, lambda i: (i, 0, 0))],
        out_shape=[jax.ShapeDtypeStruct((nseq, n_new, MIX_BLOCK), BF16),
                   jax.ShapeDtypeStruct((nseq, SWA_WINDOW, kw), F32),
                   jax.ShapeDtypeStruct((nseq, SWA_WINDOW, kw), F32)],
        compiler_params=_cparams(("parallel",)),
        name="swa_sample",
    )(sinks, proj3d, proj3d, cache_k, cache_v, qg, kg, bias)


def _ssm_kernel(u_ref, h0_ref, wb_ref, tab_ref, wc_ref, d_ref, wglu_ref, y_ref, hn_ref,
                bu_scr, carry_scr, *, chained, tiles_per_seq, tb):
    n = SSM_N
    u = u_ref[...]
    bu_scr[...] = _dot(u.astype(BF16), wb_ref[...])

    if chained:
        @pl.when(pl.program_id(1) % tiles_per_seq == 0)
        def _():
            carry_scr[...] = jnp.zeros_like(carry_scr)

    def tile_scan(r0, cr, ci):
        hr = bu_scr[pl.ds(r0, SCAN_ROWS), :n]
        hi = bu_scr[pl.ds(r0, SCAN_ROWS), n:]
        for k, shift in enumerate((1, 2, 4)):
            ar, ai = tab_ref[2 * k], tab_ref[2 * k + 1]
            sr, si = pltpu.roll(hr, shift, 0), pltpu.roll(hi, shift, 0)
            hr, hi = hr + ar * sr - ai * si, hi + ar * si + ai * sr
        pr, pi = tab_ref[6], tab_ref[7]
        hr, hi = hr + pr * cr - pi * ci, hi + pr * ci + pi * cr
        bu_scr[pl.ds(r0, SCAN_ROWS), :n] = hr
        bu_scr[pl.ds(r0, SCAN_ROWS), n:] = hi
        return hr[SCAN_ROWS - 1:], hi[SCAN_ROWS - 1:]

    if chained:
        def body(t, carry):
            r0 = pl.multiple_of(t * SCAN_ROWS, SCAN_ROWS)
            lr, li = tile_scan(r0, *carry)
            return (jnp.broadcast_to(lr, (SCAN_ROWS, n)), jnp.broadcast_to(li, (SCAN_ROWS, n)))

        cr, ci = lax.fori_loop(0, tb // SCAN_ROWS, body, (carry_scr[:, :n], carry_scr[:, n:]), unroll=2)
        carry_scr[:, :n] = cr
        carry_scr[:, n:] = ci
        hn_ref[0, :, :n] = cr
        hn_ref[0, :, n:] = ci
    else:
        def body(t, _):
            r0 = pl.multiple_of(t * SCAN_ROWS, SCAN_ROWS)
            h0 = h0_ref[pl.ds(t, 1), :]
            cr = jnp.broadcast_to(h0[:, :n], (SCAN_ROWS, n))
            ci = jnp.broadcast_to(h0[:, n:], (SCAN_ROWS, n))
            lr, li = tile_scan(r0, cr, ci)
            hn_ref[pl.ds(t, 1), :n] = lr
            hn_ref[pl.ds(t, 1), n:] = li
            return 0

        lax.fori_loop(0, tb // SCAN_ROWS, body, 0)

    y = _dot(bu_scr[...].astype(BF16), wc_ref[...]) + d_ref[...] * u
    y = 0.5 * y * (1.0 + jnp.tanh(math.sqrt(2.0 / math.pi) * (y + 0.044715 * (y * y * y))))
    y = y * _sigmoid(_dot(y.astype(BF16), wglu_ref[...]))
    if chained:
        y_ref[0] = y.astype(y_ref.dtype)
    else:
        y_ref[...] = y.astype(y_ref.dtype)


def _ssm_common_specs(zero_map2, zero_map3):
    return [pl.BlockSpec((SSM_WIDTH, 2 * SSM_N), zero_map2),
            pl.BlockSpec((8, SCAN_ROWS, SSM_N), zero_map3),
            pl.BlockSpec((2 * SSM_N, SSM_WIDTH), zero_map2),
            pl.BlockSpec((1, SSM_WIDTH), zero_map2),
            pl.BlockSpec((SSM_WIDTH, SSM_WIDTH), zero_map2)]


def _ssm_prompt(proj3d, sp, tb):
    b, l, _ = proj3d.shape
    nt = l // tb
    dummy_h0 = jnp.zeros((SCAN_ROWS, 2 * SSM_N), F32)
    kern = functools.partial(_ssm_kernel, chained=True, tiles_per_seq=nt, tb=tb)

    def kernel(u_ref, h0_ref, wb, tab, wc, d, wglu, y_ref, hn_ref, bu_scr, carry_scr):
        kern(u_ref.at[0], h0_ref, wb, tab, wc, d, wglu, y_ref, hn_ref, bu_scr, carry_scr)

    return pl.pallas_call(
        kernel,
        grid=(b, nt),
        in_specs=[pl.BlockSpec((1, tb, SSM_WIDTH), lambda bi, i: (bi, i, COL_SSM)),
                  pl.BlockSpec((SCAN_ROWS, 2 * SSM_N), lambda bi, i: (0, 0))]
                 + _ssm_common_specs(lambda bi, i: (0, 0), lambda bi, i: (0, 0, 0)),
        out_specs=[pl.BlockSpec((1, tb, SSM_WIDTH), lambda bi, i: (bi, i, 0)),
                   pl.BlockSpec((1, SCAN_ROWS, 2 * SSM_N), lambda bi, i: (bi, 0, 0))],
        out_shape=[jax.ShapeDtypeStruct((b, l, SSM_WIDTH), BF16),
                   jax.ShapeDtypeStruct((b, SCAN_ROWS, 2 * SSM_N), F32)],
        scratch_shapes=[pltpu.VMEM((tb, 2 * SSM_N), F32), pltpu.VMEM((SCAN_ROWS, 2 * SSM_N), F32)],
        compiler_params=_cparams(("parallel", "arbitrary")),
        name="ssm_prompt",
    )(proj3d, dummy_h0, sp["wb"], sp["tab"], sp["wc"], sp["d"], sp["wglu"])


def _ssm_sample(proj2d, h0, sp):
    rows = proj2d.shape[0]
    nseq = h0.shape[0]
    kern = functools.partial(_ssm_kernel, chained=False, tiles_per_seq=1, tb=rows)
    return pl.pallas_call(
        kern,
        grid=(1,),
        in_specs=[pl.BlockSpec((rows, SSM_WIDTH), lambda i: (0, COL_SSM)),
                  pl.BlockSpec((nseq, 2 * SSM_N), lambda i: (0, 0))]
                 + _ssm_common_specs(lambda i: (0, 0), lambda i: (0, 0, 0)),
        out_specs=[pl.BlockSpec((rows, SSM_WIDTH), lambda i: (0, 0)),
                   pl.BlockSpec((nseq, 2 * SSM_N), lambda i: (0, 0))],
        out_shape=[jax.ShapeDtypeStruct((rows, SSM_WIDTH), BF16),
                   jax.ShapeDtypeStruct((nseq, 2 * SSM_N), F32)],
        scratch_shapes=[pltpu.VMEM((rows, 2 * SSM_N), F32), pltpu.VMEM((SCAN_ROWS, 2 * SSM_N), F32)],
        compiler_params=_cparams(("arbitrary",)),
        name="ssm_sample",
    )(proj2d, h0, sp["wb"], sp["tab"], sp["wc"], sp["d"], sp["wglu"])


def _ssm_params(lam_re, lam_im, log_dt, b_re, b_im, c_re, c_im, d_skip, w_glu):
    lr, li = lam_re.astype(F32), lam_im.astype(F32)
    dt = jnp.exp(log_dt.astype(F32))[:, None]
    mag = jnp.exp(lr * dt)
    ab_re, ab_im = mag * jnp.cos(li * dt), mag * jnp.sin(li * dt)
    den = lr * lr + li * li
    nr = ab_re - 1.0
    f_re = (nr * lr + ab_im * li) / den
    f_im = (ab_im * lr - nr * li) / den
    br, bi = b_re.astype(F32), b_im.astype(F32)
    bb_re = f_re[..., None] * br - f_im[..., None] * bi
    bb_im = f_re[..., None] * bi + f_im[..., None] * br
    eye = jnp.eye(SSM_GROUPS, dtype=F32)

    def in_mat(bb):
        return jnp.einsum('gpc,gh->gchp', bb, eye).reshape(SSM_WIDTH, SSM_N)

    def out_mat(c):
        return jnp.einsum('gcp,gh->gphc', c.astype(F32), eye).reshape(SSM_N, SSM_WIDTH)

    wb = jnp.concatenate([in_mat(bb_re), in_mat(bb_im)], axis=1).astype(BF16)
    wc = jnp.concatenate([out_mat(c_re), -out_mat(c_im)], axis=0).astype(BF16)

    ar, ai = ab_re.reshape(1, SSM_N), ab_im.reshape(1, SSM_N)

    def cmul(x, y):
        return (x[0] * y[0] - x[1] * y[1], x[0] * y[1] + x[1] * y[0])

    pw = [(ar, ai)]
    for _ in range(SCAN_ROWS - 1):
        pw.append(cmul(pw[-1], (ar, ai)))
    row = jnp.arange(SCAN_ROWS)[:, None]
    tabs = []
    for shift in (1, 2, 4):
        for part in pw[shift - 1]:
            tabs.append(jnp.where(row >= shift, part, 0.0))
    tabs.append(jnp.concatenate([p[0] for p in pw], axis=0))
    tabs.append(jnp.concatenate([p[1] for p in pw], axis=0))
    tab = jnp.stack([jnp.broadcast_to(t, (SCAN_ROWS, SSM_N)) for t in tabs])
    return dict(wb=wb, tab=tab, wc=wc, d=d_skip.astype(F32).reshape(1, SSM_WIDTH), wglu=w_glu.astype(BF16))


_RET_G = 1.0 - np.exp2(-5.0 - np.arange(RET_HEADS, dtype=np.float64))


def _ret_consts(chunk, n_rows):
    idx = np.arange(n_rows)
    loc = idx % chunk
    same = (idx[:, None] // chunk) == (idx[None, :] // chunk)
    diff = loc[:, None] - loc[None, :]
    dec = np.where(same & (diff >= 0), _RET_G[:, None, None] ** np.maximum(diff, 0)[None], 0.0)
    qdec = np.repeat((_RET_G[None, :] ** (loc[:, None] + 1.0)), HEAD_DIM, axis=1)
    kdec = np.repeat((_RET_G[None, :] ** (chunk - 1.0 - loc[:, None])), HEAD_DIM, axis=1)
    return (jnp.asarray(dec, F32), jnp.asarray(qdec, F32), jnp.asarray(kdec, F32),
            jnp.asarray(_RET_G ** chunk, F32))


def _rope_tables(pos):
    half = HEAD_DIM // 2
    theta = 1.0 / (ROPE_BASE ** np.linspace(0.0, 1.0, half))
    ang = np.asarray(pos, np.float64)[:, None] * theta[None, :]
    cos = np.repeat(np.cos(ang), 2, axis=1)
    sin = np.repeat(np.sin(ang), 2, axis=1) * np.tile([-1.0, 1.0], half)[None]
    return (jnp.asarray(np.tile(cos, (1, RET_HEADS)), F32), jnp.asarray(np.tile(sin, (1, RET_HEADS)), F32))


def _rotate_pairs(x, cos, sin_signed):
    lane = lax.broadcasted_iota(jnp.int32, x.shape, 1)
    nxt = pltpu.roll(x, x.shape[1] - 1, 1)
    prv = pltpu.roll(x, 1, 1)
    return x * cos + jnp.where(lane % 2 == 0, nxt, prv) * sin_signed


def _ret_head_out(o, gate, norm):
    ms = jnp.mean(o * o, axis=-1, keepdims=True)
    return o * lax.rsqrt(ms + RMS_EPS) * norm * (gate * _sigmoid(gate))


RET_TILE = 512


def _ret_prompt_kernel(gc_ref, q_ref, k_ref, v_ref, g_ref, cos_ref, sin_ref, dec_ref, qdec_ref,
                       kdec_ref, norm_ref, mh_ref, y_ref, r_ref, o_scr):
    @pl.when(pl.program_id(1) == 0)
    def _():
        r_ref[...] = jnp.zeros_like(r_ref)

    cos, sin = cos_ref[...], sin_ref[...]
    q = _rotate_pairs(q_ref[0], cos, sin)
    k = _rotate_pairs(k_ref[0], cos, sin) * (HEAD_DIM ** -0.5)
    qb, kb, vb = q.astype(BF16), k.astype(BF16), v_ref[0].astype(BF16)
    kdb = (k * kdec_ref[...]).astype(BF16)
    for c in range(RET_TILE // RET_CHUNK):
        rows = slice(c * RET_CHUNK, (c + 1) * RET_CHUNK)
        for h in range(RET_HEADS):
            sl = slice(h * HEAD_DIM, (h + 1) * HEAD_DIM)
            qh, vh = qb[rows, sl], vb[rows, sl]
            s = _dot_nt(qh, kb[rows, sl]) * dec_ref[h]
            r = r_ref[0, h]
            o_scr[rows, sl] = _dot(s.astype(BF16), vh) + _dot(qh, r.astype(BF16)) * qdec_ref[rows, sl]
            r_ref[0, h] = gc_ref[h] * r + _dot_tn(kdb[rows, sl], vh)
    g = g_ref[0]
    y_ref[0] = (_head_rms(o_scr[...], mh_ref[...], norm_ref[...]) * (g * _sigmoid(g))).astype(y_ref.dtype)


def _ret_prompt(proj3d, cos, sin, norm):
    b, l, _ = proj3d.shape
    c = RET_CHUNK
    dec, _, _, gc = _ret_consts(c, c)
    _, qdec, kdec, _ = _ret_consts(c, RET_TILE)
    blk = (1, RET_TILE, MIX_BLOCK)
    tspec = pl.BlockSpec((RET_TILE, MIX_BLOCK), lambda bi, i: (i, 0))
    cspec = pl.BlockSpec((RET_TILE, MIX_BLOCK), lambda bi, i: (0, 0))
    return pl.pallas_call(
        _ret_prompt_kernel,
        grid=(b, l // RET_TILE),
        in_specs=[pl.BlockSpec(memory_space=pltpu.SMEM),
                  pl.BlockSpec(blk, lambda bi, i: (bi, i, COL_RET_Q)),
                  pl.BlockSpec(blk, lambda bi, i: (bi, i, COL_RET_K)),
                  pl.BlockSpec(blk, lambda bi, i: (bi, i, COL_RET_V)),
                  pl.BlockSpec(blk, lambda bi, i: (bi, i, COL_RET_G)),
                  tspec, tspec,
                  pl.BlockSpec((RET_HEADS, c, c), lambda bi, i: (0, 0, 0)),
                  cspec, cspec,
                  pl.BlockSpec((1, MIX_BLOCK), lambda bi, i: (0, 0)),
                  pl.BlockSpec((MIX_BLOCK, MIX_BLOCK), lambda bi, i: (0, 0))],
        out_specs=[pl.BlockSpec(blk, lambda bi, i: (bi, i, 0)),
                   pl.BlockSpec((1, RET_HEADS, HEAD_DIM, HEAD_DIM), lambda bi, i: (bi, 0, 0, 0))],
        out_shape=[jax.ShapeDtypeStruct((b, l, MIX_BLOCK), BF16),
                   jax.ShapeDtypeStruct((b, RET_HEADS, HEAD_DIM, HEAD_DIM), F32)],
        scratch_shapes=[pltpu.VMEM((RET_TILE, MIX_BLOCK), F32)],
        compiler_params=_cparams(("parallel", "arbitrary")),
        name="ret_prompt",
    )(gc, proj3d, proj3d, proj3d, proj3d, cos, sin, dec, qdec, kdec, norm, _head_mean_matrix(MIX_BLOCK))


def _ret_sample_kernel(gc_ref, q_ref, k_ref, v_ref, g_ref, cos_ref, sin_ref, dec_ref, qdec_ref,
                       kdec_ref, norm_ref, r0_ref, y_ref, rn_ref, *, n_new, s_blk):
    cos, sin = cos_ref[...], sin_ref[...]
    q = _rotate_pairs(q_ref[...], cos, sin)
    k = _rotate_pairs(k_ref[...], cos, sin) * (HEAD_DIM ** -0.5)
    v = v_ref[...]
    g = g_ref[...]
    kd = k * kdec_ref[...]
    qdec = qdec_ref[...]
    norm = norm_ref[...]
    rows = s_blk * n_new
    seq = lax.broadcasted_iota(jnp.int32, (rows, HEAD_DIM), 0) // n_new
    for h in range(RET_HEADS):
        sl = slice(h * HEAD_DIM, (h + 1) * HEAD_DIM)
        qf, kdf = q[:, sl], kd[:, sl]
        qh, kh, vh = qf.astype(BF16), k[:, sl].astype(BF16), v[:, sl].astype(BF16)
        s = _dot_nt(qh, kh) * dec_ref[h]
        cross = jnp.zeros((rows, HEAD_DIM), F32)
        for si in range(s_blk):
            mine = seq == si
            r = r0_ref[si, h]
            cross = cross + _dot(jnp.where(mine, qf, 0.0).astype(BF16), r.astype(BF16))
            rn_ref[si, h] = gc_ref[h] * r + _dot_tn(jnp.where(mine, kdf, 0.0).astype(BF16), vh)
        o = _dot(s.astype(BF16), vh) + cross * qdec[:, sl]
        y_ref[:, sl] = _ret_head_out(o, g[:, sl], norm[:, sl]).astype(y_ref.dtype)


def _ret_sample(proj2d, r0, cos, sin, norm, n_new, s_blk):
    rows = s_blk * n_new
    nseq = r0.shape[0]
    dec, qdec, kdec, gc = _ret_consts(n_new, rows)
    blk = (rows, MIX_BLOCK)
    cspec = pl.BlockSpec(blk, lambda i: (0, 0))
    rblk = (s_blk, RET_HEADS, HEAD_DIM, HEAD_DIM)
    return pl.pallas_call(
        functools.partial(_ret_sample_kernel, n_new=n_new, s_blk=s_blk),
        grid=(nseq // s_blk,),
        in_specs=[pl.BlockSpec(memory_space=pltpu.SMEM),
                  pl.BlockSpec(blk, lambda i: (i, COL_RET_Q)),
                  pl.BlockSpec(blk, lambda i: (i, COL_RET_K)),
                  pl.BlockSpec(blk, lambda i: (i, COL_RET_V)),
                  pl.BlockSpec(blk, lambda i: (i, COL_RET_G)),
                  cspec, cspec,
                  pl.BlockSpec((RET_HEADS, rows, rows), lambda i: (0, 0, 0)),
                  cspec, cspec,
                  pl.BlockSpec((1, MIX_BLOCK), lambda i: (0, 0)),
                  pl.BlockSpec(rblk, lambda i: (i, 0, 0, 0))],
        out_specs=[pl.BlockSpec(blk, lambda i: (i, 0)),
                   pl.BlockSpec(rblk, lambda i: (i, 0, 0, 0))],
        out_shape=[jax.ShapeDtypeStruct((nseq * n_new, MIX_BLOCK), BF16),
                   jax.ShapeDtypeStruct((nseq, RET_HEADS, HEAD_DIM, HEAD_DIM), F32)],
        compiler_params=_cparams(("parallel",)),
        name="ret_sample",
    )(gc, proj2d, proj2d, proj2d, proj2d, cos, sin, dec, qdec, kdec, norm, r0)


def _block_diag(w):
    g, n, _ = w.shape
    return jnp.einsum('gcd,gh->gchd', w, jnp.eye(g, dtype=w.dtype)).reshape(g * n, g * n)


def _layer_params(l, p):
    return dict(
        norm_mix=p['norm_mix'][l].reshape(1, D_MODEL),
        norm_ffn=p['norm_ffn'][l].reshape(1, D_MODEL),
        w_in=p['w_in'][l].astype(BF16),
        w_out=p['w_out'][l].astype(BF16),
        pool_w=_block_diag(p['pool_w'][l].astype(F32)).astype(BF16),
        pool_scale=p['pool_scale'][l].astype(F32).reshape(1, POOL_WIDTH),
        qg=p['swa_q_norm'][l].astype(F32).reshape(1, HEAD_DIM),
        kg=p['swa_k_norm'][l].astype(F32).reshape(1, HEAD_DIM),
        sinks=p['swa_sinks'][l].astype(F32),
        ssm=_ssm_params(p['ssm_lambda_re'][l], p['ssm_lambda_im'][l], p['ssm_log_dt'][l],
                        p['ssm_b_re'][l], p['ssm_b_im'][l], p['ssm_c_re'][l], p['ssm_c_im'][l],
                        p['ssm_d'][l], p['ssm_w_glu'][l]),
        ret_norm=p['ret_norm'][l].astype(F32).reshape(1, MIX_BLOCK),
    )


def _channel_mix(streams, l, lp, p):
    i = l // 2
    g, w_out = lp['norm_ffn'], lp['w_out']
    if l % 2 == 0:
        wg, wu, wd = (p[k][i].astype(BF16) for k in ('ffn_w_gate', 'ffn_w_up', 'ffn_w_down'))
        return [_out_proj_ffn(x, ys, w_out, g, wg, wu, wd, 512, D_FF // 2) for x, ys in streams]
    wr3 = _router_weights(p['moe_router'][i])
    wg, wu, wd = (p[k][i].astype(BF16) for k in ('moe_w_gate', 'moe_w_up', 'moe_w_down'))
    x1s, routes = zip(*[_out_proj_router(x, ys, w_out, g, wr3, 512) for x, ys in streams])
    return _moe(x1s, routes, g, wg, wu, wd)


def _mix_prompt(x2, b, l, lp, bias, cos, sin):
    proj2 = _norm_matmul(x2, lp['norm_mix'], lp['w_in'], 512)
    proj3 = proj2.reshape(b, l, IN_WIDTH)
    tb = 512
    y_pool = _pool(proj2, COL_POOL, lp['pool_w'], lp['pool_scale'], n_rows=b * l, tb=tb,
                   tiles_per_seq=l // tb, pos0=0)
    y_swa, kn = _swa_prompt(proj3, lp['sinks'], lp['qg'], lp['kg'], bias)
    y_ssm, hn = _ssm_prompt(proj3, lp['ssm'], 512)
    y_ret, rn = _ret_prompt(proj3, cos, sin, lp['ret_norm'])
    ys = (y_pool, y_swa.reshape(b * l, MIX_BLOCK), y_ssm.reshape(b * l, MIX_BLOCK),
          y_ret.reshape(b * l, MIX_BLOCK))
    kw = SWA_KV_HEADS * HEAD_DIM
    hn = hn[:, 0]
    states = (proj3[:, l - POOL_BUF:, :POOL_WIDTH],
              kn.reshape(b, SWA_WINDOW, SWA_KV_HEADS, HEAD_DIM),
              proj3[:, l - SWA_WINDOW:, COL_SWA_KV * MIX_BLOCK + kw:(COL_SWA_KV + 1) * MIX_BLOCK]
              .reshape(b, SWA_WINDOW, SWA_KV_HEADS, HEAD_DIM),
              jnp.stack([hn[:, :SSM_N], hn[:, SSM_N:]], axis=-1).reshape(b, SSM_GROUPS, SSM_STATE, 2),
              rn)
    return ys, states


SAMPLE_SEQ_BLOCK = 16


def _mix_sample(x2, nseq, n_new, start_pos, lp, st, bias, cos, sin):
    state_pool, cache_k, cache_v, state_ssm, state_ret = st
    rows = nseq * n_new
    wb = cache_k.shape[1]
    kw = SWA_KV_HEADS * HEAD_DIM
    ext_rows = POOL_HALO + n_new
    proj2 = _norm_matmul(x2, lp['norm_mix'], lp['w_in'], 512)
    proj3 = proj2.reshape(nseq, n_new, IN_WIDTH)
    u_pool = proj3[:, :, :POOL_WIDTH]
    buf = state_pool.astype(F32)
    ext = jnp.concatenate([jnp.zeros((nseq, POOL_HALO - POOL_BUF, POOL_WIDTH), F32), buf, u_pool], axis=1)
    y_pool = _pool(ext.reshape(nseq * ext_rows, POOL_WIDTH), 0, lp['pool_w'], lp['pool_scale'],
                   n_rows=nseq * ext_rows, tb=nseq * ext_rows, tiles_per_seq=1, pos0=start_pos)
    y_pool = y_pool.reshape(nseq, ext_rows, POOL_WIDTH)[:, POOL_HALO:].reshape(rows, POOL_WIDTH)
    y_swa, nk, nv = _swa_sample(proj3, cache_k.reshape(nseq, wb, kw).astype(F32),
                                cache_v.reshape(nseq, wb, kw).astype(F32),
                                lp['sinks'], lp['qg'], lp['kg'], bias, SAMPLE_SEQ_BLOCK)
    h0 = state_ssm.astype(F32).reshape(nseq, SSM_N, 2)
    h0 = jnp.concatenate([h0[..., 0], h0[..., 1]], axis=1)
    y_ssm, hn = _ssm_sample(proj2, h0, lp['ssm'])
    y_ret, rn = _ret_sample(proj2, state_ret.astype(F32), cos, sin, lp['ret_norm'], n_new, SAMPLE_SEQ_BLOCK)
    ys = (y_pool, y_swa.reshape(rows, MIX_BLOCK), y_ssm, y_ret)
    states = (jnp.concatenate([buf, u_pool], axis=1)[:, -POOL_BUF:],
              nk.reshape(nseq, SWA_WINDOW, SWA_KV_HEADS, HEAD_DIM),
              nv.reshape(nseq, SWA_WINDOW, SWA_KV_HEADS, HEAD_DIM),
              jnp.stack([hn[:, :SSM_N], hn[:, SSM_N:]], axis=-1).reshape(nseq, SSM_GROUPS, SSM_STATE, 2),
              rn)
    return ys, states


def _forward(x_prompt, x_sample, past_len, sample_state, p, rel_bias):
    b, l, d = x_prompt.shape
    nseq, n_new, _ = x_sample.shape
    wb = sample_state[1].shape[2]
    depth = p['norm_mix'].shape[0]
    bias_p = _swa_bias(rel_bias, np.arange(SWA_BLOCK)[:, None] - np.arange(2 * SWA_BLOCK)[None, :] + SWA_BLOCK)
    bias_s = _swa_bias(rel_bias, np.arange(n_new)[:, None] - np.arange(wb + n_new)[None, :] + wb)
    rope_p = _rope_tables(np.arange(l))
    rope_s = _rope_tables(past_len + (np.arange(SAMPLE_SEQ_BLOCK * n_new) % n_new))
    xp = x_prompt.reshape(b * l, d)
    xs = x_sample.reshape(nseq * n_new, d)
    st_p, st_s = [], []
    for li in range(depth):
        lp = _layer_params(li, p)
        yp, sp = _mix_prompt(xp, b, l, lp, bias_p, *rope_p)
        ys, ss = _mix_sample(xs, nseq, n_new, past_len, lp, [s[li] for s in sample_state], bias_s, *rope_s)
        xp, xs = _channel_mix([(xp, yp), (xs, ys)], li, lp, p)
        st_p.append(sp)
        st_s.append(ss)
    outs = [xp.reshape(b, l, d), xs.reshape(nseq, n_new, d)]
    for k in range(5):
        outs.append(jnp.stack([s[k] for s in st_p]))
        outs.append(jnp.stack([s[k] for s in st_s]))
    return tuple(outs)


PAST_LEN = 16384


def kernel(x_prompt, x_sample, state_pool, cache_swa_k, cache_swa_v, state_ssm, state_ret,
           norm_mix, norm_ffn, w_in, w_out, pool_w, pool_scale, swa_q_norm, swa_k_norm, swa_sinks,
           rel_bias, ssm_lambda_re, ssm_lambda_im, ssm_log_dt, ssm_b_re, ssm_b_im, ssm_c_re, ssm_c_im,
           ssm_d, ssm_w_glu, ret_norm, ffn_w_gate, ffn_w_up, ffn_w_down, moe_router, moe_w_gate,
           moe_w_up, moe_w_down):
    p = dict(norm_mix=norm_mix, norm_ffn=norm_ffn, w_in=w_in, w_out=w_out, pool_w=pool_w,
             pool_scale=pool_scale, swa_q_norm=swa_q_norm, swa_k_norm=swa_k_norm, swa_sinks=swa_sinks,
             ssm_lambda_re=ssm_lambda_re, ssm_lambda_im=ssm_lambda_im, ssm_log_dt=ssm_log_dt,
             ssm_b_re=ssm_b_re, ssm_b_im=ssm_b_im, ssm_c_re=ssm_c_re, ssm_c_im=ssm_c_im,
             ssm_d=ssm_d, ssm_w_glu=ssm_w_glu, ret_norm=ret_norm,
             ffn_w_gate=ffn_w_gate, ffn_w_up=ffn_w_up, ffn_w_down=ffn_w_down, moe_router=moe_router,
             moe_w_gate=moe_w_gate, moe_w_up=moe_w_up, moe_w_down=moe_w_down)
    return _forward(x_prompt, x_sample, PAST_LEN,
                    (state_pool, cache_swa_k, cache_swa_v, state_ssm, state_ret), p, rel_bias)
```

```python
import functools
import math

import numpy as np
import jax
import jax.numpy as jnp
from jax import lax
from jax.experimental import pallas as pl
from jax.experimental.pallas import tpu as pltpu

F32 = jnp.float32
BF16 = jnp.bfloat16

D_MODEL = 1024
HEAD_DIM = 64
POOL_WIDTH = 256
POOL_WINDOWS = (2, 4, 8, 16)
POOL_BUF = 15
POOL_HALO = 16
SWA_HEADS = 4
SWA_KV_HEADS = 2
SWA_WINDOW = 128
SWA_BLOCK = 128
SSM_WIDTH = 256
SSM_CH = 16
SSM_GROUPS = 16
SSM_STATE = 64
SSM_N = SSM_GROUPS * SSM_STATE
RET_HEADS = 4
RET_CHUNK = 128
ROPE_BASE = 10000.0
IN_WIDTH = 2048
MIX_BLOCK = 256
D_FF = 2816
N_EXPERTS = 8
T5_BUCKETS = 32
T5_MAX_DIST = 128
RMS_EPS = 1e-6
NEG = -1e30
SUBLANES = 8
SCAN_ROWS = SUBLANES

COL_POOL, COL_SWA_Q, COL_SWA_KV, COL_SSM, COL_RET_Q, COL_RET_K, COL_RET_V, COL_RET_G = range(8)

VMEM_LIMIT = 48 * 1024 * 1024


def _cparams(sem):
    return pltpu.CompilerParams(dimension_semantics=sem, vmem_limit_bytes=VMEM_LIMIT)


def _rms(x, g):
    ms = jnp.mean(x * x, axis=-1, keepdims=True)
    return x * lax.rsqrt(ms + RMS_EPS) * g


def _dot(a, b):
    return jnp.dot(a, b, preferred_element_type=F32)


def _dot_nt(a, b):
    return lax.dot_general(a, b, (((1,), (1,)), ((), ())), preferred_element_type=F32)


def _dot_tn(a, b):
    return lax.dot_general(a, b, (((0,), (0,)), ((), ())), preferred_element_type=F32)


def _sigmoid(x):
    return 1.0 / (1.0 + jnp.exp(-x))


def _norm_matmul_kernel(x_ref, g_ref, w_ref, o_ref):
    h = _rms(x_ref[...], g_ref[...]).astype(BF16)
    o_ref[...] = _dot(h, w_ref[...])


def _norm_matmul(x, g, w, tm):
    t, d = x.shape
    tm = min(tm, t)
    n = w.shape[1]
    return pl.pallas_call(
        _norm_matmul_kernel,
        grid=(t // tm,),
        in_specs=[pl.BlockSpec((tm, d), lambda i: (i, 0)),
                  pl.BlockSpec((1, d), lambda i: (0, 0)),
                  pl.BlockSpec((d, n), lambda i: (0, 0))],
        out_specs=pl.BlockSpec((tm, n), lambda i: (i, 0)),
        out_shape=jax.ShapeDtypeStruct((t, n), F32),
        compiler_params=_cparams(("parallel",)),
        name="norm_matmul",
    )(x, g, w)


def _swiglu_chunk(h, wg, wu, wd):
    a = _dot(h, wg)
    b = _dot(h, wu)
    return _dot((a * _sigmoid(a) * b).astype(BF16), wd)


def _mixed_residual(x_ref, y_refs, w_ref):
    y = jnp.concatenate([y_ref[...] for y_ref in y_refs], axis=1)
    return x_ref[...] + _dot(y, w_ref[...])


def _mix_in_specs(tm, d, imap):
    yspec = pl.BlockSpec((tm, MIX_BLOCK), imap(lambda i: (i, 0)))
    return [pl.BlockSpec((tm, d), imap(lambda i: (i, 0))), yspec, yspec, yspec, yspec,
            pl.BlockSpec((d, d), imap(lambda i: (0, 0)))]


def _out_proj_ffn_kernel(x_ref, y0_ref, y1_ref, y2_ref, y3_ref, wo_ref, g_ref, wg_ref, wu_ref, wd_ref,
                         o_ref, h_scr):
    @pl.when(pl.program_id(1) == 0)
    def _():
        x1 = _mixed_residual(x_ref, (y0_ref, y1_ref, y2_ref, y3_ref), wo_ref)
        h_scr[...] = _rms(x1, g_ref[...]).astype(BF16)
        o_ref[...] = x1

    o_ref[...] += _swiglu_chunk(h_scr[...], wg_ref[...], wu_ref[...], wd_ref[...])


def _out_proj_ffn(x, ys, w_out, g, wg, wu, wd, tm, tf):
    t, d = x.shape
    tm = min(tm, t)
    f = wg.shape[1]
    imap = lambda fn: (lambda i, j: fn(i))
    return pl.pallas_call(
        _out_proj_ffn_kernel,
        grid=(t // tm, f // tf),
        in_specs=_mix_in_specs(tm, d, imap)
                 + [pl.BlockSpec((1, d), lambda i, j: (0, 0)),
                    pl.BlockSpec((d, tf), lambda i, j: (0, j)),
                    pl.BlockSpec((d, tf), lambda i, j: (0, j)),
                    pl.BlockSpec((tf, d), lambda i, j: (j, 0))],
        out_specs=pl.BlockSpec((tm, d), lambda i, j: (i, 0)),
        out_shape=jax.ShapeDtypeStruct((t, d), F32),
        scratch_shapes=[pltpu.VMEM((tm, d), BF16)],
        compiler_params=_cparams(("parallel", "arbitrary")),
        name="out_proj_ffn",
    )(x, *ys, w_out, g, wg, wu, wd)


ROUTE_ID_LANES = (0, 1)
ROUTE_GATE_LANES = (2, 3)


def _split_bf16(x):
    hi = x.astype(BF16)
    return hi, (x - hi.astype(F32)).astype(BF16)


def _out_proj_router_kernel(x_ref, y0_ref, y1_ref, y2_ref, y3_ref, wo_ref, g_ref, wr_ref, x1_ref, c_ref):
    x1 = _mixed_residual(x_ref, (y0_ref, y1_ref, y2_ref, y3_ref), wo_ref)
    x1_ref[...] = x1
    h_hi, h_lo = _split_bf16(_rms(x1, g_ref[...]))
    logits = _dot(jnp.concatenate([h_hi, h_lo, h_hi], axis=1), wr_ref[...])
    lane = lax.broadcasted_iota(jnp.int32, logits.shape, 1).astype(F32)
    lg = jnp.where(lane < N_EXPERTS, logits, NEG)
    m1 = jnp.max(lg, axis=-1, keepdims=True)
    i1 = jnp.min(jnp.where(lg == m1, lane, 128.0), axis=-1, keepdims=True)
    lg2 = jnp.where(lane == i1, NEG, lg)
    m2 = jnp.max(lg2, axis=-1, keepdims=True)
    i2 = jnp.min(jnp.where(lg2 == m2, lane, 128.0), axis=-1, keepdims=True)
    ex = jnp.exp(m2 - m1)
    vals = (i1, i2, 1.0 / (1.0 + ex), ex / (1.0 + ex))
    out = jnp.zeros_like(logits)
    for ln, v in zip(ROUTE_ID_LANES + ROUTE_GATE_LANES, vals):
        out = jnp.where(lane == ln, v, out)
    c_ref[...] = out


def _router_weights(wr):
    w = jnp.pad(wr.astype(F32), ((0, 0), (0, 128 - N_EXPERTS)))
    hi, lo = _split_bf16(w)
    return jnp.concatenate([hi, hi, lo], axis=0)


def _out_proj_router(x, ys, w_out, g, wr3, tm):
    t, d = x.shape
    tm = min(tm, t)
    imap = lambda fn: fn
    return pl.pallas_call(
        _out_proj_router_kernel,
        grid=(t // tm,),
        in_specs=_mix_in_specs(tm, d, imap)
                 + [pl.BlockSpec((1, d), lambda i: (0, 0)),
                    pl.BlockSpec((3 * d, 128), lambda i: (0, 0))],
        out_specs=[pl.BlockSpec((tm, d), lambda i: (i, 0)),
                   pl.BlockSpec((tm, 128), lambda i: (i, 0))],
        out_shape=[jax.ShapeDtypeStruct((t, d), F32), jax.ShapeDtypeStruct((t, 128), F32)],
        compiler_params=_cparams(("parallel",)),
        name="out_proj_router",
    )(x, *ys, w_out, g, wr3)


DMA_ISSUE_UNROLL = 8


def _row_copy(src, i, dst, j, sem):
    return pltpu.make_async_copy(src.at[pl.ds(i, 1)], dst.at[pl.ds(j, 1)], sem)


def _dispatch_kernel(meta_ref, pos_ref, *rest, td, tm, n_tiles, first_step):
    n_streams = len(first_step) - 1
    x_refs, (xs_hbm, zero_scr, sem) = rest[:n_streams], rest[n_streams:]
    step = pl.program_id(0)

    def zero_row(r):
        return _row_copy(zero_scr, 0, xs_hbm, r, sem)

    @pl.when(step == 0)
    def _():
        zero_scr[...] = jnp.zeros_like(zero_scr)
        n_used = meta_ref[2 * N_EXPERTS]
        tile_fills = [(i >= n_used, pltpu.make_async_copy(zero_scr, xs_hbm.at[pl.ds(i * tm, tm)], sem))
                      for i in range(n_tiles)]
        for cond, copy in tile_fills:
            pl.when(cond)(copy.start)
        for e in range(N_EXPERTS):
            lax.fori_loop(meta_ref[e], meta_ref[N_EXPERTS + e], lambda r, c: (zero_row(r).start(), c)[1], 0)
        for e in range(N_EXPERTS):
            lax.fori_loop(meta_ref[e], meta_ref[N_EXPERTS + e], lambda r, c: (zero_row(r).wait(), c)[1], 0)
        for cond, copy in tile_fills:
            pl.when(cond)(copy.wait)

    def scatter(x_ref):
        def issue(j, c):
            for k in range(2):
                _row_copy(x_ref, j, xs_hbm, pos_ref[0, 0, 2 * j + k], sem).start()
            return c

        lax.fori_loop(0, td, issue, 0, unroll=DMA_ISSUE_UNROLL)
        for _ in range(2):
            pltpu.make_async_copy(x_ref, xs_hbm.at[pl.ds(0, td)], sem).wait()

    for s, x_ref in enumerate(x_refs):
        pl.when((step >= first_step[s]) & (step < first_step[s + 1]))(functools.partial(scatter, x_ref))


def _dispatch(xs_list, pos, meta, n_rows, tm, td):
    d = xs_list[0].shape[1]
    td = min([td] + [x.shape[0] for x in xs_list])
    first_step = [0]
    for x in xs_list:
        first_step.append(first_step[-1] + x.shape[0] // td)
    n_steps = first_step[-1]
    pos3 = pos.reshape(n_steps, 1, 2 * td)

    def tile_map(s):
        lo, hi = first_step[s], first_step[s + 1]
        return lambda i, m: (jnp.clip(i, lo, hi - 1) - lo, 0)

    in_specs = [pl.BlockSpec((1, 1, 2 * td), lambda i, m: (i, 0, 0), memory_space=pltpu.SMEM)]
    in_specs += [pl.BlockSpec((td, d), tile_map(s)) for s in range(len(xs_list))]
    return pl.pallas_call(
        functools.partial(_dispatch_kernel, td=td, tm=tm, n_tiles=n_rows // tm, first_step=tuple(first_step)),
        grid_spec=pltpu.PrefetchScalarGridSpec(
            num_scalar_prefetch=1, grid=(n_steps,), in_specs=in_specs,
            out_specs=pl.BlockSpec(memory_space=pl.ANY),
            scratch_shapes=[pltpu.VMEM((tm, d), F32), pltpu.SemaphoreType.DMA]),
        out_shape=jax.ShapeDtypeStruct((n_rows, d), F32),
        compiler_params=_cparams(("arbitrary",)),
        name="moe_dispatch",
    )(meta, pos3, *xs_list)


def _grouped_ffn_kernel(te_ref, nu_ref, x_ref, g_ref, wg_ref, wu_ref, wd_ref, o_ref, h_scr):
    del te_ref
    j = pl.program_id(1)
    used = pl.program_id(0) < nu_ref[0]

    @pl.when(jnp.logical_not(used) & (j == 0))
    def _():
        o_ref[...] = jnp.zeros_like(o_ref)

    @pl.when(used)
    def _():
        @pl.when(j == 0)
        def _():
            h_scr[...] = _rms(x_ref[...], g_ref[...]).astype(BF16)

        y = _swiglu_chunk(h_scr[...], wg_ref[0], wu_ref[0], wd_ref[0])

        @pl.when(j == 0)
        def _():
            o_ref[...] = y

        @pl.when(j > 0)
        def _():
            o_ref[...] += y


def _grouped_ffn(xs, g, tile_expert, n_used, wg, wu, wd, tm, tf):
    r, d = xs.shape
    f = wg.shape[2]
    nj = f // tf

    def row_map(i, j, te, nu):
        return (i, 0)

    def col_of(i, j, nu):
        return jnp.where(i < nu[0], j, nj - 1)

    grid_spec = pltpu.PrefetchScalarGridSpec(
        num_scalar_prefetch=2,
        grid=(r // tm, nj),
        in_specs=[pl.BlockSpec((tm, d), row_map),
                  pl.BlockSpec((1, d), lambda i, j, te, nu: (0, 0)),
                  pl.BlockSpec((1, d, tf), lambda i, j, te, nu: (te[i], 0, col_of(i, j, nu))),
                  pl.BlockSpec((1, d, tf), lambda i, j, te, nu: (te[i], 0, col_of(i, j, nu))),
                  pl.BlockSpec((1, tf, d), lambda i, j, te, nu: (te[i], col_of(i, j, nu), 0))],
        out_specs=pl.BlockSpec((tm, d), row_map),
        scratch_shapes=[pltpu.VMEM((tm, d), BF16)],
    )
    return pl.pallas_call(
        _grouped_ffn_kernel,
        grid_spec=grid_spec,
        out_shape=jax.ShapeDtypeStruct((r, d), F32),
        compiler_params=_cparams(("arbitrary", "arbitrary")),
        name="moe_grouped_ffn",
    )(tile_expert, n_used, xs, g, wg, wu, wd)


def _combine_kernel(pos_ref, pos_next_ref, x_ref, route_ref, ys_hbm, o_ref, buf0, buf1, sems, *, tc):
    step = pl.program_id(0)
    slot = step % 2

    def gather(p_ref, s):
        def issue(j, c):
            _row_copy(ys_hbm, p_ref[0, 0, 2 * j], buf0.at[s], j, sems.at[s]).start()
            _row_copy(ys_hbm, p_ref[0, 0, 2 * j + 1], buf1.at[s], j, sems.at[s]).start()
            return c

        lax.fori_loop(0, tc, issue, 0, unroll=DMA_ISSUE_UNROLL)

    pl.when(step == 0)(functools.partial(gather, pos_ref, 0))
    pl.when(step + 1 < pl.num_programs(0))(functools.partial(gather, pos_next_ref, 1 - slot))
    for buf in (buf0, buf1):
        pltpu.make_async_copy(ys_hbm.at[pl.ds(0, tc)], buf.at[slot], sems.at[slot]).wait()
    route = route_ref[...]
    g0 = route[:, ROUTE_GATE_LANES[0]:ROUTE_GATE_LANES[0] + 1]
    g1 = route[:, ROUTE_GATE_LANES[1]:ROUTE_GATE_LANES[1] + 1]
    o_ref[...] = x_ref[...] + g0 * buf0[slot] + g1 * buf1[slot]


def _combine(x, route, pos, ys, tc):
    t, d = x.shape
    tc = min(tc, t)
    n = t // tc
    pos3 = pos.reshape(n, 1, 2 * tc)
    pos_block = (1, 1, 2 * tc)
    return pl.pallas_call(
        functools.partial(_combine_kernel, tc=tc),
        grid=(n,),
        in_specs=[pl.BlockSpec(pos_block, lambda i: (i, 0, 0), memory_space=pltpu.SMEM),
                  pl.BlockSpec(pos_block, lambda i: (jnp.minimum(i + 1, n - 1), 0, 0), memory_space=pltpu.SMEM),
                  pl.BlockSpec((tc, d), lambda i: (i, 0)),
                  pl.BlockSpec((tc, 128), lambda i: (i, 0)),
                  pl.BlockSpec(memory_space=pl.ANY)],
        out_specs=pl.BlockSpec((tc, d), lambda i: (i, 0)),
        out_shape=jax.ShapeDtypeStruct((t, d), F32),
        scratch_shapes=[pltpu.VMEM((2, tc, d), F32), pltpu.VMEM((2, tc, d), F32),
                        pltpu.SemaphoreType.DMA((2,))],
        compiler_params=_cparams(("arbitrary",)),
        name="moe_combine",
    )(pos3, pos3, x, route, ys)


MOE_TM = 512


def _route_plan(expert_ids, tm):
    flat = expert_ids.reshape(-1)
    a = flat.shape[0]
    onehot = (flat[None, :] == jnp.arange(N_EXPERTS, dtype=jnp.int32)[:, None]).astype(jnp.int32)
    csum = jnp.cumsum(onehot, axis=1)
    counts = csum[:, -1]
    padded = (counts + tm - 1) // tm * tm
    ends = jnp.cumsum(padded)
    offs = ends - padded
    pos = jnp.sum(onehot * (offs[:, None] + csum - 1), axis=0)
    n_tiles = (a + N_EXPERTS * tm) // tm
    tile_start = jnp.arange(n_tiles, dtype=jnp.int32) * tm
    tile_expert = jnp.minimum(jnp.sum(tile_start[:, None] >= ends[None, :], axis=1), N_EXPERTS - 1)
    n_used = (ends[-1] // tm).reshape(1)
    last = jnp.take(tile_expert, n_used[0] - 1)
    tile_expert = jnp.where(tile_start < ends[-1], tile_expert, last)
    meta = jnp.concatenate([offs + counts, ends, n_used]).astype(jnp.int32)
    return pos.astype(jnp.int32), tile_expert.astype(jnp.int32), n_used.astype(jnp.int32), meta, n_tiles * tm


def _moe(xs_list, routes, g, wg, wu, wd):
    ids = jnp.concatenate([r[:, ROUTE_ID_LANES[0]:ROUTE_ID_LANES[1] + 1] for r in routes]).astype(jnp.int32)
    pos, tile_expert, n_used, meta, n_rows = _route_plan(ids, MOE_TM)
    bounds = np.cumsum([0] + [2 * x.shape[0] for x in xs_list])
    pos_list = [pos[lo:hi] for lo, hi in zip(bounds[:-1], bounds[1:])]
    xs = _dispatch(xs_list, pos, meta, n_rows, MOE_TM, 1024)
    ys = _grouped_ffn(xs, g, tile_expert, n_used, wg, wu, wd, MOE_TM, D_FF // 2)
    return [_combine(x, r, ps, ys, 512) for x, r, ps in zip(xs_list, routes, pos_list)]


def _pool_kernel(u_ref, halo_ref, w_ref, scale_ref, o_ref, *, tiles_per_seq, pos0, tb):
    ti = pl.program_id(0) % tiles_per_seq
    u = u_ref[...]
    halo = jnp.where(ti == 0, 0.0, halo_ref[...])
    ext = jnp.concatenate([halo, u], axis=0)
    s2 = ext + pltpu.roll(ext, 1, 0)
    s4 = s2 + pltpu.roll(s2, 2, 0)
    s8 = s4 + pltpu.roll(s4, 4, 0)
    s16 = s8 + pltpu.roll(s8, 8, 0)
    grp = lax.broadcasted_iota(jnp.int32, (tb, POOL_WIDTH), 1) // (POOL_WIDTH // 4)
    row = lax.broadcasted_iota(jnp.int32, (tb, POOL_WIDTH), 0)
    s = jnp.where(grp == 0, s2[POOL_HALO:],
                  jnp.where(grp == 1, s4[POOL_HALO:],
                            jnp.where(grp == 2, s8[POOL_HALO:], s16[POOL_HALO:])))
    win = jnp.where(grp == 0, 2, jnp.where(grp == 1, 4, jnp.where(grp == 2, 8, 16)))
    cnt = jnp.minimum(win, pos0 + ti * tb + row + 1).astype(F32)
    pooled = s / cnt - u
    o_ref[...] = (_dot(pooled.astype(BF16), w_ref[...]) * scale_ref[...]).astype(o_ref.dtype)


def _pool(proj2d, col, w, scale, *, n_rows, tb, tiles_per_seq, pos0):
    per = tb // POOL_HALO
    return pl.pallas_call(
        functools.partial(_pool_kernel, tiles_per_seq=tiles_per_seq, pos0=pos0, tb=tb),
        grid=(n_rows // tb,),
        in_specs=[pl.BlockSpec((tb, POOL_WIDTH), lambda i: (i, col)),
                  pl.BlockSpec((POOL_HALO, POOL_WIDTH), lambda i: (jnp.maximum(i * per - 1, 0), col)),
                  pl.BlockSpec((POOL_WIDTH, POOL_WIDTH), lambda i: (0, 0)),
                  pl.BlockSpec((1, POOL_WIDTH), lambda i: (0, 0))],
        out_specs=pl.BlockSpec((tb, POOL_WIDTH), lambda i: (i, 0)),
        out_shape=jax.ShapeDtypeStruct((n_rows, POOL_WIDTH), BF16),
        compiler_params=_cparams(("parallel",)),
        name="pool",
    )(proj2d, proj2d, w, scale)


def _t5_bucket_np(rel):
    n = np.maximum(rel, 0)
    max_exact = T5_BUCKETS // 2
    nf = np.maximum(n, max_exact).astype(np.float32)
    large = max_exact + (np.log(nf / max_exact) / math.log(T5_MAX_DIST / max_exact)
                         * (T5_BUCKETS - max_exact)).astype(np.int32)
    large = np.minimum(large, T5_BUCKETS - 1)
    return np.where(n < max_exact, n, large)


def _swa_bias(rel_bias, rel):
    valid = (rel >= 0) & (rel < SWA_WINDOW)
    onehot = jnp.asarray(_t5_bucket_np(rel)[..., None] == np.arange(T5_BUCKETS), F32)
    b = jnp.einsum('qsb,bh->hqs', onehot, rel_bias.astype(F32), precision=lax.Precision.HIGHEST)
    return jnp.where(valid[None], b, NEG)


def _softmax_parts(parts, sink):
    m = sink
    for s in parts:
        m = jnp.maximum(m, jnp.max(s, axis=-1, keepdims=True))
    ps = [jnp.exp(s - m) for s in parts]
    denom = jnp.exp(sink - m)
    for p in ps:
        denom = denom + jnp.sum(p, axis=-1, keepdims=True)
    return ps, denom


def _head_mean_matrix(width):
    h = np.arange(width) // HEAD_DIM
    return jnp.asarray((h[:, None] == h[None, :]) / HEAD_DIM, BF16)


def _head_rms(x, mean_mat, g):
    ms = _dot((x * x).astype(BF16), mean_mat)
    return x * lax.rsqrt(ms + RMS_EPS) * g


SWA_TILE = 512


def _swa_prompt_kernel(sinks_ref, q_ref, kv_ref, halo_ref, qg_ref, kg_ref, mq_ref, mk_ref, bias_ref,
                       y_ref, kn_ref):
    has_prev = pl.program_id(1) > 0
    kw = SWA_KV_HEADS * HEAD_DIM
    blk = SWA_BLOCK
    kv = kv_ref[0]
    halo = halo_ref[0]
    k_ext = jnp.concatenate([halo[:, :kw], kv[:, :kw]], axis=0)
    v_ext = jnp.concatenate([halo[:, kw:], kv[:, kw:]], axis=0).astype(BF16)
    kn = _head_rms(k_ext, mk_ref[...], kg_ref[...])
    kn_ref[0] = kn[SWA_TILE:]
    knb = kn.astype(BF16)
    qn = (_head_rms(q_ref[0], mq_ref[...], qg_ref[...]) * (HEAD_DIM ** -0.5)).astype(BF16)
    row = lax.broadcasted_iota(jnp.int32, (2 * blk, 1), 0)
    col = lax.broadcasted_iota(jnp.int32, (2 * blk, 2 * blk), 1)
    for c in range(SWA_TILE // blk):
        rows = slice(c * blk, (c + 1) * blk)
        keys = slice(c * blk, (c + 2) * blk)
        for kh in range(SWA_KV_HEADS):
            ksl = slice(kh * HEAD_DIM, (kh + 1) * HEAD_DIM)
            h0 = 2 * kh
            q2 = jnp.concatenate([qn[rows, h0 * HEAD_DIM:(h0 + 1) * HEAD_DIM],
                                  qn[rows, (h0 + 1) * HEAD_DIM:(h0 + 2) * HEAD_DIM]], axis=0)
            s = _dot_nt(q2, knb[keys, ksl]) + bias_ref[kh]
            if c == 0:
                s = jnp.where(has_prev | (col >= blk), s, NEG)
            sink = jnp.where(row < blk, sinks_ref[h0], sinks_ref[h0 + 1])
            (p,), denom = _softmax_parts((s,), sink)
            o = _dot(p.astype(BF16), v_ext[keys, ksl]) / denom
            y_ref[0, rows, h0 * HEAD_DIM:(h0 + 1) * HEAD_DIM] = o[:blk].astype(y_ref.dtype)
            y_ref[0, rows, (h0 + 1) * HEAD_DIM:(h0 + 2) * HEAD_DIM] = o[blk:].astype(y_ref.dtype)


def _swa_prompt(proj3d, sinks, qg, kg, bias):
    b, l, _ = proj3d.shape
    kw = SWA_KV_HEADS * HEAD_DIM
    per = SWA_TILE // SWA_BLOCK
    tile = (1, SWA_TILE, MIX_BLOCK)
    const2 = lambda bi, i: (0, 0)
    bias2 = bias.reshape(SWA_KV_HEADS, 2 * SWA_BLOCK, 2 * SWA_BLOCK)
    return pl.pallas_call(
        _swa_prompt_kernel,
        grid=(b, l // SWA_TILE),
        in_specs=[pl.BlockSpec(memory_space=pltpu.SMEM),
                  pl.BlockSpec(tile, lambda bi, i: (bi, i, COL_SWA_Q)),
                  pl.BlockSpec(tile, lambda bi, i: (bi, i, COL_SWA_KV)),
                  pl.BlockSpec((1, SWA_BLOCK, MIX_BLOCK),
                               lambda bi, i: (bi, jnp.maximum(i * per - 1, 0), COL_SWA_KV)),
                  pl.BlockSpec((1, MIX_BLOCK), const2),
                  pl.BlockSpec((1, kw), const2),
                  pl.BlockSpec((MIX_BLOCK, MIX_BLOCK), const2),
                  pl.BlockSpec((kw, kw), const2),
                  pl.BlockSpec((SWA_KV_HEADS, 2 * SWA_BLOCK, 2 * SWA_BLOCK), lambda bi, i: (0, 0, 0))],
        out_specs=[pl.BlockSpec(tile, lambda bi, i: (bi, i, 0)),
                   pl.BlockSpec((1, SWA_BLOCK, kw), lambda bi, i: (bi, 0, 0))],
        out_shape=[jax.ShapeDtypeStruct((b, l, MIX_BLOCK), BF16),
                   jax.ShapeDtypeStruct((b, SWA_BLOCK, kw), F32)],
        compiler_params=_cparams(("parallel", "arbitrary")),
        name="swa_prompt",
    )(sinks, proj3d, proj3d, proj3d, jnp.tile(qg, (1, SWA_HEADS)), jnp.tile(kg, (1, SWA_KV_HEADS)),
      _head_mean_matrix(MIX_BLOCK), _head_mean_matrix(kw), bias2)


def _swa_sample_kernel(sinks_ref, q_ref, kv_ref, ck_ref, cv_ref, qg_ref, kg_ref, bias_ref,
                       y_ref, nk_ref, nv_ref, *, n_new):
    q = q_ref[...]
    kv = kv_ref[...]
    ck = ck_ref[...]
    cv = cv_ref[...]
    qg = qg_ref[...]
    kg = kg_ref[...]
    kw = SWA_KV_HEADS * HEAD_DIM
    bdot = functools.partial(jnp.einsum, preferred_element_type=F32)
    for kh in range(SWA_KV_HEADS):
        ksl = slice(kh * HEAD_DIM, (kh + 1) * HEAD_DIM)
        vsl = slice(kw + kh * HEAD_DIM, kw + (kh + 1) * HEAD_DIM)
        kn = _rms(kv[:, :, ksl], kg)
        vn = kv[:, :, vsl]
        nk_ref[:, :SWA_WINDOW - n_new, ksl] = ck[:, n_new:, ksl]
        nk_ref[:, SWA_WINDOW - n_new:, ksl] = kn
        nv_ref[:, :SWA_WINDOW - n_new, ksl] = cv[:, n_new:, ksl]
        nv_ref[:, SWA_WINDOW - n_new:, ksl] = vn
        for gq in range(SWA_HEADS // SWA_KV_HEADS):
            h = kh * (SWA_HEADS // SWA_KV_HEADS) + gq
            hsl = slice(h * HEAD_DIM, (h + 1) * HEAD_DIM)
            qn = _rms(q[:, :, hsl], qg)
            s_c = bdot('sqd,skd->sqk', qn, ck[:, :, ksl]) * (HEAD_DIM ** -0.5) + bias_ref[h, :, :SWA_WINDOW]
            s_n = bdot('sqd,skd->sqk', qn, kn) * (HEAD_DIM ** -0.5) + bias_ref[h, :, SWA_WINDOW:]
            (p_c, p_n), denom = _softmax_parts((s_c, s_n), sinks_ref[h])
            o = bdot('sqk,skd->sqd', p_c, cv[:, :, ksl]) + bdot('sqk,skd->sqd', p_n, vn)
            y_ref[:, :, hsl] = (o / denom).astype(y_ref.dtype)


def _swa_sample(proj3d, cache_k, cache_v, sinks, qg, kg, bias, s_blk):
    nseq, n_new, _ = proj3d.shape
    kw = SWA_KV_HEADS * HEAD_DIM
    blk = (s_blk, n_new, MIX_BLOCK)
    cblk = (s_blk, SWA_WINDOW, kw)
    return pl.pallas_call(
        functools.partial(_swa_sample_kernel, n_new=n_new),
        grid=(nseq // s_blk,),
        in_specs=[pl.BlockSpec(memory_space=pltpu.SMEM),
                  pl.BlockSpec(blk, lambda i: (i, 0, COL_SWA_Q)),
                  pl.BlockSpec(blk, lambda i: (i, 0, COL_SWA_KV)),
                  pl.BlockSpec(cblk, lambda i: (i, 0, 0)),
                  pl.BlockSpec(cblk, lambda i: (i, 0, 0)),
                  pl.BlockSpec((1, HEAD_DIM), lambda i: (0, 0)),
                  pl.BlockSpec((1, HEAD_DIM), lambda i: (0, 0)),
                  pl.BlockSpec((SWA_HEADS, n_new, SWA_WINDOW + n_new), lambda i: (0, 0, 0))],
        out_specs=[pl.BlockSpec(blk, lambda i: (i, 0, 0)),
                   pl.BlockSpec(cblk, lambda i: (i, 0, 0)),
                   pl.BlockSpec(cblk, lambda i: (i, 0, 0))],
        out_shape=[jax.ShapeDtypeStruct((nseq, n_new, MIX_BLOCK), BF16),
                   jax.ShapeDtypeStruct((nseq, SWA_WINDOW, kw), F32),
                   jax.ShapeDtypeStruct((nseq, SWA_WINDOW, kw), F32)],
        compiler_params=_cparams(("parallel",)),
        name="swa_sample",
    )(sinks, proj3d, proj3d, cache_k, cache_v, qg, kg, bias)


def _ssm_kernel(u_ref, h0_ref, wb_ref, tab_ref, wc_ref, d_ref, wglu_ref, y_ref, hn_ref,
                bu_scr, carry_scr, *, chained, tiles_per_seq, tb):
    n = SSM_N
    u = u_ref[...]
    bu_scr[...] = _dot(u.astype(BF16), wb_ref[...])

    if chained:
        @pl.when(pl.program_id(1) % tiles_per_seq == 0)
        def _():
            carry_scr[...] = jnp.zeros_like(carry_scr)

    def tile_scan(r0, cr, ci):
        hr = bu_scr[pl.ds(r0, SCAN_ROWS), :n]
        hi = bu_scr[pl.ds(r0, SCAN_ROWS), n:]
        for k, shift in enumerate((1, 2, 4)):
            ar, ai = tab_ref[2 * k], tab_ref[2 * k + 1]
            sr, si = pltpu.roll(hr, shift, 0), pltpu.roll(hi, shift, 0)
            hr, hi = hr + ar * sr - ai * si, hi + ar * si + ai * sr
        pr, pi = tab_ref[6], tab_ref[7]
        hr, hi = hr + pr * cr - pi * ci, hi + pr * ci + pi * cr
        bu_scr[pl.ds(r0, SCAN_ROWS), :n] = hr
        bu_scr[pl.ds(r0, SCAN_ROWS), n:] = hi
        return hr[SCAN_ROWS - 1:], hi[SCAN_ROWS - 1:]

    if chained:
        def body(t, carry):
            r0 = pl.multiple_of(t * SCAN_ROWS, SCAN_ROWS)
            lr, li = tile_scan(r0, *carry)
            return (jnp.broadcast_to(lr, (SCAN_ROWS, n)), jnp.broadcast_to(li, (SCAN_ROWS, n)))

        cr, ci = lax.fori_loop(0, tb // SCAN_ROWS, body, (carry_scr[:, :n], carry_scr[:, n:]), unroll=2)
        carry_scr[:, :n] = cr
        carry_scr[:, n:] = ci
        hn_ref[0, :, :n] = cr
        hn_ref[0, :, n:] = ci
    else:
        def body(t, _):
            r0 = pl.multiple_of(t * SCAN_ROWS, SCAN_ROWS)
            h0 = h0_ref[pl.ds(t, 1), :]
            cr = jnp.broadcast_to(h0[:, :n], (SCAN_ROWS, n))
            ci = jnp.broadcast_to(h0[:, n:], (SCAN_ROWS, n))
            lr, li = tile_scan(r0, cr, ci)
            hn_ref[pl.ds(t, 1), :n] = lr
            hn_ref[pl.ds(t, 1), n:] = li
            return 0

        lax.fori_loop(0, tb // SCAN_ROWS, body, 0)

    y = _dot(bu_scr[...].astype(BF16), wc_ref[...]) + d_ref[...] * u
    y = 0.5 * y * (1.0 + jnp.tanh(math.sqrt(2.0 / math.pi) * (y + 0.044715 * (y * y * y))))
    y = y * _sigmoid(_dot(y.astype(BF16), wglu_ref[...]))
    if chained:
        y_ref[0] = y.astype(y_ref.dtype)
    else:
        y_ref[...] = y.astype(y_ref.dtype)


def _ssm_common_specs(zero_map2, zero_map3):
    return [pl.BlockSpec((SSM_WIDTH, 2 * SSM_N), zero_map2),
            pl.BlockSpec((8, SCAN_ROWS, SSM_N), zero_map3),
            pl.BlockSpec((2 * SSM_N, SSM_WIDTH), zero_map2),
            pl.BlockSpec((1, SSM_WIDTH), zero_map2),
            pl.BlockSpec((SSM_WIDTH, SSM_WIDTH), zero_map2)]


def _ssm_prompt(proj3d, sp, tb):
    b, l, _ = proj3d.shape
    nt = l // tb
    dummy_h0 = jnp.zeros((SCAN_ROWS, 2 * SSM_N), F32)
    kern = functools.partial(_ssm_kernel, chained=True, tiles_per_seq=nt, tb=tb)

    def kernel(u_ref, h0_ref, wb, tab, wc, d, wglu, y_ref, hn_ref, bu_scr, carry_scr):
        kern(u_ref.at[0], h0_ref, wb, tab, wc, d, wglu, y_ref, hn_ref, bu_scr, carry_scr)

    return pl.pallas_call(
        kernel,
        grid=(b, nt),
        in_specs=[pl.BlockSpec((1, tb, SSM_WIDTH), lambda bi, i: (bi, i, COL_SSM)),
                  pl.BlockSpec((SCAN_ROWS, 2 * SSM_N), lambda bi, i: (0, 0))]
                 + _ssm_common_specs(lambda bi, i: (0, 0), lambda bi, i: (0, 0, 0)),
        out_specs=[pl.BlockSpec((1, tb, SSM_WIDTH), lambda bi, i: (bi, i, 0)),
                   pl.BlockSpec((1, SCAN_ROWS, 2 * SSM_N), lambda bi, i: (bi, 0, 0))],
        out_shape=[jax.ShapeDtypeStruct((b, l, SSM_WIDTH), BF16),
                   jax.ShapeDtypeStruct((b, SCAN_ROWS, 2 * SSM_N), F32)],
        scratch_shapes=[pltpu.VMEM((tb, 2 * SSM_N), F32), pltpu.VMEM((SCAN_ROWS, 2 * SSM_N), F32)],
        compiler_params=_cparams(("parallel", "arbitrary")),
        name="ssm_prompt",
    )(proj3d, dummy_h0, sp["wb"], sp["tab"], sp["wc"], sp["d"], sp["wglu"])


def _ssm_sample(proj2d, h0, sp):
    rows = proj2d.shape[0]
    nseq = h0.shape[0]
    kern = functools.partial(_ssm_kernel, chained=False, tiles_per_seq=1, tb=rows)
    return pl.pallas_call(
        kern,
        grid=(1,),
        in_specs=[pl.BlockSpec((rows, SSM_WIDTH), lambda i: (0, COL_SSM)),
                  pl.BlockSpec((nseq, 2 * SSM_N), lambda i: (0, 0))]
                 + _ssm_common_specs(lambda i: (0, 0), lambda i: (0, 0, 0)),
        out_specs=[pl.BlockSpec((rows, SSM_WIDTH), lambda i: (0, 0)),
                   pl.BlockSpec((nseq, 2 * SSM_N), lambda i: (0, 0))],
        out_shape=[jax.ShapeDtypeStruct((rows, SSM_WIDTH), BF16),
                   jax.ShapeDtypeStruct((nseq, 2 * SSM_N), F32)],
        scratch_shapes=[pltpu.VMEM((rows, 2 * SSM_N), F32), pltpu.VMEM((SCAN_ROWS, 2 * SSM_N), F32)],
        compiler_params=_cparams(("arbitrary",)),
        name="ssm_sample",
    )(proj2d, h0, sp["wb"], sp["tab"], sp["wc"], sp["d"], sp["wglu"])


def _ssm_params(lam_re, lam_im, log_dt, b_re, b_im, c_re, c_im, d_skip, w_glu):
    lr, li = lam_re.astype(F32), lam_im.astype(F32)
    dt = jnp.exp(log_dt.astype(F32))[:, None]
    mag = jnp.exp(lr * dt)
    ab_re, ab_im = mag * jnp.cos(li * dt), mag * jnp.sin(li * dt)
    den = lr * lr + li * li
    nr = ab_re - 1.0
    f_re = (nr * lr + ab_im * li) / den
    f_im = (ab_im * lr - nr * li) / den
    br, bi = b_re.astype(F32), b_im.astype(F32)
    bb_re = f_re[..., None] * br - f_im[..., None] * bi
    bb_im = f_re[..., None] * bi + f_im[..., None] * br
    eye = jnp.eye(SSM_GROUPS, dtype=F32)

    def in_mat(bb):
        return jnp.einsum('gpc,gh->gchp', bb, eye).reshape(SSM_WIDTH, SSM_N)

    def out_mat(c):
        return jnp.einsum('gcp,gh->gphc', c.astype(F32), eye).reshape(SSM_N, SSM_WIDTH)

    wb = jnp.concatenate([in_mat(bb_re), in_mat(bb_im)], axis=1).astype(BF16)
    wc = jnp.concatenate([out_mat(c_re), -out_mat(c_im)], axis=0).astype(BF16)

    ar, ai = ab_re.reshape(1, SSM_N), ab_im.reshape(1, SSM_N)

    def cmul(x, y):
        return (x[0] * y[0] - x[1] * y[1], x[0] * y[1] + x[1] * y[0])

    pw = [(ar, ai)]
    for _ in range(SCAN_ROWS - 1):
        pw.append(cmul(pw[-1], (ar, ai)))
    row = jnp.arange(SCAN_ROWS)[:, None]
    tabs = []
    for shift in (1, 2, 4):
        for part in pw[shift - 1]:
            tabs.append(jnp.where(row >= shift, part, 0.0))
    tabs.append(jnp.concatenate([p[0] for p in pw], axis=0))
    tabs.append(jnp.concatenate([p[1] for p in pw], axis=0))
    tab = jnp.stack([jnp.broadcast_to(t, (SCAN_ROWS, SSM_N)) for t in tabs])
    return dict(wb=wb, tab=tab, wc=wc, d=d_skip.astype(F32).reshape(1, SSM_WIDTH), wglu=w_glu.astype(BF16))


_RET_G = 1.0 - np.exp2(-5.0 - np.arange(RET_HEADS, dtype=np.float64))


def _ret_consts(chunk, n_rows):
    idx = np.arange(n_rows)
    loc = idx % chunk
    same = (idx[:, None] // chunk) == (idx[None, :] // chunk)
    diff = loc[:, None] - loc[None, :]
    dec = np.where(same & (diff >= 0), _RET_G[:, None, None] ** np.maximum(diff, 0)[None], 0.0)
    qdec = np.repeat((_RET_G[None, :] ** (loc[:, None] + 1.0)), HEAD_DIM, axis=1)
    kdec = np.repeat((_RET_G[None, :] ** (chunk - 1.0 - loc[:, None])), HEAD_DIM, axis=1)
    return (jnp.asarray(dec, F32), jnp.asarray(qdec, F32), jnp.asarray(kdec, F32),
            jnp.asarray(_RET_G ** chunk, F32))


def _rope_tables(pos):
    half = HEAD_DIM // 2
    theta = 1.0 / (ROPE_BASE ** np.linspace(0.0, 1.0, half))
    ang = np.asarray(pos, np.float64)[:, None] * theta[None, :]
    cos = np.repeat(np.cos(ang), 2, axis=1)
    sin = np.repeat(np.sin(ang), 2, axis=1) * np.tile([-1.0, 1.0], half)[None]
    return (jnp.asarray(np.tile(cos, (1, RET_HEADS)), F32), jnp.asarray(np.tile(sin, (1, RET_HEADS)), F32))


def _rotate_pairs(x, cos, sin_signed):
    lane = lax.broadcasted_iota(jnp.int32, x.shape, 1)
    nxt = pltpu.roll(x, x.shape[1] - 1, 1)
    prv = pltpu.roll(x, 1, 1)
    return x * cos + jnp.where(lane % 2 == 0, nxt, prv) * sin_signed


def _ret_head_out(o, gate, norm):
    ms = jnp.mean(o * o, axis=-1, keepdims=True)
    return o * lax.rsqrt(ms + RMS_EPS) * norm * (gate * _sigmoid(gate))


RET_TILE = 512


RET_PAIR = 2 * HEAD_DIM


def _rotate_pairs_mxu(x, cos, sin_signed, swap_ref):
    hi, lo = _split_bf16(x)
    swapped = _dot(hi, swap_ref[...]) + _dot(lo, swap_ref[...])
    return x * cos + swapped * sin_signed


def _ret_prompt_kernel(q_ref, k_ref, v_ref, g_ref, cos_ref, sin_ref, swap_ref, dec_ref, qdec_ref,
                       kdec_ref, gc_ref, bmask_ref, norm_ref, mh_ref, y_ref, r_ref, o_scr):
    @pl.when(pl.program_id(1) == 0)
    def _():
        r_ref[...] = jnp.zeros_like(r_ref)

    cos, sin = cos_ref[...], sin_ref[...]
    q = _rotate_pairs_mxu(q_ref[0], cos, sin, swap_ref)
    k = _rotate_pairs_mxu(k_ref[0], cos, sin, swap_ref) * (HEAD_DIM ** -0.5)
    qb, kb, vb = q.astype(BF16), k.astype(BF16), v_ref[0].astype(BF16)
    kdb = (k * kdec_ref[...]).astype(BF16)
    low_half = lax.broadcasted_iota(jnp.int32, (RET_CHUNK, RET_PAIR), 1) < HEAD_DIM
    zero = jnp.zeros((RET_CHUNK, RET_PAIR), BF16)
    for c in range(RET_TILE // RET_CHUNK):
        rows = slice(c * RET_CHUNK, (c + 1) * RET_CHUNK)
        for p in range(RET_HEADS // 2):
            sl = slice(p * RET_PAIR, (p + 1) * RET_PAIR)
            q2, k2, v2 = qb[rows, sl], kb[rows, sl], vb[rows, sl]
            inner = [_dot((_dot_nt(jnp.where(low_half == (hh == 0), q2, zero), k2)
                           * dec_ref[2 * p + hh]).astype(BF16), v2) for hh in range(2)]
            r = r_ref[0, p]
            o_scr[rows, sl] = (jnp.where(low_half, inner[0], inner[1])
                               + _dot(q2, r.astype(BF16)) * qdec_ref[rows, sl])
            r_ref[0, p] = gc_ref[p] * r + _dot_tn(kdb[rows, sl], v2) * bmask_ref[...]
    g = g_ref[0]
    y_ref[0] = (_head_rms(o_scr[...], mh_ref[...], norm_ref[...]) * (g * _sigmoid(g))).astype(y_ref.dtype)


def _ret_prompt(proj3d, cos, sin, norm):
    b, l, _ = proj3d.shape
    c = RET_CHUNK
    n_pairs = RET_HEADS // 2
    dec, _, _, gc = _ret_consts(c, c)
    _, qdec, kdec, _ = _ret_consts(c, RET_TILE)
    gc_lanes = jnp.repeat(gc, HEAD_DIM).reshape(n_pairs, 1, RET_PAIR)
    lane = np.arange(MIX_BLOCK)
    swap = jnp.asarray(lane[:, None] == (lane[None, :] ^ 1), BF16)
    half = np.arange(RET_PAIR) // HEAD_DIM
    bmask = jnp.asarray(half[:, None] == half[None, :], F32)
    blk = (1, RET_TILE, MIX_BLOCK)
    tspec = pl.BlockSpec((RET_TILE, MIX_BLOCK), lambda bi, i: (i, 0))
    cspec = pl.BlockSpec((RET_TILE, MIX_BLOCK), lambda bi, i: (0, 0))
    sq = pl.BlockSpec((MIX_BLOCK, MIX_BLOCK), lambda bi, i: (0, 0))
    y, r2 = pl.pallas_call(
        _ret_prompt_kernel,
        grid=(b, l // RET_TILE),
        in_specs=[pl.BlockSpec(blk, lambda bi, i: (bi, i, COL_RET_Q)),
                  pl.BlockSpec(blk, lambda bi, i: (bi, i, COL_RET_K)),
                  pl.BlockSpec(blk, lambda bi, i: (bi, i, COL_RET_V)),
                  pl.BlockSpec(blk, lambda bi, i: (bi, i, COL_RET_G)),
                  tspec, tspec, sq,
                  pl.BlockSpec((RET_HEADS, c, c), lambda bi, i: (0, 0, 0)),
                  cspec, cspec,
                  pl.BlockSpec((n_pairs, 1, RET_PAIR), lambda bi, i: (0, 0, 0)),
                  pl.BlockSpec((RET_PAIR, RET_PAIR), lambda bi, i: (0, 0)),
                  pl.BlockSpec((1, MIX_BLOCK), lambda bi, i: (0, 0)),
                  sq],
        out_specs=[pl.BlockSpec(blk, lambda bi, i: (bi, i, 0)),
                   pl.BlockSpec((1, n_pairs, RET_PAIR, RET_PAIR), lambda bi, i: (bi, 0, 0, 0))],
        out_shape=[jax.ShapeDtypeStruct((b, l, MIX_BLOCK), BF16),
                   jax.ShapeDtypeStruct((b, n_pairs, RET_PAIR, RET_PAIR), F32)],
        scratch_shapes=[pltpu.VMEM((RET_TILE, MIX_BLOCK), F32)],
        compiler_params=_cparams(("parallel", "arbitrary")),
        name="ret_prompt",
    )(proj3d, proj3d, proj3d, proj3d, cos, sin, swap, dec, qdec, kdec, gc_lanes, bmask, norm,
      _head_mean_matrix(MIX_BLOCK))
    r4 = r2.reshape(b, n_pairs, 2, HEAD_DIM, 2, HEAD_DIM)
    r = jnp.stack([r4[:, :, 0, :, 0, :], r4[:, :, 1, :, 1, :]], axis=2)
    return y, r.reshape(b, RET_HEADS, HEAD_DIM, HEAD_DIM)


def _ret_sample_kernel(gc_ref, q_ref, k_ref, v_ref, g_ref, cos_ref, sin_ref, dec_ref, qdec_ref,
                       kdec_ref, norm_ref, r0_ref, y_ref, rn_ref, *, n_new, s_blk):
    cos, sin = cos_ref[...], sin_ref[...]
    q = _rotate_pairs(q_ref[...], cos, sin)
    k = _rotate_pairs(k_ref[...], cos, sin) * (HEAD_DIM ** -0.5)
    v = v_ref[...]
    g = g_ref[...]
    kd = k * kdec_ref[...]
    qdec = qdec_ref[...]
    norm = norm_ref[...]
    rows = s_blk * n_new
    seq = lax.broadcasted_iota(jnp.int32, (rows, HEAD_DIM), 0) // n_new
    for h in range(RET_HEADS):
        sl = slice(h * HEAD_DIM, (h + 1) * HEAD_DIM)
        qf, kdf = q[:, sl], kd[:, sl]
        qh, kh, vh = qf.astype(BF16), k[:, sl].astype(BF16), v[:, sl].astype(BF16)
        s = _dot_nt(qh, kh) * dec_ref[h]
        cross = jnp.zeros((rows, HEAD_DIM), F32)
        for si in range(s_blk):
            mine = seq == si
            r = r0_ref[si, h]
            cross = cross + _dot(jnp.where(mine, qf, 0.0).astype(BF16), r.astype(BF16))
            rn_ref[si, h] = gc_ref[h] * r + _dot_tn(jnp.where(mine, kdf, 0.0).astype(BF16), vh)
        o = _dot(s.astype(BF16), vh) + cross * qdec[:, sl]
        y_ref[:, sl] = _ret_head_out(o, g[:, sl], norm[:, sl]).astype(y_ref.dtype)


def _ret_sample(proj2d, r0, cos, sin, norm, n_new, s_blk):
    rows = s_blk * n_new
    nseq = r0.shape[0]
    dec, qdec, kdec, gc = _ret_consts(n_new, rows)
    blk = (rows, MIX_BLOCK)
    cspec = pl.BlockSpec(blk, lambda i: (0, 0))
    rblk = (s_blk, RET_HEADS, HEAD_DIM, HEAD_DIM)
    return pl.pallas_call(
        functools.partial(_ret_sample_kernel, n_new=n_new, s_blk=s_blk),
        grid=(nseq // s_blk,),
        in_specs=[pl.BlockSpec(memory_space=pltpu.SMEM),
                  pl.BlockSpec(blk, lambda i: (i, COL_RET_Q)),
                  pl.BlockSpec(blk, lambda i: (i, COL_RET_K)),
                  pl.BlockSpec(blk, lambda i: (i, COL_RET_V)),
                  pl.BlockSpec(blk, lambda i: (i, COL_RET_G)),
                  cspec, cspec,
                  pl.BlockSpec((RET_HEADS, rows, rows), lambda i: (0, 0, 0)),
                  cspec, cspec,
                  pl.BlockSpec((1, MIX_BLOCK), lambda i: (0, 0)),
                  pl.BlockSpec(rblk, lambda i: (i, 0, 0, 0))],
        out_specs=[pl.BlockSpec(blk, lambda i: (i, 0)),
                   pl.BlockSpec(rblk, lambda i: (i, 0, 0, 0))],
        out_shape=[jax.ShapeDtypeStruct((nseq * n_new, MIX_BLOCK), BF16),
                   jax.ShapeDtypeStruct((nseq, RET_HEADS, HEAD_DIM, HEAD_DIM), F32)],
        compiler_params=_cparams(("parallel",)),
        name="ret_sample",
    )(gc, proj2d, proj2d, proj2d, proj2d, cos, sin, dec, qdec, kdec, norm, r0)


def _block_diag(w):
    g, n, _ = w.shape
    return jnp.einsum('gcd,gh->gchd', w, jnp.eye(g, dtype=w.dtype)).reshape(g * n, g * n)


def _layer_params(l, p):
    return dict(
        norm_mix=p['norm_mix'][l].reshape(1, D_MODEL),
        norm_ffn=p['norm_ffn'][l].reshape(1, D_MODEL),
        w_in=p['w_in'][l].astype(BF16),
        w_out=p['w_out'][l].astype(BF16),
        pool_w=_block_diag(p['pool_w'][l].astype(F32)).astype(BF16),
        pool_scale=p['pool_scale'][l].astype(F32).reshape(1, POOL_WIDTH),
        qg=p['swa_q_norm'][l].astype(F32).reshape(1, HEAD_DIM),
        kg=p['swa_k_norm'][l].astype(F32).reshape(1, HEAD_DIM),
        sinks=p['swa_sinks'][l].astype(F32),
        ssm=_ssm_params(p['ssm_lambda_re'][l], p['ssm_lambda_im'][l], p['ssm_log_dt'][l],
                        p['ssm_b_re'][l], p['ssm_b_im'][l], p['ssm_c_re'][l], p['ssm_c_im'][l],
                        p['ssm_d'][l], p['ssm_w_glu'][l]),
        ret_norm=p['ret_norm'][l].astype(F32).reshape(1, MIX_BLOCK),
    )


def _channel_mix(streams, l, lp, p):
    i = l // 2
    g, w_out = lp['norm_ffn'], lp['w_out']
    if l % 2 == 0:
        wg, wu, wd = (p[k][i].astype(BF16) for k in ('ffn_w_gate', 'ffn_w_up', 'ffn_w_down'))
        return [_out_proj_ffn(x, ys, w_out, g, wg, wu, wd, 512, D_FF // 2) for x, ys in streams]
    wr3 = _router_weights(p['moe_router'][i])
    wg, wu, wd = (p[k][i].astype(BF16) for k in ('moe_w_gate', 'moe_w_up', 'moe_w_down'))
    x1s, routes = zip(*[_out_proj_router(x, ys, w_out, g, wr3, 512) for x, ys in streams])
    return _moe(x1s, routes, g, wg, wu, wd)


def _mix_prompt(x2, b, l, lp, bias, cos, sin):
    proj2 = _norm_matmul(x2, lp['norm_mix'], lp['w_in'], 512)
    proj3 = proj2.reshape(b, l, IN_WIDTH)
    tb = 512
    y_pool = _pool(proj2, COL_POOL, lp['pool_w'], lp['pool_scale'], n_rows=b * l, tb=tb,
                   tiles_per_seq=l // tb, pos0=0)
    y_swa, kn = _swa_prompt(proj3, lp['sinks'], lp['qg'], lp['kg'], bias)
    y_ssm, hn = _ssm_prompt(proj3, lp['ssm'], 512)
    y_ret, rn = _ret_prompt(proj3, cos, sin, lp['ret_norm'])
    ys = (y_pool, y_swa.reshape(b * l, MIX_BLOCK), y_ssm.reshape(b * l, MIX_BLOCK),
          y_ret.reshape(b * l, MIX_BLOCK))
    kw = SWA_KV_HEADS * HEAD_DIM
    hn = hn[:, 0]
    states = (proj3[:, l - POOL_BUF:, :POOL_WIDTH],
              kn.reshape(b, SWA_WINDOW, SWA_KV_HEADS, HEAD_DIM),
              proj3[:, l - SWA_WINDOW:, COL_SWA_KV * MIX_BLOCK + kw:(COL_SWA_KV + 1) * MIX_BLOCK]
              .reshape(b, SWA_WINDOW, SWA_KV_HEADS, HEAD_DIM),
              jnp.stack([hn[:, :SSM_N], hn[:, SSM_N:]], axis=-1).reshape(b, SSM_GROUPS, SSM_STATE, 2),
              rn)
    return ys, states


SAMPLE_SEQ_BLOCK = 16


def _mix_sample(x2, nseq, n_new, start_pos, lp, st, bias, cos, sin):
    state_pool, cache_k, cache_v, state_ssm, state_ret = st
    rows = nseq * n_new
    wb = cache_k.shape[1]
    kw = SWA_KV_HEADS * HEAD_DIM
    ext_rows = POOL_HALO + n_new
    proj2 = _norm_matmul(x2, lp['norm_mix'], lp['w_in'], 512)
    proj3 = proj2.reshape(nseq, n_new, IN_WIDTH)
    u_pool = proj3[:, :, :POOL_WIDTH]
    buf = state_pool.astype(F32)
    ext = jnp.concatenate([jnp.zeros((nseq, POOL_HALO - POOL_BUF, POOL_WIDTH), F32), buf, u_pool], axis=1)
    y_pool = _pool(ext.reshape(nseq * ext_rows, POOL_WIDTH), 0, lp['pool_w'], lp['pool_scale'],
                   n_rows=nseq * ext_rows, tb=nseq * ext_rows, tiles_per_seq=1, pos0=start_pos)
    y_pool = y_pool.reshape(nseq, ext_rows, POOL_WIDTH)[:, POOL_HALO:].reshape(rows, POOL_WIDTH)
    y_swa, nk, nv = _swa_sample(proj3, cache_k.reshape(nseq, wb, kw).astype(F32),
                                cache_v.reshape(nseq, wb, kw).astype(F32),
                                lp['sinks'], lp['qg'], lp['kg'], bias, SAMPLE_SEQ_BLOCK)
    h0 = state_ssm.astype(F32).reshape(nseq, SSM_N, 2)
    h0 = jnp.concatenate([h0[..., 0], h0[..., 1]], axis=1)
    y_ssm, hn = _ssm_sample(proj2, h0, lp['ssm'])
    y_ret, rn = _ret_sample(proj2, state_ret.astype(F32), cos, sin, lp['ret_norm'], n_new, SAMPLE_SEQ_BLOCK)
    ys = (y_pool, y_swa.reshape(rows, MIX_BLOCK), y_ssm, y_ret)
    states = (jnp.concatenate([buf, u_pool], axis=1)[:, -POOL_BUF:],
              nk.reshape(nseq, SWA_WINDOW, SWA_KV_HEADS, HEAD_DIM),
              nv.reshape(nseq, SWA_WINDOW, SWA_KV_HEADS, HEAD_DIM),
              jnp.stack([hn[:, :SSM_N], hn[:, SSM_N:]], axis=-1).reshape(nseq, SSM_GROUPS, SSM_STATE, 2),
              rn)
    return ys, states


def _forward(x_prompt, x_sample, past_len, sample_state, p, rel_bias):
    b, l, d = x_prompt.shape
    nseq, n_new, _ = x_sample.shape
    wb = sample_state[1].shape[2]
    depth = p['norm_mix'].shape[0]
    bias_p = _swa_bias(rel_bias, np.arange(SWA_BLOCK)[:, None] - np.arange(2 * SWA_BLOCK)[None, :] + SWA_BLOCK)
    bias_s = _swa_bias(rel_bias, np.arange(n_new)[:, None] - np.arange(wb + n_new)[None, :] + wb)
    rope_p = _rope_tables(np.arange(l))
    rope_s = _rope_tables(past_len + (np.arange(SAMPLE_SEQ_BLOCK * n_new) % n_new))
    xp = x_prompt.reshape(b * l, d)
    xs = x_sample.reshape(nseq * n_new, d)
    st_p, st_s = [], []
    for li in range(depth):
        lp = _layer_params(li, p)
        yp, sp = _mix_prompt(xp, b, l, lp, bias_p, *rope_p)
        ys, ss = _mix_sample(xs, nseq, n_new, past_len, lp, [s[li] for s in sample_state], bias_s, *rope_s)
        xp, xs = _channel_mix([(xp, yp), (xs, ys)], li, lp, p)
        st_p.append(sp)
        st_s.append(ss)
    outs = [xp.reshape(b, l, d), xs.reshape(nseq, n_new, d)]
    for k in range(5):
        outs.append(jnp.stack([s[k] for s in st_p]))
        outs.append(jnp.stack([s[k] for s in st_s]))
    return tuple(outs)


PAST_LEN = 16384


def kernel(x_prompt, x_sample, state_pool, cache_swa_k, cache_swa_v, state_ssm, state_ret,
           norm_mix, norm_ffn, w_in, w_out, pool_w, pool_scale, swa_q_norm, swa_k_norm, swa_sinks,
           rel_bias, ssm_lambda_re, ssm_lambda_im, ssm_log_dt, ssm_b_re, ssm_b_im, ssm_c_re, ssm_c_im,
           ssm_d, ssm_w_glu, ret_norm, ffn_w_gate, ffn_w_up, ffn_w_down, moe_router, moe_w_gate,
           moe_w_up, moe_w_down):
    p = dict(norm_mix=norm_mix, norm_ffn=norm_ffn, w_in=w_in, w_out=w_out, pool_w=pool_w,
             pool_scale=pool_scale, swa_q_norm=swa_q_norm, swa_k_norm=swa_k_norm, swa_sinks=swa_sinks,
             ssm_lambda_re=ssm_lambda_re, ssm_lambda_im=ssm_lambda_im, ssm_log_dt=ssm_log_dt,
             ssm_b_re=ssm_b_re, ssm_b_im=ssm_b_im, ssm_c_re=ssm_c_re, ssm_c_im=ssm_c_im,
             ssm_d=ssm_d, ssm_w_glu=ssm_w_glu, ret_norm=ret_norm,
             ffn_w_gate=ffn_w_gate, ffn_w_up=ffn_w_up, ffn_w_down=ffn_w_down, moe_router=moe_router,
             moe_w_gate=moe_w_gate, moe_w_up=moe_w_up, moe_w_down=moe_w_down)
    return _forward(x_prompt, x_sample, PAST_LEN,
                    (state_pool, cache_swa_k, cache_swa_v, state_ssm, state_ret), p, rel_bias)
```

```python
import functools
import math

import numpy as np
import jax
import jax.numpy as jnp
from jax import lax
from jax.experimental import pallas as pl
from jax.experimental.pallas import tpu as pltpu

F32 = jnp.float32
BF16 = jnp.bfloat16

D_MODEL = 1024
HEAD_DIM = 64
POOL_WIDTH = 256
POOL_WINDOWS = (2, 4, 8, 16)
POOL_BUF = 15
POOL_HALO = 16
SWA_HEADS = 4
SWA_KV_HEADS = 2
SWA_WINDOW = 128
SWA_BLOCK = 128
SSM_WIDTH = 256
SSM_CH = 16
SSM_GROUPS = 16
SSM_STATE = 64
SSM_N = SSM_GROUPS * SSM_STATE
RET_HEADS = 4
RET_CHUNK = 128
ROPE_BASE = 10000.0
IN_WIDTH = 2048
MIX_BLOCK = 256
D_FF = 2816
N_EXPERTS = 8
T5_BUCKETS = 32
T5_MAX_DIST = 128
RMS_EPS = 1e-6
NEG = -1e30
SUBLANES = 8
SCAN_ROWS = SUBLANES

COL_POOL, COL_SWA_Q, COL_SWA_KV, COL_SSM, COL_RET_Q, COL_RET_K, COL_RET_V, COL_RET_G = range(8)

VMEM_LIMIT = 48 * 1024 * 1024
FFN_VMEM_LIMIT = 58 * 1024 * 1024


def _cparams(sem, vmem=VMEM_LIMIT):
    return pltpu.CompilerParams(dimension_semantics=sem, vmem_limit_bytes=vmem)


def _rms(x, g):
    ms = jnp.mean(x * x, axis=-1, keepdims=True)
    return x * lax.rsqrt(ms + RMS_EPS) * g


def _dot(a, b):
    return jnp.dot(a, b, preferred_element_type=F32)


def _dot_nt(a, b):
    return lax.dot_general(a, b, (((1,), (1,)), ((), ())), preferred_element_type=F32)


def _dot_tn(a, b):
    return lax.dot_general(a, b, (((0,), (0,)), ((), ())), preferred_element_type=F32)


def _sigmoid(x):
    return 1.0 / (1.0 + jnp.exp(-x))


def _norm_matmul_kernel(x_ref, g_ref, w_ref, o_ref):
    h = _rms(x_ref[...], g_ref[...]).astype(BF16)
    o_ref[...] = _dot(h, w_ref[...])


def _norm_matmul(x, g, w, tm):
    t, d = x.shape
    tm = min(tm, t)
    n = w.shape[1]
    return pl.pallas_call(
        _norm_matmul_kernel,
        grid=(t // tm,),
        in_specs=[pl.BlockSpec((tm, d), lambda i: (i, 0)),
                  pl.BlockSpec((1, d), lambda i: (0, 0)),
                  pl.BlockSpec((d, n), lambda i: (0, 0))],
        out_specs=pl.BlockSpec((tm, n), lambda i: (i, 0)),
        out_shape=jax.ShapeDtypeStruct((t, n), F32),
        compiler_params=_cparams(("parallel",)),
        name="norm_matmul",
    )(x, g, w)


FFN_SUBCHUNK = 512


def _swiglu_chunk(h, wg_ref, wu_ref, wd_ref):
    tf = wg_ref.shape[1]
    y = None
    for lo in range(0, tf, FFN_SUBCHUNK):
        hi = min(lo + FFN_SUBCHUNK, tf)
        a = _dot(h, wg_ref[:, lo:hi])
        b = _dot(h, wu_ref[:, lo:hi])
        part = _dot((a * _sigmoid(a) * b).astype(BF16), wd_ref[lo:hi, :])
        y = part if y is None else y + part
    return y


def _mixed_residual(x_ref, y_refs, w_ref):
    y = jnp.concatenate([y_ref[...] for y_ref in y_refs], axis=1)
    return x_ref[...] + _dot(y, w_ref[...])


def _mix_in_specs(tm, d, imap):
    yspec = pl.BlockSpec((tm, MIX_BLOCK), imap(lambda i: (i, 0)))
    return [pl.BlockSpec((tm, d), imap(lambda i: (i, 0))), yspec, yspec, yspec, yspec,
            pl.BlockSpec((d, d), imap(lambda i: (0, 0)))]


def _out_proj_ffn_kernel(x_ref, y0_ref, y1_ref, y2_ref, y3_ref, wo_ref, g_ref, wg_ref, wu_ref, wd_ref,
                         o_ref, h_scr):
    @pl.when(pl.program_id(1) == 0)
    def _():
        x1 = _mixed_residual(x_ref, (y0_ref, y1_ref, y2_ref, y3_ref), wo_ref)
        h_scr[...] = _rms(x1, g_ref[...]).astype(BF16)
        o_ref[...] = x1

    o_ref[...] += _swiglu_chunk(h_scr[...], wg_ref, wu_ref, wd_ref)


def _out_proj_ffn(x, ys, w_out, g, wg, wu, wd, tm, tf):
    t, d = x.shape
    tm = min(tm, t)
    f = wg.shape[1]
    imap = lambda fn: (lambda i, j: fn(i))
    once = dict(pipeline_mode=pl.Buffered(1)) if tf == f else {}
    return pl.pallas_call(
        _out_proj_ffn_kernel,
        grid=(t // tm, f // tf),
        in_specs=_mix_in_specs(tm, d, imap)
                 + [pl.BlockSpec((1, d), lambda i, j: (0, 0)),
                    pl.BlockSpec((d, tf), lambda i, j: (0, j), **once),
                    pl.BlockSpec((d, tf), lambda i, j: (0, j), **once),
                    pl.BlockSpec((tf, d), lambda i, j: (j, 0), **once)],
        out_specs=pl.BlockSpec((tm, d), lambda i, j: (i, 0)),
        out_shape=jax.ShapeDtypeStruct((t, d), F32),
        scratch_shapes=[pltpu.VMEM((tm, d), BF16)],
        compiler_params=_cparams(("parallel", "arbitrary"), FFN_VMEM_LIMIT),
        name="out_proj_ffn",
    )(x, *ys, w_out, g, wg, wu, wd)


ROUTE_ID_LANES = (0, 1)
ROUTE_GATE_LANES = (2, 3)


def _split_bf16(x):
    hi = x.astype(BF16)
    return hi, (x - hi.astype(F32)).astype(BF16)


def _out_proj_router_kernel(x_ref, y0_ref, y1_ref, y2_ref, y3_ref, wo_ref, g_ref, wr_ref, x1_ref, c_ref):
    x1 = _mixed_residual(x_ref, (y0_ref, y1_ref, y2_ref, y3_ref), wo_ref)
    x1_ref[...] = x1
    h_hi, h_lo = _split_bf16(_rms(x1, g_ref[...]))
    logits = _dot(jnp.concatenate([h_hi, h_lo, h_hi], axis=1), wr_ref[...])
    lane = lax.broadcasted_iota(jnp.int32, logits.shape, 1).astype(F32)
    lg = jnp.where(lane < N_EXPERTS, logits, NEG)
    m1 = jnp.max(lg, axis=-1, keepdims=True)
    i1 = jnp.min(jnp.where(lg == m1, lane, 128.0), axis=-1, keepdims=True)
    lg2 = jnp.where(lane == i1, NEG, lg)
    m2 = jnp.max(lg2, axis=-1, keepdims=True)
    i2 = jnp.min(jnp.where(lg2 == m2, lane, 128.0), axis=-1, keepdims=True)
    ex = jnp.exp(m2 - m1)
    vals = (i1, i2, 1.0 / (1.0 + ex), ex / (1.0 + ex))
    out = jnp.zeros_like(logits)
    for ln, v in zip(ROUTE_ID_LANES + ROUTE_GATE_LANES, vals):
        out = jnp.where(lane == ln, v, out)
    c_ref[...] = out


def _router_weights(wr):
    w = jnp.pad(wr.astype(F32), ((0, 0), (0, 128 - N_EXPERTS)))
    hi, lo = _split_bf16(w)
    return jnp.concatenate([hi, hi, lo], axis=0)


def _out_proj_router(x, ys, w_out, g, wr3, tm):
    t, d = x.shape
    tm = min(tm, t)
    imap = lambda fn: fn
    return pl.pallas_call(
        _out_proj_router_kernel,
        grid=(t // tm,),
        in_specs=_mix_in_specs(tm, d, imap)
                 + [pl.BlockSpec((1, d), lambda i: (0, 0)),
                    pl.BlockSpec((3 * d, 128), lambda i: (0, 0))],
        out_specs=[pl.BlockSpec((tm, d), lambda i: (i, 0)),
                   pl.BlockSpec((tm, 128), lambda i: (i, 0))],
        out_shape=[jax.ShapeDtypeStruct((t, d), F32), jax.ShapeDtypeStruct((t, 128), F32)],
        compiler_params=_cparams(("parallel",)),
        name="out_proj_router",
    )(x, *ys, w_out, g, wr3)


DMA_ISSUE_UNROLL = 8


def _row_copy(src, i, dst, j, sem):
    return pltpu.make_async_copy(src.at[pl.ds(i, 1)], dst.at[pl.ds(j, 1)], sem)


def _dispatch_kernel(meta_ref, pos_ref, *rest, td, tm, n_tiles, first_step):
    n_streams = len(first_step) - 1
    x_refs, (xs_hbm, zero_scr, sem) = rest[:n_streams], rest[n_streams:]
    step = pl.program_id(0)

    def zero_row(r):
        return _row_copy(zero_scr, 0, xs_hbm, r, sem)

    @pl.when(step == 0)
    def _():
        zero_scr[...] = jnp.zeros_like(zero_scr)
        n_used = meta_ref[2 * N_EXPERTS]
        tile_fills = [(i >= n_used, pltpu.make_async_copy(zero_scr, xs_hbm.at[pl.ds(i * tm, tm)], sem))
                      for i in range(n_tiles)]
        for cond, copy in tile_fills:
            pl.when(cond)(copy.start)
        for e in range(N_EXPERTS):
            lax.fori_loop(meta_ref[e], meta_ref[N_EXPERTS + e], lambda r, c: (zero_row(r).start(), c)[1], 0)
        for e in range(N_EXPERTS):
            lax.fori_loop(meta_ref[e], meta_ref[N_EXPERTS + e], lambda r, c: (zero_row(r).wait(), c)[1], 0)
        for cond, copy in tile_fills:
            pl.when(cond)(copy.wait)

    def scatter(x_ref):
        def issue(j, c):
            for k in range(2):
                _row_copy(x_ref, j, xs_hbm, pos_ref[0, 0, 2 * j + k], sem).start()
            return c

        lax.fori_loop(0, td, issue, 0, unroll=DMA_ISSUE_UNROLL)
        for _ in range(2):
            pltpu.make_async_copy(x_ref, xs_hbm.at[pl.ds(0, td)], sem).wait()

    for s, x_ref in enumerate(x_refs):
        pl.when((step >= first_step[s]) & (step < first_step[s + 1]))(functools.partial(scatter, x_ref))


def _dispatch(xs_list, pos, meta, n_rows, tm, td):
    d = xs_list[0].shape[1]
    td = min([td] + [x.shape[0] for x in xs_list])
    first_step = [0]
    for x in xs_list:
        first_step.append(first_step[-1] + x.shape[0] // td)
    n_steps = first_step[-1]
    pos3 = pos.reshape(n_steps, 1, 2 * td)

    def tile_map(s):
        lo, hi = first_step[s], first_step[s + 1]
        return lambda i, m: (jnp.clip(i, lo, hi - 1) - lo, 0)

    in_specs = [pl.BlockSpec((1, 1, 2 * td), lambda i, m: (i, 0, 0), memory_space=pltpu.SMEM)]
    in_specs += [pl.BlockSpec((td, d), tile_map(s)) for s in range(len(xs_list))]
    return pl.pallas_call(
        functools.partial(_dispatch_kernel, td=td, tm=tm, n_tiles=n_rows // tm, first_step=tuple(first_step)),
        grid_spec=pltpu.PrefetchScalarGridSpec(
            num_scalar_prefetch=1, grid=(n_steps,), in_specs=in_specs,
            out_specs=pl.BlockSpec(memory_space=pl.ANY),
            scratch_shapes=[pltpu.VMEM((tm, d), F32), pltpu.SemaphoreType.DMA]),
        out_shape=jax.ShapeDtypeStruct((n_rows, d), F32),
        compiler_params=_cparams(("arbitrary",)),
        name="moe_dispatch",
    )(meta, pos3, *xs_list)


def _grouped_ffn_kernel(te_ref, nu_ref, x_ref, g_ref, wg_ref, wu_ref, wd_ref, o_ref, h_scr):
    del te_ref
    j = pl.program_id(1)
    used = pl.program_id(0) < nu_ref[0]

    @pl.when(jnp.logical_not(used) & (j == 0))
    def _():
        o_ref[...] = jnp.zeros_like(o_ref)

    @pl.when(used)
    def _():
        @pl.when(j == 0)
        def _():
            h_scr[...] = _rms(x_ref[...], g_ref[...]).astype(BF16)

        y = _swiglu_chunk(h_scr[...], wg_ref.at[0], wu_ref.at[0], wd_ref.at[0])

        @pl.when(j == 0)
        def _():
            o_ref[...] = y

        @pl.when(j > 0)
        def _():
            o_ref[...] += y


def _grouped_ffn(xs, g, tile_expert, n_used, wg, wu, wd, tm, tf):
    r, d = xs.shape
    f = wg.shape[2]
    nj = f // tf

    def row_map(i, j, te, nu):
        return (i, 0)

    def col_of(i, j, nu):
        return jnp.where(i < nu[0], j, nj - 1)

    grid_spec = pltpu.PrefetchScalarGridSpec(
        num_scalar_prefetch=2,
        grid=(r // tm, nj),
        in_specs=[pl.BlockSpec((tm, d), row_map),
                  pl.BlockSpec((1, d), lambda i, j, te, nu: (0, 0)),
                  pl.BlockSpec((1, d, tf), lambda i, j, te, nu: (te[i], 0, col_of(i, j, nu))),
                  pl.BlockSpec((1, d, tf), lambda i, j, te, nu: (te[i], 0, col_of(i, j, nu))),
                  pl.BlockSpec((1, tf, d), lambda i, j, te, nu: (te[i], col_of(i, j, nu), 0))],
        out_specs=pl.BlockSpec((tm, d), row_map),
        scratch_shapes=[pltpu.VMEM((tm, d), BF16)],
    )
    return pl.pallas_call(
        _grouped_ffn_kernel,
        grid_spec=grid_spec,
        out_shape=jax.ShapeDtypeStruct((r, d), F32),
        compiler_params=_cparams(("arbitrary", "arbitrary"), FFN_VMEM_LIMIT),
        name="moe_grouped_ffn",
    )(tile_expert, n_used, xs, g, wg, wu, wd)


def _combine_kernel(pos_ref, pos_next_ref, x_ref, route_ref, ys_hbm, o_ref, buf0, buf1, sems, *, tc):
    step = pl.program_id(0)
    slot = step % 2

    def gather(p_ref, s):
        def issue(j, c):
            _row_copy(ys_hbm, p_ref[0, 0, 2 * j], buf0.at[s], j, sems.at[s]).start()
            _row_copy(ys_hbm, p_ref[0, 0, 2 * j + 1], buf1.at[s], j, sems.at[s]).start()
            return c

        lax.fori_loop(0, tc, issue, 0, unroll=DMA_ISSUE_UNROLL)

    pl.when(step == 0)(functools.partial(gather, pos_ref, 0))
    pl.when(step + 1 < pl.num_programs(0))(functools.partial(gather, pos_next_ref, 1 - slot))
    for buf in (buf0, buf1):
        pltpu.make_async_copy(ys_hbm.at[pl.ds(0, tc)], buf.at[slot], sems.at[slot]).wait()
    route = route_ref[...]
    g0 = route[:, ROUTE_GATE_LANES[0]:ROUTE_GATE_LANES[0] + 1]
    g1 = route[:, ROUTE_GATE_LANES[1]:ROUTE_GATE_LANES[1] + 1]
    o_ref[...] = x_ref[...] + g0 * buf0[slot] + g1 * buf1[slot]


def _combine(x, route, pos, ys, tc):
    t, d = x.shape
    tc = min(tc, t)
    n = t // tc
    pos3 = pos.reshape(n, 1, 2 * tc)
    pos_block = (1, 1, 2 * tc)
    return pl.pallas_call(
        functools.partial(_combine_kernel, tc=tc),
        grid=(n,),
        in_specs=[pl.BlockSpec(pos_block, lambda i: (i, 0, 0), memory_space=pltpu.SMEM),
                  pl.BlockSpec(pos_block, lambda i: (jnp.minimum(i + 1, n - 1), 0, 0), memory_space=pltpu.SMEM),
                  pl.BlockSpec((tc, d), lambda i: (i, 0)),
                  pl.BlockSpec((tc, 128), lambda i: (i, 0)),
                  pl.BlockSpec(memory_space=pl.ANY)],
        out_specs=pl.BlockSpec((tc, d), lambda i: (i, 0)),
        out_shape=jax.ShapeDtypeStruct((t, d), F32),
        scratch_shapes=[pltpu.VMEM((2, tc, d), F32), pltpu.VMEM((2, tc, d), F32),
                        pltpu.SemaphoreType.DMA((2,))],
        compiler_params=_cparams(("arbitrary",)),
        name="moe_combine",
    )(pos3, pos3, x, route, ys)


MOE_TM = 512


def _route_plan(expert_ids, tm):
    flat = expert_ids.reshape(-1)
    a = flat.shape[0]
    onehot = (flat[None, :] == jnp.arange(N_EXPERTS, dtype=jnp.int32)[:, None]).astype(jnp.int32)
    csum = jnp.cumsum(onehot, axis=1)
    counts = csum[:, -1]
    padded = (counts + tm - 1) // tm * tm
    ends = jnp.cumsum(padded)
    offs = ends - padded
    pos = jnp.sum(onehot * (offs[:, None] + csum - 1), axis=0)
    n_tiles = (a + N_EXPERTS * tm) // tm
    tile_start = jnp.arange(n_tiles, dtype=jnp.int32) * tm
    tile_expert = jnp.minimum(jnp.sum(tile_start[:, None] >= ends[None, :], axis=1), N_EXPERTS - 1)
    n_used = (ends[-1] // tm).reshape(1)
    last = jnp.take(tile_expert, n_used[0] - 1)
    tile_expert = jnp.where(tile_start < ends[-1], tile_expert, last)
    meta = jnp.concatenate([offs + counts, ends, n_used]).astype(jnp.int32)
    return pos.astype(jnp.int32), tile_expert.astype(jnp.int32), n_used.astype(jnp.int32), meta, n_tiles * tm


def _moe(xs_list, routes, g, wg, wu, wd):
    ids = jnp.concatenate([r[:, ROUTE_ID_LANES[0]:ROUTE_ID_LANES[1] + 1] for r in routes]).astype(jnp.int32)
    pos, tile_expert, n_used, meta, n_rows = _route_plan(ids, MOE_TM)
    bounds = np.cumsum([0] + [2 * x.shape[0] for x in xs_list])
    pos_list = [pos[lo:hi] for lo, hi in zip(bounds[:-1], bounds[1:])]
    xs = _dispatch(xs_list, pos, meta, n_rows, MOE_TM, 1024)
    ys = _grouped_ffn(xs, g, tile_expert, n_used, wg, wu, wd, MOE_TM, D_FF)
    return [_combine(x, r, ps, ys, 512) for x, r, ps in zip(xs_list, routes, pos_list)]


def _pool_kernel(u_ref, halo_ref, w_ref, scale_ref, o_ref, *, tiles_per_seq, pos0, tb):
    ti = pl.program_id(0) % tiles_per_seq
    u = u_ref[...]
    halo = jnp.where(ti == 0, 0.0, halo_ref[...])
    ext = jnp.concatenate([halo, u], axis=0)
    s2 = ext + pltpu.roll(ext, 1, 0)
    s4 = s2 + pltpu.roll(s2, 2, 0)
    s8 = s4 + pltpu.roll(s4, 4, 0)
    s16 = s8 + pltpu.roll(s8, 8, 0)
    grp = lax.broadcasted_iota(jnp.int32, (tb, POOL_WIDTH), 1) // (POOL_WIDTH // 4)
    row = lax.broadcasted_iota(jnp.int32, (tb, POOL_WIDTH), 0)
    s = jnp.where(grp == 0, s2[POOL_HALO:],
                  jnp.where(grp == 1, s4[POOL_HALO:],
                            jnp.where(grp == 2, s8[POOL_HALO:], s16[POOL_HALO:])))
    win = jnp.where(grp == 0, 2, jnp.where(grp == 1, 4, jnp.where(grp == 2, 8, 16)))
    cnt = jnp.minimum(win, pos0 + ti * tb + row + 1).astype(F32)
    pooled = s / cnt - u
    o_ref[...] = (_dot(pooled.astype(BF16), w_ref[...]) * scale_ref[...]).astype(o_ref.dtype)


def _pool(proj2d, col, w, scale, *, n_rows, tb, tiles_per_seq, pos0):
    per = tb // POOL_HALO
    return pl.pallas_call(
        functools.partial(_pool_kernel, tiles_per_seq=tiles_per_seq, pos0=pos0, tb=tb),
        grid=(n_rows // tb,),
        in_specs=[pl.BlockSpec((tb, POOL_WIDTH), lambda i: (i, col)),
                  pl.BlockSpec((POOL_HALO, POOL_WIDTH), lambda i: (jnp.maximum(i * per - 1, 0), col)),
                  pl.BlockSpec((POOL_WIDTH, POOL_WIDTH), lambda i: (0, 0)),
                  pl.BlockSpec((1, POOL_WIDTH), lambda i: (0, 0))],
        out_specs=pl.BlockSpec((tb, POOL_WIDTH), lambda i: (i, 0)),
        out_shape=jax.ShapeDtypeStruct((n_rows, POOL_WIDTH), BF16),
        compiler_params=_cparams(("parallel",)),
        name="pool",
    )(proj2d, proj2d, w, scale)


def _t5_bucket_np(rel):
    n = np.maximum(rel, 0)
    max_exact = T5_BUCKETS // 2
    nf = np.maximum(n, max_exact).astype(np.float32)
    large = max_exact + (np.log(nf / max_exact) / math.log(T5_MAX_DIST / max_exact)
                         * (T5_BUCKETS - max_exact)).astype(np.int32)
    large = np.minimum(large, T5_BUCKETS - 1)
    return np.where(n < max_exact, n, large)


def _swa_bias(rel_bias, rel):
    valid = (rel >= 0) & (rel < SWA_WINDOW)
    onehot = jnp.asarray(_t5_bucket_np(rel)[..., None] == np.arange(T5_BUCKETS), F32)
    b = jnp.einsum('qsb,bh->hqs', onehot, rel_bias.astype(F32), precision=lax.Precision.HIGHEST)
    return jnp.where(valid[None], b, NEG)


def _softmax_parts(parts, sink):
    m = sink
    for s in parts:
        m = jnp.maximum(m, jnp.max(s, axis=-1, keepdims=True))
    ps = [jnp.exp(s - m) for s in parts]
    denom = jnp.exp(sink - m)
    for p in ps:
        denom = denom + jnp.sum(p, axis=-1, keepdims=True)
    return ps, denom


def _head_mean_matrix(width):
    h = np.arange(width) // HEAD_DIM
    return jnp.asarray((h[:, None] == h[None, :]) / HEAD_DIM, BF16)


def _head_rms(x, mean_mat, g):
    ms = _dot((x * x).astype(BF16), mean_mat)
    return x * lax.rsqrt(ms + RMS_EPS) * g


SWA_TILE = 512


def _swa_prompt_kernel(sinks_ref, q_ref, kv_ref, halo_ref, qg_ref, kg_ref, mq_ref, mk_ref, bias_ref,
                       y_ref, kn_ref):
    has_prev = pl.program_id(1) > 0
    kw = SWA_KV_HEADS * HEAD_DIM
    blk = SWA_BLOCK
    kv = kv_ref[0]
    halo = halo_ref[0]
    k_ext = jnp.concatenate([halo[:, :kw], kv[:, :kw]], axis=0)
    v_ext = jnp.concatenate([halo[:, kw:], kv[:, kw:]], axis=0).astype(BF16)
    kn = _head_rms(k_ext, mk_ref[...], kg_ref[...])
    kn_ref[0] = kn[SWA_TILE:]
    knb = kn.astype(BF16)
    qn = (_head_rms(q_ref[0], mq_ref[...], qg_ref[...]) * (HEAD_DIM ** -0.5)).astype(BF16)
    row = lax.broadcasted_iota(jnp.int32, (2 * blk, 1), 0)
    col = lax.broadcasted_iota(jnp.int32, (2 * blk, 2 * blk), 1)
    for c in range(SWA_TILE // blk):
        rows = slice(c * blk, (c + 1) * blk)
        keys = slice(c * blk, (c + 2) * blk)
        for kh in range(SWA_KV_HEADS):
            ksl = slice(kh * HEAD_DIM, (kh + 1) * HEAD_DIM)
            h0 = 2 * kh
            q2 = jnp.concatenate([qn[rows, h0 * HEAD_DIM:(h0 + 1) * HEAD_DIM],
                                  qn[rows, (h0 + 1) * HEAD_DIM:(h0 + 2) * HEAD_DIM]], axis=0)
            s = _dot_nt(q2, knb[keys, ksl]) + bias_ref[kh]
            if c == 0:
                s = jnp.where(has_prev | (col >= blk), s, NEG)
            sink = jnp.where(row < blk, sinks_ref[h0], sinks_ref[h0 + 1])
            (p,), denom = _softmax_parts((s,), sink)
            o = _dot(p.astype(BF16), v_ext[keys, ksl]) / denom
            y_ref[0, rows, h0 * HEAD_DIM:(h0 + 1) * HEAD_DIM] = o[:blk].astype(y_ref.dtype)
            y_ref[0, rows, (h0 + 1) * HEAD_DIM:(h0 + 2) * HEAD_DIM] = o[blk:].astype(y_ref.dtype)


def _swa_prompt(proj3d, sinks, qg, kg, bias):
    b, l, _ = proj3d.shape
    kw = SWA_KV_HEADS * HEAD_DIM
    per = SWA_TILE // SWA_BLOCK
    tile = (1, SWA_TILE, MIX_BLOCK)
    const2 = lambda bi, i: (0, 0)
    bias2 = bias.reshape(SWA_KV_HEADS, 2 * SWA_BLOCK, 2 * SWA_BLOCK)
    return pl.pallas_call(
        _swa_prompt_kernel,
        grid=(b, l // SWA_TILE),
        in_specs=[pl.BlockSpec(memory_space=pltpu.SMEM),
                  pl.BlockSpec(tile, lambda bi, i: (bi, i, COL_SWA_Q)),
                  pl.BlockSpec(tile, lambda bi, i: (bi, i, COL_SWA_KV)),
                  pl.BlockSpec((1, SWA_BLOCK, MIX_BLOCK),
                               lambda bi, i: (bi, jnp.maximum(i * per - 1, 0), COL_SWA_KV)),
                  pl.BlockSpec((1, MIX_BLOCK), const2),
                  pl.BlockSpec((1, kw), const2),
                  pl.BlockSpec((MIX_BLOCK, MIX_BLOCK), const2),
                  pl.BlockSpec((kw, kw), const2),
                  pl.BlockSpec((SWA_KV_HEADS, 2 * SWA_BLOCK, 2 * SWA_BLOCK), lambda bi, i: (0, 0, 0))],
        out_specs=[pl.BlockSpec(tile, lambda bi, i: (bi, i, 0)),
                   pl.BlockSpec((1, SWA_BLOCK, kw), lambda bi, i: (bi, 0, 0))],
        out_shape=[jax.ShapeDtypeStruct((b, l, MIX_BLOCK), BF16),
                   jax.ShapeDtypeStruct((b, SWA_BLOCK, kw), F32)],
        compiler_params=_cparams(("parallel", "arbitrary")),
        name="swa_prompt",
    )(sinks, proj3d, proj3d, proj3d, jnp.tile(qg, (1, SWA_HEADS)), jnp.tile(kg, (1, SWA_KV_HEADS)),
      _head_mean_matrix(MIX_BLOCK), _head_mean_matrix(kw), bias2)


def _swa_sample_kernel(sinks_ref, q_ref, kv_ref, ck_ref, cv_ref, qg_ref, kg_ref, bias_ref,
                       y_ref, nk_ref, nv_ref, *, n_new):
    q = q_ref[...]
    kv = kv_ref[...]
    ck = ck_ref[...]
    cv = cv_ref[...]
    qg = qg_ref[...]
    kg = kg_ref[...]
    kw = SWA_KV_HEADS * HEAD_DIM
    bdot = functools.partial(jnp.einsum, preferred_element_type=F32)
    for kh in range(SWA_KV_HEADS):
        ksl = slice(kh * HEAD_DIM, (kh + 1) * HEAD_DIM)
        vsl = slice(kw + kh * HEAD_DIM, kw + (kh + 1) * HEAD_DIM)
        kn = _rms(kv[:, :, ksl], kg)
        vn = kv[:, :, vsl]
        nk_ref[:, :SWA_WINDOW - n_new, ksl] = ck[:, n_new:, ksl]
        nk_ref[:, SWA_WINDOW - n_new:, ksl] = kn
        nv_ref[:, :SWA_WINDOW - n_new, ksl] = cv[:, n_new:, ksl]
        nv_ref[:, SWA_WINDOW - n_new:, ksl] = vn
        for gq in range(SWA_HEADS // SWA_KV_HEADS):
            h = kh * (SWA_HEADS // SWA_KV_HEADS) + gq
            hsl = slice(h * HEAD_DIM, (h + 1) * HEAD_DIM)
            qn = _rms(q[:, :, hsl], qg)
            s_c = bdot('sqd,skd->sqk', qn, ck[:, :, ksl]) * (HEAD_DIM ** -0.5) + bias_ref[h, :, :SWA_WINDOW]
            s_n = bdot('sqd,skd->sqk', qn, kn) * (HEAD_DIM ** -0.5) + bias_ref[h, :, SWA_WINDOW:]
            (p_c, p_n), denom = _softmax_parts((s_c, s_n), sinks_ref[h])
            o = bdot('sqk,skd->sqd', p_c, cv[:, :, ksl]) + bdot('sqk,skd->sqd', p_n, vn)
            y_ref[:, :, hsl] = (o / denom).astype(y_ref.dtype)


def _swa_sample(proj3d, cache_k, cache_v, sinks, qg, kg, bias, s_blk):
    nseq, n_new, _ = proj3d.shape
    kw = SWA_KV_HEADS * HEAD_DIM
    blk = (s_blk, n_new, MIX_BLOCK)
    cblk = (s_blk, SWA_WINDOW, kw)
    return pl.pallas_call(
        functools.partial(_swa_sample_kernel, n_new=n_new),
        grid=(nseq // s_blk,),
        in_specs=[pl.BlockSpec(memory_space=pltpu.SMEM),
                  pl.BlockSpec(blk, lambda i: (i, 0, COL_SWA_Q)),
                  pl.BlockSpec(blk, lambda i: (i, 0, COL_SWA_KV)),
                  pl.BlockSpec(cblk, lambda i: (i, 0, 0)),
                  pl.BlockSpec(cblk, lambda i: (i, 0, 0)),
                  pl.BlockSpec((1, HEAD_DIM), lambda i: (0, 0)),
                  pl.BlockSpec((1, HEAD_DIM), lambda i: (0, 0)),
                  pl.BlockSpec((SWA_HEADS, n_new, SWA_WINDOW + n_new), lambda i: (0, 0, 0))],
        out_specs=[pl.BlockSpec(blk, lambda i: (i, 0, 0)),
                   pl.BlockSpec(cblk, lambda i: (i, 0, 0)),
                   pl.BlockSpec(cblk, lambda i: (i, 0, 0))],
        out_shape=[jax.ShapeDtypeStruct((nseq, n_new, MIX_BLOCK), BF16),
                   jax.ShapeDtypeStruct((nseq, SWA_WINDOW, kw), F32),
                   jax.ShapeDtypeStruct((nseq, SWA_WINDOW, kw), F32)],
        compiler_params=_cparams(("parallel",)),
        name="swa_sample",
    )(sinks, proj3d, proj3d, cache_k, cache_v, qg, kg, bias)


def _ssm_kernel(u_ref, h0_ref, wb_ref, tab_ref, wc_ref, d_ref, wglu_ref, y_ref, hn_ref,
                bu_scr, carry_scr, *, chained, tiles_per_seq, tb):
    n = SSM_N
    u = u_ref[...]
    bu_scr[...] = _dot(u.astype(BF16), wb_ref[...])

    if chained:
        @pl.when(pl.program_id(1) % tiles_per_seq == 0)
        def _():
            carry_scr[...] = jnp.zeros_like(carry_scr)

    def tile_scan(r0, cr, ci):
        hr = bu_scr[pl.ds(r0, SCAN_ROWS), :n]
        hi = bu_scr[pl.ds(r0, SCAN_ROWS), n:]
        for k, shift in enumerate((1, 2, 4)):
            ar, ai = tab_ref[2 * k], tab_ref[2 * k + 1]
            sr, si = pltpu.roll(hr, shift, 0), pltpu.roll(hi, shift, 0)
            hr, hi = hr + ar * sr - ai * si, hi + ar * si + ai * sr
        pr, pi = tab_ref[6], tab_ref[7]
        hr, hi = hr + pr * cr - pi * ci, hi + pr * ci + pi * cr
        bu_scr[pl.ds(r0, SCAN_ROWS), :n] = hr
        bu_scr[pl.ds(r0, SCAN_ROWS), n:] = hi
        return hr[SCAN_ROWS - 1:], hi[SCAN_ROWS - 1:]

    if chained:
        def body(t, carry):
            r0 = pl.multiple_of(t * SCAN_ROWS, SCAN_ROWS)
            lr, li = tile_scan(r0, *carry)
            return (jnp.broadcast_to(lr, (SCAN_ROWS, n)), jnp.broadcast_to(li, (SCAN_ROWS, n)))

        cr, ci = lax.fori_loop(0, tb // SCAN_ROWS, body, (carry_scr[:, :n], carry_scr[:, n:]), unroll=2)
        carry_scr[:, :n] = cr
        carry_scr[:, n:] = ci
        hn_ref[0, :, :n] = cr
        hn_ref[0, :, n:] = ci
    else:
        def body(t, _):
            r0 = pl.multiple_of(t * SCAN_ROWS, SCAN_ROWS)
            h0 = h0_ref[pl.ds(t, 1), :]
            cr = jnp.broadcast_to(h0[:, :n], (SCAN_ROWS, n))
            ci = jnp.broadcast_to(h0[:, n:], (SCAN_ROWS, n))
            lr, li = tile_scan(r0, cr, ci)
            hn_ref[pl.ds(t, 1), :n] = lr
            hn_ref[pl.ds(t, 1), n:] = li
            return 0

        lax.fori_loop(0, tb // SCAN_ROWS, body, 0)

    y = _dot(bu_scr[...].astype(BF16), wc_ref[...]) + d_ref[...] * u
    y = 0.5 * y * (1.0 + jnp.tanh(math.sqrt(2.0 / math.pi) * (y + 0.044715 * (y * y * y))))
    y = y * _sigmoid(_dot(y.astype(BF16), wglu_ref[...]))
    if chained:
        y_ref[0] = y.astype(y_ref.dtype)
    else:
        y_ref[...] = y.astype(y_ref.dtype)


def _ssm_common_specs(zero_map2, zero_map3):
    return [pl.BlockSpec((SSM_WIDTH, 2 * SSM_N), zero_map2),
            pl.BlockSpec((8, SCAN_ROWS, SSM_N), zero_map3),
            pl.BlockSpec((2 * SSM_N, SSM_WIDTH), zero_map2),
            pl.BlockSpec((1, SSM_WIDTH), zero_map2),
            pl.BlockSpec((SSM_WIDTH, SSM_WIDTH), zero_map2)]


def _ssm_prompt(proj3d, sp, tb):
    b, l, _ = proj3d.shape
    nt = l // tb
    dummy_h0 = jnp.zeros((SCAN_ROWS, 2 * SSM_N), F32)
    kern = functools.partial(_ssm_kernel, chained=True, tiles_per_seq=nt, tb=tb)

    def kernel(u_ref, h0_ref, wb, tab, wc, d, wglu, y_ref, hn_ref, bu_scr, carry_scr):
        kern(u_ref.at[0], h0_ref, wb, tab, wc, d, wglu, y_ref, hn_ref, bu_scr, carry_scr)

    return pl.pallas_call(
        kernel,
        grid=(b, nt),
        in_specs=[pl.BlockSpec((1, tb, SSM_WIDTH), lambda bi, i: (bi, i, COL_SSM)),
                  pl.BlockSpec((SCAN_ROWS, 2 * SSM_N), lambda bi, i: (0, 0))]
                 + _ssm_common_specs(lambda bi, i: (0, 0), lambda bi, i: (0, 0, 0)),
        out_specs=[pl.BlockSpec((1, tb, SSM_WIDTH), lambda bi, i: (bi, i, 0)),
                   pl.BlockSpec((1, SCAN_ROWS, 2 * SSM_N), lambda bi, i: (bi, 0, 0))],
        out_shape=[jax.ShapeDtypeStruct((b, l, SSM_WIDTH), BF16),
                   jax.ShapeDtypeStruct((b, SCAN_ROWS, 2 * SSM_N), F32)],
        scratch_shapes=[pltpu.VMEM((tb, 2 * SSM_N), F32), pltpu.VMEM((SCAN_ROWS, 2 * SSM_N), F32)],
        compiler_params=_cparams(("parallel", "arbitrary")),
        name="ssm_prompt",
    )(proj3d, dummy_h0, sp["wb"], sp["tab"], sp["wc"], sp["d"], sp["wglu"])


def _ssm_sample(proj2d, h0, sp):
    rows = proj2d.shape[0]
    nseq = h0.shape[0]
    kern = functools.partial(_ssm_kernel, chained=False, tiles_per_seq=1, tb=rows)
    return pl.pallas_call(
        kern,
        grid=(1,),
        in_specs=[pl.BlockSpec((rows, SSM_WIDTH), lambda i: (0, COL_SSM)),
                  pl.BlockSpec((nseq, 2 * SSM_N), lambda i: (0, 0))]
                 + _ssm_common_specs(lambda i: (0, 0), lambda i: (0, 0, 0)),
        out_specs=[pl.BlockSpec((rows, SSM_WIDTH), lambda i: (0, 0)),
                   pl.BlockSpec((nseq, 2 * SSM_N), lambda i: (0, 0))],
        out_shape=[jax.ShapeDtypeStruct((rows, SSM_WIDTH), BF16),
                   jax.ShapeDtypeStruct((nseq, 2 * SSM_N), F32)],
        scratch_shapes=[pltpu.VMEM((rows, 2 * SSM_N), F32), pltpu.VMEM((SCAN_ROWS, 2 * SSM_N), F32)],
        compiler_params=_cparams(("arbitrary",)),
        name="ssm_sample",
    )(proj2d, h0, sp["wb"], sp["tab"], sp["wc"], sp["d"], sp["wglu"])


def _ssm_params(lam_re, lam_im, log_dt, b_re, b_im, c_re, c_im, d_skip, w_glu):
    lr, li = lam_re.astype(F32), lam_im.astype(F32)
    dt = jnp.exp(log_dt.astype(F32))[:, None]
    mag = jnp.exp(lr * dt)
    ab_re, ab_im = mag * jnp.cos(li * dt), mag * jnp.sin(li * dt)
    den = lr * lr + li * li
    nr = ab_re - 1.0
    f_re = (nr * lr + ab_im * li) / den
    f_im = (ab_im * lr - nr * li) / den
    br, bi = b_re.astype(F32), b_im.astype(F32)
    bb_re = f_re[..., None] * br - f_im[..., None] * bi
    bb_im = f_re[..., None] * bi + f_im[..., None] * br
    eye = jnp.eye(SSM_GROUPS, dtype=F32)

    def in_mat(bb):
        return jnp.einsum('gpc,gh->gchp', bb, eye).reshape(SSM_WIDTH, SSM_N)

    def out_mat(c):
        return jnp.einsum('gcp,gh->gphc', c.astype(F32), eye).reshape(SSM_N, SSM_WIDTH)

    wb = jnp.concatenate([in_mat(bb_re), in_mat(bb_im)], axis=1).astype(BF16)
    wc = jnp.concatenate([out_mat(c_re), -out_mat(c_im)], axis=0).astype(BF16)

    ar, ai = ab_re.reshape(1, SSM_N), ab_im.reshape(1, SSM_N)

    def cmul(x, y):
        return (x[0] * y[0] - x[1] * y[1], x[0] * y[1] + x[1] * y[0])

    pw = [(ar, ai)]
    for _ in range(SCAN_ROWS - 1):
        pw.append(cmul(pw[-1], (ar, ai)))
    row = jnp.arange(SCAN_ROWS)[:, None]
    tabs = []
    for shift in (1, 2, 4):
        for part in pw[shift - 1]:
            tabs.append(jnp.where(row >= shift, part, 0.0))
    tabs.append(jnp.concatenate([p[0] for p in pw], axis=0))
    tabs.append(jnp.concatenate([p[1] for p in pw], axis=0))
    tab = jnp.stack([jnp.broadcast_to(t, (SCAN_ROWS, SSM_N)) for t in tabs])
    return dict(wb=wb, tab=tab, wc=wc, d=d_skip.astype(F32).reshape(1, SSM_WIDTH), wglu=w_glu.astype(BF16))


_RET_G = 1.0 - np.exp2(-5.0 - np.arange(RET_HEADS, dtype=np.float64))


def _ret_consts(chunk, n_rows):
    idx = np.arange(n_rows)
    loc = idx % chunk
    same = (idx[:, None] // chunk) == (idx[None, :] // chunk)
    diff = loc[:, None] - loc[None, :]
    dec = np.where(same & (diff >= 0), _RET_G[:, None, None] ** np.maximum(diff, 0)[None], 0.0)
    qdec = np.repeat((_RET_G[None, :] ** (loc[:, None] + 1.0)), HEAD_DIM, axis=1)
    kdec = np.repeat((_RET_G[None, :] ** (chunk - 1.0 - loc[:, None])), HEAD_DIM, axis=1)
    return (jnp.asarray(dec, F32), jnp.asarray(qdec, F32), jnp.asarray(kdec, F32),
            jnp.asarray(_RET_G ** chunk, F32))


def _rope_tables(pos):
    half = HEAD_DIM // 2
    theta = 1.0 / (ROPE_BASE ** np.linspace(0.0, 1.0, half))
    ang = np.asarray(pos, np.float64)[:, None] * theta[None, :]
    cos = np.repeat(np.cos(ang), 2, axis=1)
    sin = np.repeat(np.sin(ang), 2, axis=1) * np.tile([-1.0, 1.0], half)[None]
    return (jnp.asarray(np.tile(cos, (1, RET_HEADS)), F32), jnp.asarray(np.tile(sin, (1, RET_HEADS)), F32))


def _rotate_pairs(x, cos, sin_signed):
    lane = lax.broadcasted_iota(jnp.int32, x.shape, 1)
    nxt = pltpu.roll(x, x.shape[1] - 1, 1)
    prv = pltpu.roll(x, 1, 1)
    return x * cos + jnp.where(lane % 2 == 0, nxt, prv) * sin_signed


def _ret_head_out(o, gate, norm):
    ms = jnp.mean(o * o, axis=-1, keepdims=True)
    return o * lax.rsqrt(ms + RMS_EPS) * norm * (gate * _sigmoid(gate))


RET_TILE = 512


def _ret_prompt_kernel(gc_ref, q_ref, k_ref, v_ref, g_ref, cos_ref, sin_ref, dec_ref, qdec_ref,
                       kdec_ref, norm_ref, mh_ref, y_ref, r_ref, o_scr):
    @pl.when(pl.program_id(1) == 0)
    def _():
        r_ref[...] = jnp.zeros_like(r_ref)

    cos, sin = cos_ref[...], sin_ref[...]
    q = _rotate_pairs(q_ref[0], cos, sin)
    k = _rotate_pairs(k_ref[0], cos, sin) * (HEAD_DIM ** -0.5)
    qb, kb, vb = q.astype(BF16), k.astype(BF16), v_ref[0].astype(BF16)
    kdb = (k * kdec_ref[...]).astype(BF16)
    for c in range(RET_TILE // RET_CHUNK):
        rows = slice(c * RET_CHUNK, (c + 1) * RET_CHUNK)
        for h in range(RET_HEADS):
            sl = slice(h * HEAD_DIM, (h + 1) * HEAD_DIM)
            qh, vh = qb[rows, sl], vb[rows, sl]
            s = _dot_nt(qh, kb[rows, sl]) * dec_ref[h]
            r = r_ref[0, h]
            o_scr[rows, sl] = _dot(s.astype(BF16), vh) + _dot(qh, r.astype(BF16)) * qdec_ref[rows, sl]
            r_ref[0, h] = gc_ref[h] * r + _dot_tn(kdb[rows, sl], vh)
    g = g_ref[0]
    y_ref[0] = (_head_rms(o_scr[...], mh_ref[...], norm_ref[...]) * (g * _sigmoid(g))).astype(y_ref.dtype)


def _ret_prompt(proj3d, cos, sin, norm):
    b, l, _ = proj3d.shape
    c = RET_CHUNK
    dec, _, _, gc = _ret_consts(c, c)
    _, qdec, kdec, _ = _ret_consts(c, RET_TILE)
    blk = (1, RET_TILE, MIX_BLOCK)
    tspec = pl.BlockSpec((RET_TILE, MIX_BLOCK), lambda bi, i: (i, 0))
    cspec = pl.BlockSpec((RET_TILE, MIX_BLOCK), lambda bi, i: (0, 0))
    return pl.pallas_call(
        _ret_prompt_kernel,
        grid=(b, l // RET_TILE),
        in_specs=[pl.BlockSpec(memory_space=pltpu.SMEM),
                  pl.BlockSpec(blk, lambda bi, i: (bi, i, COL_RET_Q)),
                  pl.BlockSpec(blk, lambda bi, i: (bi, i, COL_RET_K)),
                  pl.BlockSpec(blk, lambda bi, i: (bi, i, COL_RET_V)),
                  pl.BlockSpec(blk, lambda bi, i: (bi, i, COL_RET_G)),
                  tspec, tspec,
                  pl.BlockSpec((RET_HEADS, c, c), lambda bi, i: (0, 0, 0)),
                  cspec, cspec,
                  pl.BlockSpec((1, MIX_BLOCK), lambda bi, i: (0, 0)),
                  pl.BlockSpec((MIX_BLOCK, MIX_BLOCK), lambda bi, i: (0, 0))],
        out_specs=[pl.BlockSpec(blk, lambda bi, i: (bi, i, 0)),
                   pl.BlockSpec((1, RET_HEADS, HEAD_DIM, HEAD_DIM), lambda bi, i: (bi, 0, 0, 0))],
        out_shape=[jax.ShapeDtypeStruct((b, l, MIX_BLOCK), BF16),
                   jax.ShapeDtypeStruct((b, RET_HEADS, HEAD_DIM, HEAD_DIM), F32)],
        scratch_shapes=[pltpu.VMEM((RET_TILE, MIX_BLOCK), F32)],
        compiler_params=_cparams(("parallel", "arbitrary")),
        name="ret_prompt",
    )(gc, proj3d, proj3d, proj3d, proj3d, cos, sin, dec, qdec, kdec, norm, _head_mean_matrix(MIX_BLOCK))


def _ret_sample_kernel(gc_ref, q_ref, k_ref, v_ref, g_ref, cos_ref, sin_ref, dec_ref, qdec_ref,
                       kdec_ref, norm_ref, r0_ref, y_ref, rn_ref, *, n_new, s_blk):
    cos, sin = cos_ref[...], sin_ref[...]
    q = _rotate_pairs(q_ref[...], cos, sin)
    k = _rotate_pairs(k_ref[...], cos, sin) * (HEAD_DIM ** -0.5)
    v = v_ref[...]
    g = g_ref[...]
    kd = k * kdec_ref[...]
    qdec = qdec_ref[...]
    norm = norm_ref[...]
    rows = s_blk * n_new
    seq = lax.broadcasted_iota(jnp.int32, (rows, HEAD_DIM), 0) // n_new
    for h in range(RET_HEADS):
        sl = slice(h * HEAD_DIM, (h + 1) * HEAD_DIM)
        qf, kdf = q[:, sl], kd[:, sl]
        qh, kh, vh = qf.astype(BF16), k[:, sl].astype(BF16), v[:, sl].astype(BF16)
        s = _dot_nt(qh, kh) * dec_ref[h]
        cross = jnp.zeros((rows, HEAD_DIM), F32)
        for si in range(s_blk):
            mine = seq == si
            r = r0_ref[si, h]
            cross = cross + _dot(jnp.where(mine, qf, 0.0).astype(BF16), r.astype(BF16))
            rn_ref[si, h] = gc_ref[h] * r + _dot_tn(jnp.where(mine, kdf, 0.0).astype(BF16), vh)
        o = _dot(s.astype(BF16), vh) + cross * qdec[:, sl]
        y_ref[:, sl] = _ret_head_out(o, g[:, sl], norm[:, sl]).astype(y_ref.dtype)


def _ret_sample(proj2d, r0, cos, sin, norm, n_new, s_blk):
    rows = s_blk * n_new
    nseq = r0.shape[0]
    dec, qdec, kdec, gc = _ret_consts(n_new, rows)
    blk = (rows, MIX_BLOCK)
    cspec = pl.BlockSpec(blk, lambda i: (0, 0))
    rblk = (s_blk, RET_HEADS, HEAD_DIM, HEAD_DIM)
    return pl.pallas_call(
        functools.partial(_ret_sample_kernel, n_new=n_new, s_blk=s_blk),
        grid=(nseq // s_blk,),
        in_specs=[pl.BlockSpec(memory_space=pltpu.SMEM),
                  pl.BlockSpec(blk, lambda i: (i, COL_RET_Q)),
                  pl.BlockSpec(blk, lambda i: (i, COL_RET_K)),
                  pl.BlockSpec(blk, lambda i: (i, COL_RET_V)),
                  pl.BlockSpec(blk, lambda i: (i, COL_RET_G)),
                  cspec, cspec,
                  pl.BlockSpec((RET_HEADS, rows, rows), lambda i: (0, 0, 0)),
                  cspec, cspec,
                  pl.BlockSpec((1, MIX_BLOCK), lambda i: (0, 0)),
                  pl.BlockSpec(rblk, lambda i: (i, 0, 0, 0))],
        out_specs=[pl.BlockSpec(blk, lambda i: (i, 0)),
                   pl.BlockSpec(rblk, lambda i: (i, 0, 0, 0))],
        out_shape=[jax.ShapeDtypeStruct((nseq * n_new, MIX_BLOCK), BF16),
                   jax.ShapeDtypeStruct((nseq, RET_HEADS, HEAD_DIM, HEAD_DIM), F32)],
        compiler_params=_cparams(("parallel",)),
        name="ret_sample",
    )(gc, proj2d, proj2d, proj2d, proj2d, cos, sin, dec, qdec, kdec, norm, r0)


def _block_diag(w):
    g, n, _ = w.shape
    return jnp.einsum('gcd,gh->gchd', w, jnp.eye(g, dtype=w.dtype)).reshape(g * n, g * n)


def _layer_params(l, p):
    return dict(
        norm_mix=p['norm_mix'][l].reshape(1, D_MODEL),
        norm_ffn=p['norm_ffn'][l].reshape(1, D_MODEL),
        w_in=p['w_in'][l].astype(BF16),
        w_out=p['w_out'][l].astype(BF16),
        pool_w=_block_diag(p['pool_w'][l].astype(F32)).astype(BF16),
        pool_scale=p['pool_scale'][l].astype(F32).reshape(1, POOL_WIDTH),
        qg=p['swa_q_norm'][l].astype(F32).reshape(1, HEAD_DIM),
        kg=p['swa_k_norm'][l].astype(F32).reshape(1, HEAD_DIM),
        sinks=p['swa_sinks'][l].astype(F32),
        ssm=_ssm_params(p['ssm_lambda_re'][l], p['ssm_lambda_im'][l], p['ssm_log_dt'][l],
                        p['ssm_b_re'][l], p['ssm_b_im'][l], p['ssm_c_re'][l], p['ssm_c_im'][l],
                        p['ssm_d'][l], p['ssm_w_glu'][l]),
        ret_norm=p['ret_norm'][l].astype(F32).reshape(1, MIX_BLOCK),
    )


def _channel_mix(streams, l, lp, p):
    i = l // 2
    g, w_out = lp['norm_ffn'], lp['w_out']
    if l % 2 == 0:
        wg, wu, wd = (p[k][i].astype(BF16) for k in ('ffn_w_gate', 'ffn_w_up', 'ffn_w_down'))
        return [_out_proj_ffn(x, ys, w_out, g, wg, wu, wd, 512, D_FF) for x, ys in streams]
    wr3 = _router_weights(p['moe_router'][i])
    wg, wu, wd = (p[k][i].astype(BF16) for k in ('moe_w_gate', 'moe_w_up', 'moe_w_down'))
    x1s, routes = zip(*[_out_proj_router(x, ys, w_out, g, wr3, 512) for x, ys in streams])
    return _moe(x1s, routes, g, wg, wu, wd)


def _mix_prompt(x2, b, l, lp, bias, cos, sin):
    proj2 = _norm_matmul(x2, lp['norm_mix'], lp['w_in'], 512)
    proj3 = proj2.reshape(b, l, IN_WIDTH)
    tb = 512
    y_pool = _pool(proj2, COL_POOL, lp['pool_w'], lp['pool_scale'], n_rows=b * l, tb=tb,
                   tiles_per_seq=l // tb, pos0=0)
    y_swa, kn = _swa_prompt(proj3, lp['sinks'], lp['qg'], lp['kg'], bias)
    y_ssm, hn = _ssm_prompt(proj3, lp['ssm'], 512)
    y_ret, rn = _ret_prompt(proj3, cos, sin, lp['ret_norm'])
    ys = (y_pool, y_swa.reshape(b * l, MIX_BLOCK), y_ssm.reshape(b * l, MIX_BLOCK),
          y_ret.reshape(b * l, MIX_BLOCK))
    kw = SWA_KV_HEADS * HEAD_DIM
    hn = hn[:, 0]
    states = (proj3[:, l - POOL_BUF:, :POOL_WIDTH],
              kn.reshape(b, SWA_WINDOW, SWA_KV_HEADS, HEAD_DIM),
              proj3[:, l - SWA_WINDOW:, COL_SWA_KV * MIX_BLOCK + kw:(COL_SWA_KV + 1) * MIX_BLOCK]
              .reshape(b, SWA_WINDOW, SWA_KV_HEADS, HEAD_DIM),
              jnp.stack([hn[:, :SSM_N], hn[:, SSM_N:]], axis=-1).reshape(b, SSM_GROUPS, SSM_STATE, 2),
              rn)
    return ys, states


SAMPLE_SEQ_BLOCK = 16


def _mix_sample(x2, nseq, n_new, start_pos, lp, st, bias, cos, sin):
    state_pool, cache_k, cache_v, state_ssm, state_ret = st
    rows = nseq * n_new
    wb = cache_k.shape[1]
    kw = SWA_KV_HEADS * HEAD_DIM
    ext_rows = POOL_HALO + n_new
    proj2 = _norm_matmul(x2, lp['norm_mix'], lp['w_in'], 512)
    proj3 = proj2.reshape(nseq, n_new, IN_WIDTH)
    u_pool = proj3[:, :, :POOL_WIDTH]
    buf = state_pool.astype(F32)
    ext = jnp.concatenate([jnp.zeros((nseq, POOL_HALO - POOL_BUF, POOL_WIDTH), F32), buf, u_pool], axis=1)
    y_pool = _pool(ext.reshape(nseq * ext_rows, POOL_WIDTH), 0, lp['pool_w'], lp['pool_scale'],
                   n_rows=nseq * ext_rows, tb=nseq * ext_rows, tiles_per_seq=1, pos0=start_pos)
    y_pool = y_pool.reshape(nseq, ext_rows, POOL_WIDTH)[:, POOL_HALO:].reshape(rows, POOL_WIDTH)
    y_swa, nk, nv = _swa_sample(proj3, cache_k.reshape(nseq, wb, kw).astype(F32),
                                cache_v.reshape(nseq, wb, kw).astype(F32),
                                lp['sinks'], lp['qg'], lp['kg'], bias, SAMPLE_SEQ_BLOCK)
    h0 = state_ssm.astype(F32).reshape(nseq, SSM_N, 2)
    h0 = jnp.concatenate([h0[..., 0], h0[..., 1]], axis=1)
    y_ssm, hn = _ssm_sample(proj2, h0, lp['ssm'])
    y_ret, rn = _ret_sample(proj2, state_ret.astype(F32), cos, sin, lp['ret_norm'], n_new, SAMPLE_SEQ_BLOCK)
    ys = (y_pool, y_swa.reshape(rows, MIX_BLOCK), y_ssm, y_ret)
    states = (jnp.concatenate([buf, u_pool], axis=1)[:, -POOL_BUF:],
              nk.reshape(nseq, SWA_WINDOW, SWA_KV_HEADS, HEAD_DIM),
              nv.reshape(nseq, SWA_WINDOW, SWA_KV_HEADS, HEAD_DIM),
              jnp.stack([hn[:, :SSM_N], hn[:, SSM_N:]], axis=-1).reshape(nseq, SSM_GROUPS, SSM_STATE, 2),
              rn)
    return ys, states


def _forward(x_prompt, x_sample, past_len, sample_state, p, rel_bias):
    b, l, d = x_prompt.shape
    nseq, n_new, _ = x_sample.shape
    wb = sample_state[1].shape[2]
    depth = p['norm_mix'].shape[0]
    bias_p = _swa_bias(rel_bias, np.arange(SWA_BLOCK)[:, None] - np.arange(2 * SWA_BLOCK)[None, :] + SWA_BLOCK)
    bias_s = _swa_bias(rel_bias, np.arange(n_new)[:, None] - np.arange(wb + n_new)[None, :] + wb)
    rope_p = _rope_tables(np.arange(l))
    rope_s = _rope_tables(past_len + (np.arange(SAMPLE_SEQ_BLOCK * n_new) % n_new))
    xp = x_prompt.reshape(b * l, d)
    xs = x_sample.reshape(nseq * n_new, d)
    st_p, st_s = [], []
    for li in range(depth):
        lp = _layer_params(li, p)
        yp, sp = _mix_prompt(xp, b, l, lp, bias_p, *rope_p)
        ys, ss = _mix_sample(xs, nseq, n_new, past_len, lp, [s[li] for s in sample_state], bias_s, *rope_s)
        xp, xs = _channel_mix([(xp, yp), (xs, ys)], li, lp, p)
        st_p.append(sp)
        st_s.append(ss)
    outs = [xp.reshape(b, l, d), xs.reshape(nseq, n_new, d)]
    for k in range(5):
        outs.append(jnp.stack([s[k] for s in st_p]))
        outs.append(jnp.stack([s[k] for s in st_s]))
    return tuple(outs)


PAST_LEN = 16384


def kernel(x_prompt, x_sample, state_pool, cache_swa_k, cache_swa_v, state_ssm, state_ret,
           norm_mix, norm_ffn, w_in, w_out, pool_w, pool_scale, swa_q_norm, swa_k_norm, swa_sinks,
           rel_bias, ssm_lambda_re, ssm_lambda_im, ssm_log_dt, ssm_b_re, ssm_b_im, ssm_c_re, ssm_c_im,
           ssm_d, ssm_w_glu, ret_norm, ffn_w_gate, ffn_w_up, ffn_w_down, moe_router, moe_w_gate,
           moe_w_up, moe_w_down):
    p = dict(norm_mix=norm_mix, norm_ffn=norm_ffn, w_in=w_in, w_out=w_out, pool_w=pool_w,
             pool_scale=pool_scale, swa_q_norm=swa_q_norm, swa_k_norm=swa_k_norm, swa_sinks=swa_sinks,
             ssm_lambda_re=ssm_lambda_re, ssm_lambda_im=ssm_lambda_im, ssm_log_dt=ssm_log_dt,
             ssm_b_re=ssm_b_re, ssm_b_im=ssm_b_im, ssm_c_re=ssm_c_re, ssm_c_im=ssm_c_im,
             ssm_d=ssm_d, ssm_w_glu=ssm_w_glu, ret_norm=ret_norm,
             ffn_w_gate=ffn_w_gate, ffn_w_up=ffn_w_up, ffn_w_down=ffn_w_down, moe_router=moe_router,
             moe_w_gate=moe_w_gate, moe_w_up=moe_w_up, moe_w_down=moe_w_down)
    return _forward(x_prompt, x_sample, PAST_LEN,
                    (state_pool, cache_swa_k, cache_swa_v, state_ssm, state_ret), p, rel_bias)
```

```python
import functools
import math

import numpy as np
import jax
import jax.numpy as jnp
from jax import lax
from jax.experimental import pallas as pl
from jax.experimental.pallas import tpu as pltpu

F32 = jnp.float32
BF16 = jnp.bfloat16

D_MODEL = 1024
HEAD_DIM = 64
POOL_WIDTH = 256
POOL_WINDOWS = (2, 4, 8, 16)
POOL_BUF = 15
POOL_HALO = 16
SWA_HEADS = 4
SWA_KV_HEADS = 2
SWA_WINDOW = 128
SWA_BLOCK = 128
SSM_WIDTH = 256
SSM_CH = 16
SSM_GROUPS = 16
SSM_STATE = 64
SSM_N = SSM_GROUPS * SSM_STATE
RET_HEADS = 4
RET_CHUNK = 128
ROPE_BASE = 10000.0
IN_WIDTH = 2048
MIX_BLOCK = 256
D_FF = 2816
N_EXPERTS = 8
T5_BUCKETS = 32
T5_MAX_DIST = 128
RMS_EPS = 1e-6
NEG = -1e30
SUBLANES = 8
SCAN_ROWS = SUBLANES

COL_POOL, COL_SWA_Q, COL_SWA_KV, COL_SSM, COL_RET_Q, COL_RET_K, COL_RET_V, COL_RET_G = range(8)

VMEM_LIMIT = 48 * 1024 * 1024
FFN_VMEM_LIMIT = 58 * 1024 * 1024


def _cparams(sem, vmem=VMEM_LIMIT):
    return pltpu.CompilerParams(dimension_semantics=sem, vmem_limit_bytes=vmem)


def _rms(x, g):
    ms = jnp.mean(x * x, axis=-1, keepdims=True)
    return x * lax.rsqrt(ms + RMS_EPS) * g


def _dot(a, b):
    return jnp.dot(a, b, preferred_element_type=F32)


def _dot_nt(a, b):
    return lax.dot_general(a, b, (((1,), (1,)), ((), ())), preferred_element_type=F32)


def _dot_tn(a, b):
    return lax.dot_general(a, b, (((0,), (0,)), ((), ())), preferred_element_type=F32)


def _sigmoid(x):
    return 1.0 / (1.0 + jnp.exp(-x))


def _norm_matmul_kernel(x_ref, g_ref, w_ref, o_ref):
    h = _rms(x_ref[...], g_ref[...]).astype(BF16)
    o_ref[...] = _dot(h, w_ref[...])


def _norm_matmul(x, g, w, tm):
    t, d = x.shape
    tm = min(tm, t)
    n = w.shape[1]
    return pl.pallas_call(
        _norm_matmul_kernel,
        grid=(t // tm,),
        in_specs=[pl.BlockSpec((tm, d), lambda i: (i, 0)),
                  pl.BlockSpec((1, d), lambda i: (0, 0)),
                  pl.BlockSpec((d, n), lambda i: (0, 0))],
        out_specs=pl.BlockSpec((tm, n), lambda i: (i, 0)),
        out_shape=jax.ShapeDtypeStruct((t, n), F32),
        compiler_params=_cparams(("parallel",)),
        name="norm_matmul",
    )(x, g, w)


FFN_SUBCHUNK = 512


def _swiglu_chunk(h, wg_ref, wu_ref, wd_ref):
    tf = wg_ref.shape[1]
    y = None
    for lo in range(0, tf, FFN_SUBCHUNK):
        hi = min(lo + FFN_SUBCHUNK, tf)
        a = _dot(h, wg_ref[:, lo:hi])
        b = _dot(h, wu_ref[:, lo:hi])
        part = _dot((a * _sigmoid(a) * b).astype(BF16), wd_ref[lo:hi, :])
        y = part if y is None else y + part
    return y


def _mixed_residual(x_ref, y_refs, w_ref, rows=slice(None)):
    y = jnp.concatenate([y_ref[rows, :] for y_ref in y_refs], axis=1)
    return x_ref[rows, :] + _dot(y, w_ref[...])


def _mix_in_specs(tm, d, imap):
    yspec = pl.BlockSpec((tm, MIX_BLOCK), imap(lambda i: (i, 0)))
    return [pl.BlockSpec((tm, d), imap(lambda i: (i, 0))), yspec, yspec, yspec, yspec,
            pl.BlockSpec((d, d), imap(lambda i: (0, 0)))]


def _out_proj_ffn_kernel(x_ref, y0_ref, y1_ref, y2_ref, y3_ref, wo_ref, g_ref, wg_ref, wu_ref, wd_ref,
                         o_ref, h_scr):
    @pl.when(pl.program_id(1) == 0)
    def _():
        x1 = _mixed_residual(x_ref, (y0_ref, y1_ref, y2_ref, y3_ref), wo_ref)
        h_scr[...] = _rms(x1, g_ref[...]).astype(BF16)
        o_ref[...] = x1

    o_ref[...] += _swiglu_chunk(h_scr[...], wg_ref, wu_ref, wd_ref)


def _out_proj_ffn(x, ys, w_out, g, wg, wu, wd, tm, tf):
    t, d = x.shape
    tm = min(tm, t)
    f = wg.shape[1]
    imap = lambda fn: (lambda i, j: fn(i))
    once = dict(pipeline_mode=pl.Buffered(1)) if tf == f else {}
    return pl.pallas_call(
        _out_proj_ffn_kernel,
        grid=(t // tm, f // tf),
        in_specs=_mix_in_specs(tm, d, imap)
                 + [pl.BlockSpec((1, d), lambda i, j: (0, 0)),
                    pl.BlockSpec((d, tf), lambda i, j: (0, j), **once),
                    pl.BlockSpec((d, tf), lambda i, j: (0, j), **once),
                    pl.BlockSpec((tf, d), lambda i, j: (j, 0), **once)],
        out_specs=pl.BlockSpec((tm, d), lambda i, j: (i, 0)),
        out_shape=jax.ShapeDtypeStruct((t, d), F32),
        scratch_shapes=[pltpu.VMEM((tm, d), BF16)],
        compiler_params=_cparams(("parallel", "arbitrary"), FFN_VMEM_LIMIT),
        name="out_proj_ffn",
    )(x, *ys, w_out, g, wg, wu, wd)


ROUTE_ID_LANES = (0, 1)
ROUTE_GATE_LANES = (2, 3)


def _split_bf16(x):
    hi = x.astype(BF16)
    return hi, (x - hi.astype(F32)).astype(BF16)


ROUTER_ROWS = 256


def _out_proj_router_kernel(x_ref, y0_ref, y1_ref, y2_ref, y3_ref, wo_ref, g_ref, wr_ref, x1_ref, c_ref):
    tm = x_ref.shape[0]
    step = min(ROUTER_ROWS, tm)
    for r in range(0, tm, step):
        _route_rows(slice(r, r + step), x_ref, (y0_ref, y1_ref, y2_ref, y3_ref), wo_ref, g_ref, wr_ref,
                    x1_ref, c_ref)


def _route_rows(rows, x_ref, y_refs, wo_ref, g_ref, wr_ref, x1_ref, c_ref):
    x1 = _mixed_residual(x_ref, y_refs, wo_ref, rows)
    x1_ref[rows, :] = x1
    h_hi, h_lo = _split_bf16(_rms(x1, g_ref[...]))
    logits = _dot(jnp.concatenate([h_hi, h_lo, h_hi], axis=1), wr_ref[...])
    lane = lax.broadcasted_iota(jnp.int32, logits.shape, 1).astype(F32)
    lg = jnp.where(lane < N_EXPERTS, logits, NEG)
    m1 = jnp.max(lg, axis=-1, keepdims=True)
    i1 = jnp.min(jnp.where(lg == m1, lane, 128.0), axis=-1, keepdims=True)
    lg2 = jnp.where(lane == i1, NEG, lg)
    m2 = jnp.max(lg2, axis=-1, keepdims=True)
    i2 = jnp.min(jnp.where(lg2 == m2, lane, 128.0), axis=-1, keepdims=True)
    ex = jnp.exp(m2 - m1)
    vals = (i1, i2, 1.0 / (1.0 + ex), ex / (1.0 + ex))
    out = jnp.zeros_like(logits)
    for ln, v in zip(ROUTE_ID_LANES + ROUTE_GATE_LANES, vals):
        out = jnp.where(lane == ln, v, out)
    c_ref[rows, :] = out


def _router_weights(wr):
    w = jnp.pad(wr.astype(F32), ((0, 0), (0, 128 - N_EXPERTS)))
    hi, lo = _split_bf16(w)
    return jnp.concatenate([hi, hi, lo], axis=0)


def _out_proj_router(x, ys, w_out, g, wr3, tm):
    t, d = x.shape
    tm = min(tm, t)
    imap = lambda fn: fn
    return pl.pallas_call(
        _out_proj_router_kernel,
        grid=(t // tm,),
        in_specs=_mix_in_specs(tm, d, imap)
                 + [pl.BlockSpec((1, d), lambda i: (0, 0)),
                    pl.BlockSpec((3 * d, 128), lambda i: (0, 0))],
        out_specs=[pl.BlockSpec((tm, d), lambda i: (i, 0)),
                   pl.BlockSpec((tm, 128), lambda i: (i, 0))],
        out_shape=[jax.ShapeDtypeStruct((t, d), F32), jax.ShapeDtypeStruct((t, 128), F32)],
        compiler_params=_cparams(("parallel",)),
        name="out_proj_router",
    )(x, *ys, w_out, g, wr3)


DMA_ISSUE_UNROLL = 8


def _row_copy(src, i, dst, j, sem):
    return pltpu.make_async_copy(src.at[pl.ds(i, 1)], dst.at[pl.ds(j, 1)], sem)


def _dispatch_kernel(meta_ref, pos_ref, *rest, td, tm, n_tiles, first_step):
    n_streams = len(first_step) - 1
    x_refs, (xs_hbm, zero_scr, sem) = rest[:n_streams], rest[n_streams:]
    step = pl.program_id(0)

    def zero_row(r):
        return _row_copy(zero_scr, 0, xs_hbm, r, sem)

    @pl.when(step == 0)
    def _():
        zero_scr[...] = jnp.zeros_like(zero_scr)
        n_used = meta_ref[2 * N_EXPERTS]
        tile_fills = [(i >= n_used, pltpu.make_async_copy(zero_scr, xs_hbm.at[pl.ds(i * tm, tm)], sem))
                      for i in range(n_tiles)]
        for cond, copy in tile_fills:
            pl.when(cond)(copy.start)
        for e in range(N_EXPERTS):
            lax.fori_loop(meta_ref[e], meta_ref[N_EXPERTS + e], lambda r, c: (zero_row(r).start(), c)[1], 0)
        for e in range(N_EXPERTS):
            lax.fori_loop(meta_ref[e], meta_ref[N_EXPERTS + e], lambda r, c: (zero_row(r).wait(), c)[1], 0)
        for cond, copy in tile_fills:
            pl.when(cond)(copy.wait)

    def scatter(x_ref):
        def issue(j, c):
            for k in range(2):
                _row_copy(x_ref, j, xs_hbm, pos_ref[0, 0, 2 * j + k], sem).start()
            return c

        lax.fori_loop(0, td, issue, 0, unroll=DMA_ISSUE_UNROLL)
        for _ in range(2):
            pltpu.make_async_copy(x_ref, xs_hbm.at[pl.ds(0, td)], sem).wait()

    for s, x_ref in enumerate(x_refs):
        pl.when((step >= first_step[s]) & (step < first_step[s + 1]))(functools.partial(scatter, x_ref))


def _dispatch(xs_list, pos, meta, n_rows, tm, td):
    d = xs_list[0].shape[1]
    td = min([td] + [x.shape[0] for x in xs_list])
    first_step = [0]
    for x in xs_list:
        first_step.append(first_step[-1] + x.shape[0] // td)
    n_steps = first_step[-1]
    pos3 = pos.reshape(n_steps, 1, 2 * td)

    def tile_map(s):
        lo, hi = first_step[s], first_step[s + 1]
        return lambda i, m: (jnp.clip(i, lo, hi - 1) - lo, 0)

    in_specs = [pl.BlockSpec((1, 1, 2 * td), lambda i, m: (i, 0, 0), memory_space=pltpu.SMEM)]
    in_specs += [pl.BlockSpec((td, d), tile_map(s)) for s in range(len(xs_list))]
    return pl.pallas_call(
        functools.partial(_dispatch_kernel, td=td, tm=tm, n_tiles=n_rows // tm, first_step=tuple(first_step)),
        grid_spec=pltpu.PrefetchScalarGridSpec(
            num_scalar_prefetch=1, grid=(n_steps,), in_specs=in_specs,
            out_specs=pl.BlockSpec(memory_space=pl.ANY),
            scratch_shapes=[pltpu.VMEM((tm, d), F32), pltpu.SemaphoreType.DMA]),
        out_shape=jax.ShapeDtypeStruct((n_rows, d), F32),
        compiler_params=_cparams(("arbitrary",)),
        name="moe_dispatch",
    )(meta, pos3, *xs_list)


def _grouped_ffn_kernel(te_ref, nu_ref, x_ref, g_ref, wg_ref, wu_ref, wd_ref, o_ref, h_scr):
    del te_ref
    j = pl.program_id(1)
    used = pl.program_id(0) < nu_ref[0]

    @pl.when(jnp.logical_not(used) & (j == 0))
    def _():
        o_ref[...] = jnp.zeros_like(o_ref)

    @pl.when(used)
    def _():
        @pl.when(j == 0)
        def _():
            h_scr[...] = _rms(x_ref[...], g_ref[...]).astype(BF16)

        y = _swiglu_chunk(h_scr[...], wg_ref.at[0], wu_ref.at[0], wd_ref.at[0])

        @pl.when(j == 0)
        def _():
            o_ref[...] = y

        @pl.when(j > 0)
        def _():
            o_ref[...] += y


def _grouped_ffn(xs, g, tile_expert, n_used, wg, wu, wd, tm, tf):
    r, d = xs.shape
    f = wg.shape[2]
    nj = f // tf

    def row_map(i, j, te, nu):
        return (i, 0)

    def col_of(i, j, nu):
        return jnp.where(i < nu[0], j, nj - 1)

    grid_spec = pltpu.PrefetchScalarGridSpec(
        num_scalar_prefetch=2,
        grid=(r // tm, nj),
        in_specs=[pl.BlockSpec((tm, d), row_map),
                  pl.BlockSpec((1, d), lambda i, j, te, nu: (0, 0)),
                  pl.BlockSpec((1, d, tf), lambda i, j, te, nu: (te[i], 0, col_of(i, j, nu))),
                  pl.BlockSpec((1, d, tf), lambda i, j, te, nu: (te[i], 0, col_of(i, j, nu))),
                  pl.BlockSpec((1, tf, d), lambda i, j, te, nu: (te[i], col_of(i, j, nu), 0))],
        out_specs=pl.BlockSpec((tm, d), row_map),
        scratch_shapes=[pltpu.VMEM((tm, d), BF16)],
    )
    return pl.pallas_call(
        _grouped_ffn_kernel,
        grid_spec=grid_spec,
        out_shape=jax.ShapeDtypeStruct((r, d), F32),
        compiler_params=_cparams(("arbitrary", "arbitrary"), FFN_VMEM_LIMIT),
        name="moe_grouped_ffn",
    )(tile_expert, n_used, xs, g, wg, wu, wd)


def _combine_kernel(pos_ref, pos_next_ref, x_ref, route_ref, ys_hbm, o_ref, buf0, buf1, sems, *, tc):
    step = pl.program_id(0)
    slot = step % 2

    def gather(p_ref, s):
        def issue(j, c):
            _row_copy(ys_hbm, p_ref[0, 0, 2 * j], buf0.at[s], j, sems.at[s]).start()
            _row_copy(ys_hbm, p_ref[0, 0, 2 * j + 1], buf1.at[s], j, sems.at[s]).start()
            return c

        lax.fori_loop(0, tc, issue, 0, unroll=DMA_ISSUE_UNROLL)

    pl.when(step == 0)(functools.partial(gather, pos_ref, 0))
    pl.when(step + 1 < pl.num_programs(0))(functools.partial(gather, pos_next_ref, 1 - slot))
    for buf in (buf0, buf1):
        pltpu.make_async_copy(ys_hbm.at[pl.ds(0, tc)], buf.at[slot], sems.at[slot]).wait()
    route = route_ref[...]
    g0 = route[:, ROUTE_GATE_LANES[0]:ROUTE_GATE_LANES[0] + 1]
    g1 = route[:, ROUTE_GATE_LANES[1]:ROUTE_GATE_LANES[1] + 1]
    o_ref[...] = x_ref[...] + g0 * buf0[slot] + g1 * buf1[slot]


def _combine(x, route, pos, ys, tc):
    t, d = x.shape
    tc = min(tc, t)
    n = t // tc
    pos3 = pos.reshape(n, 1, 2 * tc)
    pos_block = (1, 1, 2 * tc)
    return pl.pallas_call(
        functools.partial(_combine_kernel, tc=tc),
        grid=(n,),
        in_specs=[pl.BlockSpec(pos_block, lambda i: (i, 0, 0), memory_space=pltpu.SMEM),
                  pl.BlockSpec(pos_block, lambda i: (jnp.minimum(i + 1, n - 1), 0, 0), memory_space=pltpu.SMEM),
                  pl.BlockSpec((tc, d), lambda i: (i, 0)),
                  pl.BlockSpec((tc, 128), lambda i: (i, 0)),
                  pl.BlockSpec(memory_space=pl.ANY)],
        out_specs=pl.BlockSpec((tc, d), lambda i: (i, 0)),
        out_shape=jax.ShapeDtypeStruct((t, d), F32),
        scratch_shapes=[pltpu.VMEM((2, tc, d), F32), pltpu.VMEM((2, tc, d), F32),
                        pltpu.SemaphoreType.DMA((2,))],
        compiler_params=_cparams(("arbitrary",)),
        name="moe_combine",
    )(pos3, pos3, x, route, ys)


MOE_TM = 512


def _route_plan(expert_ids, tm):
    flat = expert_ids.reshape(-1)
    a = flat.shape[0]
    onehot = (flat[None, :] == jnp.arange(N_EXPERTS, dtype=jnp.int32)[:, None]).astype(jnp.int32)
    csum = jnp.cumsum(onehot, axis=1)
    counts = csum[:, -1]
    padded = (counts + tm - 1) // tm * tm
    ends = jnp.cumsum(padded)
    offs = ends - padded
    pos = jnp.sum(onehot * (offs[:, None] + csum - 1), axis=0)
    n_tiles = (a + N_EXPERTS * tm) // tm
    tile_start = jnp.arange(n_tiles, dtype=jnp.int32) * tm
    tile_expert = jnp.minimum(jnp.sum(tile_start[:, None] >= ends[None, :], axis=1), N_EXPERTS - 1)
    n_used = (ends[-1] // tm).reshape(1)
    last = jnp.take(tile_expert, n_used[0] - 1)
    tile_expert = jnp.where(tile_start < ends[-1], tile_expert, last)
    meta = jnp.concatenate([offs + counts, ends, n_used]).astype(jnp.int32)
    return pos.astype(jnp.int32), tile_expert.astype(jnp.int32), n_used.astype(jnp.int32), meta, n_tiles * tm


def _moe(xs_list, routes, g, wg, wu, wd):
    ids = jnp.concatenate([r[:, ROUTE_ID_LANES[0]:ROUTE_ID_LANES[1] + 1] for r in routes]).astype(jnp.int32)
    pos, tile_expert, n_used, meta, n_rows = _route_plan(ids, MOE_TM)
    bounds = np.cumsum([0] + [2 * x.shape[0] for x in xs_list])
    pos_list = [pos[lo:hi] for lo, hi in zip(bounds[:-1], bounds[1:])]
    xs = _dispatch(xs_list, pos, meta, n_rows, MOE_TM, 1024)
    ys = _grouped_ffn(xs, g, tile_expert, n_used, wg, wu, wd, MOE_TM, D_FF)
    return [_combine(x, r, ps, ys, 512) for x, r, ps in zip(xs_list, routes, pos_list)]


def _pool_kernel(u_ref, halo_ref, w_ref, scale_ref, o_ref, *, tiles_per_seq, pos0, tb):
    ti = pl.program_id(0) % tiles_per_seq
    u = u_ref[...]
    halo = jnp.where(ti == 0, 0.0, halo_ref[...])
    ext = jnp.concatenate([halo, u], axis=0)
    s2 = ext + pltpu.roll(ext, 1, 0)
    s4 = s2 + pltpu.roll(s2, 2, 0)
    s8 = s4 + pltpu.roll(s4, 4, 0)
    s16 = s8 + pltpu.roll(s8, 8, 0)
    grp = lax.broadcasted_iota(jnp.int32, (tb, POOL_WIDTH), 1) // (POOL_WIDTH // 4)
    row = lax.broadcasted_iota(jnp.int32, (tb, POOL_WIDTH), 0)
    s = jnp.where(grp == 0, s2[POOL_HALO:],
                  jnp.where(grp == 1, s4[POOL_HALO:],
                            jnp.where(grp == 2, s8[POOL_HALO:], s16[POOL_HALO:])))
    win = jnp.where(grp == 0, 2, jnp.where(grp == 1, 4, jnp.where(grp == 2, 8, 16)))
    cnt = jnp.minimum(win, pos0 + ti * tb + row + 1).astype(F32)
    pooled = s / cnt - u
    o_ref[...] = (_dot(pooled.astype(BF16), w_ref[...]) * scale_ref[...]).astype(o_ref.dtype)


def _pool(proj2d, col, w, scale, *, n_rows, tb, tiles_per_seq, pos0):
    per = tb // POOL_HALO
    return pl.pallas_call(
        functools.partial(_pool_kernel, tiles_per_seq=tiles_per_seq, pos0=pos0, tb=tb),
        grid=(n_rows // tb,),
        in_specs=[pl.BlockSpec((tb, POOL_WIDTH), lambda i: (i, col)),
                  pl.BlockSpec((POOL_HALO, POOL_WIDTH), lambda i: (jnp.maximum(i * per - 1, 0), col)),
                  pl.BlockSpec((POOL_WIDTH, POOL_WIDTH), lambda i: (0, 0)),
                  pl.BlockSpec((1, POOL_WIDTH), lambda i: (0, 0))],
        out_specs=pl.BlockSpec((tb, POOL_WIDTH), lambda i: (i, 0)),
        out_shape=jax.ShapeDtypeStruct((n_rows, POOL_WIDTH), BF16),
        compiler_params=_cparams(("parallel",)),
        name="pool",
    )(proj2d, proj2d, w, scale)


def _t5_bucket_np(rel):
    n = np.maximum(rel, 0)
    max_exact = T5_BUCKETS // 2
    nf = np.maximum(n, max_exact).astype(np.float32)
    large = max_exact + (np.log(nf / max_exact) / math.log(T5_MAX_DIST / max_exact)
                         * (T5_BUCKETS - max_exact)).astype(np.int32)
    large = np.minimum(large, T5_BUCKETS - 1)
    return np.where(n < max_exact, n, large)


def _swa_bias(rel_bias, rel):
    valid = (rel >= 0) & (rel < SWA_WINDOW)
    onehot = jnp.asarray(_t5_bucket_np(rel)[..., None] == np.arange(T5_BUCKETS), F32)
    b = jnp.einsum('qsb,bh->hqs', onehot, rel_bias.astype(F32), precision=lax.Precision.HIGHEST)
    return jnp.where(valid[None], b, NEG)


def _softmax_parts(parts, sink):
    m = sink
    for s in parts:
        m = jnp.maximum(m, jnp.max(s, axis=-1, keepdims=True))
    ps = [jnp.exp(s - m) for s in parts]
    denom = jnp.exp(sink - m)
    for p in ps:
        denom = denom + jnp.sum(p, axis=-1, keepdims=True)
    return ps, denom


def _head_mean_matrix(width):
    h = np.arange(width) // HEAD_DIM
    return jnp.asarray((h[:, None] == h[None, :]) / HEAD_DIM, BF16)


def _head_rms(x, mean_mat, g):
    ms = _dot((x * x).astype(BF16), mean_mat)
    return x * lax.rsqrt(ms + RMS_EPS) * g


SWA_TILE = 512


def _swa_prompt_kernel(sinks_ref, q_ref, kv_ref, halo_ref, qg_ref, kg_ref, mq_ref, mk_ref, bias_ref,
                       y_ref, kn_ref):
    has_prev = pl.program_id(1) > 0
    kw = SWA_KV_HEADS * HEAD_DIM
    blk = SWA_BLOCK
    kv = kv_ref[0]
    halo = halo_ref[0]
    k_ext = jnp.concatenate([halo[:, :kw], kv[:, :kw]], axis=0)
    v_ext = jnp.concatenate([halo[:, kw:], kv[:, kw:]], axis=0).astype(BF16)
    kn = _head_rms(k_ext, mk_ref[...], kg_ref[...])
    kn_ref[0] = kn[SWA_TILE:]
    knb = kn.astype(BF16)
    qn = (_head_rms(q_ref[0], mq_ref[...], qg_ref[...]) * (HEAD_DIM ** -0.5)).astype(BF16)
    row = lax.broadcasted_iota(jnp.int32, (2 * blk, 1), 0)
    col = lax.broadcasted_iota(jnp.int32, (2 * blk, 2 * blk), 1)
    for c in range(SWA_TILE // blk):
        rows = slice(c * blk, (c + 1) * blk)
        keys = slice(c * blk, (c + 2) * blk)
        for kh in range(SWA_KV_HEADS):
            ksl = slice(kh * HEAD_DIM, (kh + 1) * HEAD_DIM)
            h0 = 2 * kh
            q2 = jnp.concatenate([qn[rows, h0 * HEAD_DIM:(h0 + 1) * HEAD_DIM],
                                  qn[rows, (h0 + 1) * HEAD_DIM:(h0 + 2) * HEAD_DIM]], axis=0)
            s = _dot_nt(q2, knb[keys, ksl]) + bias_ref[kh]
            if c == 0:
                s = jnp.where(has_prev | (col >= blk), s, NEG)
            sink = jnp.where(row < blk, sinks_ref[h0], sinks_ref[h0 + 1])
            (p,), denom = _softmax_parts((s,), sink)
            o = _dot(p.astype(BF16), v_ext[keys, ksl]) / denom
            y_ref[0, rows, h0 * HEAD_DIM:(h0 + 1) * HEAD_DIM] = o[:blk].astype(y_ref.dtype)
            y_ref[0, rows, (h0 + 1) * HEAD_DIM:(h0 + 2) * HEAD_DIM] = o[blk:].astype(y_ref.dtype)


def _swa_prompt(proj3d, sinks, qg, kg, bias):
    b, l, _ = proj3d.shape
    kw = SWA_KV_HEADS * HEAD_DIM
    per = SWA_TILE // SWA_BLOCK
    tile = (1, SWA_TILE, MIX_BLOCK)
    const2 = lambda bi, i: (0, 0)
    bias2 = bias.reshape(SWA_KV_HEADS, 2 * SWA_BLOCK, 2 * SWA_BLOCK)
    return pl.pallas_call(
        _swa_prompt_kernel,
        grid=(b, l // SWA_TILE),
        in_specs=[pl.BlockSpec(memory_space=pltpu.SMEM),
                  pl.BlockSpec(tile, lambda bi, i: (bi, i, COL_SWA_Q)),
                  pl.BlockSpec(tile, lambda bi, i: (bi, i, COL_SWA_KV)),
                  pl.BlockSpec((1, SWA_BLOCK, MIX_BLOCK),
                               lambda bi, i: (bi, jnp.maximum(i * per - 1, 0), COL_SWA_KV)),
                  pl.BlockSpec((1, MIX_BLOCK), const2),
                  pl.BlockSpec((1, kw), const2),
                  pl.BlockSpec((MIX_BLOCK, MIX_BLOCK), const2),
                  pl.BlockSpec((kw, kw), const2),
                  pl.BlockSpec((SWA_KV_HEADS, 2 * SWA_BLOCK, 2 * SWA_BLOCK), lambda bi, i: (0, 0, 0))],
        out_specs=[pl.BlockSpec(tile, lambda bi, i: (bi, i, 0)),
                   pl.BlockSpec((1, SWA_BLOCK, kw), lambda bi, i: (bi, 0, 0))],
        out_shape=[jax.ShapeDtypeStruct((b, l, MIX_BLOCK), BF16),
                   jax.ShapeDtypeStruct((b, SWA_BLOCK, kw), F32)],
        compiler_params=_cparams(("parallel", "arbitrary")),
        name="swa_prompt",
    )(sinks, proj3d, proj3d, proj3d, jnp.tile(qg, (1, SWA_HEADS)), jnp.tile(kg, (1, SWA_KV_HEADS)),
      _head_mean_matrix(MIX_BLOCK), _head_mean_matrix(kw), bias2)


def _swa_sample_kernel(sinks_ref, q_ref, kv_ref, ck_ref, cv_ref, qg_ref, kg_ref, bias_ref,
                       y_ref, nk_ref, nv_ref, *, n_new):
    q = q_ref[...]
    kv = kv_ref[...]
    ck = ck_ref[...]
    cv = cv_ref[...]
    qg = qg_ref[...]
    kg = kg_ref[...]
    kw = SWA_KV_HEADS * HEAD_DIM
    bdot = functools.partial(jnp.einsum, preferred_element_type=F32)
    for kh in range(SWA_KV_HEADS):
        ksl = slice(kh * HEAD_DIM, (kh + 1) * HEAD_DIM)
        vsl = slice(kw + kh * HEAD_DIM, kw + (kh + 1) * HEAD_DIM)
        kn = _rms(kv[:, :, ksl], kg)
        vn = kv[:, :, vsl]
        nk_ref[:, :SWA_WINDOW - n_new, ksl] = ck[:, n_new:, ksl]
        nk_ref[:, SWA_WINDOW - n_new:, ksl] = kn
        nv_ref[:, :SWA_WINDOW - n_new, ksl] = cv[:, n_new:, ksl]
        nv_ref[:, SWA_WINDOW - n_new:, ksl] = vn
        for gq in range(SWA_HEADS // SWA_KV_HEADS):
            h = kh * (SWA_HEADS // SWA_KV_HEADS) + gq
            hsl = slice(h * HEAD_DIM, (h + 1) * HEAD_DIM)
            qn = _rms(q[:, :, hsl], qg)
            s_c = bdot('sqd,skd->sqk', qn, ck[:, :, ksl]) * (HEAD_DIM ** -0.5) + bias_ref[h, :, :SWA_WINDOW]
            s_n = bdot('sqd,skd->sqk', qn, kn) * (HEAD_DIM ** -0.5) + bias_ref[h, :, SWA_WINDOW:]
            (p_c, p_n), denom = _softmax_parts((s_c, s_n), sinks_ref[h])
            o = bdot('sqk,skd->sqd', p_c, cv[:, :, ksl]) + bdot('sqk,skd->sqd', p_n, vn)
            y_ref[:, :, hsl] = (o / denom).astype(y_ref.dtype)


def _swa_sample(proj3d, cache_k, cache_v, sinks, qg, kg, bias, s_blk):
    nseq, n_new, _ = proj3d.shape
    kw = SWA_KV_HEADS * HEAD_DIM
    blk = (s_blk, n_new, MIX_BLOCK)
    cblk = (s_blk, SWA_WINDOW, kw)
    return pl.pallas_call(
        functools.partial(_swa_sample_kernel, n_new=n_new),
        grid=(nseq // s_blk,),
        in_specs=[pl.BlockSpec(memory_space=pltpu.SMEM),
                  pl.BlockSpec(blk, lambda i: (i, 0, COL_SWA_Q)),
                  pl.BlockSpec(blk, lambda i: (i, 0, COL_SWA_KV)),
                  pl.BlockSpec(cblk, lambda i: (i, 0, 0)),
                  pl.BlockSpec(cblk, lambda i: (i, 0, 0)),
                  pl.BlockSpec((1, HEAD_DIM), lambda i: (0, 0)),
                  pl.BlockSpec((1, HEAD_DIM), lambda i: (0, 0)),
                  pl.BlockSpec((SWA_HEADS, n_new, SWA_WINDOW + n_new), lambda i: (0, 0, 0))],
        out_specs=[pl.BlockSpec(blk, lambda i: (i, 0, 0)),
                   pl.BlockSpec(cblk, lambda i: (i, 0, 0)),
                   pl.BlockSpec(cblk, lambda i: (i, 0, 0))],
        out_shape=[jax.ShapeDtypeStruct((nseq, n_new, MIX_BLOCK), BF16),
                   jax.ShapeDtypeStruct((nseq, SWA_WINDOW, kw), F32),
                   jax.ShapeDtypeStruct((nseq, SWA_WINDOW, kw), F32)],
        compiler_params=_cparams(("parallel",)),
        name="swa_sample",
    )(sinks, proj3d, proj3d, cache_k, cache_v, qg, kg, bias)


def _ssm_kernel(u_ref, h0_ref, wb_ref, tab_ref, wc_ref, d_ref, wglu_ref, y_ref, hn_ref,
                bu_scr, carry_scr, *, chained, tiles_per_seq, tb):
    n = SSM_N

    def project_in(rows):
        bu_scr[rows, :] = _dot(u_ref[rows, :].astype(BF16), wb_ref[...])

    def project_out(rows):
        y = _dot(bu_scr[rows, :].astype(BF16), wc_ref[...]) + d_ref[...] * u_ref[rows, :]
        y = 0.5 * y * (1.0 + jnp.tanh(math.sqrt(2.0 / math.pi) * (y + 0.044715 * (y * y * y))))
        return (y * _sigmoid(_dot(y.astype(BF16), wglu_ref[...]))).astype(y_ref.dtype)

    def tile_scan(r0, cr, ci):
        hr = bu_scr[pl.ds(r0, SCAN_ROWS), :n]
        hi = bu_scr[pl.ds(r0, SCAN_ROWS), n:]
        for k, shift in enumerate((1, 2, 4)):
            ar, ai = tab_ref[2 * k], tab_ref[2 * k + 1]
            sr, si = pltpu.roll(hr, shift, 0), pltpu.roll(hi, shift, 0)
            hr, hi = hr + ar * sr - ai * si, hi + ar * si + ai * sr
        pr, pi = tab_ref[6], tab_ref[7]
        hr, hi = hr + pr * cr - pi * ci, hi + pr * ci + pi * cr
        bu_scr[pl.ds(r0, SCAN_ROWS), :n] = hr
        bu_scr[pl.ds(r0, SCAN_ROWS), n:] = hi
        return hr[SCAN_ROWS - 1:], hi[SCAN_ROWS - 1:]

    if chained:
        @pl.when(pl.program_id(1) % tiles_per_seq == 0)
        def _():
            carry_scr[...] = jnp.zeros_like(carry_scr)

        project_in(slice(None))
        cr, ci = carry_scr[:, :n], carry_scr[:, n:]
        for r0 in range(0, tb, SCAN_ROWS):
            lr, li = tile_scan(r0, cr, ci)
            cr, ci = jnp.broadcast_to(lr, (SCAN_ROWS, n)), jnp.broadcast_to(li, (SCAN_ROWS, n))
        y_ref[0] = project_out(slice(None))
        carry_scr[:, :n] = cr
        carry_scr[:, n:] = ci
        hn_ref[0, :, :n] = cr
        hn_ref[0, :, n:] = ci
    else:
        project_in(slice(None))

        def body(t, _):
            r0 = pl.multiple_of(t * SCAN_ROWS, SCAN_ROWS)
            h0 = h0_ref[pl.ds(t, 1), :]
            cr = jnp.broadcast_to(h0[:, :n], (SCAN_ROWS, n))
            ci = jnp.broadcast_to(h0[:, n:], (SCAN_ROWS, n))
            lr, li = tile_scan(r0, cr, ci)
            hn_ref[pl.ds(t, 1), :n] = lr
            hn_ref[pl.ds(t, 1), n:] = li
            return 0

        lax.fori_loop(0, tb // SCAN_ROWS, body, 0)
        y_ref[...] = project_out(slice(None))


def _ssm_common_specs(zero_map2, zero_map3):
    return [pl.BlockSpec((SSM_WIDTH, 2 * SSM_N), zero_map2),
            pl.BlockSpec((8, SCAN_ROWS, SSM_N), zero_map3),
            pl.BlockSpec((2 * SSM_N, SSM_WIDTH), zero_map2),
            pl.BlockSpec((1, SSM_WIDTH), zero_map2),
            pl.BlockSpec((SSM_WIDTH, SSM_WIDTH), zero_map2)]


def _ssm_prompt(proj3d, sp, tb):
    b, l, _ = proj3d.shape
    nt = l // tb
    dummy_h0 = jnp.zeros((SCAN_ROWS, 2 * SSM_N), F32)
    kern = functools.partial(_ssm_kernel, chained=True, tiles_per_seq=nt, tb=tb)

    def kernel(u_ref, h0_ref, wb, tab, wc, d, wglu, y_ref, hn_ref, bu_scr, carry_scr):
        kern(u_ref.at[0], h0_ref, wb, tab, wc, d, wglu, y_ref, hn_ref, bu_scr, carry_scr)

    return pl.pallas_call(
        kernel,
        grid=(b, nt),
        in_specs=[pl.BlockSpec((1, tb, SSM_WIDTH), lambda bi, i: (bi, i, COL_SSM)),
                  pl.BlockSpec((SCAN_ROWS, 2 * SSM_N), lambda bi, i: (0, 0))]
                 + _ssm_common_specs(lambda bi, i: (0, 0), lambda bi, i: (0, 0, 0)),
        out_specs=[pl.BlockSpec((1, tb, SSM_WIDTH), lambda bi, i: (bi, i, 0)),
                   pl.BlockSpec((1, SCAN_ROWS, 2 * SSM_N), lambda bi, i: (bi, 0, 0))],
        out_shape=[jax.ShapeDtypeStruct((b, l, SSM_WIDTH), BF16),
                   jax.ShapeDtypeStruct((b, SCAN_ROWS, 2 * SSM_N), F32)],
        scratch_shapes=[pltpu.VMEM((tb, 2 * SSM_N), F32), pltpu.VMEM((SCAN_ROWS, 2 * SSM_N), F32)],
        compiler_params=_cparams(("parallel", "arbitrary")),
        name="ssm_prompt",
    )(proj3d, dummy_h0, sp["wb"], sp["tab"], sp["wc"], sp["d"], sp["wglu"])


def _ssm_sample(proj2d, h0, sp):
    rows = proj2d.shape[0]
    nseq = h0.shape[0]
    kern = functools.partial(_ssm_kernel, chained=False, tiles_per_seq=1, tb=rows)
    return pl.pallas_call(
        kern,
        grid=(1,),
        in_specs=[pl.BlockSpec((rows, SSM_WIDTH), lambda i: (0, COL_SSM)),
                  pl.BlockSpec((nseq, 2 * SSM_N), lambda i: (0, 0))]
                 + _ssm_common_specs(lambda i: (0, 0), lambda i: (0, 0, 0)),
        out_specs=[pl.BlockSpec((rows, SSM_WIDTH), lambda i: (0, 0)),
                   pl.BlockSpec((nseq, 2 * SSM_N), lambda i: (0, 0))],
        out_shape=[jax.ShapeDtypeStruct((rows, SSM_WIDTH), BF16),
                   jax.ShapeDtypeStruct((nseq, 2 * SSM_N), F32)],
        scratch_shapes=[pltpu.VMEM((rows, 2 * SSM_N), F32), pltpu.VMEM((SCAN_ROWS, 2 * SSM_N), F32)],
        compiler_params=_cparams(("arbitrary",)),
        name="ssm_sample",
    )(proj2d, h0, sp["wb"], sp["tab"], sp["wc"], sp["d"], sp["wglu"])


def _ssm_params(lam_re, lam_im, log_dt, b_re, b_im, c_re, c_im, d_skip, w_glu):
    lr, li = lam_re.astype(F32), lam_im.astype(F32)
    dt = jnp.exp(log_dt.astype(F32))[:, None]
    mag = jnp.exp(lr * dt)
    ab_re, ab_im = mag * jnp.cos(li * dt), mag * jnp.sin(li * dt)
    den = lr * lr + li * li
    nr = ab_re - 1.0
    f_re = (nr * lr + ab_im * li) / den
    f_im = (ab_im * lr - nr * li) / den
    br, bi = b_re.astype(F32), b_im.astype(F32)
    bb_re = f_re[..., None] * br - f_im[..., None] * bi
    bb_im = f_re[..., None] * bi + f_im[..., None] * br
    eye = jnp.eye(SSM_GROUPS, dtype=F32)

    def in_mat(bb):
        return jnp.einsum('gpc,gh->gchp', bb, eye).reshape(SSM_WIDTH, SSM_N)

    def out_mat(c):
        return jnp.einsum('gcp,gh->gphc', c.astype(F32), eye).reshape(SSM_N, SSM_WIDTH)

    wb = jnp.concatenate([in_mat(bb_re), in_mat(bb_im)], axis=1).astype(BF16)
    wc = jnp.concatenate([out_mat(c_re), -out_mat(c_im)], axis=0).astype(BF16)

    ar, ai = ab_re.reshape(1, SSM_N), ab_im.reshape(1, SSM_N)

    def cmul(x, y):
        return (x[0] * y[0] - x[1] * y[1], x[0] * y[1] + x[1] * y[0])

    pw = [(ar, ai)]
    for _ in range(SCAN_ROWS - 1):
        pw.append(cmul(pw[-1], (ar, ai)))
    row = jnp.arange(SCAN_ROWS)[:, None]
    tabs = []
    for shift in (1, 2, 4):
        for part in pw[shift - 1]:
            tabs.append(jnp.where(row >= shift, part, 0.0))
    tabs.append(jnp.concatenate([p[0] for p in pw], axis=0))
    tabs.append(jnp.concatenate([p[1] for p in pw], axis=0))
    tab = jnp.stack([jnp.broadcast_to(t, (SCAN_ROWS, SSM_N)) for t in tabs])
    return dict(wb=wb, tab=tab, wc=wc, d=d_skip.astype(F32).reshape(1, SSM_WIDTH), wglu=w_glu.astype(BF16))


_RET_G = 1.0 - np.exp2(-5.0 - np.arange(RET_HEADS, dtype=np.float64))


def _ret_consts(chunk, n_rows):
    idx = np.arange(n_rows)
    loc = idx % chunk
    same = (idx[:, None] // chunk) == (idx[None, :] // chunk)
    diff = loc[:, None] - loc[None, :]
    dec = np.where(same & (diff >= 0), _RET_G[:, None, None] ** np.maximum(diff, 0)[None], 0.0)
    qdec = np.repeat((_RET_G[None, :] ** (loc[:, None] + 1.0)), HEAD_DIM, axis=1)
    kdec = np.repeat((_RET_G[None, :] ** (chunk - 1.0 - loc[:, None])), HEAD_DIM, axis=1)
    return (jnp.asarray(dec, F32), jnp.asarray(qdec, F32), jnp.asarray(kdec, F32),
            jnp.asarray(_RET_G ** chunk, F32))


def _rope_tables(pos):
    half = HEAD_DIM // 2
    theta = 1.0 / (ROPE_BASE ** np.linspace(0.0, 1.0, half))
    ang = np.asarray(pos, np.float64)[:, None] * theta[None, :]
    cos = np.repeat(np.cos(ang), 2, axis=1)
    sin = np.repeat(np.sin(ang), 2, axis=1) * np.tile([-1.0, 1.0], half)[None]
    return (jnp.asarray(np.tile(cos, (1, RET_HEADS)), F32), jnp.asarray(np.tile(sin, (1, RET_HEADS)), F32))


def _rotate_pairs(x, cos, sin_signed):
    lane = lax.broadcasted_iota(jnp.int32, x.shape, 1)
    nxt = pltpu.roll(x, x.shape[1] - 1, 1)
    prv = pltpu.roll(x, 1, 1)
    return x * cos + jnp.where(lane % 2 == 0, nxt, prv) * sin_signed


def _ret_head_out(o, gate, norm):
    ms = jnp.mean(o * o, axis=-1, keepdims=True)
    return o * lax.rsqrt(ms + RMS_EPS) * norm * (gate * _sigmoid(gate))


RET_TILE = 512


def _ret_prompt_kernel(gc_ref, q_ref, k_ref, v_ref, g_ref, cos_ref, sin_ref, dec_ref, qdec_ref,
                       kdec_ref, norm_ref, mh_ref, y_ref, r_ref, o_scr):
    @pl.when(pl.program_id(1) == 0)
    def _():
        r_ref[...] = jnp.zeros_like(r_ref)

    cos, sin = cos_ref[...], sin_ref[...]
    q = _rotate_pairs(q_ref[0], cos, sin)
    k = _rotate_pairs(k_ref[0], cos, sin) * (HEAD_DIM ** -0.5)
    qb, kb, vb = q.astype(BF16), k.astype(BF16), v_ref[0].astype(BF16)
    kdb = (k * kdec_ref[...]).astype(BF16)
    for c in range(RET_TILE // RET_CHUNK):
        rows = slice(c * RET_CHUNK, (c + 1) * RET_CHUNK)
        for h in range(RET_HEADS):
            sl = slice(h * HEAD_DIM, (h + 1) * HEAD_DIM)
            qh, vh = qb[rows, sl], vb[rows, sl]
            s = _dot_nt(qh, kb[rows, sl]) * dec_ref[h]
            r = r_ref[0, h]
            o_scr[rows, sl] = _dot(s.astype(BF16), vh) + _dot(qh, r.astype(BF16)) * qdec_ref[rows, sl]
            r_ref[0, h] = gc_ref[h] * r + _dot_tn(kdb[rows, sl], vh)
    g = g_ref[0]
    y_ref[0] = (_head_rms(o_scr[...], mh_ref[...], norm_ref[...]) * (g * _sigmoid(g))).astype(y_ref.dtype)


def _ret_prompt(proj3d, cos, sin, norm):
    b, l, _ = proj3d.shape
    c = RET_CHUNK
    dec, _, _, gc = _ret_consts(c, c)
    _, qdec, kdec, _ = _ret_consts(c, RET_TILE)
    blk = (1, RET_TILE, MIX_BLOCK)
    tspec = pl.BlockSpec((RET_TILE, MIX_BLOCK), lambda bi, i: (i, 0))
    cspec = pl.BlockSpec((RET_TILE, MIX_BLOCK), lambda bi, i: (0, 0))
    return pl.pallas_call(
        _ret_prompt_kernel,
        grid=(b, l // RET_TILE),
        in_specs=[pl.BlockSpec(memory_space=pltpu.SMEM),
                  pl.BlockSpec(blk, lambda bi, i: (bi, i, COL_RET_Q)),
                  pl.BlockSpec(blk, lambda bi, i: (bi, i, COL_RET_K)),
                  pl.BlockSpec(blk, lambda bi, i: (bi, i, COL_RET_V)),
                  pl.BlockSpec(blk, lambda bi, i: (bi, i, COL_RET_G)),
                  tspec, tspec,
                  pl.BlockSpec((RET_HEADS, c, c), lambda bi, i: (0, 0, 0)),
                  cspec, cspec,
                  pl.BlockSpec((1, MIX_BLOCK), lambda bi, i: (0, 0)),
                  pl.BlockSpec((MIX_BLOCK, MIX_BLOCK), lambda bi, i: (0, 0))],
        out_specs=[pl.BlockSpec(blk, lambda bi, i: (bi, i, 0)),
                   pl.BlockSpec((1, RET_HEADS, HEAD_DIM, HEAD_DIM), lambda bi, i: (bi, 0, 0, 0))],
        out_shape=[jax.ShapeDtypeStruct((b, l, MIX_BLOCK), BF16),
                   jax.ShapeDtypeStruct((b, RET_HEADS, HEAD_DIM, HEAD_DIM), F32)],
        scratch_shapes=[pltpu.VMEM((RET_TILE, MIX_BLOCK), F32)],
        compiler_params=_cparams(("parallel", "arbitrary")),
        name="ret_prompt",
    )(gc, proj3d, proj3d, proj3d, proj3d, cos, sin, dec, qdec, kdec, norm, _head_mean_matrix(MIX_BLOCK))


def _ret_sample_kernel(gc_ref, q_ref, k_ref, v_ref, g_ref, cos_ref, sin_ref, dec_ref, qdec_ref,
                       kdec_ref, norm_ref, r0_ref, y_ref, rn_ref, *, n_new, s_blk):
    cos, sin = cos_ref[...], sin_ref[...]
    q = _rotate_pairs(q_ref[...], cos, sin)
    k = _rotate_pairs(k_ref[...], cos, sin) * (HEAD_DIM ** -0.5)
    v = v_ref[...]
    g = g_ref[...]
    kd = k * kdec_ref[...]
    qdec = qdec_ref[...]
    norm = norm_ref[...]
    rows = s_blk * n_new
    seq = lax.broadcasted_iota(jnp.int32, (rows, HEAD_DIM), 0) // n_new
    for h in range(RET_HEADS):
        sl = slice(h * HEAD_DIM, (h + 1) * HEAD_DIM)
        qf, kdf = q[:, sl], kd[:, sl]
        qh, kh, vh = qf.astype(BF16), k[:, sl].astype(BF16), v[:, sl].astype(BF16)
        s = _dot_nt(qh, kh) * dec_ref[h]
        cross = jnp.zeros((rows, HEAD_DIM), F32)
        for si in range(s_blk):
            mine = seq == si
            r = r0_ref[si, h]
            cross = cross + _dot(jnp.where(mine, qf, 0.0).astype(BF16), r.astype(BF16))
            rn_ref[si, h] = gc_ref[h] * r + _dot_tn(jnp.where(mine, kdf, 0.0).astype(BF16), vh)
        o = _dot(s.astype(BF16), vh) + cross * qdec[:, sl]
        y_ref[:, sl] = _ret_head_out(o, g[:, sl], norm[:, sl]).astype(y_ref.dtype)


def _ret_sample(proj2d, r0, cos, sin, norm, n_new, s_blk):
    rows = s_blk * n_new
    nseq = r0.shape[0]
    dec, qdec, kdec, gc = _ret_consts(n_new, rows)
    blk = (rows, MIX_BLOCK)
    cspec = pl.BlockSpec(blk, lambda i: (0, 0))
    rblk = (s_blk, RET_HEADS, HEAD_DIM, HEAD_DIM)
    return pl.pallas_call(
        functools.partial(_ret_sample_kernel, n_new=n_new, s_blk=s_blk),
        grid=(nseq // s_blk,),
        in_specs=[pl.BlockSpec(memory_space=pltpu.SMEM),
                  pl.BlockSpec(blk, lambda i: (i, COL_RET_Q)),
                  pl.BlockSpec(blk, lambda i: (i, COL_RET_K)),
                  pl.BlockSpec(blk, lambda i: (i, COL_RET_V)),
                  pl.BlockSpec(blk, lambda i: (i, COL_RET_G)),
                  cspec, cspec,
                  pl.BlockSpec((RET_HEADS, rows, rows), lambda i: (0, 0, 0)),
                  cspec, cspec,
                  pl.BlockSpec((1, MIX_BLOCK), lambda i: (0, 0)),
                  pl.BlockSpec(rblk, lambda i: (i, 0, 0, 0))],
        out_specs=[pl.BlockSpec(blk, lambda i: (i, 0)),
                   pl.BlockSpec(rblk, lambda i: (i, 0, 0, 0))],
        out_shape=[jax.ShapeDtypeStruct((nseq * n_new, MIX_BLOCK), BF16),
                   jax.ShapeDtypeStruct((nseq, RET_HEADS, HEAD_DIM, HEAD_DIM), F32)],
        compiler_params=_cparams(("parallel",)),
        name="ret_sample",
    )(gc, proj2d, proj2d, proj2d, proj2d, cos, sin, dec, qdec, kdec, norm, r0)


def _block_diag(w):
    g, n, _ = w.shape
    return jnp.einsum('gcd,gh->gchd', w, jnp.eye(g, dtype=w.dtype)).reshape(g * n, g * n)


def _layer_params(l, p):
    return dict(
        norm_mix=p['norm_mix'][l].reshape(1, D_MODEL),
        norm_ffn=p['norm_ffn'][l].reshape(1, D_MODEL),
        w_in=p['w_in'][l].astype(BF16),
        w_out=p['w_out'][l].astype(BF16),
        pool_w=_block_diag(p['pool_w'][l].astype(F32)).astype(BF16),
        pool_scale=p['pool_scale'][l].astype(F32).reshape(1, POOL_WIDTH),
        qg=p['swa_q_norm'][l].astype(F32).reshape(1, HEAD_DIM),
        kg=p['swa_k_norm'][l].astype(F32).reshape(1, HEAD_DIM),
        sinks=p['swa_sinks'][l].astype(F32),
        ssm=_ssm_params(p['ssm_lambda_re'][l], p['ssm_lambda_im'][l], p['ssm_log_dt'][l],
                        p['ssm_b_re'][l], p['ssm_b_im'][l], p['ssm_c_re'][l], p['ssm_c_im'][l],
                        p['ssm_d'][l], p['ssm_w_glu'][l]),
        ret_norm=p['ret_norm'][l].astype(F32).reshape(1, MIX_BLOCK),
    )


def _channel_mix(streams, l, lp, p):
    i = l // 2
    g, w_out = lp['norm_ffn'], lp['w_out']
    if l % 2 == 0:
        wg, wu, wd = (p[k][i].astype(BF16) for k in ('ffn_w_gate', 'ffn_w_up', 'ffn_w_down'))
        return [_out_proj_ffn(x, ys, w_out, g, wg, wu, wd, 512, D_FF) for x, ys in streams]
    wr3 = _router_weights(p['moe_router'][i])
    wg, wu, wd = (p[k][i].astype(BF16) for k in ('moe_w_gate', 'moe_w_up', 'moe_w_down'))
    x1s, routes = zip(*[_out_proj_router(x, ys, w_out, g, wr3, 512) for x, ys in streams])
    return _moe(x1s, routes, g, wg, wu, wd)


def _mix_prompt(x2, b, l, lp, bias, cos, sin):
    proj2 = _norm_matmul(x2, lp['norm_mix'], lp['w_in'], 512)
    proj3 = proj2.reshape(b, l, IN_WIDTH)
    tb = 512
    y_pool = _pool(proj2, COL_POOL, lp['pool_w'], lp['pool_scale'], n_rows=b * l, tb=tb,
                   tiles_per_seq=l // tb, pos0=0)
    y_swa, kn = _swa_prompt(proj3, lp['sinks'], lp['qg'], lp['kg'], bias)
    y_ssm, hn = _ssm_prompt(proj3, lp['ssm'], 512)
    y_ret, rn = _ret_prompt(proj3, cos, sin, lp['ret_norm'])
    ys = (y_pool, y_swa.reshape(b * l, MIX_BLOCK), y_ssm.reshape(b * l, MIX_BLOCK),
          y_ret.reshape(b * l, MIX_BLOCK))
    kw = SWA_KV_HEADS * HEAD_DIM
    hn = hn[:, 0]
    states = (proj3[:, l - POOL_BUF:, :POOL_WIDTH],
              kn.reshape(b, SWA_WINDOW, SWA_KV_HEADS, HEAD_DIM),
              proj3[:, l - SWA_WINDOW:, COL_SWA_KV * MIX_BLOCK + kw:(COL_SWA_KV + 1) * MIX_BLOCK]
              .reshape(b, SWA_WINDOW, SWA_KV_HEADS, HEAD_DIM),
              jnp.stack([hn[:, :SSM_N], hn[:, SSM_N:]], axis=-1).reshape(b, SSM_GROUPS, SSM_STATE, 2),
              rn)
    return ys, states


SAMPLE_SEQ_BLOCK = 16


def _mix_sample(x2, nseq, n_new, start_pos, lp, st, bias, cos, sin):
    state_pool, cache_k, cache_v, state_ssm, state_ret = st
    rows = nseq * n_new
    wb = cache_k.shape[1]
    kw = SWA_KV_HEADS * HEAD_DIM
    ext_rows = POOL_HALO + n_new
    proj2 = _norm_matmul(x2, lp['norm_mix'], lp['w_in'], 512)
    proj3 = proj2.reshape(nseq, n_new, IN_WIDTH)
    u_pool = proj3[:, :, :POOL_WIDTH]
    buf = state_pool.astype(F32)
    ext = jnp.concatenate([jnp.zeros((nseq, POOL_HALO - POOL_BUF, POOL_WIDTH), F32), buf, u_pool], axis=1)
    y_pool = _pool(ext.reshape(nseq * ext_rows, POOL_WIDTH), 0, lp['pool_w'], lp['pool_scale'],
                   n_rows=nseq * ext_rows, tb=nseq * ext_rows, tiles_per_seq=1, pos0=start_pos)
    y_pool = y_pool.reshape(nseq, ext_rows, POOL_WIDTH)[:, POOL_HALO:].reshape(rows, POOL_WIDTH)
    y_swa, nk, nv = _swa_sample(proj3, cache_k.reshape(nseq, wb, kw).astype(F32),
                                cache_v.reshape(nseq, wb, kw).astype(F32),
                                lp['sinks'], lp['qg'], lp['kg'], bias, SAMPLE_SEQ_BLOCK)
    h0 = state_ssm.astype(F32).reshape(nseq, SSM_N, 2)
    h0 = jnp.concatenate([h0[..., 0], h0[..., 1]], axis=1)
    y_ssm, hn = _ssm_sample(proj2, h0, lp['ssm'])
    y_ret, rn = _ret_sample(proj2, state_ret.astype(F32), cos, sin, lp['ret_norm'], n_new, SAMPLE_SEQ_BLOCK)
    ys = (y_pool, y_swa.reshape(rows, MIX_BLOCK), y_ssm, y_ret)
    states = (jnp.concatenate([buf, u_pool], axis=1)[:, -POOL_BUF:],
              nk.reshape(nseq, SWA_WINDOW, SWA_KV_HEADS, HEAD_DIM),
              nv.reshape(nseq, SWA_WINDOW, SWA_KV_HEADS, HEAD_DIM),
              jnp.stack([hn[:, :SSM_N], hn[:, SSM_N:]], axis=-1).reshape(nseq, SSM_GROUPS, SSM_STATE, 2),
              rn)
    return ys, states


def _forward(x_prompt, x_sample, past_len, sample_state, p, rel_bias):
    b, l, d = x_prompt.shape
    nseq, n_new, _ = x_sample.shape
    wb = sample_state[1].shape[2]
    depth = p['norm_mix'].shape[0]
    bias_p = _swa_bias(rel_bias, np.arange(SWA_BLOCK)[:, None] - np.arange(2 * SWA_BLOCK)[None, :] + SWA_BLOCK)
    bias_s = _swa_bias(rel_bias, np.arange(n_new)[:, None] - np.arange(wb + n_new)[None, :] + wb)
    rope_p = _rope_tables(np.arange(l))
    rope_s = _rope_tables(past_len + (np.arange(SAMPLE_SEQ_BLOCK * n_new) % n_new))
    xp = x_prompt.reshape(b * l, d)
    xs = x_sample.reshape(nseq * n_new, d)
    st_p, st_s = [], []
    for li in range(depth):
        lp = _layer_params(li, p)
        yp, sp = _mix_prompt(xp, b, l, lp, bias_p, *rope_p)
        ys, ss = _mix_sample(xs, nseq, n_new, past_len, lp, [s[li] for s in sample_state], bias_s, *rope_s)
        xp, xs = _channel_mix([(xp, yp), (xs, ys)], li, lp, p)
        st_p.append(sp)
        st_s.append(ss)
    outs = [xp.reshape(b, l, d), xs.reshape(nseq, n_new, d)]
    for k in range(5):
        outs.append(jnp.stack([s[k] for s in st_p]))
        outs.append(jnp.stack([s[k] for s in st_s]))
    return tuple(outs)


PAST_LEN = 16384


def kernel(x_prompt, x_sample, state_pool, cache_swa_k, cache_swa_v, state_ssm, state_ret,
           norm_mix, norm_ffn, w_in, w_out, pool_w, pool_scale, swa_q_norm, swa_k_norm, swa_sinks,
           rel_bias, ssm_lambda_re, ssm_lambda_im, ssm_log_dt, ssm_b_re, ssm_b_im, ssm_c_re, ssm_c_im,
           ssm_d, ssm_w_glu, ret_norm, ffn_w_gate, ffn_w_up, ffn_w_down, moe_router, moe_w_gate,
           moe_w_up, moe_w_down):
    p = dict(norm_mix=norm_mix, norm_ffn=norm_ffn, w_in=w_in, w_out=w_out, pool_w=pool_w,
             pool_scale=pool_scale, swa_q_norm=swa_q_norm, swa_k_norm=swa_k_norm, swa_sinks=swa_sinks,
             ssm_lambda_re=ssm_lambda_re, ssm_lambda_im=ssm_lambda_im, ssm_log_dt=ssm_log_dt,
             ssm_b_re=ssm_b_re, ssm_b_im=ssm_b_im, ssm_c_re=ssm_c_re, ssm_c_im=ssm_c_im,
             ssm_d=ssm_d, ssm_w_glu=ssm_w_glu, ret_norm=ret_norm,
             ffn_w_gate=ffn_w_gate, ffn_w_up=ffn_w_up, ffn_w_down=ffn_w_down, moe_router=moe_router,
             moe_w_gate=moe_w_gate, moe_w_up=moe_w_up, moe_w_down=moe_w_down)
    return _forward(x_prompt, x_sample, PAST_LEN,
                    (state_pool, cache_swa_k, cache_swa_v, state_ssm, state_ret), p, rel_bias)
```

```python
import functools
import math

import numpy as np
import jax
import jax.numpy as jnp
from jax import lax
from jax.experimental import pallas as pl
from jax.experimental.pallas import tpu as pltpu

F32 = jnp.float32
BF16 = jnp.bfloat16

D_MODEL = 1024
HEAD_DIM = 64
POOL_WIDTH = 256
POOL_WINDOWS = (2, 4, 8, 16)
POOL_BUF = 15
POOL_HALO = 16
SWA_HEADS = 4
SWA_KV_HEADS = 2
SWA_WINDOW = 128
SWA_BLOCK = 128
SSM_WIDTH = 256
SSM_CH = 16
SSM_GROUPS = 16
SSM_STATE = 64
SSM_N = SSM_GROUPS * SSM_STATE
RET_HEADS = 4
RET_CHUNK = 128
ROPE_BASE = 10000.0
IN_WIDTH = 2048
MIX_BLOCK = 256
D_FF = 2816
N_EXPERTS = 8
T5_BUCKETS = 32
T5_MAX_DIST = 128
RMS_EPS = 1e-6
NEG = -1e30
SUBLANES = 8
SCAN_ROWS = SUBLANES

COL_POOL, COL_SWA_Q, COL_SWA_KV, COL_SSM, COL_RET_Q, COL_RET_K, COL_RET_V, COL_RET_G = range(8)

VMEM_LIMIT = 48 * 1024 * 1024
FFN_VMEM_LIMIT = 58 * 1024 * 1024


def _cparams(sem, vmem=VMEM_LIMIT):
    return pltpu.CompilerParams(dimension_semantics=sem, vmem_limit_bytes=vmem)


def _rms(x, g):
    ms = jnp.mean(x * x, axis=-1, keepdims=True)
    return x * lax.rsqrt(ms + RMS_EPS) * g


def _dot(a, b):
    return jnp.dot(a, b, preferred_element_type=F32)


def _dot_nt(a, b):
    return lax.dot_general(a, b, (((1,), (1,)), ((), ())), preferred_element_type=F32)


def _dot_tn(a, b):
    return lax.dot_general(a, b, (((0,), (0,)), ((), ())), preferred_element_type=F32)


def _sigmoid(x):
    return 1.0 / (1.0 + jnp.exp(-x))


def _norm_matmul_kernel(x_ref, g_ref, w_ref, o_ref):
    h = _rms(x_ref[...], g_ref[...]).astype(BF16)
    o_ref[...] = _dot(h, w_ref[...])


def _norm_matmul(x, g, w, tm):
    t, d = x.shape
    tm = min(tm, t)
    n = w.shape[1]
    return pl.pallas_call(
        _norm_matmul_kernel,
        grid=(t // tm,),
        in_specs=[pl.BlockSpec((tm, d), lambda i: (i, 0)),
                  pl.BlockSpec((1, d), lambda i: (0, 0)),
                  pl.BlockSpec((d, n), lambda i: (0, 0))],
        out_specs=pl.BlockSpec((tm, n), lambda i: (i, 0)),
        out_shape=jax.ShapeDtypeStruct((t, n), F32),
        compiler_params=_cparams(("parallel",)),
        name="norm_matmul",
    )(x, g, w)


FFN_SUBCHUNK = 512


def _swiglu_chunk(h, wg_ref, wu_ref, wd_ref):
    tf = wg_ref.shape[1]
    y = None
    for lo in range(0, tf, FFN_SUBCHUNK):
        hi = min(lo + FFN_SUBCHUNK, tf)
        a = _dot(h, wg_ref[:, lo:hi])
        b = _dot(h, wu_ref[:, lo:hi])
        part = _dot((a * _sigmoid(a) * b).astype(BF16), wd_ref[lo:hi, :])
        y = part if y is None else y + part
    return y


def _mixed_residual(x_ref, y_refs, w_ref, rows=slice(None)):
    y = jnp.concatenate([y_ref[rows, :] for y_ref in y_refs], axis=1)
    return x_ref[rows, :] + _dot(y, w_ref[...])


def _mix_in_specs(tm, d, imap):
    yspec = pl.BlockSpec((tm, MIX_BLOCK), imap(lambda i: (i, 0)))
    return [pl.BlockSpec((tm, d), imap(lambda i: (i, 0))), yspec, yspec, yspec, yspec,
            pl.BlockSpec((d, d), imap(lambda i: (0, 0)))]


def _out_proj_ffn_kernel(x_ref, y0_ref, y1_ref, y2_ref, y3_ref, wo_ref, g_ref, wg_ref, wu_ref, wd_ref,
                         o_ref, h_scr):
    @pl.when(pl.program_id(1) == 0)
    def _():
        x1 = _mixed_residual(x_ref, (y0_ref, y1_ref, y2_ref, y3_ref), wo_ref)
        h_scr[...] = _rms(x1, g_ref[...]).astype(BF16)
        o_ref[...] = x1

    o_ref[...] += _swiglu_chunk(h_scr[...], wg_ref, wu_ref, wd_ref)


def _out_proj_ffn(x, ys, w_out, g, wg, wu, wd, tm, tf):
    t, d = x.shape
    tm = min(tm, t)
    f = wg.shape[1]
    imap = lambda fn: (lambda i, j: fn(i))
    once = dict(pipeline_mode=pl.Buffered(1)) if tf == f else {}
    return pl.pallas_call(
        _out_proj_ffn_kernel,
        grid=(t // tm, f // tf),
        in_specs=_mix_in_specs(tm, d, imap)
                 + [pl.BlockSpec((1, d), lambda i, j: (0, 0)),
                    pl.BlockSpec((d, tf), lambda i, j: (0, j), **once),
                    pl.BlockSpec((d, tf), lambda i, j: (0, j), **once),
                    pl.BlockSpec((tf, d), lambda i, j: (j, 0), **once)],
        out_specs=pl.BlockSpec((tm, d), lambda i, j: (i, 0)),
        out_shape=jax.ShapeDtypeStruct((t, d), F32),
        scratch_shapes=[pltpu.VMEM((tm, d), BF16)],
        compiler_params=_cparams(("parallel", "arbitrary"), FFN_VMEM_LIMIT),
        name="out_proj_ffn",
    )(x, *ys, w_out, g, wg, wu, wd)


ROUTE_ID_LANES = (0, 1)
ROUTE_GATE_LANES = (2, 3)


def _split_bf16(x):
    hi = x.astype(BF16)
    return hi, (x - hi.astype(F32)).astype(BF16)


ROUTER_ROWS = 256


def _out_proj_router_kernel(x_ref, y0_ref, y1_ref, y2_ref, y3_ref, wo_ref, g_ref, wr_ref, x1_ref, c_ref):
    tm = x_ref.shape[0]
    step = min(ROUTER_ROWS, tm)
    for r in range(0, tm, step):
        _route_rows(slice(r, r + step), x_ref, (y0_ref, y1_ref, y2_ref, y3_ref), wo_ref, g_ref, wr_ref,
                    x1_ref, c_ref)


def _route_rows(rows, x_ref, y_refs, wo_ref, g_ref, wr_ref, x1_ref, c_ref):
    x1 = _mixed_residual(x_ref, y_refs, wo_ref, rows)
    x1_ref[rows, :] = x1
    h_hi, h_lo = _split_bf16(_rms(x1, g_ref[...]))
    logits = _dot(jnp.concatenate([h_hi, h_lo, h_hi], axis=1), wr_ref[...])
    lane = lax.broadcasted_iota(jnp.int32, logits.shape, 1).astype(F32)
    lg = jnp.where(lane < N_EXPERTS, logits, NEG)
    m1 = jnp.max(lg, axis=-1, keepdims=True)
    i1 = jnp.min(jnp.where(lg == m1, lane, 128.0), axis=-1, keepdims=True)
    lg2 = jnp.where(lane == i1, NEG, lg)
    m2 = jnp.max(lg2, axis=-1, keepdims=True)
    i2 = jnp.min(jnp.where(lg2 == m2, lane, 128.0), axis=-1, keepdims=True)
    ex = jnp.exp(m2 - m1)
    vals = (i1, i2, 1.0 / (1.0 + ex), ex / (1.0 + ex))
    out = jnp.zeros_like(logits)
    for ln, v in zip(ROUTE_ID_LANES + ROUTE_GATE_LANES, vals):
        out = jnp.where(lane == ln, v, out)
    c_ref[rows, :] = out


def _router_weights(wr):
    w = jnp.pad(wr.astype(F32), ((0, 0), (0, 128 - N_EXPERTS)))
    hi, lo = _split_bf16(w)
    return jnp.concatenate([hi, hi, lo], axis=0)


def _out_proj_router(x, ys, w_out, g, wr3, tm):
    t, d = x.shape
    tm = min(tm, t)
    imap = lambda fn: fn
    return pl.pallas_call(
        _out_proj_router_kernel,
        grid=(t // tm,),
        in_specs=_mix_in_specs(tm, d, imap)
                 + [pl.BlockSpec((1, d), lambda i: (0, 0)),
                    pl.BlockSpec((3 * d, 128), lambda i: (0, 0))],
        out_specs=[pl.BlockSpec((tm, d), lambda i: (i, 0)),
                   pl.BlockSpec((tm, 128), lambda i: (i, 0))],
        out_shape=[jax.ShapeDtypeStruct((t, d), F32), jax.ShapeDtypeStruct((t, 128), F32)],
        compiler_params=_cparams(("parallel",)),
        name="out_proj_router",
    )(x, *ys, w_out, g, wr3)


DMA_ISSUE_UNROLL = 8


def _row_copy(src, i, dst, j, sem):
    return pltpu.make_async_copy(src.at[pl.ds(i, 1)], dst.at[pl.ds(j, 1)], sem)


def _dispatch_kernel(meta_ref, pos_ref, *rest, td, tm, n_tiles, first_step):
    n_streams = len(first_step) - 1
    x_refs, (xs_hbm, zero_scr, sem) = rest[:n_streams], rest[n_streams:]
    step = pl.program_id(0)

    def zero_row(r):
        return _row_copy(zero_scr, 0, xs_hbm, r, sem)

    @pl.when(step == 0)
    def _():
        zero_scr[...] = jnp.zeros_like(zero_scr)
        n_used = meta_ref[2 * N_EXPERTS]
        tile_fills = [(i >= n_used, pltpu.make_async_copy(zero_scr, xs_hbm.at[pl.ds(i * tm, tm)], sem))
                      for i in range(n_tiles)]
        for cond, copy in tile_fills:
            pl.when(cond)(copy.start)
        for e in range(N_EXPERTS):
            lax.fori_loop(meta_ref[e], meta_ref[N_EXPERTS + e], lambda r, c: (zero_row(r).start(), c)[1], 0)
        for e in range(N_EXPERTS):
            lax.fori_loop(meta_ref[e], meta_ref[N_EXPERTS + e], lambda r, c: (zero_row(r).wait(), c)[1], 0)
        for cond, copy in tile_fills:
            pl.when(cond)(copy.wait)

    def scatter(x_ref):
        def issue(j, c):
            for k in range(2):
                _row_copy(x_ref, j, xs_hbm, pos_ref[0, 0, 2 * j + k], sem).start()
            return c

        lax.fori_loop(0, td, issue, 0, unroll=DMA_ISSUE_UNROLL)
        for _ in range(2):
            pltpu.make_async_copy(x_ref, xs_hbm.at[pl.ds(0, td)], sem).wait()

    for s, x_ref in enumerate(x_refs):
        pl.when((step >= first_step[s]) & (step < first_step[s + 1]))(functools.partial(scatter, x_ref))


def _dispatch(xs_list, pos, meta, n_rows, tm, td):
    d = xs_list[0].shape[1]
    td = min([td] + [x.shape[0] for x in xs_list])
    first_step = [0]
    for x in xs_list:
        first_step.append(first_step[-1] + x.shape[0] // td)
    n_steps = first_step[-1]
    pos3 = pos.reshape(n_steps, 1, 2 * td)

    def tile_map(s):
        lo, hi = first_step[s], first_step[s + 1]
        return lambda i, m: (jnp.clip(i, lo, hi - 1) - lo, 0)

    in_specs = [pl.BlockSpec((1, 1, 2 * td), lambda i, m: (i, 0, 0), memory_space=pltpu.SMEM)]
    in_specs += [pl.BlockSpec((td, d), tile_map(s)) for s in range(len(xs_list))]
    return pl.pallas_call(
        functools.partial(_dispatch_kernel, td=td, tm=tm, n_tiles=n_rows // tm, first_step=tuple(first_step)),
        grid_spec=pltpu.PrefetchScalarGridSpec(
            num_scalar_prefetch=1, grid=(n_steps,), in_specs=in_specs,
            out_specs=pl.BlockSpec(memory_space=pl.ANY),
            scratch_shapes=[pltpu.VMEM((tm, d), F32), pltpu.SemaphoreType.DMA]),
        out_shape=jax.ShapeDtypeStruct((n_rows, d), F32),
        compiler_params=_cparams(("arbitrary",)),
        name="moe_dispatch",
    )(meta, pos3, *xs_list)


def _grouped_ffn_kernel(te_ref, nu_ref, x_ref, g_ref, wg_ref, wu_ref, wd_ref, o_ref, h_scr):
    del te_ref
    j = pl.program_id(1)
    used = pl.program_id(0) < nu_ref[0]

    @pl.when(jnp.logical_not(used) & (j == 0))
    def _():
        o_ref[...] = jnp.zeros_like(o_ref)

    @pl.when(used)
    def _():
        @pl.when(j == 0)
        def _():
            h_scr[...] = _rms(x_ref[...], g_ref[...]).astype(BF16)

        y = _swiglu_chunk(h_scr[...], wg_ref.at[0], wu_ref.at[0], wd_ref.at[0])

        @pl.when(j == 0)
        def _():
            o_ref[...] = y

        @pl.when(j > 0)
        def _():
            o_ref[...] += y


def _grouped_ffn(xs, g, tile_expert, n_used, wg, wu, wd, tm, tf):
    r, d = xs.shape
    f = wg.shape[2]
    nj = f // tf

    def row_map(i, j, te, nu):
        return (i, 0)

    def col_of(i, j, nu):
        return jnp.where(i < nu[0], j, nj - 1)

    grid_spec = pltpu.PrefetchScalarGridSpec(
        num_scalar_prefetch=2,
        grid=(r // tm, nj),
        in_specs=[pl.BlockSpec((tm, d), row_map),
                  pl.BlockSpec((1, d), lambda i, j, te, nu: (0, 0)),
                  pl.BlockSpec((1, d, tf), lambda i, j, te, nu: (te[i], 0, col_of(i, j, nu))),
                  pl.BlockSpec((1, d, tf), lambda i, j, te, nu: (te[i], 0, col_of(i, j, nu))),
                  pl.BlockSpec((1, tf, d), lambda i, j, te, nu: (te[i], col_of(i, j, nu), 0))],
        out_specs=pl.BlockSpec((tm, d), row_map),
        scratch_shapes=[pltpu.VMEM((tm, d), BF16)],
    )
    return pl.pallas_call(
        _grouped_ffn_kernel,
        grid_spec=grid_spec,
        out_shape=jax.ShapeDtypeStruct((r, d), F32),
        compiler_params=_cparams(("arbitrary", "arbitrary"), FFN_VMEM_LIMIT),
        name="moe_grouped_ffn",
    )(tile_expert, n_used, xs, g, wg, wu, wd)


def _combine_kernel(pos_ref, pos_next_ref, x_ref, route_ref, ys_hbm, o_ref, buf0, buf1, sems, *, tc):
    step = pl.program_id(0)
    slot = step % 2

    def gather(p_ref, s):
        def issue(j, c):
            _row_copy(ys_hbm, p_ref[0, 0, 2 * j], buf0.at[s], j, sems.at[s]).start()
            _row_copy(ys_hbm, p_ref[0, 0, 2 * j + 1], buf1.at[s], j, sems.at[s]).start()
            return c

        lax.fori_loop(0, tc, issue, 0, unroll=DMA_ISSUE_UNROLL)

    pl.when(step == 0)(functools.partial(gather, pos_ref, 0))
    pl.when(step + 1 < pl.num_programs(0))(functools.partial(gather, pos_next_ref, 1 - slot))
    for buf in (buf0, buf1):
        pltpu.make_async_copy(ys_hbm.at[pl.ds(0, tc)], buf.at[slot], sems.at[slot]).wait()
    route = route_ref[...]
    g0 = route[:, ROUTE_GATE_LANES[0]:ROUTE_GATE_LANES[0] + 1]
    g1 = route[:, ROUTE_GATE_LANES[1]:ROUTE_GATE_LANES[1] + 1]
    o_ref[...] = x_ref[...] + g0 * buf0[slot] + g1 * buf1[slot]


def _combine(x, route, pos, ys, tc):
    t, d = x.shape
    tc = min(tc, t)
    n = t // tc
    pos3 = pos.reshape(n, 1, 2 * tc)
    pos_block = (1, 1, 2 * tc)
    return pl.pallas_call(
        functools.partial(_combine_kernel, tc=tc),
        grid=(n,),
        in_specs=[pl.BlockSpec(pos_block, lambda i: (i, 0, 0), memory_space=pltpu.SMEM),
                  pl.BlockSpec(pos_block, lambda i: (jnp.minimum(i + 1, n - 1), 0, 0), memory_space=pltpu.SMEM),
                  pl.BlockSpec((tc, d), lambda i: (i, 0)),
                  pl.BlockSpec((tc, 128), lambda i: (i, 0)),
                  pl.BlockSpec(memory_space=pl.ANY)],
        out_specs=pl.BlockSpec((tc, d), lambda i: (i, 0)),
        out_shape=jax.ShapeDtypeStruct((t, d), F32),
        scratch_shapes=[pltpu.VMEM((2, tc, d), F32), pltpu.VMEM((2, tc, d), F32),
                        pltpu.SemaphoreType.DMA((2,))],
        compiler_params=_cparams(("arbitrary",)),
        name="moe_combine",
    )(pos3, pos3, x, route, ys)


MOE_TM = 512


def _route_plan(expert_ids, tm):
    flat = expert_ids.reshape(-1)
    a = flat.shape[0]
    onehot = (flat[None, :] == jnp.arange(N_EXPERTS, dtype=jnp.int32)[:, None]).astype(jnp.int32)
    csum = jnp.cumsum(onehot, axis=1)
    counts = csum[:, -1]
    padded = (counts + tm - 1) // tm * tm
    ends = jnp.cumsum(padded)
    offs = ends - padded
    pos = jnp.sum(onehot * (offs[:, None] + csum - 1), axis=0)
    n_tiles = (a + N_EXPERTS * tm) // tm
    tile_start = jnp.arange(n_tiles, dtype=jnp.int32) * tm
    tile_expert = jnp.minimum(jnp.sum(tile_start[:, None] >= ends[None, :], axis=1), N_EXPERTS - 1)
    n_used = (ends[-1] // tm).reshape(1)
    last = jnp.take(tile_expert, n_used[0] - 1)
    tile_expert = jnp.where(tile_start < ends[-1], tile_expert, last)
    meta = jnp.concatenate([offs + counts, ends, n_used]).astype(jnp.int32)
    return pos.astype(jnp.int32), tile_expert.astype(jnp.int32), n_used.astype(jnp.int32), meta, n_tiles * tm


def _moe(xs_list, routes, g, wg, wu, wd):
    ids = jnp.concatenate([r[:, ROUTE_ID_LANES[0]:ROUTE_ID_LANES[1] + 1] for r in routes]).astype(jnp.int32)
    pos, tile_expert, n_used, meta, n_rows = _route_plan(ids, MOE_TM)
    bounds = np.cumsum([0] + [2 * x.shape[0] for x in xs_list])
    pos_list = [pos[lo:hi] for lo, hi in zip(bounds[:-1], bounds[1:])]
    xs = _dispatch(xs_list, pos, meta, n_rows, MOE_TM, 1024)
    ys = _grouped_ffn(xs, g, tile_expert, n_used, wg, wu, wd, MOE_TM, D_FF)
    return [_combine(x, r, ps, ys, 512) for x, r, ps in zip(xs_list, routes, pos_list)]


def _pool_kernel(u_ref, halo_ref, w_ref, scale_ref, o_ref, *, tiles_per_seq, pos0, tb):
    ti = pl.program_id(0) % tiles_per_seq
    u = u_ref[...]
    halo = jnp.where(ti == 0, 0.0, halo_ref[...])
    ext = jnp.concatenate([halo, u], axis=0)
    s2 = ext + pltpu.roll(ext, 1, 0)
    s4 = s2 + pltpu.roll(s2, 2, 0)
    s8 = s4 + pltpu.roll(s4, 4, 0)
    s16 = s8 + pltpu.roll(s8, 8, 0)
    grp = lax.broadcasted_iota(jnp.int32, (tb, POOL_WIDTH), 1) // (POOL_WIDTH // 4)
    row = lax.broadcasted_iota(jnp.int32, (tb, POOL_WIDTH), 0)
    s = jnp.where(grp == 0, s2[POOL_HALO:],
                  jnp.where(grp == 1, s4[POOL_HALO:],
                            jnp.where(grp == 2, s8[POOL_HALO:], s16[POOL_HALO:])))
    win = jnp.where(grp == 0, 2, jnp.where(grp == 1, 4, jnp.where(grp == 2, 8, 16)))
    cnt = jnp.minimum(win, pos0 + ti * tb + row + 1).astype(F32)
    pooled = s / cnt - u
    o_ref[...] = (_dot(pooled.astype(BF16), w_ref[...]) * scale_ref[...]).astype(o_ref.dtype)


def _pool(proj2d, col, w, scale, *, n_rows, tb, tiles_per_seq, pos0):
    per = tb // POOL_HALO
    return pl.pallas_call(
        functools.partial(_pool_kernel, tiles_per_seq=tiles_per_seq, pos0=pos0, tb=tb),
        grid=(n_rows // tb,),
        in_specs=[pl.BlockSpec((tb, POOL_WIDTH), lambda i: (i, col)),
                  pl.BlockSpec((POOL_HALO, POOL_WIDTH), lambda i: (jnp.maximum(i * per - 1, 0), col)),
                  pl.BlockSpec((POOL_WIDTH, POOL_WIDTH), lambda i: (0, 0)),
                  pl.BlockSpec((1, POOL_WIDTH), lambda i: (0, 0))],
        out_specs=pl.BlockSpec((tb, POOL_WIDTH), lambda i: (i, 0)),
        out_shape=jax.ShapeDtypeStruct((n_rows, POOL_WIDTH), BF16),
        compiler_params=_cparams(("parallel",)),
        name="pool",
    )(proj2d, proj2d, w, scale)


def _t5_bucket_np(rel):
    n = np.maximum(rel, 0)
    max_exact = T5_BUCKETS // 2
    nf = np.maximum(n, max_exact).astype(np.float32)
    large = max_exact + (np.log(nf / max_exact) / math.log(T5_MAX_DIST / max_exact)
                         * (T5_BUCKETS - max_exact)).astype(np.int32)
    large = np.minimum(large, T5_BUCKETS - 1)
    return np.where(n < max_exact, n, large)


def _swa_bias(rel_bias, rel):
    valid = (rel >= 0) & (rel < SWA_WINDOW)
    onehot = jnp.asarray(_t5_bucket_np(rel)[..., None] == np.arange(T5_BUCKETS), F32)
    b = jnp.einsum('qsb,bh->hqs', onehot, rel_bias.astype(F32), precision=lax.Precision.HIGHEST)
    return jnp.where(valid[None], b, NEG)


def _softmax_parts(parts, sink):
    m = sink
    for s in parts:
        m = jnp.maximum(m, jnp.max(s, axis=-1, keepdims=True))
    ps = [jnp.exp(s - m) for s in parts]
    denom = jnp.exp(sink - m)
    for p in ps:
        denom = denom + jnp.sum(p, axis=-1, keepdims=True)
    return ps, denom


def _head_mean_matrix(width):
    h = np.arange(width) // HEAD_DIM
    return jnp.asarray((h[:, None] == h[None, :]) / HEAD_DIM, BF16)


def _head_rms(x, mean_mat, g):
    ms = _dot((x * x).astype(BF16), mean_mat)
    return x * lax.rsqrt(ms + RMS_EPS) * g


SWA_TILE = 512


def _swa_prompt_kernel(sinks_ref, q_ref, kv_ref, halo_ref, qg_ref, kg_ref, mq_ref, mk_ref, bias_ref,
                       y_ref, kn_ref):
    has_prev = pl.program_id(1) > 0
    kw = SWA_KV_HEADS * HEAD_DIM
    blk = SWA_BLOCK
    kv = kv_ref[0]
    halo = halo_ref[0]
    k_ext = jnp.concatenate([halo[:, :kw], kv[:, :kw]], axis=0)
    v_ext = jnp.concatenate([halo[:, kw:], kv[:, kw:]], axis=0).astype(BF16)
    kn = _head_rms(k_ext, mk_ref[...], kg_ref[...])
    kn_ref[0] = kn[SWA_TILE:]
    knb = kn.astype(BF16)
    qn = (_head_rms(q_ref[0], mq_ref[...], qg_ref[...]) * (HEAD_DIM ** -0.5)).astype(BF16)
    row = lax.broadcasted_iota(jnp.int32, (2 * blk, 1), 0)
    col = lax.broadcasted_iota(jnp.int32, (2 * blk, 2 * blk), 1)
    for c in range(SWA_TILE // blk):
        rows = slice(c * blk, (c + 1) * blk)
        keys = slice(c * blk, (c + 2) * blk)
        for kh in range(SWA_KV_HEADS):
            ksl = slice(kh * HEAD_DIM, (kh + 1) * HEAD_DIM)
            h0 = 2 * kh
            q2 = jnp.concatenate([qn[rows, h0 * HEAD_DIM:(h0 + 1) * HEAD_DIM],
                                  qn[rows, (h0 + 1) * HEAD_DIM:(h0 + 2) * HEAD_DIM]], axis=0)
            s = _dot_nt(q2, knb[keys, ksl]) + bias_ref[kh]
            if c == 0:
                s = jnp.where(has_prev | (col >= blk), s, NEG)
            sink = jnp.where(row < blk, sinks_ref[h0], sinks_ref[h0 + 1])
            (p,), denom = _softmax_parts((s,), sink)
            o = _dot(p.astype(BF16), v_ext[keys, ksl]) / denom
            y_ref[0, rows, h0 * HEAD_DIM:(h0 + 1) * HEAD_DIM] = o[:blk].astype(y_ref.dtype)
            y_ref[0, rows, (h0 + 1) * HEAD_DIM:(h0 + 2) * HEAD_DIM] = o[blk:].astype(y_ref.dtype)


def _swa_prompt(proj3d, sinks, qg, kg, bias):
    b, l, _ = proj3d.shape
    kw = SWA_KV_HEADS * HEAD_DIM
    per = SWA_TILE // SWA_BLOCK
    tile = (1, SWA_TILE, MIX_BLOCK)
    const2 = lambda bi, i: (0, 0)
    bias2 = bias.reshape(SWA_KV_HEADS, 2 * SWA_BLOCK, 2 * SWA_BLOCK)
    return pl.pallas_call(
        _swa_prompt_kernel,
        grid=(b, l // SWA_TILE),
        in_specs=[pl.BlockSpec(memory_space=pltpu.SMEM),
                  pl.BlockSpec(tile, lambda bi, i: (bi, i, COL_SWA_Q)),
                  pl.BlockSpec(tile, lambda bi, i: (bi, i, COL_SWA_KV)),
                  pl.BlockSpec((1, SWA_BLOCK, MIX_BLOCK),
                               lambda bi, i: (bi, jnp.maximum(i * per - 1, 0), COL_SWA_KV)),
                  pl.BlockSpec((1, MIX_BLOCK), const2),
                  pl.BlockSpec((1, kw), const2),
                  pl.BlockSpec((MIX_BLOCK, MIX_BLOCK), const2),
                  pl.BlockSpec((kw, kw), const2),
                  pl.BlockSpec((SWA_KV_HEADS, 2 * SWA_BLOCK, 2 * SWA_BLOCK), lambda bi, i: (0, 0, 0))],
        out_specs=[pl.BlockSpec(tile, lambda bi, i: (bi, i, 0)),
                   pl.BlockSpec((1, SWA_BLOCK, kw), lambda bi, i: (bi, 0, 0))],
        out_shape=[jax.ShapeDtypeStruct((b, l, MIX_BLOCK), BF16),
                   jax.ShapeDtypeStruct((b, SWA_BLOCK, kw), F32)],
        compiler_params=_cparams(("parallel", "arbitrary")),
        name="swa_prompt",
    )(sinks, proj3d, proj3d, proj3d, jnp.tile(qg, (1, SWA_HEADS)), jnp.tile(kg, (1, SWA_KV_HEADS)),
      _head_mean_matrix(MIX_BLOCK), _head_mean_matrix(kw), bias2)


def _swa_sample_kernel(sinks_ref, q_ref, kv_ref, ckt_ref, cvt_ref, qg_ref, kg_ref, mq_ref, mk_ref, bias_ref,
                       y_ref, nkt_ref, nvt_ref, *, n_new):
    _, s_blk, kw, wb = ckt_ref.shape
    keep = wb - n_new
    q2 = _head_rms(q_ref[...], mq_ref[...], qg_ref[...]) * (HEAD_DIM ** -0.5)
    kv = kv_ref[...]
    kn2 = _head_rms(kv[:, :kw], mk_ref[...], kg_ref[...])
    v2 = kv[:, kw:]
    q3 = q2.reshape(s_blk, n_new, MIX_BLOCK)
    kn3 = kn2.reshape(s_blk, n_new, kw)
    v3 = v2.reshape(s_blk, n_new, kw)
    ckt = ckt_ref[0]
    cvt = cvt_ref[0]
    bdot = functools.partial(jnp.einsum, preferred_element_type=F32)
    for h in range(SWA_HEADS):
        kh = h // (SWA_HEADS // SWA_KV_HEADS)
        ksl = slice(kh * HEAD_DIM, (kh + 1) * HEAD_DIM)
        hsl = slice(h * HEAD_DIM, (h + 1) * HEAD_DIM)
        qh = q3[:, :, hsl]
        s_c = bdot('sqd,sdw->sqw', qh, ckt[:, ksl, :]) + bias_ref[h, :, :wb]
        s_n = bdot('sqd,sjd->sqj', qh, kn3[:, :, ksl]) + bias_ref[h, :, wb:]
        (p_c, p_n), denom = _softmax_parts((s_c, s_n), sinks_ref[h])
        o = bdot('sqw,sdw->sqd', p_c, cvt[:, ksl, :]) + bdot('sqj,sjd->sqd', p_n, v3[:, :, ksl])
        y_ref[:, hsl] = (o / denom).reshape(s_blk * n_new, HEAD_DIM).astype(y_ref.dtype)
    lane = lax.broadcasted_iota(jnp.int32, (kw, wb), 1)
    for new2, old, out_ref in ((kn2, ckt, nkt_ref), (v2, cvt, nvt_ref)):
        new_t = new2.T
        shifted = pltpu.roll(old.reshape(s_blk * kw, wb), keep, 1).reshape(s_blk, kw, wb)
        for s in range(s_blk):
            cols = pltpu.roll(new_t, (keep - s * n_new) % wb, 1)
            out_ref[s] = jnp.where(lane >= keep, cols, shifted[s])


def _swa_sample(proj2d, cache_kt, cache_vt, layer, sinks, qg, kg, bias, n_new, s_blk):
    _, nseq, kw, wb = cache_kt.shape
    assert s_blk * n_new == wb
    rows = s_blk * n_new
    blk = (rows, MIX_BLOCK)
    cblk = (s_blk, kw, wb)
    cin = pl.BlockSpec((1,) + cblk, lambda i: (layer, i, 0, 0))
    const2 = lambda i: (0, 0)
    return pl.pallas_call(
        functools.partial(_swa_sample_kernel, n_new=n_new),
        grid=(nseq // s_blk,),
        in_specs=[pl.BlockSpec(memory_space=pltpu.SMEM),
                  pl.BlockSpec(blk, lambda i: (i, COL_SWA_Q)),
                  pl.BlockSpec(blk, lambda i: (i, COL_SWA_KV)),
                  cin, cin,
                  pl.BlockSpec((1, MIX_BLOCK), const2),
                  pl.BlockSpec((1, kw), const2),
                  pl.BlockSpec((MIX_BLOCK, MIX_BLOCK), const2),
                  pl.BlockSpec((kw, kw), const2),
                  pl.BlockSpec((SWA_HEADS, n_new, wb + n_new), lambda i: (0, 0, 0))],
        out_specs=[pl.BlockSpec(blk, lambda i: (i, 0)),
                   pl.BlockSpec(cblk, lambda i: (i, 0, 0)),
                   pl.BlockSpec(cblk, lambda i: (i, 0, 0))],
        out_shape=[jax.ShapeDtypeStruct((nseq * n_new, MIX_BLOCK), BF16),
                   jax.ShapeDtypeStruct((nseq, kw, wb), F32),
                   jax.ShapeDtypeStruct((nseq, kw, wb), F32)],
        compiler_params=_cparams(("parallel",)),
        name="swa_sample",
    )(sinks, proj2d, proj2d, cache_kt, cache_vt, jnp.tile(qg, (1, SWA_HEADS)), jnp.tile(kg, (1, SWA_KV_HEADS)),
      _head_mean_matrix(MIX_BLOCK), _head_mean_matrix(kw), bias)


def _ssm_kernel(u_ref, h0_ref, wb_ref, tab_ref, wc_ref, d_ref, wglu_ref, y_ref, hn_ref,
                bu_scr, carry_scr, *, chained, tiles_per_seq, tb):
    n = SSM_N

    def project_in(rows):
        bu_scr[rows, :] = _dot(u_ref[rows, :].astype(BF16), wb_ref[...])

    def project_out(rows):
        y = _dot(bu_scr[rows, :].astype(BF16), wc_ref[...]) + d_ref[...] * u_ref[rows, :]
        y = 0.5 * y * (1.0 + jnp.tanh(math.sqrt(2.0 / math.pi) * (y + 0.044715 * (y * y * y))))
        return (y * _sigmoid(_dot(y.astype(BF16), wglu_ref[...]))).astype(y_ref.dtype)

    def tile_scan(r0, cr, ci):
        hr = bu_scr[pl.ds(r0, SCAN_ROWS), :n]
        hi = bu_scr[pl.ds(r0, SCAN_ROWS), n:]
        for k, shift in enumerate((1, 2, 4)):
            ar, ai = tab_ref[2 * k], tab_ref[2 * k + 1]
            sr, si = pltpu.roll(hr, shift, 0), pltpu.roll(hi, shift, 0)
            hr, hi = hr + ar * sr - ai * si, hi + ar * si + ai * sr
        pr, pi = tab_ref[6], tab_ref[7]
        hr, hi = hr + pr * cr - pi * ci, hi + pr * ci + pi * cr
        bu_scr[pl.ds(r0, SCAN_ROWS), :n] = hr
        bu_scr[pl.ds(r0, SCAN_ROWS), n:] = hi
        return hr[SCAN_ROWS - 1:], hi[SCAN_ROWS - 1:]

    if chained:
        @pl.when(pl.program_id(1) % tiles_per_seq == 0)
        def _():
            carry_scr[...] = jnp.zeros_like(carry_scr)

        project_in(slice(None))
        cr, ci = carry_scr[:, :n], carry_scr[:, n:]
        for r0 in range(0, tb, SCAN_ROWS):
            lr, li = tile_scan(r0, cr, ci)
            cr, ci = jnp.broadcast_to(lr, (SCAN_ROWS, n)), jnp.broadcast_to(li, (SCAN_ROWS, n))
        y_ref[0] = project_out(slice(None))
        carry_scr[:, :n] = cr
        carry_scr[:, n:] = ci
        hn_ref[0, :, :n] = cr
        hn_ref[0, :, n:] = ci
    else:
        project_in(slice(None))

        def body(t, _):
            r0 = pl.multiple_of(t * SCAN_ROWS, SCAN_ROWS)
            h0 = h0_ref[pl.ds(t, 1), :]
            cr = jnp.broadcast_to(h0[:, :n], (SCAN_ROWS, n))
            ci = jnp.broadcast_to(h0[:, n:], (SCAN_ROWS, n))
            lr, li = tile_scan(r0, cr, ci)
            hn_ref[pl.ds(t, 1), :n] = lr
            hn_ref[pl.ds(t, 1), n:] = li
            return 0

        lax.fori_loop(0, tb // SCAN_ROWS, body, 0)
        y_ref[...] = project_out(slice(None))


def _ssm_common_specs(zero_map2, zero_map3):
    return [pl.BlockSpec((SSM_WIDTH, 2 * SSM_N), zero_map2),
            pl.BlockSpec((8, SCAN_ROWS, SSM_N), zero_map3),
            pl.BlockSpec((2 * SSM_N, SSM_WIDTH), zero_map2),
            pl.BlockSpec((1, SSM_WIDTH), zero_map2),
            pl.BlockSpec((SSM_WIDTH, SSM_WIDTH), zero_map2)]


def _ssm_prompt(proj3d, sp, tb):
    b, l, _ = proj3d.shape
    nt = l // tb
    dummy_h0 = jnp.zeros((SCAN_ROWS, 2 * SSM_N), F32)
    kern = functools.partial(_ssm_kernel, chained=True, tiles_per_seq=nt, tb=tb)

    def kernel(u_ref, h0_ref, wb, tab, wc, d, wglu, y_ref, hn_ref, bu_scr, carry_scr):
        kern(u_ref.at[0], h0_ref, wb, tab, wc, d, wglu, y_ref, hn_ref, bu_scr, carry_scr)

    return pl.pallas_call(
        kernel,
        grid=(b, nt),
        in_specs=[pl.BlockSpec((1, tb, SSM_WIDTH), lambda bi, i: (bi, i, COL_SSM)),
                  pl.BlockSpec((SCAN_ROWS, 2 * SSM_N), lambda bi, i: (0, 0))]
                 + _ssm_common_specs(lambda bi, i: (0, 0), lambda bi, i: (0, 0, 0)),
        out_specs=[pl.BlockSpec((1, tb, SSM_WIDTH), lambda bi, i: (bi, i, 0)),
                   pl.BlockSpec((1, SCAN_ROWS, 2 * SSM_N), lambda bi, i: (bi, 0, 0))],
        out_shape=[jax.ShapeDtypeStruct((b, l, SSM_WIDTH), BF16),
                   jax.ShapeDtypeStruct((b, SCAN_ROWS, 2 * SSM_N), F32)],
        scratch_shapes=[pltpu.VMEM((tb, 2 * SSM_N), F32), pltpu.VMEM((SCAN_ROWS, 2 * SSM_N), F32)],
        compiler_params=_cparams(("parallel", "arbitrary")),
        name="ssm_prompt",
    )(proj3d, dummy_h0, sp["wb"], sp["tab"], sp["wc"], sp["d"], sp["wglu"])


def _ssm_sample(proj2d, h0, sp):
    rows = proj2d.shape[0]
    nseq = h0.shape[0]
    kern = functools.partial(_ssm_kernel, chained=False, tiles_per_seq=1, tb=rows)
    return pl.pallas_call(
        kern,
        grid=(1,),
        in_specs=[pl.BlockSpec((rows, SSM_WIDTH), lambda i: (0, COL_SSM)),
                  pl.BlockSpec((nseq, 2 * SSM_N), lambda i: (0, 0))]
                 + _ssm_common_specs(lambda i: (0, 0), lambda i: (0, 0, 0)),
        out_specs=[pl.BlockSpec((rows, SSM_WIDTH), lambda i: (0, 0)),
                   pl.BlockSpec((nseq, 2 * SSM_N), lambda i: (0, 0))],
        out_shape=[jax.ShapeDtypeStruct((rows, SSM_WIDTH), BF16),
                   jax.ShapeDtypeStruct((nseq, 2 * SSM_N), F32)],
        scratch_shapes=[pltpu.VMEM((rows, 2 * SSM_N), F32), pltpu.VMEM((SCAN_ROWS, 2 * SSM_N), F32)],
        compiler_params=_cparams(("arbitrary",)),
        name="ssm_sample",
    )(proj2d, h0, sp["wb"], sp["tab"], sp["wc"], sp["d"], sp["wglu"])


def _ssm_params(lam_re, lam_im, log_dt, b_re, b_im, c_re, c_im, d_skip, w_glu):
    lr, li = lam_re.astype(F32), lam_im.astype(F32)
    dt = jnp.exp(log_dt.astype(F32))[:, None]
    mag = jnp.exp(lr * dt)
    ab_re, ab_im = mag * jnp.cos(li * dt), mag * jnp.sin(li * dt)
    den = lr * lr + li * li
    nr = ab_re - 1.0
    f_re = (nr * lr + ab_im * li) / den
    f_im = (ab_im * lr - nr * li) / den
    br, bi = b_re.astype(F32), b_im.astype(F32)
    bb_re = f_re[..., None] * br - f_im[..., None] * bi
    bb_im = f_re[..., None] * bi + f_im[..., None] * br
    eye = jnp.eye(SSM_GROUPS, dtype=F32)

    def in_mat(bb):
        return jnp.einsum('gpc,gh->gchp', bb, eye).reshape(SSM_WIDTH, SSM_N)

    def out_mat(c):
        return jnp.einsum('gcp,gh->gphc', c.astype(F32), eye).reshape(SSM_N, SSM_WIDTH)

    wb = jnp.concatenate([in_mat(bb_re), in_mat(bb_im)], axis=1).astype(BF16)
    wc = jnp.concatenate([out_mat(c_re), -out_mat(c_im)], axis=0).astype(BF16)

    ar, ai = ab_re.reshape(1, SSM_N), ab_im.reshape(1, SSM_N)

    def cmul(x, y):
        return (x[0] * y[0] - x[1] * y[1], x[0] * y[1] + x[1] * y[0])

    pw = [(ar, ai)]
    for _ in range(SCAN_ROWS - 1):
        pw.append(cmul(pw[-1], (ar, ai)))
    row = jnp.arange(SCAN_ROWS)[:, None]
    tabs = []
    for shift in (1, 2, 4):
        for part in pw[shift - 1]:
            tabs.append(jnp.where(row >= shift, part, 0.0))
    tabs.append(jnp.concatenate([p[0] for p in pw], axis=0))
    tabs.append(jnp.concatenate([p[1] for p in pw], axis=0))
    tab = jnp.stack([jnp.broadcast_to(t, (SCAN_ROWS, SSM_N)) for t in tabs])
    return dict(wb=wb, tab=tab, wc=wc, d=d_skip.astype(F32).reshape(1, SSM_WIDTH), wglu=w_glu.astype(BF16))


_RET_G = 1.0 - np.exp2(-5.0 - np.arange(RET_HEADS, dtype=np.float64))


def _ret_consts(chunk, n_rows):
    idx = np.arange(n_rows)
    loc = idx % chunk
    same = (idx[:, None] // chunk) == (idx[None, :] // chunk)
    diff = loc[:, None] - loc[None, :]
    dec = np.where(same & (diff >= 0), _RET_G[:, None, None] ** np.maximum(diff, 0)[None], 0.0)
    qdec = np.repeat((_RET_G[None, :] ** (loc[:, None] + 1.0)), HEAD_DIM, axis=1)
    kdec = np.repeat((_RET_G[None, :] ** (chunk - 1.0 - loc[:, None])), HEAD_DIM, axis=1)
    return (jnp.asarray(dec, F32), jnp.asarray(qdec, F32), jnp.asarray(kdec, F32),
            jnp.asarray(_RET_G ** chunk, F32))


def _rope_tables(pos):
    half = HEAD_DIM // 2
    theta = 1.0 / (ROPE_BASE ** np.linspace(0.0, 1.0, half))
    ang = np.asarray(pos, np.float64)[:, None] * theta[None, :]
    cos = np.repeat(np.cos(ang), 2, axis=1)
    sin = np.repeat(np.sin(ang), 2, axis=1) * np.tile([-1.0, 1.0], half)[None]
    return (jnp.asarray(np.tile(cos, (1, RET_HEADS)), F32), jnp.asarray(np.tile(sin, (1, RET_HEADS)), F32))


def _rotate_pairs(x, cos, sin_signed):
    lane = lax.broadcasted_iota(jnp.int32, x.shape, 1)
    nxt = pltpu.roll(x, x.shape[1] - 1, 1)
    prv = pltpu.roll(x, 1, 1)
    return x * cos + jnp.where(lane % 2 == 0, nxt, prv) * sin_signed


def _ret_head_out(o, gate, norm):
    ms = jnp.mean(o * o, axis=-1, keepdims=True)
    return o * lax.rsqrt(ms + RMS_EPS) * norm * (gate * _sigmoid(gate))


RET_TILE = 512


def _ret_prompt_kernel(gc_ref, q_ref, k_ref, v_ref, g_ref, cos_ref, sin_ref, dec_ref, qdec_ref,
                       kdec_ref, norm_ref, mh_ref, y_ref, r_ref, o_scr):
    @pl.when(pl.program_id(1) == 0)
    def _():
        r_ref[...] = jnp.zeros_like(r_ref)

    cos, sin = cos_ref[...], sin_ref[...]
    q = _rotate_pairs(q_ref[0], cos, sin)
    k = _rotate_pairs(k_ref[0], cos, sin) * (HEAD_DIM ** -0.5)
    qb, kb, vb = q.astype(BF16), k.astype(BF16), v_ref[0].astype(BF16)
    kdb = (k * kdec_ref[...]).astype(BF16)
    for c in range(RET_TILE // RET_CHUNK):
        rows = slice(c * RET_CHUNK, (c + 1) * RET_CHUNK)
        for h in range(RET_HEADS):
            sl = slice(h * HEAD_DIM, (h + 1) * HEAD_DIM)
            qh, vh = qb[rows, sl], vb[rows, sl]
            s = _dot_nt(qh, kb[rows, sl]) * dec_ref[h]
            r = r_ref[0, h]
            o_scr[rows, sl] = _dot(s.astype(BF16), vh) + _dot(qh, r.astype(BF16)) * qdec_ref[rows, sl]
            r_ref[0, h] = gc_ref[h] * r + _dot_tn(kdb[rows, sl], vh)
    g = g_ref[0]
    y_ref[0] = (_head_rms(o_scr[...], mh_ref[...], norm_ref[...]) * (g * _sigmoid(g))).astype(y_ref.dtype)


def _ret_prompt(proj3d, cos, sin, norm):
    b, l, _ = proj3d.shape
    c = RET_CHUNK
    dec, _, _, gc = _ret_consts(c, c)
    _, qdec, kdec, _ = _ret_consts(c, RET_TILE)
    blk = (1, RET_TILE, MIX_BLOCK)
    tspec = pl.BlockSpec((RET_TILE, MIX_BLOCK), lambda bi, i: (i, 0))
    cspec = pl.BlockSpec((RET_TILE, MIX_BLOCK), lambda bi, i: (0, 0))
    return pl.pallas_call(
        _ret_prompt_kernel,
        grid=(b, l // RET_TILE),
        in_specs=[pl.BlockSpec(memory_space=pltpu.SMEM),
                  pl.BlockSpec(blk, lambda bi, i: (bi, i, COL_RET_Q)),
                  pl.BlockSpec(blk, lambda bi, i: (bi, i, COL_RET_K)),
                  pl.BlockSpec(blk, lambda bi, i: (bi, i, COL_RET_V)),
                  pl.BlockSpec(blk, lambda bi, i: (bi, i, COL_RET_G)),
                  tspec, tspec,
                  pl.BlockSpec((RET_HEADS, c, c), lambda bi, i: (0, 0, 0)),
                  cspec, cspec,
                  pl.BlockSpec((1, MIX_BLOCK), lambda bi, i: (0, 0)),
                  pl.BlockSpec((MIX_BLOCK, MIX_BLOCK), lambda bi, i: (0, 0))],
        out_specs=[pl.BlockSpec(blk, lambda bi, i: (bi, i, 0)),
                   pl.BlockSpec((1, RET_HEADS, HEAD_DIM, HEAD_DIM), lambda bi, i: (bi, 0, 0, 0))],
        out_shape=[jax.ShapeDtypeStruct((b, l, MIX_BLOCK), BF16),
                   jax.ShapeDtypeStruct((b, RET_HEADS, HEAD_DIM, HEAD_DIM), F32)],
        scratch_shapes=[pltpu.VMEM((RET_TILE, MIX_BLOCK), F32)],
        compiler_params=_cparams(("parallel", "arbitrary")),
        name="ret_prompt",
    )(gc, proj3d, proj3d, proj3d, proj3d, cos, sin, dec, qdec, kdec, norm, _head_mean_matrix(MIX_BLOCK))


def _ret_sample_kernel(gc_ref, q_ref, k_ref, v_ref, g_ref, cos_ref, sin_ref, dec_ref, qdec_ref,
                       kdec_ref, norm_ref, r0_ref, y_ref, rn_ref, *, n_new, s_blk):
    cos, sin = cos_ref[...], sin_ref[...]
    q = _rotate_pairs(q_ref[...], cos, sin)
    k = _rotate_pairs(k_ref[...], cos, sin) * (HEAD_DIM ** -0.5)
    v = v_ref[...]
    g = g_ref[...]
    kd = k * kdec_ref[...]
    qdec = qdec_ref[...]
    norm = norm_ref[...]
    rows = s_blk * n_new
    seq = lax.broadcasted_iota(jnp.int32, (rows, HEAD_DIM), 0) // n_new
    for h in range(RET_HEADS):
        sl = slice(h * HEAD_DIM, (h + 1) * HEAD_DIM)
        qf, kdf = q[:, sl], kd[:, sl]
        qh, kh, vh = qf.astype(BF16), k[:, sl].astype(BF16), v[:, sl].astype(BF16)
        s = _dot_nt(qh, kh) * dec_ref[h]
        cross = jnp.zeros((rows, HEAD_DIM), F32)
        for si in range(s_blk):
            mine = seq == si
            r = r0_ref[si, h]
            cross = cross + _dot(jnp.where(mine, qf, 0.0).astype(BF16), r.astype(BF16))
            rn_ref[si, h] = gc_ref[h] * r + _dot_tn(jnp.where(mine, kdf, 0.0).astype(BF16), vh)
        o = _dot(s.astype(BF16), vh) + cross * qdec[:, sl]
        y_ref[:, sl] = _ret_head_out(o, g[:, sl], norm[:, sl]).astype(y_ref.dtype)


def _ret_sample(proj2d, r0, cos, sin, norm, n_new, s_blk):
    rows = s_blk * n_new
    nseq = r0.shape[0]
    dec, qdec, kdec, gc = _ret_consts(n_new, rows)
    blk = (rows, MIX_BLOCK)
    cspec = pl.BlockSpec(blk, lambda i: (0, 0))
    rblk = (s_blk, RET_HEADS, HEAD_DIM, HEAD_DIM)
    return pl.pallas_call(
        functools.partial(_ret_sample_kernel, n_new=n_new, s_blk=s_blk),
        grid=(nseq // s_blk,),
        in_specs=[pl.BlockSpec(memory_space=pltpu.SMEM),
                  pl.BlockSpec(blk, lambda i: (i, COL_RET_Q)),
                  pl.BlockSpec(blk, lambda i: (i, COL_RET_K)),
                  pl.BlockSpec(blk, lambda i: (i, COL_RET_V)),
                  pl.BlockSpec(blk, lambda i: (i, COL_RET_G)),
                  cspec, cspec,
                  pl.BlockSpec((RET_HEADS, rows, rows), lambda i: (0, 0, 0)),
                  cspec, cspec,
                  pl.BlockSpec((1, MIX_BLOCK), lambda i: (0, 0)),
                  pl.BlockSpec(rblk, lambda i: (i, 0, 0, 0))],
        out_specs=[pl.BlockSpec(blk, lambda i: (i, 0)),
                   pl.BlockSpec(rblk, lambda i: (i, 0, 0, 0))],
        out_shape=[jax.ShapeDtypeStruct((nseq * n_new, MIX_BLOCK), BF16),
                   jax.ShapeDtypeStruct((nseq, RET_HEADS, HEAD_DIM, HEAD_DIM), F32)],
        compiler_params=_cparams(("parallel",)),
        name="ret_sample",
    )(gc, proj2d, proj2d, proj2d, proj2d, cos, sin, dec, qdec, kdec, norm, r0)


def _block_diag(w):
    g, n, _ = w.shape
    return jnp.einsum('gcd,gh->gchd', w, jnp.eye(g, dtype=w.dtype)).reshape(g * n, g * n)


def _layer_params(l, p):
    return dict(
        norm_mix=p['norm_mix'][l].reshape(1, D_MODEL),
        norm_ffn=p['norm_ffn'][l].reshape(1, D_MODEL),
        w_in=p['w_in'][l].astype(BF16),
        w_out=p['w_out'][l].astype(BF16),
        pool_w=_block_diag(p['pool_w'][l].astype(F32)).astype(BF16),
        pool_scale=p['pool_scale'][l].astype(F32).reshape(1, POOL_WIDTH),
        qg=p['swa_q_norm'][l].astype(F32).reshape(1, HEAD_DIM),
        kg=p['swa_k_norm'][l].astype(F32).reshape(1, HEAD_DIM),
        sinks=p['swa_sinks'][l].astype(F32),
        ssm=_ssm_params(p['ssm_lambda_re'][l], p['ssm_lambda_im'][l], p['ssm_log_dt'][l],
                        p['ssm_b_re'][l], p['ssm_b_im'][l], p['ssm_c_re'][l], p['ssm_c_im'][l],
                        p['ssm_d'][l], p['ssm_w_glu'][l]),
        ret_norm=p['ret_norm'][l].astype(F32).reshape(1, MIX_BLOCK),
    )


def _channel_mix(streams, l, lp, p):
    i = l // 2
    g, w_out = lp['norm_ffn'], lp['w_out']
    if l % 2 == 0:
        wg, wu, wd = (p[k][i].astype(BF16) for k in ('ffn_w_gate', 'ffn_w_up', 'ffn_w_down'))
        return [_out_proj_ffn(x, ys, w_out, g, wg, wu, wd, 512, D_FF) for x, ys in streams]
    wr3 = _router_weights(p['moe_router'][i])
    wg, wu, wd = (p[k][i].astype(BF16) for k in ('moe_w_gate', 'moe_w_up', 'moe_w_down'))
    x1s, routes = zip(*[_out_proj_router(x, ys, w_out, g, wr3, 512) for x, ys in streams])
    return _moe(x1s, routes, g, wg, wu, wd)


def _mix_prompt(x2, b, l, lp, bias, cos, sin):
    proj2 = _norm_matmul(x2, lp['norm_mix'], lp['w_in'], 512)
    proj3 = proj2.reshape(b, l, IN_WIDTH)
    tb = 512
    y_pool = _pool(proj2, COL_POOL, lp['pool_w'], lp['pool_scale'], n_rows=b * l, tb=tb,
                   tiles_per_seq=l // tb, pos0=0)
    y_swa, kn = _swa_prompt(proj3, lp['sinks'], lp['qg'], lp['kg'], bias)
    y_ssm, hn = _ssm_prompt(proj3, lp['ssm'], 512)
    y_ret, rn = _ret_prompt(proj3, cos, sin, lp['ret_norm'])
    ys = (y_pool, y_swa.reshape(b * l, MIX_BLOCK), y_ssm.reshape(b * l, MIX_BLOCK),
          y_ret.reshape(b * l, MIX_BLOCK))
    kw = SWA_KV_HEADS * HEAD_DIM
    hn = hn[:, 0]
    states = (proj3[:, l - POOL_BUF:, :POOL_WIDTH],
              kn.reshape(b, SWA_WINDOW, SWA_KV_HEADS, HEAD_DIM),
              proj3[:, l - SWA_WINDOW:, COL_SWA_KV * MIX_BLOCK + kw:(COL_SWA_KV + 1) * MIX_BLOCK]
              .reshape(b, SWA_WINDOW, SWA_KV_HEADS, HEAD_DIM),
              jnp.stack([hn[:, :SSM_N], hn[:, SSM_N:]], axis=-1).reshape(b, SSM_GROUPS, SSM_STATE, 2),
              rn)
    return ys, states


SAMPLE_SEQ_BLOCK = 16


def _cache_transposed(cache):
    depth, nseq, wb = cache.shape[:3]
    return jnp.swapaxes(cache.astype(F32).reshape(depth, nseq, wb, SWA_KV_HEADS * HEAD_DIM), 2, 3)


def _mix_sample(x2, nseq, n_new, start_pos, lp, layer, st, bias, cos, sin):
    state_pool, cache_kt, cache_vt, state_ssm, state_ret = st
    rows = nseq * n_new
    kw = SWA_KV_HEADS * HEAD_DIM
    ext_rows = POOL_HALO + n_new
    proj2 = _norm_matmul(x2, lp['norm_mix'], lp['w_in'], 512)
    proj3 = proj2.reshape(nseq, n_new, IN_WIDTH)
    u_pool = proj3[:, :, :POOL_WIDTH]
    buf = state_pool.astype(F32)
    ext = jnp.concatenate([jnp.zeros((nseq, POOL_HALO - POOL_BUF, POOL_WIDTH), F32), buf, u_pool], axis=1)
    y_pool = _pool(ext.reshape(nseq * ext_rows, POOL_WIDTH), 0, lp['pool_w'], lp['pool_scale'],
                   n_rows=nseq * ext_rows, tb=nseq * ext_rows, tiles_per_seq=1, pos0=start_pos)
    y_pool = y_pool.reshape(nseq, ext_rows, POOL_WIDTH)[:, POOL_HALO:].reshape(rows, POOL_WIDTH)
    y_swa, nkt, nvt = _swa_sample(proj2, cache_kt, cache_vt, layer, lp['sinks'], lp['qg'], lp['kg'], bias,
                                  n_new, SAMPLE_SEQ_BLOCK)
    h0 = state_ssm.astype(F32).reshape(nseq, SSM_N, 2)
    h0 = jnp.concatenate([h0[..., 0], h0[..., 1]], axis=1)
    y_ssm, hn = _ssm_sample(proj2, h0, lp['ssm'])
    y_ret, rn = _ret_sample(proj2, state_ret.astype(F32), cos, sin, lp['ret_norm'], n_new, SAMPLE_SEQ_BLOCK)
    ys = (y_pool, y_swa, y_ssm, y_ret)
    states = (jnp.concatenate([buf, u_pool], axis=1)[:, -POOL_BUF:],
              jnp.swapaxes(nkt, 1, 2).reshape(nseq, SWA_WINDOW, SWA_KV_HEADS, HEAD_DIM),
              jnp.swapaxes(nvt, 1, 2).reshape(nseq, SWA_WINDOW, SWA_KV_HEADS, HEAD_DIM),
              jnp.stack([hn[:, :SSM_N], hn[:, SSM_N:]], axis=-1).reshape(nseq, SSM_GROUPS, SSM_STATE, 2),
              rn)
    return ys, states


def _forward(x_prompt, x_sample, past_len, sample_state, p, rel_bias):
    b, l, d = x_prompt.shape
    nseq, n_new, _ = x_sample.shape
    wb = sample_state[1].shape[2]
    depth = p['norm_mix'].shape[0]
    bias_p = _swa_bias(rel_bias, np.arange(SWA_BLOCK)[:, None] - np.arange(2 * SWA_BLOCK)[None, :] + SWA_BLOCK)
    bias_s = _swa_bias(rel_bias, np.arange(n_new)[:, None] - np.arange(wb + n_new)[None, :] + wb)
    rope_p = _rope_tables(np.arange(l))
    rope_s = _rope_tables(past_len + (np.arange(SAMPLE_SEQ_BLOCK * n_new) % n_new))
    xp = x_prompt.reshape(b * l, d)
    xs = x_sample.reshape(nseq * n_new, d)
    cache_kt, cache_vt = _cache_transposed(sample_state[1]), _cache_transposed(sample_state[2])
    st_p, st_s = [], []
    for li in range(depth):
        lp = _layer_params(li, p)
        yp, sp = _mix_prompt(xp, b, l, lp, bias_p, *rope_p)
        st = (sample_state[0][li], cache_kt, cache_vt, sample_state[3][li], sample_state[4][li])
        ys, ss = _mix_sample(xs, nseq, n_new, past_len, lp, li, st, bias_s, *rope_s)
        xp, xs = _channel_mix([(xp, yp), (xs, ys)], li, lp, p)
        st_p.append(sp)
        st_s.append(ss)
    outs = [xp.reshape(b, l, d), xs.reshape(nseq, n_new, d)]
    for k in range(5):
        outs.append(jnp.stack([s[k] for s in st_p]))
        outs.append(jnp.stack([s[k] for s in st_s]))
    return tuple(outs)


PAST_LEN = 16384


def kernel(x_prompt, x_sample, state_pool, cache_swa_k, cache_swa_v, state_ssm, state_ret,
           norm_mix, norm_ffn, w_in, w_out, pool_w, pool_scale, swa_q_norm, swa_k_norm, swa_sinks,
           rel_bias, ssm_lambda_re, ssm_lambda_im, ssm_log_dt, ssm_b_re, ssm_b_im, ssm_c_re, ssm_c_im,
           ssm_d, ssm_w_glu, ret_norm, ffn_w_gate, ffn_w_up, ffn_w_down, moe_router, moe_w_gate,
           moe_w_up, moe_w_down):
    p = dict(norm_mix=norm_mix, norm_ffn=norm_ffn, w_in=w_in, w_out=w_out, pool_w=pool_w,
             pool_scale=pool_scale, swa_q_norm=swa_q_norm, swa_k_norm=swa_k_norm, swa_sinks=swa_sinks,
             ssm_lambda_re=ssm_lambda_re, ssm_lambda_im=ssm_lambda_im, ssm_log_dt=ssm_log_dt,
             ssm_b_re=ssm_b_re, ssm_b_im=ssm_b_im, ssm_c_re=ssm_c_re, ssm_c_im=ssm_c_im,
             ssm_d=ssm_d, ssm_w_glu=ssm_w_glu, ret_norm=ret_norm,
             ffn_w_gate=ffn_w_gate, ffn_w_up=ffn_w_up, ffn_w_down=ffn_w_down, moe_router=moe_router,
             moe_w_gate=moe_w_gate, moe_w_up=moe_w_up, moe_w_down=moe_w_down)
    return _forward(x_prompt, x_sample, PAST_LEN,
                    (state_pool, cache_swa_k, cache_swa_v, state_ssm, state_ret), p, rel_bias)
```

```python
import functools
import math

import numpy as np
import jax
import jax.numpy as jnp
from jax import lax
from jax.experimental import pallas as pl
from jax.experimental.pallas import tpu as pltpu

F32 = jnp.float32
BF16 = jnp.bfloat16

D_MODEL = 1024
HEAD_DIM = 64
POOL_WIDTH = 256
POOL_WINDOWS = (2, 4, 8, 16)
POOL_BUF = 15
POOL_HALO = 16
SWA_HEADS = 4
SWA_KV_HEADS = 2
SWA_WINDOW = 128
SWA_BLOCK = 128
SSM_WIDTH = 256
SSM_CH = 16
SSM_GROUPS = 16
SSM_STATE = 64
SSM_N = SSM_GROUPS * SSM_STATE
RET_HEADS = 4
RET_CHUNK = 128
ROPE_BASE = 10000.0
IN_WIDTH = 2048
MIX_BLOCK = 256
D_FF = 2816
N_EXPERTS = 8
T5_BUCKETS = 32
T5_MAX_DIST = 128
RMS_EPS = 1e-6
NEG = -1e30
SUBLANES = 8
SCAN_ROWS = SUBLANES

COL_POOL, COL_SWA_Q, COL_SWA_KV, COL_SSM, COL_RET_Q, COL_RET_K, COL_RET_V, COL_RET_G = range(8)

VMEM_LIMIT = 48 * 1024 * 1024
FFN_VMEM_LIMIT = 58 * 1024 * 1024


def _cparams(sem, vmem=VMEM_LIMIT):
    return pltpu.CompilerParams(dimension_semantics=sem, vmem_limit_bytes=vmem)


def _rms(x, g):
    ms = jnp.mean(x * x, axis=-1, keepdims=True)
    return x * lax.rsqrt(ms + RMS_EPS) * g


def _dot(a, b):
    return jnp.dot(a, b, preferred_element_type=F32)


def _dot_nt(a, b):
    return lax.dot_general(a, b, (((1,), (1,)), ((), ())), preferred_element_type=F32)


def _dot_tn(a, b):
    return lax.dot_general(a, b, (((0,), (0,)), ((), ())), preferred_element_type=F32)


def _sigmoid(x):
    return 1.0 / (1.0 + jnp.exp(-x))


def _norm_matmul_kernel(x_ref, g_ref, w_ref, o_ref):
    h = _rms(x_ref[...], g_ref[...]).astype(BF16)
    o_ref[...] = _dot(h, w_ref[...])


def _norm_matmul(x, g, w, tm):
    t, d = x.shape
    tm = min(tm, t)
    n = w.shape[1]
    return pl.pallas_call(
        _norm_matmul_kernel,
        grid=(t // tm,),
        in_specs=[pl.BlockSpec((tm, d), lambda i: (i, 0)),
                  pl.BlockSpec((1, d), lambda i: (0, 0)),
                  pl.BlockSpec((d, n), lambda i: (0, 0))],
        out_specs=pl.BlockSpec((tm, n), lambda i: (i, 0)),
        out_shape=jax.ShapeDtypeStruct((t, n), F32),
        compiler_params=_cparams(("parallel",)),
        name="norm_matmul",
    )(x, g, w)


FFN_SUBCHUNK = 512


def _swiglu_chunk(h, wg_ref, wu_ref, wd_ref):
    tf = wg_ref.shape[1]
    y = None
    for lo in range(0, tf, FFN_SUBCHUNK):
        hi = min(lo + FFN_SUBCHUNK, tf)
        a = _dot(h, wg_ref[:, lo:hi])
        b = _dot(h, wu_ref[:, lo:hi])
        part = _dot((a * _sigmoid(a) * b).astype(BF16), wd_ref[lo:hi, :])
        y = part if y is None else y + part
    return y


def _mixed_residual(x_ref, y_refs, w_ref, rows=slice(None)):
    y = jnp.concatenate([y_ref[rows, :] for y_ref in y_refs], axis=1)
    return x_ref[rows, :] + _dot(y, w_ref[...])


def _mix_in_specs(tm, d, imap):
    yspec = pl.BlockSpec((tm, MIX_BLOCK), imap(lambda i: (i, 0)))
    return [pl.BlockSpec((tm, d), imap(lambda i: (i, 0))), yspec, yspec, yspec, yspec,
            pl.BlockSpec((d, d), imap(lambda i: (0, 0)))]


def _out_proj_ffn_kernel(x_ref, y0_ref, y1_ref, y2_ref, y3_ref, wo_ref, g_ref, wg_ref, wu_ref, wd_ref,
                         o_ref, h_scr):
    @pl.when(pl.program_id(1) == 0)
    def _():
        x1 = _mixed_residual(x_ref, (y0_ref, y1_ref, y2_ref, y3_ref), wo_ref)
        h_scr[...] = _rms(x1, g_ref[...]).astype(BF16)
        o_ref[...] = x1

    o_ref[...] += _swiglu_chunk(h_scr[...], wg_ref, wu_ref, wd_ref)


def _out_proj_ffn(x, ys, w_out, g, wg, wu, wd, tm, tf):
    t, d = x.shape
    tm = min(tm, t)
    f = wg.shape[1]
    imap = lambda fn: (lambda i, j: fn(i))
    once = dict(pipeline_mode=pl.Buffered(1)) if tf == f else {}
    return pl.pallas_call(
        _out_proj_ffn_kernel,
        grid=(t // tm, f // tf),
        in_specs=_mix_in_specs(tm, d, imap)
                 + [pl.BlockSpec((1, d), lambda i, j: (0, 0)),
                    pl.BlockSpec((d, tf), lambda i, j: (0, j), **once),
                    pl.BlockSpec((d, tf), lambda i, j: (0, j), **once),
                    pl.BlockSpec((tf, d), lambda i, j: (j, 0), **once)],
        out_specs=pl.BlockSpec((tm, d), lambda i, j: (i, 0)),
        out_shape=jax.ShapeDtypeStruct((t, d), F32),
        scratch_shapes=[pltpu.VMEM((tm, d), BF16)],
        compiler_params=_cparams(("parallel", "arbitrary"), FFN_VMEM_LIMIT),
        name="out_proj_ffn",
    )(x, *ys, w_out, g, wg, wu, wd)


ROUTE_ID_LANES = (0, 1)
ROUTE_GATE_LANES = (2, 3)


def _split_bf16(x):
    hi = x.astype(BF16)
    return hi, (x - hi.astype(F32)).astype(BF16)


ROUTER_ROWS = 256


def _out_proj_router_kernel(x_ref, y0_ref, y1_ref, y2_ref, y3_ref, wo_ref, g_ref, wr_ref, x1_ref, c_ref):
    tm = x_ref.shape[0]
    step = min(ROUTER_ROWS, tm)
    for r in range(0, tm, step):
        _route_rows(slice(r, r + step), x_ref, (y0_ref, y1_ref, y2_ref, y3_ref), wo_ref, g_ref, wr_ref,
                    x1_ref, c_ref)


def _route_rows(rows, x_ref, y_refs, wo_ref, g_ref, wr_ref, x1_ref, c_ref):
    x1 = _mixed_residual(x_ref, y_refs, wo_ref, rows)
    x1_ref[rows, :] = x1
    h_hi, h_lo = _split_bf16(_rms(x1, g_ref[...]))
    logits = _dot(jnp.concatenate([h_hi, h_lo, h_hi], axis=1), wr_ref[...])
    lane = lax.broadcasted_iota(jnp.int32, logits.shape, 1).astype(F32)
    lg = jnp.where(lane < N_EXPERTS, logits, NEG)
    m1 = jnp.max(lg, axis=-1, keepdims=True)
    i1 = jnp.min(jnp.where(lg == m1, lane, 128.0), axis=-1, keepdims=True)
    lg2 = jnp.where(lane == i1, NEG, lg)
    m2 = jnp.max(lg2, axis=-1, keepdims=True)
    i2 = jnp.min(jnp.where(lg2 == m2, lane, 128.0), axis=-1, keepdims=True)
    ex = jnp.exp(m2 - m1)
    vals = (i1, i2, 1.0 / (1.0 + ex), ex / (1.0 + ex))
    out = jnp.zeros_like(logits)
    for ln, v in zip(ROUTE_ID_LANES + ROUTE_GATE_LANES, vals):
        out = jnp.where(lane == ln, v, out)
    c_ref[rows, :] = out


def _router_weights(wr):
    w = jnp.pad(wr.astype(F32), ((0, 0), (0, 128 - N_EXPERTS)))
    hi, lo = _split_bf16(w)
    return jnp.concatenate([hi, hi, lo], axis=0)


def _out_proj_router(x, ys, w_out, g, wr3, tm):
    t, d = x.shape
    tm = min(tm, t)
    imap = lambda fn: fn
    return pl.pallas_call(
        _out_proj_router_kernel,
        grid=(t // tm,),
        in_specs=_mix_in_specs(tm, d, imap)
                 + [pl.BlockSpec((1, d), lambda i: (0, 0)),
                    pl.BlockSpec((3 * d, 128), lambda i: (0, 0))],
        out_specs=[pl.BlockSpec((tm, d), lambda i: (i, 0)),
                   pl.BlockSpec((tm, 128), lambda i: (i, 0))],
        out_shape=[jax.ShapeDtypeStruct((t, d), F32), jax.ShapeDtypeStruct((t, 128), F32)],
        compiler_params=_cparams(("parallel",)),
        name="out_proj_router",
    )(x, *ys, w_out, g, wr3)


DMA_ISSUE_UNROLL = 8


def _row_copy(src, i, dst, j, sem):
    return pltpu.make_async_copy(src.at[pl.ds(i, 1)], dst.at[pl.ds(j, 1)], sem)


def _dispatch_kernel(meta_ref, pos_ref, *rest, td, tm, n_tiles, first_step):
    n_streams = len(first_step) - 1
    x_refs, (xs_hbm, zero_scr, sem) = rest[:n_streams], rest[n_streams:]
    step = pl.program_id(0)

    def zero_row(r):
        return _row_copy(zero_scr, 0, xs_hbm, r, sem)

    @pl.when(step == 0)
    def _():
        zero_scr[...] = jnp.zeros_like(zero_scr)
        n_used = meta_ref[2 * N_EXPERTS]
        tile_fills = [(i >= n_used, pltpu.make_async_copy(zero_scr, xs_hbm.at[pl.ds(i * tm, tm)], sem))
                      for i in range(n_tiles)]
        for cond, copy in tile_fills:
            pl.when(cond)(copy.start)
        for e in range(N_EXPERTS):
            lax.fori_loop(meta_ref[e], meta_ref[N_EXPERTS + e], lambda r, c: (zero_row(r).start(), c)[1], 0)
        for e in range(N_EXPERTS):
            lax.fori_loop(meta_ref[e], meta_ref[N_EXPERTS + e], lambda r, c: (zero_row(r).wait(), c)[1], 0)
        for cond, copy in tile_fills:
            pl.when(cond)(copy.wait)

    def scatter(x_ref):
        def issue(j, c):
            for k in range(2):
                _row_copy(x_ref, j, xs_hbm, pos_ref[0, 0, 2 * j + k], sem).start()
            return c

        lax.fori_loop(0, td, issue, 0, unroll=DMA_ISSUE_UNROLL)
        for _ in range(2):
            pltpu.make_async_copy(x_ref, xs_hbm.at[pl.ds(0, td)], sem).wait()

    for s, x_ref in enumerate(x_refs):
        pl.when((step >= first_step[s]) & (step < first_step[s + 1]))(functools.partial(scatter, x_ref))


def _dispatch(xs_list, pos, meta, n_rows, tm, td):
    d = xs_list[0].shape[1]
    td = min([td] + [x.shape[0] for x in xs_list])
    first_step = [0]
    for x in xs_list:
        first_step.append(first_step[-1] + x.shape[0] // td)
    n_steps = first_step[-1]
    pos3 = pos.reshape(n_steps, 1, 2 * td)

    def tile_map(s):
        lo, hi = first_step[s], first_step[s + 1]
        return lambda i, m: (jnp.clip(i, lo, hi - 1) - lo, 0)

    in_specs = [pl.BlockSpec((1, 1, 2 * td), lambda i, m: (i, 0, 0), memory_space=pltpu.SMEM)]
    in_specs += [pl.BlockSpec((td, d), tile_map(s)) for s in range(len(xs_list))]
    return pl.pallas_call(
        functools.partial(_dispatch_kernel, td=td, tm=tm, n_tiles=n_rows // tm, first_step=tuple(first_step)),
        grid_spec=pltpu.PrefetchScalarGridSpec(
            num_scalar_prefetch=1, grid=(n_steps,), in_specs=in_specs,
            out_specs=pl.BlockSpec(memory_space=pl.ANY),
            scratch_shapes=[pltpu.VMEM((tm, d), F32), pltpu.SemaphoreType.DMA]),
        out_shape=jax.ShapeDtypeStruct((n_rows, d), F32),
        compiler_params=_cparams(("arbitrary",)),
        name="moe_dispatch",
    )(meta, pos3, *xs_list)


def _grouped_ffn_kernel(te_ref, nu_ref, x_ref, g_ref, wg_ref, wu_ref, wd_ref, o_ref, h_scr):
    del te_ref
    j = pl.program_id(1)
    used = pl.program_id(0) < nu_ref[0]

    @pl.when(jnp.logical_not(used) & (j == 0))
    def _():
        o_ref[...] = jnp.zeros_like(o_ref)

    @pl.when(used)
    def _():
        @pl.when(j == 0)
        def _():
            h_scr[...] = _rms(x_ref[...], g_ref[...]).astype(BF16)

        y = _swiglu_chunk(h_scr[...], wg_ref.at[0], wu_ref.at[0], wd_ref.at[0])

        @pl.when(j == 0)
        def _():
            o_ref[...] = y

        @pl.when(j > 0)
        def _():
            o_ref[...] += y


def _grouped_ffn(xs, g, tile_expert, n_used, wg, wu, wd, tm, tf):
    r, d = xs.shape
    f = wg.shape[2]
    nj = f // tf

    def row_map(i, j, te, nu):
        return (i, 0)

    def col_of(i, j, nu):
        return jnp.where(i < nu[0], j, nj - 1)

    grid_spec = pltpu.PrefetchScalarGridSpec(
        num_scalar_prefetch=2,
        grid=(r // tm, nj),
        in_specs=[pl.BlockSpec((tm, d), row_map),
                  pl.BlockSpec((1, d), lambda i, j, te, nu: (0, 0)),
                  pl.BlockSpec((1, d, tf), lambda i, j, te, nu: (te[i], 0, col_of(i, j, nu))),
                  pl.BlockSpec((1, d, tf), lambda i, j, te, nu: (te[i], 0, col_of(i, j, nu))),
                  pl.BlockSpec((1, tf, d), lambda i, j, te, nu: (te[i], col_of(i, j, nu), 0))],
        out_specs=pl.BlockSpec((tm, d), row_map),
        scratch_shapes=[pltpu.VMEM((tm, d), BF16)],
    )
    return pl.pallas_call(
        _grouped_ffn_kernel,
        grid_spec=grid_spec,
        out_shape=jax.ShapeDtypeStruct((r, d), F32),
        compiler_params=_cparams(("arbitrary", "arbitrary"), FFN_VMEM_LIMIT),
        name="moe_grouped_ffn",
    )(tile_expert, n_used, xs, g, wg, wu, wd)


def _combine_kernel(pos_ref, pos_next_ref, x_ref, route_ref, ys_hbm, o_ref, buf0, buf1, sems, *, tc):
    step = pl.program_id(0)
    slot = step % 2

    def gather(p_ref, s):
        def issue(j, c):
            _row_copy(ys_hbm, p_ref[0, 0, 2 * j], buf0.at[s], j, sems.at[s]).start()
            _row_copy(ys_hbm, p_ref[0, 0, 2 * j + 1], buf1.at[s], j, sems.at[s]).start()
            return c

        lax.fori_loop(0, tc, issue, 0, unroll=DMA_ISSUE_UNROLL)

    pl.when(step == 0)(functools.partial(gather, pos_ref, 0))
    pl.when(step + 1 < pl.num_programs(0))(functools.partial(gather, pos_next_ref, 1 - slot))
    for buf in (buf0, buf1):
        pltpu.make_async_copy(ys_hbm.at[pl.ds(0, tc)], buf.at[slot], sems.at[slot]).wait()
    route = route_ref[...]
    g0 = route[:, ROUTE_GATE_LANES[0]:ROUTE_GATE_LANES[0] + 1]
    g1 = route[:, ROUTE_GATE_LANES[1]:ROUTE_GATE_LANES[1] + 1]
    o_ref[...] = x_ref[...] + g0 * buf0[slot] + g1 * buf1[slot]


def _combine(x, route, pos, ys, tc):
    t, d = x.shape
    tc = min(tc, t)
    n = t // tc
    pos3 = pos.reshape(n, 1, 2 * tc)
    pos_block = (1, 1, 2 * tc)
    return pl.pallas_call(
        functools.partial(_combine_kernel, tc=tc),
        grid=(n,),
        in_specs=[pl.BlockSpec(pos_block, lambda i: (i, 0, 0), memory_space=pltpu.SMEM),
                  pl.BlockSpec(pos_block, lambda i: (jnp.minimum(i + 1, n - 1), 0, 0), memory_space=pltpu.SMEM),
                  pl.BlockSpec((tc, d), lambda i: (i, 0)),
                  pl.BlockSpec((tc, 128), lambda i: (i, 0)),
                  pl.BlockSpec(memory_space=pl.ANY)],
        out_specs=pl.BlockSpec((tc, d), lambda i: (i, 0)),
        out_shape=jax.ShapeDtypeStruct((t, d), F32),
        scratch_shapes=[pltpu.VMEM((2, tc, d), F32), pltpu.VMEM((2, tc, d), F32),
                        pltpu.SemaphoreType.DMA((2,))],
        compiler_params=_cparams(("arbitrary",)),
        name="moe_combine",
    )(pos3, pos3, x, route, ys)


MOE_TM = 512


def _route_plan(expert_ids, tm):
    flat = expert_ids.reshape(-1)
    a = flat.shape[0]
    onehot = (flat[None, :] == jnp.arange(N_EXPERTS, dtype=jnp.int32)[:, None]).astype(jnp.int32)
    csum = jnp.cumsum(onehot, axis=1)
    counts = csum[:, -1]
    padded = (counts + tm - 1) // tm * tm
    ends = jnp.cumsum(padded)
    offs = ends - padded
    pos = jnp.sum(onehot * (offs[:, None] + csum - 1), axis=0)
    n_tiles = (a + N_EXPERTS * tm) // tm
    tile_start = jnp.arange(n_tiles, dtype=jnp.int32) * tm
    tile_expert = jnp.minimum(jnp.sum(tile_start[:, None] >= ends[None, :], axis=1), N_EXPERTS - 1)
    n_used = (ends[-1] // tm).reshape(1)
    last = jnp.take(tile_expert, n_used[0] - 1)
    tile_expert = jnp.where(tile_start < ends[-1], tile_expert, last)
    meta = jnp.concatenate([offs + counts, ends, n_used]).astype(jnp.int32)
    return pos.astype(jnp.int32), tile_expert.astype(jnp.int32), n_used.astype(jnp.int32), meta, n_tiles * tm


def _moe(xs_list, routes, g, wg, wu, wd):
    ids = jnp.concatenate([r[:, ROUTE_ID_LANES[0]:ROUTE_ID_LANES[1] + 1] for r in routes]).astype(jnp.int32)
    pos, tile_expert, n_used, meta, n_rows = _route_plan(ids, MOE_TM)
    bounds = np.cumsum([0] + [2 * x.shape[0] for x in xs_list])
    pos_list = [pos[lo:hi] for lo, hi in zip(bounds[:-1], bounds[1:])]
    xs = _dispatch(xs_list, pos, meta, n_rows, MOE_TM, 1024)
    ys = _grouped_ffn(xs, g, tile_expert, n_used, wg, wu, wd, MOE_TM, D_FF)
    return [_combine(x, r, ps, ys, 512) for x, r, ps in zip(xs_list, routes, pos_list)]


def _pool_kernel(u_ref, halo_ref, w_ref, scale_ref, o_ref, *, tiles_per_seq, pos0, tb):
    ti = pl.program_id(0) % tiles_per_seq
    u = u_ref[...]
    halo = jnp.where(ti == 0, 0.0, halo_ref[...])
    ext = jnp.concatenate([halo, u], axis=0)
    s2 = ext + pltpu.roll(ext, 1, 0)
    s4 = s2 + pltpu.roll(s2, 2, 0)
    s8 = s4 + pltpu.roll(s4, 4, 0)
    s16 = s8 + pltpu.roll(s8, 8, 0)
    grp = lax.broadcasted_iota(jnp.int32, (tb, POOL_WIDTH), 1) // (POOL_WIDTH // 4)
    row = lax.broadcasted_iota(jnp.int32, (tb, POOL_WIDTH), 0)
    s = jnp.where(grp == 0, s2[POOL_HALO:],
                  jnp.where(grp == 1, s4[POOL_HALO:],
                            jnp.where(grp == 2, s8[POOL_HALO:], s16[POOL_HALO:])))
    win = jnp.where(grp == 0, 2, jnp.where(grp == 1, 4, jnp.where(grp == 2, 8, 16)))
    cnt = jnp.minimum(win, pos0 + ti * tb + row + 1).astype(F32)
    pooled = s / cnt - u
    o_ref[...] = (_dot(pooled.astype(BF16), w_ref[...]) * scale_ref[...]).astype(o_ref.dtype)


def _pool(proj2d, col, w, scale, *, n_rows, tb, tiles_per_seq, pos0):
    per = tb // POOL_HALO
    return pl.pallas_call(
        functools.partial(_pool_kernel, tiles_per_seq=tiles_per_seq, pos0=pos0, tb=tb),
        grid=(n_rows // tb,),
        in_specs=[pl.BlockSpec((tb, POOL_WIDTH), lambda i: (i, col)),
                  pl.BlockSpec((POOL_HALO, POOL_WIDTH), lambda i: (jnp.maximum(i * per - 1, 0), col)),
                  pl.BlockSpec((POOL_WIDTH, POOL_WIDTH), lambda i: (0, 0)),
                  pl.BlockSpec((1, POOL_WIDTH), lambda i: (0, 0))],
        out_specs=pl.BlockSpec((tb, POOL_WIDTH), lambda i: (i, 0)),
        out_shape=jax.ShapeDtypeStruct((n_rows, POOL_WIDTH), BF16),
        compiler_params=_cparams(("parallel",)),
        name="pool",
    )(proj2d, proj2d, w, scale)


def _t5_bucket_np(rel):
    n = np.maximum(rel, 0)
    max_exact = T5_BUCKETS // 2
    nf = np.maximum(n, max_exact).astype(np.float32)
    large = max_exact + (np.log(nf / max_exact) / math.log(T5_MAX_DIST / max_exact)
                         * (T5_BUCKETS - max_exact)).astype(np.int32)
    large = np.minimum(large, T5_BUCKETS - 1)
    return np.where(n < max_exact, n, large)


def _swa_bias(rel_bias, rel):
    valid = (rel >= 0) & (rel < SWA_WINDOW)
    onehot = jnp.asarray(_t5_bucket_np(rel)[..., None] == np.arange(T5_BUCKETS), F32)
    b = jnp.einsum('qsb,bh->hqs', onehot, rel_bias.astype(F32), precision=lax.Precision.HIGHEST)
    return jnp.where(valid[None], b, NEG)


def _softmax_parts(parts, sink):
    m = sink
    for s in parts:
        m = jnp.maximum(m, jnp.max(s, axis=-1, keepdims=True))
    ps = [jnp.exp(s - m) for s in parts]
    denom = jnp.exp(sink - m)
    for p in ps:
        denom = denom + jnp.sum(p, axis=-1, keepdims=True)
    return ps, denom


def _head_mean_matrix(width):
    h = np.arange(width) // HEAD_DIM
    return jnp.asarray((h[:, None] == h[None, :]) / HEAD_DIM, BF16)


def _head_rms(x, mean_mat, g):
    ms = _dot((x * x).astype(BF16), mean_mat)
    return x * lax.rsqrt(ms + RMS_EPS) * g


SWA_TILE = 512


def _swa_prompt_kernel(sinks_ref, q_ref, kv_ref, halo_ref, qg_ref, kg_ref, mq_ref, mk_ref, bias_ref,
                       y_ref, kn_ref):
    has_prev = pl.program_id(1) > 0
    kw = SWA_KV_HEADS * HEAD_DIM
    blk = SWA_BLOCK
    kv = kv_ref[0]
    halo = halo_ref[0]
    k_ext = jnp.concatenate([halo[:, :kw], kv[:, :kw]], axis=0)
    v_ext = jnp.concatenate([halo[:, kw:], kv[:, kw:]], axis=0).astype(BF16)
    kn = _head_rms(k_ext, mk_ref[...], kg_ref[...])
    kn_ref[0] = kn[SWA_TILE:]
    knb = kn.astype(BF16)
    qn = (_head_rms(q_ref[0], mq_ref[...], qg_ref[...]) * (HEAD_DIM ** -0.5)).astype(BF16)
    row = lax.broadcasted_iota(jnp.int32, (2 * blk, 1), 0)
    col = lax.broadcasted_iota(jnp.int32, (2 * blk, 2 * blk), 1)
    for c in range(SWA_TILE // blk):
        rows = slice(c * blk, (c + 1) * blk)
        keys = slice(c * blk, (c + 2) * blk)
        for kh in range(SWA_KV_HEADS):
            ksl = slice(kh * HEAD_DIM, (kh + 1) * HEAD_DIM)
            h0 = 2 * kh
            q2 = jnp.concatenate([qn[rows, h0 * HEAD_DIM:(h0 + 1) * HEAD_DIM],
                                  qn[rows, (h0 + 1) * HEAD_DIM:(h0 + 2) * HEAD_DIM]], axis=0)
            s = _dot_nt(q2, knb[keys, ksl]) + bias_ref[kh]
            if c == 0:
                s = jnp.where(has_prev | (col >= blk), s, NEG)
            sink = jnp.where(row < blk, sinks_ref[h0], sinks_ref[h0 + 1])
            (p,), denom = _softmax_parts((s,), sink)
            o = _dot(p.astype(BF16), v_ext[keys, ksl]) / denom
            y_ref[0, rows, h0 * HEAD_DIM:(h0 + 1) * HEAD_DIM] = o[:blk].astype(y_ref.dtype)
            y_ref[0, rows, (h0 + 1) * HEAD_DIM:(h0 + 2) * HEAD_DIM] = o[blk:].astype(y_ref.dtype)


def _swa_prompt(proj3d, sinks, qg, kg, bias):
    b, l, _ = proj3d.shape
    kw = SWA_KV_HEADS * HEAD_DIM
    per = SWA_TILE // SWA_BLOCK
    tile = (1, SWA_TILE, MIX_BLOCK)
    const2 = lambda bi, i: (0, 0)
    bias2 = bias.reshape(SWA_KV_HEADS, 2 * SWA_BLOCK, 2 * SWA_BLOCK)
    return pl.pallas_call(
        _swa_prompt_kernel,
        grid=(b, l // SWA_TILE),
        in_specs=[pl.BlockSpec(memory_space=pltpu.SMEM),
                  pl.BlockSpec(tile, lambda bi, i: (bi, i, COL_SWA_Q)),
                  pl.BlockSpec(tile, lambda bi, i: (bi, i, COL_SWA_KV)),
                  pl.BlockSpec((1, SWA_BLOCK, MIX_BLOCK),
                               lambda bi, i: (bi, jnp.maximum(i * per - 1, 0), COL_SWA_KV)),
                  pl.BlockSpec((1, MIX_BLOCK), const2),
                  pl.BlockSpec((1, kw), const2),
                  pl.BlockSpec((MIX_BLOCK, MIX_BLOCK), const2),
                  pl.BlockSpec((kw, kw), const2),
                  pl.BlockSpec((SWA_KV_HEADS, 2 * SWA_BLOCK, 2 * SWA_BLOCK), lambda bi, i: (0, 0, 0))],
        out_specs=[pl.BlockSpec(tile, lambda bi, i: (bi, i, 0)),
                   pl.BlockSpec((1, SWA_BLOCK, kw), lambda bi, i: (bi, 0, 0))],
        out_shape=[jax.ShapeDtypeStruct((b, l, MIX_BLOCK), BF16),
                   jax.ShapeDtypeStruct((b, SWA_BLOCK, kw), F32)],
        compiler_params=_cparams(("parallel", "arbitrary")),
        name="swa_prompt",
    )(sinks, proj3d, proj3d, proj3d, jnp.tile(qg, (1, SWA_HEADS)), jnp.tile(kg, (1, SWA_KV_HEADS)),
      _head_mean_matrix(MIX_BLOCK), _head_mean_matrix(kw), bias2)


def _layer_state_out(layer, depth, blk, shape, dtype, prev):
    zeros = (0,) * (len(blk) - 1)
    shape = jax.ShapeDtypeStruct((depth,) + shape, dtype)
    if prev is None:
        assert layer == 0
        return pl.BlockSpec((depth,) + blk, lambda i: (0, i) + zeros), shape, [], []
    spec = pl.BlockSpec((1,) + blk, lambda i: (layer, i) + zeros)
    return spec, shape, [prev], [pl.BlockSpec(memory_space=pl.ANY)]


def _zero_other_layers(ref):
    if ref.shape[0] > 1:
        ref[1:] = jnp.zeros((ref.shape[0] - 1,) + ref.shape[1:], ref.dtype)


def _swa_sample_kernel(sinks_ref, q_ref, kv_ref, ckt_ref, cvt_ref, qg_ref, kg_ref, mq_ref, mk_ref, bias_ref,
                       *rest, n_new):
    y_ref, nkt_ref, nvt_ref = rest[-3:]
    _, s_blk, kw, wb = ckt_ref.shape
    keep = wb - n_new
    q2 = _head_rms(q_ref[...], mq_ref[...], qg_ref[...]) * (HEAD_DIM ** -0.5)
    kv = kv_ref[...]
    kn2 = _head_rms(kv[:, :kw], mk_ref[...], kg_ref[...])
    v2 = kv[:, kw:]
    q3 = q2.reshape(s_blk, n_new, MIX_BLOCK)
    kn3 = kn2.reshape(s_blk, n_new, kw)
    v3 = v2.reshape(s_blk, n_new, kw)
    ckt = ckt_ref[0]
    cvt = cvt_ref[0]
    bdot = functools.partial(jnp.einsum, preferred_element_type=F32)
    for h in range(SWA_HEADS):
        kh = h // (SWA_HEADS // SWA_KV_HEADS)
        ksl = slice(kh * HEAD_DIM, (kh + 1) * HEAD_DIM)
        hsl = slice(h * HEAD_DIM, (h + 1) * HEAD_DIM)
        qh = q3[:, :, hsl]
        s_c = bdot('sqd,sdw->sqw', qh, ckt[:, ksl, :]) + bias_ref[h, :, :wb]
        s_n = bdot('sqd,sjd->sqj', qh, kn3[:, :, ksl]) + bias_ref[h, :, wb:]
        (p_c, p_n), denom = _softmax_parts((s_c, s_n), sinks_ref[h])
        o = bdot('sqw,sdw->sqd', p_c, cvt[:, ksl, :]) + bdot('sqj,sjd->sqd', p_n, v3[:, :, ksl])
        y_ref[:, hsl] = (o / denom).reshape(s_blk * n_new, HEAD_DIM).astype(y_ref.dtype)
    lane = lax.broadcasted_iota(jnp.int32, (kw, wb), 1)
    for new2, old, out_ref in ((kn2, ckt, nkt_ref), (v2, cvt, nvt_ref)):
        new_t = new2.T
        shifted = pltpu.roll(old.reshape(s_blk * kw, wb), keep, 1).reshape(s_blk, kw, wb)
        _zero_other_layers(out_ref)
        for s in range(s_blk):
            cols = pltpu.roll(new_t, (keep - s * n_new) % wb, 1)
            out_ref[0, s] = jnp.where(lane >= keep, cols, shifted[s])


def _swa_sample(proj2d, cache_kt, cache_vt, layer, prev_new, sinks, qg, kg, bias, n_new, s_blk):
    depth, nseq, kw, wb = cache_kt.shape
    assert s_blk * n_new == wb
    rows = s_blk * n_new
    blk = (rows, MIX_BLOCK)
    cblk = (s_blk, kw, wb)
    cin = pl.BlockSpec((1,) + cblk, lambda i: (layer, i, 0, 0))
    const2 = lambda i: (0, 0)
    prev_k, prev_v = prev_new if prev_new is not None else (None, None)
    kspec, kshape, kin, kin_specs = _layer_state_out(layer, depth, cblk, (nseq, kw, wb), F32, prev_k)
    vspec, vshape, vin, vin_specs = _layer_state_out(layer, depth, cblk, (nseq, kw, wb), F32, prev_v)
    n_main = 10
    aliases = {n_main + j: 1 + j for j in range(len(kin + vin))}
    return pl.pallas_call(
        functools.partial(_swa_sample_kernel, n_new=n_new),
        grid=(nseq // s_blk,),
        in_specs=[pl.BlockSpec(memory_space=pltpu.SMEM),
                  pl.BlockSpec(blk, lambda i: (i, COL_SWA_Q)),
                  pl.BlockSpec(blk, lambda i: (i, COL_SWA_KV)),
                  cin, cin,
                  pl.BlockSpec((1, MIX_BLOCK), const2),
                  pl.BlockSpec((1, kw), const2),
                  pl.BlockSpec((MIX_BLOCK, MIX_BLOCK), const2),
                  pl.BlockSpec((kw, kw), const2),
                  pl.BlockSpec((SWA_HEADS, n_new, wb + n_new), lambda i: (0, 0, 0))] + kin_specs + vin_specs,
        out_specs=[pl.BlockSpec(blk, lambda i: (i, 0)), kspec, vspec],
        out_shape=[jax.ShapeDtypeStruct((nseq * n_new, MIX_BLOCK), BF16), kshape, vshape],
        input_output_aliases=aliases,
        compiler_params=_cparams(("parallel",)),
        name="swa_sample",
    )(sinks, proj2d, proj2d, cache_kt, cache_vt, jnp.tile(qg, (1, SWA_HEADS)), jnp.tile(kg, (1, SWA_KV_HEADS)),
      _head_mean_matrix(MIX_BLOCK), _head_mean_matrix(kw), bias, *kin, *vin)


def _ssm_kernel(u_ref, h0_ref, wb_ref, tab_ref, wc_ref, d_ref, wglu_ref, y_ref, hn_ref,
                bu_scr, carry_scr, *, chained, tiles_per_seq, tb):
    n = SSM_N

    def project_in(rows):
        bu_scr[rows, :] = _dot(u_ref[rows, :].astype(BF16), wb_ref[...])

    def project_out(rows):
        y = _dot(bu_scr[rows, :].astype(BF16), wc_ref[...]) + d_ref[...] * u_ref[rows, :]
        y = 0.5 * y * (1.0 + jnp.tanh(math.sqrt(2.0 / math.pi) * (y + 0.044715 * (y * y * y))))
        return (y * _sigmoid(_dot(y.astype(BF16), wglu_ref[...]))).astype(y_ref.dtype)

    def tile_scan(r0, cr, ci):
        hr = bu_scr[pl.ds(r0, SCAN_ROWS), :n]
        hi = bu_scr[pl.ds(r0, SCAN_ROWS), n:]
        for k, shift in enumerate((1, 2, 4)):
            ar, ai = tab_ref[2 * k], tab_ref[2 * k + 1]
            sr, si = pltpu.roll(hr, shift, 0), pltpu.roll(hi, shift, 0)
            hr, hi = hr + ar * sr - ai * si, hi + ar * si + ai * sr
        pr, pi = tab_ref[6], tab_ref[7]
        hr, hi = hr + pr * cr - pi * ci, hi + pr * ci + pi * cr
        bu_scr[pl.ds(r0, SCAN_ROWS), :n] = hr
        bu_scr[pl.ds(r0, SCAN_ROWS), n:] = hi
        return hr[SCAN_ROWS - 1:], hi[SCAN_ROWS - 1:]

    if chained:
        @pl.when(pl.program_id(1) % tiles_per_seq == 0)
        def _():
            carry_scr[...] = jnp.zeros_like(carry_scr)

        project_in(slice(None))
        cr, ci = carry_scr[:, :n], carry_scr[:, n:]
        for r0 in range(0, tb, SCAN_ROWS):
            lr, li = tile_scan(r0, cr, ci)
            cr, ci = jnp.broadcast_to(lr, (SCAN_ROWS, n)), jnp.broadcast_to(li, (SCAN_ROWS, n))
        y_ref[0] = project_out(slice(None))
        carry_scr[:, :n] = cr
        carry_scr[:, n:] = ci
        hn_ref[0, :, :n] = cr
        hn_ref[0, :, n:] = ci
    else:
        project_in(slice(None))

        def body(t, _):
            r0 = pl.multiple_of(t * SCAN_ROWS, SCAN_ROWS)
            h0 = h0_ref[pl.ds(t, 1), :]
            cr = jnp.broadcast_to(h0[:, :n], (SCAN_ROWS, n))
            ci = jnp.broadcast_to(h0[:, n:], (SCAN_ROWS, n))
            lr, li = tile_scan(r0, cr, ci)
            hn_ref[pl.ds(t, 1), :n] = lr
            hn_ref[pl.ds(t, 1), n:] = li
            return 0

        lax.fori_loop(0, tb // SCAN_ROWS, body, 0)
        y_ref[...] = project_out(slice(None))


def _ssm_common_specs(zero_map2, zero_map3):
    return [pl.BlockSpec((SSM_WIDTH, 2 * SSM_N), zero_map2),
            pl.BlockSpec((8, SCAN_ROWS, SSM_N), zero_map3),
            pl.BlockSpec((2 * SSM_N, SSM_WIDTH), zero_map2),
            pl.BlockSpec((1, SSM_WIDTH), zero_map2),
            pl.BlockSpec((SSM_WIDTH, SSM_WIDTH), zero_map2)]


def _ssm_prompt(proj3d, sp, tb):
    b, l, _ = proj3d.shape
    nt = l // tb
    dummy_h0 = jnp.zeros((SCAN_ROWS, 2 * SSM_N), F32)
    kern = functools.partial(_ssm_kernel, chained=True, tiles_per_seq=nt, tb=tb)

    def kernel(u_ref, h0_ref, wb, tab, wc, d, wglu, y_ref, hn_ref, bu_scr, carry_scr):
        kern(u_ref.at[0], h0_ref, wb, tab, wc, d, wglu, y_ref, hn_ref, bu_scr, carry_scr)

    return pl.pallas_call(
        kernel,
        grid=(b, nt),
        in_specs=[pl.BlockSpec((1, tb, SSM_WIDTH), lambda bi, i: (bi, i, COL_SSM)),
                  pl.BlockSpec((SCAN_ROWS, 2 * SSM_N), lambda bi, i: (0, 0))]
                 + _ssm_common_specs(lambda bi, i: (0, 0), lambda bi, i: (0, 0, 0)),
        out_specs=[pl.BlockSpec((1, tb, SSM_WIDTH), lambda bi, i: (bi, i, 0)),
                   pl.BlockSpec((1, SCAN_ROWS, 2 * SSM_N), lambda bi, i: (bi, 0, 0))],
        out_shape=[jax.ShapeDtypeStruct((b, l, SSM_WIDTH), BF16),
                   jax.ShapeDtypeStruct((b, SCAN_ROWS, 2 * SSM_N), F32)],
        scratch_shapes=[pltpu.VMEM((tb, 2 * SSM_N), F32), pltpu.VMEM((SCAN_ROWS, 2 * SSM_N), F32)],
        compiler_params=_cparams(("parallel", "arbitrary")),
        name="ssm_prompt",
    )(proj3d, dummy_h0, sp["wb"], sp["tab"], sp["wc"], sp["d"], sp["wglu"])


def _ssm_sample(proj2d, h0, sp):
    rows = proj2d.shape[0]
    nseq = h0.shape[0]
    kern = functools.partial(_ssm_kernel, chained=False, tiles_per_seq=1, tb=rows)
    return pl.pallas_call(
        kern,
        grid=(1,),
        in_specs=[pl.BlockSpec((rows, SSM_WIDTH), lambda i: (0, COL_SSM)),
                  pl.BlockSpec((nseq, 2 * SSM_N), lambda i: (0, 0))]
                 + _ssm_common_specs(lambda i: (0, 0), lambda i: (0, 0, 0)),
        out_specs=[pl.BlockSpec((rows, SSM_WIDTH), lambda i: (0, 0)),
                   pl.BlockSpec((nseq, 2 * SSM_N), lambda i: (0, 0))],
        out_shape=[jax.ShapeDtypeStruct((rows, SSM_WIDTH), BF16),
                   jax.ShapeDtypeStruct((nseq, 2 * SSM_N), F32)],
        scratch_shapes=[pltpu.VMEM((rows, 2 * SSM_N), F32), pltpu.VMEM((SCAN_ROWS, 2 * SSM_N), F32)],
        compiler_params=_cparams(("arbitrary",)),
        name="ssm_sample",
    )(proj2d, h0, sp["wb"], sp["tab"], sp["wc"], sp["d"], sp["wglu"])


def _ssm_params(lam_re, lam_im, log_dt, b_re, b_im, c_re, c_im, d_skip, w_glu):
    lr, li = lam_re.astype(F32), lam_im.astype(F32)
    dt = jnp.exp(log_dt.astype(F32))[:, None]
    mag = jnp.exp(lr * dt)
    ab_re, ab_im = mag * jnp.cos(li * dt), mag * jnp.sin(li * dt)
    den = lr * lr + li * li
    nr = ab_re - 1.0
    f_re = (nr * lr + ab_im * li) / den
    f_im = (ab_im * lr - nr * li) / den
    br, bi = b_re.astype(F32), b_im.astype(F32)
    bb_re = f_re[..., None] * br - f_im[..., None] * bi
    bb_im = f_re[..., None] * bi + f_im[..., None] * br
    eye = jnp.eye(SSM_GROUPS, dtype=F32)

    def in_mat(bb):
        return jnp.einsum('gpc,gh->gchp', bb, eye).reshape(SSM_WIDTH, SSM_N)

    def out_mat(c):
        return jnp.einsum('gcp,gh->gphc', c.astype(F32), eye).reshape(SSM_N, SSM_WIDTH)

    wb = jnp.concatenate([in_mat(bb_re), in_mat(bb_im)], axis=1).astype(BF16)
    wc = jnp.concatenate([out_mat(c_re), -out_mat(c_im)], axis=0).astype(BF16)

    ar, ai = ab_re.reshape(1, SSM_N), ab_im.reshape(1, SSM_N)

    def cmul(x, y):
        return (x[0] * y[0] - x[1] * y[1], x[0] * y[1] + x[1] * y[0])

    pw = [(ar, ai)]
    for _ in range(SCAN_ROWS - 1):
        pw.append(cmul(pw[-1], (ar, ai)))
    row = jnp.arange(SCAN_ROWS)[:, None]
    tabs = []
    for shift in (1, 2, 4):
        for part in pw[shift - 1]:
            tabs.append(jnp.where(row >= shift, part, 0.0))
    tabs.append(jnp.concatenate([p[0] for p in pw], axis=0))
    tabs.append(jnp.concatenate([p[1] for p in pw], axis=0))
    tab = jnp.stack([jnp.broadcast_to(t, (SCAN_ROWS, SSM_N)) for t in tabs])
    return dict(wb=wb, tab=tab, wc=wc, d=d_skip.astype(F32).reshape(1, SSM_WIDTH), wglu=w_glu.astype(BF16))


_RET_G = 1.0 - np.exp2(-5.0 - np.arange(RET_HEADS, dtype=np.float64))


def _ret_consts(chunk, n_rows):
    idx = np.arange(n_rows)
    loc = idx % chunk
    same = (idx[:, None] // chunk) == (idx[None, :] // chunk)
    diff = loc[:, None] - loc[None, :]
    dec = np.where(same & (diff >= 0), _RET_G[:, None, None] ** np.maximum(diff, 0)[None], 0.0)
    qdec = np.repeat((_RET_G[None, :] ** (loc[:, None] + 1.0)), HEAD_DIM, axis=1)
    kdec = np.repeat((_RET_G[None, :] ** (chunk - 1.0 - loc[:, None])), HEAD_DIM, axis=1)
    return (jnp.asarray(dec, F32), jnp.asarray(qdec, F32), jnp.asarray(kdec, F32),
            jnp.asarray(_RET_G ** chunk, F32))


def _rope_tables(pos):
    half = HEAD_DIM // 2
    theta = 1.0 / (ROPE_BASE ** np.linspace(0.0, 1.0, half))
    ang = np.asarray(pos, np.float64)[:, None] * theta[None, :]
    cos = np.repeat(np.cos(ang), 2, axis=1)
    sin = np.repeat(np.sin(ang), 2, axis=1) * np.tile([-1.0, 1.0], half)[None]
    return (jnp.asarray(np.tile(cos, (1, RET_HEADS)), F32), jnp.asarray(np.tile(sin, (1, RET_HEADS)), F32))


def _rotate_pairs(x, cos, sin_signed):
    lane = lax.broadcasted_iota(jnp.int32, x.shape, 1)
    nxt = pltpu.roll(x, x.shape[1] - 1, 1)
    prv = pltpu.roll(x, 1, 1)
    return x * cos + jnp.where(lane % 2 == 0, nxt, prv) * sin_signed


def _ret_head_out(o, gate, norm):
    ms = jnp.mean(o * o, axis=-1, keepdims=True)
    return o * lax.rsqrt(ms + RMS_EPS) * norm * (gate * _sigmoid(gate))


RET_TILE = 512


def _ret_prompt_kernel(gc_ref, q_ref, k_ref, v_ref, g_ref, cos_ref, sin_ref, dec_ref, qdec_ref,
                       kdec_ref, norm_ref, mh_ref, y_ref, r_ref, o_scr):
    @pl.when(pl.program_id(1) == 0)
    def _():
        r_ref[...] = jnp.zeros_like(r_ref)

    cos, sin = cos_ref[...], sin_ref[...]
    q = _rotate_pairs(q_ref[0], cos, sin)
    k = _rotate_pairs(k_ref[0], cos, sin) * (HEAD_DIM ** -0.5)
    qb, kb, vb = q.astype(BF16), k.astype(BF16), v_ref[0].astype(BF16)
    kdb = (k * kdec_ref[...]).astype(BF16)
    for c in range(RET_TILE // RET_CHUNK):
        rows = slice(c * RET_CHUNK, (c + 1) * RET_CHUNK)
        for h in range(RET_HEADS):
            sl = slice(h * HEAD_DIM, (h + 1) * HEAD_DIM)
            qh, vh = qb[rows, sl], vb[rows, sl]
            s = _dot_nt(qh, kb[rows, sl]) * dec_ref[h]
            r = r_ref[0, h]
            o_scr[rows, sl] = _dot(s.astype(BF16), vh) + _dot(qh, r.astype(BF16)) * qdec_ref[rows, sl]
            r_ref[0, h] = gc_ref[h] * r + _dot_tn(kdb[rows, sl], vh)
    g = g_ref[0]
    y_ref[0] = (_head_rms(o_scr[...], mh_ref[...], norm_ref[...]) * (g * _sigmoid(g))).astype(y_ref.dtype)


def _ret_prompt(proj3d, cos, sin, norm):
    b, l, _ = proj3d.shape
    c = RET_CHUNK
    dec, _, _, gc = _ret_consts(c, c)
    _, qdec, kdec, _ = _ret_consts(c, RET_TILE)
    blk = (1, RET_TILE, MIX_BLOCK)
    tspec = pl.BlockSpec((RET_TILE, MIX_BLOCK), lambda bi, i: (i, 0))
    cspec = pl.BlockSpec((RET_TILE, MIX_BLOCK), lambda bi, i: (0, 0))
    return pl.pallas_call(
        _ret_prompt_kernel,
        grid=(b, l // RET_TILE),
        in_specs=[pl.BlockSpec(memory_space=pltpu.SMEM),
                  pl.BlockSpec(blk, lambda bi, i: (bi, i, COL_RET_Q)),
                  pl.BlockSpec(blk, lambda bi, i: (bi, i, COL_RET_K)),
                  pl.BlockSpec(blk, lambda bi, i: (bi, i, COL_RET_V)),
                  pl.BlockSpec(blk, lambda bi, i: (bi, i, COL_RET_G)),
                  tspec, tspec,
                  pl.BlockSpec((RET_HEADS, c, c), lambda bi, i: (0, 0, 0)),
                  cspec, cspec,
                  pl.BlockSpec((1, MIX_BLOCK), lambda bi, i: (0, 0)),
                  pl.BlockSpec((MIX_BLOCK, MIX_BLOCK), lambda bi, i: (0, 0))],
        out_specs=[pl.BlockSpec(blk, lambda bi, i: (bi, i, 0)),
                   pl.BlockSpec((1, RET_HEADS, HEAD_DIM, HEAD_DIM), lambda bi, i: (bi, 0, 0, 0))],
        out_shape=[jax.ShapeDtypeStruct((b, l, MIX_BLOCK), BF16),
                   jax.ShapeDtypeStruct((b, RET_HEADS, HEAD_DIM, HEAD_DIM), F32)],
        scratch_shapes=[pltpu.VMEM((RET_TILE, MIX_BLOCK), F32)],
        compiler_params=_cparams(("parallel", "arbitrary")),
        name="ret_prompt",
    )(gc, proj3d, proj3d, proj3d, proj3d, cos, sin, dec, qdec, kdec, norm, _head_mean_matrix(MIX_BLOCK))


def _ret_sample_kernel(gc_ref, q_ref, k_ref, v_ref, g_ref, cos_ref, sin_ref, dec_ref, qdec_ref,
                       kdec_ref, norm_ref, r0_ref, *rest, n_new, s_blk):
    y_ref, rn_ref = rest[-2:]
    _zero_other_layers(rn_ref)
    cos, sin = cos_ref[...], sin_ref[...]
    q = _rotate_pairs(q_ref[...], cos, sin)
    k = _rotate_pairs(k_ref[...], cos, sin) * (HEAD_DIM ** -0.5)
    v = v_ref[...]
    g = g_ref[...]
    kd = k * kdec_ref[...]
    qdec = qdec_ref[...]
    norm = norm_ref[...]
    rows = s_blk * n_new
    seq = lax.broadcasted_iota(jnp.int32, (rows, HEAD_DIM), 0) // n_new
    for h in range(RET_HEADS):
        sl = slice(h * HEAD_DIM, (h + 1) * HEAD_DIM)
        qf, kdf = q[:, sl], kd[:, sl]
        qh, kh, vh = qf.astype(BF16), k[:, sl].astype(BF16), v[:, sl].astype(BF16)
        s = _dot_nt(qh, kh) * dec_ref[h]
        cross = jnp.zeros((rows, HEAD_DIM), F32)
        for si in range(s_blk):
            mine = seq == si
            r = r0_ref[0, si, h]
            cross = cross + _dot(jnp.where(mine, qf, 0.0).astype(BF16), r.astype(BF16))
            rn_ref[0, si, h] = gc_ref[h] * r + _dot_tn(jnp.where(mine, kdf, 0.0).astype(BF16), vh)
        o = _dot(s.astype(BF16), vh) + cross * qdec[:, sl]
        y_ref[:, sl] = _ret_head_out(o, g[:, sl], norm[:, sl]).astype(y_ref.dtype)


def _ret_sample(proj2d, r0, layer, prev_new, cos, sin, norm, n_new, s_blk):
    rows = s_blk * n_new
    depth, nseq = r0.shape[:2]
    dec, qdec, kdec, gc = _ret_consts(n_new, rows)
    blk = (rows, MIX_BLOCK)
    cspec = pl.BlockSpec(blk, lambda i: (0, 0))
    rblk = (s_blk, RET_HEADS, HEAD_DIM, HEAD_DIM)
    rspec, rshape, rin, rin_specs = _layer_state_out(layer, depth, rblk, (nseq,) + rblk[1:], F32, prev_new)
    return pl.pallas_call(
        functools.partial(_ret_sample_kernel, n_new=n_new, s_blk=s_blk),
        grid=(nseq // s_blk,),
        in_specs=[pl.BlockSpec(memory_space=pltpu.SMEM),
                  pl.BlockSpec(blk, lambda i: (i, COL_RET_Q)),
                  pl.BlockSpec(blk, lambda i: (i, COL_RET_K)),
                  pl.BlockSpec(blk, lambda i: (i, COL_RET_V)),
                  pl.BlockSpec(blk, lambda i: (i, COL_RET_G)),
                  cspec, cspec,
                  pl.BlockSpec((RET_HEADS, rows, rows), lambda i: (0, 0, 0)),
                  cspec, cspec,
                  pl.BlockSpec((1, MIX_BLOCK), lambda i: (0, 0)),
                  pl.BlockSpec((1,) + rblk, lambda i: (layer, i, 0, 0, 0))] + rin_specs,
        out_specs=[pl.BlockSpec(blk, lambda i: (i, 0)), rspec],
        out_shape=[jax.ShapeDtypeStruct((nseq * n_new, MIX_BLOCK), BF16), rshape],
        input_output_aliases={12 + j: 1 for j in range(len(rin))},
        compiler_params=_cparams(("parallel",)),
        name="ret_sample",
    )(gc, proj2d, proj2d, proj2d, proj2d, cos, sin, dec, qdec, kdec, norm, r0, *rin)


def _block_diag(w):
    g, n, _ = w.shape
    return jnp.einsum('gcd,gh->gchd', w, jnp.eye(g, dtype=w.dtype)).reshape(g * n, g * n)


def _layer_params(l, p):
    return dict(
        norm_mix=p['norm_mix'][l].reshape(1, D_MODEL),
        norm_ffn=p['norm_ffn'][l].reshape(1, D_MODEL),
        w_in=p['w_in'][l].astype(BF16),
        w_out=p['w_out'][l].astype(BF16),
        pool_w=_block_diag(p['pool_w'][l].astype(F32)).astype(BF16),
        pool_scale=p['pool_scale'][l].astype(F32).reshape(1, POOL_WIDTH),
        qg=p['swa_q_norm'][l].astype(F32).reshape(1, HEAD_DIM),
        kg=p['swa_k_norm'][l].astype(F32).reshape(1, HEAD_DIM),
        sinks=p['swa_sinks'][l].astype(F32),
        ssm=_ssm_params(p['ssm_lambda_re'][l], p['ssm_lambda_im'][l], p['ssm_log_dt'][l],
                        p['ssm_b_re'][l], p['ssm_b_im'][l], p['ssm_c_re'][l], p['ssm_c_im'][l],
                        p['ssm_d'][l], p['ssm_w_glu'][l]),
        ret_norm=p['ret_norm'][l].astype(F32).reshape(1, MIX_BLOCK),
    )


def _channel_mix(streams, l, lp, p):
    i = l // 2
    g, w_out = lp['norm_ffn'], lp['w_out']
    if l % 2 == 0:
        wg, wu, wd = (p[k][i].astype(BF16) for k in ('ffn_w_gate', 'ffn_w_up', 'ffn_w_down'))
        return [_out_proj_ffn(x, ys, w_out, g, wg, wu, wd, 512, D_FF) for x, ys in streams]
    wr3 = _router_weights(p['moe_router'][i])
    wg, wu, wd = (p[k][i].astype(BF16) for k in ('moe_w_gate', 'moe_w_up', 'moe_w_down'))
    x1s, routes = zip(*[_out_proj_router(x, ys, w_out, g, wr3, 512) for x, ys in streams])
    return _moe(x1s, routes, g, wg, wu, wd)


def _mix_prompt(x2, b, l, lp, bias, cos, sin):
    proj2 = _norm_matmul(x2, lp['norm_mix'], lp['w_in'], 512)
    proj3 = proj2.reshape(b, l, IN_WIDTH)
    tb = 512
    y_pool = _pool(proj2, COL_POOL, lp['pool_w'], lp['pool_scale'], n_rows=b * l, tb=tb,
                   tiles_per_seq=l // tb, pos0=0)
    y_swa, kn = _swa_prompt(proj3, lp['sinks'], lp['qg'], lp['kg'], bias)
    y_ssm, hn = _ssm_prompt(proj3, lp['ssm'], 512)
    y_ret, rn = _ret_prompt(proj3, cos, sin, lp['ret_norm'])
    ys = (y_pool, y_swa.reshape(b * l, MIX_BLOCK), y_ssm.reshape(b * l, MIX_BLOCK),
          y_ret.reshape(b * l, MIX_BLOCK))
    kw = SWA_KV_HEADS * HEAD_DIM
    hn = hn[:, 0]
    states = (proj3[:, l - POOL_BUF:, :POOL_WIDTH],
              kn.reshape(b, SWA_WINDOW, SWA_KV_HEADS, HEAD_DIM),
              proj3[:, l - SWA_WINDOW:, COL_SWA_KV * MIX_BLOCK + kw:(COL_SWA_KV + 1) * MIX_BLOCK]
              .reshape(b, SWA_WINDOW, SWA_KV_HEADS, HEAD_DIM),
              jnp.stack([hn[:, :SSM_N], hn[:, SSM_N:]], axis=-1).reshape(b, SSM_GROUPS, SSM_STATE, 2),
              rn)
    return ys, states


SAMPLE_SEQ_BLOCK = 16


def _cache_transposed(cache):
    depth, nseq, wb = cache.shape[:3]
    return jnp.swapaxes(cache.astype(F32).reshape(depth, nseq, wb, SWA_KV_HEADS * HEAD_DIM), 2, 3)


def _mix_sample(x2, nseq, n_new, start_pos, lp, layer, st, prev_new, bias, cos, sin):
    state_pool, cache_kt, cache_vt, state_ssm, state_ret = st
    prev_kv, prev_ret = prev_new if prev_new is not None else (None, None)
    rows = nseq * n_new
    kw = SWA_KV_HEADS * HEAD_DIM
    ext_rows = POOL_HALO + n_new
    proj2 = _norm_matmul(x2, lp['norm_mix'], lp['w_in'], 512)
    proj3 = proj2.reshape(nseq, n_new, IN_WIDTH)
    u_pool = proj3[:, :, :POOL_WIDTH]
    buf = state_pool.astype(F32)
    ext = jnp.concatenate([jnp.zeros((nseq, POOL_HALO - POOL_BUF, POOL_WIDTH), F32), buf, u_pool], axis=1)
    y_pool = _pool(ext.reshape(nseq * ext_rows, POOL_WIDTH), 0, lp['pool_w'], lp['pool_scale'],
                   n_rows=nseq * ext_rows, tb=nseq * ext_rows, tiles_per_seq=1, pos0=start_pos)
    y_pool = y_pool.reshape(nseq, ext_rows, POOL_WIDTH)[:, POOL_HALO:].reshape(rows, POOL_WIDTH)
    y_swa, nkt, nvt = _swa_sample(proj2, cache_kt, cache_vt, layer, prev_kv, lp['sinks'], lp['qg'], lp['kg'],
                                  bias, n_new, SAMPLE_SEQ_BLOCK)
    h0 = state_ssm.astype(F32).reshape(nseq, SSM_N, 2)
    h0 = jnp.concatenate([h0[..., 0], h0[..., 1]], axis=1)
    y_ssm, hn = _ssm_sample(proj2, h0, lp['ssm'])
    y_ret, rn = _ret_sample(proj2, state_ret, layer, prev_ret, cos, sin, lp['ret_norm'], n_new,
                            SAMPLE_SEQ_BLOCK)
    ys = (y_pool, y_swa, y_ssm, y_ret)
    states = (jnp.concatenate([buf, u_pool], axis=1)[:, -POOL_BUF:],
              jnp.stack([hn[:, :SSM_N], hn[:, SSM_N:]], axis=-1).reshape(nseq, SSM_GROUPS, SSM_STATE, 2))
    return ys, states, ((nkt, nvt), rn)


def _forward(x_prompt, x_sample, past_len, sample_state, p, rel_bias):
    b, l, d = x_prompt.shape
    nseq, n_new, _ = x_sample.shape
    wb = sample_state[1].shape[2]
    depth = p['norm_mix'].shape[0]
    bias_p = _swa_bias(rel_bias, np.arange(SWA_BLOCK)[:, None] - np.arange(2 * SWA_BLOCK)[None, :] + SWA_BLOCK)
    bias_s = _swa_bias(rel_bias, np.arange(n_new)[:, None] - np.arange(wb + n_new)[None, :] + wb)
    rope_p = _rope_tables(np.arange(l))
    rope_s = _rope_tables(past_len + (np.arange(SAMPLE_SEQ_BLOCK * n_new) % n_new))
    xp = x_prompt.reshape(b * l, d)
    xs = x_sample.reshape(nseq * n_new, d)
    cache_kt, cache_vt = _cache_transposed(sample_state[1]), _cache_transposed(sample_state[2])
    st_p, st_s, stacked = [], [], None
    for li in range(depth):
        lp = _layer_params(li, p)
        yp, sp = _mix_prompt(xp, b, l, lp, bias_p, *rope_p)
        st = (sample_state[0][li], cache_kt, cache_vt, sample_state[3][li], sample_state[4])
        ys, ss, stacked = _mix_sample(xs, nseq, n_new, past_len, lp, li, st, stacked, bias_s, *rope_s)
        xp, xs = _channel_mix([(xp, yp), (xs, ys)], li, lp, p)
        st_p.append(sp)
        st_s.append(ss)
    (new_kt, new_vt), new_ret = stacked
    kv_shape = (depth, nseq, wb, SWA_KV_HEADS, HEAD_DIM)
    sample_out = (jnp.stack([s[0] for s in st_s]),
                  jnp.swapaxes(new_kt, 2, 3).reshape(kv_shape),
                  jnp.swapaxes(new_vt, 2, 3).reshape(kv_shape),
                  jnp.stack([s[1] for s in st_s]),
                  new_ret)
    outs = [xp.reshape(b, l, d), xs.reshape(nseq, n_new, d)]
    for k in range(5):
        outs.append(jnp.stack([s[k] for s in st_p]))
        outs.append(sample_out[k])
    return tuple(outs)


PAST_LEN = 16384


def kernel(x_prompt, x_sample, state_pool, cache_swa_k, cache_swa_v, state_ssm, state_ret,
           norm_mix, norm_ffn, w_in, w_out, pool_w, pool_scale, swa_q_norm, swa_k_norm, swa_sinks,
           rel_bias, ssm_lambda_re, ssm_lambda_im, ssm_log_dt, ssm_b_re, ssm_b_im, ssm_c_re, ssm_c_im,
           ssm_d, ssm_w_glu, ret_norm, ffn_w_gate, ffn_w_up, ffn_w_down, moe_router, moe_w_gate,
           moe_w_up, moe_w_down):
    p = dict(norm_mix=norm_mix, norm_ffn=norm_ffn, w_in=w_in, w_out=w_out, pool_w=pool_w,
             pool_scale=pool_scale, swa_q_norm=swa_q_norm, swa_k_norm=swa_k_norm, swa_sinks=swa_sinks,
             ssm_lambda_re=ssm_lambda_re, ssm_lambda_im=ssm_lambda_im, ssm_log_dt=ssm_log_dt,
             ssm_b_re=ssm_b_re, ssm_b_im=ssm_b_im, ssm_c_re=ssm_c_re, ssm_c_im=ssm_c_im,
             ssm_d=ssm_d, ssm_w_glu=ssm_w_glu, ret_norm=ret_norm,
             ffn_w_gate=ffn_w_gate, ffn_w_up=ffn_w_up, ffn_w_down=ffn_w_down, moe_router=moe_router,
             moe_w_gate=moe_w_gate, moe_w_up=moe_w_up, moe_w_down=moe_w_down)
    return _forward(x_prompt, x_sample, PAST_LEN,
                    (state_pool, cache_swa_k, cache_swa_v, state_ssm, state_ret), p, rel_bias)
```

```python
import functools
import math

import numpy as np
import jax
import jax.numpy as jnp
from jax import lax
from jax.experimental import pallas as pl
from jax.experimental.pallas import tpu as pltpu

F32 = jnp.float32
BF16 = jnp.bfloat16

D_MODEL = 1024
HEAD_DIM = 64
POOL_WIDTH = 256
POOL_WINDOWS = (2, 4, 8, 16)
POOL_BUF = 15
POOL_HALO = 16
SWA_HEADS = 4
SWA_KV_HEADS = 2
SWA_WINDOW = 128
SWA_BLOCK = 128
SSM_WIDTH = 256
SSM_CH = 16
SSM_GROUPS = 16
SSM_STATE = 64
SSM_N = SSM_GROUPS * SSM_STATE
RET_HEADS = 4
RET_CHUNK = 128
ROPE_BASE = 10000.0
IN_WIDTH = 2048
MIX_BLOCK = 256
D_FF = 2816
N_EXPERTS = 8
T5_BUCKETS = 32
T5_MAX_DIST = 128
RMS_EPS = 1e-6
NEG = -1e30
SUBLANES = 8
SCAN_ROWS = SUBLANES

COL_POOL, COL_SWA_Q, COL_SWA_KV, COL_SSM, COL_RET_Q, COL_RET_K, COL_RET_V, COL_RET_G = range(8)

VMEM_LIMIT = 48 * 1024 * 1024
FFN_VMEM_LIMIT = 58 * 1024 * 1024


def _cparams(sem, vmem=VMEM_LIMIT):
    return pltpu.CompilerParams(dimension_semantics=sem, vmem_limit_bytes=vmem)


def _rms(x, g):
    ms = jnp.mean(x * x, axis=-1, keepdims=True)
    return x * lax.rsqrt(ms + RMS_EPS) * g


def _dot(a, b):
    return jnp.dot(a, b, preferred_element_type=F32)


def _dot_nt(a, b):
    return lax.dot_general(a, b, (((1,), (1,)), ((), ())), preferred_element_type=F32)


def _dot_tn(a, b):
    return lax.dot_general(a, b, (((0,), (0,)), ((), ())), preferred_element_type=F32)


def _sigmoid(x):
    return 1.0 / (1.0 + jnp.exp(-x))


def _norm_matmul_kernel(x_ref, g_ref, w_ref, o_ref):
    h = _rms(x_ref[...], g_ref[...]).astype(BF16)
    o_ref[...] = _dot(h, w_ref[...])


def _norm_matmul(x, g, w, tm):
    t, d = x.shape
    tm = min(tm, t)
    n = w.shape[1]
    return pl.pallas_call(
        _norm_matmul_kernel,
        grid=(t // tm,),
        in_specs=[pl.BlockSpec((tm, d), lambda i: (i, 0)),
                  pl.BlockSpec((1, d), lambda i: (0, 0)),
                  pl.BlockSpec((d, n), lambda i: (0, 0))],
        out_specs=pl.BlockSpec((tm, n), lambda i: (i, 0)),
        out_shape=jax.ShapeDtypeStruct((t, n), F32),
        compiler_params=_cparams(("parallel",)),
        name="norm_matmul",
    )(x, g, w)


FFN_SUBCHUNK = 512


def _swiglu_chunk(h, wg_ref, wu_ref, wd_ref):
    tf = wg_ref.shape[1]
    y = None
    for lo in range(0, tf, FFN_SUBCHUNK):
        hi = min(lo + FFN_SUBCHUNK, tf)
        a = _dot(h, wg_ref[:, lo:hi])
        b = _dot(h, wu_ref[:, lo:hi])
        part = _dot((a * _sigmoid(a) * b).astype(BF16), wd_ref[lo:hi, :])
        y = part if y is None else y + part
    return y


def _mixed_residual(x_ref, y_refs, w_ref, rows=slice(None)):
    y = jnp.concatenate([y_ref[rows, :] for y_ref in y_refs], axis=1)
    return x_ref[rows, :] + _dot(y, w_ref[...])


def _mix_in_specs(tm, d, imap):
    yspec = pl.BlockSpec((tm, MIX_BLOCK), imap(lambda i: (i, 0)))
    return [pl.BlockSpec((tm, d), imap(lambda i: (i, 0))), yspec, yspec, yspec, yspec,
            pl.BlockSpec((d, d), imap(lambda i: (0, 0)))]


def _out_proj_ffn_kernel(x_ref, y0_ref, y1_ref, y2_ref, y3_ref, wo_ref, g_ref, wg_ref, wu_ref, wd_ref,
                         o_ref, h_scr):
    @pl.when(pl.program_id(1) == 0)
    def _():
        x1 = _mixed_residual(x_ref, (y0_ref, y1_ref, y2_ref, y3_ref), wo_ref)
        h_scr[...] = _rms(x1, g_ref[...]).astype(BF16)
        o_ref[...] = x1

    o_ref[...] += _swiglu_chunk(h_scr[...], wg_ref, wu_ref, wd_ref)


def _out_proj_ffn(x, ys, w_out, g, wg, wu, wd, tm, tf):
    t, d = x.shape
    tm = min(tm, t)
    f = wg.shape[1]
    imap = lambda fn: (lambda i, j: fn(i))
    once = dict(pipeline_mode=pl.Buffered(1)) if tf == f else {}
    return pl.pallas_call(
        _out_proj_ffn_kernel,
        grid=(t // tm, f // tf),
        in_specs=_mix_in_specs(tm, d, imap)
                 + [pl.BlockSpec((1, d), lambda i, j: (0, 0)),
                    pl.BlockSpec((d, tf), lambda i, j: (0, j), **once),
                    pl.BlockSpec((d, tf), lambda i, j: (0, j), **once),
                    pl.BlockSpec((tf, d), lambda i, j: (j, 0), **once)],
        out_specs=pl.BlockSpec((tm, d), lambda i, j: (i, 0)),
        out_shape=jax.ShapeDtypeStruct((t, d), F32),
        scratch_shapes=[pltpu.VMEM((tm, d), BF16)],
        compiler_params=_cparams(("parallel", "arbitrary"), FFN_VMEM_LIMIT),
        name="out_proj_ffn",
    )(x, *ys, w_out, g, wg, wu, wd)


ROUTE_ID_LANES = (0, 1)
ROUTE_GATE_LANES = (2, 3)


def _split_bf16(x):
    hi = x.astype(BF16)
    return hi, (x - hi.astype(F32)).astype(BF16)


ROUTER_ROWS = 256


def _out_proj_router_kernel(x_ref, y0_ref, y1_ref, y2_ref, y3_ref, wo_ref, g_ref, wr_ref, x1_ref, c_ref):
    tm = x_ref.shape[0]
    step = min(ROUTER_ROWS, tm)
    for r in range(0, tm, step):
        _route_rows(slice(r, r + step), x_ref, (y0_ref, y1_ref, y2_ref, y3_ref), wo_ref, g_ref, wr_ref,
                    x1_ref, c_ref)


def _route_rows(rows, x_ref, y_refs, wo_ref, g_ref, wr_ref, x1_ref, c_ref):
    x1 = _mixed_residual(x_ref, y_refs, wo_ref, rows)
    x1_ref[rows, :] = x1
    h_hi, h_lo = _split_bf16(_rms(x1, g_ref[...]))
    logits = _dot(jnp.concatenate([h_hi, h_lo, h_hi], axis=1), wr_ref[...])
    lane = lax.broadcasted_iota(jnp.int32, logits.shape, 1).astype(F32)
    lg = jnp.where(lane < N_EXPERTS, logits, NEG)
    m1 = jnp.max(lg, axis=-1, keepdims=True)
    i1 = jnp.min(jnp.where(lg == m1, lane, 128.0), axis=-1, keepdims=True)
    lg2 = jnp.where(lane == i1, NEG, lg)
    m2 = jnp.max(lg2, axis=-1, keepdims=True)
    i2 = jnp.min(jnp.where(lg2 == m2, lane, 128.0), axis=-1, keepdims=True)
    ex = jnp.exp(m2 - m1)
    vals = (i1, i2, 1.0 / (1.0 + ex), ex / (1.0 + ex))
    out = jnp.zeros_like(logits)
    for ln, v in zip(ROUTE_ID_LANES + ROUTE_GATE_LANES, vals):
        out = jnp.where(lane == ln, v, out)
    c_ref[rows, :] = out


def _router_weights(wr):
    w = jnp.pad(wr.astype(F32), ((0, 0), (0, 128 - N_EXPERTS)))
    hi, lo = _split_bf16(w)
    return jnp.concatenate([hi, hi, lo], axis=0)


def _out_proj_router(x, ys, w_out, g, wr3, tm):
    t, d = x.shape
    tm = min(tm, t)
    imap = lambda fn: fn
    return pl.pallas_call(
        _out_proj_router_kernel,
        grid=(t // tm,),
        in_specs=_mix_in_specs(tm, d, imap)
                 + [pl.BlockSpec((1, d), lambda i: (0, 0)),
                    pl.BlockSpec((3 * d, 128), lambda i: (0, 0))],
        out_specs=[pl.BlockSpec((tm, d), lambda i: (i, 0)),
                   pl.BlockSpec((tm, 128), lambda i: (i, 0))],
        out_shape=[jax.ShapeDtypeStruct((t, d), F32), jax.ShapeDtypeStruct((t, 128), F32)],
        compiler_params=_cparams(("parallel",)),
        name="out_proj_router",
    )(x, *ys, w_out, g, wr3)


DMA_ISSUE_UNROLL = 8


def _row_copy(src, i, dst, j, sem):
    return pltpu.make_async_copy(src.at[pl.ds(i, 1)], dst.at[pl.ds(j, 1)], sem)


def _dispatch_kernel(meta_ref, pos_ref, *rest, td, tm, n_tiles, first_step):
    n_streams = len(first_step) - 1
    x_refs, (xs_hbm, zero_scr, sem) = rest[:n_streams], rest[n_streams:]
    step = pl.program_id(0)

    def zero_row(r):
        return _row_copy(zero_scr, 0, xs_hbm, r, sem)

    @pl.when(step == 0)
    def _():
        zero_scr[...] = jnp.zeros_like(zero_scr)
        n_used = meta_ref[2 * N_EXPERTS]
        tile_fills = [(i >= n_used, pltpu.make_async_copy(zero_scr, xs_hbm.at[pl.ds(i * tm, tm)], sem))
                      for i in range(n_tiles)]
        for cond, copy in tile_fills:
            pl.when(cond)(copy.start)
        for e in range(N_EXPERTS):
            lax.fori_loop(meta_ref[e], meta_ref[N_EXPERTS + e], lambda r, c: (zero_row(r).start(), c)[1], 0)
        for e in range(N_EXPERTS):
            lax.fori_loop(meta_ref[e], meta_ref[N_EXPERTS + e], lambda r, c: (zero_row(r).wait(), c)[1], 0)
        for cond, copy in tile_fills:
            pl.when(cond)(copy.wait)

    def scatter(x_ref):
        def issue(j, c):
            for k in range(2):
                _row_copy(x_ref, j, xs_hbm, pos_ref[0, 0, 2 * j + k], sem).start()
            return c

        lax.fori_loop(0, td, issue, 0, unroll=DMA_ISSUE_UNROLL)
        for _ in range(2):
            pltpu.make_async_copy(x_ref, xs_hbm.at[pl.ds(0, td)], sem).wait()

    for s, x_ref in enumerate(x_refs):
        pl.when((step >= first_step[s]) & (step < first_step[s + 1]))(functools.partial(scatter, x_ref))


def _dispatch(xs_list, pos, meta, n_rows, tm, td):
    d = xs_list[0].shape[1]
    td = min([td] + [x.shape[0] for x in xs_list])
    first_step = [0]
    for x in xs_list:
        first_step.append(first_step[-1] + x.shape[0] // td)
    n_steps = first_step[-1]
    pos3 = pos.reshape(n_steps, 1, 2 * td)

    def tile_map(s):
        lo, hi = first_step[s], first_step[s + 1]
        return lambda i, m: (jnp.clip(i, lo, hi - 1) - lo, 0)

    in_specs = [pl.BlockSpec((1, 1, 2 * td), lambda i, m: (i, 0, 0), memory_space=pltpu.SMEM)]
    in_specs += [pl.BlockSpec((td, d), tile_map(s)) for s in range(len(xs_list))]
    return pl.pallas_call(
        functools.partial(_dispatch_kernel, td=td, tm=tm, n_tiles=n_rows // tm, first_step=tuple(first_step)),
        grid_spec=pltpu.PrefetchScalarGridSpec(
            num_scalar_prefetch=1, grid=(n_steps,), in_specs=in_specs,
            out_specs=pl.BlockSpec(memory_space=pl.ANY),
            scratch_shapes=[pltpu.VMEM((tm, d), F32), pltpu.SemaphoreType.DMA]),
        out_shape=jax.ShapeDtypeStruct((n_rows, d), F32),
        compiler_params=_cparams(("arbitrary",)),
        name="moe_dispatch",
    )(meta, pos3, *xs_list)


def _grouped_ffn_kernel(te_ref, nu_ref, x_ref, g_ref, wg_ref, wu_ref, wd_ref, o_ref, h_scr):
    del te_ref
    j = pl.program_id(1)
    used = pl.program_id(0) < nu_ref[0]

    @pl.when(jnp.logical_not(used) & (j == 0))
    def _():
        o_ref[...] = jnp.zeros_like(o_ref)

    @pl.when(used)
    def _():
        @pl.when(j == 0)
        def _():
            h_scr[...] = _rms(x_ref[...], g_ref[...]).astype(BF16)

        y = _swiglu_chunk(h_scr[...], wg_ref.at[0], wu_ref.at[0], wd_ref.at[0])

        @pl.when(j == 0)
        def _():
            o_ref[...] = y

        @pl.when(j > 0)
        def _():
            o_ref[...] += y


def _grouped_ffn(xs, g, tile_expert, n_used, wg, wu, wd, tm, tf):
    r, d = xs.shape
    f = wg.shape[2]
    nj = f // tf

    def row_map(i, j, te, nu):
        return (i, 0)

    def col_of(i, j, nu):
        return jnp.where(i < nu[0], j, nj - 1)

    grid_spec = pltpu.PrefetchScalarGridSpec(
        num_scalar_prefetch=2,
        grid=(r // tm, nj),
        in_specs=[pl.BlockSpec((tm, d), row_map),
                  pl.BlockSpec((1, d), lambda i, j, te, nu: (0, 0)),
                  pl.BlockSpec((1, d, tf), lambda i, j, te, nu: (te[i], 0, col_of(i, j, nu))),
                  pl.BlockSpec((1, d, tf), lambda i, j, te, nu: (te[i], 0, col_of(i, j, nu))),
                  pl.BlockSpec((1, tf, d), lambda i, j, te, nu: (te[i], col_of(i, j, nu), 0))],
        out_specs=pl.BlockSpec((tm, d), row_map),
        scratch_shapes=[pltpu.VMEM((tm, d), BF16)],
    )
    return pl.pallas_call(
        _grouped_ffn_kernel,
        grid_spec=grid_spec,
        out_shape=jax.ShapeDtypeStruct((r, d), F32),
        compiler_params=_cparams(("arbitrary", "arbitrary"), FFN_VMEM_LIMIT),
        name="moe_grouped_ffn",
    )(tile_expert, n_used, xs, g, wg, wu, wd)


def _combine_kernel(pos_ref, pos_next_ref, x_ref, route_ref, ys_hbm, o_ref, buf0, buf1, sems, *, tc):
    step = pl.program_id(0)
    slot = step % 2

    def gather(p_ref, s):
        def issue(j, c):
            _row_copy(ys_hbm, p_ref[0, 0, 2 * j], buf0.at[s], j, sems.at[s]).start()
            _row_copy(ys_hbm, p_ref[0, 0, 2 * j + 1], buf1.at[s], j, sems.at[s]).start()
            return c

        lax.fori_loop(0, tc, issue, 0, unroll=DMA_ISSUE_UNROLL)

    pl.when(step == 0)(functools.partial(gather, pos_ref, 0))
    pl.when(step + 1 < pl.num_programs(0))(functools.partial(gather, pos_next_ref, 1 - slot))
    for buf in (buf0, buf1):
        pltpu.make_async_copy(ys_hbm.at[pl.ds(0, tc)], buf.at[slot], sems.at[slot]).wait()
    route = route_ref[...]
    g0 = route[:, ROUTE_GATE_LANES[0]:ROUTE_GATE_LANES[0] + 1]
    g1 = route[:, ROUTE_GATE_LANES[1]:ROUTE_GATE_LANES[1] + 1]
    o_ref[...] = x_ref[...] + g0 * buf0[slot] + g1 * buf1[slot]


def _combine(x, route, pos, ys, tc):
    t, d = x.shape
    tc = min(tc, t)
    n = t // tc
    pos3 = pos.reshape(n, 1, 2 * tc)
    pos_block = (1, 1, 2 * tc)
    return pl.pallas_call(
        functools.partial(_combine_kernel, tc=tc),
        grid=(n,),
        in_specs=[pl.BlockSpec(pos_block, lambda i: (i, 0, 0), memory_space=pltpu.SMEM),
                  pl.BlockSpec(pos_block, lambda i: (jnp.minimum(i + 1, n - 1), 0, 0), memory_space=pltpu.SMEM),
                  pl.BlockSpec((tc, d), lambda i: (i, 0)),
                  pl.BlockSpec((tc, 128), lambda i: (i, 0)),
                  pl.BlockSpec(memory_space=pl.ANY)],
        out_specs=pl.BlockSpec((tc, d), lambda i: (i, 0)),
        out_shape=jax.ShapeDtypeStruct((t, d), F32),
        scratch_shapes=[pltpu.VMEM((2, tc, d), F32), pltpu.VMEM((2, tc, d), F32),
                        pltpu.SemaphoreType.DMA((2,))],
        compiler_params=_cparams(("arbitrary",)),
        name="moe_combine",
    )(pos3, pos3, x, route, ys)


MOE_TM = 512


def _route_plan(expert_ids, tm):
    flat = expert_ids.reshape(-1)
    a = flat.shape[0]
    onehot = (flat[None, :] == jnp.arange(N_EXPERTS, dtype=jnp.int32)[:, None]).astype(jnp.int32)
    csum = jnp.cumsum(onehot, axis=1)
    counts = csum[:, -1]
    padded = (counts + tm - 1) // tm * tm
    ends = jnp.cumsum(padded)
    offs = ends - padded
    pos = jnp.sum(onehot * (offs[:, None] + csum - 1), axis=0)
    n_tiles = (a + N_EXPERTS * tm) // tm
    tile_start = jnp.arange(n_tiles, dtype=jnp.int32) * tm
    tile_expert = jnp.minimum(jnp.sum(tile_start[:, None] >= ends[None, :], axis=1), N_EXPERTS - 1)
    n_used = (ends[-1] // tm).reshape(1)
    last = jnp.take(tile_expert, n_used[0] - 1)
    tile_expert = jnp.where(tile_start < ends[-1], tile_expert, last)
    meta = jnp.concatenate([offs + counts, ends, n_used]).astype(jnp.int32)
    return pos.astype(jnp.int32), tile_expert.astype(jnp.int32), n_used.astype(jnp.int32), meta, n_tiles * tm


def _moe(xs_list, routes, g, wg, wu, wd):
    ids = jnp.concatenate([r[:, ROUTE_ID_LANES[0]:ROUTE_ID_LANES[1] + 1] for r in routes]).astype(jnp.int32)
    pos, tile_expert, n_used, meta, n_rows = _route_plan(ids, MOE_TM)
    bounds = np.cumsum([0] + [2 * x.shape[0] for x in xs_list])
    pos_list = [pos[lo:hi] for lo, hi in zip(bounds[:-1], bounds[1:])]
    xs = _dispatch(xs_list, pos, meta, n_rows, MOE_TM, 1024)
    ys = _grouped_ffn(xs, g, tile_expert, n_used, wg, wu, wd, MOE_TM, D_FF)
    return [_combine(x, r, ps, ys, 1024) for x, r, ps in zip(xs_list, routes, pos_list)]


def _pool_kernel(u_ref, halo_ref, inv_ref, w_ref, scale_ref, o_ref, *, tiles_per_seq):
    ti = pl.program_id(0) % tiles_per_seq
    u = u_ref[...]
    halo = jnp.where(ti == 0, 0.0, halo_ref[...])
    ext = jnp.concatenate([halo, u], axis=0)
    half = POOL_WIDTH // 2
    short = lax.broadcasted_iota(jnp.int32, (ext.shape[0], half), 1) < half // 2

    def window_sums(x, n_doublings):
        sums = []
        for k in range(n_doublings):
            x = x + pltpu.roll(x, 2 ** k, 0)
            sums.append(x)
        return jnp.where(short, sums[-2], sums[-1])[POOL_HALO:]

    s = jnp.concatenate([window_sums(ext[:, :half], 2), window_sums(ext[:, half:], 4)], axis=1)
    inv_rest = inv_ref[1]
    inv_head = inv_ref[jnp.minimum(ti, 1)]
    pooled = jnp.concatenate([s[:POOL_HALO] * inv_head, s[POOL_HALO:] * inv_rest[:1]], axis=0) - u
    o_ref[...] = (_dot(pooled.astype(BF16), w_ref[...]) * scale_ref[...]).astype(o_ref.dtype)


def _pool(proj2d, col, w, scale, *, n_rows, tb, tiles_per_seq, pos0):
    per = tb // POOL_HALO
    win = np.repeat(np.asarray(POOL_WINDOWS), POOL_WIDTH // len(POOL_WINDOWS))[None, :]
    count = np.minimum(win, pos0 + np.arange(POOL_HALO)[:, None] + 1)
    inv = jnp.asarray(np.stack([1.0 / count, np.broadcast_to(1.0 / win, count.shape)]), F32)
    return pl.pallas_call(
        functools.partial(_pool_kernel, tiles_per_seq=tiles_per_seq),
        grid=(n_rows // tb,),
        in_specs=[pl.BlockSpec((tb, POOL_WIDTH), lambda i: (i, col)),
                  pl.BlockSpec((POOL_HALO, POOL_WIDTH), lambda i: (jnp.maximum(i * per - 1, 0), col)),
                  pl.BlockSpec((2, POOL_HALO, POOL_WIDTH), lambda i: (0, 0, 0)),
                  pl.BlockSpec((POOL_WIDTH, POOL_WIDTH), lambda i: (0, 0)),
                  pl.BlockSpec((1, POOL_WIDTH), lambda i: (0, 0))],
        out_specs=pl.BlockSpec((tb, POOL_WIDTH), lambda i: (i, 0)),
        out_shape=jax.ShapeDtypeStruct((n_rows, POOL_WIDTH), BF16),
        compiler_params=_cparams(("parallel",)),
        name="pool",
    )(proj2d, proj2d, inv, w, scale)


def _t5_bucket_np(rel):
    n = np.maximum(rel, 0)
    max_exact = T5_BUCKETS // 2
    nf = np.maximum(n, max_exact).astype(np.float32)
    large = max_exact + (np.log(nf / max_exact) / math.log(T5_MAX_DIST / max_exact)
                         * (T5_BUCKETS - max_exact)).astype(np.int32)
    large = np.minimum(large, T5_BUCKETS - 1)
    return np.where(n < max_exact, n, large)


def _swa_bias(rel_bias, rel):
    valid = (rel >= 0) & (rel < SWA_WINDOW)
    onehot = jnp.asarray(_t5_bucket_np(rel)[..., None] == np.arange(T5_BUCKETS), F32)
    b = jnp.einsum('qsb,bh->hqs', onehot, rel_bias.astype(F32), precision=lax.Precision.HIGHEST)
    return jnp.where(valid[None], b, NEG)


def _softmax_parts(parts, sink):
    m = sink
    for s in parts:
        m = jnp.maximum(m, jnp.max(s, axis=-1, keepdims=True))
    ps = [jnp.exp(s - m) for s in parts]
    denom = jnp.exp(sink - m)
    for p in ps:
        denom = denom + jnp.sum(p, axis=-1, keepdims=True)
    return ps, denom


def _head_mean_matrix(width):
    h = np.arange(width) // HEAD_DIM
    return jnp.asarray((h[:, None] == h[None, :]) / HEAD_DIM, BF16)


def _head_rms(x, mean_mat, g):
    ms = _dot((x * x).astype(BF16), mean_mat)
    return x * lax.rsqrt(ms + RMS_EPS) * g


SWA_TILE = 512


def _swa_prompt_kernel(sinks_ref, q_ref, kv_ref, halo_ref, qg_ref, kg_ref, mq_ref, mk_ref, bias_ref,
                       y_ref, kn_ref):
    has_prev = pl.program_id(1) > 0
    kw = SWA_KV_HEADS * HEAD_DIM
    blk = SWA_BLOCK
    kv = kv_ref[0]
    halo = halo_ref[0]
    k_ext = jnp.concatenate([halo[:, :kw], kv[:, :kw]], axis=0)
    v_ext = jnp.concatenate([halo[:, kw:], kv[:, kw:]], axis=0).astype(BF16)
    kn = _head_rms(k_ext, mk_ref[...], kg_ref[...])
    kn_ref[0] = kn[SWA_TILE:]
    knb = kn.astype(BF16)
    qn = (_head_rms(q_ref[0], mq_ref[...], qg_ref[...]) * (HEAD_DIM ** -0.5)).astype(BF16)
    row = lax.broadcasted_iota(jnp.int32, (2 * blk, 1), 0)
    col = lax.broadcasted_iota(jnp.int32, (2 * blk, 2 * blk), 1)
    for c in range(SWA_TILE // blk):
        rows = slice(c * blk, (c + 1) * blk)
        keys = slice(c * blk, (c + 2) * blk)
        for kh in range(SWA_KV_HEADS):
            ksl = slice(kh * HEAD_DIM, (kh + 1) * HEAD_DIM)
            h0 = 2 * kh
            q2 = jnp.concatenate([qn[rows, h0 * HEAD_DIM:(h0 + 1) * HEAD_DIM],
                                  qn[rows, (h0 + 1) * HEAD_DIM:(h0 + 2) * HEAD_DIM]], axis=0)
            s = _dot_nt(q2, knb[keys, ksl]) + bias_ref[kh]
            if c == 0:
                s = jnp.where(has_prev | (col >= blk), s, NEG)
            sink = jnp.where(row < blk, sinks_ref[h0], sinks_ref[h0 + 1])
            (p,), denom = _softmax_parts((s,), sink)
            o = _dot(p.astype(BF16), v_ext[keys, ksl]) / denom
            y_ref[0, rows, h0 * HEAD_DIM:(h0 + 1) * HEAD_DIM] = o[:blk].astype(y_ref.dtype)
            y_ref[0, rows, (h0 + 1) * HEAD_DIM:(h0 + 2) * HEAD_DIM] = o[blk:].astype(y_ref.dtype)


def _swa_prompt(proj3d, sinks, qg, kg, bias):
    b, l, _ = proj3d.shape
    kw = SWA_KV_HEADS * HEAD_DIM
    per = SWA_TILE // SWA_BLOCK
    tile = (1, SWA_TILE, MIX_BLOCK)
    const2 = lambda bi, i: (0, 0)
    bias2 = bias.reshape(SWA_KV_HEADS, 2 * SWA_BLOCK, 2 * SWA_BLOCK)
    return pl.pallas_call(
        _swa_prompt_kernel,
        grid=(b, l // SWA_TILE),
        in_specs=[pl.BlockSpec(memory_space=pltpu.SMEM),
                  pl.BlockSpec(tile, lambda bi, i: (bi, i, COL_SWA_Q)),
                  pl.BlockSpec(tile, lambda bi, i: (bi, i, COL_SWA_KV)),
                  pl.BlockSpec((1, SWA_BLOCK, MIX_BLOCK),
                               lambda bi, i: (bi, jnp.maximum(i * per - 1, 0), COL_SWA_KV)),
                  pl.BlockSpec((1, MIX_BLOCK), const2),
                  pl.BlockSpec((1, kw), const2),
                  pl.BlockSpec((MIX_BLOCK, MIX_BLOCK), const2),
                  pl.BlockSpec((kw, kw), const2),
                  pl.BlockSpec((SWA_KV_HEADS, 2 * SWA_BLOCK, 2 * SWA_BLOCK), lambda bi, i: (0, 0, 0))],
        out_specs=[pl.BlockSpec(tile, lambda bi, i: (bi, i, 0)),
                   pl.BlockSpec((1, SWA_BLOCK, kw), lambda bi, i: (bi, 0, 0))],
        out_shape=[jax.ShapeDtypeStruct((b, l, MIX_BLOCK), BF16),
                   jax.ShapeDtypeStruct((b, SWA_BLOCK, kw), F32)],
        compiler_params=_cparams(("parallel", "arbitrary")),
        name="swa_prompt",
    )(sinks, proj3d, proj3d, proj3d, jnp.tile(qg, (1, SWA_HEADS)), jnp.tile(kg, (1, SWA_KV_HEADS)),
      _head_mean_matrix(MIX_BLOCK), _head_mean_matrix(kw), bias2)


def _layer_state_out(layer, depth, blk, shape, dtype, prev):
    zeros = (0,) * (len(blk) - 1)
    shape = jax.ShapeDtypeStruct((depth,) + shape, dtype)
    if prev is None:
        assert layer == 0
        return pl.BlockSpec((depth,) + blk, lambda i: (0, i) + zeros), shape, [], []
    spec = pl.BlockSpec((1,) + blk, lambda i: (layer, i) + zeros)
    return spec, shape, [prev], [pl.BlockSpec(memory_space=pl.ANY)]


def _zero_other_layers(ref):
    if ref.shape[0] > 1:
        ref[1:] = jnp.zeros((ref.shape[0] - 1,) + ref.shape[1:], ref.dtype)


def _swa_sample_kernel(sinks_ref, q_ref, kv_ref, ckt_ref, cvt_ref, qg_ref, kg_ref, mq_ref, mk_ref, bias_ref,
                       *rest, n_new):
    y_ref, nkt_ref, nvt_ref = rest[-3:]
    _, s_blk, kw, wb = ckt_ref.shape
    keep = wb - n_new
    q2 = _head_rms(q_ref[...], mq_ref[...], qg_ref[...]) * (HEAD_DIM ** -0.5)
    kv = kv_ref[...]
    kn2 = _head_rms(kv[:, :kw], mk_ref[...], kg_ref[...])
    v2 = kv[:, kw:]
    q3 = q2.reshape(s_blk, n_new, MIX_BLOCK)
    kn3 = kn2.reshape(s_blk, n_new, kw)
    v3 = v2.reshape(s_blk, n_new, kw)
    ckt = ckt_ref[0]
    cvt = cvt_ref[0]
    bdot = functools.partial(jnp.einsum, preferred_element_type=F32)
    for h in range(SWA_HEADS):
        kh = h // (SWA_HEADS // SWA_KV_HEADS)
        ksl = slice(kh * HEAD_DIM, (kh + 1) * HEAD_DIM)
        hsl = slice(h * HEAD_DIM, (h + 1) * HEAD_DIM)
        qh = q3[:, :, hsl]
        s_c = bdot('sqd,sdw->sqw', qh, ckt[:, ksl, :]) + bias_ref[h, :, :wb]
        s_n = bdot('sqd,sjd->sqj', qh, kn3[:, :, ksl]) + bias_ref[h, :, wb:]
        (p_c, p_n), denom = _softmax_parts((s_c, s_n), sinks_ref[h])
        o = bdot('sqw,sdw->sqd', p_c, cvt[:, ksl, :]) + bdot('sqj,sjd->sqd', p_n, v3[:, :, ksl])
        y_ref[:, hsl] = (o / denom).reshape(s_blk * n_new, HEAD_DIM).astype(y_ref.dtype)
    lane = lax.broadcasted_iota(jnp.int32, (kw, wb), 1)
    for new2, old, out_ref in ((kn2, ckt, nkt_ref), (v2, cvt, nvt_ref)):
        new_t = new2.T
        shifted = pltpu.roll(old.reshape(s_blk * kw, wb), keep, 1).reshape(s_blk, kw, wb)
        _zero_other_layers(out_ref)
        for s in range(s_blk):
            cols = pltpu.roll(new_t, (keep - s * n_new) % wb, 1)
            out_ref[0, s] = jnp.where(lane >= keep, cols, shifted[s])


def _swa_sample(proj2d, cache_kt, cache_vt, layer, prev_new, sinks, qg, kg, bias, n_new, s_blk):
    depth, nseq, kw, wb = cache_kt.shape
    assert s_blk * n_new == wb
    rows = s_blk * n_new
    blk = (rows, MIX_BLOCK)
    cblk = (s_blk, kw, wb)
    cin = pl.BlockSpec((1,) + cblk, lambda i: (layer, i, 0, 0))
    const2 = lambda i: (0, 0)
    prev_k, prev_v = prev_new if prev_new is not None else (None, None)
    kspec, kshape, kin, kin_specs = _layer_state_out(layer, depth, cblk, (nseq, kw, wb), F32, prev_k)
    vspec, vshape, vin, vin_specs = _layer_state_out(layer, depth, cblk, (nseq, kw, wb), F32, prev_v)
    n_main = 10
    aliases = {n_main + j: 1 + j for j in range(len(kin + vin))}
    return pl.pallas_call(
        functools.partial(_swa_sample_kernel, n_new=n_new),
        grid=(nseq // s_blk,),
        in_specs=[pl.BlockSpec(memory_space=pltpu.SMEM),
                  pl.BlockSpec(blk, lambda i: (i, COL_SWA_Q)),
                  pl.BlockSpec(blk, lambda i: (i, COL_SWA_KV)),
                  cin, cin,
                  pl.BlockSpec((1, MIX_BLOCK), const2),
                  pl.BlockSpec((1, kw), const2),
                  pl.BlockSpec((MIX_BLOCK, MIX_BLOCK), const2),
                  pl.BlockSpec((kw, kw), const2),
                  pl.BlockSpec((SWA_HEADS, n_new, wb + n_new), lambda i: (0, 0, 0))] + kin_specs + vin_specs,
        out_specs=[pl.BlockSpec(blk, lambda i: (i, 0)), kspec, vspec],
        out_shape=[jax.ShapeDtypeStruct((nseq * n_new, MIX_BLOCK), BF16), kshape, vshape],
        input_output_aliases=aliases,
        compiler_params=_cparams(("parallel",)),
        name="swa_sample",
    )(sinks, proj2d, proj2d, cache_kt, cache_vt, jnp.tile(qg, (1, SWA_HEADS)), jnp.tile(kg, (1, SWA_KV_HEADS)),
      _head_mean_matrix(MIX_BLOCK), _head_mean_matrix(kw), bias, *kin, *vin)


def _ssm_kernel(u_ref, h0_ref, wb_ref, tab_ref, wc_ref, d_ref, wglu_ref, y_ref, hn_ref,
                bu_scr, carry_scr, *, chained, tiles_per_seq, tb):
    n = SSM_N

    def project_in(rows):
        bu_scr[rows, :] = _dot(u_ref[rows, :].astype(BF16), wb_ref[...])

    def project_out(rows):
        y = _dot(bu_scr[rows, :].astype(BF16), wc_ref[...]) + d_ref[...] * u_ref[rows, :]
        y = 0.5 * y * (1.0 + jnp.tanh(math.sqrt(2.0 / math.pi) * (y + 0.044715 * (y * y * y))))
        return (y * _sigmoid(_dot(y.astype(BF16), wglu_ref[...]))).astype(y_ref.dtype)

    def tile_scan(r0, cr, ci):
        hr = bu_scr[pl.ds(r0, SCAN_ROWS), :n]
        hi = bu_scr[pl.ds(r0, SCAN_ROWS), n:]
        for k, shift in enumerate((1, 2, 4)):
            ar, ai = tab_ref[2 * k], tab_ref[2 * k + 1]
            sr, si = pltpu.roll(hr, shift, 0), pltpu.roll(hi, shift, 0)
            hr, hi = hr + ar * sr - ai * si, hi + ar * si + ai * sr
        pr, pi = tab_ref[6], tab_ref[7]
        hr, hi = hr + pr * cr - pi * ci, hi + pr * ci + pi * cr
        bu_scr[pl.ds(r0, SCAN_ROWS), :n] = hr
        bu_scr[pl.ds(r0, SCAN_ROWS), n:] = hi
        return hr[SCAN_ROWS - 1:], hi[SCAN_ROWS - 1:]

    if chained:
        @pl.when(pl.program_id(1) % tiles_per_seq == 0)
        def _():
            carry_scr[...] = jnp.zeros_like(carry_scr)

        project_in(slice(None))
        cr, ci = carry_scr[:, :n], carry_scr[:, n:]
        for r0 in range(0, tb, SCAN_ROWS):
            lr, li = tile_scan(r0, cr, ci)
            cr, ci = jnp.broadcast_to(lr, (SCAN_ROWS, n)), jnp.broadcast_to(li, (SCAN_ROWS, n))
        y_ref[0] = project_out(slice(None))
        carry_scr[:, :n] = cr
        carry_scr[:, n:] = ci
        hn_ref[0, :, :n] = cr
        hn_ref[0, :, n:] = ci
    else:
        project_in(slice(None))

        def body(t, _):
            r0 = pl.multiple_of(t * SCAN_ROWS, SCAN_ROWS)
            h0 = h0_ref[pl.ds(t, 1), :]
            cr = jnp.broadcast_to(h0[:, :n], (SCAN_ROWS, n))
            ci = jnp.broadcast_to(h0[:, n:], (SCAN_ROWS, n))
            lr, li = tile_scan(r0, cr, ci)
            hn_ref[pl.ds(t, 1), :n] = lr
            hn_ref[pl.ds(t, 1), n:] = li
            return 0

        lax.fori_loop(0, tb // SCAN_ROWS, body, 0)
        y_ref[...] = project_out(slice(None))


def _ssm_common_specs(zero_map2, zero_map3):
    return [pl.BlockSpec((SSM_WIDTH, 2 * SSM_N), zero_map2),
            pl.BlockSpec((8, SCAN_ROWS, SSM_N), zero_map3),
            pl.BlockSpec((2 * SSM_N, SSM_WIDTH), zero_map2),
            pl.BlockSpec((1, SSM_WIDTH), zero_map2),
            pl.BlockSpec((SSM_WIDTH, SSM_WIDTH), zero_map2)]


def _ssm_prompt(proj3d, sp, tb):
    b, l, _ = proj3d.shape
    nt = l // tb
    dummy_h0 = jnp.zeros((SCAN_ROWS, 2 * SSM_N), F32)
    kern = functools.partial(_ssm_kernel, chained=True, tiles_per_seq=nt, tb=tb)

    def kernel(u_ref, h0_ref, wb, tab, wc, d, wglu, y_ref, hn_ref, bu_scr, carry_scr):
        kern(u_ref.at[0], h0_ref, wb, tab, wc, d, wglu, y_ref, hn_ref, bu_scr, carry_scr)

    return pl.pallas_call(
        kernel,
        grid=(b, nt),
        in_specs=[pl.BlockSpec((1, tb, SSM_WIDTH), lambda bi, i: (bi, i, COL_SSM)),
                  pl.BlockSpec((SCAN_ROWS, 2 * SSM_N), lambda bi, i: (0, 0))]
                 + _ssm_common_specs(lambda bi, i: (0, 0), lambda bi, i: (0, 0, 0)),
        out_specs=[pl.BlockSpec((1, tb, SSM_WIDTH), lambda bi, i: (bi, i, 0)),
                   pl.BlockSpec((1, SCAN_ROWS, 2 * SSM_N), lambda bi, i: (bi, 0, 0))],
        out_shape=[jax.ShapeDtypeStruct((b, l, SSM_WIDTH), BF16),
                   jax.ShapeDtypeStruct((b, SCAN_ROWS, 2 * SSM_N), F32)],
        scratch_shapes=[pltpu.VMEM((tb, 2 * SSM_N), F32), pltpu.VMEM((SCAN_ROWS, 2 * SSM_N), F32)],
        compiler_params=_cparams(("parallel", "arbitrary")),
        name="ssm_prompt",
    )(proj3d, dummy_h0, sp["wb"], sp["tab"], sp["wc"], sp["d"], sp["wglu"])


def _ssm_sample(proj2d, h0, sp):
    rows = proj2d.shape[0]
    nseq = h0.shape[0]
    kern = functools.partial(_ssm_kernel, chained=False, tiles_per_seq=1, tb=rows)
    return pl.pallas_call(
        kern,
        grid=(1,),
        in_specs=[pl.BlockSpec((rows, SSM_WIDTH), lambda i: (0, COL_SSM)),
                  pl.BlockSpec((nseq, 2 * SSM_N), lambda i: (0, 0))]
                 + _ssm_common_specs(lambda i: (0, 0), lambda i: (0, 0, 0)),
        out_specs=[pl.BlockSpec((rows, SSM_WIDTH), lambda i: (0, 0)),
                   pl.BlockSpec((nseq, 2 * SSM_N), lambda i: (0, 0))],
        out_shape=[jax.ShapeDtypeStruct((rows, SSM_WIDTH), BF16),
                   jax.ShapeDtypeStruct((nseq, 2 * SSM_N), F32)],
        scratch_shapes=[pltpu.VMEM((rows, 2 * SSM_N), F32), pltpu.VMEM((SCAN_ROWS, 2 * SSM_N), F32)],
        compiler_params=_cparams(("arbitrary",)),
        name="ssm_sample",
    )(proj2d, h0, sp["wb"], sp["tab"], sp["wc"], sp["d"], sp["wglu"])


def _ssm_params(lam_re, lam_im, log_dt, b_re, b_im, c_re, c_im, d_skip, w_glu):
    lr, li = lam_re.astype(F32), lam_im.astype(F32)
    dt = jnp.exp(log_dt.astype(F32))[:, None]
    mag = jnp.exp(lr * dt)
    ab_re, ab_im = mag * jnp.cos(li * dt), mag * jnp.sin(li * dt)
    den = lr * lr + li * li
    nr = ab_re - 1.0
    f_re = (nr * lr + ab_im * li) / den
    f_im = (ab_im * lr - nr * li) / den
    br, bi = b_re.astype(F32), b_im.astype(F32)
    bb_re = f_re[..., None] * br - f_im[..., None] * bi
    bb_im = f_re[..., None] * bi + f_im[..., None] * br
    eye = jnp.eye(SSM_GROUPS, dtype=F32)

    def in_mat(bb):
        return jnp.einsum('gpc,gh->gchp', bb, eye).reshape(SSM_WIDTH, SSM_N)

    def out_mat(c):
        return jnp.einsum('gcp,gh->gphc', c.astype(F32), eye).reshape(SSM_N, SSM_WIDTH)

    wb = jnp.concatenate([in_mat(bb_re), in_mat(bb_im)], axis=1).astype(BF16)
    wc = jnp.concatenate([out_mat(c_re), -out_mat(c_im)], axis=0).astype(BF16)

    ar, ai = ab_re.reshape(1, SSM_N), ab_im.reshape(1, SSM_N)

    def cmul(x, y):
        return (x[0] * y[0] - x[1] * y[1], x[0] * y[1] + x[1] * y[0])

    pw = [(ar, ai)]
    for _ in range(SCAN_ROWS - 1):
        pw.append(cmul(pw[-1], (ar, ai)))
    row = jnp.arange(SCAN_ROWS)[:, None]
    tabs = []
    for shift in (1, 2, 4):
        for part in pw[shift - 1]:
            tabs.append(jnp.where(row >= shift, part, 0.0))
    tabs.append(jnp.concatenate([p[0] for p in pw], axis=0))
    tabs.append(jnp.concatenate([p[1] for p in pw], axis=0))
    tab = jnp.stack([jnp.broadcast_to(t, (SCAN_ROWS, SSM_N)) for t in tabs])
    return dict(wb=wb, tab=tab, wc=wc, d=d_skip.astype(F32).reshape(1, SSM_WIDTH), wglu=w_glu.astype(BF16))


_RET_G = 1.0 - np.exp2(-5.0 - np.arange(RET_HEADS, dtype=np.float64))


def _ret_consts(chunk, n_rows):
    idx = np.arange(n_rows)
    loc = idx % chunk
    same = (idx[:, None] // chunk) == (idx[None, :] // chunk)
    diff = loc[:, None] - loc[None, :]
    dec = np.where(same & (diff >= 0), _RET_G[:, None, None] ** np.maximum(diff, 0)[None], 0.0)
    qdec = np.repeat((_RET_G[None, :] ** (loc[:, None] + 1.0)), HEAD_DIM, axis=1)
    kdec = np.repeat((_RET_G[None, :] ** (chunk - 1.0 - loc[:, None])), HEAD_DIM, axis=1)
    return (jnp.asarray(dec, F32), jnp.asarray(qdec, F32), jnp.asarray(kdec, F32),
            jnp.asarray(_RET_G ** chunk, F32))


def _rope_tables(pos):
    half = HEAD_DIM // 2
    theta = 1.0 / (ROPE_BASE ** np.linspace(0.0, 1.0, half))
    ang = np.asarray(pos, np.float64)[:, None] * theta[None, :]
    cos = np.repeat(np.cos(ang), 2, axis=1)
    sin = np.repeat(np.sin(ang), 2, axis=1) * np.tile([-1.0, 1.0], half)[None]
    return (jnp.asarray(np.tile(cos, (1, RET_HEADS)), F32), jnp.asarray(np.tile(sin, (1, RET_HEADS)), F32))


def _rotate_pairs(x, cos, sin_signed):
    lane = lax.broadcasted_iota(jnp.int32, x.shape, 1)
    nxt = pltpu.roll(x, x.shape[1] - 1, 1)
    prv = pltpu.roll(x, 1, 1)
    return x * cos + jnp.where(lane % 2 == 0, nxt, prv) * sin_signed


def _ret_head_out(o, gate, norm):
    ms = jnp.mean(o * o, axis=-1, keepdims=True)
    return o * lax.rsqrt(ms + RMS_EPS) * norm * (gate * _sigmoid(gate))


RET_TILE = 512


def _ret_prompt_kernel(gc_ref, q_ref, k_ref, v_ref, g_ref, cos_ref, sin_ref, dec_ref, qdec_ref,
                       kdec_ref, norm_ref, mh_ref, y_ref, r_ref, o_scr):
    @pl.when(pl.program_id(1) == 0)
    def _():
        r_ref[...] = jnp.zeros_like(r_ref)

    cos, sin = cos_ref[...], sin_ref[...]
    q = _rotate_pairs(q_ref[0], cos, sin)
    k = _rotate_pairs(k_ref[0], cos, sin) * (HEAD_DIM ** -0.5)
    qb, kb, vb = q.astype(BF16), k.astype(BF16), v_ref[0].astype(BF16)
    kdb = (k * kdec_ref[...]).astype(BF16)
    for c in range(RET_TILE // RET_CHUNK):
        rows = slice(c * RET_CHUNK, (c + 1) * RET_CHUNK)
        for h in range(RET_HEADS):
            sl = slice(h * HEAD_DIM, (h + 1) * HEAD_DIM)
            qh, vh = qb[rows, sl], vb[rows, sl]
            s = _dot_nt(qh, kb[rows, sl]) * dec_ref[h]
            r = r_ref[0, h]
            o_scr[rows, sl] = _dot(s.astype(BF16), vh) + _dot(qh, r.astype(BF16)) * qdec_ref[rows, sl]
            r_ref[0, h] = gc_ref[h] * r + _dot_tn(kdb[rows, sl], vh)
    g = g_ref[0]
    y_ref[0] = (_head_rms(o_scr[...], mh_ref[...], norm_ref[...]) * (g * _sigmoid(g))).astype(y_ref.dtype)


def _ret_prompt(proj3d, cos, sin, norm):
    b, l, _ = proj3d.shape
    c = RET_CHUNK
    dec, _, _, gc = _ret_consts(c, c)
    _, qdec, kdec, _ = _ret_consts(c, RET_TILE)
    blk = (1, RET_TILE, MIX_BLOCK)
    tspec = pl.BlockSpec((RET_TILE, MIX_BLOCK), lambda bi, i: (i, 0))
    cspec = pl.BlockSpec((RET_TILE, MIX_BLOCK), lambda bi, i: (0, 0))
    return pl.pallas_call(
        _ret_prompt_kernel,
        grid=(b, l // RET_TILE),
        in_specs=[pl.BlockSpec(memory_space=pltpu.SMEM),
                  pl.BlockSpec(blk, lambda bi, i: (bi, i, COL_RET_Q)),
                  pl.BlockSpec(blk, lambda bi, i: (bi, i, COL_RET_K)),
                  pl.BlockSpec(blk, lambda bi, i: (bi, i, COL_RET_V)),
                  pl.BlockSpec(blk, lambda bi, i: (bi, i, COL_RET_G)),
                  tspec, tspec,
                  pl.BlockSpec((RET_HEADS, c, c), lambda bi, i: (0, 0, 0)),
                  cspec, cspec,
                  pl.BlockSpec((1, MIX_BLOCK), lambda bi, i: (0, 0)),
                  pl.BlockSpec((MIX_BLOCK, MIX_BLOCK), lambda bi, i: (0, 0))],
        out_specs=[pl.BlockSpec(blk, lambda bi, i: (bi, i, 0)),
                   pl.BlockSpec((1, RET_HEADS, HEAD_DIM, HEAD_DIM), lambda bi, i: (bi, 0, 0, 0))],
        out_shape=[jax.ShapeDtypeStruct((b, l, MIX_BLOCK), BF16),
                   jax.ShapeDtypeStruct((b, RET_HEADS, HEAD_DIM, HEAD_DIM), F32)],
        scratch_shapes=[pltpu.VMEM((RET_TILE, MIX_BLOCK), F32)],
        compiler_params=_cparams(("parallel", "arbitrary")),
        name="ret_prompt",
    )(gc, proj3d, proj3d, proj3d, proj3d, cos, sin, dec, qdec, kdec, norm, _head_mean_matrix(MIX_BLOCK))


def _ret_sample_kernel(gc_ref, q_ref, k_ref, v_ref, g_ref, cos_ref, sin_ref, dec_ref, qdec_ref,
                       kdec_ref, norm_ref, r0_ref, *rest, n_new, s_blk):
    y_ref, rn_ref = rest[-2:]
    _zero_other_layers(rn_ref)
    cos, sin = cos_ref[...], sin_ref[...]
    q = _rotate_pairs(q_ref[...], cos, sin)
    k = _rotate_pairs(k_ref[...], cos, sin) * (HEAD_DIM ** -0.5)
    v = v_ref[...]
    g = g_ref[...]
    kd = k * kdec_ref[...]
    qdec = qdec_ref[...]
    norm = norm_ref[...]
    rows = s_blk * n_new
    seq = lax.broadcasted_iota(jnp.int32, (rows, HEAD_DIM), 0) // n_new
    for h in range(RET_HEADS):
        sl = slice(h * HEAD_DIM, (h + 1) * HEAD_DIM)
        qf, kdf = q[:, sl], kd[:, sl]
        qh, kh, vh = qf.astype(BF16), k[:, sl].astype(BF16), v[:, sl].astype(BF16)
        s = _dot_nt(qh, kh) * dec_ref[h]
        cross = jnp.zeros((rows, HEAD_DIM), F32)
        for si in range(s_blk):
            mine = seq == si
            r = r0_ref[si, h]
            cross = cross + _dot(jnp.where(mine, qf, 0.0).astype(BF16), r.astype(BF16))
            rn_ref[0, si, h] = gc_ref[h] * r + _dot_tn(jnp.where(mine, kdf, 0.0).astype(BF16), vh)
        o = _dot(s.astype(BF16), vh) + cross * qdec[:, sl]
        y_ref[:, sl] = _ret_head_out(o, g[:, sl], norm[:, sl]).astype(y_ref.dtype)


def _ret_sample(proj2d, r0, layer, depth, prev_new, cos, sin, norm, n_new, s_blk):
    rows = s_blk * n_new
    nseq = r0.shape[0]
    dec, qdec, kdec, gc = _ret_consts(n_new, rows)
    blk = (rows, MIX_BLOCK)
    cspec = pl.BlockSpec(blk, lambda i: (0, 0))
    rblk = (s_blk, RET_HEADS, HEAD_DIM, HEAD_DIM)
    rspec, rshape, rin, rin_specs = _layer_state_out(layer, depth, rblk, (nseq,) + rblk[1:], F32, prev_new)
    return pl.pallas_call(
        functools.partial(_ret_sample_kernel, n_new=n_new, s_blk=s_blk),
        grid=(nseq // s_blk,),
        in_specs=[pl.BlockSpec(memory_space=pltpu.SMEM),
                  pl.BlockSpec(blk, lambda i: (i, COL_RET_Q)),
                  pl.BlockSpec(blk, lambda i: (i, COL_RET_K)),
                  pl.BlockSpec(blk, lambda i: (i, COL_RET_V)),
                  pl.BlockSpec(blk, lambda i: (i, COL_RET_G)),
                  cspec, cspec,
                  pl.BlockSpec((RET_HEADS, rows, rows), lambda i: (0, 0, 0)),
                  cspec, cspec,
                  pl.BlockSpec((1, MIX_BLOCK), lambda i: (0, 0)),
                  pl.BlockSpec(rblk, lambda i: (i, 0, 0, 0))] + rin_specs,
        out_specs=[pl.BlockSpec(blk, lambda i: (i, 0)), rspec],
        out_shape=[jax.ShapeDtypeStruct((nseq * n_new, MIX_BLOCK), BF16), rshape],
        input_output_aliases={12 + j: 1 for j in range(len(rin))},
        compiler_params=_cparams(("parallel",)),
        name="ret_sample",
    )(gc, proj2d, proj2d, proj2d, proj2d, cos, sin, dec, qdec, kdec, norm, r0, *rin)


def _block_diag(w):
    g, n, _ = w.shape
    return jnp.einsum('gcd,gh->gchd', w, jnp.eye(g, dtype=w.dtype)).reshape(g * n, g * n)


def _layer_params(l, p):
    return dict(
        norm_mix=p['norm_mix'][l].reshape(1, D_MODEL),
        norm_ffn=p['norm_ffn'][l].reshape(1, D_MODEL),
        w_in=p['w_in'][l].astype(BF16),
        w_out=p['w_out'][l].astype(BF16),
        pool_w=_block_diag(p['pool_w'][l].astype(F32)).astype(BF16),
        pool_scale=p['pool_scale'][l].astype(F32).reshape(1, POOL_WIDTH),
        qg=p['swa_q_norm'][l].astype(F32).reshape(1, HEAD_DIM),
        kg=p['swa_k_norm'][l].astype(F32).reshape(1, HEAD_DIM),
        sinks=p['swa_sinks'][l].astype(F32),
        ssm=_ssm_params(p['ssm_lambda_re'][l], p['ssm_lambda_im'][l], p['ssm_log_dt'][l],
                        p['ssm_b_re'][l], p['ssm_b_im'][l], p['ssm_c_re'][l], p['ssm_c_im'][l],
                        p['ssm_d'][l], p['ssm_w_glu'][l]),
        ret_norm=p['ret_norm'][l].astype(F32).reshape(1, MIX_BLOCK),
    )


def _channel_mix(streams, l, lp, p):
    i = l // 2
    g, w_out = lp['norm_ffn'], lp['w_out']
    if l % 2 == 0:
        wg, wu, wd = (p[k][i].astype(BF16) for k in ('ffn_w_gate', 'ffn_w_up', 'ffn_w_down'))
        return [_out_proj_ffn(x, ys, w_out, g, wg, wu, wd, 512, D_FF) for x, ys in streams]
    wr3 = _router_weights(p['moe_router'][i])
    wg, wu, wd = (p[k][i].astype(BF16) for k in ('moe_w_gate', 'moe_w_up', 'moe_w_down'))
    x1s, routes = zip(*[_out_proj_router(x, ys, w_out, g, wr3, 512) for x, ys in streams])
    return _moe(x1s, routes, g, wg, wu, wd)


def _mix_prompt(x2, b, l, lp, bias, cos, sin):
    proj2 = _norm_matmul(x2, lp['norm_mix'], lp['w_in'], 512)
    proj3 = proj2.reshape(b, l, IN_WIDTH)
    tb = 512
    y_pool = _pool(proj2, COL_POOL, lp['pool_w'], lp['pool_scale'], n_rows=b * l, tb=tb,
                   tiles_per_seq=l // tb, pos0=0)
    y_swa, kn = _swa_prompt(proj3, lp['sinks'], lp['qg'], lp['kg'], bias)
    y_ssm, hn = _ssm_prompt(proj3, lp['ssm'], 512)
    y_ret, rn = _ret_prompt(proj3, cos, sin, lp['ret_norm'])
    ys = (y_pool, y_swa.reshape(b * l, MIX_BLOCK), y_ssm.reshape(b * l, MIX_BLOCK),
          y_ret.reshape(b * l, MIX_BLOCK))
    kw = SWA_KV_HEADS * HEAD_DIM
    hn = hn[:, 0]
    states = (proj3[:, l - POOL_BUF:, :POOL_WIDTH],
              kn.reshape(b, SWA_WINDOW, SWA_KV_HEADS, HEAD_DIM),
              proj3[:, l - SWA_WINDOW:, COL_SWA_KV * MIX_BLOCK + kw:(COL_SWA_KV + 1) * MIX_BLOCK]
              .reshape(b, SWA_WINDOW, SWA_KV_HEADS, HEAD_DIM),
              jnp.stack([hn[:, :SSM_N], hn[:, SSM_N:]], axis=-1).reshape(b, SSM_GROUPS, SSM_STATE, 2),
              rn)
    return ys, states


SAMPLE_SEQ_BLOCK = 16


def _cache_transposed(cache):
    depth, nseq, wb = cache.shape[:3]
    return jnp.swapaxes(cache.astype(F32).reshape(depth, nseq, wb, SWA_KV_HEADS * HEAD_DIM), 2, 3)


def _mix_sample(x2, nseq, n_new, start_pos, lp, layer, st, prev_new, bias, cos, sin):
    state_pool, cache_kt, cache_vt, state_ssm, state_ret = st
    prev_kv, prev_ret = prev_new if prev_new is not None else (None, None)
    rows = nseq * n_new
    kw = SWA_KV_HEADS * HEAD_DIM
    ext_rows = POOL_HALO + n_new
    proj2 = _norm_matmul(x2, lp['norm_mix'], lp['w_in'], 512)
    proj3 = proj2.reshape(nseq, n_new, IN_WIDTH)
    u_pool = proj3[:, :, :POOL_WIDTH]
    buf = state_pool.astype(F32)
    ext = jnp.concatenate([jnp.zeros((nseq, POOL_HALO - POOL_BUF, POOL_WIDTH), F32), buf, u_pool], axis=1)
    y_pool = _pool(ext.reshape(nseq * ext_rows, POOL_WIDTH), 0, lp['pool_w'], lp['pool_scale'],
                   n_rows=nseq * ext_rows, tb=nseq * ext_rows, tiles_per_seq=1, pos0=start_pos)
    y_pool = y_pool.reshape(nseq, ext_rows, POOL_WIDTH)[:, POOL_HALO:].reshape(rows, POOL_WIDTH)
    y_swa, nkt, nvt = _swa_sample(proj2, cache_kt, cache_vt, layer, prev_kv, lp['sinks'], lp['qg'], lp['kg'],
                                  bias, n_new, SAMPLE_SEQ_BLOCK)
    h0 = state_ssm.astype(F32).reshape(nseq, SSM_N, 2)
    h0 = jnp.concatenate([h0[..., 0], h0[..., 1]], axis=1)
    y_ssm, hn = _ssm_sample(proj2, h0, lp['ssm'])
    y_ret, rn = _ret_sample(proj2, state_ret[layer], layer, state_ret.shape[0], prev_ret, cos, sin,
                            lp['ret_norm'], n_new, SAMPLE_SEQ_BLOCK)
    ys = (y_pool, y_swa, y_ssm, y_ret)
    states = (jnp.concatenate([buf, u_pool], axis=1)[:, -POOL_BUF:],
              jnp.stack([hn[:, :SSM_N], hn[:, SSM_N:]], axis=-1).reshape(nseq, SSM_GROUPS, SSM_STATE, 2))
    return ys, states, ((nkt, nvt), rn)


def _forward(x_prompt, x_sample, past_len, sample_state, p, rel_bias):
    b, l, d = x_prompt.shape
    nseq, n_new, _ = x_sample.shape
    wb = sample_state[1].shape[2]
    depth = p['norm_mix'].shape[0]
    bias_p = _swa_bias(rel_bias, np.arange(SWA_BLOCK)[:, None] - np.arange(2 * SWA_BLOCK)[None, :] + SWA_BLOCK)
    bias_s = _swa_bias(rel_bias, np.arange(n_new)[:, None] - np.arange(wb + n_new)[None, :] + wb)
    rope_p = _rope_tables(np.arange(l))
    rope_s = _rope_tables(past_len + (np.arange(SAMPLE_SEQ_BLOCK * n_new) % n_new))
    xp = x_prompt.reshape(b * l, d)
    xs = x_sample.reshape(nseq * n_new, d)
    cache_kt, cache_vt = _cache_transposed(sample_state[1]), _cache_transposed(sample_state[2])
    st_p, st_s, stacked = [], [], None
    for li in range(depth):
        lp = _layer_params(li, p)
        yp, sp = _mix_prompt(xp, b, l, lp, bias_p, *rope_p)
        st = (sample_state[0][li], cache_kt, cache_vt, sample_state[3][li], sample_state[4])
        ys, ss, stacked = _mix_sample(xs, nseq, n_new, past_len, lp, li, st, stacked, bias_s, *rope_s)
        xp, xs = _channel_mix([(xp, yp), (xs, ys)], li, lp, p)
        st_p.append(sp)
        st_s.append(ss)
    (new_kt, new_vt), new_ret = stacked
    kv_shape = (depth, nseq, wb, SWA_KV_HEADS, HEAD_DIM)
    sample_out = (jnp.stack([s[0] for s in st_s]),
                  jnp.swapaxes(new_kt, 2, 3).reshape(kv_shape),
                  jnp.swapaxes(new_vt, 2, 3).reshape(kv_shape),
                  jnp.stack([s[1] for s in st_s]),
                  new_ret)
    outs = [xp.reshape(b, l, d), xs.reshape(nseq, n_new, d)]
    for k in range(5):
        outs.append(jnp.stack([s[k] for s in st_p]))
        outs.append(sample_out[k])
    return tuple(outs)


PAST_LEN = 16384


def kernel(x_prompt, x_sample, state_pool, cache_swa_k, cache_swa_v, state_ssm, state_ret,
           norm_mix, norm_ffn, w_in, w_out, pool_w, pool_scale, swa_q_norm, swa_k_norm, swa_sinks,
           rel_bias, ssm_lambda_re, ssm_lambda_im, ssm_log_dt, ssm_b_re, ssm_b_im, ssm_c_re, ssm_c_im,
           ssm_d, ssm_w_glu, ret_norm, ffn_w_gate, ffn_w_up, ffn_w_down, moe_router, moe_w_gate,
           moe_w_up, moe_w_down):
    p = dict(norm_mix=norm_mix, norm_ffn=norm_ffn, w_in=w_in, w_out=w_out, pool_w=pool_w,
             pool_scale=pool_scale, swa_q_norm=swa_q_norm, swa_k_norm=swa_k_norm, swa_sinks=swa_sinks,
             ssm_lambda_re=ssm_lambda_re, ssm_lambda_im=ssm_lambda_im, ssm_log_dt=ssm_log_dt,
             ssm_b_re=ssm_b_re, ssm_b_im=ssm_b_im, ssm_c_re=ssm_c_re, ssm_c_im=ssm_c_im,
             ssm_d=ssm_d, ssm_w_glu=ssm_w_glu, ret_norm=ret_norm,
             ffn_w_gate=ffn_w_gate, ffn_w_up=ffn_w_up, ffn_w_down=ffn_w_down, moe_router=moe_router,
             moe_w_gate=moe_w_gate, moe_w_up=moe_w_up, moe_w_down=moe_w_down)
    return _forward(x_prompt, x_sample, PAST_LEN,
                    (state_pool, cache_swa_k, cache_swa_v, state_ssm, state_ret), p, rel_bias)
```

```python
import functools
import math

import numpy as np
import jax
import jax.numpy as jnp
from jax import lax
from jax.experimental import pallas as pl
from jax.experimental.pallas import tpu as pltpu

F32 = jnp.float32
BF16 = jnp.bfloat16

D_MODEL = 1024
HEAD_DIM = 64
POOL_WIDTH = 256
POOL_WINDOWS = (2, 4, 8, 16)
POOL_BUF = 15
POOL_HALO = 16
SWA_HEADS = 4
SWA_KV_HEADS = 2
SWA_WINDOW = 128
SWA_BLOCK = 128
SSM_WIDTH = 256
SSM_CH = 16
SSM_GROUPS = 16
SSM_STATE = 64
SSM_N = SSM_GROUPS * SSM_STATE
RET_HEADS = 4
RET_CHUNK = 128
ROPE_BASE = 10000.0
IN_WIDTH = 2048
MIX_BLOCK = 256
D_FF = 2816
N_EXPERTS = 8
T5_BUCKETS = 32
T5_MAX_DIST = 128
RMS_EPS = 1e-6
NEG = -1e30
SUBLANES = 8
SCAN_ROWS = SUBLANES

COL_POOL, COL_SWA_Q, COL_SWA_KV, COL_SSM, COL_RET_Q, COL_RET_K, COL_RET_V, COL_RET_G = range(8)

VMEM_LIMIT = 48 * 1024 * 1024
FFN_VMEM_LIMIT = 58 * 1024 * 1024


def _cparams(sem, vmem=VMEM_LIMIT):
    return pltpu.CompilerParams(dimension_semantics=sem, vmem_limit_bytes=vmem)


def _rms(x, g):
    ms = jnp.mean(x * x, axis=-1, keepdims=True)
    return x * lax.rsqrt(ms + RMS_EPS) * g


def _dot(a, b):
    return jnp.dot(a, b, preferred_element_type=F32)


def _dot_nt(a, b):
    return lax.dot_general(a, b, (((1,), (1,)), ((), ())), preferred_element_type=F32)


def _dot_tn(a, b):
    return lax.dot_general(a, b, (((0,), (0,)), ((), ())), preferred_element_type=F32)


def _sigmoid(x):
    return 1.0 / (1.0 + jnp.exp(-x))


def _norm_matmul_kernel(x_ref, g_ref, w_ref, o_ref):
    h = _rms(x_ref[...], g_ref[...]).astype(BF16)
    o_ref[...] = _dot(h, w_ref[...])


def _norm_matmul(x, g, w, tm):
    t, d = x.shape
    tm = min(tm, t)
    n = w.shape[1]
    return pl.pallas_call(
        _norm_matmul_kernel,
        grid=(t // tm,),
        in_specs=[pl.BlockSpec((tm, d), lambda i: (i, 0)),
                  pl.BlockSpec((1, d), lambda i: (0, 0)),
                  pl.BlockSpec((d, n), lambda i: (0, 0))],
        out_specs=pl.BlockSpec((tm, n), lambda i: (i, 0)),
        out_shape=jax.ShapeDtypeStruct((t, n), F32),
        compiler_params=_cparams(("parallel",)),
        name="norm_matmul",
    )(x, g, w)


FFN_SUBCHUNK = 512


def _swiglu_chunk(h, wg_ref, wu_ref, wd_ref):
    tf = wg_ref.shape[1]
    y = None
    for lo in range(0, tf, FFN_SUBCHUNK):
        hi = min(lo + FFN_SUBCHUNK, tf)
        a = _dot(h, wg_ref[:, lo:hi])
        b = _dot(h, wu_ref[:, lo:hi])
        part = _dot((a * _sigmoid(a) * b).astype(BF16), wd_ref[lo:hi, :])
        y = part if y is None else y + part
    return y


def _mixed_residual(x_ref, y_refs, w_ref, rows=slice(None)):
    y = jnp.concatenate([y_ref[rows, :] for y_ref in y_refs], axis=1)
    return x_ref[rows, :] + _dot(y, w_ref[...])


def _mix_in_specs(tm, d, imap):
    yspec = pl.BlockSpec((tm, MIX_BLOCK), imap(lambda i: (i, 0)))
    return [pl.BlockSpec((tm, d), imap(lambda i: (i, 0))), yspec, yspec, yspec, yspec,
            pl.BlockSpec((d, d), imap(lambda i: (0, 0)))]


def _out_proj_ffn_kernel(x_ref, y0_ref, y1_ref, y2_ref, y3_ref, wo_ref, g_ref, wg_ref, wu_ref, wd_ref,
                         o_ref, h_scr):
    @pl.when(pl.program_id(1) == 0)
    def _():
        x1 = _mixed_residual(x_ref, (y0_ref, y1_ref, y2_ref, y3_ref), wo_ref)
        h_scr[...] = _rms(x1, g_ref[...]).astype(BF16)
        o_ref[...] = x1

    o_ref[...] += _swiglu_chunk(h_scr[...], wg_ref, wu_ref, wd_ref)


def _out_proj_ffn(x, ys, w_out, g, wg, wu, wd, tm, tf):
    t, d = x.shape
    tm = min(tm, t)
    f = wg.shape[1]
    imap = lambda fn: (lambda i, j: fn(i))
    once = dict(pipeline_mode=pl.Buffered(1)) if tf == f else {}
    return pl.pallas_call(
        _out_proj_ffn_kernel,
        grid=(t // tm, f // tf),
        in_specs=_mix_in_specs(tm, d, imap)
                 + [pl.BlockSpec((1, d), lambda i, j: (0, 0)),
                    pl.BlockSpec((d, tf), lambda i, j: (0, j), **once),
                    pl.BlockSpec((d, tf), lambda i, j: (0, j), **once),
                    pl.BlockSpec((tf, d), lambda i, j: (j, 0), **once)],
        out_specs=pl.BlockSpec((tm, d), lambda i, j: (i, 0)),
        out_shape=jax.ShapeDtypeStruct((t, d), F32),
        scratch_shapes=[pltpu.VMEM((tm, d), BF16)],
        compiler_params=_cparams(("parallel", "arbitrary"), FFN_VMEM_LIMIT),
        name="out_proj_ffn",
    )(x, *ys, w_out, g, wg, wu, wd)


ROUTE_ID_LANES = (0, 1)
ROUTE_GATE_LANES = (2, 3)


def _split_bf16(x):
    hi = x.astype(BF16)
    return hi, (x - hi.astype(F32)).astype(BF16)


ROUTER_ROWS = 256


def _out_proj_router_kernel(x_ref, y0_ref, y1_ref, y2_ref, y3_ref, wo_ref, g_ref, wr_ref, x1_ref, c_ref):
    tm = x_ref.shape[0]
    step = min(ROUTER_ROWS, tm)
    for r in range(0, tm, step):
        _route_rows(slice(r, r + step), x_ref, (y0_ref, y1_ref, y2_ref, y3_ref), wo_ref, g_ref, wr_ref,
                    x1_ref, c_ref)


def _route_rows(rows, x_ref, y_refs, wo_ref, g_ref, wr_ref, x1_ref, c_ref):
    x1 = _mixed_residual(x_ref, y_refs, wo_ref, rows)
    x1_ref[rows, :] = x1
    h_hi, h_lo = _split_bf16(_rms(x1, g_ref[...]))
    logits = _dot(jnp.concatenate([h_hi, h_lo, h_hi], axis=1), wr_ref[...])
    lane = lax.broadcasted_iota(jnp.int32, logits.shape, 1).astype(F32)
    lg = jnp.where(lane < N_EXPERTS, logits, NEG)
    m1 = jnp.max(lg, axis=-1, keepdims=True)
    i1 = jnp.min(jnp.where(lg == m1, lane, 128.0), axis=-1, keepdims=True)
    lg2 = jnp.where(lane == i1, NEG, lg)
    m2 = jnp.max(lg2, axis=-1, keepdims=True)
    i2 = jnp.min(jnp.where(lg2 == m2, lane, 128.0), axis=-1, keepdims=True)
    ex = jnp.exp(m2 - m1)
    vals = (i1, i2, 1.0 / (1.0 + ex), ex / (1.0 + ex))
    out = jnp.zeros_like(logits)
    for ln, v in zip(ROUTE_ID_LANES + ROUTE_GATE_LANES, vals):
        out = jnp.where(lane == ln, v, out)
    c_ref[rows, :] = out


def _router_weights(wr):
    w = jnp.pad(wr.astype(F32), ((0, 0), (0, 128 - N_EXPERTS)))
    hi, lo = _split_bf16(w)
    return jnp.concatenate([hi, hi, lo], axis=0)


def _out_proj_router(x, ys, w_out, g, wr3, tm):
    t, d = x.shape
    tm = min(tm, t)
    imap = lambda fn: fn
    return pl.pallas_call(
        _out_proj_router_kernel,
        grid=(t // tm,),
        in_specs=_mix_in_specs(tm, d, imap)
                 + [pl.BlockSpec((1, d), lambda i: (0, 0)),
                    pl.BlockSpec((3 * d, 128), lambda i: (0, 0))],
        out_specs=[pl.BlockSpec((tm, d), lambda i: (i, 0)),
                   pl.BlockSpec((tm, 128), lambda i: (i, 0))],
        out_shape=[jax.ShapeDtypeStruct((t, d), F32), jax.ShapeDtypeStruct((t, 128), F32)],
        compiler_params=_cparams(("parallel",)),
        name="out_proj_router",
    )(x, *ys, w_out, g, wr3)


DMA_ISSUE_UNROLL = 8


def _row_copy(src, i, dst, j, sem):
    return pltpu.make_async_copy(src.at[pl.ds(i, 1)], dst.at[pl.ds(j, 1)], sem)


def _dispatch_kernel(meta_ref, pos_ref, *rest, td, tm, n_tiles, first_step):
    n_streams = len(first_step) - 1
    x_refs, (xs_hbm, zero_scr, sem) = rest[:n_streams], rest[n_streams:]
    step = pl.program_id(0)

    def zero_row(r):
        return _row_copy(zero_scr, 0, xs_hbm, r, sem)

    @pl.when(step == 0)
    def _():
        zero_scr[...] = jnp.zeros_like(zero_scr)
        n_used = meta_ref[2 * N_EXPERTS]
        tile_fills = [(i >= n_used, pltpu.make_async_copy(zero_scr, xs_hbm.at[pl.ds(i * tm, tm)], sem))
                      for i in range(n_tiles)]
        for cond, copy in tile_fills:
            pl.when(cond)(copy.start)
        for e in range(N_EXPERTS):
            lax.fori_loop(meta_ref[e], meta_ref[N_EXPERTS + e], lambda r, c: (zero_row(r).start(), c)[1], 0)
        for e in range(N_EXPERTS):
            lax.fori_loop(meta_ref[e], meta_ref[N_EXPERTS + e], lambda r, c: (zero_row(r).wait(), c)[1], 0)
        for cond, copy in tile_fills:
            pl.when(cond)(copy.wait)

    def scatter(x_ref):
        def issue(j, c):
            for k in range(2):
                _row_copy(x_ref, j, xs_hbm, pos_ref[0, 0, 2 * j + k], sem).start()
            return c

        lax.fori_loop(0, td, issue, 0, unroll=DMA_ISSUE_UNROLL)
        for _ in range(2):
            pltpu.make_async_copy(x_ref, xs_hbm.at[pl.ds(0, td)], sem).wait()

    for s, x_ref in enumerate(x_refs):
        pl.when((step >= first_step[s]) & (step < first_step[s + 1]))(functools.partial(scatter, x_ref))


def _dispatch(xs_list, pos, meta, n_rows, tm, td):
    d = xs_list[0].shape[1]
    td = min([td] + [x.shape[0] for x in xs_list])
    first_step = [0]
    for x in xs_list:
        first_step.append(first_step[-1] + x.shape[0] // td)
    n_steps = first_step[-1]
    pos3 = pos.reshape(n_steps, 1, 2 * td)

    def tile_map(s):
        lo, hi = first_step[s], first_step[s + 1]
        return lambda i, m: (jnp.clip(i, lo, hi - 1) - lo, 0)

    in_specs = [pl.BlockSpec((1, 1, 2 * td), lambda i, m: (i, 0, 0), memory_space=pltpu.SMEM)]
    in_specs += [pl.BlockSpec((td, d), tile_map(s)) for s in range(len(xs_list))]
    return pl.pallas_call(
        functools.partial(_dispatch_kernel, td=td, tm=tm, n_tiles=n_rows // tm, first_step=tuple(first_step)),
        grid_spec=pltpu.PrefetchScalarGridSpec(
            num_scalar_prefetch=1, grid=(n_steps,), in_specs=in_specs,
            out_specs=pl.BlockSpec(memory_space=pl.ANY),
            scratch_shapes=[pltpu.VMEM((tm, d), F32), pltpu.SemaphoreType.DMA]),
        out_shape=jax.ShapeDtypeStruct((n_rows, d), F32),
        compiler_params=_cparams(("arbitrary",)),
        name="moe_dispatch",
    )(meta, pos3, *xs_list)


def _grouped_ffn_kernel(te_ref, nu_ref, x_ref, g_ref, wg_ref, wu_ref, wd_ref, o_ref, h_scr):
    del te_ref
    j = pl.program_id(1)
    used = pl.program_id(0) < nu_ref[0]

    @pl.when(jnp.logical_not(used) & (j == 0))
    def _():
        o_ref[...] = jnp.zeros_like(o_ref)

    @pl.when(used)
    def _():
        @pl.when(j == 0)
        def _():
            h_scr[...] = _rms(x_ref[...], g_ref[...]).astype(BF16)

        y = _swiglu_chunk(h_scr[...], wg_ref.at[0], wu_ref.at[0], wd_ref.at[0])

        @pl.when(j == 0)
        def _():
            o_ref[...] = y

        @pl.when(j > 0)
        def _():
            o_ref[...] += y


def _grouped_ffn(xs, g, tile_expert, n_used, wg, wu, wd, tm, tf):
    r, d = xs.shape
    f = wg.shape[2]
    nj = f // tf

    def row_map(i, j, te, nu):
        return (i, 0)

    def col_of(i, j, nu):
        return jnp.where(i < nu[0], j, nj - 1)

    grid_spec = pltpu.PrefetchScalarGridSpec(
        num_scalar_prefetch=2,
        grid=(r // tm, nj),
        in_specs=[pl.BlockSpec((tm, d), row_map),
                  pl.BlockSpec((1, d), lambda i, j, te, nu: (0, 0)),
                  pl.BlockSpec((1, d, tf), lambda i, j, te, nu: (te[i], 0, col_of(i, j, nu))),
                  pl.BlockSpec((1, d, tf), lambda i, j, te, nu: (te[i], 0, col_of(i, j, nu))),
                  pl.BlockSpec((1, tf, d), lambda i, j, te, nu: (te[i], col_of(i, j, nu), 0))],
        out_specs=pl.BlockSpec((tm, d), row_map),
        scratch_shapes=[pltpu.VMEM((tm, d), BF16)],
    )
    return pl.pallas_call(
        _grouped_ffn_kernel,
        grid_spec=grid_spec,
        out_shape=jax.ShapeDtypeStruct((r, d), F32),
        compiler_params=_cparams(("arbitrary", "arbitrary"), FFN_VMEM_LIMIT),
        name="moe_grouped_ffn",
    )(tile_expert, n_used, xs, g, wg, wu, wd)


def _combine_kernel(pos_ref, pos_next_ref, x_ref, route_ref, ys_hbm, o_ref, buf0, buf1, sems, *, tc):
    step = pl.program_id(0)
    slot = step % 2

    def gather(p_ref, s):
        def issue(j, c):
            _row_copy(ys_hbm, p_ref[0, 0, 2 * j], buf0.at[s], j, sems.at[s]).start()
            _row_copy(ys_hbm, p_ref[0, 0, 2 * j + 1], buf1.at[s], j, sems.at[s]).start()
            return c

        lax.fori_loop(0, tc, issue, 0, unroll=DMA_ISSUE_UNROLL)

    pl.when(step == 0)(functools.partial(gather, pos_ref, 0))
    pl.when(step + 1 < pl.num_programs(0))(functools.partial(gather, pos_next_ref, 1 - slot))
    for buf in (buf0, buf1):
        pltpu.make_async_copy(ys_hbm.at[pl.ds(0, tc)], buf.at[slot], sems.at[slot]).wait()
    route = route_ref[...]
    g0 = route[:, ROUTE_GATE_LANES[0]:ROUTE_GATE_LANES[0] + 1]
    g1 = route[:, ROUTE_GATE_LANES[1]:ROUTE_GATE_LANES[1] + 1]
    o_ref[...] = x_ref[...] + g0 * buf0[slot] + g1 * buf1[slot]


def _combine(x, route, pos, ys, tc):
    t, d = x.shape
    tc = min(tc, t)
    n = t // tc
    pos3 = pos.reshape(n, 1, 2 * tc)
    pos_block = (1, 1, 2 * tc)
    return pl.pallas_call(
        functools.partial(_combine_kernel, tc=tc),
        grid=(n,),
        in_specs=[pl.BlockSpec(pos_block, lambda i: (i, 0, 0), memory_space=pltpu.SMEM),
                  pl.BlockSpec(pos_block, lambda i: (jnp.minimum(i + 1, n - 1), 0, 0), memory_space=pltpu.SMEM),
                  pl.BlockSpec((tc, d), lambda i: (i, 0)),
                  pl.BlockSpec((tc, 128), lambda i: (i, 0)),
                  pl.BlockSpec(memory_space=pl.ANY)],
        out_specs=pl.BlockSpec((tc, d), lambda i: (i, 0)),
        out_shape=jax.ShapeDtypeStruct((t, d), F32),
        scratch_shapes=[pltpu.VMEM((2, tc, d), F32), pltpu.VMEM((2, tc, d), F32),
                        pltpu.SemaphoreType.DMA((2,))],
        compiler_params=_cparams(("arbitrary",)),
        name="moe_combine",
    )(pos3, pos3, x, route, ys)


MOE_TM = 512


def _route_plan(expert_ids, tm):
    flat = expert_ids.reshape(-1)
    a = flat.shape[0]
    onehot = (flat[None, :] == jnp.arange(N_EXPERTS, dtype=jnp.int32)[:, None]).astype(jnp.int32)
    csum = jnp.cumsum(onehot, axis=1)
    counts = csum[:, -1]
    padded = (counts + tm - 1) // tm * tm
    ends = jnp.cumsum(padded)
    offs = ends - padded
    pos = jnp.sum(onehot * (offs[:, None] + csum - 1), axis=0)
    n_tiles = (a + N_EXPERTS * tm) // tm
    tile_start = jnp.arange(n_tiles, dtype=jnp.int32) * tm
    tile_expert = jnp.minimum(jnp.sum(tile_start[:, None] >= ends[None, :], axis=1), N_EXPERTS - 1)
    n_used = (ends[-1] // tm).reshape(1)
    last = jnp.take(tile_expert, n_used[0] - 1)
    tile_expert = jnp.where(tile_start < ends[-1], tile_expert, last)
    meta = jnp.concatenate([offs + counts, ends, n_used]).astype(jnp.int32)
    return pos.astype(jnp.int32), tile_expert.astype(jnp.int32), n_used.astype(jnp.int32), meta, n_tiles * tm


def _moe(xs_list, routes, g, wg, wu, wd):
    ids = jnp.concatenate([r[:, ROUTE_ID_LANES[0]:ROUTE_ID_LANES[1] + 1] for r in routes]).astype(jnp.int32)
    pos, tile_expert, n_used, meta, n_rows = _route_plan(ids, MOE_TM)
    bounds = np.cumsum([0] + [2 * x.shape[0] for x in xs_list])
    pos_list = [pos[lo:hi] for lo, hi in zip(bounds[:-1], bounds[1:])]
    xs = _dispatch(xs_list, pos, meta, n_rows, MOE_TM, 1024)
    ys = _grouped_ffn(xs, g, tile_expert, n_used, wg, wu, wd, MOE_TM, D_FF)
    return [_combine(x, r, ps, ys, 512) for x, r, ps in zip(xs_list, routes, pos_list)]


def _pool_kernel(u_ref, halo_ref, inv_ref, w_ref, scale_ref, o_ref, *, tiles_per_seq):
    ti = pl.program_id(0) % tiles_per_seq
    u = u_ref[...]
    halo = jnp.where(ti == 0, 0.0, halo_ref[...])
    ext = jnp.concatenate([halo, u], axis=0)
    half = POOL_WIDTH // 2
    short = lax.broadcasted_iota(jnp.int32, (ext.shape[0], half), 1) < half // 2

    def window_sums(x, n_doublings):
        sums = []
        for k in range(n_doublings):
            x = x + pltpu.roll(x, 2 ** k, 0)
            sums.append(x)
        return jnp.where(short, sums[-2], sums[-1])[POOL_HALO:]

    s = jnp.concatenate([window_sums(ext[:, :half], 2), window_sums(ext[:, half:], 4)], axis=1)
    inv_rest = inv_ref[1]
    inv_head = inv_ref[jnp.minimum(ti, 1)]
    pooled = jnp.concatenate([s[:POOL_HALO] * inv_head, s[POOL_HALO:] * inv_rest[:1]], axis=0) - u
    o_ref[...] = (_dot(pooled.astype(BF16), w_ref[...]) * scale_ref[...]).astype(o_ref.dtype)


def _pool(proj2d, col, w, scale, *, n_rows, tb, tiles_per_seq, pos0):
    per = tb // POOL_HALO
    win = np.repeat(np.asarray(POOL_WINDOWS), POOL_WIDTH // len(POOL_WINDOWS))[None, :]
    count = np.minimum(win, pos0 + np.arange(POOL_HALO)[:, None] + 1)
    inv = jnp.asarray(np.stack([1.0 / count, np.broadcast_to(1.0 / win, count.shape)]), F32)
    return pl.pallas_call(
        functools.partial(_pool_kernel, tiles_per_seq=tiles_per_seq),
        grid=(n_rows // tb,),
        in_specs=[pl.BlockSpec((tb, POOL_WIDTH), lambda i: (i, col)),
                  pl.BlockSpec((POOL_HALO, POOL_WIDTH), lambda i: (jnp.maximum(i * per - 1, 0), col)),
                  pl.BlockSpec((2, POOL_HALO, POOL_WIDTH), lambda i: (0, 0, 0)),
                  pl.BlockSpec((POOL_WIDTH, POOL_WIDTH), lambda i: (0, 0)),
                  pl.BlockSpec((1, POOL_WIDTH), lambda i: (0, 0))],
        out_specs=pl.BlockSpec((tb, POOL_WIDTH), lambda i: (i, 0)),
        out_shape=jax.ShapeDtypeStruct((n_rows, POOL_WIDTH), BF16),
        compiler_params=_cparams(("parallel",)),
        name="pool",
    )(proj2d, proj2d, inv, w, scale)


def _t5_bucket_np(rel):
    n = np.maximum(rel, 0)
    max_exact = T5_BUCKETS // 2
    nf = np.maximum(n, max_exact).astype(np.float32)
    large = max_exact + (np.log(nf / max_exact) / math.log(T5_MAX_DIST / max_exact)
                         * (T5_BUCKETS - max_exact)).astype(np.int32)
    large = np.minimum(large, T5_BUCKETS - 1)
    return np.where(n < max_exact, n, large)


def _swa_bias(rel_bias, rel):
    valid = (rel >= 0) & (rel < SWA_WINDOW)
    onehot = jnp.asarray(_t5_bucket_np(rel)[..., None] == np.arange(T5_BUCKETS), F32)
    b = jnp.einsum('qsb,bh->hqs', onehot, rel_bias.astype(F32), precision=lax.Precision.HIGHEST)
    return jnp.where(valid[None], b, NEG)


def _softmax_parts(parts, sink):
    m = sink
    for s in parts:
        m = jnp.maximum(m, jnp.max(s, axis=-1, keepdims=True))
    ps = [jnp.exp(s - m) for s in parts]
    denom = jnp.exp(sink - m)
    for p in ps:
        denom = denom + jnp.sum(p, axis=-1, keepdims=True)
    return ps, denom


def _head_mean_matrix(width):
    h = np.arange(width) // HEAD_DIM
    return jnp.asarray((h[:, None] == h[None, :]) / HEAD_DIM, BF16)


def _head_rms(x, mean_mat, g):
    ms = _dot((x * x).astype(BF16), mean_mat)
    return x * lax.rsqrt(ms + RMS_EPS) * g


SWA_TILE = 512


def _swa_prompt_kernel(sinks_ref, q_ref, kv_ref, halo_ref, qg_ref, kg_ref, mq_ref, mk_ref, bias_ref,
                       y_ref, kn_ref):
    has_prev = pl.program_id(1) > 0
    kw = SWA_KV_HEADS * HEAD_DIM
    blk = SWA_BLOCK
    kv = kv_ref[0]
    halo = halo_ref[0]
    k_ext = jnp.concatenate([halo[:, :kw], kv[:, :kw]], axis=0)
    v_ext = jnp.concatenate([halo[:, kw:], kv[:, kw:]], axis=0).astype(BF16)
    kn = _head_rms(k_ext, mk_ref[...], kg_ref[...])
    kn_ref[0] = kn[SWA_TILE:]
    knb = kn.astype(BF16)
    qn = (_head_rms(q_ref[0], mq_ref[...], qg_ref[...]) * (HEAD_DIM ** -0.5)).astype(BF16)
    row = lax.broadcasted_iota(jnp.int32, (2 * blk, 1), 0)
    col = lax.broadcasted_iota(jnp.int32, (2 * blk, 2 * blk), 1)
    for c in range(SWA_TILE // blk):
        rows = slice(c * blk, (c + 1) * blk)
        keys = slice(c * blk, (c + 2) * blk)
        for kh in range(SWA_KV_HEADS):
            ksl = slice(kh * HEAD_DIM, (kh + 1) * HEAD_DIM)
            h0 = 2 * kh
            q2 = jnp.concatenate([qn[rows, h0 * HEAD_DIM:(h0 + 1) * HEAD_DIM],
                                  qn[rows, (h0 + 1) * HEAD_DIM:(h0 + 2) * HEAD_DIM]], axis=0)
            s = _dot_nt(q2, knb[keys, ksl]) + bias_ref[kh]
            if c == 0:
                s = jnp.where(has_prev | (col >= blk), s, NEG)
            sink = jnp.where(row < blk, sinks_ref[h0], sinks_ref[h0 + 1])
            (p,), denom = _softmax_parts((s,), sink)
            o = _dot(p.astype(BF16), v_ext[keys, ksl]) / denom
            y_ref[0, rows, h0 * HEAD_DIM:(h0 + 1) * HEAD_DIM] = o[:blk].astype(y_ref.dtype)
            y_ref[0, rows, (h0 + 1) * HEAD_DIM:(h0 + 2) * HEAD_DIM] = o[blk:].astype(y_ref.dtype)


def _swa_prompt(proj3d, sinks, qg, kg, bias):
    b, l, _ = proj3d.shape
    kw = SWA_KV_HEADS * HEAD_DIM
    per = SWA_TILE // SWA_BLOCK
    tile = (1, SWA_TILE, MIX_BLOCK)
    const2 = lambda bi, i: (0, 0)
    bias2 = bias.reshape(SWA_KV_HEADS, 2 * SWA_BLOCK, 2 * SWA_BLOCK)
    return pl.pallas_call(
        _swa_prompt_kernel,
        grid=(b, l // SWA_TILE),
        in_specs=[pl.BlockSpec(memory_space=pltpu.SMEM),
                  pl.BlockSpec(tile, lambda bi, i: (bi, i, COL_SWA_Q)),
                  pl.BlockSpec(tile, lambda bi, i: (bi, i, COL_SWA_KV)),
                  pl.BlockSpec((1, SWA_BLOCK, MIX_BLOCK),
                               lambda bi, i: (bi, jnp.maximum(i * per - 1, 0), COL_SWA_KV)),
                  pl.BlockSpec((1, MIX_BLOCK), const2),
                  pl.BlockSpec((1, kw), const2),
                  pl.BlockSpec((MIX_BLOCK, MIX_BLOCK), const2),
                  pl.BlockSpec((kw, kw), const2),
                  pl.BlockSpec((SWA_KV_HEADS, 2 * SWA_BLOCK, 2 * SWA_BLOCK), lambda bi, i: (0, 0, 0))],
        out_specs=[pl.BlockSpec(tile, lambda bi, i: (bi, i, 0)),
                   pl.BlockSpec((1, SWA_BLOCK, kw), lambda bi, i: (bi, 0, 0))],
        out_shape=[jax.ShapeDtypeStruct((b, l, MIX_BLOCK), BF16),
                   jax.ShapeDtypeStruct((b, SWA_BLOCK, kw), F32)],
        compiler_params=_cparams(("parallel", "arbitrary")),
        name="swa_prompt",
    )(sinks, proj3d, proj3d, proj3d, jnp.tile(qg, (1, SWA_HEADS)), jnp.tile(kg, (1, SWA_KV_HEADS)),
      _head_mean_matrix(MIX_BLOCK), _head_mean_matrix(kw), bias2)


def _layer_state_out(layer, depth, blk, shape, dtype, prev):
    zeros = (0,) * (len(blk) - 1)
    shape = jax.ShapeDtypeStruct((depth,) + shape, dtype)
    if prev is None:
        assert layer == 0
        return pl.BlockSpec((depth,) + blk, lambda i: (0, i) + zeros), shape, [], []
    spec = pl.BlockSpec((1,) + blk, lambda i: (layer, i) + zeros)
    return spec, shape, [prev], [pl.BlockSpec(memory_space=pl.ANY)]


def _zero_other_layers(ref):
    if ref.shape[0] > 1:
        ref[1:] = jnp.zeros((ref.shape[0] - 1,) + ref.shape[1:], ref.dtype)


def _swa_sample_kernel(sinks_ref, q_ref, kv_ref, ckt_ref, cvt_ref, qg_ref, kg_ref, mq_ref, mk_ref, bias_ref,
                       *rest, n_new):
    y_ref, nkt_ref, nvt_ref = rest[-3:]
    _, s_blk, kw, wb = ckt_ref.shape
    keep = wb - n_new
    q2 = _head_rms(q_ref[...], mq_ref[...], qg_ref[...]) * (HEAD_DIM ** -0.5)
    kv = kv_ref[...]
    kn2 = _head_rms(kv[:, :kw], mk_ref[...], kg_ref[...])
    v2 = kv[:, kw:]
    q3 = q2.reshape(s_blk, n_new, MIX_BLOCK)
    kn3 = kn2.reshape(s_blk, n_new, kw)
    v3 = v2.reshape(s_blk, n_new, kw)
    ckt = ckt_ref[0]
    cvt = cvt_ref[0]
    bdot = functools.partial(jnp.einsum, preferred_element_type=F32)
    for h in range(SWA_HEADS):
        kh = h // (SWA_HEADS // SWA_KV_HEADS)
        ksl = slice(kh * HEAD_DIM, (kh + 1) * HEAD_DIM)
        hsl = slice(h * HEAD_DIM, (h + 1) * HEAD_DIM)
        qh = q3[:, :, hsl]
        s_c = bdot('sqd,sdw->sqw', qh, ckt[:, ksl, :]) + bias_ref[h, :, :wb]
        s_n = bdot('sqd,sjd->sqj', qh, kn3[:, :, ksl]) + bias_ref[h, :, wb:]
        (p_c, p_n), denom = _softmax_parts((s_c, s_n), sinks_ref[h])
        o = bdot('sqw,sdw->sqd', p_c, cvt[:, ksl, :]) + bdot('sqj,sjd->sqd', p_n, v3[:, :, ksl])
        y_ref[:, hsl] = (o / denom).reshape(s_blk * n_new, HEAD_DIM).astype(y_ref.dtype)
    lane = lax.broadcasted_iota(jnp.int32, (kw, wb), 1)
    for new2, old, out_ref in ((kn2, ckt, nkt_ref), (v2, cvt, nvt_ref)):
        new_t = new2.T
        shifted = pltpu.roll(old.reshape(s_blk * kw, wb), keep, 1).reshape(s_blk, kw, wb)
        _zero_other_layers(out_ref)
        for s in range(s_blk):
            cols = pltpu.roll(new_t, (keep - s * n_new) % wb, 1)
            out_ref[0, s] = jnp.where(lane >= keep, cols, shifted[s])


def _swa_sample(proj2d, cache_kt, cache_vt, layer, prev_new, sinks, qg, kg, bias, n_new, s_blk):
    depth, nseq, kw, wb = cache_kt.shape
    assert s_blk * n_new == wb
    rows = s_blk * n_new
    blk = (rows, MIX_BLOCK)
    cblk = (s_blk, kw, wb)
    cin = pl.BlockSpec((1,) + cblk, lambda i: (layer, i, 0, 0))
    const2 = lambda i: (0, 0)
    prev_k, prev_v = prev_new if prev_new is not None else (None, None)
    kspec, kshape, kin, kin_specs = _layer_state_out(layer, depth, cblk, (nseq, kw, wb), F32, prev_k)
    vspec, vshape, vin, vin_specs = _layer_state_out(layer, depth, cblk, (nseq, kw, wb), F32, prev_v)
    n_main = 10
    aliases = {n_main + j: 1 + j for j in range(len(kin + vin))}
    return pl.pallas_call(
        functools.partial(_swa_sample_kernel, n_new=n_new),
        grid=(nseq // s_blk,),
        in_specs=[pl.BlockSpec(memory_space=pltpu.SMEM),
                  pl.BlockSpec(blk, lambda i: (i, COL_SWA_Q)),
                  pl.BlockSpec(blk, lambda i: (i, COL_SWA_KV)),
                  cin, cin,
                  pl.BlockSpec((1, MIX_BLOCK), const2),
                  pl.BlockSpec((1, kw), const2),
                  pl.BlockSpec((MIX_BLOCK, MIX_BLOCK), const2),
                  pl.BlockSpec((kw, kw), const2),
                  pl.BlockSpec((SWA_HEADS, n_new, wb + n_new), lambda i: (0, 0, 0))] + kin_specs + vin_specs,
        out_specs=[pl.BlockSpec(blk, lambda i: (i, 0)), kspec, vspec],
        out_shape=[jax.ShapeDtypeStruct((nseq * n_new, MIX_BLOCK), BF16), kshape, vshape],
        input_output_aliases=aliases,
        compiler_params=_cparams(("parallel",)),
        name="swa_sample",
    )(sinks, proj2d, proj2d, cache_kt, cache_vt, jnp.tile(qg, (1, SWA_HEADS)), jnp.tile(kg, (1, SWA_KV_HEADS)),
      _head_mean_matrix(MIX_BLOCK), _head_mean_matrix(kw), bias, *kin, *vin)


def _ssm_kernel(u_ref, h0_ref, wb_ref, tab_ref, wc_ref, d_ref, wglu_ref, y_ref, hn_ref,
                bu_scr, carry_scr, *, chained, tiles_per_seq, tb):
    n = SSM_N

    def project_in(rows):
        bu_scr[rows, :] = _dot(u_ref[rows, :].astype(BF16), wb_ref[...])

    def project_out(rows):
        y = _dot(bu_scr[rows, :].astype(BF16), wc_ref[...]) + d_ref[...] * u_ref[rows, :]
        y = 0.5 * y * (1.0 + jnp.tanh(math.sqrt(2.0 / math.pi) * (y + 0.044715 * (y * y * y))))
        return (y * _sigmoid(_dot(y.astype(BF16), wglu_ref[...]))).astype(y_ref.dtype)

    def tile_scan(r0, cr, ci):
        hr = bu_scr[pl.ds(r0, SCAN_ROWS), :n]
        hi = bu_scr[pl.ds(r0, SCAN_ROWS), n:]
        for k, shift in enumerate((1, 2, 4)):
            ar, ai = tab_ref[2 * k], tab_ref[2 * k + 1]
            sr, si = pltpu.roll(hr, shift, 0), pltpu.roll(hi, shift, 0)
            hr, hi = hr + ar * sr - ai * si, hi + ar * si + ai * sr
        pr, pi = tab_ref[6], tab_ref[7]
        hr, hi = hr + pr * cr - pi * ci, hi + pr * ci + pi * cr
        bu_scr[pl.ds(r0, SCAN_ROWS), :n] = hr
        bu_scr[pl.ds(r0, SCAN_ROWS), n:] = hi
        return hr[SCAN_ROWS - 1:], hi[SCAN_ROWS - 1:]

    if chained:
        @pl.when(pl.program_id(1) % tiles_per_seq == 0)
        def _():
            carry_scr[...] = jnp.zeros_like(carry_scr)

        project_in(slice(None))
        cr, ci = carry_scr[:, :n], carry_scr[:, n:]
        for r0 in range(0, tb, SCAN_ROWS):
            lr, li = tile_scan(r0, cr, ci)
            cr, ci = jnp.broadcast_to(lr, (SCAN_ROWS, n)), jnp.broadcast_to(li, (SCAN_ROWS, n))
        y_ref[0] = project_out(slice(None))
        carry_scr[:, :n] = cr
        carry_scr[:, n:] = ci
        hn_ref[0, :, :n] = cr
        hn_ref[0, :, n:] = ci
    else:
        project_in(slice(None))

        def body(t, _):
            r0 = pl.multiple_of(t * SCAN_ROWS, SCAN_ROWS)
            h0 = h0_ref[pl.ds(t, 1), :]
            cr = jnp.broadcast_to(h0[:, :n], (SCAN_ROWS, n))
            ci = jnp.broadcast_to(h0[:, n:], (SCAN_ROWS, n))
            lr, li = tile_scan(r0, cr, ci)
            hn_ref[pl.ds(t, 1), :n] = lr
            hn_ref[pl.ds(t, 1), n:] = li
            return 0

        lax.fori_loop(0, tb // SCAN_ROWS, body, 0)
        y_ref[...] = project_out(slice(None))


def _ssm_common_specs(zero_map2, zero_map3):
    return [pl.BlockSpec((SSM_WIDTH, 2 * SSM_N), zero_map2),
            pl.BlockSpec((8, SCAN_ROWS, SSM_N), zero_map3),
            pl.BlockSpec((2 * SSM_N, SSM_WIDTH), zero_map2),
            pl.BlockSpec((1, SSM_WIDTH), zero_map2),
            pl.BlockSpec((SSM_WIDTH, SSM_WIDTH), zero_map2)]


def _ssm_prompt(proj3d, sp, tb):
    b, l, _ = proj3d.shape
    nt = l // tb
    dummy_h0 = jnp.zeros((SCAN_ROWS, 2 * SSM_N), F32)
    kern = functools.partial(_ssm_kernel, chained=True, tiles_per_seq=nt, tb=tb)

    def kernel(u_ref, h0_ref, wb, tab, wc, d, wglu, y_ref, hn_ref, bu_scr, carry_scr):
        kern(u_ref.at[0], h0_ref, wb, tab, wc, d, wglu, y_ref, hn_ref, bu_scr, carry_scr)

    return pl.pallas_call(
        kernel,
        grid=(b, nt),
        in_specs=[pl.BlockSpec((1, tb, SSM_WIDTH), lambda bi, i: (bi, i, COL_SSM)),
                  pl.BlockSpec((SCAN_ROWS, 2 * SSM_N), lambda bi, i: (0, 0))]
                 + _ssm_common_specs(lambda bi, i: (0, 0), lambda bi, i: (0, 0, 0)),
        out_specs=[pl.BlockSpec((1, tb, SSM_WIDTH), lambda bi, i: (bi, i, 0)),
                   pl.BlockSpec((1, SCAN_ROWS, 2 * SSM_N), lambda bi, i: (bi, 0, 0))],
        out_shape=[jax.ShapeDtypeStruct((b, l, SSM_WIDTH), BF16),
                   jax.ShapeDtypeStruct((b, SCAN_ROWS, 2 * SSM_N), F32)],
        scratch_shapes=[pltpu.VMEM((tb, 2 * SSM_N), F32), pltpu.VMEM((SCAN_ROWS, 2 * SSM_N), F32)],
        compiler_params=_cparams(("parallel", "arbitrary")),
        name="ssm_prompt",
    )(proj3d, dummy_h0, sp["wb"], sp["tab"], sp["wc"], sp["d"], sp["wglu"])


def _ssm_sample(proj2d, h0, sp):
    rows = proj2d.shape[0]
    nseq = h0.shape[0]
    kern = functools.partial(_ssm_kernel, chained=False, tiles_per_seq=1, tb=rows)
    return pl.pallas_call(
        kern,
        grid=(1,),
        in_specs=[pl.BlockSpec((rows, SSM_WIDTH), lambda i: (0, COL_SSM)),
                  pl.BlockSpec((nseq, 2 * SSM_N), lambda i: (0, 0))]
                 + _ssm_common_specs(lambda i: (0, 0), lambda i: (0, 0, 0)),
        out_specs=[pl.BlockSpec((rows, SSM_WIDTH), lambda i: (0, 0)),
                   pl.BlockSpec((nseq, 2 * SSM_N), lambda i: (0, 0))],
        out_shape=[jax.ShapeDtypeStruct((rows, SSM_WIDTH), BF16),
                   jax.ShapeDtypeStruct((nseq, 2 * SSM_N), F32)],
        scratch_shapes=[pltpu.VMEM((rows, 2 * SSM_N), F32), pltpu.VMEM((SCAN_ROWS, 2 * SSM_N), F32)],
        compiler_params=_cparams(("arbitrary",)),
        name="ssm_sample",
    )(proj2d, h0, sp["wb"], sp["tab"], sp["wc"], sp["d"], sp["wglu"])


def _ssm_params(lam_re, lam_im, log_dt, b_re, b_im, c_re, c_im, d_skip, w_glu):
    lr, li = lam_re.astype(F32), lam_im.astype(F32)
    dt = jnp.exp(log_dt.astype(F32))[:, None]
    mag = jnp.exp(lr * dt)
    ab_re, ab_im = mag * jnp.cos(li * dt), mag * jnp.sin(li * dt)
    den = lr * lr + li * li
    nr = ab_re - 1.0
    f_re = (nr * lr + ab_im * li) / den
    f_im = (ab_im * lr - nr * li) / den
    br, bi = b_re.astype(F32), b_im.astype(F32)
    bb_re = f_re[..., None] * br - f_im[..., None] * bi
    bb_im = f_re[..., None] * bi + f_im[..., None] * br
    eye = jnp.eye(SSM_GROUPS, dtype=F32)

    def in_mat(bb):
        return jnp.einsum('gpc,gh->gchp', bb, eye).reshape(SSM_WIDTH, SSM_N)

    def out_mat(c):
        return jnp.einsum('gcp,gh->gphc', c.astype(F32), eye).reshape(SSM_N, SSM_WIDTH)

    wb = jnp.concatenate([in_mat(bb_re), in_mat(bb_im)], axis=1).astype(BF16)
    wc = jnp.concatenate([out_mat(c_re), -out_mat(c_im)], axis=0).astype(BF16)

    ar, ai = ab_re.reshape(1, SSM_N), ab_im.reshape(1, SSM_N)

    def cmul(x, y):
        return (x[0] * y[0] - x[1] * y[1], x[0] * y[1] + x[1] * y[0])

    pw = [(ar, ai)]
    for _ in range(SCAN_ROWS - 1):
        pw.append(cmul(pw[-1], (ar, ai)))
    row = jnp.arange(SCAN_ROWS)[:, None]
    tabs = []
    for shift in (1, 2, 4):
        for part in pw[shift - 1]:
            tabs.append(jnp.where(row >= shift, part, 0.0))
    tabs.append(jnp.concatenate([p[0] for p in pw], axis=0))
    tabs.append(jnp.concatenate([p[1] for p in pw], axis=0))
    tab = jnp.stack([jnp.broadcast_to(t, (SCAN_ROWS, SSM_N)) for t in tabs])
    return dict(wb=wb, tab=tab, wc=wc, d=d_skip.astype(F32).reshape(1, SSM_WIDTH), wglu=w_glu.astype(BF16))


_RET_G = 1.0 - np.exp2(-5.0 - np.arange(RET_HEADS, dtype=np.float64))


def _ret_consts(chunk, n_rows):
    idx = np.arange(n_rows)
    loc = idx % chunk
    same = (idx[:, None] // chunk) == (idx[None, :] // chunk)
    diff = loc[:, None] - loc[None, :]
    dec = np.where(same & (diff >= 0), _RET_G[:, None, None] ** np.maximum(diff, 0)[None], 0.0)
    qdec = np.repeat((_RET_G[None, :] ** (loc[:, None] + 1.0)), HEAD_DIM, axis=1)
    kdec = np.repeat((_RET_G[None, :] ** (chunk - 1.0 - loc[:, None])), HEAD_DIM, axis=1)
    return (jnp.asarray(dec, F32), jnp.asarray(qdec, F32), jnp.asarray(kdec, F32),
            jnp.asarray(_RET_G ** chunk, F32))


def _rope_tables(pos):
    half = HEAD_DIM // 2
    theta = 1.0 / (ROPE_BASE ** np.linspace(0.0, 1.0, half))
    ang = np.asarray(pos, np.float64)[:, None] * theta[None, :]
    cos = np.repeat(np.cos(ang), 2, axis=1)
    sin = np.repeat(np.sin(ang), 2, axis=1) * np.tile([-1.0, 1.0], half)[None]
    return (jnp.asarray(np.tile(cos, (1, RET_HEADS)), F32), jnp.asarray(np.tile(sin, (1, RET_HEADS)), F32))


def _rotate_pairs(x, cos, sin_signed):
    lane = lax.broadcasted_iota(jnp.int32, x.shape, 1)
    nxt = pltpu.roll(x, x.shape[1] - 1, 1)
    prv = pltpu.roll(x, 1, 1)
    return x * cos + jnp.where(lane % 2 == 0, nxt, prv) * sin_signed


def _ret_head_out(o, gate, norm):
    ms = jnp.mean(o * o, axis=-1, keepdims=True)
    return o * lax.rsqrt(ms + RMS_EPS) * norm * (gate * _sigmoid(gate))


RET_TILE = 1024


def _ret_prompt_kernel(gc_ref, q_ref, k_ref, v_ref, g_ref, cos_ref, sin_ref, dec_ref, qdec_ref,
                       kdec_ref, norm_ref, mh_ref, y_ref, r_ref, o_scr):
    @pl.when(pl.program_id(1) == 0)
    def _():
        r_ref[...] = jnp.zeros_like(r_ref)

    cos, sin = cos_ref[...], sin_ref[...]
    q = _rotate_pairs(q_ref[0], cos, sin)
    k = _rotate_pairs(k_ref[0], cos, sin) * (HEAD_DIM ** -0.5)
    qb, kb, vb = q.astype(BF16), k.astype(BF16), v_ref[0].astype(BF16)
    kdb = (k * kdec_ref[...]).astype(BF16)
    for c in range(RET_TILE // RET_CHUNK):
        rows = slice(c * RET_CHUNK, (c + 1) * RET_CHUNK)
        for h in range(RET_HEADS):
            sl = slice(h * HEAD_DIM, (h + 1) * HEAD_DIM)
            qh, vh = qb[rows, sl], vb[rows, sl]
            s = _dot_nt(qh, kb[rows, sl]) * dec_ref[h]
            r = r_ref[0, h]
            o_scr[rows, sl] = _dot(s.astype(BF16), vh) + _dot(qh, r.astype(BF16)) * qdec_ref[rows, sl]
            r_ref[0, h] = gc_ref[h] * r + _dot_tn(kdb[rows, sl], vh)
    g = g_ref[0]
    y_ref[0] = (_head_rms(o_scr[...], mh_ref[...], norm_ref[...]) * (g * _sigmoid(g))).astype(y_ref.dtype)


def _ret_prompt(proj3d, cos, sin, norm):
    b, l, _ = proj3d.shape
    c = RET_CHUNK
    dec, _, _, gc = _ret_consts(c, c)
    _, qdec, kdec, _ = _ret_consts(c, RET_TILE)
    blk = (1, RET_TILE, MIX_BLOCK)
    tspec = pl.BlockSpec((RET_TILE, MIX_BLOCK), lambda bi, i: (i, 0))
    cspec = pl.BlockSpec((RET_TILE, MIX_BLOCK), lambda bi, i: (0, 0))
    return pl.pallas_call(
        _ret_prompt_kernel,
        grid=(b, l // RET_TILE),
        in_specs=[pl.BlockSpec(memory_space=pltpu.SMEM),
                  pl.BlockSpec(blk, lambda bi, i: (bi, i, COL_RET_Q)),
                  pl.BlockSpec(blk, lambda bi, i: (bi, i, COL_RET_K)),
                  pl.BlockSpec(blk, lambda bi, i: (bi, i, COL_RET_V)),
                  pl.BlockSpec(blk, lambda bi, i: (bi, i, COL_RET_G)),
                  tspec, tspec,
                  pl.BlockSpec((RET_HEADS, c, c), lambda bi, i: (0, 0, 0)),
                  cspec, cspec,
                  pl.BlockSpec((1, MIX_BLOCK), lambda bi, i: (0, 0)),
                  pl.BlockSpec((MIX_BLOCK, MIX_BLOCK), lambda bi, i: (0, 0))],
        out_specs=[pl.BlockSpec(blk, lambda bi, i: (bi, i, 0)),
                   pl.BlockSpec((1, RET_HEADS, HEAD_DIM, HEAD_DIM), lambda bi, i: (bi, 0, 0, 0))],
        out_shape=[jax.ShapeDtypeStruct((b, l, MIX_BLOCK), BF16),
                   jax.ShapeDtypeStruct((b, RET_HEADS, HEAD_DIM, HEAD_DIM), F32)],
        scratch_shapes=[pltpu.VMEM((RET_TILE, MIX_BLOCK), F32)],
        compiler_params=_cparams(("parallel", "arbitrary")),
        name="ret_prompt",
    )(gc, proj3d, proj3d, proj3d, proj3d, cos, sin, dec, qdec, kdec, norm, _head_mean_matrix(MIX_BLOCK))


def _ret_sample_kernel(gc_ref, q_ref, k_ref, v_ref, g_ref, cos_ref, sin_ref, dec_ref, qdec_ref,
                       kdec_ref, norm_ref, r0_ref, *rest, n_new, s_blk):
    y_ref, rn_ref = rest[-2:]
    _zero_other_layers(rn_ref)
    cos, sin = cos_ref[...], sin_ref[...]
    q = _rotate_pairs(q_ref[...], cos, sin)
    k = _rotate_pairs(k_ref[...], cos, sin) * (HEAD_DIM ** -0.5)
    v = v_ref[...]
    g = g_ref[...]
    kd = k * kdec_ref[...]
    qdec = qdec_ref[...]
    norm = norm_ref[...]
    rows = s_blk * n_new
    seq = lax.broadcasted_iota(jnp.int32, (rows, HEAD_DIM), 0) // n_new
    for h in range(RET_HEADS):
        sl = slice(h * HEAD_DIM, (h + 1) * HEAD_DIM)
        qf, kdf = q[:, sl], kd[:, sl]
        qh, kh, vh = qf.astype(BF16), k[:, sl].astype(BF16), v[:, sl].astype(BF16)
        s = _dot_nt(qh, kh) * dec_ref[h]
        cross = jnp.zeros((rows, HEAD_DIM), F32)
        for si in range(s_blk):
            mine = seq == si
            r = r0_ref[si, h]
            cross = cross + _dot(jnp.where(mine, qf, 0.0).astype(BF16), r.astype(BF16))
            rn_ref[0, si, h] = gc_ref[h] * r + _dot_tn(jnp.where(mine, kdf, 0.0).astype(BF16), vh)
        o = _dot(s.astype(BF16), vh) + cross * qdec[:, sl]
        y_ref[:, sl] = _ret_head_out(o, g[:, sl], norm[:, sl]).astype(y_ref.dtype)


def _ret_sample(proj2d, r0, layer, depth, prev_new, cos, sin, norm, n_new, s_blk):
    rows = s_blk * n_new
    nseq = r0.shape[0]
    dec, qdec, kdec, gc = _ret_consts(n_new, rows)
    blk = (rows, MIX_BLOCK)
    cspec = pl.BlockSpec(blk, lambda i: (0, 0))
    rblk = (s_blk, RET_HEADS, HEAD_DIM, HEAD_DIM)
    rspec, rshape, rin, rin_specs = _layer_state_out(layer, depth, rblk, (nseq,) + rblk[1:], F32, prev_new)
    return pl.pallas_call(
        functools.partial(_ret_sample_kernel, n_new=n_new, s_blk=s_blk),
        grid=(nseq // s_blk,),
        in_specs=[pl.BlockSpec(memory_space=pltpu.SMEM),
                  pl.BlockSpec(blk, lambda i: (i, COL_RET_Q)),
                  pl.BlockSpec(blk, lambda i: (i, COL_RET_K)),
                  pl.BlockSpec(blk, lambda i: (i, COL_RET_V)),
                  pl.BlockSpec(blk, lambda i: (i, COL_RET_G)),
                  cspec, cspec,
                  pl.BlockSpec((RET_HEADS, rows, rows), lambda i: (0, 0, 0)),
                  cspec, cspec,
                  pl.BlockSpec((1, MIX_BLOCK), lambda i: (0, 0)),
                  pl.BlockSpec(rblk, lambda i: (i, 0, 0, 0))] + rin_specs,
        out_specs=[pl.BlockSpec(blk, lambda i: (i, 0)), rspec],
        out_shape=[jax.ShapeDtypeStruct((nseq * n_new, MIX_BLOCK), BF16), rshape],
        input_output_aliases={12 + j: 1 for j in range(len(rin))},
        compiler_params=_cparams(("parallel",)),
        name="ret_sample",
    )(gc, proj2d, proj2d, proj2d, proj2d, cos, sin, dec, qdec, kdec, norm, r0, *rin)


def _block_diag(w):
    g, n, _ = w.shape
    return jnp.einsum('gcd,gh->gchd', w, jnp.eye(g, dtype=w.dtype)).reshape(g * n, g * n)


def _layer_params(l, p):
    return dict(
        norm_mix=p['norm_mix'][l].reshape(1, D_MODEL),
        norm_ffn=p['norm_ffn'][l].reshape(1, D_MODEL),
        w_in=p['w_in'][l].astype(BF16),
        w_out=p['w_out'][l].astype(BF16),
        pool_w=_block_diag(p['pool_w'][l].astype(F32)).astype(BF16),
        pool_scale=p['pool_scale'][l].astype(F32).reshape(1, POOL_WIDTH),
        qg=p['swa_q_norm'][l].astype(F32).reshape(1, HEAD_DIM),
        kg=p['swa_k_norm'][l].astype(F32).reshape(1, HEAD_DIM),
        sinks=p['swa_sinks'][l].astype(F32),
        ssm=_ssm_params(p['ssm_lambda_re'][l], p['ssm_lambda_im'][l], p['ssm_log_dt'][l],
                        p['ssm_b_re'][l], p['ssm_b_im'][l], p['ssm_c_re'][l], p['ssm_c_im'][l],
                        p['ssm_d'][l], p['ssm_w_glu'][l]),
        ret_norm=p['ret_norm'][l].astype(F32).reshape(1, MIX_BLOCK),
    )


def _channel_mix(streams, l, lp, p):
    i = l // 2
    g, w_out = lp['norm_ffn'], lp['w_out']
    if l % 2 == 0:
        wg, wu, wd = (p[k][i].astype(BF16) for k in ('ffn_w_gate', 'ffn_w_up', 'ffn_w_down'))
        return [_out_proj_ffn(x, ys, w_out, g, wg, wu, wd, 512, D_FF) for x, ys in streams]
    wr3 = _router_weights(p['moe_router'][i])
    wg, wu, wd = (p[k][i].astype(BF16) for k in ('moe_w_gate', 'moe_w_up', 'moe_w_down'))
    x1s, routes = zip(*[_out_proj_router(x, ys, w_out, g, wr3, 512) for x, ys in streams])
    return _moe(x1s, routes, g, wg, wu, wd)


def _mix_prompt(x2, b, l, lp, bias, cos, sin):
    proj2 = _norm_matmul(x2, lp['norm_mix'], lp['w_in'], 1024)
    proj3 = proj2.reshape(b, l, IN_WIDTH)
    tb = min(2048, l)
    y_pool = _pool(proj2, COL_POOL, lp['pool_w'], lp['pool_scale'], n_rows=b * l, tb=tb,
                   tiles_per_seq=l // tb, pos0=0)
    y_swa, kn = _swa_prompt(proj3, lp['sinks'], lp['qg'], lp['kg'], bias)
    y_ssm, hn = _ssm_prompt(proj3, lp['ssm'], min(1024, l))
    y_ret, rn = _ret_prompt(proj3, cos, sin, lp['ret_norm'])
    ys = (y_pool, y_swa.reshape(b * l, MIX_BLOCK), y_ssm.reshape(b * l, MIX_BLOCK),
          y_ret.reshape(b * l, MIX_BLOCK))
    kw = SWA_KV_HEADS * HEAD_DIM
    hn = hn[:, 0]
    states = (proj3[:, l - POOL_BUF:, :POOL_WIDTH],
              kn.reshape(b, SWA_WINDOW, SWA_KV_HEADS, HEAD_DIM),
              proj3[:, l - SWA_WINDOW:, COL_SWA_KV * MIX_BLOCK + kw:(COL_SWA_KV + 1) * MIX_BLOCK]
              .reshape(b, SWA_WINDOW, SWA_KV_HEADS, HEAD_DIM),
              jnp.stack([hn[:, :SSM_N], hn[:, SSM_N:]], axis=-1).reshape(b, SSM_GROUPS, SSM_STATE, 2),
              rn)
    return ys, states


SAMPLE_SEQ_BLOCK = 16


def _cache_transposed(cache):
    depth, nseq, wb = cache.shape[:3]
    return jnp.swapaxes(cache.astype(F32).reshape(depth, nseq, wb, SWA_KV_HEADS * HEAD_DIM), 2, 3)


def _mix_sample(x2, nseq, n_new, start_pos, lp, layer, st, prev_new, bias, cos, sin):
    state_pool, cache_kt, cache_vt, state_ssm, state_ret = st
    prev_kv, prev_ret = prev_new if prev_new is not None else (None, None)
    rows = nseq * n_new
    kw = SWA_KV_HEADS * HEAD_DIM
    ext_rows = POOL_HALO + n_new
    proj2 = _norm_matmul(x2, lp['norm_mix'], lp['w_in'], 512)
    proj3 = proj2.reshape(nseq, n_new, IN_WIDTH)
    u_pool = proj3[:, :, :POOL_WIDTH]
    buf = state_pool.astype(F32)
    ext = jnp.concatenate([jnp.zeros((nseq, POOL_HALO - POOL_BUF, POOL_WIDTH), F32), buf, u_pool], axis=1)
    y_pool = _pool(ext.reshape(nseq * ext_rows, POOL_WIDTH), 0, lp['pool_w'], lp['pool_scale'],
                   n_rows=nseq * ext_rows, tb=nseq * ext_rows, tiles_per_seq=1, pos0=start_pos)
    y_pool = y_pool.reshape(nseq, ext_rows, POOL_WIDTH)[:, POOL_HALO:].reshape(rows, POOL_WIDTH)
    y_swa, nkt, nvt = _swa_sample(proj2, cache_kt, cache_vt, layer, prev_kv, lp['sinks'], lp['qg'], lp['kg'],
                                  bias, n_new, SAMPLE_SEQ_BLOCK)
    h0 = state_ssm.astype(F32).reshape(nseq, SSM_N, 2)
    h0 = jnp.concatenate([h0[..., 0], h0[..., 1]], axis=1)
    y_ssm, hn = _ssm_sample(proj2, h0, lp['ssm'])
    y_ret, rn = _ret_sample(proj2, state_ret[layer], layer, state_ret.shape[0], prev_ret, cos, sin,
                            lp['ret_norm'], n_new, SAMPLE_SEQ_BLOCK)
    ys = (y_pool, y_swa, y_ssm, y_ret)
    states = (jnp.concatenate([buf, u_pool], axis=1)[:, -POOL_BUF:],
              jnp.stack([hn[:, :SSM_N], hn[:, SSM_N:]], axis=-1).reshape(nseq, SSM_GROUPS, SSM_STATE, 2))
    return ys, states, ((nkt, nvt), rn)


def _forward(x_prompt, x_sample, past_len, sample_state, p, rel_bias):
    b, l, d = x_prompt.shape
    nseq, n_new, _ = x_sample.shape
    wb = sample_state[1].shape[2]
    depth = p['norm_mix'].shape[0]
    bias_p = _swa_bias(rel_bias, np.arange(SWA_BLOCK)[:, None] - np.arange(2 * SWA_BLOCK)[None, :] + SWA_BLOCK)
    bias_s = _swa_bias(rel_bias, np.arange(n_new)[:, None] - np.arange(wb + n_new)[None, :] + wb)
    rope_p = _rope_tables(np.arange(l))
    rope_s = _rope_tables(past_len + (np.arange(SAMPLE_SEQ_BLOCK * n_new) % n_new))
    xp = x_prompt.reshape(b * l, d)
    xs = x_sample.reshape(nseq * n_new, d)
    cache_kt, cache_vt = _cache_transposed(sample_state[1]), _cache_transposed(sample_state[2])
    st_p, st_s, stacked = [], [], None
    for li in range(depth):
        lp = _layer_params(li, p)
        yp, sp = _mix_prompt(xp, b, l, lp, bias_p, *rope_p)
        st = (sample_state[0][li], cache_kt, cache_vt, sample_state[3][li], sample_state[4])
        ys, ss, stacked = _mix_sample(xs, nseq, n_new, past_len, lp, li, st, stacked, bias_s, *rope_s)
        xp, xs = _channel_mix([(xp, yp), (xs, ys)], li, lp, p)
        st_p.append(sp)
        st_s.append(ss)
    (new_kt, new_vt), new_ret = stacked
    kv_shape = (depth, nseq, wb, SWA_KV_HEADS, HEAD_DIM)
    sample_out = (jnp.stack([s[0] for s in st_s]),
                  jnp.swapaxes(new_kt, 2, 3).reshape(kv_shape),
                  jnp.swapaxes(new_vt, 2, 3).reshape(kv_shape),
                  jnp.stack([s[1] for s in st_s]),
                  new_ret)
    outs = [xp.reshape(b, l, d), xs.reshape(nseq, n_new, d)]
    for k in range(5):
        outs.append(jnp.stack([s[k] for s in st_p]))
        outs.append(sample_out[k])
    return tuple(outs)


PAST_LEN = 16384


def kernel(x_prompt, x_sample, state_pool, cache_swa_k, cache_swa_v, state_ssm, state_ret,
           norm_mix, norm_ffn, w_in, w_out, pool_w, pool_scale, swa_q_norm, swa_k_norm, swa_sinks,
           rel_bias, ssm_lambda_re, ssm_lambda_im, ssm_log_dt, ssm_b_re, ssm_b_im, ssm_c_re, ssm_c_im,
           ssm_d, ssm_w_glu, ret_norm, ffn_w_gate, ffn_w_up, ffn_w_down, moe_router, moe_w_gate,
           moe_w_up, moe_w_down):
    p = dict(norm_mix=norm_mix, norm_ffn=norm_ffn, w_in=w_in, w_out=w_out, pool_w=pool_w,
             pool_scale=pool_scale, swa_q_norm=swa_q_norm, swa_k_norm=swa_k_norm, swa_sinks=swa_sinks,
             ssm_lambda_re=ssm_lambda_re, ssm_lambda_im=ssm_lambda_im, ssm_log_dt=ssm_log_dt,
             ssm_b_re=ssm_b_re, ssm_b_im=ssm_b_im, ssm_c_re=ssm_c_re, ssm_c_im=ssm_c_im,
             ssm_d=ssm_d, ssm_w_glu=ssm_w_glu, ret_norm=ret_norm,
             ffn_w_gate=ffn_w_gate, ffn_w_up=ffn_w_up, ffn_w_down=ffn_w_down, moe_router=moe_router,
             moe_w_gate=moe_w_gate, moe_w_up=moe_w_up, moe_w_down=moe_w_down)
    return _forward(x_prompt, x_sample, PAST_LEN,
                    (state_pool, cache_swa_k, cache_swa_v, state_ssm, state_ret), p, rel_bias)
```

```python
import functools
import math

import numpy as np
import jax
import jax.numpy as jnp
from jax import lax
from jax.experimental import pallas as pl
from jax.experimental.pallas import tpu as pltpu

F32 = jnp.float32
BF16 = jnp.bfloat16

D_MODEL = 1024
HEAD_DIM = 64
POOL_WIDTH = 256
POOL_WINDOWS = (2, 4, 8, 16)
POOL_BUF = 15
POOL_HALO = 16
SWA_HEADS = 4
SWA_KV_HEADS = 2
SWA_WINDOW = 128
SWA_BLOCK = 128
SSM_WIDTH = 256
SSM_CH = 16
SSM_GROUPS = 16
SSM_STATE = 64
SSM_N = SSM_GROUPS * SSM_STATE
RET_HEADS = 4
RET_CHUNK = 128
ROPE_BASE = 10000.0
IN_WIDTH = 2048
MIX_BLOCK = 256
D_FF = 2816
N_EXPERTS = 8
T5_BUCKETS = 32
T5_MAX_DIST = 128
RMS_EPS = 1e-6
NEG = -1e30
SUBLANES = 8
SCAN_ROWS = SUBLANES

COL_POOL, COL_SWA_Q, COL_SWA_KV, COL_SSM, COL_RET_Q, COL_RET_K, COL_RET_V, COL_RET_G = range(8)

VMEM_LIMIT = 48 * 1024 * 1024
FFN_VMEM_LIMIT = 58 * 1024 * 1024


def _cparams(sem, vmem=VMEM_LIMIT):
    return pltpu.CompilerParams(dimension_semantics=sem, vmem_limit_bytes=vmem)


def _rms(x, g):
    ms = jnp.mean(x * x, axis=-1, keepdims=True)
    return x * lax.rsqrt(ms + RMS_EPS) * g


def _dot(a, b):
    return jnp.dot(a, b, preferred_element_type=F32)


def _dot_nt(a, b):
    return lax.dot_general(a, b, (((1,), (1,)), ((), ())), preferred_element_type=F32)


def _dot_tn(a, b):
    return lax.dot_general(a, b, (((0,), (0,)), ((), ())), preferred_element_type=F32)


def _sigmoid(x):
    return 1.0 / (1.0 + jnp.exp(-x))


def _norm_matmul_kernel(x_ref, g_ref, w_ref, o_ref):
    h = _rms(x_ref[...], g_ref[...]).astype(BF16)
    o_ref[...] = _dot(h, w_ref[...])


def _norm_matmul(x, g, w, tm):
    t, d = x.shape
    tm = min(tm, t)
    n = w.shape[1]
    return pl.pallas_call(
        _norm_matmul_kernel,
        grid=(t // tm,),
        in_specs=[pl.BlockSpec((tm, d), lambda i: (i, 0)),
                  pl.BlockSpec((1, d), lambda i: (0, 0)),
                  pl.BlockSpec((d, n), lambda i: (0, 0))],
        out_specs=pl.BlockSpec((tm, n), lambda i: (i, 0)),
        out_shape=jax.ShapeDtypeStruct((t, n), F32),
        compiler_params=_cparams(("parallel",)),
        name="norm_matmul",
    )(x, g, w)


FFN_SUBCHUNK = 512


def _swiglu_chunk(h, wg_ref, wu_ref, wd_ref):
    tf = wg_ref.shape[1]
    y = None
    for lo in range(0, tf, FFN_SUBCHUNK):
        hi = min(lo + FFN_SUBCHUNK, tf)
        a = _dot(h, wg_ref[:, lo:hi])
        b = _dot(h, wu_ref[:, lo:hi])
        part = _dot((a * _sigmoid(a) * b).astype(BF16), wd_ref[lo:hi, :])
        y = part if y is None else y + part
    return y


def _mixed_residual(x_ref, y_refs, w_ref, rows=slice(None)):
    y = jnp.concatenate([y_ref[rows, :] for y_ref in y_refs], axis=1)
    return x_ref[rows, :] + _dot(y, w_ref[...])


def _mix_in_specs(tm, d, imap):
    yspec = pl.BlockSpec((tm, MIX_BLOCK), imap(lambda i: (i, 0)))
    return [pl.BlockSpec((tm, d), imap(lambda i: (i, 0))), yspec, yspec, yspec, yspec,
            pl.BlockSpec((d, d), imap(lambda i: (0, 0)))]


def _out_proj_ffn_kernel(x_ref, y0_ref, y1_ref, y2_ref, y3_ref, wo_ref, g_ref, wg_ref, wu_ref, wd_ref,
                         o_ref, h_scr):
    @pl.when(pl.program_id(1) == 0)
    def _():
        x1 = _mixed_residual(x_ref, (y0_ref, y1_ref, y2_ref, y3_ref), wo_ref)
        h_scr[...] = _rms(x1, g_ref[...]).astype(BF16)
        o_ref[...] = x1

    o_ref[...] += _swiglu_chunk(h_scr[...], wg_ref, wu_ref, wd_ref)


def _out_proj_ffn(x, ys, w_out, g, wg, wu, wd, tm, tf):
    t, d = x.shape
    tm = min(tm, t)
    f = wg.shape[1]
    imap = lambda fn: (lambda i, j: fn(i))
    once = dict(pipeline_mode=pl.Buffered(1)) if tf == f else {}
    return pl.pallas_call(
        _out_proj_ffn_kernel,
        grid=(t // tm, f // tf),
        in_specs=_mix_in_specs(tm, d, imap)
                 + [pl.BlockSpec((1, d), lambda i, j: (0, 0)),
                    pl.BlockSpec((d, tf), lambda i, j: (0, j), **once),
                    pl.BlockSpec((d, tf), lambda i, j: (0, j), **once),
                    pl.BlockSpec((tf, d), lambda i, j: (j, 0), **once)],
        out_specs=pl.BlockSpec((tm, d), lambda i, j: (i, 0)),
        out_shape=jax.ShapeDtypeStruct((t, d), F32),
        scratch_shapes=[pltpu.VMEM((tm, d), BF16)],
        compiler_params=_cparams(("parallel", "arbitrary"), FFN_VMEM_LIMIT),
        name="out_proj_ffn",
    )(x, *ys, w_out, g, wg, wu, wd)


ROUTE_ID_LANES = (0, 1)
ROUTE_GATE_LANES = (2, 3)


def _split_bf16(x):
    hi = x.astype(BF16)
    return hi, (x - hi.astype(F32)).astype(BF16)


ROUTER_ROWS = 256


def _out_proj_router_kernel(x_ref, y0_ref, y1_ref, y2_ref, y3_ref, wo_ref, g_ref, wr_ref, x1_ref, c_ref):
    tm = x_ref.shape[0]
    step = min(ROUTER_ROWS, tm)
    for r in range(0, tm, step):
        _route_rows(slice(r, r + step), x_ref, (y0_ref, y1_ref, y2_ref, y3_ref), wo_ref, g_ref, wr_ref,
                    x1_ref, c_ref)


def _route_rows(rows, x_ref, y_refs, wo_ref, g_ref, wr_ref, x1_ref, c_ref):
    x1 = _mixed_residual(x_ref, y_refs, wo_ref, rows)
    x1_ref[rows, :] = x1
    h_hi, h_lo = _split_bf16(_rms(x1, g_ref[...]))
    logits = _dot(jnp.concatenate([h_hi, h_lo, h_hi], axis=1), wr_ref[...])
    lane = lax.broadcasted_iota(jnp.int32, logits.shape, 1).astype(F32)
    lg = jnp.where(lane < N_EXPERTS, logits, NEG)
    m1 = jnp.max(lg, axis=-1, keepdims=True)
    i1 = jnp.min(jnp.where(lg == m1, lane, 128.0), axis=-1, keepdims=True)
    lg2 = jnp.where(lane == i1, NEG, lg)
    m2 = jnp.max(lg2, axis=-1, keepdims=True)
    i2 = jnp.min(jnp.where(lg2 == m2, lane, 128.0), axis=-1, keepdims=True)
    ex = jnp.exp(m2 - m1)
    vals = (i1, i2, 1.0 / (1.0 + ex), ex / (1.0 + ex))
    out = jnp.zeros_like(logits)
    for ln, v in zip(ROUTE_ID_LANES + ROUTE_GATE_LANES, vals):
        out = jnp.where(lane == ln, v, out)
    c_ref[rows, :] = out


def _router_weights(wr):
    w = jnp.pad(wr.astype(F32), ((0, 0), (0, 128 - N_EXPERTS)))
    hi, lo = _split_bf16(w)
    return jnp.concatenate([hi, hi, lo], axis=0)


def _out_proj_router(x, ys, w_out, g, wr3, tm):
    t, d = x.shape
    tm = min(tm, t)
    imap = lambda fn: fn
    return pl.pallas_call(
        _out_proj_router_kernel,
        grid=(t // tm,),
        in_specs=_mix_in_specs(tm, d, imap)
                 + [pl.BlockSpec((1, d), lambda i: (0, 0)),
                    pl.BlockSpec((3 * d, 128), lambda i: (0, 0))],
        out_specs=[pl.BlockSpec((tm, d), lambda i: (i, 0)),
                   pl.BlockSpec((tm, 128), lambda i: (i, 0))],
        out_shape=[jax.ShapeDtypeStruct((t, d), F32), jax.ShapeDtypeStruct((t, 128), F32)],
        compiler_params=_cparams(("parallel",)),
        name="out_proj_router",
    )(x, *ys, w_out, g, wr3)


DMA_ISSUE_UNROLL = 8


def _row_copy(src, i, dst, j, sem):
    return pltpu.make_async_copy(src.at[pl.ds(i, 1)], dst.at[pl.ds(j, 1)], sem)


def _dispatch_kernel(meta_ref, pos_ref, *rest, td, tm, n_tiles, first_step):
    n_streams = len(first_step) - 1
    x_refs, (xs_hbm, zero_scr, sem) = rest[:n_streams], rest[n_streams:]
    step = pl.program_id(0)

    def zero_row(r):
        return _row_copy(zero_scr, 0, xs_hbm, r, sem)

    @pl.when(step == 0)
    def _():
        zero_scr[...] = jnp.zeros_like(zero_scr)
        n_used = meta_ref[2 * N_EXPERTS]
        tile_fills = [(i >= n_used, pltpu.make_async_copy(zero_scr, xs_hbm.at[pl.ds(i * tm, tm)], sem))
                      for i in range(n_tiles)]
        for cond, copy in tile_fills:
            pl.when(cond)(copy.start)
        for e in range(N_EXPERTS):
            lax.fori_loop(meta_ref[e], meta_ref[N_EXPERTS + e], lambda r, c: (zero_row(r).start(), c)[1], 0)
        for e in range(N_EXPERTS):
            lax.fori_loop(meta_ref[e], meta_ref[N_EXPERTS + e], lambda r, c: (zero_row(r).wait(), c)[1], 0)
        for cond, copy in tile_fills:
            pl.when(cond)(copy.wait)

    def scatter(x_ref):
        def issue(j, c):
            for k in range(2):
                _row_copy(x_ref, j, xs_hbm, pos_ref[0, 0, 2 * j + k], sem).start()
            return c

        lax.fori_loop(0, td, issue, 0, unroll=DMA_ISSUE_UNROLL)
        for _ in range(2):
            pltpu.make_async_copy(x_ref, xs_hbm.at[pl.ds(0, td)], sem).wait()

    for s, x_ref in enumerate(x_refs):
        pl.when((step >= first_step[s]) & (step < first_step[s + 1]))(functools.partial(scatter, x_ref))


def _dispatch(xs_list, pos, meta, n_rows, tm, td):
    d = xs_list[0].shape[1]
    td = min([td] + [x.shape[0] for x in xs_list])
    first_step = [0]
    for x in xs_list:
        first_step.append(first_step[-1] + x.shape[0] // td)
    n_steps = first_step[-1]
    pos3 = pos.reshape(n_steps, 1, 2 * td)

    def tile_map(s):
        lo, hi = first_step[s], first_step[s + 1]
        return lambda i, m: (jnp.clip(i, lo, hi - 1) - lo, 0)

    in_specs = [pl.BlockSpec((1, 1, 2 * td), lambda i, m: (i, 0, 0), memory_space=pltpu.SMEM)]
    in_specs += [pl.BlockSpec((td, d), tile_map(s)) for s in range(len(xs_list))]
    return pl.pallas_call(
        functools.partial(_dispatch_kernel, td=td, tm=tm, n_tiles=n_rows // tm, first_step=tuple(first_step)),
        grid_spec=pltpu.PrefetchScalarGridSpec(
            num_scalar_prefetch=1, grid=(n_steps,), in_specs=in_specs,
            out_specs=pl.BlockSpec(memory_space=pl.ANY),
            scratch_shapes=[pltpu.VMEM((tm, d), F32), pltpu.SemaphoreType.DMA]),
        out_shape=jax.ShapeDtypeStruct((n_rows, d), F32),
        compiler_params=_cparams(("arbitrary",)),
        name="moe_dispatch",
    )(meta, pos3, *xs_list)


def _grouped_ffn_kernel(te_ref, nu_ref, x_ref, g_ref, wg_ref, wu_ref, wd_ref, o_ref, h_scr):
    del te_ref
    j = pl.program_id(1)
    used = pl.program_id(0) < nu_ref[0]

    @pl.when(jnp.logical_not(used) & (j == 0))
    def _():
        o_ref[...] = jnp.zeros_like(o_ref)

    @pl.when(used)
    def _():
        @pl.when(j == 0)
        def _():
            h_scr[...] = _rms(x_ref[...], g_ref[...]).astype(BF16)

        y = _swiglu_chunk(h_scr[...], wg_ref.at[0], wu_ref.at[0], wd_ref.at[0])

        @pl.when(j == 0)
        def _():
            o_ref[...] = y

        @pl.when(j > 0)
        def _():
            o_ref[...] += y


def _grouped_ffn(xs, g, tile_expert, n_used, wg, wu, wd, tm, tf):
    r, d = xs.shape
    f = wg.shape[2]
    nj = f // tf

    def row_map(i, j, te, nu):
        return (i, 0)

    def col_of(i, j, nu):
        return jnp.where(i < nu[0], j, nj - 1)

    grid_spec = pltpu.PrefetchScalarGridSpec(
        num_scalar_prefetch=2,
        grid=(r // tm, nj),
        in_specs=[pl.BlockSpec((tm, d), row_map),
                  pl.BlockSpec((1, d), lambda i, j, te, nu: (0, 0)),
                  pl.BlockSpec((1, d, tf), lambda i, j, te, nu: (te[i], 0, col_of(i, j, nu))),
                  pl.BlockSpec((1, d, tf), lambda i, j, te, nu: (te[i], 0, col_of(i, j, nu))),
                  pl.BlockSpec((1, tf, d), lambda i, j, te, nu: (te[i], col_of(i, j, nu), 0))],
        out_specs=pl.BlockSpec((tm, d), row_map),
        scratch_shapes=[pltpu.VMEM((tm, d), BF16)],
    )
    return pl.pallas_call(
        _grouped_ffn_kernel,
        grid_spec=grid_spec,
        out_shape=jax.ShapeDtypeStruct((r, d), F32),
        compiler_params=_cparams(("arbitrary", "arbitrary"), FFN_VMEM_LIMIT),
        name="moe_grouped_ffn",
    )(tile_expert, n_used, xs, g, wg, wu, wd)


def _combine_kernel(pos_ref, pos_next_ref, x_ref, route_ref, ys_hbm, o_ref, buf0, buf1, sems, *, tc):
    step = pl.program_id(0)
    slot = step % 2

    def gather(p_ref, s):
        def issue(j, c):
            _row_copy(ys_hbm, p_ref[0, 0, 2 * j], buf0.at[s], j, sems.at[s]).start()
            _row_copy(ys_hbm, p_ref[0, 0, 2 * j + 1], buf1.at[s], j, sems.at[s]).start()
            return c

        lax.fori_loop(0, tc, issue, 0, unroll=DMA_ISSUE_UNROLL)

    pl.when(step == 0)(functools.partial(gather, pos_ref, 0))
    pl.when(step + 1 < pl.num_programs(0))(functools.partial(gather, pos_next_ref, 1 - slot))
    for buf in (buf0, buf1):
        pltpu.make_async_copy(ys_hbm.at[pl.ds(0, tc)], buf.at[slot], sems.at[slot]).wait()
    route = route_ref[...]
    g0 = route[:, ROUTE_GATE_LANES[0]:ROUTE_GATE_LANES[0] + 1]
    g1 = route[:, ROUTE_GATE_LANES[1]:ROUTE_GATE_LANES[1] + 1]
    o_ref[...] = x_ref[...] + g0 * buf0[slot] + g1 * buf1[slot]


def _combine(x, route, pos, ys, tc):
    t, d = x.shape
    tc = min(tc, t)
    n = t // tc
    pos3 = pos.reshape(n, 1, 2 * tc)
    pos_block = (1, 1, 2 * tc)
    return pl.pallas_call(
        functools.partial(_combine_kernel, tc=tc),
        grid=(n,),
        in_specs=[pl.BlockSpec(pos_block, lambda i: (i, 0, 0), memory_space=pltpu.SMEM),
                  pl.BlockSpec(pos_block, lambda i: (jnp.minimum(i + 1, n - 1), 0, 0), memory_space=pltpu.SMEM),
                  pl.BlockSpec((tc, d), lambda i: (i, 0)),
                  pl.BlockSpec((tc, 128), lambda i: (i, 0)),
                  pl.BlockSpec(memory_space=pl.ANY)],
        out_specs=pl.BlockSpec((tc, d), lambda i: (i, 0)),
        out_shape=jax.ShapeDtypeStruct((t, d), F32),
        scratch_shapes=[pltpu.VMEM((2, tc, d), F32), pltpu.VMEM((2, tc, d), F32),
                        pltpu.SemaphoreType.DMA((2,))],
        compiler_params=_cparams(("arbitrary",)),
        name="moe_combine",
    )(pos3, pos3, x, route, ys)


MOE_TM = 512


def _route_plan(expert_ids, tm):
    flat = expert_ids.reshape(-1)
    a = flat.shape[0]
    onehot = (flat[None, :] == jnp.arange(N_EXPERTS, dtype=jnp.int32)[:, None]).astype(jnp.int32)
    csum = jnp.cumsum(onehot, axis=1)
    counts = csum[:, -1]
    padded = (counts + tm - 1) // tm * tm
    ends = jnp.cumsum(padded)
    offs = ends - padded
    pos = jnp.sum(onehot * (offs[:, None] + csum - 1), axis=0)
    n_tiles = (a + N_EXPERTS * tm) // tm
    tile_start = jnp.arange(n_tiles, dtype=jnp.int32) * tm
    tile_expert = jnp.minimum(jnp.sum(tile_start[:, None] >= ends[None, :], axis=1), N_EXPERTS - 1)
    n_used = (ends[-1] // tm).reshape(1)
    last = jnp.take(tile_expert, n_used[0] - 1)
    tile_expert = jnp.where(tile_start < ends[-1], tile_expert, last)
    meta = jnp.concatenate([offs + counts, ends, n_used]).astype(jnp.int32)
    return pos.astype(jnp.int32), tile_expert.astype(jnp.int32), n_used.astype(jnp.int32), meta, n_tiles * tm


def _moe(xs_list, routes, g, wg, wu, wd):
    ids = jnp.concatenate([r[:, ROUTE_ID_LANES[0]:ROUTE_ID_LANES[1] + 1] for r in routes]).astype(jnp.int32)
    pos, tile_expert, n_used, meta, n_rows = _route_plan(ids, MOE_TM)
    bounds = np.cumsum([0] + [2 * x.shape[0] for x in xs_list])
    pos_list = [pos[lo:hi] for lo, hi in zip(bounds[:-1], bounds[1:])]
    xs = _dispatch(xs_list, pos, meta, n_rows, MOE_TM, 1024)
    ys = _grouped_ffn(xs, g, tile_expert, n_used, wg, wu, wd, MOE_TM, D_FF)
    return [_combine(x, r, ps, ys, 512) for x, r, ps in zip(xs_list, routes, pos_list)]


def _pool_kernel(u_ref, halo_ref, inv_ref, w_ref, scale_ref, o_ref, *, tiles_per_seq):
    ti = pl.program_id(0) % tiles_per_seq
    u = u_ref[...]
    halo = jnp.where(ti == 0, 0.0, halo_ref[...])
    ext = jnp.concatenate([halo, u], axis=0)
    half = POOL_WIDTH // 2
    short = lax.broadcasted_iota(jnp.int32, (ext.shape[0], half), 1) < half // 2

    def window_sums(x, n_doublings):
        sums = []
        for k in range(n_doublings):
            x = x + pltpu.roll(x, 2 ** k, 0)
            sums.append(x)
        return jnp.where(short, sums[-2], sums[-1])[POOL_HALO:]

    s = jnp.concatenate([window_sums(ext[:, :half], 2), window_sums(ext[:, half:], 4)], axis=1)
    inv_rest = inv_ref[1]
    inv_head = inv_ref[jnp.minimum(ti, 1)]
    pooled = jnp.concatenate([s[:POOL_HALO] * inv_head, s[POOL_HALO:] * inv_rest[:1]], axis=0) - u
    o_ref[...] = (_dot(pooled.astype(BF16), w_ref[...]) * scale_ref[...]).astype(o_ref.dtype)


def _pool(proj2d, col, w, scale, *, n_rows, tb, tiles_per_seq, pos0):
    per = tb // POOL_HALO
    win = np.repeat(np.asarray(POOL_WINDOWS), POOL_WIDTH // len(POOL_WINDOWS))[None, :]
    count = np.minimum(win, pos0 + np.arange(POOL_HALO)[:, None] + 1)
    inv = jnp.asarray(np.stack([1.0 / count, np.broadcast_to(1.0 / win, count.shape)]), F32)
    return pl.pallas_call(
        functools.partial(_pool_kernel, tiles_per_seq=tiles_per_seq),
        grid=(n_rows // tb,),
        in_specs=[pl.BlockSpec((tb, POOL_WIDTH), lambda i: (i, col)),
                  pl.BlockSpec((POOL_HALO, POOL_WIDTH), lambda i: (jnp.maximum(i * per - 1, 0), col)),
                  pl.BlockSpec((2, POOL_HALO, POOL_WIDTH), lambda i: (0, 0, 0)),
                  pl.BlockSpec((POOL_WIDTH, POOL_WIDTH), lambda i: (0, 0)),
                  pl.BlockSpec((1, POOL_WIDTH), lambda i: (0, 0))],
        out_specs=pl.BlockSpec((tb, POOL_WIDTH), lambda i: (i, 0)),
        out_shape=jax.ShapeDtypeStruct((n_rows, POOL_WIDTH), BF16),
        compiler_params=_cparams(("parallel",)),
        name="pool",
    )(proj2d, proj2d, inv, w, scale)


def _t5_bucket_np(rel):
    n = np.maximum(rel, 0)
    max_exact = T5_BUCKETS // 2
    nf = np.maximum(n, max_exact).astype(np.float32)
    large = max_exact + (np.log(nf / max_exact) / math.log(T5_MAX_DIST / max_exact)
                         * (T5_BUCKETS - max_exact)).astype(np.int32)
    large = np.minimum(large, T5_BUCKETS - 1)
    return np.where(n < max_exact, n, large)


def _swa_bias(rel_bias, rel):
    valid = (rel >= 0) & (rel < SWA_WINDOW)
    onehot = jnp.asarray(_t5_bucket_np(rel)[..., None] == np.arange(T5_BUCKETS), F32)
    b = jnp.einsum('qsb,bh->hqs', onehot, rel_bias.astype(F32), precision=lax.Precision.HIGHEST)
    return jnp.where(valid[None], b, NEG)


def _softmax_parts(parts, sink):
    m = sink
    for s in parts:
        m = jnp.maximum(m, jnp.max(s, axis=-1, keepdims=True))
    ps = [jnp.exp(s - m) for s in parts]
    denom = jnp.exp(sink - m)
    for p in ps:
        denom = denom + jnp.sum(p, axis=-1, keepdims=True)
    return ps, denom


def _head_mean_matrix(width):
    h = np.arange(width) // HEAD_DIM
    return jnp.asarray((h[:, None] == h[None, :]) / HEAD_DIM, BF16)


def _head_rms(x, mean_mat, g):
    ms = _dot((x * x).astype(BF16), mean_mat)
    return x * lax.rsqrt(ms + RMS_EPS) * g


SWA_TILE = 512


def _swa_prompt_kernel(sinks_ref, q_ref, kv_ref, halo_ref, qg_ref, kg_ref, mq_ref, mk_ref, bias_ref,
                       y_ref, kn_ref):
    has_prev = pl.program_id(1) > 0
    kw = SWA_KV_HEADS * HEAD_DIM
    blk = SWA_BLOCK
    kv = kv_ref[0]
    halo = halo_ref[0]
    k_ext = jnp.concatenate([halo[:, :kw], kv[:, :kw]], axis=0)
    v_ext = jnp.concatenate([halo[:, kw:], kv[:, kw:]], axis=0).astype(BF16)
    kn = _head_rms(k_ext, mk_ref[...], kg_ref[...])
    kn_ref[0] = kn[SWA_TILE:]
    knb = kn.astype(BF16)
    qn = (_head_rms(q_ref[0], mq_ref[...], qg_ref[...]) * (HEAD_DIM ** -0.5)).astype(BF16)
    row = lax.broadcasted_iota(jnp.int32, (2 * blk, 1), 0)
    col = lax.broadcasted_iota(jnp.int32, (2 * blk, 2 * blk), 1)
    for c in range(SWA_TILE // blk):
        rows = slice(c * blk, (c + 1) * blk)
        keys = slice(c * blk, (c + 2) * blk)
        for kh in range(SWA_KV_HEADS):
            ksl = slice(kh * HEAD_DIM, (kh + 1) * HEAD_DIM)
            h0 = 2 * kh
            q2 = jnp.concatenate([qn[rows, h0 * HEAD_DIM:(h0 + 1) * HEAD_DIM],
                                  qn[rows, (h0 + 1) * HEAD_DIM:(h0 + 2) * HEAD_DIM]], axis=0)
            s = _dot_nt(q2, knb[keys, ksl]) + bias_ref[kh]
            if c == 0:
                s = jnp.where(has_prev | (col >= blk), s, NEG)
            sink = jnp.where(row < blk, sinks_ref[h0], sinks_ref[h0 + 1])
            (p,), denom = _softmax_parts((s,), sink)
            o = _dot(p.astype(BF16), v_ext[keys, ksl]) / denom
            y_ref[0, rows, h0 * HEAD_DIM:(h0 + 1) * HEAD_DIM] = o[:blk].astype(y_ref.dtype)
            y_ref[0, rows, (h0 + 1) * HEAD_DIM:(h0 + 2) * HEAD_DIM] = o[blk:].astype(y_ref.dtype)


def _swa_prompt(proj3d, sinks, qg, kg, bias):
    b, l, _ = proj3d.shape
    kw = SWA_KV_HEADS * HEAD_DIM
    per = SWA_TILE // SWA_BLOCK
    tile = (1, SWA_TILE, MIX_BLOCK)
    const2 = lambda bi, i: (0, 0)
    bias2 = bias.reshape(SWA_KV_HEADS, 2 * SWA_BLOCK, 2 * SWA_BLOCK)
    return pl.pallas_call(
        _swa_prompt_kernel,
        grid=(b, l // SWA_TILE),
        in_specs=[pl.BlockSpec(memory_space=pltpu.SMEM),
                  pl.BlockSpec(tile, lambda bi, i: (bi, i, COL_SWA_Q)),
                  pl.BlockSpec(tile, lambda bi, i: (bi, i, COL_SWA_KV)),
                  pl.BlockSpec((1, SWA_BLOCK, MIX_BLOCK),
                               lambda bi, i: (bi, jnp.maximum(i * per - 1, 0), COL_SWA_KV)),
                  pl.BlockSpec((1, MIX_BLOCK), const2),
                  pl.BlockSpec((1, kw), const2),
                  pl.BlockSpec((MIX_BLOCK, MIX_BLOCK), const2),
                  pl.BlockSpec((kw, kw), const2),
                  pl.BlockSpec((SWA_KV_HEADS, 2 * SWA_BLOCK, 2 * SWA_BLOCK), lambda bi, i: (0, 0, 0))],
        out_specs=[pl.BlockSpec(tile, lambda bi, i: (bi, i, 0)),
                   pl.BlockSpec((1, SWA_BLOCK, kw), lambda bi, i: (bi, 0, 0))],
        out_shape=[jax.ShapeDtypeStruct((b, l, MIX_BLOCK), BF16),
                   jax.ShapeDtypeStruct((b, SWA_BLOCK, kw), F32)],
        compiler_params=_cparams(("parallel", "arbitrary")),
        name="swa_prompt",
    )(sinks, proj3d, proj3d, proj3d, jnp.tile(qg, (1, SWA_HEADS)), jnp.tile(kg, (1, SWA_KV_HEADS)),
      _head_mean_matrix(MIX_BLOCK), _head_mean_matrix(kw), bias2)


def _layer_state_out(layer, depth, blk, shape, dtype, prev):
    zeros = (0,) * (len(blk) - 1)
    shape = jax.ShapeDtypeStruct((depth,) + shape, dtype)
    if prev is None:
        assert layer == 0
        return pl.BlockSpec((depth,) + blk, lambda i: (0, i) + zeros), shape, [], []
    spec = pl.BlockSpec((1,) + blk, lambda i: (layer, i) + zeros)
    return spec, shape, [prev], [pl.BlockSpec(memory_space=pl.ANY)]


def _zero_other_layers(ref):
    if ref.shape[0] > 1:
        ref[1:] = jnp.zeros((ref.shape[0] - 1,) + ref.shape[1:], ref.dtype)


def _swa_sample_kernel(sinks_ref, q_ref, kv_ref, ckt_ref, cvt_ref, qg_ref, kg_ref, mq_ref, mk_ref, bias_ref,
                       *rest, n_new):
    y_ref, nkt_ref, nvt_ref = rest[-3:]
    _, s_blk, kw, wb = ckt_ref.shape
    keep = wb - n_new
    q2 = _head_rms(q_ref[...], mq_ref[...], qg_ref[...]) * (HEAD_DIM ** -0.5)
    kv = kv_ref[...]
    kn2 = _head_rms(kv[:, :kw], mk_ref[...], kg_ref[...])
    v2 = kv[:, kw:]
    q3 = q2.reshape(s_blk, n_new, MIX_BLOCK)
    kn3 = kn2.reshape(s_blk, n_new, kw)
    v3 = v2.reshape(s_blk, n_new, kw)
    ckt = ckt_ref[0]
    cvt = cvt_ref[0]
    bdot = functools.partial(jnp.einsum, preferred_element_type=F32)
    for h in range(SWA_HEADS):
        kh = h // (SWA_HEADS // SWA_KV_HEADS)
        ksl = slice(kh * HEAD_DIM, (kh + 1) * HEAD_DIM)
        hsl = slice(h * HEAD_DIM, (h + 1) * HEAD_DIM)
        qh = q3[:, :, hsl]
        s_c = bdot('sqd,sdw->sqw', qh, ckt[:, ksl, :]) + bias_ref[h, :, :wb]
        s_n = bdot('sqd,sjd->sqj', qh, kn3[:, :, ksl]) + bias_ref[h, :, wb:]
        (p_c, p_n), denom = _softmax_parts((s_c, s_n), sinks_ref[h])
        o = bdot('sqw,sdw->sqd', p_c, cvt[:, ksl, :]) + bdot('sqj,sjd->sqd', p_n, v3[:, :, ksl])
        y_ref[:, hsl] = (o / denom).reshape(s_blk * n_new, HEAD_DIM).astype(y_ref.dtype)
    lane = lax.broadcasted_iota(jnp.int32, (kw, wb), 1)
    for new2, old, out_ref in ((kn2, ckt, nkt_ref), (v2, cvt, nvt_ref)):
        new_t = new2.T
        shifted = pltpu.roll(old.reshape(s_blk * kw, wb), keep, 1).reshape(s_blk, kw, wb)
        _zero_other_layers(out_ref)
        for s in range(s_blk):
            cols = pltpu.roll(new_t, (keep - s * n_new) % wb, 1)
            out_ref[0, s] = jnp.where(lane >= keep, cols, shifted[s])


def _swa_sample(proj2d, cache_kt, cache_vt, layer, prev_new, sinks, qg, kg, bias, n_new, s_blk):
    depth, nseq, kw, wb = cache_kt.shape
    assert s_blk * n_new == wb
    rows = s_blk * n_new
    blk = (rows, MIX_BLOCK)
    cblk = (s_blk, kw, wb)
    cin = pl.BlockSpec((1,) + cblk, lambda i: (layer, i, 0, 0))
    const2 = lambda i: (0, 0)
    prev_k, prev_v = prev_new if prev_new is not None else (None, None)
    kspec, kshape, kin, kin_specs = _layer_state_out(layer, depth, cblk, (nseq, kw, wb), F32, prev_k)
    vspec, vshape, vin, vin_specs = _layer_state_out(layer, depth, cblk, (nseq, kw, wb), F32, prev_v)
    n_main = 10
    aliases = {n_main + j: 1 + j for j in range(len(kin + vin))}
    return pl.pallas_call(
        functools.partial(_swa_sample_kernel, n_new=n_new),
        grid=(nseq // s_blk,),
        in_specs=[pl.BlockSpec(memory_space=pltpu.SMEM),
                  pl.BlockSpec(blk, lambda i: (i, COL_SWA_Q)),
                  pl.BlockSpec(blk, lambda i: (i, COL_SWA_KV)),
                  cin, cin,
                  pl.BlockSpec((1, MIX_BLOCK), const2),
                  pl.BlockSpec((1, kw), const2),
                  pl.BlockSpec((MIX_BLOCK, MIX_BLOCK), const2),
                  pl.BlockSpec((kw, kw), const2),
                  pl.BlockSpec((SWA_HEADS, n_new, wb + n_new), lambda i: (0, 0, 0))] + kin_specs + vin_specs,
        out_specs=[pl.BlockSpec(blk, lambda i: (i, 0)), kspec, vspec],
        out_shape=[jax.ShapeDtypeStruct((nseq * n_new, MIX_BLOCK), BF16), kshape, vshape],
        input_output_aliases=aliases,
        compiler_params=_cparams(("parallel",)),
        name="swa_sample",
    )(sinks, proj2d, proj2d, cache_kt, cache_vt, jnp.tile(qg, (1, SWA_HEADS)), jnp.tile(kg, (1, SWA_KV_HEADS)),
      _head_mean_matrix(MIX_BLOCK), _head_mean_matrix(kw), bias, *kin, *vin)


def _ssm_kernel(u_ref, h0_ref, wb_ref, tab_ref, wc_ref, d_ref, wglu_ref, y_ref, hn_ref,
                bu_scr, carry_scr, *, chained, tiles_per_seq, tb):
    n = SSM_N

    def project_in(rows):
        bu_scr[rows, :] = _dot(u_ref[rows, :].astype(BF16), wb_ref[...])

    def project_out(rows):
        y = _dot(bu_scr[rows, :].astype(BF16), wc_ref[...]) + d_ref[...] * u_ref[rows, :]
        y = 0.5 * y * (1.0 + jnp.tanh(math.sqrt(2.0 / math.pi) * (y + 0.044715 * (y * y * y))))
        return (y * _sigmoid(_dot(y.astype(BF16), wglu_ref[...]))).astype(y_ref.dtype)

    def tile_scan(r0, cr, ci):
        hr = bu_scr[pl.ds(r0, SCAN_ROWS), :n]
        hi = bu_scr[pl.ds(r0, SCAN_ROWS), n:]
        for k, shift in enumerate((1, 2, 4)):
            ar, ai = tab_ref[2 * k], tab_ref[2 * k + 1]
            sr, si = pltpu.roll(hr, shift, 0), pltpu.roll(hi, shift, 0)
            hr, hi = hr + ar * sr - ai * si, hi + ar * si + ai * sr
        pr, pi = tab_ref[6], tab_ref[7]
        hr, hi = hr + pr * cr - pi * ci, hi + pr * ci + pi * cr
        bu_scr[pl.ds(r0, SCAN_ROWS), :n] = hr
        bu_scr[pl.ds(r0, SCAN_ROWS), n:] = hi
        return hr[SCAN_ROWS - 1:], hi[SCAN_ROWS - 1:]

    if chained:
        @pl.when(pl.program_id(1) % tiles_per_seq == 0)
        def _():
            carry_scr[...] = jnp.zeros_like(carry_scr)

        project_in(slice(None))
        cr, ci = carry_scr[:, :n], carry_scr[:, n:]
        for r0 in range(0, tb, SCAN_ROWS):
            lr, li = tile_scan(r0, cr, ci)
            cr, ci = jnp.broadcast_to(lr, (SCAN_ROWS, n)), jnp.broadcast_to(li, (SCAN_ROWS, n))
        y_ref[0] = project_out(slice(None))
        carry_scr[:, :n] = cr
        carry_scr[:, n:] = ci
        hn_ref[0, :, :n] = cr
        hn_ref[0, :, n:] = ci
    else:
        project_in(slice(None))
        for t in range(tb // SCAN_ROWS):
            h0 = h0_ref[t:t + 1, :]
            cr = jnp.broadcast_to(h0[:, :n], (SCAN_ROWS, n))
            ci = jnp.broadcast_to(h0[:, n:], (SCAN_ROWS, n))
            lr, li = tile_scan(t * SCAN_ROWS, cr, ci)
            hn_ref[t:t + 1, :n] = lr
            hn_ref[t:t + 1, n:] = li
        y_ref[...] = project_out(slice(None))


def _ssm_common_specs(zero_map2, zero_map3):
    return [pl.BlockSpec((SSM_WIDTH, 2 * SSM_N), zero_map2),
            pl.BlockSpec((8, SCAN_ROWS, SSM_N), zero_map3),
            pl.BlockSpec((2 * SSM_N, SSM_WIDTH), zero_map2),
            pl.BlockSpec((1, SSM_WIDTH), zero_map2),
            pl.BlockSpec((SSM_WIDTH, SSM_WIDTH), zero_map2)]


def _ssm_prompt(proj3d, sp, tb):
    b, l, _ = proj3d.shape
    nt = l // tb
    dummy_h0 = jnp.zeros((SCAN_ROWS, 2 * SSM_N), F32)
    kern = functools.partial(_ssm_kernel, chained=True, tiles_per_seq=nt, tb=tb)

    def kernel(u_ref, h0_ref, wb, tab, wc, d, wglu, y_ref, hn_ref, bu_scr, carry_scr):
        kern(u_ref.at[0], h0_ref, wb, tab, wc, d, wglu, y_ref, hn_ref, bu_scr, carry_scr)

    return pl.pallas_call(
        kernel,
        grid=(b, nt),
        in_specs=[pl.BlockSpec((1, tb, SSM_WIDTH), lambda bi, i: (bi, i, COL_SSM)),
                  pl.BlockSpec((SCAN_ROWS, 2 * SSM_N), lambda bi, i: (0, 0))]
                 + _ssm_common_specs(lambda bi, i: (0, 0), lambda bi, i: (0, 0, 0)),
        out_specs=[pl.BlockSpec((1, tb, SSM_WIDTH), lambda bi, i: (bi, i, 0)),
                   pl.BlockSpec((1, SCAN_ROWS, 2 * SSM_N), lambda bi, i: (bi, 0, 0))],
        out_shape=[jax.ShapeDtypeStruct((b, l, SSM_WIDTH), BF16),
                   jax.ShapeDtypeStruct((b, SCAN_ROWS, 2 * SSM_N), F32)],
        scratch_shapes=[pltpu.VMEM((tb, 2 * SSM_N), F32), pltpu.VMEM((SCAN_ROWS, 2 * SSM_N), F32)],
        compiler_params=_cparams(("parallel", "arbitrary")),
        name="ssm_prompt",
    )(proj3d, dummy_h0, sp["wb"], sp["tab"], sp["wc"], sp["d"], sp["wglu"])


def _ssm_sample(proj2d, h0, sp):
    rows = proj2d.shape[0]
    nseq = h0.shape[0]
    kern = functools.partial(_ssm_kernel, chained=False, tiles_per_seq=1, tb=rows)
    return pl.pallas_call(
        kern,
        grid=(1,),
        in_specs=[pl.BlockSpec((rows, SSM_WIDTH), lambda i: (0, COL_SSM)),
                  pl.BlockSpec((nseq, 2 * SSM_N), lambda i: (0, 0))]
                 + _ssm_common_specs(lambda i: (0, 0), lambda i: (0, 0, 0)),
        out_specs=[pl.BlockSpec((rows, SSM_WIDTH), lambda i: (0, 0)),
                   pl.BlockSpec((nseq, 2 * SSM_N), lambda i: (0, 0))],
        out_shape=[jax.ShapeDtypeStruct((rows, SSM_WIDTH), BF16),
                   jax.ShapeDtypeStruct((nseq, 2 * SSM_N), F32)],
        scratch_shapes=[pltpu.VMEM((rows, 2 * SSM_N), F32), pltpu.VMEM((SCAN_ROWS, 2 * SSM_N), F32)],
        compiler_params=_cparams(("arbitrary",)),
        name="ssm_sample",
    )(proj2d, h0, sp["wb"], sp["tab"], sp["wc"], sp["d"], sp["wglu"])


def _ssm_params(lam_re, lam_im, log_dt, b_re, b_im, c_re, c_im, d_skip, w_glu):
    lr, li = lam_re.astype(F32), lam_im.astype(F32)
    dt = jnp.exp(log_dt.astype(F32))[:, None]
    mag = jnp.exp(lr * dt)
    ab_re, ab_im = mag * jnp.cos(li * dt), mag * jnp.sin(li * dt)
    den = lr * lr + li * li
    nr = ab_re - 1.0
    f_re = (nr * lr + ab_im * li) / den
    f_im = (ab_im * lr - nr * li) / den
    br, bi = b_re.astype(F32), b_im.astype(F32)
    bb_re = f_re[..., None] * br - f_im[..., None] * bi
    bb_im = f_re[..., None] * bi + f_im[..., None] * br
    eye = jnp.eye(SSM_GROUPS, dtype=F32)

    def in_mat(bb):
        return jnp.einsum('gpc,gh->gchp', bb, eye).reshape(SSM_WIDTH, SSM_N)

    def out_mat(c):
        return jnp.einsum('gcp,gh->gphc', c.astype(F32), eye).reshape(SSM_N, SSM_WIDTH)

    wb = jnp.concatenate([in_mat(bb_re), in_mat(bb_im)], axis=1).astype(BF16)
    wc = jnp.concatenate([out_mat(c_re), -out_mat(c_im)], axis=0).astype(BF16)

    ar, ai = ab_re.reshape(1, SSM_N), ab_im.reshape(1, SSM_N)

    def cmul(x, y):
        return (x[0] * y[0] - x[1] * y[1], x[0] * y[1] + x[1] * y[0])

    pw = [(ar, ai)]
    for _ in range(SCAN_ROWS - 1):
        pw.append(cmul(pw[-1], (ar, ai)))
    row = jnp.arange(SCAN_ROWS)[:, None]
    tabs = []
    for shift in (1, 2, 4):
        for part in pw[shift - 1]:
            tabs.append(jnp.where(row >= shift, part, 0.0))
    tabs.append(jnp.concatenate([p[0] for p in pw], axis=0))
    tabs.append(jnp.concatenate([p[1] for p in pw], axis=0))
    tab = jnp.stack([jnp.broadcast_to(t, (SCAN_ROWS, SSM_N)) for t in tabs])
    return dict(wb=wb, tab=tab, wc=wc, d=d_skip.astype(F32).reshape(1, SSM_WIDTH), wglu=w_glu.astype(BF16))


_RET_G = 1.0 - np.exp2(-5.0 - np.arange(RET_HEADS, dtype=np.float64))


def _ret_consts(chunk, n_rows):
    idx = np.arange(n_rows)
    loc = idx % chunk
    same = (idx[:, None] // chunk) == (idx[None, :] // chunk)
    diff = loc[:, None] - loc[None, :]
    dec = np.where(same & (diff >= 0), _RET_G[:, None, None] ** np.maximum(diff, 0)[None], 0.0)
    qdec = np.repeat((_RET_G[None, :] ** (loc[:, None] + 1.0)), HEAD_DIM, axis=1)
    kdec = np.repeat((_RET_G[None, :] ** (chunk - 1.0 - loc[:, None])), HEAD_DIM, axis=1)
    return (jnp.asarray(dec, F32), jnp.asarray(qdec, F32), jnp.asarray(kdec, F32),
            jnp.asarray(_RET_G ** chunk, F32))


def _rope_tables(pos):
    half = HEAD_DIM // 2
    theta = 1.0 / (ROPE_BASE ** np.linspace(0.0, 1.0, half))
    ang = np.asarray(pos, np.float64)[:, None] * theta[None, :]
    cos = np.repeat(np.cos(ang), 2, axis=1)
    sin = np.repeat(np.sin(ang), 2, axis=1) * np.tile([-1.0, 1.0], half)[None]
    return (jnp.asarray(np.tile(cos, (1, RET_HEADS)), F32), jnp.asarray(np.tile(sin, (1, RET_HEADS)), F32))


def _rotate_pairs(x, cos, sin_signed):
    lane = lax.broadcasted_iota(jnp.int32, x.shape, 1)
    nxt = pltpu.roll(x, x.shape[1] - 1, 1)
    prv = pltpu.roll(x, 1, 1)
    return x * cos + jnp.where(lane % 2 == 0, nxt, prv) * sin_signed


def _ret_head_out(o, gate, norm):
    ms = jnp.mean(o * o, axis=-1, keepdims=True)
    return o * lax.rsqrt(ms + RMS_EPS) * norm * (gate * _sigmoid(gate))


RET_TILE = 1024


def _ret_prompt_kernel(gc_ref, q_ref, k_ref, v_ref, g_ref, cos_ref, sin_ref, dec_ref, qdec_ref,
                       kdec_ref, norm_ref, mh_ref, y_ref, r_ref, o_scr):
    @pl.when(pl.program_id(1) == 0)
    def _():
        r_ref[...] = jnp.zeros_like(r_ref)

    cos, sin = cos_ref[...], sin_ref[...]
    q = _rotate_pairs(q_ref[0], cos, sin)
    k = _rotate_pairs(k_ref[0], cos, sin) * (HEAD_DIM ** -0.5)
    qb, kb, vb = q.astype(BF16), k.astype(BF16), v_ref[0].astype(BF16)
    kdb = (k * kdec_ref[...]).astype(BF16)
    for c in range(RET_TILE // RET_CHUNK):
        rows = slice(c * RET_CHUNK, (c + 1) * RET_CHUNK)
        for h in range(RET_HEADS):
            sl = slice(h * HEAD_DIM, (h + 1) * HEAD_DIM)
            qh, vh = qb[rows, sl], vb[rows, sl]
            s = _dot_nt(qh, kb[rows, sl]) * dec_ref[h]
            r = r_ref[0, h]
            o_scr[rows, sl] = _dot(s.astype(BF16), vh) + _dot(qh, r.astype(BF16)) * qdec_ref[rows, sl]
            r_ref[0, h] = gc_ref[h] * r + _dot_tn(kdb[rows, sl], vh)
    g = g_ref[0]
    y_ref[0] = (_head_rms(o_scr[...], mh_ref[...], norm_ref[...]) * (g * _sigmoid(g))).astype(y_ref.dtype)


def _ret_prompt(proj3d, cos, sin, norm):
    b, l, _ = proj3d.shape
    c = RET_CHUNK
    dec, _, _, gc = _ret_consts(c, c)
    _, qdec, kdec, _ = _ret_consts(c, RET_TILE)
    blk = (1, RET_TILE, MIX_BLOCK)
    tspec = pl.BlockSpec((RET_TILE, MIX_BLOCK), lambda bi, i: (i, 0))
    cspec = pl.BlockSpec((RET_TILE, MIX_BLOCK), lambda bi, i: (0, 0))
    return pl.pallas_call(
        _ret_prompt_kernel,
        grid=(b, l // RET_TILE),
        in_specs=[pl.BlockSpec(memory_space=pltpu.SMEM),
                  pl.BlockSpec(blk, lambda bi, i: (bi, i, COL_RET_Q)),
                  pl.BlockSpec(blk, lambda bi, i: (bi, i, COL_RET_K)),
                  pl.BlockSpec(blk, lambda bi, i: (bi, i, COL_RET_V)),
                  pl.BlockSpec(blk, lambda bi, i: (bi, i, COL_RET_G)),
                  tspec, tspec,
                  pl.BlockSpec((RET_HEADS, c, c), lambda bi, i: (0, 0, 0)),
                  cspec, cspec,
                  pl.BlockSpec((1, MIX_BLOCK), lambda bi, i: (0, 0)),
                  pl.BlockSpec((MIX_BLOCK, MIX_BLOCK), lambda bi, i: (0, 0))],
        out_specs=[pl.BlockSpec(blk, lambda bi, i: (bi, i, 0)),
                   pl.BlockSpec((1, RET_HEADS, HEAD_DIM, HEAD_DIM), lambda bi, i: (bi, 0, 0, 0))],
        out_shape=[jax.ShapeDtypeStruct((b, l, MIX_BLOCK), BF16),
                   jax.ShapeDtypeStruct((b, RET_HEADS, HEAD_DIM, HEAD_DIM), F32)],
        scratch_shapes=[pltpu.VMEM((RET_TILE, MIX_BLOCK), F32)],
        compiler_params=_cparams(("parallel", "arbitrary")),
        name="ret_prompt",
    )(gc, proj3d, proj3d, proj3d, proj3d, cos, sin, dec, qdec, kdec, norm, _head_mean_matrix(MIX_BLOCK))


def _ret_sample_kernel(gc_ref, q_ref, k_ref, v_ref, g_ref, cos_ref, sin_ref, dec_ref, qdec_ref,
                       kdec_ref, norm_ref, r0_ref, *rest, n_new, s_blk):
    y_ref, rn_ref = rest[-2:]
    _zero_other_layers(rn_ref)
    cos, sin = cos_ref[...], sin_ref[...]
    q = _rotate_pairs(q_ref[...], cos, sin)
    k = _rotate_pairs(k_ref[...], cos, sin) * (HEAD_DIM ** -0.5)
    v = v_ref[...]
    g = g_ref[...]
    kd = k * kdec_ref[...]
    qdec = qdec_ref[...]
    norm = norm_ref[...]
    rows = s_blk * n_new
    seq = lax.broadcasted_iota(jnp.int32, (rows, HEAD_DIM), 0) // n_new
    seq_t = lax.broadcasted_iota(jnp.int32, (HEAD_DIM, rows), 1) // n_new
    kd_t = kd.T
    for h in range(RET_HEADS):
        sl = slice(h * HEAD_DIM, (h + 1) * HEAD_DIM)
        qf, kdt = q[:, sl], kd_t[sl, :]
        qh, kh, vh = qf.astype(BF16), k[:, sl].astype(BF16), v[:, sl].astype(BF16)
        s = _dot_nt(qh, kh) * dec_ref[h]
        cross = jnp.zeros((rows, HEAD_DIM), F32)
        for si in range(s_blk):
            mine = seq == si
            r = r0_ref[si, h]
            cross = cross + _dot(jnp.where(mine, qf, 0.0).astype(BF16), r.astype(BF16))
            rn_ref[0, si, h] = gc_ref[h] * r + _dot(jnp.where(seq_t == si, kdt, 0.0).astype(BF16), vh)
        o = _dot(s.astype(BF16), vh) + cross * qdec[:, sl]
        y_ref[:, sl] = _ret_head_out(o, g[:, sl], norm[:, sl]).astype(y_ref.dtype)


def _ret_sample(proj2d, r0, layer, depth, prev_new, cos, sin, norm, n_new, s_blk):
    rows = s_blk * n_new
    nseq = r0.shape[0]
    dec, qdec, kdec, gc = _ret_consts(n_new, rows)
    blk = (rows, MIX_BLOCK)
    cspec = pl.BlockSpec(blk, lambda i: (0, 0))
    rblk = (s_blk, RET_HEADS, HEAD_DIM, HEAD_DIM)
    rspec, rshape, rin, rin_specs = _layer_state_out(layer, depth, rblk, (nseq,) + rblk[1:], F32, prev_new)
    return pl.pallas_call(
        functools.partial(_ret_sample_kernel, n_new=n_new, s_blk=s_blk),
        grid=(nseq // s_blk,),
        in_specs=[pl.BlockSpec(memory_space=pltpu.SMEM),
                  pl.BlockSpec(blk, lambda i: (i, COL_RET_Q)),
                  pl.BlockSpec(blk, lambda i: (i, COL_RET_K)),
                  pl.BlockSpec(blk, lambda i: (i, COL_RET_V)),
                  pl.BlockSpec(blk, lambda i: (i, COL_RET_G)),
                  cspec, cspec,
                  pl.BlockSpec((RET_HEADS, rows, rows), lambda i: (0, 0, 0)),
                  cspec, cspec,
                  pl.BlockSpec((1, MIX_BLOCK), lambda i: (0, 0)),
                  pl.BlockSpec(rblk, lambda i: (i, 0, 0, 0))] + rin_specs,
        out_specs=[pl.BlockSpec(blk, lambda i: (i, 0)), rspec],
        out_shape=[jax.ShapeDtypeStruct((nseq * n_new, MIX_BLOCK), BF16), rshape],
        input_output_aliases={12 + j: 1 for j in range(len(rin))},
        compiler_params=_cparams(("parallel",)),
        name="ret_sample",
    )(gc, proj2d, proj2d, proj2d, proj2d, cos, sin, dec, qdec, kdec, norm, r0, *rin)


def _block_diag(w):
    g, n, _ = w.shape
    return jnp.einsum('gcd,gh->gchd', w, jnp.eye(g, dtype=w.dtype)).reshape(g * n, g * n)


def _layer_params(l, p):
    return dict(
        norm_mix=p['norm_mix'][l].reshape(1, D_MODEL),
        norm_ffn=p['norm_ffn'][l].reshape(1, D_MODEL),
        w_in=p['w_in'][l].astype(BF16),
        w_out=p['w_out'][l].astype(BF16),
        pool_w=_block_diag(p['pool_w'][l].astype(F32)).astype(BF16),
        pool_scale=p['pool_scale'][l].astype(F32).reshape(1, POOL_WIDTH),
        qg=p['swa_q_norm'][l].astype(F32).reshape(1, HEAD_DIM),
        kg=p['swa_k_norm'][l].astype(F32).reshape(1, HEAD_DIM),
        sinks=p['swa_sinks'][l].astype(F32),
        ssm=_ssm_params(p['ssm_lambda_re'][l], p['ssm_lambda_im'][l], p['ssm_log_dt'][l],
                        p['ssm_b_re'][l], p['ssm_b_im'][l], p['ssm_c_re'][l], p['ssm_c_im'][l],
                        p['ssm_d'][l], p['ssm_w_glu'][l]),
        ret_norm=p['ret_norm'][l].astype(F32).reshape(1, MIX_BLOCK),
    )


def _channel_mix(streams, l, lp, p):
    i = l // 2
    g, w_out = lp['norm_ffn'], lp['w_out']
    if l % 2 == 0:
        wg, wu, wd = (p[k][i].astype(BF16) for k in ('ffn_w_gate', 'ffn_w_up', 'ffn_w_down'))
        return [_out_proj_ffn(x, ys, w_out, g, wg, wu, wd, 512, D_FF) for x, ys in streams]
    wr3 = _router_weights(p['moe_router'][i])
    wg, wu, wd = (p[k][i].astype(BF16) for k in ('moe_w_gate', 'moe_w_up', 'moe_w_down'))
    x1s, routes = zip(*[_out_proj_router(x, ys, w_out, g, wr3, 1024) for x, ys in streams])
    return _moe(x1s, routes, g, wg, wu, wd)


def _mix_prompt(x2, b, l, lp, bias, cos, sin):
    proj2 = _norm_matmul(x2, lp['norm_mix'], lp['w_in'], 1024)
    proj3 = proj2.reshape(b, l, IN_WIDTH)
    tb = min(4096, l)
    y_pool = _pool(proj2, COL_POOL, lp['pool_w'], lp['pool_scale'], n_rows=b * l, tb=tb,
                   tiles_per_seq=l // tb, pos0=0)
    y_swa, kn = _swa_prompt(proj3, lp['sinks'], lp['qg'], lp['kg'], bias)
    y_ssm, hn = _ssm_prompt(proj3, lp['ssm'], min(1024, l))
    y_ret, rn = _ret_prompt(proj3, cos, sin, lp['ret_norm'])
    ys = (y_pool, y_swa.reshape(b * l, MIX_BLOCK), y_ssm.reshape(b * l, MIX_BLOCK),
          y_ret.reshape(b * l, MIX_BLOCK))
    kw = SWA_KV_HEADS * HEAD_DIM
    hn = hn[:, 0]
    states = (proj3[:, l - POOL_BUF:, :POOL_WIDTH],
              kn.reshape(b, SWA_WINDOW, SWA_KV_HEADS, HEAD_DIM),
              proj3[:, l - SWA_WINDOW:, COL_SWA_KV * MIX_BLOCK + kw:(COL_SWA_KV + 1) * MIX_BLOCK]
              .reshape(b, SWA_WINDOW, SWA_KV_HEADS, HEAD_DIM),
              jnp.stack([hn[:, :SSM_N], hn[:, SSM_N:]], axis=-1).reshape(b, SSM_GROUPS, SSM_STATE, 2),
              rn)
    return ys, states


SAMPLE_SEQ_BLOCK = 16


def _cache_transposed(cache):
    depth, nseq, wb = cache.shape[:3]
    return jnp.swapaxes(cache.astype(F32).reshape(depth, nseq, wb, SWA_KV_HEADS * HEAD_DIM), 2, 3)


def _mix_sample(x2, nseq, n_new, start_pos, lp, layer, st, prev_new, bias, cos, sin):
    state_pool, cache_kt, cache_vt, state_ssm, state_ret = st
    prev_kv, prev_ret = prev_new if prev_new is not None else (None, None)
    rows = nseq * n_new
    kw = SWA_KV_HEADS * HEAD_DIM
    ext_rows = POOL_HALO + n_new
    proj2 = _norm_matmul(x2, lp['norm_mix'], lp['w_in'], 512)
    proj3 = proj2.reshape(nseq, n_new, IN_WIDTH)
    u_pool = proj3[:, :, :POOL_WIDTH]
    buf = state_pool.astype(F32)
    ext = jnp.concatenate([jnp.zeros((nseq, POOL_HALO - POOL_BUF, POOL_WIDTH), F32), buf, u_pool], axis=1)
    y_pool = _pool(ext.reshape(nseq * ext_rows, POOL_WIDTH), 0, lp['pool_w'], lp['pool_scale'],
                   n_rows=nseq * ext_rows, tb=nseq * ext_rows, tiles_per_seq=1, pos0=start_pos)
    y_pool = y_pool.reshape(nseq, ext_rows, POOL_WIDTH)[:, POOL_HALO:].reshape(rows, POOL_WIDTH)
    y_swa, nkt, nvt = _swa_sample(proj2, cache_kt, cache_vt, layer, prev_kv, lp['sinks'], lp['qg'], lp['kg'],
                                  bias, n_new, SAMPLE_SEQ_BLOCK)
    h0 = state_ssm.astype(F32).reshape(nseq, SSM_N, 2)
    h0 = jnp.concatenate([h0[..., 0], h0[..., 1]], axis=1)
    y_ssm, hn = _ssm_sample(proj2, h0, lp['ssm'])
    y_ret, rn = _ret_sample(proj2, state_ret[layer], layer, state_ret.shape[0], prev_ret, cos, sin,
                            lp['ret_norm'], n_new, SAMPLE_SEQ_BLOCK)
    ys = (y_pool, y_swa, y_ssm, y_ret)
    states = (jnp.concatenate([buf, u_pool], axis=1)[:, -POOL_BUF:],
              jnp.stack([hn[:, :SSM_N], hn[:, SSM_N:]], axis=-1).reshape(nseq, SSM_GROUPS, SSM_STATE, 2))
    return ys, states, ((nkt, nvt), rn)


def _forward(x_prompt, x_sample, past_len, sample_state, p, rel_bias):
    b, l, d = x_prompt.shape
    nseq, n_new, _ = x_sample.shape
    wb = sample_state[1].shape[2]
    depth = p['norm_mix'].shape[0]
    bias_p = _swa_bias(rel_bias, np.arange(SWA_BLOCK)[:, None] - np.arange(2 * SWA_BLOCK)[None, :] + SWA_BLOCK)
    bias_s = _swa_bias(rel_bias, np.arange(n_new)[:, None] - np.arange(wb + n_new)[None, :] + wb)
    rope_p = _rope_tables(np.arange(l))
    rope_s = _rope_tables(past_len + (np.arange(SAMPLE_SEQ_BLOCK * n_new) % n_new))
    xp = x_prompt.reshape(b * l, d)
    xs = x_sample.reshape(nseq * n_new, d)
    cache_kt, cache_vt = _cache_transposed(sample_state[1]), _cache_transposed(sample_state[2])
    st_p, st_s, stacked = [], [], None
    for li in range(depth):
        lp = _layer_params(li, p)
        yp, sp = _mix_prompt(xp, b, l, lp, bias_p, *rope_p)
        st = (sample_state[0][li], cache_kt, cache_vt, sample_state[3][li], sample_state[4])
        ys, ss, stacked = _mix_sample(xs, nseq, n_new, past_len, lp, li, st, stacked, bias_s, *rope_s)
        xp, xs = _channel_mix([(xp, yp), (xs, ys)], li, lp, p)
        st_p.append(sp)
        st_s.append(ss)
    (new_kt, new_vt), new_ret = stacked
    kv_shape = (depth, nseq, wb, SWA_KV_HEADS, HEAD_DIM)
    sample_out = (jnp.stack([s[0] for s in st_s]),
                  jnp.swapaxes(new_kt, 2, 3).reshape(kv_shape),
                  jnp.swapaxes(new_vt, 2, 3).reshape(kv_shape),
                  jnp.stack([s[1] for s in st_s]),
                  new_ret)
    outs = [xp.reshape(b, l, d), xs.reshape(nseq, n_new, d)]
    for k in range(5):
        outs.append(jnp.stack([s[k] for s in st_p]))
        outs.append(sample_out[k])
    return tuple(outs)


PAST_LEN = 16384


def kernel(x_prompt, x_sample, state_pool, cache_swa_k, cache_swa_v, state_ssm, state_ret,
           norm_mix, norm_ffn, w_in, w_out, pool_w, pool_scale, swa_q_norm, swa_k_norm, swa_sinks,
           rel_bias, ssm_lambda_re, ssm_lambda_im, ssm_log_dt, ssm_b_re, ssm_b_im, ssm_c_re, ssm_c_im,
           ssm_d, ssm_w_glu, ret_norm, ffn_w_gate, ffn_w_up, ffn_w_down, moe_router, moe_w_gate,
           moe_w_up, moe_w_down):
    p = dict(norm_mix=norm_mix, norm_ffn=norm_ffn, w_in=w_in, w_out=w_out, pool_w=pool_w,
             pool_scale=pool_scale, swa_q_norm=swa_q_norm, swa_k_norm=swa_k_norm, swa_sinks=swa_sinks,
             ssm_lambda_re=ssm_lambda_re, ssm_lambda_im=ssm_lambda_im, ssm_log_dt=ssm_log_dt,
             ssm_b_re=ssm_b_re, ssm_b_im=ssm_b_im, ssm_c_re=ssm_c_re, ssm_c_im=ssm_c_im,
             ssm_d=ssm_d, ssm_w_glu=ssm_w_glu, ret_norm=ret_norm,
             ffn_w_gate=ffn_w_gate, ffn_w_up=ffn_w_up, ffn_w_down=ffn_w_down, moe_router=moe_router,
             moe_w_gate=moe_w_gate, moe_w_up=moe_w_up, moe_w_down=moe_w_down)
    return _forward(x_prompt, x_sample, PAST_LEN,
                    (state_pool, cache_swa_k, cache_swa_v, state_ssm, state_ret), p, rel_bias)
```

```python
import functools
import math

import numpy as np
import jax
import jax.numpy as jnp
from jax import lax
from jax.experimental import pallas as pl
from jax.experimental.pallas import tpu as pltpu

F32 = jnp.float32
BF16 = jnp.bfloat16

D_MODEL = 1024
HEAD_DIM = 64
POOL_WIDTH = 256
POOL_WINDOWS = (2, 4, 8, 16)
POOL_BUF = 15
POOL_HALO = 16
SWA_HEADS = 4
SWA_KV_HEADS = 2
SWA_WINDOW = 128
SWA_BLOCK = 128
SSM_WIDTH = 256
SSM_CH = 16
SSM_GROUPS = 16
SSM_STATE = 64
SSM_N = SSM_GROUPS * SSM_STATE
RET_HEADS = 4
RET_CHUNK = 128
ROPE_BASE = 10000.0
IN_WIDTH = 2048
MIX_BLOCK = 256
D_FF = 2816
N_EXPERTS = 8
T5_BUCKETS = 32
T5_MAX_DIST = 128
RMS_EPS = 1e-6
NEG = -1e30
SUBLANES = 8
SCAN_ROWS = SUBLANES

COL_POOL, COL_SWA_Q, COL_SWA_KV, COL_SSM, COL_RET_Q, COL_RET_K, COL_RET_V, COL_RET_G = range(8)

VMEM_LIMIT = 48 * 1024 * 1024
FFN_VMEM_LIMIT = 58 * 1024 * 1024


def _cparams(sem, vmem=VMEM_LIMIT):
    return pltpu.CompilerParams(dimension_semantics=sem, vmem_limit_bytes=vmem)


def _rms(x, g):
    ms = jnp.mean(x * x, axis=-1, keepdims=True)
    return x * lax.rsqrt(ms + RMS_EPS) * g


def _dot(a, b):
    return jnp.dot(a, b, preferred_element_type=F32)


def _dot_nt(a, b):
    return lax.dot_general(a, b, (((1,), (1,)), ((), ())), preferred_element_type=F32)


def _dot_tn(a, b):
    return lax.dot_general(a, b, (((0,), (0,)), ((), ())), preferred_element_type=F32)


def _sigmoid(x):
    return 1.0 / (1.0 + jnp.exp(-x))


def _norm_matmul_kernel(x_ref, g_ref, w_ref, o_ref):
    h = _rms(x_ref[...], g_ref[...]).astype(BF16)
    o_ref[...] = _dot(h, w_ref[...])


def _norm_matmul(x, g, w, tm):
    t, d = x.shape
    tm = min(tm, t)
    n = w.shape[1]
    return pl.pallas_call(
        _norm_matmul_kernel,
        grid=(t // tm,),
        in_specs=[pl.BlockSpec((tm, d), lambda i: (i, 0)),
                  pl.BlockSpec((1, d), lambda i: (0, 0)),
                  pl.BlockSpec((d, n), lambda i: (0, 0))],
        out_specs=pl.BlockSpec((tm, n), lambda i: (i, 0)),
        out_shape=jax.ShapeDtypeStruct((t, n), F32),
        compiler_params=_cparams(("parallel",)),
        name="norm_matmul",
    )(x, g, w)


FFN_SUBCHUNK = 512


def _swiglu_chunk(h, wg_ref, wu_ref, wd_ref):
    tf = wg_ref.shape[1]
    y = None
    for lo in range(0, tf, FFN_SUBCHUNK):
        hi = min(lo + FFN_SUBCHUNK, tf)
        a = _dot(h, wg_ref[:, lo:hi])
        b = _dot(h, wu_ref[:, lo:hi])
        part = _dot((a * _sigmoid(a) * b).astype(BF16), wd_ref[lo:hi, :])
        y = part if y is None else y + part
    return y


def _mixed_residual(x_ref, y_refs, w_ref, rows=slice(None)):
    y = jnp.concatenate([y_ref[rows, :] for y_ref in y_refs], axis=1)
    return x_ref[rows, :] + _dot(y, w_ref[...])


def _mix_in_specs(tm, d, imap):
    yspec = pl.BlockSpec((tm, MIX_BLOCK), imap(lambda i: (i, 0)))
    return [pl.BlockSpec((tm, d), imap(lambda i: (i, 0))), yspec, yspec, yspec, yspec,
            pl.BlockSpec((d, d), imap(lambda i: (0, 0)))]


def _out_proj_ffn_kernel(x_ref, y0_ref, y1_ref, y2_ref, y3_ref, wo_ref, g_ref, wg_ref, wu_ref, wd_ref,
                         o_ref, h_scr):
    @pl.when(pl.program_id(1) == 0)
    def _():
        x1 = _mixed_residual(x_ref, (y0_ref, y1_ref, y2_ref, y3_ref), wo_ref)
        h_scr[...] = _rms(x1, g_ref[...]).astype(BF16)
        o_ref[...] = x1

    o_ref[...] += _swiglu_chunk(h_scr[...], wg_ref, wu_ref, wd_ref)


def _out_proj_ffn(x, ys, w_out, g, wg, wu, wd, tm, tf):
    t, d = x.shape
    tm = min(tm, t)
    f = wg.shape[1]
    imap = lambda fn: (lambda i, j: fn(i))
    once = dict(pipeline_mode=pl.Buffered(1)) if tf == f else {}
    return pl.pallas_call(
        _out_proj_ffn_kernel,
        grid=(t // tm, f // tf),
        in_specs=_mix_in_specs(tm, d, imap)
                 + [pl.BlockSpec((1, d), lambda i, j: (0, 0)),
                    pl.BlockSpec((d, tf), lambda i, j: (0, j), **once),
                    pl.BlockSpec((d, tf), lambda i, j: (0, j), **once),
                    pl.BlockSpec((tf, d), lambda i, j: (j, 0), **once)],
        out_specs=pl.BlockSpec((tm, d), lambda i, j: (i, 0)),
        out_shape=jax.ShapeDtypeStruct((t, d), F32),
        scratch_shapes=[pltpu.VMEM((tm, d), BF16)],
        compiler_params=_cparams(("parallel", "arbitrary"), FFN_VMEM_LIMIT),
        name="out_proj_ffn",
    )(x, *ys, w_out, g, wg, wu, wd)


ROUTE_ID_LANES = (0, 1)
ROUTE_GATE_LANES = (2, 3)


def _split_bf16(x):
    hi = x.astype(BF16)
    return hi, (x - hi.astype(F32)).astype(BF16)


ROUTER_ROWS = 256


def _out_proj_router_kernel(x_ref, y0_ref, y1_ref, y2_ref, y3_ref, wo_ref, g_ref, wr_ref, x1_ref, c_ref):
    tm = x_ref.shape[0]
    step = min(ROUTER_ROWS, tm)
    for r in range(0, tm, step):
        _route_rows(slice(r, r + step), x_ref, (y0_ref, y1_ref, y2_ref, y3_ref), wo_ref, g_ref, wr_ref,
                    x1_ref, c_ref)


def _route_rows(rows, x_ref, y_refs, wo_ref, g_ref, wr_ref, x1_ref, c_ref):
    x1 = _mixed_residual(x_ref, y_refs, wo_ref, rows)
    x1_ref[rows, :] = x1
    h_hi, h_lo = _split_bf16(_rms(x1, g_ref[...]))
    logits = _dot(jnp.concatenate([h_hi, h_lo, h_hi], axis=1), wr_ref[...])
    lane = lax.broadcasted_iota(jnp.int32, logits.shape, 1).astype(F32)
    lg = jnp.where(lane < N_EXPERTS, logits, NEG)
    m1 = jnp.max(lg, axis=-1, keepdims=True)
    i1 = jnp.min(jnp.where(lg == m1, lane, 128.0), axis=-1, keepdims=True)
    lg2 = jnp.where(lane == i1, NEG, lg)
    m2 = jnp.max(lg2, axis=-1, keepdims=True)
    i2 = jnp.min(jnp.where(lg2 == m2, lane, 128.0), axis=-1, keepdims=True)
    ex = jnp.exp(m2 - m1)
    vals = (i1, i2, 1.0 / (1.0 + ex), ex / (1.0 + ex))
    out = jnp.zeros_like(logits)
    for ln, v in zip(ROUTE_ID_LANES + ROUTE_GATE_LANES, vals):
        out = jnp.where(lane == ln, v, out)
    c_ref[rows, :] = out


def _router_weights(wr):
    w = jnp.pad(wr.astype(F32), ((0, 0), (0, 128 - N_EXPERTS)))
    hi, lo = _split_bf16(w)
    return jnp.concatenate([hi, hi, lo], axis=0)


def _out_proj_router(x, ys, w_out, g, wr3, tm):
    t, d = x.shape
    tm = min(tm, t)
    imap = lambda fn: fn
    return pl.pallas_call(
        _out_proj_router_kernel,
        grid=(t // tm,),
        in_specs=_mix_in_specs(tm, d, imap)
                 + [pl.BlockSpec((1, d), lambda i: (0, 0)),
                    pl.BlockSpec((3 * d, 128), lambda i: (0, 0))],
        out_specs=[pl.BlockSpec((tm, d), lambda i: (i, 0)),
                   pl.BlockSpec((tm, 128), lambda i: (i, 0))],
        out_shape=[jax.ShapeDtypeStruct((t, d), F32), jax.ShapeDtypeStruct((t, 128), F32)],
        compiler_params=_cparams(("parallel",)),
        name="out_proj_router",
    )(x, *ys, w_out, g, wr3)


DMA_ISSUE_UNROLL = 8


def _row_copy(src, i, dst, j, sem):
    return pltpu.make_async_copy(src.at[pl.ds(i, 1)], dst.at[pl.ds(j, 1)], sem)


def _dispatch_kernel(meta_ref, pos_ref, *rest, td, tm, n_tiles, first_step):
    n_streams = len(first_step) - 1
    x_refs, (xs_hbm, zero_scr, sem) = rest[:n_streams], rest[n_streams:]
    step = pl.program_id(0)

    def zero_row(r):
        return _row_copy(zero_scr, 0, xs_hbm, r, sem)

    @pl.when(step == 0)
    def _():
        zero_scr[...] = jnp.zeros_like(zero_scr)
        n_used = meta_ref[2 * N_EXPERTS]
        tile_fills = [(i >= n_used, pltpu.make_async_copy(zero_scr, xs_hbm.at[pl.ds(i * tm, tm)], sem))
                      for i in range(n_tiles)]
        for cond, copy in tile_fills:
            pl.when(cond)(copy.start)
        for e in range(N_EXPERTS):
            lax.fori_loop(meta_ref[e], meta_ref[N_EXPERTS + e], lambda r, c: (zero_row(r).start(), c)[1], 0)
        for e in range(N_EXPERTS):
            lax.fori_loop(meta_ref[e], meta_ref[N_EXPERTS + e], lambda r, c: (zero_row(r).wait(), c)[1], 0)
        for cond, copy in tile_fills:
            pl.when(cond)(copy.wait)

    def scatter(x_ref):
        def issue(j, c):
            for k in range(2):
                _row_copy(x_ref, j, xs_hbm, pos_ref[0, 0, 2 * j + k], sem).start()
            return c

        lax.fori_loop(0, td, issue, 0, unroll=DMA_ISSUE_UNROLL)
        for _ in range(2):
            pltpu.make_async_copy(x_ref, xs_hbm.at[pl.ds(0, td)], sem).wait()

    for s, x_ref in enumerate(x_refs):
        pl.when((step >= first_step[s]) & (step < first_step[s + 1]))(functools.partial(scatter, x_ref))


def _dispatch(xs_list, pos, meta, n_rows, tm, td):
    d = xs_list[0].shape[1]
    td = min([td] + [x.shape[0] for x in xs_list])
    first_step = [0]
    for x in xs_list:
        first_step.append(first_step[-1] + x.shape[0] // td)
    n_steps = first_step[-1]
    pos3 = pos.reshape(n_steps, 1, 2 * td)

    def tile_map(s):
        lo, hi = first_step[s], first_step[s + 1]
        return lambda i, m: (jnp.clip(i, lo, hi - 1) - lo, 0)

    in_specs = [pl.BlockSpec((1, 1, 2 * td), lambda i, m: (i, 0, 0), memory_space=pltpu.SMEM)]
    in_specs += [pl.BlockSpec((td, d), tile_map(s)) for s in range(len(xs_list))]
    return pl.pallas_call(
        functools.partial(_dispatch_kernel, td=td, tm=tm, n_tiles=n_rows // tm, first_step=tuple(first_step)),
        grid_spec=pltpu.PrefetchScalarGridSpec(
            num_scalar_prefetch=1, grid=(n_steps,), in_specs=in_specs,
            out_specs=pl.BlockSpec(memory_space=pl.ANY),
            scratch_shapes=[pltpu.VMEM((tm, d), F32), pltpu.SemaphoreType.DMA]),
        out_shape=jax.ShapeDtypeStruct((n_rows, d), F32),
        compiler_params=_cparams(("arbitrary",)),
        name="moe_dispatch",
    )(meta, pos3, *xs_list)


def _grouped_ffn_kernel(te_ref, nu_ref, x_ref, g_ref, wg_ref, wu_ref, wd_ref, o_ref, h_scr):
    del te_ref
    j = pl.program_id(1)
    used = pl.program_id(0) < nu_ref[0]

    @pl.when(jnp.logical_not(used) & (j == 0))
    def _():
        o_ref[...] = jnp.zeros_like(o_ref)

    @pl.when(used)
    def _():
        @pl.when(j == 0)
        def _():
            h_scr[...] = _rms(x_ref[...], g_ref[...]).astype(BF16)

        y = _swiglu_chunk(h_scr[...], wg_ref.at[0], wu_ref.at[0], wd_ref.at[0])

        @pl.when(j == 0)
        def _():
            o_ref[...] = y

        @pl.when(j > 0)
        def _():
            o_ref[...] += y


def _grouped_ffn(xs, g, tile_expert, n_used, wg, wu, wd, tm, tf):
    r, d = xs.shape
    f = wg.shape[2]
    nj = f // tf

    def row_map(i, j, te, nu):
        return (i, 0)

    def col_of(i, j, nu):
        return jnp.where(i < nu[0], j, nj - 1)

    grid_spec = pltpu.PrefetchScalarGridSpec(
        num_scalar_prefetch=2,
        grid=(r // tm, nj),
        in_specs=[pl.BlockSpec((tm, d), row_map),
                  pl.BlockSpec((1, d), lambda i, j, te, nu: (0, 0)),
                  pl.BlockSpec((1, d, tf), lambda i, j, te, nu: (te[i], 0, col_of(i, j, nu))),
                  pl.BlockSpec((1, d, tf), lambda i, j, te, nu: (te[i], 0, col_of(i, j, nu))),
                  pl.BlockSpec((1, tf, d), lambda i, j, te, nu: (te[i], col_of(i, j, nu), 0))],
        out_specs=pl.BlockSpec((tm, d), row_map),
        scratch_shapes=[pltpu.VMEM((tm, d), BF16)],
    )
    return pl.pallas_call(
        _grouped_ffn_kernel,
        grid_spec=grid_spec,
        out_shape=jax.ShapeDtypeStruct((r, d), F32),
        compiler_params=_cparams(("arbitrary", "arbitrary"), FFN_VMEM_LIMIT),
        name="moe_grouped_ffn",
    )(tile_expert, n_used, xs, g, wg, wu, wd)


def _combine_kernel(pos_ref, pos_next_ref, x_ref, route_ref, ys_hbm, o_ref, buf0, buf1, sems, *, tc):
    step = pl.program_id(0)
    slot = step % 2

    def gather(p_ref, s):
        def issue(j, c):
            _row_copy(ys_hbm, p_ref[0, 0, 2 * j], buf0.at[s], j, sems.at[s]).start()
            _row_copy(ys_hbm, p_ref[0, 0, 2 * j + 1], buf1.at[s], j, sems.at[s]).start()
            return c

        lax.fori_loop(0, tc, issue, 0, unroll=DMA_ISSUE_UNROLL)

    pl.when(step == 0)(functools.partial(gather, pos_ref, 0))
    pl.when(step + 1 < pl.num_programs(0))(functools.partial(gather, pos_next_ref, 1 - slot))
    for buf in (buf0, buf1):
        pltpu.make_async_copy(ys_hbm.at[pl.ds(0, tc)], buf.at[slot], sems.at[slot]).wait()
    route = route_ref[...]
    g0 = route[:, ROUTE_GATE_LANES[0]:ROUTE_GATE_LANES[0] + 1]
    g1 = route[:, ROUTE_GATE_LANES[1]:ROUTE_GATE_LANES[1] + 1]
    o_ref[...] = x_ref[...] + g0 * buf0[slot] + g1 * buf1[slot]


def _combine(x, route, pos, ys, tc):
    t, d = x.shape
    tc = min(tc, t)
    n = t // tc
    pos3 = pos.reshape(n, 1, 2 * tc)
    pos_block = (1, 1, 2 * tc)
    return pl.pallas_call(
        functools.partial(_combine_kernel, tc=tc),
        grid=(n,),
        in_specs=[pl.BlockSpec(pos_block, lambda i: (i, 0, 0), memory_space=pltpu.SMEM),
                  pl.BlockSpec(pos_block, lambda i: (jnp.minimum(i + 1, n - 1), 0, 0), memory_space=pltpu.SMEM),
                  pl.BlockSpec((tc, d), lambda i: (i, 0)),
                  pl.BlockSpec((tc, 128), lambda i: (i, 0)),
                  pl.BlockSpec(memory_space=pl.ANY)],
        out_specs=pl.BlockSpec((tc, d), lambda i: (i, 0)),
        out_shape=jax.ShapeDtypeStruct((t, d), F32),
        scratch_shapes=[pltpu.VMEM((2, tc, d), F32), pltpu.VMEM((2, tc, d), F32),
                        pltpu.SemaphoreType.DMA((2,))],
        compiler_params=_cparams(("arbitrary",)),
        name="moe_combine",
    )(pos3, pos3, x, route, ys)


MOE_TM = 512


def _route_plan(expert_ids, tm):
    flat = expert_ids.reshape(-1)
    a = flat.shape[0]
    onehot = (flat[None, :] == jnp.arange(N_EXPERTS, dtype=jnp.int32)[:, None]).astype(jnp.int32)
    csum = jnp.cumsum(onehot, axis=1)
    counts = csum[:, -1]
    padded = (counts + tm - 1) // tm * tm
    ends = jnp.cumsum(padded)
    offs = ends - padded
    pos = jnp.sum(onehot * (offs[:, None] + csum - 1), axis=0)
    n_tiles = (a + N_EXPERTS * tm) // tm
    tile_start = jnp.arange(n_tiles, dtype=jnp.int32) * tm
    tile_expert = jnp.minimum(jnp.sum(tile_start[:, None] >= ends[None, :], axis=1), N_EXPERTS - 1)
    n_used = (ends[-1] // tm).reshape(1)
    last = jnp.take(tile_expert, n_used[0] - 1)
    tile_expert = jnp.where(tile_start < ends[-1], tile_expert, last)
    meta = jnp.concatenate([offs + counts, ends, n_used]).astype(jnp.int32)
    return pos.astype(jnp.int32), tile_expert.astype(jnp.int32), n_used.astype(jnp.int32), meta, n_tiles * tm


def _moe(xs_list, routes, g, wg, wu, wd):
    ids = jnp.concatenate([r[:, ROUTE_ID_LANES[0]:ROUTE_ID_LANES[1] + 1] for r in routes]).astype(jnp.int32)
    pos, tile_expert, n_used, meta, n_rows = _route_plan(ids, MOE_TM)
    bounds = np.cumsum([0] + [2 * x.shape[0] for x in xs_list])
    pos_list = [pos[lo:hi] for lo, hi in zip(bounds[:-1], bounds[1:])]
    xs = _dispatch(xs_list, pos, meta, n_rows, MOE_TM, 1024)
    ys = _grouped_ffn(xs, g, tile_expert, n_used, wg, wu, wd, MOE_TM, D_FF)
    return [_combine(x, r, ps, ys, 512) for x, r, ps in zip(xs_list, routes, pos_list)]


def _pool_kernel(u_ref, halo_ref, inv_ref, w_ref, scale_ref, o_ref, *, tiles_per_seq):
    ti = pl.program_id(0) % tiles_per_seq
    u = u_ref[...]
    halo = jnp.where(ti == 0, 0.0, halo_ref[...])
    ext = jnp.concatenate([halo, u], axis=0)
    half = POOL_WIDTH // 2
    short = lax.broadcasted_iota(jnp.int32, (ext.shape[0], half), 1) < half // 2

    def window_sums(x, n_doublings):
        sums = []
        for k in range(n_doublings):
            x = x + pltpu.roll(x, 2 ** k, 0)
            sums.append(x)
        return jnp.where(short, sums[-2], sums[-1])[POOL_HALO:]

    s = jnp.concatenate([window_sums(ext[:, :half], 2), window_sums(ext[:, half:], 4)], axis=1)
    inv_rest = inv_ref[1]
    inv_head = inv_ref[jnp.minimum(ti, 1)]
    pooled = jnp.concatenate([s[:POOL_HALO] * inv_head, s[POOL_HALO:] * inv_rest[:1]], axis=0) - u
    o_ref[...] = (_dot(pooled.astype(BF16), w_ref[...]) * scale_ref[...]).astype(o_ref.dtype)


def _pool(proj2d, col, w, scale, *, n_rows, tb, tiles_per_seq, pos0):
    per = tb // POOL_HALO
    win = np.repeat(np.asarray(POOL_WINDOWS), POOL_WIDTH // len(POOL_WINDOWS))[None, :]
    count = np.minimum(win, pos0 + np.arange(POOL_HALO)[:, None] + 1)
    inv = jnp.asarray(np.stack([1.0 / count, np.broadcast_to(1.0 / win, count.shape)]), F32)
    return pl.pallas_call(
        functools.partial(_pool_kernel, tiles_per_seq=tiles_per_seq),
        grid=(n_rows // tb,),
        in_specs=[pl.BlockSpec((tb, POOL_WIDTH), lambda i: (i, col)),
                  pl.BlockSpec((POOL_HALO, POOL_WIDTH), lambda i: (jnp.maximum(i * per - 1, 0), col)),
                  pl.BlockSpec((2, POOL_HALO, POOL_WIDTH), lambda i: (0, 0, 0)),
                  pl.BlockSpec((POOL_WIDTH, POOL_WIDTH), lambda i: (0, 0)),
                  pl.BlockSpec((1, POOL_WIDTH), lambda i: (0, 0))],
        out_specs=pl.BlockSpec((tb, POOL_WIDTH), lambda i: (i, 0)),
        out_shape=jax.ShapeDtypeStruct((n_rows, POOL_WIDTH), BF16),
        compiler_params=_cparams(("parallel",)),
        name="pool",
    )(proj2d, proj2d, inv, w, scale)


def _t5_bucket_np(rel):
    n = np.maximum(rel, 0)
    max_exact = T5_BUCKETS // 2
    nf = np.maximum(n, max_exact).astype(np.float32)
    large = max_exact + (np.log(nf / max_exact) / math.log(T5_MAX_DIST / max_exact)
                         * (T5_BUCKETS - max_exact)).astype(np.int32)
    large = np.minimum(large, T5_BUCKETS - 1)
    return np.where(n < max_exact, n, large)


def _swa_bias(rel_bias, rel):
    valid = (rel >= 0) & (rel < SWA_WINDOW)
    onehot = jnp.asarray(_t5_bucket_np(rel)[..., None] == np.arange(T5_BUCKETS), F32)
    b = jnp.einsum('qsb,bh->hqs', onehot, rel_bias.astype(F32), precision=lax.Precision.HIGHEST)
    return jnp.where(valid[None], b, NEG)


def _softmax_parts(parts, sink):
    m = sink
    for s in parts:
        m = jnp.maximum(m, jnp.max(s, axis=-1, keepdims=True))
    ps = [jnp.exp(s - m) for s in parts]
    denom = jnp.exp(sink - m)
    for p in ps:
        denom = denom + jnp.sum(p, axis=-1, keepdims=True)
    return ps, denom


def _head_mean_matrix(width):
    h = np.arange(width) // HEAD_DIM
    return jnp.asarray((h[:, None] == h[None, :]) / HEAD_DIM, BF16)


def _head_rms(x, mean_mat, g):
    ms = _dot((x * x).astype(BF16), mean_mat)
    return x * lax.rsqrt(ms + RMS_EPS) * g


SWA_TILE = 512


def _swa_prompt_kernel(sinks_ref, q_ref, kv_ref, halo_ref, qg_ref, kg_ref, mq_ref, mk_ref, bias_ref,
                       y_ref, kn_ref):
    has_prev = pl.program_id(1) > 0
    kw = SWA_KV_HEADS * HEAD_DIM
    blk = SWA_BLOCK
    kv = kv_ref[0]
    halo = halo_ref[0]
    k_ext = jnp.concatenate([halo[:, :kw], kv[:, :kw]], axis=0)
    v_ext = jnp.concatenate([halo[:, kw:], kv[:, kw:]], axis=0).astype(BF16)
    kn = _head_rms(k_ext, mk_ref[...], kg_ref[...])
    kn_ref[0] = kn[SWA_TILE:]
    knb = kn.astype(BF16)
    qn = (_head_rms(q_ref[0], mq_ref[...], qg_ref[...]) * (HEAD_DIM ** -0.5)).astype(BF16)
    row = lax.broadcasted_iota(jnp.int32, (2 * blk, 1), 0)
    col = lax.broadcasted_iota(jnp.int32, (2 * blk, 2 * blk), 1)
    for c in range(SWA_TILE // blk):
        rows = slice(c * blk, (c + 1) * blk)
        keys = slice(c * blk, (c + 2) * blk)
        for kh in range(SWA_KV_HEADS):
            ksl = slice(kh * HEAD_DIM, (kh + 1) * HEAD_DIM)
            h0 = 2 * kh
            q2 = jnp.concatenate([qn[rows, h0 * HEAD_DIM:(h0 + 1) * HEAD_DIM],
                                  qn[rows, (h0 + 1) * HEAD_DIM:(h0 + 2) * HEAD_DIM]], axis=0)
            s = _dot_nt(q2, knb[keys, ksl]) + bias_ref[kh]
            if c == 0:
                s = jnp.where(has_prev | (col >= blk), s, NEG)
            sink = jnp.where(row < blk, sinks_ref[h0], sinks_ref[h0 + 1])
            (p,), denom = _softmax_parts((s,), sink)
            o = _dot(p.astype(BF16), v_ext[keys, ksl]) / denom
            y_ref[0, rows, h0 * HEAD_DIM:(h0 + 1) * HEAD_DIM] = o[:blk].astype(y_ref.dtype)
            y_ref[0, rows, (h0 + 1) * HEAD_DIM:(h0 + 2) * HEAD_DIM] = o[blk:].astype(y_ref.dtype)


def _swa_prompt(proj3d, sinks, qg, kg, bias):
    b, l, _ = proj3d.shape
    kw = SWA_KV_HEADS * HEAD_DIM
    per = SWA_TILE // SWA_BLOCK
    tile = (1, SWA_TILE, MIX_BLOCK)
    const2 = lambda bi, i: (0, 0)
    bias2 = bias.reshape(SWA_KV_HEADS, 2 * SWA_BLOCK, 2 * SWA_BLOCK)
    return pl.pallas_call(
        _swa_prompt_kernel,
        grid=(b, l // SWA_TILE),
        in_specs=[pl.BlockSpec(memory_space=pltpu.SMEM),
                  pl.BlockSpec(tile, lambda bi, i: (bi, i, COL_SWA_Q)),
                  pl.BlockSpec(tile, lambda bi, i: (bi, i, COL_SWA_KV)),
                  pl.BlockSpec((1, SWA_BLOCK, MIX_BLOCK),
                               lambda bi, i: (bi, jnp.maximum(i * per - 1, 0), COL_SWA_KV)),
                  pl.BlockSpec((1, MIX_BLOCK), const2),
                  pl.BlockSpec((1, kw), const2),
                  pl.BlockSpec((MIX_BLOCK, MIX_BLOCK), const2),
                  pl.BlockSpec((kw, kw), const2),
                  pl.BlockSpec((SWA_KV_HEADS, 2 * SWA_BLOCK, 2 * SWA_BLOCK), lambda bi, i: (0, 0, 0))],
        out_specs=[pl.BlockSpec(tile, lambda bi, i: (bi, i, 0)),
                   pl.BlockSpec((1, SWA_BLOCK, kw), lambda bi, i: (bi, 0, 0))],
        out_shape=[jax.ShapeDtypeStruct((b, l, MIX_BLOCK), BF16),
                   jax.ShapeDtypeStruct((b, SWA_BLOCK, kw), F32)],
        compiler_params=_cparams(("parallel", "arbitrary")),
        name="swa_prompt",
    )(sinks, proj3d, proj3d, proj3d, jnp.tile(qg, (1, SWA_HEADS)), jnp.tile(kg, (1, SWA_KV_HEADS)),
      _head_mean_matrix(MIX_BLOCK), _head_mean_matrix(kw), bias2)


def _layer_state_out(layer, depth, blk, shape, dtype, prev):
    zeros = (0,) * (len(blk) - 1)
    shape = jax.ShapeDtypeStruct((depth,) + shape, dtype)
    if prev is None:
        assert layer == 0
        return pl.BlockSpec((depth,) + blk, lambda i: (0, i) + zeros), shape, [], []
    spec = pl.BlockSpec((1,) + blk, lambda i: (layer, i) + zeros)
    return spec, shape, [prev], [pl.BlockSpec(memory_space=pl.ANY)]


def _zero_other_layers(ref):
    if ref.shape[0] > 1:
        ref[1:] = jnp.zeros((ref.shape[0] - 1,) + ref.shape[1:], ref.dtype)


def _swa_sample_kernel(sinks_ref, q_ref, kv_ref, ckt_ref, cvt_ref, qg_ref, kg_ref, mq_ref, mk_ref, bias_ref,
                       *rest, n_new):
    y_ref, nkt_ref, nvt_ref = rest[-3:]
    _, s_blk, kw, wb = ckt_ref.shape
    keep = wb - n_new
    q2 = _head_rms(q_ref[...], mq_ref[...], qg_ref[...]) * (HEAD_DIM ** -0.5)
    kv = kv_ref[...]
    kn2 = _head_rms(kv[:, :kw], mk_ref[...], kg_ref[...])
    v2 = kv[:, kw:]
    q3 = q2.reshape(s_blk, n_new, MIX_BLOCK)
    kn3 = kn2.reshape(s_blk, n_new, kw)
    v3 = v2.reshape(s_blk, n_new, kw)
    ckt = ckt_ref[0]
    cvt = cvt_ref[0]
    bdot = functools.partial(jnp.einsum, preferred_element_type=F32)
    for h in range(SWA_HEADS):
        kh = h // (SWA_HEADS // SWA_KV_HEADS)
        ksl = slice(kh * HEAD_DIM, (kh + 1) * HEAD_DIM)
        hsl = slice(h * HEAD_DIM, (h + 1) * HEAD_DIM)
        qh = q3[:, :, hsl]
        s_c = bdot('sqd,sdw->sqw', qh, ckt[:, ksl, :]) + bias_ref[h, :, :wb]
        s_n = bdot('sqd,sjd->sqj', qh, kn3[:, :, ksl]) + bias_ref[h, :, wb:]
        (p_c, p_n), denom = _softmax_parts((s_c, s_n), sinks_ref[h])
        o = bdot('sqw,sdw->sqd', p_c, cvt[:, ksl, :]) + bdot('sqj,sjd->sqd', p_n, v3[:, :, ksl])
        y_ref[:, hsl] = (o / denom).reshape(s_blk * n_new, HEAD_DIM).astype(y_ref.dtype)
    lane = lax.broadcasted_iota(jnp.int32, (kw, wb), 1)
    for new2, old, out_ref in ((kn2, ckt, nkt_ref), (v2, cvt, nvt_ref)):
        new_t = new2.T
        shifted = pltpu.roll(old.reshape(s_blk * kw, wb), keep, 1).reshape(s_blk, kw, wb)
        _zero_other_layers(out_ref)
        for s in range(s_blk):
            cols = pltpu.roll(new_t, (keep - s * n_new) % wb, 1)
            out_ref[0, s] = jnp.where(lane >= keep, cols, shifted[s])


def _swa_sample(proj2d, cache_kt, cache_vt, layer, prev_new, sinks, qg, kg, bias, n_new, s_blk):
    depth, nseq, kw, wb = cache_kt.shape
    assert s_blk * n_new == wb
    rows = s_blk * n_new
    blk = (rows, MIX_BLOCK)
    cblk = (s_blk, kw, wb)
    cin = pl.BlockSpec((1,) + cblk, lambda i: (layer, i, 0, 0))
    const2 = lambda i: (0, 0)
    prev_k, prev_v = prev_new if prev_new is not None else (None, None)
    kspec, kshape, kin, kin_specs = _layer_state_out(layer, depth, cblk, (nseq, kw, wb), F32, prev_k)
    vspec, vshape, vin, vin_specs = _layer_state_out(layer, depth, cblk, (nseq, kw, wb), F32, prev_v)
    n_main = 10
    aliases = {n_main + j: 1 + j for j in range(len(kin + vin))}
    return pl.pallas_call(
        functools.partial(_swa_sample_kernel, n_new=n_new),
        grid=(nseq // s_blk,),
        in_specs=[pl.BlockSpec(memory_space=pltpu.SMEM),
                  pl.BlockSpec(blk, lambda i: (i, COL_SWA_Q)),
                  pl.BlockSpec(blk, lambda i: (i, COL_SWA_KV)),
                  cin, cin,
                  pl.BlockSpec((1, MIX_BLOCK), const2),
                  pl.BlockSpec((1, kw), const2),
                  pl.BlockSpec((MIX_BLOCK, MIX_BLOCK), const2),
                  pl.BlockSpec((kw, kw), const2),
                  pl.BlockSpec((SWA_HEADS, n_new, wb + n_new), lambda i: (0, 0, 0))] + kin_specs + vin_specs,
        out_specs=[pl.BlockSpec(blk, lambda i: (i, 0)), kspec, vspec],
        out_shape=[jax.ShapeDtypeStruct((nseq * n_new, MIX_BLOCK), BF16), kshape, vshape],
        input_output_aliases=aliases,
        compiler_params=_cparams(("parallel",)),
        name="swa_sample",
    )(sinks, proj2d, proj2d, cache_kt, cache_vt, jnp.tile(qg, (1, SWA_HEADS)), jnp.tile(kg, (1, SWA_KV_HEADS)),
      _head_mean_matrix(MIX_BLOCK), _head_mean_matrix(kw), bias, *kin, *vin)


def _ssm_kernel(u_ref, h0_ref, wb_ref, tab_ref, wc_ref, d_ref, wglu_ref, y_ref, hn_ref,
                bu_scr, carry_scr, *, chained, tiles_per_seq, tb):
    n = SSM_N

    def project_in(rows):
        bu_scr[rows, :] = _dot(u_ref[rows, :].astype(BF16), wb_ref[...])

    def project_out(rows):
        y = _dot(bu_scr[rows, :].astype(BF16), wc_ref[...]) + d_ref[...] * u_ref[rows, :]
        y = 0.5 * y * (1.0 + jnp.tanh(math.sqrt(2.0 / math.pi) * (y + 0.044715 * (y * y * y))))
        return (y * _sigmoid(_dot(y.astype(BF16), wglu_ref[...]))).astype(y_ref.dtype)

    def tile_scan(r0, cr, ci):
        hr = bu_scr[pl.ds(r0, SCAN_ROWS), :n]
        hi = bu_scr[pl.ds(r0, SCAN_ROWS), n:]
        for k, shift in enumerate((1, 2, 4)):
            ar, ai = tab_ref[2 * k], tab_ref[2 * k + 1]
            sr, si = pltpu.roll(hr, shift, 0), pltpu.roll(hi, shift, 0)
            hr, hi = hr + ar * sr - ai * si, hi + ar * si + ai * sr
        pr, pi = tab_ref[6], tab_ref[7]
        hr, hi = hr + pr * cr - pi * ci, hi + pr * ci + pi * cr
        bu_scr[pl.ds(r0, SCAN_ROWS), :n] = hr
        bu_scr[pl.ds(r0, SCAN_ROWS), n:] = hi
        return hr[SCAN_ROWS - 1:], hi[SCAN_ROWS - 1:]

    if chained:
        @pl.when(pl.program_id(1) % tiles_per_seq == 0)
        def _():
            carry_scr[...] = jnp.zeros_like(carry_scr)

        project_in(slice(None))
        cr, ci = carry_scr[:, :n], carry_scr[:, n:]
        for r0 in range(0, tb, SCAN_ROWS):
            lr, li = tile_scan(r0, cr, ci)
            cr, ci = jnp.broadcast_to(lr, (SCAN_ROWS, n)), jnp.broadcast_to(li, (SCAN_ROWS, n))
        y_ref[0] = project_out(slice(None))
        carry_scr[:, :n] = cr
        carry_scr[:, n:] = ci
        hn_ref[0, :, :n] = cr
        hn_ref[0, :, n:] = ci
    else:
        project_in(slice(None))
        for t in range(tb // SCAN_ROWS):
            h0 = h0_ref[t:t + 1, :]
            cr = jnp.broadcast_to(h0[:, :n], (SCAN_ROWS, n))
            ci = jnp.broadcast_to(h0[:, n:], (SCAN_ROWS, n))
            lr, li = tile_scan(t * SCAN_ROWS, cr, ci)
            hn_ref[t:t + 1, :n] = lr
            hn_ref[t:t + 1, n:] = li
        y_ref[...] = project_out(slice(None))


def _ssm_common_specs(zero_map2, zero_map3):
    return [pl.BlockSpec((SSM_WIDTH, 2 * SSM_N), zero_map2),
            pl.BlockSpec((8, SCAN_ROWS, SSM_N), zero_map3),
            pl.BlockSpec((2 * SSM_N, SSM_WIDTH), zero_map2),
            pl.BlockSpec((1, SSM_WIDTH), zero_map2),
            pl.BlockSpec((SSM_WIDTH, SSM_WIDTH), zero_map2)]


def _ssm_prompt(proj3d, sp, tb):
    b, l, _ = proj3d.shape
    nt = l // tb
    dummy_h0 = jnp.zeros((SCAN_ROWS, 2 * SSM_N), F32)
    kern = functools.partial(_ssm_kernel, chained=True, tiles_per_seq=nt, tb=tb)

    def kernel(u_ref, h0_ref, wb, tab, wc, d, wglu, y_ref, hn_ref, bu_scr, carry_scr):
        kern(u_ref.at[0], h0_ref, wb, tab, wc, d, wglu, y_ref, hn_ref, bu_scr, carry_scr)

    return pl.pallas_call(
        kernel,
        grid=(b, nt),
        in_specs=[pl.BlockSpec((1, tb, SSM_WIDTH), lambda bi, i: (bi, i, COL_SSM)),
                  pl.BlockSpec((SCAN_ROWS, 2 * SSM_N), lambda bi, i: (0, 0))]
                 + _ssm_common_specs(lambda bi, i: (0, 0), lambda bi, i: (0, 0, 0)),
        out_specs=[pl.BlockSpec((1, tb, SSM_WIDTH), lambda bi, i: (bi, i, 0)),
                   pl.BlockSpec((1, SCAN_ROWS, 2 * SSM_N), lambda bi, i: (bi, 0, 0))],
        out_shape=[jax.ShapeDtypeStruct((b, l, SSM_WIDTH), BF16),
                   jax.ShapeDtypeStruct((b, SCAN_ROWS, 2 * SSM_N), F32)],
        scratch_shapes=[pltpu.VMEM((tb, 2 * SSM_N), F32), pltpu.VMEM((SCAN_ROWS, 2 * SSM_N), F32)],
        compiler_params=_cparams(("parallel", "arbitrary")),
        name="ssm_prompt",
    )(proj3d, dummy_h0, sp["wb"], sp["tab"], sp["wc"], sp["d"], sp["wglu"])


def _ssm_sample(proj2d, h0, sp):
    rows = proj2d.shape[0]
    nseq = h0.shape[0]
    kern = functools.partial(_ssm_kernel, chained=False, tiles_per_seq=1, tb=rows)
    return pl.pallas_call(
        kern,
        grid=(1,),
        in_specs=[pl.BlockSpec((rows, SSM_WIDTH), lambda i: (0, COL_SSM)),
                  pl.BlockSpec((nseq, 2 * SSM_N), lambda i: (0, 0))]
                 + _ssm_common_specs(lambda i: (0, 0), lambda i: (0, 0, 0)),
        out_specs=[pl.BlockSpec((rows, SSM_WIDTH), lambda i: (0, 0)),
                   pl.BlockSpec((nseq, 2 * SSM_N), lambda i: (0, 0))],
        out_shape=[jax.ShapeDtypeStruct((rows, SSM_WIDTH), BF16),
                   jax.ShapeDtypeStruct((nseq, 2 * SSM_N), F32)],
        scratch_shapes=[pltpu.VMEM((rows, 2 * SSM_N), F32), pltpu.VMEM((SCAN_ROWS, 2 * SSM_N), F32)],
        compiler_params=_cparams(("arbitrary",)),
        name="ssm_sample",
    )(proj2d, h0, sp["wb"], sp["tab"], sp["wc"], sp["d"], sp["wglu"])


def _ssm_params(lam_re, lam_im, log_dt, b_re, b_im, c_re, c_im, d_skip, w_glu):
    lr, li = lam_re.astype(F32), lam_im.astype(F32)
    dt = jnp.exp(log_dt.astype(F32))[:, None]
    mag = jnp.exp(lr * dt)
    ab_re, ab_im = mag * jnp.cos(li * dt), mag * jnp.sin(li * dt)
    den = lr * lr + li * li
    nr = ab_re - 1.0
    f_re = (nr * lr + ab_im * li) / den
    f_im = (ab_im * lr - nr * li) / den
    br, bi = b_re.astype(F32), b_im.astype(F32)
    bb_re = f_re[..., None] * br - f_im[..., None] * bi
    bb_im = f_re[..., None] * bi + f_im[..., None] * br
    eye = jnp.eye(SSM_GROUPS, dtype=F32)

    def in_mat(bb):
        return jnp.einsum('gpc,gh->gchp', bb, eye).reshape(SSM_WIDTH, SSM_N)

    def out_mat(c):
        return jnp.einsum('gcp,gh->gphc', c.astype(F32), eye).reshape(SSM_N, SSM_WIDTH)

    wb = jnp.concatenate([in_mat(bb_re), in_mat(bb_im)], axis=1).astype(BF16)
    wc = jnp.concatenate([out_mat(c_re), -out_mat(c_im)], axis=0).astype(BF16)

    ar, ai = ab_re.reshape(1, SSM_N), ab_im.reshape(1, SSM_N)

    def cmul(x, y):
        return (x[0] * y[0] - x[1] * y[1], x[0] * y[1] + x[1] * y[0])

    pw = [(ar, ai)]
    for _ in range(SCAN_ROWS - 1):
        pw.append(cmul(pw[-1], (ar, ai)))
    row = jnp.arange(SCAN_ROWS)[:, None]
    tabs = []
    for shift in (1, 2, 4):
        for part in pw[shift - 1]:
            tabs.append(jnp.where(row >= shift, part, 0.0))
    tabs.append(jnp.concatenate([p[0] for p in pw], axis=0))
    tabs.append(jnp.concatenate([p[1] for p in pw], axis=0))
    tab = jnp.stack([jnp.broadcast_to(t, (SCAN_ROWS, SSM_N)) for t in tabs])
    return dict(wb=wb, tab=tab, wc=wc, d=d_skip.astype(F32).reshape(1, SSM_WIDTH), wglu=w_glu.astype(BF16))


_RET_G = 1.0 - np.exp2(-5.0 - np.arange(RET_HEADS, dtype=np.float64))


def _ret_consts(chunk, n_rows):
    idx = np.arange(n_rows)
    loc = idx % chunk
    same = (idx[:, None] // chunk) == (idx[None, :] // chunk)
    diff = loc[:, None] - loc[None, :]
    dec = np.where(same & (diff >= 0), _RET_G[:, None, None] ** np.maximum(diff, 0)[None], 0.0)
    qdec = np.repeat((_RET_G[None, :] ** (loc[:, None] + 1.0)), HEAD_DIM, axis=1)
    kdec = np.repeat((_RET_G[None, :] ** (chunk - 1.0 - loc[:, None])), HEAD_DIM, axis=1)
    return (jnp.asarray(dec, F32), jnp.asarray(qdec, F32), jnp.asarray(kdec, F32),
            jnp.asarray(_RET_G ** chunk, F32))


def _rope_tables(pos):
    half = HEAD_DIM // 2
    theta = 1.0 / (ROPE_BASE ** np.linspace(0.0, 1.0, half))
    ang = np.asarray(pos, np.float64)[:, None] * theta[None, :]
    cos = np.repeat(np.cos(ang), 2, axis=1)
    sin = np.repeat(np.sin(ang), 2, axis=1) * np.tile([-1.0, 1.0], half)[None]
    return (jnp.asarray(np.tile(cos, (1, RET_HEADS)), F32), jnp.asarray(np.tile(sin, (1, RET_HEADS)), F32))


def _rotate_pairs(x, cos, sin_signed):
    lane = lax.broadcasted_iota(jnp.int32, x.shape, 1)
    nxt = pltpu.roll(x, x.shape[1] - 1, 1)
    prv = pltpu.roll(x, 1, 1)
    return x * cos + jnp.where(lane % 2 == 0, nxt, prv) * sin_signed


def _ret_head_out(o, gate, norm):
    ms = jnp.mean(o * o, axis=-1, keepdims=True)
    return o * lax.rsqrt(ms + RMS_EPS) * norm * (gate * _sigmoid(gate))


RET_TILE = 1024


def _ret_prompt_kernel(gc_ref, q_ref, k_ref, v_ref, g_ref, cos_ref, sin_ref, dec_ref, qdec_ref,
                       kdec_ref, norm_ref, mh_ref, y_ref, r_ref, o_scr):
    @pl.when(pl.program_id(1) == 0)
    def _():
        r_ref[...] = jnp.zeros_like(r_ref)

    cos, sin = cos_ref[...], sin_ref[...]
    q = _rotate_pairs(q_ref[0], cos, sin)
    k = _rotate_pairs(k_ref[0], cos, sin) * (HEAD_DIM ** -0.5)
    qb, kb, vb = q.astype(BF16), k.astype(BF16), v_ref[0].astype(BF16)
    kdb = (k * kdec_ref[...]).astype(BF16)
    for c in range(RET_TILE // RET_CHUNK):
        rows = slice(c * RET_CHUNK, (c + 1) * RET_CHUNK)
        for h in range(RET_HEADS):
            sl = slice(h * HEAD_DIM, (h + 1) * HEAD_DIM)
            qh, vh = qb[rows, sl], vb[rows, sl]
            s = _dot_nt(qh, kb[rows, sl]) * dec_ref[h]
            r = r_ref[0, h]
            o_scr[rows, sl] = _dot(s.astype(BF16), vh) + _dot(qh, r.astype(BF16)) * qdec_ref[rows, sl]
            r_ref[0, h] = gc_ref[h] * r + _dot_tn(kdb[rows, sl], vh)
    g = g_ref[0]
    y_ref[0] = (_head_rms(o_scr[...], mh_ref[...], norm_ref[...]) * (g * _sigmoid(g))).astype(y_ref.dtype)


def _ret_prompt(proj3d, cos, sin, norm):
    b, l, _ = proj3d.shape
    c = RET_CHUNK
    dec, _, _, gc = _ret_consts(c, c)
    _, qdec, kdec, _ = _ret_consts(c, RET_TILE)
    blk = (1, RET_TILE, MIX_BLOCK)
    tspec = pl.BlockSpec((RET_TILE, MIX_BLOCK), lambda bi, i: (i, 0))
    cspec = pl.BlockSpec((RET_TILE, MIX_BLOCK), lambda bi, i: (0, 0))
    return pl.pallas_call(
        _ret_prompt_kernel,
        grid=(b, l // RET_TILE),
        in_specs=[pl.BlockSpec(memory_space=pltpu.SMEM),
                  pl.BlockSpec(blk, lambda bi, i: (bi, i, COL_RET_Q)),
                  pl.BlockSpec(blk, lambda bi, i: (bi, i, COL_RET_K)),
                  pl.BlockSpec(blk, lambda bi, i: (bi, i, COL_RET_V)),
                  pl.BlockSpec(blk, lambda bi, i: (bi, i, COL_RET_G)),
                  tspec, tspec,
                  pl.BlockSpec((RET_HEADS, c, c), lambda bi, i: (0, 0, 0)),
                  cspec, cspec,
                  pl.BlockSpec((1, MIX_BLOCK), lambda bi, i: (0, 0)),
                  pl.BlockSpec((MIX_BLOCK, MIX_BLOCK), lambda bi, i: (0, 0))],
        out_specs=[pl.BlockSpec(blk, lambda bi, i: (bi, i, 0)),
                   pl.BlockSpec((1, RET_HEADS, HEAD_DIM, HEAD_DIM), lambda bi, i: (bi, 0, 0, 0))],
        out_shape=[jax.ShapeDtypeStruct((b, l, MIX_BLOCK), BF16),
                   jax.ShapeDtypeStruct((b, RET_HEADS, HEAD_DIM, HEAD_DIM), F32)],
        scratch_shapes=[pltpu.VMEM((RET_TILE, MIX_BLOCK), F32)],
        compiler_params=_cparams(("parallel", "arbitrary")),
        name="ret_prompt",
    )(gc, proj3d, proj3d, proj3d, proj3d, cos, sin, dec, qdec, kdec, norm, _head_mean_matrix(MIX_BLOCK))


def _ret_sample_kernel(gc_ref, q_ref, k_ref, v_ref, g_ref, cos_ref, sin_ref, dec_ref, qdec_ref,
                       kdec_ref, norm_ref, r0_ref, *rest, n_new, s_blk):
    y_ref, rn_ref = rest[-2:]
    _zero_other_layers(rn_ref)
    cos, sin = cos_ref[...], sin_ref[...]
    q = _rotate_pairs(q_ref[...], cos, sin)
    k = _rotate_pairs(k_ref[...], cos, sin) * (HEAD_DIM ** -0.5)
    v = v_ref[...]
    g = g_ref[...]
    kd = k * kdec_ref[...]
    qdec = qdec_ref[...]
    norm = norm_ref[...]
    rows = s_blk * n_new
    seq = lax.broadcasted_iota(jnp.int32, (rows, HEAD_DIM), 0) // n_new
    seq_t = lax.broadcasted_iota(jnp.int32, (HEAD_DIM, rows), 1) // n_new
    kd_t = kd.T
    for h in range(RET_HEADS):
        sl = slice(h * HEAD_DIM, (h + 1) * HEAD_DIM)
        qf, kdt = q[:, sl], kd_t[sl, :]
        qh, kh, vh = qf.astype(BF16), k[:, sl].astype(BF16), v[:, sl].astype(BF16)
        s = _dot_nt(qh, kh) * dec_ref[h]
        cross = jnp.zeros((rows, HEAD_DIM), F32)
        for si in range(s_blk):
            mine = seq == si
            r = r0_ref[si, h]
            cross = cross + _dot(jnp.where(mine, qf, 0.0).astype(BF16), r.astype(BF16))
            rn_ref[0, si, h] = gc_ref[h] * r + _dot(jnp.where(seq_t == si, kdt, 0.0).astype(BF16), vh)
        o = _dot(s.astype(BF16), vh) + cross * qdec[:, sl]
        y_ref[:, sl] = _ret_head_out(o, g[:, sl], norm[:, sl]).astype(y_ref.dtype)


def _ret_sample(proj2d, r0, layer, depth, prev_new, cos, sin, norm, n_new, s_blk):
    rows = s_blk * n_new
    nseq = r0.shape[0]
    dec, qdec, kdec, gc = _ret_consts(n_new, rows)
    blk = (rows, MIX_BLOCK)
    cspec = pl.BlockSpec(blk, lambda i: (0, 0))
    rblk = (s_blk, RET_HEADS, HEAD_DIM, HEAD_DIM)
    rspec, rshape, rin, rin_specs = _layer_state_out(layer, depth, rblk, (nseq,) + rblk[1:], F32, prev_new)
    return pl.pallas_call(
        functools.partial(_ret_sample_kernel, n_new=n_new, s_blk=s_blk),
        grid=(nseq // s_blk,),
        in_specs=[pl.BlockSpec(memory_space=pltpu.SMEM),
                  pl.BlockSpec(blk, lambda i: (i, COL_RET_Q)),
                  pl.BlockSpec(blk, lambda i: (i, COL_RET_K)),
                  pl.BlockSpec(blk, lambda i: (i, COL_RET_V)),
                  pl.BlockSpec(blk, lambda i: (i, COL_RET_G)),
                  cspec, cspec,
                  pl.BlockSpec((RET_HEADS, rows, rows), lambda i: (0, 0, 0)),
                  cspec, cspec,
                  pl.BlockSpec((1, MIX_BLOCK), lambda i: (0, 0)),
                  pl.BlockSpec(rblk, lambda i: (i, 0, 0, 0))] + rin_specs,
        out_specs=[pl.BlockSpec(blk, lambda i: (i, 0)), rspec],
        out_shape=[jax.ShapeDtypeStruct((nseq * n_new, MIX_BLOCK), BF16), rshape],
        input_output_aliases={12 + j: 1 for j in range(len(rin))},
        compiler_params=_cparams(("parallel",)),
        name="ret_sample",
    )(gc, proj2d, proj2d, proj2d, proj2d, cos, sin, dec, qdec, kdec, norm, r0, *rin)


def _block_diag(w):
    g, n, _ = w.shape
    return jnp.einsum('gcd,gh->gchd', w, jnp.eye(g, dtype=w.dtype)).reshape(g * n, g * n)


def _layer_params(l, p):
    return dict(
        norm_mix=p['norm_mix'][l].reshape(1, D_MODEL),
        norm_ffn=p['norm_ffn'][l].reshape(1, D_MODEL),
        w_in=p['w_in'][l].astype(BF16),
        w_out=p['w_out'][l].astype(BF16),
        pool_w=_block_diag(p['pool_w'][l].astype(F32)).astype(BF16),
        pool_scale=p['pool_scale'][l].astype(F32).reshape(1, POOL_WIDTH),
        qg=p['swa_q_norm'][l].astype(F32).reshape(1, HEAD_DIM),
        kg=p['swa_k_norm'][l].astype(F32).reshape(1, HEAD_DIM),
        sinks=p['swa_sinks'][l].astype(F32),
        ssm=_ssm_params(p['ssm_lambda_re'][l], p['ssm_lambda_im'][l], p['ssm_log_dt'][l],
                        p['ssm_b_re'][l], p['ssm_b_im'][l], p['ssm_c_re'][l], p['ssm_c_im'][l],
                        p['ssm_d'][l], p['ssm_w_glu'][l]),
        ret_norm=p['ret_norm'][l].astype(F32).reshape(1, MIX_BLOCK),
    )


def _channel_mix(streams, l, lp, p):
    i = l // 2
    g, w_out = lp['norm_ffn'], lp['w_out']
    if l % 2 == 0:
        wg, wu, wd = (p[k][i].astype(BF16) for k in ('ffn_w_gate', 'ffn_w_up', 'ffn_w_down'))
        return [_out_proj_ffn(x, ys, w_out, g, wg, wu, wd, 1024, D_FF) for x, ys in streams]
    wr3 = _router_weights(p['moe_router'][i])
    wg, wu, wd = (p[k][i].astype(BF16) for k in ('moe_w_gate', 'moe_w_up', 'moe_w_down'))
    x1s, routes = zip(*[_out_proj_router(x, ys, w_out, g, wr3, 1024) for x, ys in streams])
    return _moe(x1s, routes, g, wg, wu, wd)


def _mix_prompt(x2, b, l, lp, bias, cos, sin):
    proj2 = _norm_matmul(x2, lp['norm_mix'], lp['w_in'], 1024)
    proj3 = proj2.reshape(b, l, IN_WIDTH)
    tb = min(4096, l)
    y_pool = _pool(proj2, COL_POOL, lp['pool_w'], lp['pool_scale'], n_rows=b * l, tb=tb,
                   tiles_per_seq=l // tb, pos0=0)
    y_swa, kn = _swa_prompt(proj3, lp['sinks'], lp['qg'], lp['kg'], bias)
    y_ssm, hn = _ssm_prompt(proj3, lp['ssm'], min(2048, l))
    y_ret, rn = _ret_prompt(proj3, cos, sin, lp['ret_norm'])
    ys = (y_pool, y_swa.reshape(b * l, MIX_BLOCK), y_ssm.reshape(b * l, MIX_BLOCK),
          y_ret.reshape(b * l, MIX_BLOCK))
    kw = SWA_KV_HEADS * HEAD_DIM
    hn = hn[:, 0]
    states = (proj3[:, l - POOL_BUF:, :POOL_WIDTH],
              kn.reshape(b, SWA_WINDOW, SWA_KV_HEADS, HEAD_DIM),
              proj3[:, l - SWA_WINDOW:, COL_SWA_KV * MIX_BLOCK + kw:(COL_SWA_KV + 1) * MIX_BLOCK]
              .reshape(b, SWA_WINDOW, SWA_KV_HEADS, HEAD_DIM),
              jnp.stack([hn[:, :SSM_N], hn[:, SSM_N:]], axis=-1).reshape(b, SSM_GROUPS, SSM_STATE, 2),
              rn)
    return ys, states


SAMPLE_SEQ_BLOCK = 16


def _cache_transposed(cache):
    depth, nseq, wb = cache.shape[:3]
    return jnp.swapaxes(cache.astype(F32).reshape(depth, nseq, wb, SWA_KV_HEADS * HEAD_DIM), 2, 3)


def _mix_sample(x2, nseq, n_new, start_pos, lp, layer, st, prev_new, bias, cos, sin):
    state_pool, cache_kt, cache_vt, state_ssm, state_ret = st
    prev_kv, prev_ret = prev_new if prev_new is not None else (None, None)
    rows = nseq * n_new
    kw = SWA_KV_HEADS * HEAD_DIM
    ext_rows = POOL_HALO + n_new
    proj2 = _norm_matmul(x2, lp['norm_mix'], lp['w_in'], 512)
    proj3 = proj2.reshape(nseq, n_new, IN_WIDTH)
    u_pool = proj3[:, :, :POOL_WIDTH]
    buf = state_pool.astype(F32)
    ext = jnp.concatenate([jnp.zeros((nseq, POOL_HALO - POOL_BUF, POOL_WIDTH), F32), buf, u_pool], axis=1)
    y_pool = _pool(ext.reshape(nseq * ext_rows, POOL_WIDTH), 0, lp['pool_w'], lp['pool_scale'],
                   n_rows=nseq * ext_rows, tb=nseq * ext_rows, tiles_per_seq=1, pos0=start_pos)
    y_pool = y_pool.reshape(nseq, ext_rows, POOL_WIDTH)[:, POOL_HALO:].reshape(rows, POOL_WIDTH)
    y_swa, nkt, nvt = _swa_sample(proj2, cache_kt, cache_vt, layer, prev_kv, lp['sinks'], lp['qg'], lp['kg'],
                                  bias, n_new, SAMPLE_SEQ_BLOCK)
    h0 = state_ssm.astype(F32).reshape(nseq, SSM_N, 2)
    h0 = jnp.concatenate([h0[..., 0], h0[..., 1]], axis=1)
    y_ssm, hn = _ssm_sample(proj2, h0, lp['ssm'])
    y_ret, rn = _ret_sample(proj2, state_ret[layer], layer, state_ret.shape[0], prev_ret, cos, sin,
                            lp['ret_norm'], n_new, SAMPLE_SEQ_BLOCK)
    ys = (y_pool, y_swa, y_ssm, y_ret)
    states = (jnp.concatenate([buf, u_pool], axis=1)[:, -POOL_BUF:],
              jnp.stack([hn[:, :SSM_N], hn[:, SSM_N:]], axis=-1).reshape(nseq, SSM_GROUPS, SSM_STATE, 2))
    return ys, states, ((nkt, nvt), rn)


def _forward(x_prompt, x_sample, past_len, sample_state, p, rel_bias):
    b, l, d = x_prompt.shape
    nseq, n_new, _ = x_sample.shape
    wb = sample_state[1].shape[2]
    depth = p['norm_mix'].shape[0]
    bias_p = _swa_bias(rel_bias, np.arange(SWA_BLOCK)[:, None] - np.arange(2 * SWA_BLOCK)[None, :] + SWA_BLOCK)
    bias_s = _swa_bias(rel_bias, np.arange(n_new)[:, None] - np.arange(wb + n_new)[None, :] + wb)
    rope_p = _rope_tables(np.arange(l))
    rope_s = _rope_tables(past_len + (np.arange(SAMPLE_SEQ_BLOCK * n_new) % n_new))
    xp = x_prompt.reshape(b * l, d)
    xs = x_sample.reshape(nseq * n_new, d)
    cache_kt, cache_vt = _cache_transposed(sample_state[1]), _cache_transposed(sample_state[2])
    st_p, st_s, stacked = [], [], None
    for li in range(depth):
        lp = _layer_params(li, p)
        yp, sp = _mix_prompt(xp, b, l, lp, bias_p, *rope_p)
        st = (sample_state[0][li], cache_kt, cache_vt, sample_state[3][li], sample_state[4])
        ys, ss, stacked = _mix_sample(xs, nseq, n_new, past_len, lp, li, st, stacked, bias_s, *rope_s)
        xp, xs = _channel_mix([(xp, yp), (xs, ys)], li, lp, p)
        st_p.append(sp)
        st_s.append(ss)
    (new_kt, new_vt), new_ret = stacked
    kv_shape = (depth, nseq, wb, SWA_KV_HEADS, HEAD_DIM)
    sample_out = (jnp.stack([s[0] for s in st_s]),
                  jnp.swapaxes(new_kt, 2, 3).reshape(kv_shape),
                  jnp.swapaxes(new_vt, 2, 3).reshape(kv_shape),
                  jnp.stack([s[1] for s in st_s]),
                  new_ret)
    outs = [xp.reshape(b, l, d), xs.reshape(nseq, n_new, d)]
    for k in range(5):
        outs.append(jnp.stack([s[k] for s in st_p]))
        outs.append(sample_out[k])
    return tuple(outs)


PAST_LEN = 16384


def kernel(x_prompt, x_sample, state_pool, cache_swa_k, cache_swa_v, state_ssm, state_ret,
           norm_mix, norm_ffn, w_in, w_out, pool_w, pool_scale, swa_q_norm, swa_k_norm, swa_sinks,
           rel_bias, ssm_lambda_re, ssm_lambda_im, ssm_log_dt, ssm_b_re, ssm_b_im, ssm_c_re, ssm_c_im,
           ssm_d, ssm_w_glu, ret_norm, ffn_w_gate, ffn_w_up, ffn_w_down, moe_router, moe_w_gate,
           moe_w_up, moe_w_down):
    p = dict(norm_mix=norm_mix, norm_ffn=norm_ffn, w_in=w_in, w_out=w_out, pool_w=pool_w,
             pool_scale=pool_scale, swa_q_norm=swa_q_norm, swa_k_norm=swa_k_norm, swa_sinks=swa_sinks,
             ssm_lambda_re=ssm_lambda_re, ssm_lambda_im=ssm_lambda_im, ssm_log_dt=ssm_log_dt,
             ssm_b_re=ssm_b_re, ssm_b_im=ssm_b_im, ssm_c_re=ssm_c_re, ssm_c_im=ssm_c_im,
             ssm_d=ssm_d, ssm_w_glu=ssm_w_glu, ret_norm=ret_norm,
             ffn_w_gate=ffn_w_gate, ffn_w_up=ffn_w_up, ffn_w_down=ffn_w_down, moe_router=moe_router,
             moe_w_gate=moe_w_gate, moe_w_up=moe_w_up, moe_w_down=moe_w_down)
    return _forward(x_prompt, x_sample, PAST_LEN,
                    (state_pool, cache_swa_k, cache_swa_v, state_ssm, state_ret), p, rel_bias)
```

```python
import functools
import math

import numpy as np
import jax
import jax.numpy as jnp
from jax import lax
from jax.experimental import pallas as pl
from jax.experimental.pallas import tpu as pltpu

F32 = jnp.float32
BF16 = jnp.bfloat16

D_MODEL = 1024
HEAD_DIM = 64
POOL_WIDTH = 256
POOL_WINDOWS = (2, 4, 8, 16)
POOL_BUF = 15
POOL_HALO = 16
SWA_HEADS = 4
SWA_KV_HEADS = 2
SWA_WINDOW = 128
SWA_BLOCK = 128
SSM_WIDTH = 256
SSM_CH = 16
SSM_GROUPS = 16
SSM_STATE = 64
SSM_N = SSM_GROUPS * SSM_STATE
RET_HEADS = 4
RET_CHUNK = 128
ROPE_BASE = 10000.0
IN_WIDTH = 2048
MIX_BLOCK = 256
D_FF = 2816
N_EXPERTS = 8
T5_BUCKETS = 32
T5_MAX_DIST = 128
RMS_EPS = 1e-6
NEG = -1e30
SUBLANES = 8
SCAN_ROWS = SUBLANES

COL_POOL, COL_SWA_Q, COL_SWA_KV, COL_SSM, COL_RET_Q, COL_RET_K, COL_RET_V, COL_RET_G = range(8)

VMEM_LIMIT = 48 * 1024 * 1024
FFN_VMEM_LIMIT = 58 * 1024 * 1024


def _cparams(sem, vmem=VMEM_LIMIT):
    return pltpu.CompilerParams(dimension_semantics=sem, vmem_limit_bytes=vmem)


def _rms(x, g):
    ms = jnp.mean(x * x, axis=-1, keepdims=True)
    return x * lax.rsqrt(ms + RMS_EPS) * g


def _dot(a, b):
    return jnp.dot(a, b, preferred_element_type=F32)


def _dot_nt(a, b):
    return lax.dot_general(a, b, (((1,), (1,)), ((), ())), preferred_element_type=F32)


def _dot_tn(a, b):
    return lax.dot_general(a, b, (((0,), (0,)), ((), ())), preferred_element_type=F32)


def _sigmoid(x):
    return 1.0 / (1.0 + jnp.exp(-x))


def _norm_matmul_kernel(x_ref, g_ref, w_ref, o_ref):
    h = _rms(x_ref[...], g_ref[...]).astype(BF16)
    o_ref[...] = _dot(h, w_ref[...])


def _norm_matmul(x, g, w, tm):
    t, d = x.shape
    tm = min(tm, t)
    n = w.shape[1]
    return pl.pallas_call(
        _norm_matmul_kernel,
        grid=(t // tm,),
        in_specs=[pl.BlockSpec((tm, d), lambda i: (i, 0)),
                  pl.BlockSpec((1, d), lambda i: (0, 0)),
                  pl.BlockSpec((d, n), lambda i: (0, 0))],
        out_specs=pl.BlockSpec((tm, n), lambda i: (i, 0)),
        out_shape=jax.ShapeDtypeStruct((t, n), F32),
        compiler_params=_cparams(("parallel",)),
        name="norm_matmul",
    )(x, g, w)


FFN_SUBCHUNK = 512


def _swiglu_chunk(h, wg_ref, wu_ref, wd_ref):
    tf = wg_ref.shape[1]
    y = None
    for lo in range(0, tf, FFN_SUBCHUNK):
        hi = min(lo + FFN_SUBCHUNK, tf)
        a = _dot(h, wg_ref[:, lo:hi])
        b = _dot(h, wu_ref[:, lo:hi])
        part = _dot((a * _sigmoid(a) * b).astype(BF16), wd_ref[lo:hi, :])
        y = part if y is None else y + part
    return y


def _mixed_residual(x_ref, y_refs, w_ref, rows=slice(None)):
    y = jnp.concatenate([y_ref[rows, :] for y_ref in y_refs], axis=1)
    return x_ref[rows, :] + _dot(y, w_ref[...])


def _mix_in_specs(tm, d, imap):
    yspec = pl.BlockSpec((tm, MIX_BLOCK), imap(lambda i: (i, 0)))
    return [pl.BlockSpec((tm, d), imap(lambda i: (i, 0))), yspec, yspec, yspec, yspec,
            pl.BlockSpec((d, d), imap(lambda i: (0, 0)))]


def _out_proj_ffn_kernel(x_ref, y0_ref, y1_ref, y2_ref, y3_ref, wo_ref, g_ref, wg_ref, wu_ref, wd_ref,
                         o_ref, h_scr):
    @pl.when(pl.program_id(1) == 0)
    def _():
        x1 = _mixed_residual(x_ref, (y0_ref, y1_ref, y2_ref, y3_ref), wo_ref)
        h_scr[...] = _rms(x1, g_ref[...]).astype(BF16)
        o_ref[...] = x1

    o_ref[...] += _swiglu_chunk(h_scr[...], wg_ref, wu_ref, wd_ref)


def _out_proj_ffn(x, ys, w_out, g, wg, wu, wd, tm, tf):
    t, d = x.shape
    tm = min(tm, t)
    f = wg.shape[1]
    imap = lambda fn: (lambda i, j: fn(i))
    once = dict(pipeline_mode=pl.Buffered(1)) if tf == f else {}
    return pl.pallas_call(
        _out_proj_ffn_kernel,
        grid=(t // tm, f // tf),
        in_specs=_mix_in_specs(tm, d, imap)
                 + [pl.BlockSpec((1, d), lambda i, j: (0, 0)),
                    pl.BlockSpec((d, tf), lambda i, j: (0, j), **once),
                    pl.BlockSpec((d, tf), lambda i, j: (0, j), **once),
                    pl.BlockSpec((tf, d), lambda i, j: (j, 0), **once)],
        out_specs=pl.BlockSpec((tm, d), lambda i, j: (i, 0)),
        out_shape=jax.ShapeDtypeStruct((t, d), F32),
        scratch_shapes=[pltpu.VMEM((tm, d), BF16)],
        compiler_params=_cparams(("parallel", "arbitrary"), FFN_VMEM_LIMIT),
        name="out_proj_ffn",
    )(x, *ys, w_out, g, wg, wu, wd)


ROUTE_ID_LANES = (0, 1)
ROUTE_GATE_LANES = (2, 3)


def _split_bf16(x):
    hi = x.astype(BF16)
    return hi, (x - hi.astype(F32)).astype(BF16)


ROUTER_ROWS = 256


def _out_proj_router_kernel(x_ref, y0_ref, y1_ref, y2_ref, y3_ref, wo_ref, g_ref, wr_ref, x1_ref, c_ref):
    tm = x_ref.shape[0]
    step = min(ROUTER_ROWS, tm)
    for r in range(0, tm, step):
        _route_rows(slice(r, r + step), x_ref, (y0_ref, y1_ref, y2_ref, y3_ref), wo_ref, g_ref, wr_ref,
                    x1_ref, c_ref)


def _route_rows(rows, x_ref, y_refs, wo_ref, g_ref, wr_ref, x1_ref, c_ref):
    x1 = _mixed_residual(x_ref, y_refs, wo_ref, rows)
    x1_ref[rows, :] = x1
    h_hi, h_lo = _split_bf16(_rms(x1, g_ref[...]))
    logits = _dot(jnp.concatenate([h_hi, h_lo, h_hi], axis=1), wr_ref[...])
    lane = lax.broadcasted_iota(jnp.int32, logits.shape, 1).astype(F32)
    lg = jnp.where(lane < N_EXPERTS, logits, NEG)
    m1 = jnp.max(lg, axis=-1, keepdims=True)
    i1 = jnp.min(jnp.where(lg == m1, lane, 128.0), axis=-1, keepdims=True)
    lg2 = jnp.where(lane == i1, NEG, lg)
    m2 = jnp.max(lg2, axis=-1, keepdims=True)
    i2 = jnp.min(jnp.where(lg2 == m2, lane, 128.0), axis=-1, keepdims=True)
    ex = jnp.exp(m2 - m1)
    vals = (i1, i2, 1.0 / (1.0 + ex), ex / (1.0 + ex))
    out = jnp.zeros_like(logits)
    for ln, v in zip(ROUTE_ID_LANES + ROUTE_GATE_LANES, vals):
        out = jnp.where(lane == ln, v, out)
    c_ref[rows, :] = out


def _router_weights(wr):
    w = jnp.pad(wr.astype(F32), ((0, 0), (0, 128 - N_EXPERTS)))
    hi, lo = _split_bf16(w)
    return jnp.concatenate([hi, hi, lo], axis=0)


def _out_proj_router(x, ys, w_out, g, wr3, tm):
    t, d = x.shape
    tm = min(tm, t)
    imap = lambda fn: fn
    return pl.pallas_call(
        _out_proj_router_kernel,
        grid=(t // tm,),
        in_specs=_mix_in_specs(tm, d, imap)
                 + [pl.BlockSpec((1, d), lambda i: (0, 0)),
                    pl.BlockSpec((3 * d, 128), lambda i: (0, 0))],
        out_specs=[pl.BlockSpec((tm, d), lambda i: (i, 0)),
                   pl.BlockSpec((tm, 128), lambda i: (i, 0))],
        out_shape=[jax.ShapeDtypeStruct((t, d), F32), jax.ShapeDtypeStruct((t, 128), F32)],
        compiler_params=_cparams(("parallel",)),
        name="out_proj_router",
    )(x, *ys, w_out, g, wr3)


DMA_ISSUE_UNROLL = 8


def _row_copy(src, i, dst, j, sem):
    return pltpu.make_async_copy(src.at[pl.ds(i, 1)], dst.at[pl.ds(j, 1)], sem)


def _dispatch_kernel(meta_ref, pos_ref, *rest, td, tm, n_tiles, first_step):
    n_streams = len(first_step) - 1
    x_refs, (xs_hbm, zero_scr, sem) = rest[:n_streams], rest[n_streams:]
    step = pl.program_id(0)

    def zero_row(r):
        return _row_copy(zero_scr, 0, xs_hbm, r, sem)

    @pl.when(step == 0)
    def _():
        zero_scr[...] = jnp.zeros_like(zero_scr)
        n_used = meta_ref[2 * N_EXPERTS]
        tile_fills = [(i >= n_used, pltpu.make_async_copy(zero_scr, xs_hbm.at[pl.ds(i * tm, tm)], sem))
                      for i in range(n_tiles)]
        for cond, copy in tile_fills:
            pl.when(cond)(copy.start)
        for e in range(N_EXPERTS):
            lax.fori_loop(meta_ref[e], meta_ref[N_EXPERTS + e], lambda r, c: (zero_row(r).start(), c)[1], 0)
        for e in range(N_EXPERTS):
            lax.fori_loop(meta_ref[e], meta_ref[N_EXPERTS + e], lambda r, c: (zero_row(r).wait(), c)[1], 0)
        for cond, copy in tile_fills:
            pl.when(cond)(copy.wait)

    def scatter(x_ref):
        def issue(j, c):
            for k in range(2):
                _row_copy(x_ref, j, xs_hbm, pos_ref[0, 0, 2 * j + k], sem).start(priority=k)
            return c

        lax.fori_loop(0, td, issue, 0, unroll=DMA_ISSUE_UNROLL)
        for _ in range(2):
            pltpu.make_async_copy(x_ref, xs_hbm.at[pl.ds(0, td)], sem).wait()

    for s, x_ref in enumerate(x_refs):
        pl.when((step >= first_step[s]) & (step < first_step[s + 1]))(functools.partial(scatter, x_ref))


def _dispatch(xs_list, pos, meta, n_rows, tm, td):
    d = xs_list[0].shape[1]
    td = min([td] + [x.shape[0] for x in xs_list])
    first_step = [0]
    for x in xs_list:
        first_step.append(first_step[-1] + x.shape[0] // td)
    n_steps = first_step[-1]
    pos3 = pos.reshape(n_steps, 1, 2 * td)

    def tile_map(s):
        lo, hi = first_step[s], first_step[s + 1]
        return lambda i, m: (jnp.clip(i, lo, hi - 1) - lo, 0)

    in_specs = [pl.BlockSpec((1, 1, 2 * td), lambda i, m: (i, 0, 0), memory_space=pltpu.SMEM)]
    in_specs += [pl.BlockSpec((td, d), tile_map(s)) for s in range(len(xs_list))]
    return pl.pallas_call(
        functools.partial(_dispatch_kernel, td=td, tm=tm, n_tiles=n_rows // tm, first_step=tuple(first_step)),
        grid_spec=pltpu.PrefetchScalarGridSpec(
            num_scalar_prefetch=1, grid=(n_steps,), in_specs=in_specs,
            out_specs=pl.BlockSpec(memory_space=pl.ANY),
            scratch_shapes=[pltpu.VMEM((tm, d), F32), pltpu.SemaphoreType.DMA]),
        out_shape=jax.ShapeDtypeStruct((n_rows, d), F32),
        compiler_params=_cparams(("arbitrary",)),
        name="moe_dispatch",
    )(meta, pos3, *xs_list)


def _grouped_ffn_kernel(te_ref, nu_ref, x_ref, g_ref, wg_ref, wu_ref, wd_ref, o_ref, h_scr):
    del te_ref
    j = pl.program_id(1)
    used = pl.program_id(0) < nu_ref[0]

    @pl.when(jnp.logical_not(used) & (j == 0))
    def _():
        o_ref[...] = jnp.zeros_like(o_ref)

    @pl.when(used)
    def _():
        @pl.when(j == 0)
        def _():
            h_scr[...] = _rms(x_ref[...], g_ref[...]).astype(BF16)

        y = _swiglu_chunk(h_scr[...], wg_ref.at[0], wu_ref.at[0], wd_ref.at[0])

        @pl.when(j == 0)
        def _():
            o_ref[...] = y

        @pl.when(j > 0)
        def _():
            o_ref[...] += y


def _grouped_ffn(xs, g, tile_expert, n_used, wg, wu, wd, tm, tf):
    r, d = xs.shape
    f = wg.shape[2]
    nj = f // tf

    def row_map(i, j, te, nu):
        return (i, 0)

    def col_of(i, j, nu):
        return jnp.where(i < nu[0], j, nj - 1)

    grid_spec = pltpu.PrefetchScalarGridSpec(
        num_scalar_prefetch=2,
        grid=(r // tm, nj),
        in_specs=[pl.BlockSpec((tm, d), row_map),
                  pl.BlockSpec((1, d), lambda i, j, te, nu: (0, 0)),
                  pl.BlockSpec((1, d, tf), lambda i, j, te, nu: (te[i], 0, col_of(i, j, nu))),
                  pl.BlockSpec((1, d, tf), lambda i, j, te, nu: (te[i], 0, col_of(i, j, nu))),
                  pl.BlockSpec((1, tf, d), lambda i, j, te, nu: (te[i], col_of(i, j, nu), 0))],
        out_specs=pl.BlockSpec((tm, d), row_map),
        scratch_shapes=[pltpu.VMEM((tm, d), BF16)],
    )
    return pl.pallas_call(
        _grouped_ffn_kernel,
        grid_spec=grid_spec,
        out_shape=jax.ShapeDtypeStruct((r, d), F32),
        compiler_params=_cparams(("arbitrary", "arbitrary"), FFN_VMEM_LIMIT),
        name="moe_grouped_ffn",
    )(tile_expert, n_used, xs, g, wg, wu, wd)


def _combine_kernel(pos_ref, pos_next_ref, x_ref, route_ref, ys_hbm, o_ref, buf0, buf1, sems, *, tc):
    step = pl.program_id(0)
    slot = step % 2

    def gather(p_ref, s):
        def issue(j, c):
            _row_copy(ys_hbm, p_ref[0, 0, 2 * j], buf0.at[s], j, sems.at[s]).start(priority=0)
            _row_copy(ys_hbm, p_ref[0, 0, 2 * j + 1], buf1.at[s], j, sems.at[s]).start(priority=1)
            return c

        lax.fori_loop(0, tc, issue, 0, unroll=DMA_ISSUE_UNROLL)

    pl.when(step == 0)(functools.partial(gather, pos_ref, 0))
    pl.when(step + 1 < pl.num_programs(0))(functools.partial(gather, pos_next_ref, 1 - slot))
    for buf in (buf0, buf1):
        pltpu.make_async_copy(ys_hbm.at[pl.ds(0, tc)], buf.at[slot], sems.at[slot]).wait()
    route = route_ref[...]
    g0 = route[:, ROUTE_GATE_LANES[0]:ROUTE_GATE_LANES[0] + 1]
    g1 = route[:, ROUTE_GATE_LANES[1]:ROUTE_GATE_LANES[1] + 1]
    o_ref[...] = x_ref[...] + g0 * buf0[slot] + g1 * buf1[slot]


def _combine(x, route, pos, ys, tc):
    t, d = x.shape
    tc = min(tc, t)
    n = t // tc
    pos3 = pos.reshape(n, 1, 2 * tc)
    pos_block = (1, 1, 2 * tc)
    return pl.pallas_call(
        functools.partial(_combine_kernel, tc=tc),
        grid=(n,),
        in_specs=[pl.BlockSpec(pos_block, lambda i: (i, 0, 0), memory_space=pltpu.SMEM),
                  pl.BlockSpec(pos_block, lambda i: (jnp.minimum(i + 1, n - 1), 0, 0), memory_space=pltpu.SMEM),
                  pl.BlockSpec((tc, d), lambda i: (i, 0)),
                  pl.BlockSpec((tc, 128), lambda i: (i, 0)),
                  pl.BlockSpec(memory_space=pl.ANY)],
        out_specs=pl.BlockSpec((tc, d), lambda i: (i, 0)),
        out_shape=jax.ShapeDtypeStruct((t, d), F32),
        scratch_shapes=[pltpu.VMEM((2, tc, d), F32), pltpu.VMEM((2, tc, d), F32),
                        pltpu.SemaphoreType.DMA((2,))],
        compiler_params=_cparams(("arbitrary",)),
        name="moe_combine",
    )(pos3, pos3, x, route, ys)


MOE_TM = 512


def _route_plan(expert_ids, tm):
    flat = expert_ids.reshape(-1)
    a = flat.shape[0]
    onehot = (flat[None, :] == jnp.arange(N_EXPERTS, dtype=jnp.int32)[:, None]).astype(jnp.int32)
    csum = jnp.cumsum(onehot, axis=1)
    counts = csum[:, -1]
    padded = (counts + tm - 1) // tm * tm
    ends = jnp.cumsum(padded)
    offs = ends - padded
    pos = jnp.sum(onehot * (offs[:, None] + csum - 1), axis=0)
    n_tiles = (a + N_EXPERTS * tm) // tm
    tile_start = jnp.arange(n_tiles, dtype=jnp.int32) * tm
    tile_expert = jnp.minimum(jnp.sum(tile_start[:, None] >= ends[None, :], axis=1), N_EXPERTS - 1)
    n_used = (ends[-1] // tm).reshape(1)
    last = jnp.take(tile_expert, n_used[0] - 1)
    tile_expert = jnp.where(tile_start < ends[-1], tile_expert, last)
    meta = jnp.concatenate([offs + counts, ends, n_used]).astype(jnp.int32)
    return pos.astype(jnp.int32), tile_expert.astype(jnp.int32), n_used.astype(jnp.int32), meta, n_tiles * tm


def _moe(xs_list, routes, g, wg, wu, wd):
    ids = jnp.concatenate([r[:, ROUTE_ID_LANES[0]:ROUTE_ID_LANES[1] + 1] for r in routes]).astype(jnp.int32)
    pos, tile_expert, n_used, meta, n_rows = _route_plan(ids, MOE_TM)
    bounds = np.cumsum([0] + [2 * x.shape[0] for x in xs_list])
    pos_list = [pos[lo:hi] for lo, hi in zip(bounds[:-1], bounds[1:])]
    xs = _dispatch(xs_list, pos, meta, n_rows, MOE_TM, 1024)
    ys = _grouped_ffn(xs, g, tile_expert, n_used, wg, wu, wd, MOE_TM, D_FF)
    return [_combine(x, r, ps, ys, 512) for x, r, ps in zip(xs_list, routes, pos_list)]


def _pool_kernel(u_ref, halo_ref, inv_ref, w_ref, scale_ref, o_ref, *, tiles_per_seq):
    ti = pl.program_id(0) % tiles_per_seq
    u = u_ref[...]
    halo = jnp.where(ti == 0, 0.0, halo_ref[...])
    ext = jnp.concatenate([halo, u], axis=0)
    half = POOL_WIDTH // 2
    short = lax.broadcasted_iota(jnp.int32, (ext.shape[0], half), 1) < half // 2

    def window_sums(x, n_doublings):
        sums = []
        for k in range(n_doublings):
            x = x + pltpu.roll(x, 2 ** k, 0)
            sums.append(x)
        return jnp.where(short, sums[-2], sums[-1])[POOL_HALO:]

    s = jnp.concatenate([window_sums(ext[:, :half], 2), window_sums(ext[:, half:], 4)], axis=1)
    inv_rest = inv_ref[1]
    inv_head = inv_ref[jnp.minimum(ti, 1)]
    pooled = jnp.concatenate([s[:POOL_HALO] * inv_head, s[POOL_HALO:] * inv_rest[:1]], axis=0) - u
    o_ref[...] = (_dot(pooled.astype(BF16), w_ref[...]) * scale_ref[...]).astype(o_ref.dtype)


def _pool(proj2d, col, w, scale, *, n_rows, tb, tiles_per_seq, pos0):
    per = tb // POOL_HALO
    win = np.repeat(np.asarray(POOL_WINDOWS), POOL_WIDTH // len(POOL_WINDOWS))[None, :]
    count = np.minimum(win, pos0 + np.arange(POOL_HALO)[:, None] + 1)
    inv = jnp.asarray(np.stack([1.0 / count, np.broadcast_to(1.0 / win, count.shape)]), F32)
    return pl.pallas_call(
        functools.partial(_pool_kernel, tiles_per_seq=tiles_per_seq),
        grid=(n_rows // tb,),
        in_specs=[pl.BlockSpec((tb, POOL_WIDTH), lambda i: (i, col)),
                  pl.BlockSpec((POOL_HALO, POOL_WIDTH), lambda i: (jnp.maximum(i * per - 1, 0), col)),
                  pl.BlockSpec((2, POOL_HALO, POOL_WIDTH), lambda i: (0, 0, 0)),
                  pl.BlockSpec((POOL_WIDTH, POOL_WIDTH), lambda i: (0, 0)),
                  pl.BlockSpec((1, POOL_WIDTH), lambda i: (0, 0))],
        out_specs=pl.BlockSpec((tb, POOL_WIDTH), lambda i: (i, 0)),
        out_shape=jax.ShapeDtypeStruct((n_rows, POOL_WIDTH), BF16),
        compiler_params=_cparams(("parallel",)),
        name="pool",
    )(proj2d, proj2d, inv, w, scale)


def _t5_bucket_np(rel):
    n = np.maximum(rel, 0)
    max_exact = T5_BUCKETS // 2
    nf = np.maximum(n, max_exact).astype(np.float32)
    large = max_exact + (np.log(nf / max_exact) / math.log(T5_MAX_DIST / max_exact)
                         * (T5_BUCKETS - max_exact)).astype(np.int32)
    large = np.minimum(large, T5_BUCKETS - 1)
    return np.where(n < max_exact, n, large)


def _swa_bias(rel_bias, rel):
    valid = (rel >= 0) & (rel < SWA_WINDOW)
    onehot = jnp.asarray(_t5_bucket_np(rel)[..., None] == np.arange(T5_BUCKETS), F32)
    b = jnp.einsum('qsb,bh->hqs', onehot, rel_bias.astype(F32), precision=lax.Precision.HIGHEST)
    return jnp.where(valid[None], b, NEG)


def _softmax_parts(parts, sink):
    m = sink
    for s in parts:
        m = jnp.maximum(m, jnp.max(s, axis=-1, keepdims=True))
    ps = [jnp.exp(s - m) for s in parts]
    denom = jnp.exp(sink - m)
    for p in ps:
        denom = denom + jnp.sum(p, axis=-1, keepdims=True)
    return ps, denom


def _head_mean_matrix(width):
    h = np.arange(width) // HEAD_DIM
    return jnp.asarray((h[:, None] == h[None, :]) / HEAD_DIM, BF16)


def _head_rms(x, mean_mat, g):
    ms = _dot((x * x).astype(BF16), mean_mat)
    return x * lax.rsqrt(ms + RMS_EPS) * g


SWA_TILE = 512


def _swa_prompt_kernel(sinks_ref, q_ref, kv_ref, halo_ref, qg_ref, kg_ref, mq_ref, mk_ref, bias_ref,
                       y_ref, kn_ref):
    has_prev = pl.program_id(1) > 0
    kw = SWA_KV_HEADS * HEAD_DIM
    blk = SWA_BLOCK
    kv = kv_ref[0]
    halo = halo_ref[0]
    k_ext = jnp.concatenate([halo[:, :kw], kv[:, :kw]], axis=0)
    v_ext = jnp.concatenate([halo[:, kw:], kv[:, kw:]], axis=0).astype(BF16)
    kn = _head_rms(k_ext, mk_ref[...], kg_ref[...])
    kn_ref[0] = kn[SWA_TILE:]
    knb = kn.astype(BF16)
    qn = (_head_rms(q_ref[0], mq_ref[...], qg_ref[...]) * (HEAD_DIM ** -0.5)).astype(BF16)
    row = lax.broadcasted_iota(jnp.int32, (2 * blk, 1), 0)
    col = lax.broadcasted_iota(jnp.int32, (2 * blk, 2 * blk), 1)
    for c in range(SWA_TILE // blk):
        rows = slice(c * blk, (c + 1) * blk)
        keys = slice(c * blk, (c + 2) * blk)
        for kh in range(SWA_KV_HEADS):
            ksl = slice(kh * HEAD_DIM, (kh + 1) * HEAD_DIM)
            h0 = 2 * kh
            q2 = jnp.concatenate([qn[rows, h0 * HEAD_DIM:(h0 + 1) * HEAD_DIM],
                                  qn[rows, (h0 + 1) * HEAD_DIM:(h0 + 2) * HEAD_DIM]], axis=0)
            s = _dot_nt(q2, knb[keys, ksl]) + bias_ref[kh]
            if c == 0:
                s = jnp.where(has_prev | (col >= blk), s, NEG)
            sink = jnp.where(row < blk, sinks_ref[h0], sinks_ref[h0 + 1])
            (p,), denom = _softmax_parts((s,), sink)
            o = _dot(p.astype(BF16), v_ext[keys, ksl]) / denom
            y_ref[0, rows, h0 * HEAD_DIM:(h0 + 1) * HEAD_DIM] = o[:blk].astype(y_ref.dtype)
            y_ref[0, rows, (h0 + 1) * HEAD_DIM:(h0 + 2) * HEAD_DIM] = o[blk:].astype(y_ref.dtype)


def _swa_prompt(proj3d, sinks, qg, kg, bias):
    b, l, _ = proj3d.shape
    kw = SWA_KV_HEADS * HEAD_DIM
    per = SWA_TILE // SWA_BLOCK
    tile = (1, SWA_TILE, MIX_BLOCK)
    const2 = lambda bi, i: (0, 0)
    bias2 = bias.reshape(SWA_KV_HEADS, 2 * SWA_BLOCK, 2 * SWA_BLOCK)
    return pl.pallas_call(
        _swa_prompt_kernel,
        grid=(b, l // SWA_TILE),
        in_specs=[pl.BlockSpec(memory_space=pltpu.SMEM),
                  pl.BlockSpec(tile, lambda bi, i: (bi, i, COL_SWA_Q)),
                  pl.BlockSpec(tile, lambda bi, i: (bi, i, COL_SWA_KV)),
                  pl.BlockSpec((1, SWA_BLOCK, MIX_BLOCK),
                               lambda bi, i: (bi, jnp.maximum(i * per - 1, 0), COL_SWA_KV)),
                  pl.BlockSpec((1, MIX_BLOCK), const2),
                  pl.BlockSpec((1, kw), const2),
                  pl.BlockSpec((MIX_BLOCK, MIX_BLOCK), const2),
                  pl.BlockSpec((kw, kw), const2),
                  pl.BlockSpec((SWA_KV_HEADS, 2 * SWA_BLOCK, 2 * SWA_BLOCK), lambda bi, i: (0, 0, 0))],
        out_specs=[pl.BlockSpec(tile, lambda bi, i: (bi, i, 0)),
                   pl.BlockSpec((1, SWA_BLOCK, kw), lambda bi, i: (bi, 0, 0))],
        out_shape=[jax.ShapeDtypeStruct((b, l, MIX_BLOCK), BF16),
                   jax.ShapeDtypeStruct((b, SWA_BLOCK, kw), F32)],
        compiler_params=_cparams(("parallel", "arbitrary")),
        name="swa_prompt",
    )(sinks, proj3d, proj3d, proj3d, jnp.tile(qg, (1, SWA_HEADS)), jnp.tile(kg, (1, SWA_KV_HEADS)),
      _head_mean_matrix(MIX_BLOCK), _head_mean_matrix(kw), bias2)


def _layer_state_out(layer, depth, blk, shape, dtype, prev):
    zeros = (0,) * (len(blk) - 1)
    shape = jax.ShapeDtypeStruct((depth,) + shape, dtype)
    if prev is None:
        assert layer == 0
        return pl.BlockSpec((depth,) + blk, lambda i: (0, i) + zeros), shape, [], []
    spec = pl.BlockSpec((1,) + blk, lambda i: (layer, i) + zeros)
    return spec, shape, [prev], [pl.BlockSpec(memory_space=pl.ANY)]


def _zero_other_layers(ref):
    if ref.shape[0] > 1:
        ref[1:] = jnp.zeros((ref.shape[0] - 1,) + ref.shape[1:], ref.dtype)


def _swa_sample_kernel(sinks_ref, q_ref, kv_ref, ckt_ref, cvt_ref, qg_ref, kg_ref, mq_ref, mk_ref, bias_ref,
                       *rest, n_new):
    y_ref, nkt_ref, nvt_ref = rest[-3:]
    _, s_blk, kw, wb = ckt_ref.shape
    keep = wb - n_new
    q2 = _head_rms(q_ref[...], mq_ref[...], qg_ref[...]) * (HEAD_DIM ** -0.5)
    kv = kv_ref[...]
    kn2 = _head_rms(kv[:, :kw], mk_ref[...], kg_ref[...])
    v2 = kv[:, kw:]
    q3 = q2.reshape(s_blk, n_new, MIX_BLOCK)
    kn3 = kn2.reshape(s_blk, n_new, kw)
    v3 = v2.reshape(s_blk, n_new, kw)
    ckt = ckt_ref[0]
    cvt = cvt_ref[0]
    bdot = functools.partial(jnp.einsum, preferred_element_type=F32)
    for h in range(SWA_HEADS):
        kh = h // (SWA_HEADS // SWA_KV_HEADS)
        ksl = slice(kh * HEAD_DIM, (kh + 1) * HEAD_DIM)
        hsl = slice(h * HEAD_DIM, (h + 1) * HEAD_DIM)
        qh = q3[:, :, hsl]
        s_c = bdot('sqd,sdw->sqw', qh, ckt[:, ksl, :]) + bias_ref[h, :, :wb]
        s_n = bdot('sqd,sjd->sqj', qh, kn3[:, :, ksl]) + bias_ref[h, :, wb:]
        (p_c, p_n), denom = _softmax_parts((s_c, s_n), sinks_ref[h])
        o = bdot('sqw,sdw->sqd', p_c, cvt[:, ksl, :]) + bdot('sqj,sjd->sqd', p_n, v3[:, :, ksl])
        y_ref[:, hsl] = (o / denom).reshape(s_blk * n_new, HEAD_DIM).astype(y_ref.dtype)
    lane = lax.broadcasted_iota(jnp.int32, (kw, wb), 1)
    for new2, old, out_ref in ((kn2, ckt, nkt_ref), (v2, cvt, nvt_ref)):
        new_t = new2.T
        shifted = pltpu.roll(old.reshape(s_blk * kw, wb), keep, 1).reshape(s_blk, kw, wb)
        _zero_other_layers(out_ref)
        for s in range(s_blk):
            cols = pltpu.roll(new_t, (keep - s * n_new) % wb, 1)
            out_ref[0, s] = jnp.where(lane >= keep, cols, shifted[s])


def _swa_sample(proj2d, cache_kt, cache_vt, layer, prev_new, sinks, qg, kg, bias, n_new, s_blk):
    depth, nseq, kw, wb = cache_kt.shape
    assert s_blk * n_new == wb
    rows = s_blk * n_new
    blk = (rows, MIX_BLOCK)
    cblk = (s_blk, kw, wb)
    cin = pl.BlockSpec((1,) + cblk, lambda i: (layer, i, 0, 0))
    const2 = lambda i: (0, 0)
    prev_k, prev_v = prev_new if prev_new is not None else (None, None)
    kspec, kshape, kin, kin_specs = _layer_state_out(layer, depth, cblk, (nseq, kw, wb), F32, prev_k)
    vspec, vshape, vin, vin_specs = _layer_state_out(layer, depth, cblk, (nseq, kw, wb), F32, prev_v)
    n_main = 10
    aliases = {n_main + j: 1 + j for j in range(len(kin + vin))}
    return pl.pallas_call(
        functools.partial(_swa_sample_kernel, n_new=n_new),
        grid=(nseq // s_blk,),
        in_specs=[pl.BlockSpec(memory_space=pltpu.SMEM),
                  pl.BlockSpec(blk, lambda i: (i, COL_SWA_Q)),
                  pl.BlockSpec(blk, lambda i: (i, COL_SWA_KV)),
                  cin, cin,
                  pl.BlockSpec((1, MIX_BLOCK), const2),
                  pl.BlockSpec((1, kw), const2),
                  pl.BlockSpec((MIX_BLOCK, MIX_BLOCK), const2),
                  pl.BlockSpec((kw, kw), const2),
                  pl.BlockSpec((SWA_HEADS, n_new, wb + n_new), lambda i: (0, 0, 0))] + kin_specs + vin_specs,
        out_specs=[pl.BlockSpec(blk, lambda i: (i, 0)), kspec, vspec],
        out_shape=[jax.ShapeDtypeStruct((nseq * n_new, MIX_BLOCK), BF16), kshape, vshape],
        input_output_aliases=aliases,
        compiler_params=_cparams(("parallel",)),
        name="swa_sample",
    )(sinks, proj2d, proj2d, cache_kt, cache_vt, jnp.tile(qg, (1, SWA_HEADS)), jnp.tile(kg, (1, SWA_KV_HEADS)),
      _head_mean_matrix(MIX_BLOCK), _head_mean_matrix(kw), bias, *kin, *vin)


def _ssm_kernel(u_ref, h0_ref, wb_ref, tab_ref, wc_ref, d_ref, wglu_ref, y_ref, hn_ref,
                bu_scr, carry_scr, *, chained, tiles_per_seq, tb):
    n = SSM_N

    def project_in(rows):
        bu_scr[rows, :] = _dot(u_ref[rows, :].astype(BF16), wb_ref[...])

    def project_out(rows):
        y = _dot(bu_scr[rows, :].astype(BF16), wc_ref[...]) + d_ref[...] * u_ref[rows, :]
        y = 0.5 * y * (1.0 + jnp.tanh(math.sqrt(2.0 / math.pi) * (y + 0.044715 * (y * y * y))))
        return (y * _sigmoid(_dot(y.astype(BF16), wglu_ref[...]))).astype(y_ref.dtype)

    def tile_scan(r0, cr, ci):
        hr = bu_scr[pl.ds(r0, SCAN_ROWS), :n]
        hi = bu_scr[pl.ds(r0, SCAN_ROWS), n:]
        for k, shift in enumerate((1, 2, 4)):
            ar, ai = tab_ref[2 * k], tab_ref[2 * k + 1]
            sr, si = pltpu.roll(hr, shift, 0), pltpu.roll(hi, shift, 0)
            hr, hi = hr + ar * sr - ai * si, hi + ar * si + ai * sr
        pr, pi = tab_ref[6], tab_ref[7]
        hr, hi = hr + pr * cr - pi * ci, hi + pr * ci + pi * cr
        bu_scr[pl.ds(r0, SCAN_ROWS), :n] = hr
        bu_scr[pl.ds(r0, SCAN_ROWS), n:] = hi
        return hr[SCAN_ROWS - 1:], hi[SCAN_ROWS - 1:]

    if chained:
        @pl.when(pl.program_id(1) % tiles_per_seq == 0)
        def _():
            carry_scr[...] = jnp.zeros_like(carry_scr)

        project_in(slice(None))
        cr, ci = carry_scr[:, :n], carry_scr[:, n:]
        for r0 in range(0, tb, SCAN_ROWS):
            lr, li = tile_scan(r0, cr, ci)
            cr, ci = jnp.broadcast_to(lr, (SCAN_ROWS, n)), jnp.broadcast_to(li, (SCAN_ROWS, n))
        y_ref[0] = project_out(slice(None))
        carry_scr[:, :n] = cr
        carry_scr[:, n:] = ci
        hn_ref[0, :, :n] = cr
        hn_ref[0, :, n:] = ci
    else:
        project_in(slice(None))
        for t in range(tb // SCAN_ROWS):
            h0 = h0_ref[t:t + 1, :]
            cr = jnp.broadcast_to(h0[:, :n], (SCAN_ROWS, n))
            ci = jnp.broadcast_to(h0[:, n:], (SCAN_ROWS, n))
            lr, li = tile_scan(t * SCAN_ROWS, cr, ci)
            hn_ref[t:t + 1, :n] = lr
            hn_ref[t:t + 1, n:] = li
        y_ref[...] = project_out(slice(None))


def _ssm_common_specs(zero_map2, zero_map3):
    return [pl.BlockSpec((SSM_WIDTH, 2 * SSM_N), zero_map2),
            pl.BlockSpec((8, SCAN_ROWS, SSM_N), zero_map3),
            pl.BlockSpec((2 * SSM_N, SSM_WIDTH), zero_map2),
            pl.BlockSpec((1, SSM_WIDTH), zero_map2),
            pl.BlockSpec((SSM_WIDTH, SSM_WIDTH), zero_map2)]


def _ssm_prompt(proj3d, sp, tb):
    b, l, _ = proj3d.shape
    nt = l // tb
    dummy_h0 = jnp.zeros((SCAN_ROWS, 2 * SSM_N), F32)
    kern = functools.partial(_ssm_kernel, chained=True, tiles_per_seq=nt, tb=tb)

    def kernel(u_ref, h0_ref, wb, tab, wc, d, wglu, y_ref, hn_ref, bu_scr, carry_scr):
        kern(u_ref.at[0], h0_ref, wb, tab, wc, d, wglu, y_ref, hn_ref, bu_scr, carry_scr)

    return pl.pallas_call(
        kernel,
        grid=(b, nt),
        in_specs=[pl.BlockSpec((1, tb, SSM_WIDTH), lambda bi, i: (bi, i, COL_SSM)),
                  pl.BlockSpec((SCAN_ROWS, 2 * SSM_N), lambda bi, i: (0, 0))]
                 + _ssm_common_specs(lambda bi, i: (0, 0), lambda bi, i: (0, 0, 0)),
        out_specs=[pl.BlockSpec((1, tb, SSM_WIDTH), lambda bi, i: (bi, i, 0)),
                   pl.BlockSpec((1, SCAN_ROWS, 2 * SSM_N), lambda bi, i: (bi, 0, 0))],
        out_shape=[jax.ShapeDtypeStruct((b, l, SSM_WIDTH), BF16),
                   jax.ShapeDtypeStruct((b, SCAN_ROWS, 2 * SSM_N), F32)],
        scratch_shapes=[pltpu.VMEM((tb, 2 * SSM_N), F32), pltpu.VMEM((SCAN_ROWS, 2 * SSM_N), F32)],
        compiler_params=_cparams(("parallel", "arbitrary")),
        name="ssm_prompt",
    )(proj3d, dummy_h0, sp["wb"], sp["tab"], sp["wc"], sp["d"], sp["wglu"])


def _ssm_sample(proj2d, h0, sp):
    rows = proj2d.shape[0]
    nseq = h0.shape[0]
    kern = functools.partial(_ssm_kernel, chained=False, tiles_per_seq=1, tb=rows)
    return pl.pallas_call(
        kern,
        grid=(1,),
        in_specs=[pl.BlockSpec((rows, SSM_WIDTH), lambda i: (0, COL_SSM)),
                  pl.BlockSpec((nseq, 2 * SSM_N), lambda i: (0, 0))]
                 + _ssm_common_specs(lambda i: (0, 0), lambda i: (0, 0, 0)),
        out_specs=[pl.BlockSpec((rows, SSM_WIDTH), lambda i: (0, 0)),
                   pl.BlockSpec((nseq, 2 * SSM_N), lambda i: (0, 0))],
        out_shape=[jax.ShapeDtypeStruct((rows, SSM_WIDTH), BF16),
                   jax.ShapeDtypeStruct((nseq, 2 * SSM_N), F32)],
        scratch_shapes=[pltpu.VMEM((rows, 2 * SSM_N), F32), pltpu.VMEM((SCAN_ROWS, 2 * SSM_N), F32)],
        compiler_params=_cparams(("arbitrary",)),
        name="ssm_sample",
    )(proj2d, h0, sp["wb"], sp["tab"], sp["wc"], sp["d"], sp["wglu"])


def _ssm_params(lam_re, lam_im, log_dt, b_re, b_im, c_re, c_im, d_skip, w_glu):
    lr, li = lam_re.astype(F32), lam_im.astype(F32)
    dt = jnp.exp(log_dt.astype(F32))[:, None]
    mag = jnp.exp(lr * dt)
    ab_re, ab_im = mag * jnp.cos(li * dt), mag * jnp.sin(li * dt)
    den = lr * lr + li * li
    nr = ab_re - 1.0
    f_re = (nr * lr + ab_im * li) / den
    f_im = (ab_im * lr - nr * li) / den
    br, bi = b_re.astype(F32), b_im.astype(F32)
    bb_re = f_re[..., None] * br - f_im[..., None] * bi
    bb_im = f_re[..., None] * bi + f_im[..., None] * br
    eye = jnp.eye(SSM_GROUPS, dtype=F32)

    def in_mat(bb):
        return jnp.einsum('gpc,gh->gchp', bb, eye).reshape(SSM_WIDTH, SSM_N)

    def out_mat(c):
        return jnp.einsum('gcp,gh->gphc', c.astype(F32), eye).reshape(SSM_N, SSM_WIDTH)

    wb = jnp.concatenate([in_mat(bb_re), in_mat(bb_im)], axis=1).astype(BF16)
    wc = jnp.concatenate([out_mat(c_re), -out_mat(c_im)], axis=0).astype(BF16)

    ar, ai = ab_re.reshape(1, SSM_N), ab_im.reshape(1, SSM_N)

    def cmul(x, y):
        return (x[0] * y[0] - x[1] * y[1], x[0] * y[1] + x[1] * y[0])

    pw = [(ar, ai)]
    for _ in range(SCAN_ROWS - 1):
        pw.append(cmul(pw[-1], (ar, ai)))
    row = jnp.arange(SCAN_ROWS)[:, None]
    tabs = []
    for shift in (1, 2, 4):
        for part in pw[shift - 1]:
            tabs.append(jnp.where(row >= shift, part, 0.0))
    tabs.append(jnp.concatenate([p[0] for p in pw], axis=0))
    tabs.append(jnp.concatenate([p[1] for p in pw], axis=0))
    tab = jnp.stack([jnp.broadcast_to(t, (SCAN_ROWS, SSM_N)) for t in tabs])
    return dict(wb=wb, tab=tab, wc=wc, d=d_skip.astype(F32).reshape(1, SSM_WIDTH), wglu=w_glu.astype(BF16))


_RET_G = 1.0 - np.exp2(-5.0 - np.arange(RET_HEADS, dtype=np.float64))


def _ret_consts(chunk, n_rows):
    idx = np.arange(n_rows)
    loc = idx % chunk
    same = (idx[:, None] // chunk) == (idx[None, :] // chunk)
    diff = loc[:, None] - loc[None, :]
    dec = np.where(same & (diff >= 0), _RET_G[:, None, None] ** np.maximum(diff, 0)[None], 0.0)
    qdec = np.repeat((_RET_G[None, :] ** (loc[:, None] + 1.0)), HEAD_DIM, axis=1)
    kdec = np.repeat((_RET_G[None, :] ** (chunk - 1.0 - loc[:, None])), HEAD_DIM, axis=1)
    return (jnp.asarray(dec, F32), jnp.asarray(qdec, F32), jnp.asarray(kdec, F32),
            jnp.asarray(_RET_G ** chunk, F32))


def _rope_tables(pos):
    half = HEAD_DIM // 2
    theta = 1.0 / (ROPE_BASE ** np.linspace(0.0, 1.0, half))
    ang = np.asarray(pos, np.float64)[:, None] * theta[None, :]
    cos = np.repeat(np.cos(ang), 2, axis=1)
    sin = np.repeat(np.sin(ang), 2, axis=1) * np.tile([-1.0, 1.0], half)[None]
    return (jnp.asarray(np.tile(cos, (1, RET_HEADS)), F32), jnp.asarray(np.tile(sin, (1, RET_HEADS)), F32))


def _rotate_pairs(x, cos, sin_signed):
    lane = lax.broadcasted_iota(jnp.int32, x.shape, 1)
    nxt = pltpu.roll(x, x.shape[1] - 1, 1)
    prv = pltpu.roll(x, 1, 1)
    return x * cos + jnp.where(lane % 2 == 0, nxt, prv) * sin_signed


def _ret_head_out(o, gate, norm):
    ms = jnp.mean(o * o, axis=-1, keepdims=True)
    return o * lax.rsqrt(ms + RMS_EPS) * norm * (gate * _sigmoid(gate))


RET_TILE = 1024


def _ret_prompt_kernel(gc_ref, q_ref, k_ref, v_ref, g_ref, cos_ref, sin_ref, dec_ref, qdec_ref,
                       kdec_ref, norm_ref, mh_ref, y_ref, r_ref, o_scr):
    @pl.when(pl.program_id(1) == 0)
    def _():
        r_ref[...] = jnp.zeros_like(r_ref)

    cos, sin = cos_ref[...], sin_ref[...]
    q = _rotate_pairs(q_ref[0], cos, sin)
    k = _rotate_pairs(k_ref[0], cos, sin) * (HEAD_DIM ** -0.5)
    qb, kb, vb = q.astype(BF16), k.astype(BF16), v_ref[0].astype(BF16)
    kdb = (k * kdec_ref[...]).astype(BF16)
    for c in range(RET_TILE // RET_CHUNK):
        rows = slice(c * RET_CHUNK, (c + 1) * RET_CHUNK)
        for h in range(RET_HEADS):
            sl = slice(h * HEAD_DIM, (h + 1) * HEAD_DIM)
            qh, vh = qb[rows, sl], vb[rows, sl]
            s = _dot_nt(qh, kb[rows, sl]) * dec_ref[h]
            r = r_ref[0, h]
            o_scr[rows, sl] = _dot(s.astype(BF16), vh) + _dot(qh, r.astype(BF16)) * qdec_ref[rows, sl]
            r_ref[0, h] = gc_ref[h] * r + _dot_tn(kdb[rows, sl], vh)
    g = g_ref[0]
    y_ref[0] = (_head_rms(o_scr[...], mh_ref[...], norm_ref[...]) * (g * _sigmoid(g))).astype(y_ref.dtype)


def _ret_prompt(proj3d, cos, sin, norm):
    b, l, _ = proj3d.shape
    c = RET_CHUNK
    dec, _, _, gc = _ret_consts(c, c)
    _, qdec, kdec, _ = _ret_consts(c, RET_TILE)
    blk = (1, RET_TILE, MIX_BLOCK)
    tspec = pl.BlockSpec((RET_TILE, MIX_BLOCK), lambda bi, i: (i, 0))
    cspec = pl.BlockSpec((RET_TILE, MIX_BLOCK), lambda bi, i: (0, 0))
    return pl.pallas_call(
        _ret_prompt_kernel,
        grid=(b, l // RET_TILE),
        in_specs=[pl.BlockSpec(memory_space=pltpu.SMEM),
                  pl.BlockSpec(blk, lambda bi, i: (bi, i, COL_RET_Q)),
                  pl.BlockSpec(blk, lambda bi, i: (bi, i, COL_RET_K)),
                  pl.BlockSpec(blk, lambda bi, i: (bi, i, COL_RET_V)),
                  pl.BlockSpec(blk, lambda bi, i: (bi, i, COL_RET_G)),
                  tspec, tspec,
                  pl.BlockSpec((RET_HEADS, c, c), lambda bi, i: (0, 0, 0)),
                  cspec, cspec,
                  pl.BlockSpec((1, MIX_BLOCK), lambda bi, i: (0, 0)),
                  pl.BlockSpec((MIX_BLOCK, MIX_BLOCK), lambda bi, i: (0, 0))],
        out_specs=[pl.BlockSpec(blk, lambda bi, i: (bi, i, 0)),
                   pl.BlockSpec((1, RET_HEADS, HEAD_DIM, HEAD_DIM), lambda bi, i: (bi, 0, 0, 0))],
        out_shape=[jax.ShapeDtypeStruct((b, l, MIX_BLOCK), BF16),
                   jax.ShapeDtypeStruct((b, RET_HEADS, HEAD_DIM, HEAD_DIM), F32)],
        scratch_shapes=[pltpu.VMEM((RET_TILE, MIX_BLOCK), F32)],
        compiler_params=_cparams(("parallel", "arbitrary")),
        name="ret_prompt",
    )(gc, proj3d, proj3d, proj3d, proj3d, cos, sin, dec, qdec, kdec, norm, _head_mean_matrix(MIX_BLOCK))


def _ret_sample_kernel(gc_ref, q_ref, k_ref, v_ref, g_ref, cos_ref, sin_ref, dec_ref, qdec_ref,
                       kdec_ref, norm_ref, r0_ref, *rest, n_new, s_blk):
    y_ref, rn_ref = rest[-2:]
    _zero_other_layers(rn_ref)
    cos, sin = cos_ref[...], sin_ref[...]
    q = _rotate_pairs(q_ref[...], cos, sin)
    k = _rotate_pairs(k_ref[...], cos, sin) * (HEAD_DIM ** -0.5)
    v = v_ref[...]
    g = g_ref[...]
    kd = k * kdec_ref[...]
    qdec = qdec_ref[...]
    norm = norm_ref[...]
    rows = s_blk * n_new
    seq = lax.broadcasted_iota(jnp.int32, (rows, HEAD_DIM), 0) // n_new
    seq_t = lax.broadcasted_iota(jnp.int32, (HEAD_DIM, rows), 1) // n_new
    kd_t = kd.T
    for h in range(RET_HEADS):
        sl = slice(h * HEAD_DIM, (h + 1) * HEAD_DIM)
        qf, kdt = q[:, sl], kd_t[sl, :]
        qh, kh, vh = qf.astype(BF16), k[:, sl].astype(BF16), v[:, sl].astype(BF16)
        s = _dot_nt(qh, kh) * dec_ref[h]
        cross = jnp.zeros((rows, HEAD_DIM), F32)
        for si in range(s_blk):
            mine = seq == si
            r = r0_ref[si, h]
            cross = cross + _dot(jnp.where(mine, qf, 0.0).astype(BF16), r.astype(BF16))
            rn_ref[0, si, h] = gc_ref[h] * r + _dot(jnp.where(seq_t == si, kdt, 0.0).astype(BF16), vh)
        o = _dot(s.astype(BF16), vh) + cross * qdec[:, sl]
        y_ref[:, sl] = _ret_head_out(o, g[:, sl], norm[:, sl]).astype(y_ref.dtype)


def _ret_sample(proj2d, r0, layer, depth, prev_new, cos, sin, norm, n_new, s_blk):
    rows = s_blk * n_new
    nseq = r0.shape[0]
    dec, qdec, kdec, gc = _ret_consts(n_new, rows)
    blk = (rows, MIX_BLOCK)
    cspec = pl.BlockSpec(blk, lambda i: (0, 0))
    rblk = (s_blk, RET_HEADS, HEAD_DIM, HEAD_DIM)
    rspec, rshape, rin, rin_specs = _layer_state_out(layer, depth, rblk, (nseq,) + rblk[1:], F32, prev_new)
    return pl.pallas_call(
        functools.partial(_ret_sample_kernel, n_new=n_new, s_blk=s_blk),
        grid=(nseq // s_blk,),
        in_specs=[pl.BlockSpec(memory_space=pltpu.SMEM),
                  pl.BlockSpec(blk, lambda i: (i, COL_RET_Q)),
                  pl.BlockSpec(blk, lambda i: (i, COL_RET_K)),
                  pl.BlockSpec(blk, lambda i: (i, COL_RET_V)),
                  pl.BlockSpec(blk, lambda i: (i, COL_RET_G)),
                  cspec, cspec,
                  pl.BlockSpec((RET_HEADS, rows, rows), lambda i: (0, 0, 0)),
                  cspec, cspec,
                  pl.BlockSpec((1, MIX_BLOCK), lambda i: (0, 0)),
                  pl.BlockSpec(rblk, lambda i: (i, 0, 0, 0))] + rin_specs,
        out_specs=[pl.BlockSpec(blk, lambda i: (i, 0)), rspec],
        out_shape=[jax.ShapeDtypeStruct((nseq * n_new, MIX_BLOCK), BF16), rshape],
        input_output_aliases={12 + j: 1 for j in range(len(rin))},
        compiler_params=_cparams(("parallel",)),
        name="ret_sample",
    )(gc, proj2d, proj2d, proj2d, proj2d, cos, sin, dec, qdec, kdec, norm, r0, *rin)


def _block_diag(w):
    g, n, _ = w.shape
    return jnp.einsum('gcd,gh->gchd', w, jnp.eye(g, dtype=w.dtype)).reshape(g * n, g * n)


def _layer_params(l, p):
    return dict(
        norm_mix=p['norm_mix'][l].reshape(1, D_MODEL),
        norm_ffn=p['norm_ffn'][l].reshape(1, D_MODEL),
        w_in=p['w_in'][l].astype(BF16),
        w_out=p['w_out'][l].astype(BF16),
        pool_w=_block_diag(p['pool_w'][l].astype(F32)).astype(BF16),
        pool_scale=p['pool_scale'][l].astype(F32).reshape(1, POOL_WIDTH),
        qg=p['swa_q_norm'][l].astype(F32).reshape(1, HEAD_DIM),
        kg=p['swa_k_norm'][l].astype(F32).reshape(1, HEAD_DIM),
        sinks=p['swa_sinks'][l].astype(F32),
        ssm=_ssm_params(p['ssm_lambda_re'][l], p['ssm_lambda_im'][l], p['ssm_log_dt'][l],
                        p['ssm_b_re'][l], p['ssm_b_im'][l], p['ssm_c_re'][l], p['ssm_c_im'][l],
                        p['ssm_d'][l], p['ssm_w_glu'][l]),
        ret_norm=p['ret_norm'][l].astype(F32).reshape(1, MIX_BLOCK),
    )


def _channel_mix(streams, l, lp, p):
    i = l // 2
    g, w_out = lp['norm_ffn'], lp['w_out']
    if l % 2 == 0:
        wg, wu, wd = (p[k][i].astype(BF16) for k in ('ffn_w_gate', 'ffn_w_up', 'ffn_w_down'))
        return [_out_proj_ffn(x, ys, w_out, g, wg, wu, wd, 1024, D_FF) for x, ys in streams]
    wr3 = _router_weights(p['moe_router'][i])
    wg, wu, wd = (p[k][i].astype(BF16) for k in ('moe_w_gate', 'moe_w_up', 'moe_w_down'))
    x1s, routes = zip(*[_out_proj_router(x, ys, w_out, g, wr3, 1024) for x, ys in streams])
    return _moe(x1s, routes, g, wg, wu, wd)


def _mix_prompt(x2, b, l, lp, bias, cos, sin):
    proj2 = _norm_matmul(x2, lp['norm_mix'], lp['w_in'], 1024)
    proj3 = proj2.reshape(b, l, IN_WIDTH)
    tb = min(4096, l)
    y_pool = _pool(proj2, COL_POOL, lp['pool_w'], lp['pool_scale'], n_rows=b * l, tb=tb,
                   tiles_per_seq=l // tb, pos0=0)
    y_swa, kn = _swa_prompt(proj3, lp['sinks'], lp['qg'], lp['kg'], bias)
    y_ssm, hn = _ssm_prompt(proj3, lp['ssm'], min(2048, l))
    y_ret, rn = _ret_prompt(proj3, cos, sin, lp['ret_norm'])
    ys = (y_pool, y_swa.reshape(b * l, MIX_BLOCK), y_ssm.reshape(b * l, MIX_BLOCK),
          y_ret.reshape(b * l, MIX_BLOCK))
    kw = SWA_KV_HEADS * HEAD_DIM
    hn = hn[:, 0]
    states = (proj3[:, l - POOL_BUF:, :POOL_WIDTH],
              kn.reshape(b, SWA_WINDOW, SWA_KV_HEADS, HEAD_DIM),
              proj3[:, l - SWA_WINDOW:, COL_SWA_KV * MIX_BLOCK + kw:(COL_SWA_KV + 1) * MIX_BLOCK]
              .reshape(b, SWA_WINDOW, SWA_KV_HEADS, HEAD_DIM),
              jnp.stack([hn[:, :SSM_N], hn[:, SSM_N:]], axis=-1).reshape(b, SSM_GROUPS, SSM_STATE, 2),
              rn)
    return ys, states


SAMPLE_SEQ_BLOCK = 16


def _cache_transposed(cache):
    depth, nseq, wb = cache.shape[:3]
    return jnp.swapaxes(cache.astype(F32).reshape(depth, nseq, wb, SWA_KV_HEADS * HEAD_DIM), 2, 3)


def _mix_sample(x2, nseq, n_new, start_pos, lp, layer, st, prev_new, bias, cos, sin):
    state_pool, cache_kt, cache_vt, state_ssm, state_ret = st
    prev_kv, prev_ret = prev_new if prev_new is not None else (None, None)
    rows = nseq * n_new
    kw = SWA_KV_HEADS * HEAD_DIM
    ext_rows = POOL_HALO + n_new
    proj2 = _norm_matmul(x2, lp['norm_mix'], lp['w_in'], 512)
    proj3 = proj2.reshape(nseq, n_new, IN_WIDTH)
    u_pool = proj3[:, :, :POOL_WIDTH]
    buf = state_pool.astype(F32)
    ext = jnp.concatenate([jnp.zeros((nseq, POOL_HALO - POOL_BUF, POOL_WIDTH), F32), buf, u_pool], axis=1)
    y_pool = _pool(ext.reshape(nseq * ext_rows, POOL_WIDTH), 0, lp['pool_w'], lp['pool_scale'],
                   n_rows=nseq * ext_rows, tb=nseq * ext_rows, tiles_per_seq=1, pos0=start_pos)
    y_pool = y_pool.reshape(nseq, ext_rows, POOL_WIDTH)[:, POOL_HALO:].reshape(rows, POOL_WIDTH)
    y_swa, nkt, nvt = _swa_sample(proj2, cache_kt, cache_vt, layer, prev_kv, lp['sinks'], lp['qg'], lp['kg'],
                                  bias, n_new, SAMPLE_SEQ_BLOCK)
    h0 = state_ssm.astype(F32).reshape(nseq, SSM_N, 2)
    h0 = jnp.concatenate([h0[..., 0], h0[..., 1]], axis=1)
    y_ssm, hn = _ssm_sample(proj2, h0, lp['ssm'])
    y_ret, rn = _ret_sample(proj2, state_ret[layer], layer, state_ret.shape[0], prev_ret, cos, sin,
                            lp['ret_norm'], n_new, SAMPLE_SEQ_BLOCK)
    ys = (y_pool, y_swa, y_ssm, y_ret)
    states = (jnp.concatenate([buf, u_pool], axis=1)[:, -POOL_BUF:],
              jnp.stack([hn[:, :SSM_N], hn[:, SSM_N:]], axis=-1).reshape(nseq, SSM_GROUPS, SSM_STATE, 2))
    return ys, states, ((nkt, nvt), rn)


def _forward(x_prompt, x_sample, past_len, sample_state, p, rel_bias):
    b, l, d = x_prompt.shape
    nseq, n_new, _ = x_sample.shape
    wb = sample_state[1].shape[2]
    depth = p['norm_mix'].shape[0]
    bias_p = _swa_bias(rel_bias, np.arange(SWA_BLOCK)[:, None] - np.arange(2 * SWA_BLOCK)[None, :] + SWA_BLOCK)
    bias_s = _swa_bias(rel_bias, np.arange(n_new)[:, None] - np.arange(wb + n_new)[None, :] + wb)
    rope_p = _rope_tables(np.arange(l))
    rope_s = _rope_tables(past_len + (np.arange(SAMPLE_SEQ_BLOCK * n_new) % n_new))
    xp = x_prompt.reshape(b * l, d)
    xs = x_sample.reshape(nseq * n_new, d)
    cache_kt, cache_vt = _cache_transposed(sample_state[1]), _cache_transposed(sample_state[2])
    st_p, st_s, stacked = [], [], None
    for li in range(depth):
        lp = _layer_params(li, p)
        yp, sp = _mix_prompt(xp, b, l, lp, bias_p, *rope_p)
        st = (sample_state[0][li], cache_kt, cache_vt, sample_state[3][li], sample_state[4])
        ys, ss, stacked = _mix_sample(xs, nseq, n_new, past_len, lp, li, st, stacked, bias_s, *rope_s)
        xp, xs = _channel_mix([(xp, yp), (xs, ys)], li, lp, p)
        st_p.append(sp)
        st_s.append(ss)
    (new_kt, new_vt), new_ret = stacked
    kv_shape = (depth, nseq, wb, SWA_KV_HEADS, HEAD_DIM)
    sample_out = (jnp.stack([s[0] for s in st_s]),
                  jnp.swapaxes(new_kt, 2, 3).reshape(kv_shape),
                  jnp.swapaxes(new_vt, 2, 3).reshape(kv_shape),
                  jnp.stack([s[1] for s in st_s]),
                  new_ret)
    outs = [xp.reshape(b, l, d), xs.reshape(nseq, n_new, d)]
    for k in range(5):
        outs.append(jnp.stack([s[k] for s in st_p]))
        outs.append(sample_out[k])
    return tuple(outs)


PAST_LEN = 16384


def kernel(x_prompt, x_sample, state_pool, cache_swa_k, cache_swa_v, state_ssm, state_ret,
           norm_mix, norm_ffn, w_in, w_out, pool_w, pool_scale, swa_q_norm, swa_k_norm, swa_sinks,
           rel_bias, ssm_lambda_re, ssm_lambda_im, ssm_log_dt, ssm_b_re, ssm_b_im, ssm_c_re, ssm_c_im,
           ssm_d, ssm_w_glu, ret_norm, ffn_w_gate, ffn_w_up, ffn_w_down, moe_router, moe_w_gate,
           moe_w_up, moe_w_down):
    p = dict(norm_mix=norm_mix, norm_ffn=norm_ffn, w_in=w_in, w_out=w_out, pool_w=pool_w,
             pool_scale=pool_scale, swa_q_norm=swa_q_norm, swa_k_norm=swa_k_norm, swa_sinks=swa_sinks,
             ssm_lambda_re=ssm_lambda_re, ssm_lambda_im=ssm_lambda_im, ssm_log_dt=ssm_log_dt,
             ssm_b_re=ssm_b_re, ssm_b_im=ssm_b_im, ssm_c_re=ssm_c_re, ssm_c_im=ssm_c_im,
             ssm_d=ssm_d, ssm_w_glu=ssm_w_glu, ret_norm=ret_norm,
             ffn_w_gate=ffn_w_gate, ffn_w_up=ffn_w_up, ffn_w_down=ffn_w_down, moe_router=moe_router,
             moe_w_gate=moe_w_gate, moe_w_up=moe_w_up, moe_w_down=moe_w_down)
    return _forward(x_prompt, x_sample, PAST_LEN,
                    (state_pool, cache_swa_k, cache_swa_v, state_ssm, state_ret), p, rel_bias)
```
